```python
import math
import jax, jax.numpy as jnp
from jax import lax
import numpy as np

D_MODEL = 1024
BATCH = 8
SEQ = 2048
DEPTH = 2

HEAD_DIM = 64
N_A_LAYERS = DEPTH // 2
N_B_LAYERS = DEPTH - N_A_LAYERS
FOX_HEADS = D_MODEL // HEAD_DIM
SWA_Q_HEADS = D_MODEL // HEAD_DIM
SWA_KV_HEADS = SWA_Q_HEADS // 8
SWA_GROUP = SWA_Q_HEADS // SWA_KV_HEADS
WINDOW = 128
Q_BLOCK = 128
D_FF = 4 * D_MODEL
N_BUCKETS = 32
REL_MAX_DIST = 128
NORM_EPS = 1e-6

kernel_name = "yoco_fox_swa_sink_hybrid"


def _rmsnorm(x, g):
    xf = x.astype(jnp.float32)
    y = xf * lax.rsqrt(jnp.mean(xf * xf, axis=-1, keepdims=True) + NORM_EPS)
    return (y * g.astype(jnp.float32)).astype(x.dtype)


def _t5_causal_bucket(dist):
    n = np.maximum(dist, 0)
    max_exact = N_BUCKETS // 2
    large = max_exact + (np.log(np.maximum(n, 1) / max_exact)
                         / np.log(REL_MAX_DIST / max_exact)
                         * (N_BUCKETS - max_exact)).astype(np.int32)
    large = np.minimum(large, N_BUCKETS - 1)
    return np.where(n < max_exact, n, large).astype(np.int32)


def _sq_relu_mlp(h, w_up, w_down):
    u = h @ w_up
    return (jnp.square(jax.nn.relu(u))) @ w_down


def _fox_attention(h, w_in, b_f, g_q, g_k, w_out):
    bsz, seq, _ = h.shape
    hw = FOX_HEADS * HEAD_DIM
    proj = h @ w_in
    q = _rmsnorm(proj[..., :hw].reshape(bsz, seq, FOX_HEADS, HEAD_DIM), g_q)
    k = _rmsnorm(proj[..., hw:2 * hw].reshape(bsz, seq, FOX_HEADS, HEAD_DIM), g_k)
    v = proj[..., 2 * hw:3 * hw].reshape(bsz, seq, FOX_HEADS, HEAD_DIM)
    log_f = jax.nn.log_sigmoid(proj[..., 3 * hw:].astype(jnp.float32)
                               + b_f.astype(jnp.float32))
    c = jnp.cumsum(log_f, axis=1).transpose(0, 2, 1)
    scale = HEAD_DIM ** -0.5
    outs = []
    for blk in range(seq // Q_BLOCK):
        t0, t1 = blk * Q_BLOCK, (blk + 1) * Q_BLOCK
        s = jnp.einsum('bqhd,bkhd->bhqk', q[:, t0:t1], k[:, :t1]).astype(jnp.float32) * scale
        s = s + c[:, :, t0:t1, None] - c[:, :, None, :t1]
        mask = np.arange(t1)[None, :] <= np.arange(t0, t1)[:, None]
        s = jnp.where(mask, s, -jnp.inf)
        p = jax.nn.softmax(s, axis=-1).astype(v.dtype)
        outs.append(jnp.einsum('bhqk,bkhd->bqhd', p, v[:, :t1]))
    o = jnp.concatenate(outs, axis=1).reshape(bsz, seq, hw)
    return o @ w_out


def _shared_kv(h, g_kv, w_kv, g_k):
    bsz, seq, _ = h.shape
    kvw = SWA_KV_HEADS * HEAD_DIM
    kv = _rmsnorm(h, g_kv) @ w_kv
    k = _rmsnorm(kv[..., :kvw].reshape(bsz, seq, SWA_KV_HEADS, HEAD_DIM), g_k)
    v = kv[..., kvw:].reshape(bsz, seq, SWA_KV_HEADS, HEAD_DIM)
    return k, v


def _band(x):
    bsz, seq = x.shape[:2]
    xb = x.reshape(bsz, seq // WINDOW, WINDOW, *x.shape[2:])
    prev = jnp.pad(xb[:, :-1], ((0, 0), (1, 0), (0, 0), (0, 0), (0, 0)))
    return jnp.concatenate([prev, xb], axis=2)


def _swa_sink_attention(h, w_q, g_q, k_band, v_band, sinks, rel_bias, w_out):
    bsz, seq, _ = h.shape
    nblk = seq // WINDOW
    q = _rmsnorm((h @ w_q).reshape(bsz, seq, SWA_Q_HEADS, HEAD_DIM), g_q)
    qb = q.reshape(bsz, nblk, WINDOW, SWA_KV_HEADS, SWA_GROUP, HEAD_DIM)
    s = jnp.einsum('bnqkgd,bnjkd->bnkgqj', qb, k_band).astype(jnp.float32) * (HEAD_DIM ** -0.5)
    dist = np.arange(WINDOW)[:, None] + WINDOW - np.arange(2 * WINDOW)[None, :]
    bias = rel_bias.astype(jnp.float32)[_t5_causal_bucket(dist)]
    bias = bias.transpose(2, 0, 1).reshape(SWA_KV_HEADS, SWA_GROUP, WINDOW, 2 * WINDOW)
    s = s + bias[None, None]
    valid = (dist >= 0) & (dist < WINDOW)
    blk_ok = (np.arange(nblk)[:, None] > 0) | (np.arange(2 * WINDOW)[None, :] >= WINDOW)
    mask = valid[None, :, :] & blk_ok[:, None, :]
    s = jnp.where(mask[None, :, None, None], s, -jnp.inf)
    sink = jnp.broadcast_to(
        sinks.astype(jnp.float32).reshape(1, 1, SWA_KV_HEADS, SWA_GROUP, 1, 1),
        s.shape[:-1] + (1,))
    p = jax.nn.softmax(jnp.concatenate([s, sink], axis=-1), axis=-1)[..., :-1]
    o = jnp.einsum('bnkgqj,bnjkd->bnqkgd', p.astype(v_band.dtype), v_band)
    return o.reshape(bsz, seq, SWA_Q_HEADS * HEAD_DIM) @ w_out


def _fwd_setup_inputs(seed: int = 0) -> dict:
    key = jax.random.key(seed)
    ks = jax.random.split(key, 20)
    f32 = jnp.float32
    hw = FOX_HEADS * HEAD_DIM
    qw = SWA_Q_HEADS * HEAD_DIM
    kvw = SWA_KV_HEADS * HEAD_DIM

    def nrm(k, shape, fan_in):
        return jax.random.normal(k, shape, f32) * (fan_in ** -0.5)

    def gain(k, shape):
        return 1.0 + 0.05 * jax.random.normal(k, shape, f32)

    return {
        "x": jax.random.normal(ks[0], (BATCH, SEQ, D_MODEL), f32),
        "g_attn": gain(ks[1], (DEPTH, D_MODEL)),
        "g_mlp": gain(ks[2], (DEPTH, D_MODEL)),
        "w_in_a": nrm(ks[3], (N_A_LAYERS, D_MODEL, 3 * hw + FOX_HEADS), D_MODEL),
        "b_f": 3.0 + 0.5 * jax.random.normal(ks[4], (N_A_LAYERS, FOX_HEADS), f32),
        "gq_a": gain(ks[5], (N_A_LAYERS, HEAD_DIM)),
        "gk_a": gain(ks[6], (N_A_LAYERS, HEAD_DIM)),
        "w_out_a": nrm(ks[7], (N_A_LAYERS, hw, D_MODEL), hw),
        "g_kv": gain(ks[8], (D_MODEL,)),
        "w_kv": nrm(ks[9], (D_MODEL, 2 * kvw), D_MODEL),
        "gk_b": gain(ks[10], (HEAD_DIM,)),
        "w_q_b": nrm(ks[11], (N_B_LAYERS, D_MODEL, qw), D_MODEL),
        "gq_b": gain(ks[12], (N_B_LAYERS, HEAD_DIM)),
        "sinks": 0.5 * jax.random.normal(ks[13], (N_B_LAYERS, SWA_Q_HEADS), f32),
        "rel_bias": 0.2 * jax.random.normal(ks[14], (N_BUCKETS, SWA_Q_HEADS), f32),
        "w_out_b": nrm(ks[15], (N_B_LAYERS, qw, D_MODEL), qw),
        "w_up": nrm(ks[16], (DEPTH, D_MODEL, D_FF), D_MODEL),
        "w_down": nrm(ks[17], (DEPTH, D_FF, D_MODEL), D_FF),
    }


def _fwd_reference(x, g_attn, g_mlp, w_in_a, b_f, gq_a, gk_a, w_out_a, g_kv, w_kv, gk_b,
              w_q_b, gq_b, sinks, rel_bias, w_out_b, w_up, w_down):
    h = x
    k_band = None
    v_band = None
    for layer in range(DEPTH):
        if layer < N_A_LAYERS:
            a = layer
            h = h + _fox_attention(_rmsnorm(h, g_attn[layer]), w_in_a[a], b_f[a],
                                   gq_a[a], gk_a[a], w_out_a[a])
        else:
            b = layer - N_A_LAYERS
            if b == 0:
                k_sh, v_sh = _shared_kv(h, g_kv, w_kv, gk_b)
                k_band, v_band = _band(k_sh), _band(v_sh)
            h = h + _swa_sink_attention(_rmsnorm(h, g_attn[layer]), w_q_b[b], gq_b[b],
                                        k_band, v_band, sinks[b], rel_bias, w_out_b[b])
        h = h + _sq_relu_mlp(_rmsnorm(h, g_mlp[layer]), w_up[layer], w_down[layer])
    return h


import jax as _jax
import jax.numpy as _jnp

TWIN_FORMAT = 'train_step'
FWD_PARAMS = ['x', 'g_attn', 'g_mlp', 'w_in_a', 'b_f', 'gq_a', 'gk_a', 'w_out_a', 'g_kv', 'w_kv', 'gk_b', 'w_q_b', 'gq_b', 'sinks', 'rel_bias', 'w_out_b', 'w_up', 'w_down']
TWIN_WEIGHTS = ['g_attn', 'g_mlp', 'w_in_a', 'b_f', 'gq_a', 'gk_a', 'w_out_a', 'g_kv', 'w_kv', 'gk_b', 'w_q_b', 'gq_b', 'sinks', 'rel_bias', 'w_out_b', 'w_up', 'w_down']
TWIN_DIFF_INPUT = 'x'
TWIN_INPUTS = ['x', 'g_attn', 'g_mlp', 'w_in_a', 'b_f', 'gq_a', 'gk_a', 'w_out_a', 'g_kv', 'w_kv', 'gk_b', 'w_q_b', 'gq_b', 'sinks', 'rel_bias', 'w_out_b', 'w_up', 'w_down', 'loss_target', 'm_g_attn', 'm_g_mlp', 'm_w_in_a', 'm_b_f', 'm_gq_a', 'm_gk_a', 'm_w_out_a', 'm_g_kv', 'm_w_kv', 'm_gk_b', 'm_w_q_b', 'm_gq_b', 'm_sinks', 'm_rel_bias', 'm_w_out_b', 'm_w_up', 'm_w_down', 'v_g_attn', 'v_g_mlp', 'v_w_in_a', 'v_b_f', 'v_gq_a', 'v_gk_a', 'v_w_out_a', 'v_g_kv', 'v_w_kv', 'v_gk_b', 'v_w_q_b', 'v_gq_b', 'v_sinks', 'v_rel_bias', 'v_w_out_b', 'v_w_up', 'v_w_down']
TWIN_OUTPUTS = ['loss', 'grad_x', 'grad_g_attn', 'grad_g_mlp', 'grad_w_in_a', 'grad_b_f', 'grad_gq_a', 'grad_gk_a', 'grad_w_out_a', 'grad_g_kv', 'grad_w_kv', 'grad_gk_b', 'grad_w_q_b', 'grad_gq_b', 'grad_sinks', 'grad_rel_bias', 'grad_w_out_b', 'grad_w_up', 'grad_w_down', 'delta_g_attn', 'delta_g_mlp', 'delta_w_in_a', 'delta_b_f', 'delta_gq_a', 'delta_gk_a', 'delta_w_out_a', 'delta_g_kv', 'delta_w_kv', 'delta_gk_b', 'delta_w_q_b', 'delta_gq_b', 'delta_sinks', 'delta_rel_bias', 'delta_w_out_b', 'delta_w_up', 'delta_w_down', 'new_m_g_attn', 'new_m_g_mlp', 'new_m_w_in_a', 'new_m_b_f', 'new_m_gq_a', 'new_m_gk_a', 'new_m_w_out_a', 'new_m_g_kv', 'new_m_w_kv', 'new_m_gk_b', 'new_m_w_q_b', 'new_m_gq_b', 'new_m_sinks', 'new_m_rel_bias', 'new_m_w_out_b', 'new_m_w_up', 'new_m_w_down', 'new_v_g_attn', 'new_v_g_mlp', 'new_v_w_in_a', 'new_v_b_f', 'new_v_gq_a', 'new_v_gk_a', 'new_v_w_out_a', 'new_v_g_kv', 'new_v_w_kv', 'new_v_gk_b', 'new_v_w_q_b', 'new_v_gq_b', 'new_v_sinks', 'new_v_rel_bias', 'new_v_w_out_b', 'new_v_w_up', 'new_v_w_down']
TWIN_LEAF_KINDS = {'loss': 'loss', 'grad_x': 'grad_x', 'grad_g_attn': 'grad_w', 'grad_g_mlp': 'grad_w', 'grad_w_in_a': 'grad_w', 'grad_b_f': 'grad_w', 'grad_gq_a': 'grad_w', 'grad_gk_a': 'grad_w', 'grad_w_out_a': 'grad_w', 'grad_g_kv': 'grad_w', 'grad_w_kv': 'grad_w', 'grad_gk_b': 'grad_w', 'grad_w_q_b': 'grad_w', 'grad_gq_b': 'grad_w', 'grad_sinks': 'grad_w', 'grad_rel_bias': 'grad_w', 'grad_w_out_b': 'grad_w', 'grad_w_up': 'grad_w', 'grad_w_down': 'grad_w', 'delta_g_attn': 'delta_w', 'delta_g_mlp': 'delta_w', 'delta_w_in_a': 'delta_w', 'delta_b_f': 'delta_w', 'delta_gq_a': 'delta_w', 'delta_gk_a': 'delta_w', 'delta_w_out_a': 'delta_w', 'delta_g_kv': 'delta_w', 'delta_w_kv': 'delta_w', 'delta_gk_b': 'delta_w', 'delta_w_q_b': 'delta_w', 'delta_gq_b': 'delta_w', 'delta_sinks': 'delta_w', 'delta_rel_bias': 'delta_w', 'delta_w_out_b': 'delta_w', 'delta_w_up': 'delta_w', 'delta_w_down': 'delta_w', 'new_m_g_attn': 'new_m', 'new_m_g_mlp': 'new_m', 'new_m_w_in_a': 'new_m', 'new_m_b_f': 'new_m', 'new_m_gq_a': 'new_m', 'new_m_gk_a': 'new_m', 'new_m_w_out_a': 'new_m', 'new_m_g_kv': 'new_m', 'new_m_w_kv': 'new_m', 'new_m_gk_b': 'new_m', 'new_m_w_q_b': 'new_m', 'new_m_gq_b': 'new_m', 'new_m_sinks': 'new_m', 'new_m_rel_bias': 'new_m', 'new_m_w_out_b': 'new_m', 'new_m_w_up': 'new_m', 'new_m_w_down': 'new_m', 'new_v_g_attn': 'new_v', 'new_v_g_mlp': 'new_v', 'new_v_w_in_a': 'new_v', 'new_v_b_f': 'new_v', 'new_v_gq_a': 'new_v', 'new_v_gk_a': 'new_v', 'new_v_w_out_a': 'new_v', 'new_v_g_kv': 'new_v', 'new_v_w_kv': 'new_v', 'new_v_gk_b': 'new_v', 'new_v_w_q_b': 'new_v', 'new_v_gq_b': 'new_v', 'new_v_sinks': 'new_v', 'new_v_rel_bias': 'new_v', 'new_v_w_out_b': 'new_v', 'new_v_w_up': 'new_v', 'new_v_w_down': 'new_v'}


def _forward(args):
    return _fwd_reference(*[args[k] for k in FWD_PARAMS])


def _output_shape():
    out = _jax.eval_shape(lambda: _forward(_fwd_setup_inputs(0)))
    return out.shape, out.dtype

N_MICROBATCH = 1
ADAM_LR = 0.001
ADAM_B1 = 0.9
ADAM_B2 = 0.999
ADAM_EPS = 1e-08
ADAM_WD = 0.01
ADAM_STEP = 10
PER_EXAMPLE_BATCH_AXIS = {'x': 0, 'loss_target': 0}
SHARED_INPUTS = []
_WEIGHT_DTYPES = {'g_attn': _jnp.float32, 'g_mlp': _jnp.float32, 'w_in_a': _jnp.float32, 'b_f': _jnp.float32, 'gq_a': _jnp.float32, 'gk_a': _jnp.float32, 'w_out_a': _jnp.float32, 'g_kv': _jnp.float32, 'w_kv': _jnp.float32, 'gk_b': _jnp.float32, 'w_q_b': _jnp.float32, 'gq_b': _jnp.float32, 'sinks': _jnp.float32, 'rel_bias': _jnp.float32, 'w_out_b': _jnp.float32, 'w_up': _jnp.float32, 'w_down': _jnp.float32}
MOMENT_SCALE = {'g_attn': 1.163177e+00, 'g_mlp': 5.028462e+01, 'w_in_a': 3.130715e-01, 'b_f': 3.168940e+01, 'gq_a': 1.165912e+01, 'gk_a': 1.180805e+01, 'w_out_a': 4.182951e-01, 'g_kv': 6.925380e+00, 'w_kv': 1.289870e+01, 'gk_b': 2.946502e+00, 'w_q_b': 1.407096e-01, 'gq_b': 2.980665e+00, 'sinks': 1.019267e+00, 'rel_bias': 1.177439e-01, 'w_out_b': 6.404418e+00, 'w_up': 2.530659e+00, 'w_down': 1.005833e+01}


def _to_microbatches(a, axis):
    t = _jnp.moveaxis(a, axis, 0)
    t = t.reshape((N_MICROBATCH, t.shape[0] // N_MICROBATCH) + t.shape[1:])
    return _jnp.moveaxis(t, 1, axis + 1)


def setup_inputs(seed: int = 0) -> dict:
    inp = _fwd_setup_inputs(seed)
    key = _jax.random.fold_in(_jax.random.key(seed), 7919)
    shape, _ = _output_shape()
    out = dict(inp)
    out["loss_target"] = _jax.random.normal(_jax.random.fold_in(key, 0), shape, _jnp.float32)
    for i, name in enumerate(TWIN_WEIGHTS):
        w = inp[name].astype(_jnp.float32)
        if MOMENT_SCALE is None:
            s = _jnp.sqrt(_jnp.mean(_jnp.square(w)) + 1e-30)
        else:
            s = MOMENT_SCALE[name]
        km, kv = _jax.random.split(_jax.random.fold_in(key, i + 1))
        out[name] = w
        out["m_" + name] = s * _jax.random.normal(km, w.shape, _jnp.float32)
        out["v_" + name] = (s * s) * _jax.random.uniform(kv, w.shape, _jnp.float32, 0.5, 1.5)
    if N_MICROBATCH > 1:
        for name, axis in PER_EXAMPLE_BATCH_AXIS.items():
            out[name] = _to_microbatches(out[name], axis)
    return {'x': out['x'], 'g_attn': out['g_attn'], 'g_mlp': out['g_mlp'], 'w_in_a': out['w_in_a'], 'b_f': out['b_f'], 'gq_a': out['gq_a'], 'gk_a': out['gk_a'], 'w_out_a': out['w_out_a'], 'g_kv': out['g_kv'], 'w_kv': out['w_kv'], 'gk_b': out['gk_b'], 'w_q_b': out['w_q_b'], 'gq_b': out['gq_b'], 'sinks': out['sinks'], 'rel_bias': out['rel_bias'], 'w_out_b': out['w_out_b'], 'w_up': out['w_up'], 'w_down': out['w_down'], 'loss_target': out['loss_target'], 'm_g_attn': out['m_g_attn'], 'm_g_mlp': out['m_g_mlp'], 'm_w_in_a': out['m_w_in_a'], 'm_b_f': out['m_b_f'], 'm_gq_a': out['m_gq_a'], 'm_gk_a': out['m_gk_a'], 'm_w_out_a': out['m_w_out_a'], 'm_g_kv': out['m_g_kv'], 'm_w_kv': out['m_w_kv'], 'm_gk_b': out['m_gk_b'], 'm_w_q_b': out['m_w_q_b'], 'm_gq_b': out['m_gq_b'], 'm_sinks': out['m_sinks'], 'm_rel_bias': out['m_rel_bias'], 'm_w_out_b': out['m_w_out_b'], 'm_w_up': out['m_w_up'], 'm_w_down': out['m_w_down'], 'v_g_attn': out['v_g_attn'], 'v_g_mlp': out['v_g_mlp'], 'v_w_in_a': out['v_w_in_a'], 'v_b_f': out['v_b_f'], 'v_gq_a': out['v_gq_a'], 'v_gk_a': out['v_gk_a'], 'v_w_out_a': out['v_w_out_a'], 'v_g_kv': out['v_g_kv'], 'v_w_kv': out['v_w_kv'], 'v_gk_b': out['v_gk_b'], 'v_w_q_b': out['v_w_q_b'], 'v_gq_b': out['v_gq_b'], 'v_sinks': out['v_sinks'], 'v_rel_bias': out['v_rel_bias'], 'v_w_out_b': out['v_w_out_b'], 'v_w_up': out['v_w_up'], 'v_w_down': out['v_w_down']}


def _loss(weights, diff, rest, loss_target):
    with _jax.named_scope("forward"):
        args = {**rest, TWIN_DIFF_INPUT: diff, **{k: w.astype(_WEIGHT_DTYPES[k]) for k, w in weights.items()}}
        y = _forward(args)
    with _jax.named_scope("loss_head"):
        err = _jnp.square(y.astype(_jnp.float32) - loss_target)
        return 0.5 * _jnp.sum(_jnp.mean(err, axis=-1)) if err.ndim else 0.5 * err


def _adamw(w, g, m, v):
    m = ADAM_B1 * m + (1.0 - ADAM_B1) * g
    v = ADAM_B2 * v + (1.0 - ADAM_B2) * _jnp.square(g)
    m_hat = m / (1.0 - ADAM_B1 ** ADAM_STEP)
    v_hat = v / (1.0 - ADAM_B2 ** ADAM_STEP)
    delta = -ADAM_LR * (m_hat / (_jnp.sqrt(v_hat) + ADAM_EPS) + ADAM_WD * w)
    return delta, m, v


def reference(x, g_attn, g_mlp, w_in_a, b_f, gq_a, gk_a, w_out_a, g_kv, w_kv, gk_b, w_q_b, gq_b, sinks, rel_bias, w_out_b, w_up, w_down, loss_target, m_g_attn, m_g_mlp, m_w_in_a, m_b_f, m_gq_a, m_gk_a, m_w_out_a, m_g_kv, m_w_kv, m_gk_b, m_w_q_b, m_gq_b, m_sinks, m_rel_bias, m_w_out_b, m_w_up, m_w_down, v_g_attn, v_g_mlp, v_w_in_a, v_b_f, v_gq_a, v_gk_a, v_w_out_a, v_g_kv, v_w_kv, v_gk_b, v_w_q_b, v_gq_b, v_sinks, v_rel_bias, v_w_out_b, v_w_up, v_w_down):
    given = dict(x=x, g_attn=g_attn, g_mlp=g_mlp, w_in_a=w_in_a, b_f=b_f, gq_a=gq_a, gk_a=gk_a, w_out_a=w_out_a, g_kv=g_kv, w_kv=w_kv, gk_b=gk_b, w_q_b=w_q_b, gq_b=gq_b, sinks=sinks, rel_bias=rel_bias, w_out_b=w_out_b, w_up=w_up, w_down=w_down, loss_target=loss_target, m_g_attn=m_g_attn, m_g_mlp=m_g_mlp, m_w_in_a=m_w_in_a, m_b_f=m_b_f, m_gq_a=m_gq_a, m_gk_a=m_gk_a, m_w_out_a=m_w_out_a, m_g_kv=m_g_kv, m_w_kv=m_w_kv, m_gk_b=m_gk_b, m_w_q_b=m_w_q_b, m_gq_b=m_gq_b, m_sinks=m_sinks, m_rel_bias=m_rel_bias, m_w_out_b=m_w_out_b, m_w_up=m_w_up, m_w_down=m_w_down, v_g_attn=v_g_attn, v_g_mlp=v_g_mlp, v_w_in_a=v_w_in_a, v_b_f=v_b_f, v_gq_a=v_gq_a, v_gk_a=v_gk_a, v_w_out_a=v_w_out_a, v_g_kv=v_g_kv, v_w_kv=v_w_kv, v_gk_b=v_gk_b, v_w_q_b=v_w_q_b, v_gq_b=v_gq_b, v_sinks=v_sinks, v_rel_bias=v_rel_bias, v_w_out_b=v_w_out_b, v_w_up=v_w_up, v_w_down=v_w_down)
    weights = {n: given[n] for n in TWIN_WEIGHTS}
    shared = {n: given[n] for n in SHARED_INPUTS}
    per_example = {n: given[n] for n in ['x']}
    grad_fn = _jax.value_and_grad(_loss, argnums=(0, 1))

    def one_microbatch(ex, loss_target):
        ex = dict(ex)
        diff = ex.pop(TWIN_DIFF_INPUT)
        return grad_fn(weights, diff, {**shared, **ex}, loss_target)

    if N_MICROBATCH == 1:
        loss, (grad_w, grad_x) = one_microbatch(per_example, given["loss_target"])
    else:
        def body(carry, xs):
            loss_sum, grad_sum = carry
            l_k, (gw_k, gx_k) = one_microbatch(xs[0], xs[1])
            with _jax.named_scope("update"):
                return (loss_sum + l_k, _jax.tree.map(_jnp.add, grad_sum, gw_k)), gx_k

        init = (_jnp.zeros((), _jnp.float32), _jax.tree.map(_jnp.zeros_like, weights))
        (loss, grad_w), grad_x = _jax.lax.scan(body, init, (per_example, given["loss_target"]))
    with _jax.named_scope("update"):
        delta_w, new_m, new_v = {}, {}, {}
        for n in TWIN_WEIGHTS:
            delta_w[n], new_m[n], new_v[n] = _adamw(weights[n], grad_w[n], given["m_" + n], given["v_" + n])
    return (loss, grad_x, *[grad_w[n] for n in TWIN_WEIGHTS], *[delta_w[n] for n in TWIN_WEIGHTS],
            *[new_m[n] for n in TWIN_WEIGHTS], *[new_v[n] for n in TWIN_WEIGHTS])
```

```python
import functools

import numpy as np
import jax
import jax.numpy as jnp
from jax import lax
from jax.experimental import pallas as pl
from jax.experimental.pallas import tpu as pltpu

F32 = jnp.float32
BF16 = jnp.bfloat16

N_DEV = 8
HEAD_DIM = 64
WINDOW = 128
N_BUCKETS = 32
REL_MAX_DIST = 128
NORM_EPS = 1e-6
NEG = -1e30
LANES = 128
VMEM_LIMIT = 56 * 1024 * 1024

ADAM_LR = 0.001
ADAM_B1 = 0.9
ADAM_B2 = 0.999
ADAM_EPS = 1e-08
ADAM_WD = 0.01
ADAM_STEP = 10

SMALL = ("g_attn", "g_mlp", "b_f", "gq_a", "gk_a", "g_kv", "gk_b", "gq_b", "sinks", "rel_bias")
BIG = ("w_in_a", "w_out_a", "w_kv", "w_q_b", "w_out_b", "w_up", "w_down")
WEIGHTS = ("g_attn", "g_mlp", "w_in_a", "b_f", "gq_a", "gk_a", "w_out_a", "g_kv", "w_kv", "gk_b",
           "w_q_b", "gq_b", "sinks", "rel_bias", "w_out_b", "w_up", "w_down")


def _params(n_grid):
    return pltpu.CompilerParams(dimension_semantics=("arbitrary",) * n_grid, vmem_limit_bytes=VMEM_LIMIT)


def _sds(shape, dtype):
    return jax.ShapeDtypeStruct(tuple(shape), dtype)


def _rms(x, g):
    return (x * lax.rsqrt(jnp.mean(x * x, axis=-1, keepdims=True) + NORM_EPS)) * g


def _dot_nt(a, b):
    return lax.dot_general(a, b, (((1,), (1,)), ((), ())), preferred_element_type=F32)


def _dot_tn(a, b):
    return lax.dot_general(a, b, (((0,), (0,)), ((), ())), preferred_element_type=F32)


def _dot(a, b):
    return jnp.dot(a, b, preferred_element_type=F32)


def _dot_exact(a, b):
    return jnp.dot(a, b, preferred_element_type=F32, precision=lax.Precision.HIGHEST)


def _norm_matmul(name, x, g, w, *, tn, relu2=False):
    t, d = x.shape
    n = w.shape[1]
    tm = min(512, t)

    def body(x_ref, g_ref, w_ref, y_ref, xn_ref, *a_ref):
        @pl.when(pl.program_id(1) == 0)
        def _():
            xn_ref[...] = _rms(x_ref[...], g_ref[...]).astype(BF16)

        y = _dot(xn_ref[...], w_ref[...])
        y_ref[...] = y
        if relu2:
            a_ref[0][...] = jnp.square(jnp.maximum(y, 0.0)).astype(BF16)

    out_shape = [_sds((t, n), F32), _sds((t, d), BF16)]
    out_specs = [pl.BlockSpec((tm, tn), lambda i, j: (i, j)), pl.BlockSpec((tm, d), lambda i, j: (i, 0))]
    if relu2:
        out_shape.append(_sds((t, n), BF16))
        out_specs.append(pl.BlockSpec((tm, tn), lambda i, j: (i, j)))
    return pl.pallas_call(
        body, name=name, grid=(t // tm, n // tn),
        in_specs=[pl.BlockSpec((tm, d), lambda i, j: (i, 0)), pl.BlockSpec((1, d), lambda i, j: (0, 0)),
                  pl.BlockSpec((d, tn), lambda i, j: (0, j))],
        out_specs=out_specs, out_shape=out_shape, compiler_params=_params(2),
    )(x, g.reshape(1, d), w)


def _matmul_res(name, a, w, res, *, tn=512):
    t, k = a.shape
    n = w.shape[1]
    tm = min(512, t)

    def body(a_ref, w_ref, r_ref, o_ref):
        o_ref[...] = r_ref[...] + _dot(a_ref[...], w_ref[...])

    return pl.pallas_call(
        body, name=name, grid=(t // tm, n // tn),
        in_specs=[pl.BlockSpec((tm, k), lambda i, j: (i, 0)), pl.BlockSpec((k, tn), lambda i, j: (0, j)),
                  pl.BlockSpec((tm, tn), lambda i, j: (i, j))],
        out_specs=pl.BlockSpec((tm, tn), lambda i, j: (i, j)), out_shape=_sds((t, n), F32),
        compiler_params=_params(2),
    )(a, w, res)


def _matmul_nt(name, dy, w, *, u=None, tk=512):
    t, n = dy.shape
    k = w.shape[0]
    tm = min(512, t)

    def body(dy_ref, w_ref, *rest):
        o_ref = rest[-1]
        r = _dot_nt(dy_ref[...].astype(BF16), w_ref[...])
        if u is not None:
            r = r * (2.0 * jnp.maximum(rest[0][...], 0.0))
        o_ref[...] = r.astype(BF16)

    in_specs = [pl.BlockSpec((tm, n), lambda i, j: (i, 0)), pl.BlockSpec((tk, n), lambda i, j: (j, 0))]
    args = [dy, w]
    if u is not None:
        in_specs.append(pl.BlockSpec((tm, tk), lambda i, j: (i, j)))
        args.append(u)
    return pl.pallas_call(
        body, name=name, grid=(t // tm, k // tk), in_specs=in_specs,
        out_specs=pl.BlockSpec((tm, tk), lambda i, j: (i, j)), out_shape=_sds((t, k), BF16),
        compiler_params=_params(2),
    )(*args)


def _matmul_nt_rmsbwd(name, dy, w, x, g, dres):
    t, k = dy.shape
    d = w.shape[0]
    tm = min(256, t)

    def body(dy_ref, w_ref, x_ref, g_ref, r_ref, dx_ref, dg_ref):
        dxn = _dot_nt(dy_ref[...].astype(BF16), w_ref[...])
        _, vjp = jax.vjp(_rms, x_ref[...], g_ref[...])
        dx, dg = vjp(dxn)
        dx_ref[...] = r_ref[...] + dx

        @pl.when(pl.program_id(0) == 0)
        def _():
            dg_ref[...] = jnp.zeros_like(dg_ref)

        dg_ref[...] += dg

    row = lambda i: (i, 0)
    fixed = lambda i: (0, 0)
    return pl.pallas_call(
        body, name=name, grid=(t // tm,),
        in_specs=[pl.BlockSpec((tm, k), row), pl.BlockSpec((d, k), fixed), pl.BlockSpec((tm, d), row),
                  pl.BlockSpec((1, d), fixed), pl.BlockSpec((tm, d), row)],
        out_specs=[pl.BlockSpec((tm, d), row), pl.BlockSpec((1, d), fixed)],
        out_shape=[_sds((t, d), F32), _sds((1, d), F32)], compiler_params=_params(1),
    )(dy, w, x, g.reshape(1, d), dres)


def _matmul_tn(name, a, b, *, tk=512, tn=512):
    t, k = a.shape
    n = b.shape[1]
    tk, tn = min(tk, k), min(tn, n)

    def body(a_ref, b_ref, o_ref):
        o_ref[...] = _dot_tn(a_ref[...].astype(BF16), b_ref[...].astype(BF16)).astype(BF16)

    return pl.pallas_call(
        body, name=name, grid=(k // tk, n // tn),
        in_specs=[pl.BlockSpec((t, tk), lambda i, j: (0, i)), pl.BlockSpec((t, tn), lambda i, j: (0, j))],
        out_specs=pl.BlockSpec((tk, tn), lambda i, j: (i, j)), out_shape=_sds((k, n), BF16),
        compiler_params=_params(2),
    )(a, b)


def _loss_grad(name, y, target):
    t, d = y.shape
    tm = min(512, t)

    def body(y_ref, t_ref, dy_ref, l_ref):
        e = y_ref[...] - t_ref[...]
        dy_ref[...] = e * (1.0 / d)

        @pl.when(pl.program_id(0) == 0)
        def _():
            l_ref[...] = jnp.zeros_like(l_ref)

        l_ref[...] += 0.5 * jnp.sum(jnp.mean(e * e, axis=-1))

    row = lambda i: (i, 0)
    return pl.pallas_call(
        body, name=name, grid=(t // tm,), in_specs=[pl.BlockSpec((tm, d), row), pl.BlockSpec((tm, d), row)],
        out_specs=[pl.BlockSpec((tm, d), row), pl.BlockSpec((8, LANES), lambda i: (0, 0))],
        out_shape=[_sds((t, d), F32), _sds((8, LANES), F32)], compiler_params=_params(1),
    )(y, target)


def _gate_fwd(name, proj, b_pad, n_heads, gate_col):
    t = proj.shape[0]
    tb = min(256, t)
    tri = jnp.asarray(np.tril(np.ones((tb, tb), np.float32)))

    def body(p_ref, b_ref, tri_ref, c_ref, carry):
        @pl.when(pl.program_id(0) == 0)
        def _():
            carry[...] = jnp.zeros_like(carry)

        lane = lax.broadcasted_iota(jnp.int32, (tb, LANES), 1)
        lf = jnp.where(lane < n_heads, jax.nn.log_sigmoid(p_ref[...] + b_ref[...]), 0.0)
        c = _dot_exact(tri_ref[...], lf) + carry[0:1, :]
        c_ref[...] = c
        carry[...] = jnp.broadcast_to(c[tb - 1:tb, :], carry.shape)

    return pl.pallas_call(
        body, name=name, grid=(t // tb,),
        in_specs=[pl.BlockSpec((tb, LANES), lambda i: (i, gate_col)), pl.BlockSpec((1, LANES), lambda i: (0, 0)),
                  pl.BlockSpec((tb, tb), lambda i: (0, 0))],
        out_specs=pl.BlockSpec((tb, LANES), lambda i: (i, 0)), out_shape=_sds((t, LANES), F32),
        scratch_shapes=[pltpu.VMEM((8, LANES), F32)], compiler_params=_params(1),
    )(proj, b_pad, tri)


def _gate_bwd(name, proj, b_pad, dc, n_heads, gate_col):
    t = proj.shape[0]
    tb = min(256, t)
    nb = t // tb
    triu = jnp.asarray(np.triu(np.ones((tb, tb), np.float32)))

    def body(p_ref, b_ref, dc_ref, tri_ref, df_ref, db_ref, carry):
        @pl.when(pl.program_id(0) == 0)
        def _():
            carry[...] = jnp.zeros_like(carry)
            db_ref[...] = jnp.zeros_like(db_ref)

        dcv = dc_ref[...]
        dlf = _dot_exact(tri_ref[...], dcv) + carry[0:1, :]
        carry[...] = jnp.broadcast_to(dlf[0:1, :], carry.shape)
        lane = lax.broadcasted_iota(jnp.int32, (tb, LANES), 1)
        z = p_ref[...] + b_ref[...]
        df = jnp.where(lane < n_heads, dlf / (1.0 + jnp.exp(z)), 0.0)
        df_ref[...] = df.astype(BF16)
        db_ref[...] += jnp.sum(df, axis=0, keepdims=True)

    return pl.pallas_call(
        body, name=name, grid=(nb,),
        in_specs=[pl.BlockSpec((tb, LANES), lambda i: (nb - 1 - i, gate_col)), pl.BlockSpec((1, LANES), lambda i: (0, 0)),
                  pl.BlockSpec((tb, LANES), lambda i: (nb - 1 - i, 0)), pl.BlockSpec((tb, tb), lambda i: (0, 0))],
        out_specs=[pl.BlockSpec((tb, LANES), lambda i: (nb - 1 - i, 0)), pl.BlockSpec((1, LANES), lambda i: (0, 0))],
        out_shape=[_sds((t, LANES), BF16), _sds((1, LANES), F32)],
        scratch_shapes=[pltpu.VMEM((8, LANES), F32)], compiler_params=_params(1),
    )(proj, b_pad, dc, triu)


def _qhead(qp, g):
    return _rms(qp, g) * (HEAD_DIM ** -0.5)


def _column(mat, idx):
    lane = lax.broadcasted_iota(jnp.int32, mat.shape, 1)
    return jnp.sum(jnp.where(lane == idx, mat, 0.0), axis=1, keepdims=True)


def _fox_scores(qi, kk, ccol_i, crow, i, bq):
    length = kk.shape[0]
    s = _dot_nt(qi, kk) + ccol_i - crow[:, :length]
    row = lax.broadcasted_iota(jnp.int32, (bq, length), 0) + i * bq
    col = lax.broadcasted_iota(jnp.int32, (bq, length), 1)
    return jnp.where(col <= row, s, NEG)


def _fox_fwd(name, proj, c, crow, gq, gk, n_heads):
    t = proj.shape[0]
    hw = n_heads * HEAD_DIM
    npair = n_heads // 2
    bq = min(256, t)
    nq = t // bq

    def body(q_ref, k_ref, v_ref, c_ref, crow_ref, gq_ref, gk_ref, o_ref, lse_ref):
        hp = pl.program_id(0)
        lane = lax.broadcasted_iota(jnp.int32, (t, LANES), 1)
        outs, lse_pair = [], jnp.zeros((t, LANES), F32)
        for hh in range(2):
            sl = slice(hh * HEAD_DIM, (hh + 1) * HEAD_DIM)
            qn = _qhead(q_ref[:, sl], gq_ref[...]).astype(BF16)
            kn = _rms(k_ref[:, sl], gk_ref[...]).astype(BF16)
            vb = v_ref[:, sl].astype(BF16)
            ccol = _column(c_ref[...], 2 * hp + hh)
            cr = crow_ref[0, hh:hh + 1, :]
            o_blocks, lse_blocks = [], []
            for i in range(nq):
                length = (i + 1) * bq
                s = _fox_scores(qn[i * bq:(i + 1) * bq], kn[:length], ccol[i * bq:(i + 1) * bq], cr, i, bq)
                m = jnp.max(s, axis=1, keepdims=True)
                p = jnp.exp(s - m)
                l = jnp.sum(p, axis=1, keepdims=True)
                o_blocks.append(_dot(p.astype(BF16), vb[:length]) / l)
                lse_blocks.append(m + jnp.log(l))
            outs.append(jnp.concatenate(o_blocks, axis=0))
            lse_pair = jnp.where(lane == hh, jnp.concatenate(lse_blocks, axis=0), lse_pair)
        o_ref[...] = jnp.concatenate(outs, axis=1).astype(BF16)
        lse_ref[...] = lse_pair

    col = lambda off: (lambda h: (0, off + h))
    fixed = lambda h: (0, 0)
    return pl.pallas_call(
        body, name=name, grid=(npair,),
        in_specs=[pl.BlockSpec((t, LANES), col(0)), pl.BlockSpec((t, LANES), col(npair)), pl.BlockSpec((t, LANES), col(2 * npair)),
                  pl.BlockSpec((t, LANES), fixed), pl.BlockSpec((1, 2, t), lambda h: (h, 0, 0)),
                  pl.BlockSpec((1, HEAD_DIM), fixed), pl.BlockSpec((1, HEAD_DIM), fixed)],
        out_specs=[pl.BlockSpec((t, LANES), col(0)), pl.BlockSpec((t, LANES), col(0))],
        out_shape=[_sds((t, hw), BF16), _sds((t, npair * LANES), F32)], compiler_params=_params(1),
    )(proj, proj, proj, c, crow, gq, gk)


def _fox_bwd(name, proj, c, crow, gq, gk, lse, do, n_heads):
    t = proj.shape[0]
    hw = n_heads * HEAD_DIM
    npair = n_heads // 2
    bq = min(256, t)
    nq = t // bq

    def body(q_ref, k_ref, v_ref, c_ref, crow_ref, gq_ref, gk_ref, lse_ref, do_ref,
             dq_ref, dk_ref, dv_ref, dc_ref, dgq_ref, dgk_ref, dk_acc, dv_acc, dc_acc):
        hp = pl.program_id(0)

        @pl.when(hp == 0)
        def _():
            dgq_ref[...] = jnp.zeros_like(dgq_ref)
            dgk_ref[...] = jnp.zeros_like(dgk_ref)

        dc_ref[...] = jnp.zeros_like(dc_ref)
        dqs, dks, dvs = [], [], []
        for hh in range(2):
            sl = slice(hh * HEAD_DIM, (hh + 1) * HEAD_DIM)
            qp, kp = q_ref[:, sl], k_ref[:, sl]
            qf, q_vjp = jax.vjp(_qhead, qp, gq_ref[...])
            kf, k_vjp = jax.vjp(_rms, kp, gk_ref[...])
            qn, kn = qf.astype(BF16), kf.astype(BF16)
            vb = v_ref[:, sl].astype(BF16)
            dob = do_ref[:, sl]
            ccol = _column(c_ref[...], 2 * hp + hh)
            cr = crow_ref[0, hh:hh + 1, :]
            lse_h = lse_ref[:, hh:hh + 1]
            dk_acc[...] = jnp.zeros_like(dk_acc)
            dv_acc[...] = jnp.zeros_like(dv_acc)
            dc_acc[...] = jnp.zeros_like(dc_acc)
            dq_blocks = []
            for i in range(nq):
                rows = slice(i * bq, (i + 1) * bq)
                length = (i + 1) * bq
                qi, doi = qn[rows], dob[rows]
                s = _fox_scores(qi, kn[:length], ccol[rows], cr, i, bq)
                p = jnp.exp(s - lse_h[rows])
                dp = _dot_nt(doi, vb[:length])
                ds = p * (dp - jnp.sum(p * dp, axis=1, keepdims=True))
                dsb = ds.astype(BF16)
                dq_blocks.append(_dot(dsb, kn[:length]))
                dk_acc[0:length, :] += _dot_tn(dsb, qi)
                dv_acc[0:length, :] += _dot_tn(p.astype(BF16), doi)
                dc_acc[0:1, 0:length] -= jnp.sum(ds, axis=0, keepdims=True)
            dqp, dgq = q_vjp(jnp.concatenate(dq_blocks, axis=0))
            dkp, dgk = k_vjp(dk_acc[...])
            dgq_ref[...] += dgq
            dgk_ref[...] += dgk
            dqs.append(dqp)
            dks.append(dkp)
            dvs.append(dv_acc[...])
            dc_ref[0, hh:hh + 1, :] = dc_acc[0:1, :]
        dq_ref[...] = jnp.concatenate(dqs, axis=1).astype(BF16)
        dk_ref[...] = jnp.concatenate(dks, axis=1).astype(BF16)
        dv_ref[...] = jnp.concatenate(dvs, axis=1).astype(BF16)

    col = lambda off: (lambda h: (0, off + h))
    fixed = lambda h: (0, 0)
    pair_blk = pl.BlockSpec((t, LANES), col(0))
    return pl.pallas_call(
        body, name=name, grid=(npair,),
        in_specs=[pl.BlockSpec((t, LANES), col(0)), pl.BlockSpec((t, LANES), col(npair)), pl.BlockSpec((t, LANES), col(2 * npair)),
                  pl.BlockSpec((t, LANES), fixed), pl.BlockSpec((1, 2, t), lambda h: (h, 0, 0)),
                  pl.BlockSpec((1, HEAD_DIM), fixed), pl.BlockSpec((1, HEAD_DIM), fixed),
                  pair_blk, pair_blk],
        out_specs=[pair_blk, pair_blk, pair_blk, pl.BlockSpec((1, 8, t), lambda h: (h, 0, 0)),
                   pl.BlockSpec((1, HEAD_DIM), fixed), pl.BlockSpec((1, HEAD_DIM), fixed)],
        out_shape=[_sds((t, hw), BF16), _sds((t, hw), BF16), _sds((t, hw), BF16), _sds((npair, 8, t), F32),
                   _sds((1, HEAD_DIM), F32), _sds((1, HEAD_DIM), F32)],
        scratch_shapes=[pltpu.VMEM((t, HEAD_DIM), F32), pltpu.VMEM((t, HEAD_DIM), F32), pltpu.VMEM((8, t), F32)],
        compiler_params=_params(1),
    )(proj, proj, proj, c, crow, gq, gk, lse, do)


def _t5_bucket_table():
    dist = np.arange(WINDOW)[:, None] + WINDOW - np.arange(2 * WINDOW)[None, :]
    n = np.maximum(dist, 0)
    max_exact = N_BUCKETS // 2
    large = max_exact + (np.log(np.maximum(n, 1) / max_exact) / np.log(REL_MAX_DIST / max_exact)
                         * (N_BUCKETS - max_exact)).astype(np.int32)
    large = np.minimum(large, N_BUCKETS - 1)
    return np.where(n < max_exact, n, large).astype(np.int32).reshape(1, -1)


def _bias_expand(name, rel_bias_t):
    n_heads = rel_bias_t.shape[0]
    tbl = jnp.asarray(_t5_bucket_table())
    width = tbl.shape[1]

    def body(rb_ref, tbl_ref, o_ref):
        onehot = (lax.broadcasted_iota(jnp.int32, (N_BUCKETS, width), 0) == tbl_ref[...]).astype(F32)
        o_ref[...] = _dot_exact(rb_ref[...], onehot)

    return pl.pallas_call(body, name=name, out_shape=_sds((n_heads, width), F32), compiler_params=_params(0))(rel_bias_t, tbl)


def _bias_reduce(name, dbias):
    n_heads, width = dbias.shape
    tbl = jnp.asarray(_t5_bucket_table())

    def body(db_ref, tbl_ref, o_ref):
        onehot = (lax.broadcasted_iota(jnp.int32, (N_BUCKETS, width), 0) == tbl_ref[...]).astype(F32)
        o_ref[...] = lax.dot_general(db_ref[...], onehot, (((1,), (1,)), ((), ())), preferred_element_type=F32,
                                     precision=lax.Precision.HIGHEST)

    return pl.pallas_call(body, name=name, out_shape=_sds((n_heads, N_BUCKETS), F32), compiler_params=_params(0))(dbias, tbl)


def _swa_mask(n, rows):
    i = lax.broadcasted_iota(jnp.int32, (rows, 2 * WINDOW), 0) & (WINDOW - 1)
    j = lax.broadcasted_iota(jnp.int32, (rows, 2 * WINDOW), 1)
    ok = (j > i) & (j <= i + WINDOW) & ((n > 0) | (j >= WINDOW))
    return jnp.where(ok, 0.0, NEG)


def _swa_fwd(name, qb, kh, vh, gq, gk, sink_rows, bias, group):
    t = qb.shape[0]
    kvh = kh.shape[0]
    nblk = t // WINDOW
    rows = group * WINDOW
    gw = group * HEAD_DIM

    def body(q_ref, k_ref, v_ref, gq_ref, gk_ref, sink_ref, bias_ref, o_ref, lse_ref, qs, kpad, vpad):
        for g in range(group):
            qs[:, g * HEAD_DIM:(g + 1) * HEAD_DIM] = _qhead(q_ref[:, g * HEAD_DIM:(g + 1) * HEAD_DIM], gq_ref[...]).astype(BF16)
        kpad[0:WINDOW, :] = jnp.zeros((WINDOW, HEAD_DIM), BF16)
        vpad[0:WINDOW, :] = jnp.zeros((WINDOW, HEAD_DIM), BF16)
        kpad[WINDOW:, :] = _rms(k_ref[0], gk_ref[...]).astype(BF16)
        vpad[WINDOW:, :] = v_ref[0].astype(BF16)
        sink = sink_ref[0]
        lane = lax.broadcasted_iota(jnp.int32, (WINDOW, LANES), 1)

        def block(n, carry):
            start = pl.multiple_of(n * WINDOW, WINDOW)
            q = jnp.concatenate([qs[pl.ds(start, WINDOW), g * HEAD_DIM:(g + 1) * HEAD_DIM] for g in range(group)], axis=0)
            kb = kpad[pl.ds(start, 2 * WINDOW), :]
            vb = vpad[pl.ds(start, 2 * WINDOW), :]
            s = _dot_nt(q, kb) + bias_ref[0] + _swa_mask(n, rows)
            m = jnp.maximum(jnp.max(s, axis=1, keepdims=True), sink)
            e = jnp.exp(s - m)
            l = jnp.sum(e, axis=1, keepdims=True) + jnp.exp(sink - m)
            ob = (_dot(e.astype(BF16), vb) / l).astype(BF16)
            lse = m + jnp.log(l)
            lse_tile = jnp.zeros((WINDOW, LANES), F32)
            for g in range(group):
                o_ref[pl.ds(start, WINDOW), g * HEAD_DIM:(g + 1) * HEAD_DIM] = ob[g * WINDOW:(g + 1) * WINDOW]
                lse_tile = jnp.where(lane == g, lse[g * WINDOW:(g + 1) * WINDOW], lse_tile)
            lse_ref[pl.ds(start, WINDOW), :] = lse_tile
            return carry

        lax.fori_loop(0, nblk, block, 0)

    fixed = lambda h: (0, 0)
    per = lambda h: (h, 0, 0)
    return pl.pallas_call(
        body, name=name, grid=(kvh,),
        in_specs=[pl.BlockSpec((t, gw), lambda h: (0, h)), pl.BlockSpec((1, t, HEAD_DIM), per), pl.BlockSpec((1, t, HEAD_DIM), per),
                  pl.BlockSpec((1, HEAD_DIM), fixed), pl.BlockSpec((1, HEAD_DIM), fixed),
                  pl.BlockSpec((1, rows, 1), per), pl.BlockSpec((1, rows, 2 * WINDOW), per)],
        out_specs=[pl.BlockSpec((t, gw), lambda h: (0, h)), pl.BlockSpec((t, LANES), lambda h: (0, h))],
        out_shape=[_sds((t, kvh * gw), BF16), _sds((t, kvh * LANES), F32)],
        scratch_shapes=[pltpu.VMEM((t, gw), BF16), pltpu.VMEM((t + WINDOW, HEAD_DIM), BF16),
                        pltpu.VMEM((t + WINDOW, HEAD_DIM), BF16)],
        compiler_params=_params(1),
    )(qb, kh, vh, gq, gk, sink_rows, bias)


def _swa_bwd(name, qb, kh, vh, gq, gk, sink_rows, bias, lse, do, group):
    t = qb.shape[0]
    kvh = kh.shape[0]
    nblk = t // WINDOW
    rows = group * WINDOW
    gw = group * HEAD_DIM

    def body(q_ref, k_ref, v_ref, gq_ref, gk_ref, sink_ref, bias_ref, lse_ref, do_ref,
             dq_ref, dk_ref, dv_ref, dgq_ref, dgk_ref, dsink_ref, dbias_ref,
             qs, kpad, vpad, dqs, dk_acc, dv_acc, dsink_acc):
        @pl.when(pl.program_id(0) == 0)
        def _():
            dgq_ref[...] = jnp.zeros_like(dgq_ref)
            dgk_ref[...] = jnp.zeros_like(dgk_ref)

        for g in range(group):
            qs[:, g * HEAD_DIM:(g + 1) * HEAD_DIM] = _qhead(q_ref[:, g * HEAD_DIM:(g + 1) * HEAD_DIM], gq_ref[...]).astype(BF16)
        kpad[0:WINDOW, :] = jnp.zeros((WINDOW, HEAD_DIM), BF16)
        vpad[0:WINDOW, :] = jnp.zeros((WINDOW, HEAD_DIM), BF16)
        kpad[WINDOW:, :] = _rms(k_ref[0], gk_ref[...]).astype(BF16)
        vpad[WINDOW:, :] = v_ref[0].astype(BF16)
        dk_acc[...] = jnp.zeros_like(dk_acc)
        dv_acc[...] = jnp.zeros_like(dv_acc)
        dsink_acc[...] = jnp.zeros_like(dsink_acc)
        dbias_ref[...] = jnp.zeros_like(dbias_ref)
        sink = sink_ref[0]

        def block(n, carry):
            start = pl.multiple_of(n * WINDOW, WINDOW)
            stack = lambda ref: jnp.concatenate(
                [ref[pl.ds(start, WINDOW), g * HEAD_DIM:(g + 1) * HEAD_DIM] for g in range(group)], axis=0)
            q = jnp.concatenate([qs[pl.ds(start, WINDOW), g * HEAD_DIM:(g + 1) * HEAD_DIM] for g in range(group)], axis=0)
            dob = stack(do_ref)
            lse_tile = lse_ref[pl.ds(start, WINDOW), :]
            lse_s = jnp.concatenate([lse_tile[:, g:g + 1] for g in range(group)], axis=0)
            kb = kpad[pl.ds(start, 2 * WINDOW), :]
            vb = vpad[pl.ds(start, 2 * WINDOW), :]
            s = _dot_nt(q, kb) + bias_ref[0] + _swa_mask(n, rows)
            p = jnp.exp(s - lse_s)
            dp = _dot_nt(dob, vb)
            dsum = jnp.sum(p * dp, axis=1, keepdims=True)
            ds = p * (dp - dsum)
            dsb = ds.astype(BF16)
            dsink_acc[...] -= jnp.exp(sink - lse_s) * dsum
            dbias_ref[0] += ds
            dqb = _dot(dsb, kb)
            for g in range(group):
                dqs[pl.ds(start, WINDOW), g * HEAD_DIM:(g + 1) * HEAD_DIM] = dqb[g * WINDOW:(g + 1) * WINDOW]
            dk_acc[pl.ds(start, 2 * WINDOW), :] += _dot_tn(dsb, q)
            dv_acc[pl.ds(start, 2 * WINDOW), :] += _dot_tn(p.astype(BF16), dob)
            return carry

        lax.fori_loop(0, nblk, block, 0)
        for g in range(group):
            _, q_vjp = jax.vjp(_qhead, q_ref[:, g * HEAD_DIM:(g + 1) * HEAD_DIM], gq_ref[...])
            dqp, dgq = q_vjp(dqs[:, g * HEAD_DIM:(g + 1) * HEAD_DIM])
            dq_ref[:, g * HEAD_DIM:(g + 1) * HEAD_DIM] = dqp.astype(BF16)
            dgq_ref[...] += dgq
            dsink_g = jnp.sum(dsink_acc[g * WINDOW:(g + 1) * WINDOW, :], axis=0, keepdims=True)
            dsink_ref[0, g:g + 1, :] = jnp.broadcast_to(dsink_g, (1, LANES))
        _, k_vjp = jax.vjp(_rms, k_ref[0], gk_ref[...])
        dkp, dgk = k_vjp(dk_acc[WINDOW:, :])
        dk_ref[0] = dkp
        dgk_ref[...] += dgk
        dv_ref[0] = dv_acc[WINDOW:, :]

    fixed = lambda h: (0, 0)
    per = lambda h: (h, 0, 0)
    wide = pl.BlockSpec((t, gw), lambda h: (0, h))
    head = pl.BlockSpec((1, t, HEAD_DIM), per)
    vec = pl.BlockSpec((1, HEAD_DIM), fixed)
    return pl.pallas_call(
        body, name=name, grid=(kvh,),
        in_specs=[wide, head, head, vec, vec, pl.BlockSpec((1, rows, 1), per), pl.BlockSpec((1, rows, 2 * WINDOW), per),
                  pl.BlockSpec((t, LANES), lambda h: (0, h)), wide],
        out_specs=[wide, head, head, vec, vec, pl.BlockSpec((1, group, LANES), per), pl.BlockSpec((1, rows, 2 * WINDOW), per)],
        out_shape=[_sds((t, kvh * gw), BF16), _sds((kvh, t, HEAD_DIM), F32), _sds((kvh, t, HEAD_DIM), F32),
                   _sds((1, HEAD_DIM), F32), _sds((1, HEAD_DIM), F32),
                   _sds((kvh, group, LANES), F32), _sds((kvh, rows, 2 * WINDOW), F32)],
        scratch_shapes=[pltpu.VMEM((t, gw), BF16), pltpu.VMEM((t + WINDOW, HEAD_DIM), BF16),
                        pltpu.VMEM((t + WINDOW, HEAD_DIM), BF16), pltpu.VMEM((t, gw), F32),
                        pltpu.VMEM((t + WINDOW, HEAD_DIM), F32), pltpu.VMEM((t + WINDOW, HEAD_DIM), F32),
                        pltpu.VMEM((rows, 1), F32)],
        compiler_params=_params(1),
    )(qb, kh, vh, gq, gk, sink_rows, bias, lse, do)


def _mlp_fwd(tag, h, g, w_up, w_down):
    u, hn, a = _norm_matmul(f"{tag}_up", h, g, w_up, tn=512, relu2=True)
    return _matmul_res(f"{tag}_down", a, w_down, h), (h, g, u, hn, a)


def _mlp_bwd(tag, saved, w_up, w_down, dy):
    h, g, u, hn, a = saved
    du = _matmul_nt(f"{tag}_du", dy, w_down, u=u)
    dw_down = _matmul_tn(f"{tag}_dwdown", a, dy)
    dw_up = _matmul_tn(f"{tag}_dwup", hn, du)
    dh, dg = _matmul_nt_rmsbwd(f"{tag}_dh", du, w_up, h, g, dy)
    return dh, dg, dw_up, dw_down


def _local_step(x, target, p):
    t, d = x.shape
    n_heads = d // HEAD_DIM
    kv_heads = n_heads // 8
    group = n_heads // kv_heads
    hw = n_heads * HEAD_DIM
    gate_col = 3 * hw // LANES
    grads = {}

    proj, xn1 = _norm_matmul("a_inproj", x, p["g_attn"][0], p["w_in_a"], tn=640)
    b_pad = jnp.pad(p["b_f"], ((0, 0), (0, LANES - n_heads)))
    c = _gate_fwd("a_gate", proj, b_pad, n_heads, gate_col)
    crow = c[:, :n_heads].T.reshape(n_heads // 2, 2, t)
    o_a, lse_a = _fox_fwd("a_attn", proj, c, crow, p["gq_a"], p["gk_a"], n_heads)
    h1 = _matmul_res("a_outproj", o_a, p["w_out_a"], x)
    h2, mlp0 = _mlp_fwd("mlp0", h1, p["g_mlp"][0], p["w_up"][0], p["w_down"][0])

    kv, hn_kv = _norm_matmul("kv_proj", h2, p["g_kv"], p["w_kv"], tn=2 * kv_heads * HEAD_DIM)
    kvw = kv_heads * HEAD_DIM
    kh = kv[:, :kvw].reshape(t, kv_heads, HEAD_DIM).transpose(1, 0, 2)
    vh = kv[:, kvw:].reshape(t, kv_heads, HEAD_DIM).transpose(1, 0, 2)
    qb, hn_q = _norm_matmul("b_qproj", h2, p["g_attn"][1], p["w_q_b"], tn=512)
    gqb, gkb = p["gq_b"], p["gk_b"].reshape(1, HEAD_DIM)
    bias = _bias_expand("b_bias", p["rel_bias"].T).reshape(kv_heads, group * WINDOW, 2 * WINDOW)
    sink_rows = jnp.broadcast_to(p["sinks"].reshape(n_heads, 1), (n_heads, WINDOW)).reshape(kv_heads, group * WINDOW, 1)
    o_b, lse_b = _swa_fwd("b_attn", qb, kh, vh, gqb, gkb, sink_rows, bias, group)
    h3 = _matmul_res("b_outproj", o_b, p["w_out_b"], h2)
    y, mlp1 = _mlp_fwd("mlp1", h3, p["g_mlp"][1], p["w_up"][1], p["w_down"][1])
    dy, loss_tile = _loss_grad("loss", y, target)

    dh3, dg_mlp1, dw_up1, dw_down1 = _mlp_bwd("mlp1", mlp1, p["w_up"][1], p["w_down"][1], dy)
    do_b = _matmul_nt("b_do", dh3, p["w_out_b"])
    grads["w_out_b"] = _matmul_tn("b_dwout", o_b, dh3)
    dqb, dkh, dvh, grads["gq_b"], dgk_b, dsink, dbias = _swa_bwd(
        "b_attn_bwd", qb, kh, vh, gqb, gkb, sink_rows, bias, lse_b, do_b, group)
    grads["gk_b"] = dgk_b.reshape(HEAD_DIM)
    grads["sinks"] = dsink[:, :, 0].reshape(1, n_heads)
    grads["rel_bias"] = _bias_reduce("b_dbias", dbias.reshape(n_heads, WINDOW * 2 * WINDOW)).T
    grads["w_q_b"] = _matmul_tn("b_dwq", hn_q, dqb)
    dh2, dg_attn1 = _matmul_nt_rmsbwd("b_dhq", dqb, p["w_q_b"], h2, p["g_attn"][1], dh3)
    dkv = jnp.concatenate([dkh.transpose(1, 0, 2).reshape(t, kvw), dvh.transpose(1, 0, 2).reshape(t, kvw)], axis=1)
    grads["w_kv"] = _matmul_tn("kv_dw", hn_kv, dkv)
    dh2, dg_kv = _matmul_nt_rmsbwd("kv_dh", dkv, p["w_kv"], h2, p["g_kv"], dh2)
    grads["g_kv"] = dg_kv.reshape(d)
    dh1, dg_mlp0, dw_up0, dw_down0 = _mlp_bwd("mlp0", mlp0, p["w_up"][0], p["w_down"][0], dh2)
    grads["g_mlp"] = jnp.concatenate([dg_mlp0, dg_mlp1], axis=0)
    grads["w_up"] = (dw_up0, dw_up1)
    grads["w_down"] = (dw_down0, dw_down1)

    do_a = _matmul_nt("a_do", dh1, p["w_out_a"])
    grads["w_out_a"] = _matmul_tn("a_dwout", o_a, dh1)
    dq, dk, dv, dc_rows, grads["gq_a"], grads["gk_a"] = _fox_bwd(
        "a_attn_bwd", proj, c, crow, p["gq_a"], p["gk_a"], lse_a, do_a, n_heads)
    dc = jnp.pad(dc_rows[:, :2, :].reshape(n_heads, t).T, ((0, 0), (0, LANES - n_heads)))
    dfl, db_f = _gate_bwd("a_gate_bwd", proj, b_pad, dc, n_heads, gate_col)
    grads["b_f"] = db_f[:, :n_heads]
    dproj = jnp.concatenate([dq, dk, dv, dfl], axis=1)
    grads["w_in_a"] = _matmul_tn("a_dwin", xn1, dproj, tn=640)
    grad_x, dg_attn0 = _matmul_nt_rmsbwd("a_dx", dproj, p["w_in_a"], x, p["g_attn"][0], dh1)
    grads["g_attn"] = jnp.concatenate([dg_attn0, dg_attn1], axis=0)
    return loss_tile, grad_x, grads


def _exchange(name, arrays, scatter):
    n = len(arrays)

    def body(*refs):
        src, out = refs[:n], refs[n:2 * n]
        send_sems, recv_sems, local_sems = refs[2 * n:]
        x, y, c = lax.axis_index("x"), lax.axis_index("y"), lax.axis_index("c")
        me = 4 * x + 2 * y + c
        peers = []
        for k in range(1, N_DEV):
            px, py, pc = x ^ ((k >> 2) & 1), y ^ ((k >> 1) & 1), c ^ (k & 1)
            peers.append(((px, py, pc), 4 * px + 2 * py + pc))
        local, sends = [], []
        for a in range(n):
            mine = src[a].at[me] if scatter else src[a]
            cp = pltpu.make_async_copy(mine, out[a].at[me], local_sems.at[a])
            cp.start()
            local.append(cp)
        for k, (peer, peer_pos) in enumerate(peers):
            for a in range(n):
                cp = pltpu.make_async_remote_copy(
                    src_ref=src[a].at[peer_pos] if scatter else src[a], dst_ref=out[a].at[me],
                    send_sem=send_sems.at[a, k], recv_sem=recv_sems.at[a, k],
                    device_id=peer, device_id_type=pl.DeviceIdType.MESH)
                cp.start()
                sends.append(cp)
        for k, (peer, peer_pos) in enumerate(peers):
            for a in range(n):
                pltpu.make_async_remote_copy(
                    src_ref=out[a].at[peer_pos], dst_ref=out[a].at[peer_pos],
                    send_sem=send_sems.at[a, k], recv_sem=recv_sems.at[a, k],
                    device_id=peer, device_id_type=pl.DeviceIdType.MESH).wait_recv()
        for cp in sends:
            cp.wait_send()
        for cp in local:
            cp.wait()

    any_spec = pl.BlockSpec(memory_space=pl.ANY)
    out_shape = [_sds(a.shape if scatter else (N_DEV,) + a.shape, a.dtype) for a in arrays]
    return pl.pallas_call(
        body, name=name, in_specs=[any_spec] * n, out_specs=[any_spec] * n, out_shape=out_shape,
        scratch_shapes=[pltpu.SemaphoreType.DMA((n, N_DEV - 1)), pltpu.SemaphoreType.DMA((n, N_DEV - 1)),
                        pltpu.SemaphoreType.DMA((n,))],
    )(*arrays)


def _adamw(name, parts, w, m, v):
    r, c = w.shape
    tr = r if r <= 256 else 256

    def body(p_ref, w_ref, m_ref, v_ref, g_ref, d_ref, mo_ref, vo_ref):
        g = p_ref[0].astype(F32)
        for dev in range(1, N_DEV):
            g = g + p_ref[dev].astype(F32)
        m_new = ADAM_B1 * m_ref[...] + (1.0 - ADAM_B1) * g
        v_new = ADAM_B2 * v_ref[...] + (1.0 - ADAM_B2) * jnp.square(g)
        m_hat = m_new / (1.0 - ADAM_B1 ** ADAM_STEP)
        v_hat = v_new / (1.0 - ADAM_B2 ** ADAM_STEP)
        g_ref[...] = g
        d_ref[...] = -ADAM_LR * (m_hat / (jnp.sqrt(v_hat) + ADAM_EPS) + ADAM_WD * w_ref[...])
        mo_ref[...] = m_new
        vo_ref[...] = v_new

    blk = pl.BlockSpec((tr, c), lambda i: (i, 0))
    return pl.pallas_call(
        body, name=name, grid=(r // tr,),
        in_specs=[pl.BlockSpec((N_DEV, tr, c), lambda i: (0, i, 0)), blk, blk, blk],
        out_specs=[blk] * 4, out_shape=[_sds((r, c), F32)] * 4, compiler_params=_params(1),
    )(parts, w, m, v)


def _pack_small(tree):
    flat = jnp.concatenate([tree[k].reshape(-1) for k in SMALL])
    size = -(-flat.shape[0] // (8 * LANES)) * (8 * LANES)
    return jnp.pad(flat, (0, size - flat.shape[0])).reshape(-1, LANES)


def _unpack_small(packed, like):
    flat, out, off = packed.reshape(-1), {}, 0
    for k in SMALL:
        size = like[k].size
        out[k] = flat[off:off + size].reshape(like[k].shape)
        off += size
    return out


def kernel(x, g_attn, g_mlp, w_in_a, b_f, gq_a, gk_a, w_out_a, g_kv, w_kv, gk_b, w_q_b, gq_b, sinks, rel_bias, w_out_b, w_up, w_down, loss_target, m_g_attn, m_g_mlp, m_w_in_a, m_b_f, m_gq_a, m_gk_a, m_w_out_a, m_g_kv, m_w_kv, m_gk_b, m_w_q_b, m_gq_b, m_sinks, m_rel_bias, m_w_out_b, m_w_up, m_w_down, v_g_attn, v_g_mlp, v_w_in_a, v_b_f, v_gq_a, v_gk_a, v_w_out_a, v_g_kv, v_w_kv, v_gk_b, v_w_q_b, v_gq_b, v_sinks, v_rel_bias, v_w_out_b, v_w_up, v_w_down):
    w = dict(g_attn=g_attn, g_mlp=g_mlp, w_in_a=w_in_a, b_f=b_f, gq_a=gq_a, gk_a=gk_a, w_out_a=w_out_a, g_kv=g_kv,
             w_kv=w_kv, gk_b=gk_b, w_q_b=w_q_b, gq_b=gq_b, sinks=sinks, rel_bias=rel_bias, w_out_b=w_out_b,
             w_up=w_up, w_down=w_down)
    mom = dict(g_attn=m_g_attn, g_mlp=m_g_mlp, w_in_a=m_w_in_a, b_f=m_b_f, gq_a=m_gq_a, gk_a=m_gk_a, w_out_a=m_w_out_a,
               g_kv=m_g_kv, w_kv=m_w_kv, gk_b=m_gk_b, w_q_b=m_w_q_b, gq_b=m_gq_b, sinks=m_sinks, rel_bias=m_rel_bias,
               w_out_b=m_w_out_b, w_up=m_w_up, w_down=m_w_down)
    var = dict(g_attn=v_g_attn, g_mlp=v_g_mlp, w_in_a=v_w_in_a, b_f=v_b_f, gq_a=v_gq_a, gk_a=v_gk_a, w_out_a=v_w_out_a,
               g_kv=v_g_kv, w_kv=v_w_kv, gk_b=v_gk_b, w_q_b=v_w_q_b, gq_b=v_gq_b, sinks=v_sinks, rel_bias=v_rel_bias,
               w_out_b=v_w_out_b, w_up=v_w_up, w_down=v_w_down)
    t, d = x.shape[1], x.shape[2]
    dff = w_up.shape[2] * N_DEV
    n_in = w_in_a.shape[2] * N_DEV
    n_in_pad = -(-n_in // LANES) * LANES
    cols_in = w_in_a.shape[2]

    shards = [w_in_a[0], w_out_a[0], w_kv, w_q_b[0], w_out_b[0], w_up[0], w_up[1], w_down[0], w_down[1]]
    g_in, g_out_a, g_kvw, g_q_b, g_out_b, g_up0, g_up1, g_down0, g_down1 = _exchange(
        "gather_weights", [s.astype(BF16) for s in shards], scatter=False)
    full = dict(w)
    full["w_in_a"] = jnp.pad(g_in.transpose(1, 0, 2).reshape(d, n_in), ((0, 0), (0, n_in_pad - n_in)))
    full["w_out_a"] = g_out_a.reshape(d, d)
    full["w_kv"] = g_kvw.reshape(d, -1)
    full["w_q_b"] = g_q_b.reshape(d, d)
    full["w_out_b"] = g_out_b.reshape(d, d)
    full["w_up"] = (g_up0.transpose(1, 0, 2).reshape(d, dff), g_up1.transpose(1, 0, 2).reshape(d, dff))
    full["w_down"] = (g_down0.reshape(dff, d), g_down1.reshape(dff, d))

    loss_tile, grad_x, grads = _local_step(x[0], loss_target[0], full)
    loss = lax.psum(loss_tile[0, 0], ("x", "y", "c"))

    col_blocks = lambda g, n: g.reshape(d, N_DEV, n).transpose(1, 0, 2)
    partials = [col_blocks(grads["w_in_a"][:, :n_in], cols_in),
                grads["w_out_a"].reshape(N_DEV, d // N_DEV, d), grads["w_kv"].reshape(N_DEV, d // N_DEV, -1),
                grads["w_q_b"].reshape(N_DEV, d // N_DEV, d), grads["w_out_b"].reshape(N_DEV, d // N_DEV, d),
                col_blocks(grads["w_up"][0], dff // N_DEV), col_blocks(grads["w_up"][1], dff // N_DEV),
                grads["w_down"][0].reshape(N_DEV, dff // N_DEV, d), grads["w_down"][1].reshape(N_DEV, dff // N_DEV, d)]
    r_in, r_out_a, r_kv, r_q_b, r_out_b, r_up0, r_up1, r_down0, r_down1 = _exchange("scatter_grads", partials, scatter=True)
    small_parts, = _exchange("gather_small_grads", [_pack_small(grads)], scatter=False)

    res = {}
    res["w_in_a"] = [a[None] for a in _adamw("adam_w_in", r_in, w_in_a[0], m_w_in_a[0], v_w_in_a[0])]
    res["w_out_a"] = [a[None] for a in _adamw("adam_w_out_a", r_out_a, w_out_a[0], m_w_out_a[0], v_w_out_a[0])]
    res["w_kv"] = _adamw("adam_w_kv", r_kv, w_kv, m_w_kv, v_w_kv)
    res["w_q_b"] = [a[None] for a in _adamw("adam_w_q_b", r_q_b, w_q_b[0], m_w_q_b[0], v_w_q_b[0])]
    res["w_out_b"] = [a[None] for a in _adamw("adam_w_out_b", r_out_b, w_out_b[0], m_w_out_b[0], v_w_out_b[0])]
    up = [_adamw(f"adam_w_up{l}", r, w_up[l], m_w_up[l], v_w_up[l]) for l, r in enumerate((r_up0, r_up1))]
    res["w_up"] = [jnp.stack([up[0][i], up[1][i]]) for i in range(4)]
    down = [_adamw(f"adam_w_down{l}", r, w_down[l], m_w_down[l], v_w_down[l]) for l, r in enumerate((r_down0, r_down1))]
    res["w_down"] = [jnp.stack([down[0][i], down[1][i]]) for i in range(4)]
    small = _adamw("adam_small", small_parts, _pack_small(w), _pack_small(mom), _pack_small(var))
    small = [_unpack_small(s, w) for s in small]
    for k in SMALL:
        res[k] = [s[k] for s in small]

    outs = [loss, grad_x[None]]
    for i in range(4):
        outs.extend(res[k][i] for k in WEIGHTS)
    return tuple(outs)
```

```python
import functools

import numpy as np
import jax
import jax.numpy as jnp
from jax import lax
from jax.experimental import pallas as pl
from jax.experimental.pallas import tpu as pltpu

F32 = jnp.float32
BF16 = jnp.bfloat16

N_DEV = 8
HEAD_DIM = 64
WINDOW = 128
N_BUCKETS = 32
REL_MAX_DIST = 128
NORM_EPS = 1e-6
NEG = -1e30
LANES = 128
VMEM_LIMIT = 56 * 1024 * 1024

ADAM_LR = 0.001
ADAM_B1 = 0.9
ADAM_B2 = 0.999
ADAM_EPS = 1e-08
ADAM_WD = 0.01
ADAM_STEP = 10

SMALL = ("g_attn", "g_mlp", "b_f", "gq_a", "gk_a", "g_kv", "gk_b", "gq_b", "sinks", "rel_bias")
BIG = ("w_in_a", "w_out_a", "w_kv", "w_q_b", "w_out_b", "w_up", "w_down")
WEIGHTS = ("g_attn", "g_mlp", "w_in_a", "b_f", "gq_a", "gk_a", "w_out_a", "g_kv", "w_kv", "gk_b",
           "w_q_b", "gq_b", "sinks", "rel_bias", "w_out_b", "w_up", "w_down")


def _params(n_grid):
    return pltpu.CompilerParams(dimension_semantics=("arbitrary",) * n_grid, vmem_limit_bytes=VMEM_LIMIT)


def _sds(shape, dtype):
    return jax.ShapeDtypeStruct(tuple(shape), dtype)


def _rms(x, g):
    return (x * lax.rsqrt(jnp.mean(x * x, axis=-1, keepdims=True) + NORM_EPS)) * g


def _dot_nt(a, b):
    return lax.dot_general(a, b, (((1,), (1,)), ((), ())), preferred_element_type=F32)


def _dot_tn(a, b):
    return lax.dot_general(a, b, (((0,), (0,)), ((), ())), preferred_element_type=F32)


def _dot(a, b):
    return jnp.dot(a, b, preferred_element_type=F32)


def _dot_exact(a, b):
    return jnp.dot(a, b, preferred_element_type=F32, precision=lax.Precision.HIGHEST)


def _norm_matmul(name, x, g, w, *, tn, relu2=False):
    t, d = x.shape
    n = w.shape[1]
    tm = min(512, t)

    def body(x_ref, g_ref, w_ref, y_ref, xn_ref, *a_ref):
        @pl.when(pl.program_id(1) == 0)
        def _():
            xn_ref[...] = _rms(x_ref[...], g_ref[...]).astype(BF16)

        y = _dot(xn_ref[...], w_ref[...])
        y_ref[...] = y
        if relu2:
            a_ref[0][...] = jnp.square(jnp.maximum(y, 0.0)).astype(BF16)

    out_shape = [_sds((t, n), F32), _sds((t, d), BF16)]
    out_specs = [pl.BlockSpec((tm, tn), lambda i, j: (i, j)), pl.BlockSpec((tm, d), lambda i, j: (i, 0))]
    if relu2:
        out_shape.append(_sds((t, n), BF16))
        out_specs.append(pl.BlockSpec((tm, tn), lambda i, j: (i, j)))
    return pl.pallas_call(
        body, name=name, grid=(t // tm, n // tn),
        in_specs=[pl.BlockSpec((tm, d), lambda i, j: (i, 0)), pl.BlockSpec((1, d), lambda i, j: (0, 0)),
                  pl.BlockSpec((d, tn), lambda i, j: (0, j))],
        out_specs=out_specs, out_shape=out_shape, compiler_params=_params(2),
    )(x, g.reshape(1, d), w)


def _matmul_res(name, a, w, res, *, tn=512):
    t, k = a.shape
    n = w.shape[1]
    tm = min(512, t)

    def body(a_ref, w_ref, r_ref, o_ref):
        o_ref[...] = r_ref[...] + _dot(a_ref[...], w_ref[...])

    return pl.pallas_call(
        body, name=name, grid=(t // tm, n // tn),
        in_specs=[pl.BlockSpec((tm, k), lambda i, j: (i, 0)), pl.BlockSpec((k, tn), lambda i, j: (0, j)),
                  pl.BlockSpec((tm, tn), lambda i, j: (i, j))],
        out_specs=pl.BlockSpec((tm, tn), lambda i, j: (i, j)), out_shape=_sds((t, n), F32),
        compiler_params=_params(2),
    )(a, w, res)


def _matmul_nt(name, dy, w, *, u=None, tk=512):
    t, n = dy.shape
    k = w.shape[0]
    tm = min(512, t)

    def body(dy_ref, w_ref, *rest):
        o_ref = rest[-1]
        r = _dot_nt(dy_ref[...].astype(BF16), w_ref[...])
        if u is not None:
            r = r * (2.0 * jnp.maximum(rest[0][...], 0.0))
        o_ref[...] = r.astype(BF16)

    in_specs = [pl.BlockSpec((tm, n), lambda i, j: (i, 0)), pl.BlockSpec((tk, n), lambda i, j: (j, 0))]
    args = [dy, w]
    if u is not None:
        in_specs.append(pl.BlockSpec((tm, tk), lambda i, j: (i, j)))
        args.append(u)
    return pl.pallas_call(
        body, name=name, grid=(t // tm, k // tk), in_specs=in_specs,
        out_specs=pl.BlockSpec((tm, tk), lambda i, j: (i, j)), out_shape=_sds((t, k), BF16),
        compiler_params=_params(2),
    )(*args)


def _matmul_nt_rmsbwd(name, dy, w, x, g, dres):
    t, k = dy.shape
    d = w.shape[0]
    tm = min(256, t)

    def body(dy_ref, w_ref, x_ref, g_ref, r_ref, dx_ref, dg_ref):
        dxn = _dot_nt(dy_ref[...].astype(BF16), w_ref[...])
        _, vjp = jax.vjp(_rms, x_ref[...], g_ref[...])
        dx, dg = vjp(dxn)
        dx_ref[...] = r_ref[...] + dx

        @pl.when(pl.program_id(0) == 0)
        def _():
            dg_ref[...] = jnp.zeros_like(dg_ref)

        dg_ref[...] += dg

    row = lambda i: (i, 0)
    fixed = lambda i: (0, 0)
    return pl.pallas_call(
        body, name=name, grid=(t // tm,),
        in_specs=[pl.BlockSpec((tm, k), row), pl.BlockSpec((d, k), fixed), pl.BlockSpec((tm, d), row),
                  pl.BlockSpec((1, d), fixed), pl.BlockSpec((tm, d), row)],
        out_specs=[pl.BlockSpec((tm, d), row), pl.BlockSpec((1, d), fixed)],
        out_shape=[_sds((t, d), F32), _sds((1, d), F32)], compiler_params=_params(1),
    )(dy, w, x, g.reshape(1, d), dres)


def _matmul_tn(name, a, b, *, tk=512, tn=512):
    t, k = a.shape
    n = b.shape[1]
    tk, tn = min(tk, k), min(tn, n)

    def body(a_ref, b_ref, o_ref):
        o_ref[...] = _dot_tn(a_ref[...].astype(BF16), b_ref[...].astype(BF16)).astype(BF16)

    return pl.pallas_call(
        body, name=name, grid=(k // tk, n // tn),
        in_specs=[pl.BlockSpec((t, tk), lambda i, j: (0, i)), pl.BlockSpec((t, tn), lambda i, j: (0, j))],
        out_specs=pl.BlockSpec((tk, tn), lambda i, j: (i, j)), out_shape=_sds((k, n), BF16),
        compiler_params=_params(2),
    )(a, b)


def _loss_grad(name, y, target):
    t, d = y.shape
    tm = min(512, t)

    def body(y_ref, t_ref, dy_ref, l_ref):
        e = y_ref[...] - t_ref[...]
        dy_ref[...] = e * (1.0 / d)

        @pl.when(pl.program_id(0) == 0)
        def _():
            l_ref[...] = jnp.zeros_like(l_ref)

        l_ref[...] += 0.5 * jnp.sum(jnp.mean(e * e, axis=-1))

    row = lambda i: (i, 0)
    return pl.pallas_call(
        body, name=name, grid=(t // tm,), in_specs=[pl.BlockSpec((tm, d), row), pl.BlockSpec((tm, d), row)],
        out_specs=[pl.BlockSpec((tm, d), row), pl.BlockSpec((8, LANES), lambda i: (0, 0))],
        out_shape=[_sds((t, d), F32), _sds((8, LANES), F32)], compiler_params=_params(1),
    )(y, target)


def _gate_fwd(name, proj, b_pad, n_heads, gate_col):
    t = proj.shape[0]
    tb = min(256, t)
    tri = jnp.asarray(np.tril(np.ones((tb, tb), np.float32)))

    def body(p_ref, b_ref, tri_ref, c_ref, carry):
        @pl.when(pl.program_id(0) == 0)
        def _():
            carry[...] = jnp.zeros_like(carry)

        lane = lax.broadcasted_iota(jnp.int32, (tb, LANES), 1)
        lf = jnp.where(lane < n_heads, jax.nn.log_sigmoid(p_ref[...] + b_ref[...]), 0.0)
        c = _dot_exact(tri_ref[...], lf) + carry[0:1, :]
        c_ref[...] = c
        carry[...] = jnp.broadcast_to(c[tb - 1:tb, :], carry.shape)

    return pl.pallas_call(
        body, name=name, grid=(t // tb,),
        in_specs=[pl.BlockSpec((tb, LANES), lambda i: (i, gate_col)), pl.BlockSpec((1, LANES), lambda i: (0, 0)),
                  pl.BlockSpec((tb, tb), lambda i: (0, 0))],
        out_specs=pl.BlockSpec((tb, LANES), lambda i: (i, 0)), out_shape=_sds((t, LANES), F32),
        scratch_shapes=[pltpu.VMEM((8, LANES), F32)], compiler_params=_params(1),
    )(proj, b_pad, tri)


def _gate_bwd(name, proj, b_pad, dc, n_heads, gate_col):
    t = proj.shape[0]
    tb = min(256, t)
    nb = t // tb
    triu = jnp.asarray(np.triu(np.ones((tb, tb), np.float32)))

    def body(p_ref, b_ref, dc_ref, tri_ref, df_ref, db_ref, carry):
        @pl.when(pl.program_id(0) == 0)
        def _():
            carry[...] = jnp.zeros_like(carry)
            db_ref[...] = jnp.zeros_like(db_ref)

        dcv = dc_ref[...]
        dlf = _dot_exact(tri_ref[...], dcv) + carry[0:1, :]
        carry[...] = jnp.broadcast_to(dlf[0:1, :], carry.shape)
        lane = lax.broadcasted_iota(jnp.int32, (tb, LANES), 1)
        z = p_ref[...] + b_ref[...]
        df = jnp.where(lane < n_heads, dlf / (1.0 + jnp.exp(z)), 0.0)
        df_ref[...] = df.astype(BF16)
        db_ref[...] += jnp.sum(df, axis=0, keepdims=True)

    return pl.pallas_call(
        body, name=name, grid=(nb,),
        in_specs=[pl.BlockSpec((tb, LANES), lambda i: (nb - 1 - i, gate_col)), pl.BlockSpec((1, LANES), lambda i: (0, 0)),
                  pl.BlockSpec((tb, LANES), lambda i: (nb - 1 - i, 0)), pl.BlockSpec((tb, tb), lambda i: (0, 0))],
        out_specs=[pl.BlockSpec((tb, LANES), lambda i: (nb - 1 - i, 0)), pl.BlockSpec((1, LANES), lambda i: (0, 0))],
        out_shape=[_sds((t, LANES), BF16), _sds((1, LANES), F32)],
        scratch_shapes=[pltpu.VMEM((8, LANES), F32)], compiler_params=_params(1),
    )(proj, b_pad, dc, triu)


def _qhead(qp, g):
    return _rms(qp, g) * (HEAD_DIM ** -0.5)


def _column(mat, idx):
    lane = lax.broadcasted_iota(jnp.int32, mat.shape, 1)
    return jnp.sum(jnp.where(lane == idx, mat, 0.0), axis=1, keepdims=True)


def _fox_scores(qi, kk, ccol_i, crow, i, bq):
    length = kk.shape[0]
    s = _dot_nt(qi, kk) + ccol_i - crow[:, :length]
    row = lax.broadcasted_iota(jnp.int32, (bq, length), 0) + i * bq
    col = lax.broadcasted_iota(jnp.int32, (bq, length), 1)
    return jnp.where(col <= row, s, NEG)


def _fox_fwd(name, proj, c, crow, gq, gk, n_heads):
    t = proj.shape[0]
    hw = n_heads * HEAD_DIM
    npair = n_heads // 2
    bq = min(256, t)
    nq = t // bq

    def body(q_ref, k_ref, v_ref, c_ref, crow_ref, gq_ref, gk_ref, o_ref, lse_ref):
        hp = pl.program_id(0)
        lane = lax.broadcasted_iota(jnp.int32, (t, LANES), 1)
        outs, lse_pair = [], jnp.zeros((t, LANES), F32)
        for hh in range(2):
            sl = slice(hh * HEAD_DIM, (hh + 1) * HEAD_DIM)
            qn = _qhead(q_ref[:, sl], gq_ref[...]).astype(BF16)
            kn = _rms(k_ref[:, sl], gk_ref[...]).astype(BF16)
            vb = v_ref[:, sl].astype(BF16)
            ccol = _column(c_ref[...], 2 * hp + hh)
            cr = crow_ref[0, hh:hh + 1, :]
            o_blocks, lse_blocks = [], []
            for i in range(nq):
                length = (i + 1) * bq
                s = _fox_scores(qn[i * bq:(i + 1) * bq], kn[:length], ccol[i * bq:(i + 1) * bq], cr, i, bq)
                m = jnp.max(s, axis=1, keepdims=True)
                p = jnp.exp(s - m)
                l = jnp.sum(p, axis=1, keepdims=True)
                o_blocks.append(_dot(p.astype(BF16), vb[:length]) / l)
                lse_blocks.append(m + jnp.log(l))
            outs.append(jnp.concatenate(o_blocks, axis=0))
            lse_pair = jnp.where(lane == hh, jnp.concatenate(lse_blocks, axis=0), lse_pair)
        o_ref[...] = jnp.concatenate(outs, axis=1).astype(BF16)
        lse_ref[...] = lse_pair

    col = lambda off: (lambda h: (0, off + h))
    fixed = lambda h: (0, 0)
    return pl.pallas_call(
        body, name=name, grid=(npair,),
        in_specs=[pl.BlockSpec((t, LANES), col(0)), pl.BlockSpec((t, LANES), col(npair)), pl.BlockSpec((t, LANES), col(2 * npair)),
                  pl.BlockSpec((t, LANES), fixed), pl.BlockSpec((1, 2, t), lambda h: (h, 0, 0)),
                  pl.BlockSpec((1, HEAD_DIM), fixed), pl.BlockSpec((1, HEAD_DIM), fixed)],
        out_specs=[pl.BlockSpec((t, LANES), col(0)), pl.BlockSpec((t, LANES), col(0))],
        out_shape=[_sds((t, hw), BF16), _sds((t, npair * LANES), F32)], compiler_params=_params(1),
    )(proj, proj, proj, c, crow, gq, gk)


def _fox_bwd(name, proj, c, crow, gq, gk, lse, do, n_heads):
    t = proj.shape[0]
    hw = n_heads * HEAD_DIM
    npair = n_heads // 2
    bq = min(256, t)
    nq = t // bq

    def body(q_ref, k_ref, v_ref, c_ref, crow_ref, gq_ref, gk_ref, lse_ref, do_ref,
             dq_ref, dk_ref, dv_ref, dc_ref, dgq_ref, dgk_ref, dk_acc, dv_acc, dc_acc):
        hp = pl.program_id(0)

        @pl.when(hp == 0)
        def _():
            dgq_ref[...] = jnp.zeros_like(dgq_ref)
            dgk_ref[...] = jnp.zeros_like(dgk_ref)

        dc_ref[...] = jnp.zeros_like(dc_ref)
        dqs, dks, dvs = [], [], []
        for hh in range(2):
            sl = slice(hh * HEAD_DIM, (hh + 1) * HEAD_DIM)
            qp, kp = q_ref[:, sl], k_ref[:, sl]
            qf, q_vjp = jax.vjp(_qhead, qp, gq_ref[...])
            kf, k_vjp = jax.vjp(_rms, kp, gk_ref[...])
            qn, kn = qf.astype(BF16), kf.astype(BF16)
            vb = v_ref[:, sl].astype(BF16)
            dob = do_ref[:, sl]
            ccol = _column(c_ref[...], 2 * hp + hh)
            cr = crow_ref[0, hh:hh + 1, :]
            lse_h = lse_ref[:, hh:hh + 1]
            dk_acc[...] = jnp.zeros_like(dk_acc)
            dv_acc[...] = jnp.zeros_like(dv_acc)
            dc_acc[...] = jnp.zeros_like(dc_acc)
            dq_blocks = []
            for i in range(nq):
                rows = slice(i * bq, (i + 1) * bq)
                length = (i + 1) * bq
                qi, doi = qn[rows], dob[rows]
                s = _fox_scores(qi, kn[:length], ccol[rows], cr, i, bq)
                p = jnp.exp(s - lse_h[rows])
                dp = _dot_nt(doi, vb[:length])
                ds = p * (dp - jnp.sum(p * dp, axis=1, keepdims=True))
                dsb = ds.astype(BF16)
                dq_blocks.append(_dot(dsb, kn[:length]))
                dk_acc[0:length, :] += _dot_tn(dsb, qi)
                dv_acc[0:length, :] += _dot_tn(p.astype(BF16), doi)
                dc_acc[0:1, 0:length] -= jnp.sum(ds, axis=0, keepdims=True)
            dqp, dgq = q_vjp(jnp.concatenate(dq_blocks, axis=0))
            dkp, dgk = k_vjp(dk_acc[...])
            dgq_ref[...] += dgq
            dgk_ref[...] += dgk
            dqs.append(dqp)
            dks.append(dkp)
            dvs.append(dv_acc[...])
            dc_ref[0, hh:hh + 1, :] = dc_acc[0:1, :]
        dq_ref[...] = jnp.concatenate(dqs, axis=1).astype(BF16)
        dk_ref[...] = jnp.concatenate(dks, axis=1).astype(BF16)
        dv_ref[...] = jnp.concatenate(dvs, axis=1).astype(BF16)

    col = lambda off: (lambda h: (0, off + h))
    fixed = lambda h: (0, 0)
    pair_blk = pl.BlockSpec((t, LANES), col(0))
    return pl.pallas_call(
        body, name=name, grid=(npair,),
        in_specs=[pl.BlockSpec((t, LANES), col(0)), pl.BlockSpec((t, LANES), col(npair)), pl.BlockSpec((t, LANES), col(2 * npair)),
                  pl.BlockSpec((t, LANES), fixed), pl.BlockSpec((1, 2, t), lambda h: (h, 0, 0)),
                  pl.BlockSpec((1, HEAD_DIM), fixed), pl.BlockSpec((1, HEAD_DIM), fixed),
                  pair_blk, pair_blk],
        out_specs=[pair_blk, pair_blk, pair_blk, pl.BlockSpec((1, 8, t), lambda h: (h, 0, 0)),
                   pl.BlockSpec((1, HEAD_DIM), fixed), pl.BlockSpec((1, HEAD_DIM), fixed)],
        out_shape=[_sds((t, hw), BF16), _sds((t, hw), BF16), _sds((t, hw), BF16), _sds((npair, 8, t), F32),
                   _sds((1, HEAD_DIM), F32), _sds((1, HEAD_DIM), F32)],
        scratch_shapes=[pltpu.VMEM((t, HEAD_DIM), F32), pltpu.VMEM((t, HEAD_DIM), F32), pltpu.VMEM((8, t), F32)],
        compiler_params=_params(1),
    )(proj, proj, proj, c, crow, gq, gk, lse, do)


def _t5_bucket_table():
    dist = np.arange(WINDOW)[:, None] + WINDOW - np.arange(2 * WINDOW)[None, :]
    n = np.maximum(dist, 0)
    max_exact = N_BUCKETS // 2
    large = max_exact + (np.log(np.maximum(n, 1) / max_exact) / np.log(REL_MAX_DIST / max_exact)
                         * (N_BUCKETS - max_exact)).astype(np.int32)
    large = np.minimum(large, N_BUCKETS - 1)
    return np.where(n < max_exact, n, large).astype(np.int32).reshape(1, -1)


def _bias_expand(name, rel_bias_t):
    n_heads = rel_bias_t.shape[0]
    tbl = jnp.asarray(_t5_bucket_table())
    width = tbl.shape[1]

    def body(rb_ref, tbl_ref, o_ref):
        onehot = (lax.broadcasted_iota(jnp.int32, (N_BUCKETS, width), 0) == tbl_ref[...]).astype(F32)
        o_ref[...] = _dot_exact(rb_ref[...], onehot)

    return pl.pallas_call(body, name=name, out_shape=_sds((n_heads, width), F32), compiler_params=_params(0))(rel_bias_t, tbl)


def _bias_reduce(name, dbias):
    n_heads, width = dbias.shape
    tbl = jnp.asarray(_t5_bucket_table())

    def body(db_ref, tbl_ref, o_ref):
        onehot = (lax.broadcasted_iota(jnp.int32, (N_BUCKETS, width), 0) == tbl_ref[...]).astype(F32)
        o_ref[...] = lax.dot_general(db_ref[...], onehot, (((1,), (1,)), ((), ())), preferred_element_type=F32,
                                     precision=lax.Precision.HIGHEST)

    return pl.pallas_call(body, name=name, out_shape=_sds((n_heads, N_BUCKETS), F32), compiler_params=_params(0))(dbias, tbl)


def _swa_mask(n, rows):
    i = lax.broadcasted_iota(jnp.int32, (rows, 2 * WINDOW), 0) & (WINDOW - 1)
    j = lax.broadcasted_iota(jnp.int32, (rows, 2 * WINDOW), 1)
    ok = (j > i) & (j <= i + WINDOW) & ((n > 0) | (j >= WINDOW))
    return jnp.where(ok, 0.0, NEG)


def _swa_fwd(name, qb, kh, vh, gq, gk, sink_rows, bias, group):
    t = qb.shape[0]
    kvh = kh.shape[0]
    nblk = t // WINDOW
    rows = group * WINDOW
    gw = group * HEAD_DIM

    def body(q_ref, k_ref, v_ref, gq_ref, gk_ref, sink_ref, bias_ref, o_ref, lse_ref, qs, kpad, vpad):
        for g in range(group):
            qs[:, g * HEAD_DIM:(g + 1) * HEAD_DIM] = _qhead(q_ref[:, g * HEAD_DIM:(g + 1) * HEAD_DIM], gq_ref[...]).astype(BF16)
        kpad[0:WINDOW, :] = jnp.zeros((WINDOW, HEAD_DIM), BF16)
        vpad[0:WINDOW, :] = jnp.zeros((WINDOW, HEAD_DIM), BF16)
        kpad[WINDOW:, :] = _rms(k_ref[0], gk_ref[...]).astype(BF16)
        vpad[WINDOW:, :] = v_ref[0].astype(BF16)
        sink = sink_ref[0]
        lane = lax.broadcasted_iota(jnp.int32, (WINDOW, LANES), 1)

        def block(n, carry):
            start = pl.multiple_of(n * WINDOW, WINDOW)
            q = jnp.concatenate([qs[pl.ds(start, WINDOW), g * HEAD_DIM:(g + 1) * HEAD_DIM] for g in range(group)], axis=0)
            kb = kpad[pl.ds(start, 2 * WINDOW), :]
            vb = vpad[pl.ds(start, 2 * WINDOW), :]
            s = _dot_nt(q, kb) + bias_ref[0] + _swa_mask(n, rows)
            m = jnp.maximum(jnp.max(s, axis=1, keepdims=True), sink)
            e = jnp.exp(s - m)
            l = jnp.sum(e, axis=1, keepdims=True) + jnp.exp(sink - m)
            ob = (_dot(e.astype(BF16), vb) / l).astype(BF16)
            lse = m + jnp.log(l)
            lse_tile = jnp.zeros((WINDOW, LANES), F32)
            for g in range(group):
                o_ref[pl.ds(start, WINDOW), g * HEAD_DIM:(g + 1) * HEAD_DIM] = ob[g * WINDOW:(g + 1) * WINDOW]
                lse_tile = jnp.where(lane == g, lse[g * WINDOW:(g + 1) * WINDOW], lse_tile)
            lse_ref[pl.ds(start, WINDOW), :] = lse_tile
            return carry

        lax.fori_loop(0, nblk, block, 0)

    fixed = lambda h: (0, 0)
    per = lambda h: (h, 0, 0)
    return pl.pallas_call(
        body, name=name, grid=(kvh,),
        in_specs=[pl.BlockSpec((t, gw), lambda h: (0, h)), pl.BlockSpec((1, t, HEAD_DIM), per), pl.BlockSpec((1, t, HEAD_DIM), per),
                  pl.BlockSpec((1, HEAD_DIM), fixed), pl.BlockSpec((1, HEAD_DIM), fixed),
                  pl.BlockSpec((1, rows, 1), per), pl.BlockSpec((1, rows, 2 * WINDOW), per)],
        out_specs=[pl.BlockSpec((t, gw), lambda h: (0, h)), pl.BlockSpec((t, LANES), lambda h: (0, h))],
        out_shape=[_sds((t, kvh * gw), BF16), _sds((t, kvh * LANES), F32)],
        scratch_shapes=[pltpu.VMEM((t, gw), BF16), pltpu.VMEM((t + WINDOW, HEAD_DIM), BF16),
                        pltpu.VMEM((t + WINDOW, HEAD_DIM), BF16)],
        compiler_params=_params(1),
    )(qb, kh, vh, gq, gk, sink_rows, bias)


def _swa_bwd(name, qb, kh, vh, gq, gk, sink_rows, bias, lse, do, group):
    t = qb.shape[0]
    kvh = kh.shape[0]
    nblk = t // WINDOW
    rows = group * WINDOW
    gw = group * HEAD_DIM

    def body(q_ref, k_ref, v_ref, gq_ref, gk_ref, sink_ref, bias_ref, lse_ref, do_ref,
             dq_ref, dk_ref, dv_ref, dgq_ref, dgk_ref, dsink_ref, dbias_ref,
             qs, kpad, vpad, dqs, dk_acc, dv_acc, dsink_acc):
        @pl.when(pl.program_id(0) == 0)
        def _():
            dgq_ref[...] = jnp.zeros_like(dgq_ref)
            dgk_ref[...] = jnp.zeros_like(dgk_ref)

        for g in range(group):
            qs[:, g * HEAD_DIM:(g + 1) * HEAD_DIM] = _qhead(q_ref[:, g * HEAD_DIM:(g + 1) * HEAD_DIM], gq_ref[...]).astype(BF16)
        kpad[0:WINDOW, :] = jnp.zeros((WINDOW, HEAD_DIM), BF16)
        vpad[0:WINDOW, :] = jnp.zeros((WINDOW, HEAD_DIM), BF16)
        kpad[WINDOW:, :] = _rms(k_ref[0], gk_ref[...]).astype(BF16)
        vpad[WINDOW:, :] = v_ref[0].astype(BF16)
        dk_acc[...] = jnp.zeros_like(dk_acc)
        dv_acc[...] = jnp.zeros_like(dv_acc)
        dsink_acc[...] = jnp.zeros_like(dsink_acc)
        dbias_ref[...] = jnp.zeros_like(dbias_ref)
        sink = sink_ref[0]

        def block(n, carry):
            start = pl.multiple_of(n * WINDOW, WINDOW)
            stack = lambda ref: jnp.concatenate(
                [ref[pl.ds(start, WINDOW), g * HEAD_DIM:(g + 1) * HEAD_DIM] for g in range(group)], axis=0)
            q = jnp.concatenate([qs[pl.ds(start, WINDOW), g * HEAD_DIM:(g + 1) * HEAD_DIM] for g in range(group)], axis=0)
            dob = stack(do_ref)
            lse_tile = lse_ref[pl.ds(start, WINDOW), :]
            lse_s = jnp.concatenate([lse_tile[:, g:g + 1] for g in range(group)], axis=0)
            kb = kpad[pl.ds(start, 2 * WINDOW), :]
            vb = vpad[pl.ds(start, 2 * WINDOW), :]
            s = _dot_nt(q, kb) + bias_ref[0] + _swa_mask(n, rows)
            p = jnp.exp(s - lse_s)
            dp = _dot_nt(dob, vb)
            dsum = jnp.sum(p * dp, axis=1, keepdims=True)
            ds = p * (dp - dsum)
            dsb = ds.astype(BF16)
            dsink_acc[...] -= jnp.exp(sink - lse_s) * dsum
            dbias_ref[0] += ds
            dqb = _dot(dsb, kb)
            for g in range(group):
                dqs[pl.ds(start, WINDOW), g * HEAD_DIM:(g + 1) * HEAD_DIM] = dqb[g * WINDOW:(g + 1) * WINDOW]
            dk_acc[pl.ds(start, 2 * WINDOW), :] += _dot_tn(dsb, q)
            dv_acc[pl.ds(start, 2 * WINDOW), :] += _dot_tn(p.astype(BF16), dob)
            return carry

        lax.fori_loop(0, nblk, block, 0)
        for g in range(group):
            _, q_vjp = jax.vjp(_qhead, q_ref[:, g * HEAD_DIM:(g + 1) * HEAD_DIM], gq_ref[...])
            dqp, dgq = q_vjp(dqs[:, g * HEAD_DIM:(g + 1) * HEAD_DIM])
            dq_ref[:, g * HEAD_DIM:(g + 1) * HEAD_DIM] = dqp.astype(BF16)
            dgq_ref[...] += dgq
            dsink_g = jnp.sum(dsink_acc[g * WINDOW:(g + 1) * WINDOW, :], axis=0, keepdims=True)
            dsink_ref[0, g:g + 1, :] = jnp.broadcast_to(dsink_g, (1, LANES))
        _, k_vjp = jax.vjp(_rms, k_ref[0], gk_ref[...])
        dkp, dgk = k_vjp(dk_acc[WINDOW:, :])
        dk_ref[0] = dkp
        dgk_ref[...] += dgk
        dv_ref[0] = dv_acc[WINDOW:, :]

    fixed = lambda h: (0, 0)
    per = lambda h: (h, 0, 0)
    wide = pl.BlockSpec((t, gw), lambda h: (0, h))
    head = pl.BlockSpec((1, t, HEAD_DIM), per)
    vec = pl.BlockSpec((1, HEAD_DIM), fixed)
    return pl.pallas_call(
        body, name=name, grid=(kvh,),
        in_specs=[wide, head, head, vec, vec, pl.BlockSpec((1, rows, 1), per), pl.BlockSpec((1, rows, 2 * WINDOW), per),
                  pl.BlockSpec((t, LANES), lambda h: (0, h)), wide],
        out_specs=[wide, head, head, vec, vec, pl.BlockSpec((1, group, LANES), per), pl.BlockSpec((1, rows, 2 * WINDOW), per)],
        out_shape=[_sds((t, kvh * gw), BF16), _sds((kvh, t, HEAD_DIM), F32), _sds((kvh, t, HEAD_DIM), F32),
                   _sds((1, HEAD_DIM), F32), _sds((1, HEAD_DIM), F32),
                   _sds((kvh, group, LANES), F32), _sds((kvh, rows, 2 * WINDOW), F32)],
        scratch_shapes=[pltpu.VMEM((t, gw), BF16), pltpu.VMEM((t + WINDOW, HEAD_DIM), BF16),
                        pltpu.VMEM((t + WINDOW, HEAD_DIM), BF16), pltpu.VMEM((t, gw), F32),
                        pltpu.VMEM((t + WINDOW, HEAD_DIM), F32), pltpu.VMEM((t + WINDOW, HEAD_DIM), F32),
                        pltpu.VMEM((rows, 1), F32)],
        compiler_params=_params(1),
    )(qb, kh, vh, gq, gk, sink_rows, bias, lse, do)


def _local_step(x, target, p, comm):
    t, d = x.shape
    n_heads = d // HEAD_DIM
    kv_heads = n_heads // 8
    group = n_heads // kv_heads
    hw = n_heads * HEAD_DIM
    gate_col = 3 * hw // LANES
    kvw = kv_heads * HEAD_DIM
    grads = {}

    def mlp_fwd(tag, h, g, layer):
        w_up, = comm.weights([f"w_up{layer}"], h)
        u, hn, a = _norm_matmul(f"{tag}_up", h, g, w_up, tn=512, relu2=True)
        w_down, = comm.weights([f"w_down{layer}"], a)
        return _matmul_res(f"{tag}_down", a, w_down, h), (h, g, u, hn, a, w_up, w_down)

    def mlp_bwd(tag, saved, layer, dy):
        h, g, u, hn, a, w_up, w_down = saved
        du = _matmul_nt(f"{tag}_du", dy, w_down, u=u)
        dw_down = _matmul_tn(f"{tag}_dwdown", a, dy)
        dw_up = _matmul_tn(f"{tag}_dwup", hn, du)
        zero = comm.send_grads(tag, {f"w_down{layer}": dw_down, f"w_up{layer}": dw_up})
        return _matmul_nt_rmsbwd(f"{tag}_dh", du, w_up, h, g + zero, dy)

    w_in, = comm.weights(["w_in_a"], None)
    proj, xn1 = _norm_matmul("a_inproj", x, p["g_attn"][0], w_in, tn=640)
    b_pad = jnp.pad(p["b_f"], ((0, 0), (0, LANES - n_heads)))
    c = _gate_fwd("a_gate", proj, b_pad, n_heads, gate_col)
    crow = c[:, :n_heads].T.reshape(n_heads // 2, 2, t)
    o_a, lse_a = _fox_fwd("a_attn", proj, c, crow, p["gq_a"], p["gk_a"], n_heads)
    w_out_a, = comm.weights(["w_out_a"], o_a)
    h1 = _matmul_res("a_outproj", o_a, w_out_a, x)
    h2, mlp0 = mlp_fwd("mlp0", h1, p["g_mlp"][0], 0)

    w_kv, w_q_b = comm.weights(["w_kv", "w_q_b"], h2)
    kv, hn_kv = _norm_matmul("kv_proj", h2, p["g_kv"], w_kv, tn=2 * kvw)
    kh = kv[:, :kvw].reshape(t, kv_heads, HEAD_DIM).transpose(1, 0, 2)
    vh = kv[:, kvw:].reshape(t, kv_heads, HEAD_DIM).transpose(1, 0, 2)
    qb, hn_q = _norm_matmul("b_qproj", h2, p["g_attn"][1], w_q_b, tn=512)
    gqb, gkb = p["gq_b"], p["gk_b"].reshape(1, HEAD_DIM)
    bias = _bias_expand("b_bias", p["rel_bias"].T).reshape(kv_heads, group * WINDOW, 2 * WINDOW)
    sink_rows = jnp.broadcast_to(p["sinks"].reshape(n_heads, 1), (n_heads, WINDOW)).reshape(kv_heads, group * WINDOW, 1)
    o_b, lse_b = _swa_fwd("b_attn", qb, kh, vh, gqb, gkb, sink_rows, bias, group)
    w_out_b, = comm.weights(["w_out_b"], o_b)
    h3 = _matmul_res("b_outproj", o_b, w_out_b, h2)
    y, mlp1 = mlp_fwd("mlp1", h3, p["g_mlp"][1], 1)
    dy, loss_tile = _loss_grad("loss", y, target)

    dh3, dg_mlp1 = mlp_bwd("mlp1", mlp1, 1, dy)
    do_b = _matmul_nt("b_do", dh3, w_out_b)
    dw_out_b = _matmul_tn("b_dwout", o_b, dh3)
    dqb, dkh, dvh, grads["gq_b"], dgk_b, dsink, dbias = _swa_bwd(
        "b_attn_bwd", qb, kh, vh, gqb, gkb, sink_rows, bias, lse_b, do_b, group)
    grads["gk_b"] = dgk_b.reshape(HEAD_DIM)
    grads["sinks"] = dsink[:, :, 0].reshape(1, n_heads)
    grads["rel_bias"] = _bias_reduce("b_dbias", dbias.reshape(n_heads, WINDOW * 2 * WINDOW)).T
    dw_q_b = _matmul_tn("b_dwq", hn_q, dqb)
    dh2, dg_attn1 = _matmul_nt_rmsbwd("b_dhq", dqb, w_q_b, h2, p["g_attn"][1], dh3)
    dkv = jnp.concatenate([dkh.transpose(1, 0, 2).reshape(t, kvw), dvh.transpose(1, 0, 2).reshape(t, kvw)], axis=1)
    dw_kv = _matmul_tn("kv_dw", hn_kv, dkv)
    zero = comm.send_grads("attn_b", {"w_out_b": dw_out_b, "w_q_b": dw_q_b, "w_kv": dw_kv})
    dh2, dg_kv = _matmul_nt_rmsbwd("kv_dh", dkv, w_kv, h2, p["g_kv"] + zero, dh2)
    grads["g_kv"] = dg_kv.reshape(d)
    dh1, dg_mlp0 = mlp_bwd("mlp0", mlp0, 0, dh2)
    grads["g_mlp"] = jnp.concatenate([dg_mlp0, dg_mlp1], axis=0)

    do_a = _matmul_nt("a_do", dh1, w_out_a)
    dw_out_a = _matmul_tn("a_dwout", o_a, dh1)
    zero = comm.send_grads("attn_a_out", {"w_out_a": dw_out_a})
    dq, dk, dv, dc_rows, grads["gq_a"], grads["gk_a"] = _fox_bwd(
        "a_attn_bwd", proj, c, crow, p["gq_a"] + zero, p["gk_a"], lse_a, do_a, n_heads)
    dc = jnp.pad(dc_rows[:, :2, :].reshape(n_heads, t).T, ((0, 0), (0, LANES - n_heads)))
    dfl, db_f = _gate_bwd("a_gate_bwd", proj, b_pad, dc, n_heads, gate_col)
    grads["b_f"] = db_f[:, :n_heads]
    dproj = jnp.concatenate([dq, dk, dv, dfl], axis=1)
    dw_in = _matmul_tn("a_dwin", xn1, dproj, tn=640)
    zero = comm.send_grads("attn_a_in", {"w_in_a": dw_in})
    grad_x, dg_attn0 = _matmul_nt_rmsbwd("a_dx", dproj, w_in, x, p["g_attn"][0] + zero, dh1)
    grads["g_attn"] = jnp.concatenate([dg_attn0, dg_attn1], axis=0)
    return loss_tile, grad_x, grads


def _exchange(name, arrays, scatter):
    n = len(arrays)

    def body(*refs):
        src, out = refs[:n], refs[n:2 * n]
        send_sems, recv_sems, local_sems = refs[2 * n:]
        x, y, c = lax.axis_index("x"), lax.axis_index("y"), lax.axis_index("c")
        me = 4 * x + 2 * y + c
        peers = []
        for k in range(1, N_DEV):
            px, py, pc = x ^ ((k >> 2) & 1), y ^ ((k >> 1) & 1), c ^ (k & 1)
            peers.append(((px, py, pc), 4 * px + 2 * py + pc))
        local, sends = [], []
        for a in range(n):
            mine = src[a].at[me] if scatter else src[a]
            cp = pltpu.make_async_copy(mine, out[a].at[me], local_sems.at[a])
            cp.start()
            local.append(cp)
        for k, (peer, peer_pos) in enumerate(peers):
            for a in range(n):
                cp = pltpu.make_async_remote_copy(
                    src_ref=src[a].at[peer_pos] if scatter else src[a], dst_ref=out[a].at[me],
                    send_sem=send_sems.at[a, k], recv_sem=recv_sems.at[a, k],
                    device_id=peer, device_id_type=pl.DeviceIdType.MESH)
                cp.start()
                sends.append(cp)
        for k, (peer, peer_pos) in enumerate(peers):
            for a in range(n):
                pltpu.make_async_remote_copy(
                    src_ref=out[a].at[peer_pos], dst_ref=out[a].at[peer_pos],
                    send_sem=send_sems.at[a, k], recv_sem=recv_sems.at[a, k],
                    device_id=peer, device_id_type=pl.DeviceIdType.MESH).wait_recv()
        for cp in sends:
            cp.wait_send()
        for cp in local:
            cp.wait()

    any_spec = pl.BlockSpec(memory_space=pl.ANY)
    out_shape = [_sds(a.shape if scatter else (N_DEV,) + a.shape, a.dtype) for a in arrays]
    return pl.pallas_call(
        body, name=name, in_specs=[any_spec] * n, out_specs=[any_spec] * n, out_shape=out_shape,
        scratch_shapes=[pltpu.SemaphoreType.DMA((n, N_DEV - 1)), pltpu.SemaphoreType.DMA((n, N_DEV - 1)),
                        pltpu.SemaphoreType.DMA((n,))],
    )(*arrays)


class _InFlight:
    def __init__(self, scatter, send_sems, recv_sems, srcs, lands, token):
        self.scatter, self.send_sems, self.recv_sems = scatter, send_sems, recv_sems
        self.srcs, self.lands, self.token = list(srcs), list(lands), token


def _mesh_peers():
    x, y, c = lax.axis_index("x"), lax.axis_index("y"), lax.axis_index("c")
    peers = []
    for k in range(1, N_DEV):
        px, py, pc = x ^ ((k >> 2) & 1), y ^ ((k >> 1) & 1), c ^ (k & 1)
        peers.append(((px, py, pc), 4 * px + 2 * py + pc))
    return 4 * x + 2 * y + c, peers


_HBM_SPEC = pl.BlockSpec(memory_space=pltpu.HBM)
_SEM_SPEC = pl.BlockSpec(memory_space=pltpu.SEMAPHORE)
_SIDE_EFFECT = pltpu.SideEffectType.DATAFLOW_SIDE_EFFECTING


def _exchange_start(name, arrays, scatter):
    n = len(arrays)
    me, _ = _mesh_peers()
    lands = []
    for a in arrays:
        own = lax.dynamic_index_in_dim(a, me, 0, keepdims=False) if scatter else a
        shape = a.shape if scatter else (N_DEV,) + a.shape
        lands.append(lax.dynamic_update_index_in_dim(lax.empty(shape, a.dtype), own, me, 0))

    def body(*refs):
        src, land = refs[:n], refs[n:2 * n]
        send_sems, recv_sems, token = refs[2 * n], refs[2 * n + 1], refs[-1]
        pos, peers = _mesh_peers()
        for a in range(n):
            for k, (peer, peer_pos) in enumerate(peers):
                pltpu.make_async_remote_copy(
                    src_ref=src[a].at[peer_pos] if scatter else src[a], dst_ref=land[a].at[pos],
                    send_sem=send_sems.at[a * (N_DEV - 1) + k], recv_sem=recv_sems.at[a * (N_DEV - 1) + k],
                    device_id=peer, device_id_type=pl.DeviceIdType.MESH).start()
        token[...] = jnp.zeros_like(token)

    operands = [pltpu.with_memory_space_constraint(a, pltpu.HBM) for a in list(arrays) + lands]
    outs = pl.pallas_call(
        body, name=name,
        out_shape=(pltpu.SemaphoreType.DMA((n * (N_DEV - 1),)), pltpu.SemaphoreType.DMA((n * (N_DEV - 1),)),
                   *[pltpu.HBM(a.shape, a.dtype) for a in operands], _sds((8, LANES), F32)),
        in_specs=[_HBM_SPEC] * (2 * n),
        out_specs=(_SEM_SPEC, _SEM_SPEC, *[_HBM_SPEC] * (2 * n), pl.BlockSpec(memory_space=pltpu.VMEM)),
        input_output_aliases={i: 2 + i for i in range(2 * n)},
        compiler_params=pltpu.CompilerParams(has_side_effects=_SIDE_EFFECT),
    )(*operands)
    return _InFlight(scatter, outs[0], outs[1], outs[2:2 + n], outs[2 + n:2 + 2 * n], outs[-1])


def _exchange_wait(name, flight, which, after):
    m = len(which)
    scatter = flight.scatter

    def body(*refs):
        src, land = refs[:m], refs[m:2 * m]
        send_sems, recv_sems = refs[2 * m], refs[2 * m + 1]
        _, peers = _mesh_peers()
        for i, a in enumerate(which):
            for k, (peer, peer_pos) in enumerate(peers):
                cp = pltpu.make_async_remote_copy(
                    src_ref=src[i].at[peer_pos] if scatter else src[i], dst_ref=land[i].at[peer_pos],
                    send_sem=send_sems.at[a * (N_DEV - 1) + k], recv_sem=recv_sems.at[a * (N_DEV - 1) + k],
                    device_id=peer, device_id_type=pl.DeviceIdType.MESH)
                cp.wait_send()
                cp.wait_recv()

    operands = [flight.srcs[a] for a in which] + [flight.lands[a] for a in which]
    outs = pl.pallas_call(
        body, name=name, out_shape=tuple(pltpu.HBM(a.shape, a.dtype) for a in operands),
        in_specs=[_HBM_SPEC] * (2 * m) + [_SEM_SPEC, _SEM_SPEC, pl.BlockSpec(memory_space=pl.ANY)],
        out_specs=tuple([_HBM_SPEC] * (2 * m)), input_output_aliases={i: i for i in range(2 * m)},
        compiler_params=pltpu.CompilerParams(has_side_effects=_SIDE_EFFECT),
    )(*operands, flight.send_sems, flight.recv_sems, after)
    return list(outs[m:])


def _adamw(name, parts, w, m, v):
    r, c = w.shape
    tr = r if r <= 256 else 256

    def body(p_ref, w_ref, m_ref, v_ref, g_ref, d_ref, mo_ref, vo_ref):
        g = p_ref[0].astype(F32)
        for dev in range(1, N_DEV):
            g = g + p_ref[dev].astype(F32)
        m_new = ADAM_B1 * m_ref[...] + (1.0 - ADAM_B1) * g
        v_new = ADAM_B2 * v_ref[...] + (1.0 - ADAM_B2) * jnp.square(g)
        m_hat = m_new / (1.0 - ADAM_B1 ** ADAM_STEP)
        v_hat = v_new / (1.0 - ADAM_B2 ** ADAM_STEP)
        g_ref[...] = g
        d_ref[...] = -ADAM_LR * (m_hat / (jnp.sqrt(v_hat) + ADAM_EPS) + ADAM_WD * w_ref[...])
        mo_ref[...] = m_new
        vo_ref[...] = v_new

    blk = pl.BlockSpec((tr, c), lambda i: (i, 0))
    return pl.pallas_call(
        body, name=name, grid=(r // tr,),
        in_specs=[pl.BlockSpec((N_DEV, tr, c), lambda i: (0, i, 0)), blk, blk, blk],
        out_specs=[blk] * 4, out_shape=[_sds((r, c), F32)] * 4, compiler_params=_params(1),
    )(parts, w, m, v)


def _pack_small(tree):
    flat = jnp.concatenate([tree[k].reshape(-1) for k in SMALL])
    size = -(-flat.shape[0] // (8 * LANES)) * (8 * LANES)
    return jnp.pad(flat, (0, size - flat.shape[0])).reshape(-1, LANES)


def _unpack_small(packed, like):
    flat, out, off = packed.reshape(-1), {}, 0
    for k in SMALL:
        size = like[k].size
        out[k] = flat[off:off + size].reshape(like[k].shape)
        off += size
    return out


class _Comm:
    ORDER = ("w_in_a", "w_out_a", "w_up0", "w_down0", "w_kv", "w_q_b", "w_out_b", "w_up1", "w_down1")

    def __init__(self, shards, d, dff, n_in):
        self.d, self.dff, self.n_in = d, dff, n_in
        self.flight = _exchange_start("gather_start", [shards[n].astype(BF16) for n in self.ORDER], scatter=False)
        self.sent = []

    def weights(self, names, after):
        which = [self.ORDER.index(n) for n in names]
        landed = _exchange_wait(f"gather_wait_{names[0]}", self.flight, which, self.flight.token if after is None else after)
        return [self._whole(n, g) for n, g in zip(names, landed)]

    def _whole(self, name, g):
        if name == "w_in_a":
            pad = -(-self.n_in // LANES) * LANES - self.n_in
            return jnp.pad(g.transpose(1, 0, 2).reshape(self.d, self.n_in), ((0, 0), (0, pad)))
        if name.startswith("w_up"):
            return g.transpose(1, 0, 2).reshape(self.d, self.dff)
        return g.reshape(-1, g.shape[-1])

    def _chunks(self, name, g):
        if name == "w_in_a":
            return g[:, :self.n_in].reshape(self.d, N_DEV, -1).transpose(1, 0, 2)
        if name.startswith("w_up"):
            return g.reshape(self.d, N_DEV, -1).transpose(1, 0, 2)
        return g.reshape(N_DEV, g.shape[0] // N_DEV, g.shape[1])

    def send_grads(self, tag, partials):
        names = list(partials)
        flight = _exchange_start(f"scatter_start_{tag}", [self._chunks(n, partials[n]) for n in names], scatter=True)
        self.sent.append((tag, flight, names))
        return flight.token[0, 0]

    def received(self, after):
        out = {}
        for tag, flight, names in self.sent:
            landed = _exchange_wait(f"scatter_wait_{tag}", flight, list(range(len(names))), after)
            out.update(zip(names, landed))
        return out


def kernel(x, g_attn, g_mlp, w_in_a, b_f, gq_a, gk_a, w_out_a, g_kv, w_kv, gk_b, w_q_b, gq_b, sinks, rel_bias, w_out_b, w_up, w_down, loss_target, m_g_attn, m_g_mlp, m_w_in_a, m_b_f, m_gq_a, m_gk_a, m_w_out_a, m_g_kv, m_w_kv, m_gk_b, m_w_q_b, m_gq_b, m_sinks, m_rel_bias, m_w_out_b, m_w_up, m_w_down, v_g_attn, v_g_mlp, v_w_in_a, v_b_f, v_gq_a, v_gk_a, v_w_out_a, v_g_kv, v_w_kv, v_gk_b, v_w_q_b, v_gq_b, v_sinks, v_rel_bias, v_w_out_b, v_w_up, v_w_down):
    w = dict(g_attn=g_attn, g_mlp=g_mlp, w_in_a=w_in_a, b_f=b_f, gq_a=gq_a, gk_a=gk_a, w_out_a=w_out_a, g_kv=g_kv,
             w_kv=w_kv, gk_b=gk_b, w_q_b=w_q_b, gq_b=gq_b, sinks=sinks, rel_bias=rel_bias, w_out_b=w_out_b,
             w_up=w_up, w_down=w_down)
    mom = dict(g_attn=m_g_attn, g_mlp=m_g_mlp, w_in_a=m_w_in_a, b_f=m_b_f, gq_a=m_gq_a, gk_a=m_gk_a, w_out_a=m_w_out_a,
               g_kv=m_g_kv, w_kv=m_w_kv, gk_b=m_gk_b, w_q_b=m_w_q_b, gq_b=m_gq_b, sinks=m_sinks, rel_bias=m_rel_bias,
               w_out_b=m_w_out_b, w_up=m_w_up, w_down=m_w_down)
    var = dict(g_attn=v_g_attn, g_mlp=v_g_mlp, w_in_a=v_w_in_a, b_f=v_b_f, gq_a=v_gq_a, gk_a=v_gk_a, w_out_a=v_w_out_a,
               g_kv=v_g_kv, w_kv=v_w_kv, gk_b=v_gk_b, w_q_b=v_w_q_b, gq_b=v_gq_b, sinks=v_sinks, rel_bias=v_rel_bias,
               w_out_b=v_w_out_b, w_up=v_w_up, w_down=v_w_down)
    d = x.shape[2]

    def shard(tree):
        return {"w_in_a": tree["w_in_a"][0], "w_out_a": tree["w_out_a"][0], "w_kv": tree["w_kv"], "w_q_b": tree["w_q_b"][0],
                "w_out_b": tree["w_out_b"][0], "w_up0": tree["w_up"][0], "w_up1": tree["w_up"][1],
                "w_down0": tree["w_down"][0], "w_down1": tree["w_down"][1]}

    w_sh, m_sh, v_sh = shard(w), shard(mom), shard(var)
    comm = _Comm(w_sh, d, w_up.shape[2] * N_DEV, w_in_a.shape[2] * N_DEV)
    loss_tile, grad_x, grads = _local_step(x[0], loss_target[0], {k: w[k] for k in SMALL}, comm)
    loss = lax.psum(loss_tile[0, 0], ("x", "y", "c"))

    small_parts, = _exchange("gather_small_grads", [_pack_small(grads)], scatter=False)
    parts = comm.received(grad_x)
    big = {n: _adamw(f"adam_{n}", parts[n], w_sh[n], m_sh[n], v_sh[n]) for n in _Comm.ORDER}
    res = {}
    for n in ("w_in_a", "w_out_a", "w_q_b", "w_out_b"):
        res[n] = [a[None] for a in big[n]]
    res["w_kv"] = big["w_kv"]
    res["w_up"] = [jnp.stack([big["w_up0"][i], big["w_up1"][i]]) for i in range(4)]
    res["w_down"] = [jnp.stack([big["w_down0"][i], big["w_down1"][i]]) for i in range(4)]
    small = _adamw("adam_small", small_parts, _pack_small(w), _pack_small(mom), _pack_small(var))
    small = [_unpack_small(s, w) for s in small]
    for k in SMALL:
        res[k] = [s[k] for s in small]

    outs = [loss, grad_x[None]]
    for i in range(4):
        outs.extend(res[k][i] for k in WEIGHTS)
    return tuple(outs)
```

```python
import functools

import numpy as np
import jax
import jax.numpy as jnp
from jax import lax
from jax.experimental import pallas as pl
from jax.experimental.pallas import tpu as pltpu

F32 = jnp.float32
BF16 = jnp.bfloat16

N_DEV = 8
HEAD_DIM = 64
WINDOW = 128
N_BUCKETS = 32
REL_MAX_DIST = 128
NORM_EPS = 1e-6
NEG = -1e30
LANES = 128
VMEM_LIMIT = 56 * 1024 * 1024

ADAM_LR = 0.001
ADAM_B1 = 0.9
ADAM_B2 = 0.999
ADAM_EPS = 1e-08
ADAM_WD = 0.01
ADAM_STEP = 10

SMALL = ("g_attn", "g_mlp", "b_f", "gq_a", "gk_a", "g_kv", "gk_b", "gq_b", "sinks", "rel_bias")
BIG = ("w_in_a", "w_out_a", "w_kv", "w_q_b", "w_out_b", "w_up", "w_down")
WEIGHTS = ("g_attn", "g_mlp", "w_in_a", "b_f", "gq_a", "gk_a", "w_out_a", "g_kv", "w_kv", "gk_b",
           "w_q_b", "gq_b", "sinks", "rel_bias", "w_out_b", "w_up", "w_down")


def _params(n_grid):
    return pltpu.CompilerParams(dimension_semantics=("arbitrary",) * n_grid, vmem_limit_bytes=VMEM_LIMIT)


def _sds(shape, dtype):
    return jax.ShapeDtypeStruct(tuple(shape), dtype)


def _rms(x, g):
    return (x * lax.rsqrt(jnp.mean(x * x, axis=-1, keepdims=True) + NORM_EPS)) * g


def _dot_nt(a, b):
    return lax.dot_general(a, b, (((1,), (1,)), ((), ())), preferred_element_type=F32)


def _dot_tn(a, b):
    return lax.dot_general(a, b, (((0,), (0,)), ((), ())), preferred_element_type=F32)


def _dot(a, b):
    return jnp.dot(a, b, preferred_element_type=F32)


def _dot_exact(a, b):
    return jnp.dot(a, b, preferred_element_type=F32, precision=lax.Precision.HIGHEST)


def _norm_matmul(name, x, g, w, *, tn, relu2=False):
    t, d = x.shape
    n = w.shape[1]
    tm = min(1024, t)

    def body(x_ref, g_ref, w_ref, y_ref, xn_ref, *a_ref):
        @pl.when(pl.program_id(1) == 0)
        def _():
            xn_ref[...] = _rms(x_ref[...], g_ref[...]).astype(BF16)

        y = _dot(xn_ref[...], w_ref[...])
        y_ref[...] = y
        if relu2:
            a_ref[0][...] = jnp.square(jnp.maximum(y, 0.0)).astype(BF16)

    out_shape = [_sds((t, n), F32), _sds((t, d), BF16)]
    out_specs = [pl.BlockSpec((tm, tn), lambda i, j: (i, j)), pl.BlockSpec((tm, d), lambda i, j: (i, 0))]
    if relu2:
        out_shape.append(_sds((t, n), BF16))
        out_specs.append(pl.BlockSpec((tm, tn), lambda i, j: (i, j)))
    return pl.pallas_call(
        body, name=name, grid=(t // tm, n // tn),
        in_specs=[pl.BlockSpec((tm, d), lambda i, j: (i, 0)), pl.BlockSpec((1, d), lambda i, j: (0, 0)),
                  pl.BlockSpec((d, tn), lambda i, j: (0, j))],
        out_specs=out_specs, out_shape=out_shape, compiler_params=_params(2),
    )(x, g.reshape(1, d), w)


def _matmul_res(name, a, w, res, *, tn=512):
    t, k = a.shape
    n = w.shape[1]
    tm = min(1024, t)

    def body(a_ref, w_ref, r_ref, o_ref):
        o_ref[...] = r_ref[...] + _dot(a_ref[...], w_ref[...])

    return pl.pallas_call(
        body, name=name, grid=(t // tm, n // tn),
        in_specs=[pl.BlockSpec((tm, k), lambda i, j: (i, 0)), pl.BlockSpec((k, tn), lambda i, j: (0, j)),
                  pl.BlockSpec((tm, tn), lambda i, j: (i, j))],
        out_specs=pl.BlockSpec((tm, tn), lambda i, j: (i, j)), out_shape=_sds((t, n), F32),
        compiler_params=_params(2),
    )(a, w, res)


def _matmul_nt(name, dy, w, *, u=None, tk=1024):
    t, n = dy.shape
    k = w.shape[0]
    tm = min(1024, t)

    def body(dy_ref, w_ref, *rest):
        o_ref = rest[-1]
        r = _dot_nt(dy_ref[...].astype(BF16), w_ref[...])
        if u is not None:
            r = r * (2.0 * jnp.maximum(rest[0][...], 0.0))
        o_ref[...] = r.astype(BF16)

    in_specs = [pl.BlockSpec((tm, n), lambda i, j: (i, 0)), pl.BlockSpec((tk, n), lambda i, j: (j, 0))]
    args = [dy, w]
    if u is not None:
        in_specs.append(pl.BlockSpec((tm, tk), lambda i, j: (i, j)))
        args.append(u)
    return pl.pallas_call(
        body, name=name, grid=(t // tm, k // tk), in_specs=in_specs,
        out_specs=pl.BlockSpec((tm, tk), lambda i, j: (i, j)), out_shape=_sds((t, k), BF16),
        compiler_params=_params(2),
    )(*args)


def _matmul_nt_rmsbwd(name, dy, w, x, g, dres):
    t, k = dy.shape
    d = w.shape[0]
    tm = min(512, t)

    def body(dy_ref, w_ref, x_ref, g_ref, r_ref, dx_ref, dg_ref):
        dxn = _dot_nt(dy_ref[...].astype(BF16), w_ref[...])
        _, vjp = jax.vjp(_rms, x_ref[...], g_ref[...])
        dx, dg = vjp(dxn)
        dx_ref[...] = r_ref[...] + dx

        @pl.when(pl.program_id(0) == 0)
        def _():
            dg_ref[...] = jnp.zeros_like(dg_ref)

        dg_ref[...] += dg

    row = lambda i: (i, 0)
    fixed = lambda i: (0, 0)
    return pl.pallas_call(
        body, name=name, grid=(t // tm,),
        in_specs=[pl.BlockSpec((tm, k), row), pl.BlockSpec((d, k), fixed), pl.BlockSpec((tm, d), row),
                  pl.BlockSpec((1, d), fixed), pl.BlockSpec((tm, d), row)],
        out_specs=[pl.BlockSpec((tm, d), row), pl.BlockSpec((1, d), fixed)],
        out_shape=[_sds((t, d), F32), _sds((1, d), F32)], compiler_params=_params(1),
    )(dy, w, x, g.reshape(1, d), dres)


def _matmul_tn(name, a, b, *, tk=1024, tn=1024):
    t, k = a.shape
    n = b.shape[1]
    tk, tn = min(tk, k), min(tn, n)

    def body(a_ref, b_ref, o_ref):
        o_ref[...] = _dot_tn(a_ref[...].astype(BF16), b_ref[...].astype(BF16)).astype(BF16)

    return pl.pallas_call(
        body, name=name, grid=(k // tk, n // tn),
        in_specs=[pl.BlockSpec((t, tk), lambda i, j: (0, i)), pl.BlockSpec((t, tn), lambda i, j: (0, j))],
        out_specs=pl.BlockSpec((tk, tn), lambda i, j: (i, j)), out_shape=_sds((k, n), BF16),
        compiler_params=_params(2),
    )(a, b)


def _loss_grad(name, y, target):
    t, d = y.shape
    tm = min(512, t)

    def body(y_ref, t_ref, dy_ref, l_ref):
        e = y_ref[...] - t_ref[...]
        dy_ref[...] = e * (1.0 / d)

        @pl.when(pl.program_id(0) == 0)
        def _():
            l_ref[...] = jnp.zeros_like(l_ref)

        l_ref[...] += 0.5 * jnp.sum(jnp.mean(e * e, axis=-1))

    row = lambda i: (i, 0)
    return pl.pallas_call(
        body, name=name, grid=(t // tm,), in_specs=[pl.BlockSpec((tm, d), row), pl.BlockSpec((tm, d), row)],
        out_specs=[pl.BlockSpec((tm, d), row), pl.BlockSpec((8, LANES), lambda i: (0, 0))],
        out_shape=[_sds((t, d), F32), _sds((8, LANES), F32)], compiler_params=_params(1),
    )(y, target)


def _gate_fwd(name, proj, b_pad, n_heads, gate_col):
    t = proj.shape[0]
    tb = min(256, t)
    tri = jnp.asarray(np.tril(np.ones((tb, tb), np.float32)))

    def body(p_ref, b_ref, tri_ref, c_ref, carry):
        @pl.when(pl.program_id(0) == 0)
        def _():
            carry[...] = jnp.zeros_like(carry)

        lane = lax.broadcasted_iota(jnp.int32, (tb, LANES), 1)
        lf = jnp.where(lane < n_heads, jax.nn.log_sigmoid(p_ref[...] + b_ref[...]), 0.0)
        c = _dot_exact(tri_ref[...], lf) + carry[0:1, :]
        c_ref[...] = c
        carry[...] = jnp.broadcast_to(c[tb - 1:tb, :], carry.shape)

    return pl.pallas_call(
        body, name=name, grid=(t // tb,),
        in_specs=[pl.BlockSpec((tb, LANES), lambda i: (i, gate_col)), pl.BlockSpec((1, LANES), lambda i: (0, 0)),
                  pl.BlockSpec((tb, tb), lambda i: (0, 0))],
        out_specs=pl.BlockSpec((tb, LANES), lambda i: (i, 0)), out_shape=_sds((t, LANES), F32),
        scratch_shapes=[pltpu.VMEM((8, LANES), F32)], compiler_params=_params(1),
    )(proj, b_pad, tri)


def _gate_bwd(name, proj, b_pad, dc, n_heads, gate_col):
    t = proj.shape[0]
    tb = min(256, t)
    nb = t // tb
    triu = jnp.asarray(np.triu(np.ones((tb, tb), np.float32)))

    def body(p_ref, b_ref, dc_ref, tri_ref, df_ref, db_ref, carry):
        @pl.when(pl.program_id(0) == 0)
        def _():
            carry[...] = jnp.zeros_like(carry)
            db_ref[...] = jnp.zeros_like(db_ref)

        dcv = dc_ref[...]
        dlf = _dot_exact(tri_ref[...], dcv) + carry[0:1, :]
        carry[...] = jnp.broadcast_to(dlf[0:1, :], carry.shape)
        lane = lax.broadcasted_iota(jnp.int32, (tb, LANES), 1)
        z = p_ref[...] + b_ref[...]
        df = jnp.where(lane < n_heads, dlf / (1.0 + jnp.exp(z)), 0.0)
        df_ref[...] = df.astype(BF16)
        db_ref[...] += jnp.sum(df, axis=0, keepdims=True)

    return pl.pallas_call(
        body, name=name, grid=(nb,),
        in_specs=[pl.BlockSpec((tb, LANES), lambda i: (nb - 1 - i, gate_col)), pl.BlockSpec((1, LANES), lambda i: (0, 0)),
                  pl.BlockSpec((tb, LANES), lambda i: (nb - 1 - i, 0)), pl.BlockSpec((tb, tb), lambda i: (0, 0))],
        out_specs=[pl.BlockSpec((tb, LANES), lambda i: (nb - 1 - i, 0)), pl.BlockSpec((1, LANES), lambda i: (0, 0))],
        out_shape=[_sds((t, LANES), BF16), _sds((1, LANES), F32)],
        scratch_shapes=[pltpu.VMEM((8, LANES), F32)], compiler_params=_params(1),
    )(proj, b_pad, dc, triu)


def _qhead(qp, g):
    return _rms(qp, g) * (HEAD_DIM ** -0.5)


def _column(mat, idx):
    lane = lax.broadcasted_iota(jnp.int32, mat.shape, 1)
    return jnp.sum(jnp.where(lane == idx, mat, 0.0), axis=1, keepdims=True)


def _fox_scores(qi, kk, ccol_i, crow, i, bq):
    length = kk.shape[0]
    s = _dot_nt(qi, kk) + ccol_i - crow[:, :length]
    row = lax.broadcasted_iota(jnp.int32, (bq, length), 0) + i * bq
    col = lax.broadcasted_iota(jnp.int32, (bq, length), 1)
    return jnp.where(col <= row, s, NEG)


def _fox_fwd(name, proj, c, crow, gq, gk, n_heads):
    t = proj.shape[0]
    hw = n_heads * HEAD_DIM
    npair = n_heads // 2
    bq = min(256, t)
    nq = t // bq

    def body(q_ref, k_ref, v_ref, c_ref, crow_ref, gq_ref, gk_ref, o_ref, lse_ref):
        hp = pl.program_id(0)
        lane = lax.broadcasted_iota(jnp.int32, (t, LANES), 1)
        outs, lse_pair = [], jnp.zeros((t, LANES), F32)
        for hh in range(2):
            sl = slice(hh * HEAD_DIM, (hh + 1) * HEAD_DIM)
            qn = _qhead(q_ref[:, sl], gq_ref[...]).astype(BF16)
            kn = _rms(k_ref[:, sl], gk_ref[...]).astype(BF16)
            vb = v_ref[:, sl].astype(BF16)
            ccol = _column(c_ref[...], 2 * hp + hh)
            cr = crow_ref[0, hh:hh + 1, :]
            o_blocks, lse_blocks = [], []
            for i in range(nq):
                length = (i + 1) * bq
                s = _fox_scores(qn[i * bq:(i + 1) * bq], kn[:length], ccol[i * bq:(i + 1) * bq], cr, i, bq)
                m = jnp.max(s, axis=1, keepdims=True)
                p = jnp.exp(s - m)
                l = jnp.sum(p, axis=1, keepdims=True)
                o_blocks.append(_dot(p.astype(BF16), vb[:length]) / l)
                lse_blocks.append(m + jnp.log(l))
            outs.append(jnp.concatenate(o_blocks, axis=0))
            lse_pair = jnp.where(lane == hh, jnp.concatenate(lse_blocks, axis=0), lse_pair)
        o_ref[...] = jnp.concatenate(outs, axis=1).astype(BF16)
        lse_ref[...] = lse_pair

    col = lambda off: (lambda h: (0, off + h))
    fixed = lambda h: (0, 0)
    return pl.pallas_call(
        body, name=name, grid=(npair,),
        in_specs=[pl.BlockSpec((t, LANES), col(0)), pl.BlockSpec((t, LANES), col(npair)), pl.BlockSpec((t, LANES), col(2 * npair)),
                  pl.BlockSpec((t, LANES), fixed), pl.BlockSpec((1, 2, t), lambda h: (h, 0, 0)),
                  pl.BlockSpec((1, HEAD_DIM), fixed), pl.BlockSpec((1, HEAD_DIM), fixed)],
        out_specs=[pl.BlockSpec((t, LANES), col(0)), pl.BlockSpec((t, LANES), col(0))],
        out_shape=[_sds((t, hw), BF16), _sds((t, npair * LANES), F32)], compiler_params=_params(1),
    )(proj, proj, proj, c, crow, gq, gk)


def _fox_bwd(name, proj, c, crow, gq, gk, lse, do, n_heads):
    t = proj.shape[0]
    hw = n_heads * HEAD_DIM
    npair = n_heads // 2
    bq = min(256, t)
    nq = t // bq

    def body(q_ref, k_ref, v_ref, c_ref, crow_ref, gq_ref, gk_ref, lse_ref, do_ref,
             dq_ref, dk_ref, dv_ref, dc_ref, dgq_ref, dgk_ref, dk_acc, dv_acc, dc_acc):
        hp = pl.program_id(0)

        @pl.when(hp == 0)
        def _():
            dgq_ref[...] = jnp.zeros_like(dgq_ref)
            dgk_ref[...] = jnp.zeros_like(dgk_ref)

        dc_ref[...] = jnp.zeros_like(dc_ref)
        dqs, dks, dvs = [], [], []
        for hh in range(2):
            sl = slice(hh * HEAD_DIM, (hh + 1) * HEAD_DIM)
            qp, kp = q_ref[:, sl], k_ref[:, sl]
            qf, q_vjp = jax.vjp(_qhead, qp, gq_ref[...])
            kf, k_vjp = jax.vjp(_rms, kp, gk_ref[...])
            qn, kn = qf.astype(BF16), kf.astype(BF16)
            vb = v_ref[:, sl].astype(BF16)
            dob = do_ref[:, sl]
            ccol = _column(c_ref[...], 2 * hp + hh)
            cr = crow_ref[0, hh:hh + 1, :]
            lse_h = lse_ref[:, hh:hh + 1]
            dk_acc[...] = jnp.zeros_like(dk_acc)
            dv_acc[...] = jnp.zeros_like(dv_acc)
            dc_acc[...] = jnp.zeros_like(dc_acc)
            dq_blocks = []
            for i in range(nq):
                rows = slice(i * bq, (i + 1) * bq)
                length = (i + 1) * bq
                qi, doi = qn[rows], dob[rows]
                s = _fox_scores(qi, kn[:length], ccol[rows], cr, i, bq)
                p = jnp.exp(s - lse_h[rows])
                dp = _dot_nt(doi, vb[:length])
                ds = p * (dp - jnp.sum(p * dp, axis=1, keepdims=True))
                dsb = ds.astype(BF16)
                dq_blocks.append(_dot(dsb, kn[:length]))
                dk_acc[0:length, :] += _dot_tn(dsb, qi)
                dv_acc[0:length, :] += _dot_tn(p.astype(BF16), doi)
                dc_acc[0:1, 0:length] -= jnp.sum(ds, axis=0, keepdims=True)
            dqp, dgq = q_vjp(jnp.concatenate(dq_blocks, axis=0))
            dkp, dgk = k_vjp(dk_acc[...])
            dgq_ref[...] += dgq
            dgk_ref[...] += dgk
            dqs.append(dqp)
            dks.append(dkp)
            dvs.append(dv_acc[...])
            dc_ref[0, hh:hh + 1, :] = dc_acc[0:1, :]
        dq_ref[...] = jnp.concatenate(dqs, axis=1).astype(BF16)
        dk_ref[...] = jnp.concatenate(dks, axis=1).astype(BF16)
        dv_ref[...] = jnp.concatenate(dvs, axis=1).astype(BF16)

    col = lambda off: (lambda h: (0, off + h))
    fixed = lambda h: (0, 0)
    pair_blk = pl.BlockSpec((t, LANES), col(0))
    return pl.pallas_call(
        body, name=name, grid=(npair,),
        in_specs=[pl.BlockSpec((t, LANES), col(0)), pl.BlockSpec((t, LANES), col(npair)), pl.BlockSpec((t, LANES), col(2 * npair)),
                  pl.BlockSpec((t, LANES), fixed), pl.BlockSpec((1, 2, t), lambda h: (h, 0, 0)),
                  pl.BlockSpec((1, HEAD_DIM), fixed), pl.BlockSpec((1, HEAD_DIM), fixed),
                  pair_blk, pair_blk],
        out_specs=[pair_blk, pair_blk, pair_blk, pl.BlockSpec((1, 8, t), lambda h: (h, 0, 0)),
                   pl.BlockSpec((1, HEAD_DIM), fixed), pl.BlockSpec((1, HEAD_DIM), fixed)],
        out_shape=[_sds((t, hw), BF16), _sds((t, hw), BF16), _sds((t, hw), BF16), _sds((npair, 8, t), F32),
                   _sds((1, HEAD_DIM), F32), _sds((1, HEAD_DIM), F32)],
        scratch_shapes=[pltpu.VMEM((t, HEAD_DIM), F32), pltpu.VMEM((t, HEAD_DIM), F32), pltpu.VMEM((8, t), F32)],
        compiler_params=_params(1),
    )(proj, proj, proj, c, crow, gq, gk, lse, do)


def _t5_bucket_table():
    dist = np.arange(WINDOW)[:, None] + WINDOW - np.arange(2 * WINDOW)[None, :]
    n = np.maximum(dist, 0)
    max_exact = N_BUCKETS // 2
    large = max_exact + (np.log(np.maximum(n, 1) / max_exact) / np.log(REL_MAX_DIST / max_exact)
                         * (N_BUCKETS - max_exact)).astype(np.int32)
    large = np.minimum(large, N_BUCKETS - 1)
    return np.where(n < max_exact, n, large).astype(np.int32).reshape(1, -1)


def _bias_expand(name, rel_bias_t):
    n_heads = rel_bias_t.shape[0]
    tbl = jnp.asarray(_t5_bucket_table())
    width = tbl.shape[1]

    def body(rb_ref, tbl_ref, o_ref):
        onehot = (lax.broadcasted_iota(jnp.int32, (N_BUCKETS, width), 0) == tbl_ref[...]).astype(F32)
        o_ref[...] = _dot_exact(rb_ref[...], onehot)

    return pl.pallas_call(body, name=name, out_shape=_sds((n_heads, width), F32), compiler_params=_params(0))(rel_bias_t, tbl)


def _bias_reduce(name, dbias):
    n_heads, width = dbias.shape
    tbl = jnp.asarray(_t5_bucket_table())

    def body(db_ref, tbl_ref, o_ref):
        onehot = (lax.broadcasted_iota(jnp.int32, (N_BUCKETS, width), 0) == tbl_ref[...]).astype(F32)
        o_ref[...] = lax.dot_general(db_ref[...], onehot, (((1,), (1,)), ((), ())), preferred_element_type=F32,
                                     precision=lax.Precision.HIGHEST)

    return pl.pallas_call(body, name=name, out_shape=_sds((n_heads, N_BUCKETS), F32), compiler_params=_params(0))(dbias, tbl)


def _swa_mask(n, rows):
    i = lax.broadcasted_iota(jnp.int32, (rows, 2 * WINDOW), 0) & (WINDOW - 1)
    j = lax.broadcasted_iota(jnp.int32, (rows, 2 * WINDOW), 1)
    ok = (j > i) & (j <= i + WINDOW) & ((n > 0) | (j >= WINDOW))
    return jnp.where(ok, 0.0, NEG)


def _swa_fwd(name, qb, kh, vh, gq, gk, sink_rows, bias, group):
    t = qb.shape[0]
    kvh = kh.shape[0]
    nblk = t // WINDOW
    rows = group * WINDOW
    gw = group * HEAD_DIM

    def body(q_ref, k_ref, v_ref, gq_ref, gk_ref, sink_ref, bias_ref, o_ref, lse_ref, qs, kpad, vpad):
        for g in range(group):
            qs[:, g * HEAD_DIM:(g + 1) * HEAD_DIM] = _qhead(q_ref[:, g * HEAD_DIM:(g + 1) * HEAD_DIM], gq_ref[...]).astype(BF16)
        kpad[0:WINDOW, :] = jnp.zeros((WINDOW, HEAD_DIM), BF16)
        vpad[0:WINDOW, :] = jnp.zeros((WINDOW, HEAD_DIM), BF16)
        kpad[WINDOW:, :] = _rms(k_ref[0], gk_ref[...]).astype(BF16)
        vpad[WINDOW:, :] = v_ref[0].astype(BF16)
        sink = sink_ref[0]
        lane = lax.broadcasted_iota(jnp.int32, (WINDOW, LANES), 1)

        def block(n, carry):
            start = pl.multiple_of(n * WINDOW, WINDOW)
            kb = kpad[pl.ds(start, 2 * WINDOW), :]
            vb = vpad[pl.ds(start, 2 * WINDOW), :]
            mask = _swa_mask(n, WINDOW)
            lse_tile = jnp.zeros((WINDOW, LANES), F32)
            for g in range(group):
                head = slice(g * WINDOW, (g + 1) * WINDOW)
                cols = slice(g * HEAD_DIM, (g + 1) * HEAD_DIM)
                s = _dot_nt(qs[pl.ds(start, WINDOW), cols], kb) + bias_ref[0, head, :] + mask
                m = jnp.maximum(jnp.max(s, axis=1, keepdims=True), sink[head])
                e = jnp.exp(s - m)
                l = jnp.sum(e, axis=1, keepdims=True) + jnp.exp(sink[head] - m)
                o_ref[pl.ds(start, WINDOW), cols] = (_dot(e.astype(BF16), vb) / l).astype(BF16)
                lse_tile = jnp.where(lane == g, m + jnp.log(l), lse_tile)
            lse_ref[pl.ds(start, WINDOW), :] = lse_tile
            return carry

        lax.fori_loop(0, nblk, block, 0)

    fixed = lambda h: (0, 0)
    per = lambda h: (h, 0, 0)
    return pl.pallas_call(
        body, name=name, grid=(kvh,),
        in_specs=[pl.BlockSpec((t, gw), lambda h: (0, h)), pl.BlockSpec((1, t, HEAD_DIM), per), pl.BlockSpec((1, t, HEAD_DIM), per),
                  pl.BlockSpec((1, HEAD_DIM), fixed), pl.BlockSpec((1, HEAD_DIM), fixed),
                  pl.BlockSpec((1, rows, 1), per), pl.BlockSpec((1, rows, 2 * WINDOW), per)],
        out_specs=[pl.BlockSpec((t, gw), lambda h: (0, h)), pl.BlockSpec((t, LANES), lambda h: (0, h))],
        out_shape=[_sds((t, kvh * gw), BF16), _sds((t, kvh * LANES), F32)],
        scratch_shapes=[pltpu.VMEM((t, gw), BF16), pltpu.VMEM((t + WINDOW, HEAD_DIM), BF16),
                        pltpu.VMEM((t + WINDOW, HEAD_DIM), BF16)],
        compiler_params=_params(1),
    )(qb, kh, vh, gq, gk, sink_rows, bias)


def _swa_bwd(name, qb, kh, vh, gq, gk, sink_rows, bias, lse, do, group):
    t = qb.shape[0]
    kvh = kh.shape[0]
    nblk = t // WINDOW
    rows = group * WINDOW
    gw = group * HEAD_DIM

    def body(q_ref, k_ref, v_ref, gq_ref, gk_ref, sink_ref, bias_ref, lse_ref, do_ref,
             dq_ref, dk_ref, dv_ref, dgq_ref, dgk_ref, dsink_ref, dbias_ref,
             qs, kpad, vpad, dqs, dk_acc, dv_acc, dsink_acc):
        @pl.when(pl.program_id(0) == 0)
        def _():
            dgq_ref[...] = jnp.zeros_like(dgq_ref)
            dgk_ref[...] = jnp.zeros_like(dgk_ref)

        for g in range(group):
            qs[:, g * HEAD_DIM:(g + 1) * HEAD_DIM] = _qhead(q_ref[:, g * HEAD_DIM:(g + 1) * HEAD_DIM], gq_ref[...]).astype(BF16)
        kpad[0:WINDOW, :] = jnp.zeros((WINDOW, HEAD_DIM), BF16)
        vpad[0:WINDOW, :] = jnp.zeros((WINDOW, HEAD_DIM), BF16)
        kpad[WINDOW:, :] = _rms(k_ref[0], gk_ref[...]).astype(BF16)
        vpad[WINDOW:, :] = v_ref[0].astype(BF16)
        dk_acc[...] = jnp.zeros_like(dk_acc)
        dv_acc[...] = jnp.zeros_like(dv_acc)
        dsink_acc[...] = jnp.zeros_like(dsink_acc)
        dbias_ref[...] = jnp.zeros_like(dbias_ref)
        sink = sink_ref[0]

        def block(n, carry):
            start = pl.multiple_of(n * WINDOW, WINDOW)
            lse_tile = lse_ref[pl.ds(start, WINDOW), :]
            kb = kpad[pl.ds(start, 2 * WINDOW), :]
            vb = vpad[pl.ds(start, 2 * WINDOW), :]
            mask = _swa_mask(n, WINDOW)
            dk_blk = jnp.zeros((2 * WINDOW, HEAD_DIM), F32)
            dv_blk = jnp.zeros((2 * WINDOW, HEAD_DIM), F32)
            for g in range(group):
                head = slice(g * WINDOW, (g + 1) * WINDOW)
                cols = slice(g * HEAD_DIM, (g + 1) * HEAD_DIM)
                q = qs[pl.ds(start, WINDOW), cols]
                dob = do_ref[pl.ds(start, WINDOW), cols]
                lse_g = lse_tile[:, g:g + 1]
                s = _dot_nt(q, kb) + bias_ref[0, head, :] + mask
                p = jnp.exp(s - lse_g)
                dp = _dot_nt(dob, vb)
                dsum = jnp.sum(p * dp, axis=1, keepdims=True)
                ds = p * (dp - dsum)
                dsb = ds.astype(BF16)
                dsink_acc[head, :] -= jnp.exp(sink[head] - lse_g) * dsum
                dbias_ref[0, head, :] += ds
                dqs[pl.ds(start, WINDOW), cols] = _dot(dsb, kb)
                dk_blk += _dot_tn(dsb, q)
                dv_blk += _dot_tn(p.astype(BF16), dob)
            dk_acc[pl.ds(start, 2 * WINDOW), :] += dk_blk
            dv_acc[pl.ds(start, 2 * WINDOW), :] += dv_blk
            return carry

        lax.fori_loop(0, nblk, block, 0)
        for g in range(group):
            _, q_vjp = jax.vjp(_qhead, q_ref[:, g * HEAD_DIM:(g + 1) * HEAD_DIM], gq_ref[...])
            dqp, dgq = q_vjp(dqs[:, g * HEAD_DIM:(g + 1) * HEAD_DIM])
            dq_ref[:, g * HEAD_DIM:(g + 1) * HEAD_DIM] = dqp.astype(BF16)
            dgq_ref[...] += dgq
            dsink_g = jnp.sum(dsink_acc[g * WINDOW:(g + 1) * WINDOW, :], axis=0, keepdims=True)
            dsink_ref[0, g:g + 1, :] = jnp.broadcast_to(dsink_g, (1, LANES))
        _, k_vjp = jax.vjp(_rms, k_ref[0], gk_ref[...])
        dkp, dgk = k_vjp(dk_acc[WINDOW:, :])
        dk_ref[0] = dkp
        dgk_ref[...] += dgk
        dv_ref[0] = dv_acc[WINDOW:, :]

    fixed = lambda h: (0, 0)
    per = lambda h: (h, 0, 0)
    wide = pl.BlockSpec((t, gw), lambda h: (0, h))
    head = pl.BlockSpec((1, t, HEAD_DIM), per)
    vec = pl.BlockSpec((1, HEAD_DIM), fixed)
    return pl.pallas_call(
        body, name=name, grid=(kvh,),
        in_specs=[wide, head, head, vec, vec, pl.BlockSpec((1, rows, 1), per), pl.BlockSpec((1, rows, 2 * WINDOW), per),
                  pl.BlockSpec((t, LANES), lambda h: (0, h)), wide],
        out_specs=[wide, head, head, vec, vec, pl.BlockSpec((1, group, LANES), per), pl.BlockSpec((1, rows, 2 * WINDOW), per)],
        out_shape=[_sds((t, kvh * gw), BF16), _sds((kvh, t, HEAD_DIM), F32), _sds((kvh, t, HEAD_DIM), F32),
                   _sds((1, HEAD_DIM), F32), _sds((1, HEAD_DIM), F32),
                   _sds((kvh, group, LANES), F32), _sds((kvh, rows, 2 * WINDOW), F32)],
        scratch_shapes=[pltpu.VMEM((t, gw), BF16), pltpu.VMEM((t + WINDOW, HEAD_DIM), BF16),
                        pltpu.VMEM((t + WINDOW, HEAD_DIM), BF16), pltpu.VMEM((t, gw), F32),
                        pltpu.VMEM((t + WINDOW, HEAD_DIM), F32), pltpu.VMEM((t + WINDOW, HEAD_DIM), F32),
                        pltpu.VMEM((rows, 1), F32)],
        compiler_params=_params(1),
    )(qb, kh, vh, gq, gk, sink_rows, bias, lse, do)


def _local_step(x, target, p, comm):
    t, d = x.shape
    n_heads = d // HEAD_DIM
    kv_heads = n_heads // 8
    group = n_heads // kv_heads
    hw = n_heads * HEAD_DIM
    gate_col = 3 * hw // LANES
    kvw = kv_heads * HEAD_DIM
    grads = {}

    def mlp_fwd(tag, h, g, layer):
        w_up, = comm.weights([f"w_up{layer}"], h)
        u, hn, a = _norm_matmul(f"{tag}_up", h, g, w_up, tn=1024, relu2=True)
        w_down, = comm.weights([f"w_down{layer}"], a)
        return _matmul_res(f"{tag}_down", a, w_down, h), (h, g, u, hn, a, w_up, w_down)

    def mlp_bwd(tag, saved, layer, dy):
        h, g, u, hn, a, w_up, w_down = saved
        du = _matmul_nt(f"{tag}_du", dy, w_down, u=u)
        dw_down = _matmul_tn(f"{tag}_dwdown", a, dy)
        dw_up = _matmul_tn(f"{tag}_dwup", hn, du)
        zero = comm.send_grads(tag, {f"w_down{layer}": dw_down, f"w_up{layer}": dw_up})
        return _matmul_nt_rmsbwd(f"{tag}_dh", du, w_up, h, g + zero, dy)

    w_in, = comm.weights(["w_in_a"], None)
    proj, xn1 = _norm_matmul("a_inproj", x, p["g_attn"][0], w_in, tn=640)
    b_pad = jnp.pad(p["b_f"], ((0, 0), (0, LANES - n_heads)))
    c = _gate_fwd("a_gate", proj, b_pad, n_heads, gate_col)
    crow = c[:, :n_heads].T.reshape(n_heads // 2, 2, t)
    o_a, lse_a = _fox_fwd("a_attn", proj, c, crow, p["gq_a"], p["gk_a"], n_heads)
    w_out_a, = comm.weights(["w_out_a"], o_a)
    h1 = _matmul_res("a_outproj", o_a, w_out_a, x)
    h2, mlp0 = mlp_fwd("mlp0", h1, p["g_mlp"][0], 0)

    w_kv, w_q_b = comm.weights(["w_kv", "w_q_b"], h2)
    kv, hn_kv = _norm_matmul("kv_proj", h2, p["g_kv"], w_kv, tn=2 * kvw)
    kh = kv[:, :kvw].reshape(t, kv_heads, HEAD_DIM).transpose(1, 0, 2)
    vh = kv[:, kvw:].reshape(t, kv_heads, HEAD_DIM).transpose(1, 0, 2)
    qb, hn_q = _norm_matmul("b_qproj", h2, p["g_attn"][1], w_q_b, tn=512)
    gqb, gkb = p["gq_b"], p["gk_b"].reshape(1, HEAD_DIM)
    bias = _bias_expand("b_bias", p["rel_bias"].T).reshape(kv_heads, group * WINDOW, 2 * WINDOW)
    sink_rows = jnp.broadcast_to(p["sinks"].reshape(n_heads, 1), (n_heads, WINDOW)).reshape(kv_heads, group * WINDOW, 1)
    o_b, lse_b = _swa_fwd("b_attn", qb, kh, vh, gqb, gkb, sink_rows, bias, group)
    w_out_b, = comm.weights(["w_out_b"], o_b)
    h3 = _matmul_res("b_outproj", o_b, w_out_b, h2)
    y, mlp1 = mlp_fwd("mlp1", h3, p["g_mlp"][1], 1)
    dy, loss_tile = _loss_grad("loss", y, target)

    dh3, dg_mlp1 = mlp_bwd("mlp1", mlp1, 1, dy)
    do_b = _matmul_nt("b_do", dh3, w_out_b)
    dw_out_b = _matmul_tn("b_dwout", o_b, dh3)
    dqb, dkh, dvh, grads["gq_b"], dgk_b, dsink, dbias = _swa_bwd(
        "b_attn_bwd", qb, kh, vh, gqb, gkb, sink_rows, bias, lse_b, do_b, group)
    grads["gk_b"] = dgk_b.reshape(HEAD_DIM)
    grads["sinks"] = dsink[:, :, 0].reshape(1, n_heads)
    grads["rel_bias"] = _bias_reduce("b_dbias", dbias.reshape(n_heads, WINDOW * 2 * WINDOW)).T
    dw_q_b = _matmul_tn("b_dwq", hn_q, dqb)
    dh2, dg_attn1 = _matmul_nt_rmsbwd("b_dhq", dqb, w_q_b, h2, p["g_attn"][1], dh3)
    dkv = jnp.concatenate([dkh.transpose(1, 0, 2).reshape(t, kvw), dvh.transpose(1, 0, 2).reshape(t, kvw)], axis=1)
    dw_kv = _matmul_tn("kv_dw", hn_kv, dkv)
    zero = comm.send_grads("attn_b", {"w_out_b": dw_out_b, "w_q_b": dw_q_b, "w_kv": dw_kv})
    dh2, dg_kv = _matmul_nt_rmsbwd("kv_dh", dkv, w_kv, h2, p["g_kv"] + zero, dh2)
    grads["g_kv"] = dg_kv.reshape(d)
    dh1, dg_mlp0 = mlp_bwd("mlp0", mlp0, 0, dh2)
    grads["g_mlp"] = jnp.concatenate([dg_mlp0, dg_mlp1], axis=0)

    do_a = _matmul_nt("a_do", dh1, w_out_a)
    dw_out_a = _matmul_tn("a_dwout", o_a, dh1)
    zero = comm.send_grads("attn_a_out", {"w_out_a": dw_out_a})
    dq, dk, dv, dc_rows, grads["gq_a"], grads["gk_a"] = _fox_bwd(
        "a_attn_bwd", proj, c, crow, p["gq_a"] + zero, p["gk_a"], lse_a, do_a, n_heads)
    dc = jnp.pad(dc_rows[:, :2, :].reshape(n_heads, t).T, ((0, 0), (0, LANES - n_heads)))
    dfl, db_f = _gate_bwd("a_gate_bwd", proj, b_pad, dc, n_heads, gate_col)
    grads["b_f"] = db_f[:, :n_heads]
    dproj = jnp.concatenate([dq, dk, dv, dfl], axis=1)
    dw_in = _matmul_tn("a_dwin", xn1, dproj, tn=640)
    zero = comm.send_grads("attn_a_in", {"w_in_a": dw_in})
    grad_x, dg_attn0 = _matmul_nt_rmsbwd("a_dx", dproj, w_in, x, p["g_attn"][0] + zero, dh1)
    grads["g_attn"] = jnp.concatenate([dg_attn0, dg_attn1], axis=0)
    return loss_tile, grad_x, grads


def _exchange(name, arrays, scatter, after):
    n = len(arrays)

    def body(*refs):
        src, out = refs[:n], refs[n + 1:2 * n + 1]
        send_sems, recv_sems, local_sems = refs[2 * n + 1:]
        x, y, c = lax.axis_index("x"), lax.axis_index("y"), lax.axis_index("c")
        me = 4 * x + 2 * y + c
        peers = []
        for k in range(1, N_DEV):
            px, py, pc = x ^ ((k >> 2) & 1), y ^ ((k >> 1) & 1), c ^ (k & 1)
            peers.append(((px, py, pc), 4 * px + 2 * py + pc))
        local, sends = [], []
        for a in range(n):
            mine = src[a].at[me] if scatter else src[a]
            cp = pltpu.make_async_copy(mine, out[a].at[me], local_sems.at[a])
            cp.start()
            local.append(cp)
        for k, (peer, peer_pos) in enumerate(peers):
            for a in range(n):
                cp = pltpu.make_async_remote_copy(
                    src_ref=src[a].at[peer_pos] if scatter else src[a], dst_ref=out[a].at[me],
                    send_sem=send_sems.at[a, k], recv_sem=recv_sems.at[a, k],
                    device_id=peer, device_id_type=pl.DeviceIdType.MESH)
                cp.start()
                sends.append(cp)
        for k, (peer, peer_pos) in enumerate(peers):
            for a in range(n):
                pltpu.make_async_remote_copy(
                    src_ref=out[a].at[peer_pos], dst_ref=out[a].at[peer_pos],
                    send_sem=send_sems.at[a, k], recv_sem=recv_sems.at[a, k],
                    device_id=peer, device_id_type=pl.DeviceIdType.MESH).wait_recv()
        for cp in sends:
            cp.wait_send()
        for cp in local:
            cp.wait()

    any_spec = pl.BlockSpec(memory_space=pl.ANY)
    out_shape = [_sds(a.shape if scatter else (N_DEV,) + a.shape, a.dtype) for a in arrays]
    return pl.pallas_call(
        body, name=name, in_specs=[any_spec] * (n + 1), out_specs=[any_spec] * n, out_shape=out_shape,
        scratch_shapes=[pltpu.SemaphoreType.DMA((n, N_DEV - 1)), pltpu.SemaphoreType.DMA((n, N_DEV - 1)),
                        pltpu.SemaphoreType.DMA((n,))],
    )(*arrays, after)


class _InFlight:
    def __init__(self, scatter, send_sems, recv_sems, srcs, lands, token):
        self.scatter, self.send_sems, self.recv_sems = scatter, send_sems, recv_sems
        self.srcs, self.lands, self.token = list(srcs), list(lands), token


def _mesh_peers():
    x, y, c = lax.axis_index("x"), lax.axis_index("y"), lax.axis_index("c")
    peers = []
    for k in range(1, N_DEV):
        px, py, pc = x ^ ((k >> 2) & 1), y ^ ((k >> 1) & 1), c ^ (k & 1)
        peers.append(((px, py, pc), 4 * px + 2 * py + pc))
    return 4 * x + 2 * y + c, peers


_HBM_SPEC = pl.BlockSpec(memory_space=pltpu.HBM)
_SEM_SPEC = pl.BlockSpec(memory_space=pltpu.SEMAPHORE)
_SIDE_EFFECT = pltpu.SideEffectType.DATAFLOW_SIDE_EFFECTING


def _exchange_start(name, arrays, scatter):
    n = len(arrays)
    me, _ = _mesh_peers()
    lands = []
    for a in arrays:
        own = lax.dynamic_index_in_dim(a, me, 0, keepdims=False) if scatter else a
        shape = a.shape if scatter else (N_DEV,) + a.shape
        lands.append(lax.dynamic_update_index_in_dim(lax.empty(shape, a.dtype), own, me, 0))

    def body(*refs):
        src, land = refs[:n], refs[n:2 * n]
        send_sems, recv_sems, token = refs[2 * n], refs[2 * n + 1], refs[-1]
        pos, peers = _mesh_peers()
        for a in range(n):
            for k, (peer, peer_pos) in enumerate(peers):
                pltpu.make_async_remote_copy(
                    src_ref=src[a].at[peer_pos] if scatter else src[a], dst_ref=land[a].at[pos],
                    send_sem=send_sems.at[a * (N_DEV - 1) + k], recv_sem=recv_sems.at[a * (N_DEV - 1) + k],
                    device_id=peer, device_id_type=pl.DeviceIdType.MESH).start()
        token[...] = jnp.zeros_like(token)

    operands = [pltpu.with_memory_space_constraint(a, pltpu.HBM) for a in list(arrays) + lands]
    outs = pl.pallas_call(
        body, name=name,
        out_shape=(pltpu.SemaphoreType.DMA((n * (N_DEV - 1),)), pltpu.SemaphoreType.DMA((n * (N_DEV - 1),)),
                   *[pltpu.HBM(a.shape, a.dtype) for a in operands], _sds((8, LANES), F32)),
        in_specs=[_HBM_SPEC] * (2 * n),
        out_specs=(_SEM_SPEC, _SEM_SPEC, *[_HBM_SPEC] * (2 * n), pl.BlockSpec(memory_space=pltpu.VMEM)),
        input_output_aliases={i: 2 + i for i in range(2 * n)},
        compiler_params=pltpu.CompilerParams(has_side_effects=_SIDE_EFFECT),
    )(*operands)
    return _InFlight(scatter, outs[0], outs[1], outs[2:2 + n], outs[2 + n:2 + 2 * n], outs[-1])


def _exchange_wait(name, flight, which, after):
    m = len(which)
    scatter = flight.scatter

    def body(*refs):
        src, land = refs[:m], refs[m:2 * m]
        send_sems, recv_sems = refs[2 * m], refs[2 * m + 1]
        _, peers = _mesh_peers()
        for i, a in enumerate(which):
            for k, (peer, peer_pos) in enumerate(peers):
                cp = pltpu.make_async_remote_copy(
                    src_ref=src[i].at[peer_pos] if scatter else src[i], dst_ref=land[i].at[peer_pos],
                    send_sem=send_sems.at[a * (N_DEV - 1) + k], recv_sem=recv_sems.at[a * (N_DEV - 1) + k],
                    device_id=peer, device_id_type=pl.DeviceIdType.MESH)
                cp.wait_send()
                cp.wait_recv()

    operands = [flight.srcs[a] for a in which] + [flight.lands[a] for a in which]
    outs = pl.pallas_call(
        body, name=name, out_shape=tuple(pltpu.HBM(a.shape, a.dtype) for a in operands),
        in_specs=[_HBM_SPEC] * (2 * m) + [_SEM_SPEC, _SEM_SPEC, pl.BlockSpec(memory_space=pl.ANY)],
        out_specs=tuple([_HBM_SPEC] * (2 * m)), input_output_aliases={i: i for i in range(2 * m)},
        compiler_params=pltpu.CompilerParams(has_side_effects=_SIDE_EFFECT),
    )(*operands, flight.send_sems, flight.recv_sems, after)
    return list(outs[m:])


def _adamw(name, parts, w, m, v):
    r, c = w.shape
    tr = r if r <= 256 else 256

    def body(p_ref, w_ref, m_ref, v_ref, g_ref, d_ref, mo_ref, vo_ref):
        g = p_ref[0].astype(F32)
        for dev in range(1, N_DEV):
            g = g + p_ref[dev].astype(F32)
        m_new = ADAM_B1 * m_ref[...] + (1.0 - ADAM_B1) * g
        v_new = ADAM_B2 * v_ref[...] + (1.0 - ADAM_B2) * jnp.square(g)
        m_hat = m_new / (1.0 - ADAM_B1 ** ADAM_STEP)
        v_hat = v_new / (1.0 - ADAM_B2 ** ADAM_STEP)
        g_ref[...] = g
        d_ref[...] = -ADAM_LR * (m_hat / (jnp.sqrt(v_hat) + ADAM_EPS) + ADAM_WD * w_ref[...])
        mo_ref[...] = m_new
        vo_ref[...] = v_new

    blk = pl.BlockSpec((tr, c), lambda i: (i, 0))
    return pl.pallas_call(
        body, name=name, grid=(r // tr,),
        in_specs=[pl.BlockSpec((N_DEV, tr, c), lambda i: (0, i, 0)), blk, blk, blk],
        out_specs=[blk] * 4, out_shape=[_sds((r, c), F32)] * 4, compiler_params=_params(1),
    )(parts, w, m, v)


def _pack_small(tree):
    flat = jnp.concatenate([tree[k].reshape(-1) for k in SMALL])
    size = -(-flat.shape[0] // (8 * LANES)) * (8 * LANES)
    return jnp.pad(flat, (0, size - flat.shape[0])).reshape(-1, LANES)


def _unpack_small(packed, like):
    flat, out, off = packed.reshape(-1), {}, 0
    for k in SMALL:
        size = like[k].size
        out[k] = flat[off:off + size].reshape(like[k].shape)
        off += size
    return out


class _Comm:
    ORDER = ("w_in_a", "w_out_a", "w_up0", "w_down0", "w_kv", "w_q_b", "w_out_b", "w_up1", "w_down1")

    def __init__(self, shards, d, dff, n_in):
        self.d, self.dff, self.n_in = d, dff, n_in
        self.flight = _exchange_start("gather_start", [shards[n].astype(BF16) for n in self.ORDER], scatter=False)
        self.sent = []

    def weights(self, names, after):
        which = [self.ORDER.index(n) for n in names]
        landed = _exchange_wait(f"gather_wait_{names[0]}", self.flight, which, self.flight.token if after is None else after)
        return [self._whole(n, g) for n, g in zip(names, landed)]

    def _whole(self, name, g):
        if name == "w_in_a":
            pad = -(-self.n_in // LANES) * LANES - self.n_in
            return jnp.pad(g.transpose(1, 0, 2).reshape(self.d, self.n_in), ((0, 0), (0, pad)))
        if name.startswith("w_up"):
            return g.transpose(1, 0, 2).reshape(self.d, self.dff)
        return g.reshape(-1, g.shape[-1])

    def _chunks(self, name, g):
        if name == "w_in_a":
            return g[:, :self.n_in].reshape(self.d, N_DEV, -1).transpose(1, 0, 2)
        if name.startswith("w_up"):
            return g.reshape(self.d, N_DEV, -1).transpose(1, 0, 2)
        return g.reshape(N_DEV, g.shape[0] // N_DEV, g.shape[1])

    def send_grads(self, tag, partials):
        names = list(partials)
        flight = _exchange_start(f"scatter_start_{tag}", [self._chunks(n, partials[n]) for n in names], scatter=True)
        self.sent.append((tag, flight, names))
        return flight.token[0, 0]

    def received(self, index, after):
        tag, flight, names = self.sent[index]
        landed = _exchange_wait(f"scatter_wait_{tag}", flight, list(range(len(names))), after)
        return dict(zip(names, landed))


def kernel(x, g_attn, g_mlp, w_in_a, b_f, gq_a, gk_a, w_out_a, g_kv, w_kv, gk_b, w_q_b, gq_b, sinks, rel_bias, w_out_b, w_up, w_down, loss_target, m_g_attn, m_g_mlp, m_w_in_a, m_b_f, m_gq_a, m_gk_a, m_w_out_a, m_g_kv, m_w_kv, m_gk_b, m_w_q_b, m_gq_b, m_sinks, m_rel_bias, m_w_out_b, m_w_up, m_w_down, v_g_attn, v_g_mlp, v_w_in_a, v_b_f, v_gq_a, v_gk_a, v_w_out_a, v_g_kv, v_w_kv, v_gk_b, v_w_q_b, v_gq_b, v_sinks, v_rel_bias, v_w_out_b, v_w_up, v_w_down):
    w = dict(g_attn=g_attn, g_mlp=g_mlp, w_in_a=w_in_a, b_f=b_f, gq_a=gq_a, gk_a=gk_a, w_out_a=w_out_a, g_kv=g_kv,
             w_kv=w_kv, gk_b=gk_b, w_q_b=w_q_b, gq_b=gq_b, sinks=sinks, rel_bias=rel_bias, w_out_b=w_out_b,
             w_up=w_up, w_down=w_down)
    mom = dict(g_attn=m_g_attn, g_mlp=m_g_mlp, w_in_a=m_w_in_a, b_f=m_b_f, gq_a=m_gq_a, gk_a=m_gk_a, w_out_a=m_w_out_a,
               g_kv=m_g_kv, w_kv=m_w_kv, gk_b=m_gk_b, w_q_b=m_w_q_b, gq_b=m_gq_b, sinks=m_sinks, rel_bias=m_rel_bias,
               w_out_b=m_w_out_b, w_up=m_w_up, w_down=m_w_down)
    var = dict(g_attn=v_g_attn, g_mlp=v_g_mlp, w_in_a=v_w_in_a, b_f=v_b_f, gq_a=v_gq_a, gk_a=v_gk_a, w_out_a=v_w_out_a,
               g_kv=v_g_kv, w_kv=v_w_kv, gk_b=v_gk_b, w_q_b=v_w_q_b, gq_b=v_gq_b, sinks=v_sinks, rel_bias=v_rel_bias,
               w_out_b=v_w_out_b, w_up=v_w_up, w_down=v_w_down)
    d = x.shape[2]

    def shard(tree):
        return {"w_in_a": tree["w_in_a"][0], "w_out_a": tree["w_out_a"][0], "w_kv": tree["w_kv"], "w_q_b": tree["w_q_b"][0],
                "w_out_b": tree["w_out_b"][0], "w_up0": tree["w_up"][0], "w_up1": tree["w_up"][1],
                "w_down0": tree["w_down"][0], "w_down1": tree["w_down"][1]}

    w_sh, m_sh, v_sh = shard(w), shard(mom), shard(var)
    comm = _Comm(w_sh, d, w_up.shape[2] * N_DEV, w_in_a.shape[2] * N_DEV)
    loss_tile, grad_x, grads = _local_step(x[0], loss_target[0], {k: w[k] for k in SMALL}, comm)
    loss = lax.psum(loss_tile[0, 0], ("x", "y", "c"))

    big, after = {}, grad_x
    last = len(comm.sent) - 1
    for index in range(last + 1):
        if index == last:
            small_parts, = _exchange("gather_small_grads", [_pack_small(grads)], scatter=False, after=after)
            small = _adamw("adam_small", small_parts, _pack_small(w), _pack_small(mom), _pack_small(var))
            after = small[0]
        for n, parts in comm.received(index, after).items():
            big[n] = _adamw(f"adam_{n}", parts, w_sh[n], m_sh[n], v_sh[n])
            after = big[n][0]
    res = {}
    for n in ("w_in_a", "w_out_a", "w_q_b", "w_out_b"):
        res[n] = [a[None] for a in big[n]]
    res["w_kv"] = big["w_kv"]
    res["w_up"] = [jnp.stack([big["w_up0"][i], big["w_up1"][i]]) for i in range(4)]
    res["w_down"] = [jnp.stack([big["w_down0"][i], big["w_down1"][i]]) for i in range(4)]
    small = [_unpack_small(s, w) for s in small]
    for k in SMALL:
        res[k] = [s[k] for s in small]

    outs = [loss, grad_x[None]]
    for i in range(4):
        outs.extend(res[k][i] for k in WEIGHTS)
    return tuple(outs)
```

```python
import functools

import numpy as np
import jax
import jax.numpy as jnp
from jax import lax
from jax.experimental import pallas as pl
from jax.experimental.pallas import tpu as pltpu

F32 = jnp.float32
BF16 = jnp.bfloat16

N_DEV = 8
HEAD_DIM = 64
WINDOW = 128
N_BUCKETS = 32
REL_MAX_DIST = 128
NORM_EPS = 1e-6
NEG = -1e30
LANES = 128
VMEM_LIMIT = 56 * 1024 * 1024

ADAM_LR = 0.001
ADAM_B1 = 0.9
ADAM_B2 = 0.999
ADAM_EPS = 1e-08
ADAM_WD = 0.01
ADAM_STEP = 10

SMALL = ("g_attn", "g_mlp", "b_f", "gq_a", "gk_a", "g_kv", "gk_b", "gq_b", "sinks", "rel_bias")
BIG = ("w_in_a", "w_out_a", "w_kv", "w_q_b", "w_out_b", "w_up", "w_down")
WEIGHTS = ("g_attn", "g_mlp", "w_in_a", "b_f", "gq_a", "gk_a", "w_out_a", "g_kv", "w_kv", "gk_b",
           "w_q_b", "gq_b", "sinks", "rel_bias", "w_out_b", "w_up", "w_down")


def _params(n_grid):
    return pltpu.CompilerParams(dimension_semantics=("arbitrary",) * n_grid, vmem_limit_bytes=VMEM_LIMIT)


def _sds(shape, dtype):
    return jax.ShapeDtypeStruct(tuple(shape), dtype)


def _rms(x, g):
    return (x * lax.rsqrt(jnp.mean(x * x, axis=-1, keepdims=True) + NORM_EPS)) * g


def _dot_nt(a, b):
    return lax.dot_general(a, b, (((1,), (1,)), ((), ())), preferred_element_type=F32)


def _dot_tn(a, b):
    return lax.dot_general(a, b, (((0,), (0,)), ((), ())), preferred_element_type=F32)


def _dot(a, b):
    return jnp.dot(a, b, preferred_element_type=F32)


def _dot_exact(a, b):
    return jnp.dot(a, b, preferred_element_type=F32, precision=lax.Precision.HIGHEST)


def _norm_matmul(name, x, g, w, *, tn=None, relu2=False):
    t, d = x.shape
    blocked = w.ndim == 3
    if blocked:
        tn = w.shape[2]
        n = w.shape[0] * tn
        w_spec = pl.BlockSpec((None, d, tn), lambda i, j: (j, 0, 0))
    else:
        n = w.shape[1]
        w_spec = pl.BlockSpec((d, tn), lambda i, j: (0, j))
    tm = min(1024, t)

    def body(x_ref, g_ref, w_ref, y_ref, xn_ref, *a_ref):
        @pl.when(pl.program_id(1) == 0)
        def _():
            xn_ref[...] = _rms(x_ref[...], g_ref[...]).astype(BF16)

        y = _dot(xn_ref[...], w_ref[...])
        y_ref[...] = y
        if relu2:
            a_ref[0][...] = jnp.square(jnp.maximum(y, 0.0)).astype(BF16)

    out_shape = [_sds((t, n), F32), _sds((t, d), BF16)]
    out_specs = [pl.BlockSpec((tm, tn), lambda i, j: (i, j)), pl.BlockSpec((tm, d), lambda i, j: (i, 0))]
    if relu2:
        out_shape.append(_sds((t, n), BF16))
        out_specs.append(pl.BlockSpec((tm, tn), lambda i, j: (i, j)))
    return pl.pallas_call(
        body, name=name, grid=(t // tm, n // tn),
        in_specs=[pl.BlockSpec((tm, d), lambda i, j: (i, 0)), pl.BlockSpec((1, d), lambda i, j: (0, 0)), w_spec],
        out_specs=out_specs, out_shape=out_shape, compiler_params=_params(2),
    )(x, g.reshape(1, d), w)


def _matmul_res(name, a, w, res, *, tn=512):
    t, k = a.shape
    n = w.shape[1]
    tm = min(1024, t)

    def body(a_ref, w_ref, r_ref, o_ref):
        o_ref[...] = r_ref[...] + _dot(a_ref[...], w_ref[...])

    return pl.pallas_call(
        body, name=name, grid=(t // tm, n // tn),
        in_specs=[pl.BlockSpec((tm, k), lambda i, j: (i, 0)), pl.BlockSpec((k, tn), lambda i, j: (0, j)),
                  pl.BlockSpec((tm, tn), lambda i, j: (i, j))],
        out_specs=pl.BlockSpec((tm, tn), lambda i, j: (i, j)), out_shape=_sds((t, n), F32),
        compiler_params=_params(2),
    )(a, w, res)


def _matmul_nt(name, dy, w, *, u=None, tk=1024):
    t, n = dy.shape
    k = w.shape[0]
    tm = min(1024, t)

    def body(dy_ref, w_ref, *rest):
        o_ref = rest[-1]
        r = _dot_nt(dy_ref[...].astype(BF16), w_ref[...])
        if u is not None:
            r = r * (2.0 * jnp.maximum(rest[0][...], 0.0))
        o_ref[...] = r.astype(BF16)

    in_specs = [pl.BlockSpec((tm, n), lambda i, j: (i, 0)), pl.BlockSpec((tk, n), lambda i, j: (j, 0))]
    args = [dy, w]
    if u is not None:
        in_specs.append(pl.BlockSpec((tm, tk), lambda i, j: (i, j)))
        args.append(u)
    return pl.pallas_call(
        body, name=name, grid=(t // tm, k // tk), in_specs=in_specs,
        out_specs=pl.BlockSpec((tm, tk), lambda i, j: (i, j)), out_shape=_sds((t, k), BF16),
        compiler_params=_params(2),
    )(*args)


def _matmul_nt_rmsbwd(name, dy, w, x, g, dres):
    t, k = dy.shape
    blocked = w.ndim == 3
    d = w.shape[1] if blocked else w.shape[0]
    tm = min(512, t)

    def body(dy_ref, w_ref, x_ref, g_ref, r_ref, dx_ref, dg_ref):
        if blocked:
            kb = w.shape[2]
            dxn = _dot_nt(dy_ref[:, 0:kb].astype(BF16), w_ref[0])
            for j in range(1, w.shape[0]):
                dxn += _dot_nt(dy_ref[:, j * kb:(j + 1) * kb].astype(BF16), w_ref[j])
        else:
            dxn = _dot_nt(dy_ref[...].astype(BF16), w_ref[...])
        _, vjp = jax.vjp(_rms, x_ref[...], g_ref[...])
        dx, dg = vjp(dxn)
        dx_ref[...] = r_ref[...] + dx

        @pl.when(pl.program_id(0) == 0)
        def _():
            dg_ref[...] = jnp.zeros_like(dg_ref)

        dg_ref[...] += dg

    row = lambda i: (i, 0)
    fixed = lambda i: (0, 0)
    return pl.pallas_call(
        body, name=name, grid=(t // tm,),
        in_specs=[pl.BlockSpec((tm, k), row), pl.BlockSpec(w.shape, (lambda i: (0, 0, 0)) if blocked else fixed),
                  pl.BlockSpec((tm, d), row), pl.BlockSpec((1, d), fixed), pl.BlockSpec((tm, d), row)],
        out_specs=[pl.BlockSpec((tm, d), row), pl.BlockSpec((1, d), fixed)],
        out_shape=[_sds((t, d), F32), _sds((1, d), F32)], compiler_params=_params(1),
    )(dy, w, x, g.reshape(1, d), dres)


def _matmul_tn(name, a, b, *, tk=1024, tn=1024, col_blocks=None):
    t, k = a.shape
    n = b.shape[1]
    tk = min(tk, k)
    if col_blocks:
        tn = n // col_blocks
        out_spec, out_shape = pl.BlockSpec((None, tk, tn), lambda i, j: (j, i, 0)), _sds((col_blocks, k, tn), BF16)
    else:
        tn = min(tn, n)
        out_spec, out_shape = pl.BlockSpec((tk, tn), lambda i, j: (i, j)), _sds((k, n), BF16)

    def body(a_ref, b_ref, o_ref):
        o_ref[...] = _dot_tn(a_ref[...].astype(BF16), b_ref[...].astype(BF16)).astype(BF16)

    return pl.pallas_call(
        body, name=name, grid=(k // tk, n // tn),
        in_specs=[pl.BlockSpec((t, tk), lambda i, j: (0, i)), pl.BlockSpec((t, tn), lambda i, j: (0, j))],
        out_specs=out_spec, out_shape=out_shape, compiler_params=_params(2),
    )(a, b)


def _loss_grad(name, y, target):
    t, d = y.shape
    tm = min(512, t)

    def body(y_ref, t_ref, dy_ref, l_ref):
        e = y_ref[...] - t_ref[...]
        dy_ref[...] = e * (1.0 / d)

        @pl.when(pl.program_id(0) == 0)
        def _():
            l_ref[...] = jnp.zeros_like(l_ref)

        l_ref[...] += 0.5 * jnp.sum(jnp.mean(e * e, axis=-1))

    row = lambda i: (i, 0)
    return pl.pallas_call(
        body, name=name, grid=(t // tm,), in_specs=[pl.BlockSpec((tm, d), row), pl.BlockSpec((tm, d), row)],
        out_specs=[pl.BlockSpec((tm, d), row), pl.BlockSpec((8, LANES), lambda i: (0, 0))],
        out_shape=[_sds((t, d), F32), _sds((8, LANES), F32)], compiler_params=_params(1),
    )(y, target)


def _gate_fwd(name, proj, b_pad, n_heads, gate_col):
    t = proj.shape[0]
    tb = min(256, t)
    tri = jnp.asarray(np.tril(np.ones((tb, tb), np.float32)))

    def body(p_ref, b_ref, tri_ref, c_ref, carry):
        @pl.when(pl.program_id(0) == 0)
        def _():
            carry[...] = jnp.zeros_like(carry)

        lane = lax.broadcasted_iota(jnp.int32, (tb, LANES), 1)
        lf = jnp.where(lane < n_heads, jax.nn.log_sigmoid(p_ref[...] + b_ref[...]), 0.0)
        c = _dot_exact(tri_ref[...], lf) + carry[0:1, :]
        c_ref[...] = c
        carry[...] = jnp.broadcast_to(c[tb - 1:tb, :], carry.shape)

    return pl.pallas_call(
        body, name=name, grid=(t // tb,),
        in_specs=[pl.BlockSpec((tb, LANES), lambda i: (i, gate_col)), pl.BlockSpec((1, LANES), lambda i: (0, 0)),
                  pl.BlockSpec((tb, tb), lambda i: (0, 0))],
        out_specs=pl.BlockSpec((tb, LANES), lambda i: (i, 0)), out_shape=_sds((t, LANES), F32),
        scratch_shapes=[pltpu.VMEM((8, LANES), F32)], compiler_params=_params(1),
    )(proj, b_pad, tri)


def _gate_bwd(name, proj, b_pad, dc, n_heads, gate_col):
    t = proj.shape[0]
    tb = min(256, t)
    nb = t // tb
    triu = jnp.asarray(np.triu(np.ones((tb, tb), np.float32)))

    def body(p_ref, b_ref, dc_ref, tri_ref, df_ref, db_ref, carry):
        @pl.when(pl.program_id(0) == 0)
        def _():
            carry[...] = jnp.zeros_like(carry)
            db_ref[...] = jnp.zeros_like(db_ref)

        dcv = dc_ref[...]
        dlf = _dot_exact(tri_ref[...], dcv) + carry[0:1, :]
        carry[...] = jnp.broadcast_to(dlf[0:1, :], carry.shape)
        lane = lax.broadcasted_iota(jnp.int32, (tb, LANES), 1)
        z = p_ref[...] + b_ref[...]
        df = jnp.where(lane < n_heads, dlf / (1.0 + jnp.exp(z)), 0.0)
        df_ref[...] = df.astype(BF16)
        db_ref[...] += jnp.sum(df, axis=0, keepdims=True)

    return pl.pallas_call(
        body, name=name, grid=(nb,),
        in_specs=[pl.BlockSpec((tb, LANES), lambda i: (nb - 1 - i, gate_col)), pl.BlockSpec((1, LANES), lambda i: (0, 0)),
                  pl.BlockSpec((tb, LANES), lambda i: (nb - 1 - i, 0)), pl.BlockSpec((tb, tb), lambda i: (0, 0))],
        out_specs=[pl.BlockSpec((tb, LANES), lambda i: (nb - 1 - i, 0)), pl.BlockSpec((1, LANES), lambda i: (0, 0))],
        out_shape=[_sds((t, LANES), BF16), _sds((1, LANES), F32)],
        scratch_shapes=[pltpu.VMEM((8, LANES), F32)], compiler_params=_params(1),
    )(proj, b_pad, dc, triu)


def _qhead(qp, g):
    return _rms(qp, g) * (HEAD_DIM ** -0.5)


def _column(mat, idx):
    lane = lax.broadcasted_iota(jnp.int32, mat.shape, 1)
    return jnp.sum(jnp.where(lane == idx, mat, 0.0), axis=1, keepdims=True)


def _fox_scores(qi, kk, ccol_i, crow, i, bq):
    length = kk.shape[0]
    s = _dot_nt(qi, kk) + ccol_i - crow[:, :length]
    row = lax.broadcasted_iota(jnp.int32, (bq, length), 0) + i * bq
    col = lax.broadcasted_iota(jnp.int32, (bq, length), 1)
    return jnp.where(col <= row, s, NEG)


def _fox_fwd(name, proj, c, crow, gq, gk, n_heads):
    t = proj.shape[0]
    hw = n_heads * HEAD_DIM
    npair = n_heads // 2
    bq = min(256, t)
    nq = t // bq

    def body(q_ref, k_ref, v_ref, c_ref, crow_ref, gq_ref, gk_ref, o_ref, lse_ref):
        hp = pl.program_id(0)
        lane = lax.broadcasted_iota(jnp.int32, (t, LANES), 1)
        outs, lse_pair = [], jnp.zeros((t, LANES), F32)
        for hh in range(2):
            sl = slice(hh * HEAD_DIM, (hh + 1) * HEAD_DIM)
            qn = _qhead(q_ref[:, sl], gq_ref[...]).astype(BF16)
            kn = _rms(k_ref[:, sl], gk_ref[...]).astype(BF16)
            vb = v_ref[:, sl].astype(BF16)
            ccol = _column(c_ref[...], 2 * hp + hh)
            cr = crow_ref[0, hh:hh + 1, :]
            o_blocks, lse_blocks = [], []
            for i in range(nq):
                length = (i + 1) * bq
                s = _fox_scores(qn[i * bq:(i + 1) * bq], kn[:length], ccol[i * bq:(i + 1) * bq], cr, i, bq)
                m = jnp.max(s, axis=1, keepdims=True)
                p = jnp.exp(s - m)
                l = jnp.sum(p, axis=1, keepdims=True)
                o_blocks.append(_dot(p.astype(BF16), vb[:length]) / l)
                lse_blocks.append(m + jnp.log(l))
            outs.append(jnp.concatenate(o_blocks, axis=0))
            lse_pair = jnp.where(lane == hh, jnp.concatenate(lse_blocks, axis=0), lse_pair)
        o_ref[...] = jnp.concatenate(outs, axis=1).astype(BF16)
        lse_ref[...] = lse_pair

    col = lambda off: (lambda h: (0, off + h))
    fixed = lambda h: (0, 0)
    return pl.pallas_call(
        body, name=name, grid=(npair,),
        in_specs=[pl.BlockSpec((t, LANES), col(0)), pl.BlockSpec((t, LANES), col(npair)), pl.BlockSpec((t, LANES), col(2 * npair)),
                  pl.BlockSpec((t, LANES), fixed), pl.BlockSpec((1, 2, t), lambda h: (h, 0, 0)),
                  pl.BlockSpec((1, HEAD_DIM), fixed), pl.BlockSpec((1, HEAD_DIM), fixed)],
        out_specs=[pl.BlockSpec((t, LANES), col(0)), pl.BlockSpec((t, LANES), col(0))],
        out_shape=[_sds((t, hw), BF16), _sds((t, npair * LANES), F32)], compiler_params=_params(1),
    )(proj, proj, proj, c, crow, gq, gk)


def _fox_bwd(name, proj, c, crow, gq, gk, lse, do, n_heads):
    t = proj.shape[0]
    hw = n_heads * HEAD_DIM
    npair = n_heads // 2
    bq = min(256, t)
    nq = t // bq

    def body(q_ref, k_ref, v_ref, c_ref, crow_ref, gq_ref, gk_ref, lse_ref, do_ref,
             dq_ref, dk_ref, dv_ref, dc_ref, dgq_ref, dgk_ref, dk_acc, dv_acc, dc_acc):
        hp = pl.program_id(0)

        @pl.when(hp == 0)
        def _():
            dgq_ref[...] = jnp.zeros_like(dgq_ref)
            dgk_ref[...] = jnp.zeros_like(dgk_ref)

        dc_ref[...] = jnp.zeros_like(dc_ref)
        dqs, dks, dvs = [], [], []
        for hh in range(2):
            sl = slice(hh * HEAD_DIM, (hh + 1) * HEAD_DIM)
            qp, kp = q_ref[:, sl], k_ref[:, sl]
            qf, q_vjp = jax.vjp(_qhead, qp, gq_ref[...])
            kf, k_vjp = jax.vjp(_rms, kp, gk_ref[...])
            qn, kn = qf.astype(BF16), kf.astype(BF16)
            vb = v_ref[:, sl].astype(BF16)
            dob = do_ref[:, sl]
            ccol = _column(c_ref[...], 2 * hp + hh)
            cr = crow_ref[0, hh:hh + 1, :]
            lse_h = lse_ref[:, hh:hh + 1]
            dk_acc[...] = jnp.zeros_like(dk_acc)
            dv_acc[...] = jnp.zeros_like(dv_acc)
            dc_acc[...] = jnp.zeros_like(dc_acc)
            dq_blocks = []
            for i in range(nq):
                rows = slice(i * bq, (i + 1) * bq)
                length = (i + 1) * bq
                qi, doi = qn[rows], dob[rows]
                s = _fox_scores(qi, kn[:length], ccol[rows], cr, i, bq)
                p = jnp.exp(s - lse_h[rows])
                dp = _dot_nt(doi, vb[:length])
                ds = p * (dp - jnp.sum(p * dp, axis=1, keepdims=True))
                dsb = ds.astype(BF16)
                dq_blocks.append(_dot(dsb, kn[:length]))
                dk_acc[0:length, :] += _dot_tn(dsb, qi)
                dv_acc[0:length, :] += _dot_tn(p.astype(BF16), doi)
                dc_acc[0:1, 0:length] -= jnp.sum(ds, axis=0, keepdims=True)
            dqp, dgq = q_vjp(jnp.concatenate(dq_blocks, axis=0))
            dkp, dgk = k_vjp(dk_acc[...])
            dgq_ref[...] += dgq
            dgk_ref[...] += dgk
            dqs.append(dqp)
            dks.append(dkp)
            dvs.append(dv_acc[...])
            dc_ref[0, hh:hh + 1, :] = dc_acc[0:1, :]
        dq_ref[...] = jnp.concatenate(dqs, axis=1).astype(BF16)
        dk_ref[...] = jnp.concatenate(dks, axis=1).astype(BF16)
        dv_ref[...] = jnp.concatenate(dvs, axis=1).astype(BF16)

    col = lambda off: (lambda h: (0, off + h))
    fixed = lambda h: (0, 0)
    pair_blk = pl.BlockSpec((t, LANES), col(0))
    return pl.pallas_call(
        body, name=name, grid=(npair,),
        in_specs=[pl.BlockSpec((t, LANES), col(0)), pl.BlockSpec((t, LANES), col(npair)), pl.BlockSpec((t, LANES), col(2 * npair)),
                  pl.BlockSpec((t, LANES), fixed), pl.BlockSpec((1, 2, t), lambda h: (h, 0, 0)),
                  pl.BlockSpec((1, HEAD_DIM), fixed), pl.BlockSpec((1, HEAD_DIM), fixed),
                  pair_blk, pair_blk],
        out_specs=[pair_blk, pair_blk, pair_blk, pl.BlockSpec((1, 8, t), lambda h: (h, 0, 0)),
                   pl.BlockSpec((1, HEAD_DIM), fixed), pl.BlockSpec((1, HEAD_DIM), fixed)],
        out_shape=[_sds((t, hw), BF16), _sds((t, hw), BF16), _sds((t, hw), BF16), _sds((npair, 8, t), F32),
                   _sds((1, HEAD_DIM), F32), _sds((1, HEAD_DIM), F32)],
        scratch_shapes=[pltpu.VMEM((t, HEAD_DIM), F32), pltpu.VMEM((t, HEAD_DIM), F32), pltpu.VMEM((8, t), F32)],
        compiler_params=_params(1),
    )(proj, proj, proj, c, crow, gq, gk, lse, do)


def _t5_bucket_table():
    dist = np.arange(WINDOW)[None, :] + WINDOW - np.arange(2 * WINDOW)[:, None]
    n = np.maximum(dist, 0)
    max_exact = N_BUCKETS // 2
    large = max_exact + (np.log(np.maximum(n, 1) / max_exact) / np.log(REL_MAX_DIST / max_exact)
                         * (N_BUCKETS - max_exact)).astype(np.int32)
    large = np.minimum(large, N_BUCKETS - 1)
    return np.where(n < max_exact, n, large).astype(np.int32).reshape(1, -1)


def _bias_expand(name, rel_bias_t):
    n_heads = rel_bias_t.shape[0]
    tbl = jnp.asarray(_t5_bucket_table())
    width = tbl.shape[1]

    def body(rb_ref, tbl_ref, o_ref):
        onehot = (lax.broadcasted_iota(jnp.int32, (N_BUCKETS, width), 0) == tbl_ref[...]).astype(F32)
        o_ref[...] = _dot_exact(rb_ref[...], onehot)

    return pl.pallas_call(body, name=name, out_shape=_sds((n_heads, width), F32), compiler_params=_params(0))(rel_bias_t, tbl)


def _bias_reduce(name, dbias):
    n_heads, width = dbias.shape
    tbl = jnp.asarray(_t5_bucket_table())

    def body(db_ref, tbl_ref, o_ref):
        onehot = (lax.broadcasted_iota(jnp.int32, (N_BUCKETS, width), 0) == tbl_ref[...]).astype(F32)
        o_ref[...] = lax.dot_general(db_ref[...], onehot, (((1,), (1,)), ((), ())), preferred_element_type=F32,
                                     precision=lax.Precision.HIGHEST)

    return pl.pallas_call(body, name=name, out_shape=_sds((n_heads, N_BUCKETS), F32), compiler_params=_params(0))(dbias, tbl)


def _swa_mask(n, group):
    j = lax.broadcasted_iota(jnp.int32, (2 * WINDOW, group * WINDOW), 0)
    i = lax.broadcasted_iota(jnp.int32, (2 * WINDOW, group * WINDOW), 1) & (WINDOW - 1)
    ok = (j > i) & (j <= i + WINDOW) & ((n > 0) | (j >= WINDOW))
    return jnp.where(ok, 0.0, NEG)


def _swa_stack(ref, start, group):
    return jnp.concatenate([ref[pl.ds(start, WINDOW), g * HEAD_DIM:(g + 1) * HEAD_DIM] for g in range(group)], axis=0)


def _swa_fwd(name, qb, kh, vh, gq, gk, sinks, bias, group):
    t = qb.shape[0]
    kvh = kh.shape[0]
    nblk = t // WINDOW
    gw = group * HEAD_DIM
    band = 2 * WINDOW
    cols = group * WINDOW

    def body(q_ref, k_ref, v_ref, gq_ref, gk_ref, sink_ref, bias_ref, o_ref, lse_ref, qs, kpad, vpad):
        for g in range(group):
            qs[:, g * HEAD_DIM:(g + 1) * HEAD_DIM] = _qhead(q_ref[:, g * HEAD_DIM:(g + 1) * HEAD_DIM], gq_ref[...]).astype(BF16)
        kpad[0:WINDOW, :] = jnp.zeros((WINDOW, HEAD_DIM), BF16)
        vpad[0:WINDOW, :] = jnp.zeros((WINDOW, HEAD_DIM), BF16)
        kpad[WINDOW:, :] = _rms(k_ref[0], gk_ref[...]).astype(BF16)
        vpad[WINDOW:, :] = v_ref[0].astype(BF16)
        sink = sink_ref[0]

        def block(n, carry):
            start = pl.multiple_of(n * WINDOW, WINDOW)
            kb = kpad[pl.ds(start, band), :]
            vb = vpad[pl.ds(start, band), :]
            s = _dot_nt(kb, _swa_stack(qs, start, group)) + bias_ref[0] + _swa_mask(n, group)
            m = jnp.maximum(jnp.max(s, axis=0, keepdims=True), sink)
            e = jnp.exp(s - m)
            l = jnp.sum(e, axis=0, keepdims=True) + jnp.exp(sink - m)
            o_t = _dot_tn(vb, e.astype(BF16)) / l
            for g in range(group):
                o_ref[pl.ds(start, WINDOW), g * HEAD_DIM:(g + 1) * HEAD_DIM] = o_t[:, g * WINDOW:(g + 1) * WINDOW].T.astype(BF16)
            lse_ref[pl.ds(n, 1), :] = m + jnp.log(l)
            return carry

        lax.fori_loop(0, nblk, block, 0)

    fixed = lambda h: (0, 0)
    per = lambda h: (h, 0, 0)
    return pl.pallas_call(
        body, name=name, grid=(kvh,),
        in_specs=[pl.BlockSpec((t, gw), lambda h: (0, h)), pl.BlockSpec((1, t, HEAD_DIM), per), pl.BlockSpec((1, t, HEAD_DIM), per),
                  pl.BlockSpec((1, HEAD_DIM), fixed), pl.BlockSpec((1, HEAD_DIM), fixed),
                  pl.BlockSpec((1, 1, cols), per), pl.BlockSpec((1, band, cols), per)],
        out_specs=[pl.BlockSpec((t, gw), lambda h: (0, h)), pl.BlockSpec((nblk, cols), lambda h: (h, 0))],
        out_shape=[_sds((t, kvh * gw), BF16), _sds((kvh * nblk, cols), F32)],
        scratch_shapes=[pltpu.VMEM((t, gw), BF16), pltpu.VMEM((t + WINDOW, HEAD_DIM), BF16),
                        pltpu.VMEM((t + WINDOW, HEAD_DIM), BF16)],
        compiler_params=_params(1),
    )(qb, kh, vh, gq, gk, sinks, bias)


def _swa_bwd(name, qb, kh, vh, gq, gk, sinks, bias, lse, do, group):
    t = qb.shape[0]
    kvh = kh.shape[0]
    nblk = t // WINDOW
    gw = group * HEAD_DIM
    band = 2 * WINDOW
    cols = group * WINDOW

    def body(q_ref, k_ref, v_ref, gq_ref, gk_ref, sink_ref, bias_ref, lse_ref, do_ref,
             dq_ref, dk_ref, dv_ref, dgq_ref, dgk_ref, dsink_ref, dbias_ref,
             qs, kpad, vpad, dqs, dk_acc, dv_acc, dsink_acc):
        @pl.when(pl.program_id(0) == 0)
        def _():
            dgq_ref[...] = jnp.zeros_like(dgq_ref)
            dgk_ref[...] = jnp.zeros_like(dgk_ref)

        for g in range(group):
            qs[:, g * HEAD_DIM:(g + 1) * HEAD_DIM] = _qhead(q_ref[:, g * HEAD_DIM:(g + 1) * HEAD_DIM], gq_ref[...]).astype(BF16)
        kpad[0:WINDOW, :] = jnp.zeros((WINDOW, HEAD_DIM), BF16)
        vpad[0:WINDOW, :] = jnp.zeros((WINDOW, HEAD_DIM), BF16)
        kpad[WINDOW:, :] = _rms(k_ref[0], gk_ref[...]).astype(BF16)
        vpad[WINDOW:, :] = v_ref[0].astype(BF16)
        dk_acc[...] = jnp.zeros_like(dk_acc)
        dv_acc[...] = jnp.zeros_like(dv_acc)
        dsink_acc[...] = jnp.zeros_like(dsink_acc)
        dbias_ref[...] = jnp.zeros_like(dbias_ref)
        sink = sink_ref[0]

        def block(n, carry):
            start = pl.multiple_of(n * WINDOW, WINDOW)
            kb = kpad[pl.ds(start, band), :]
            vb = vpad[pl.ds(start, band), :]
            q = _swa_stack(qs, start, group)
            dob = _swa_stack(do_ref, start, group)
            lse_n = lse_ref[pl.ds(n, 1), :]
            s = _dot_nt(kb, q) + bias_ref[0] + _swa_mask(n, group)
            p = jnp.exp(s - lse_n)
            dp = _dot_nt(vb, dob)
            dsum = jnp.sum(p * dp, axis=0, keepdims=True)
            ds = p * (dp - dsum)
            dsb = ds.astype(BF16)
            dsink_acc[...] -= jnp.exp(sink - lse_n) * dsum
            dbias_ref[0] += ds
            dq = _dot_tn(dsb, kb)
            for g in range(group):
                dqs[pl.ds(start, WINDOW), g * HEAD_DIM:(g + 1) * HEAD_DIM] = dq[g * WINDOW:(g + 1) * WINDOW]
            dk_acc[pl.ds(start, band), :] += _dot(dsb, q)
            dv_acc[pl.ds(start, band), :] += _dot(p.astype(BF16), dob)
            return carry

        lax.fori_loop(0, nblk, block, 0)
        for g in range(group):
            _, q_vjp = jax.vjp(_qhead, q_ref[:, g * HEAD_DIM:(g + 1) * HEAD_DIM], gq_ref[...])
            dqp, dgq = q_vjp(dqs[:, g * HEAD_DIM:(g + 1) * HEAD_DIM])
            dq_ref[:, g * HEAD_DIM:(g + 1) * HEAD_DIM] = dqp.astype(BF16)
            dgq_ref[...] += dgq
            dsink_g = jnp.sum(dsink_acc[:, g * WINDOW:(g + 1) * WINDOW], axis=1, keepdims=True)
            dsink_ref[0, g:g + 1, :] = jnp.broadcast_to(dsink_g, (1, LANES))
        _, k_vjp = jax.vjp(_rms, k_ref[0], gk_ref[...])
        dkp, dgk = k_vjp(dk_acc[WINDOW:, :])
        dk_ref[0] = dkp
        dgk_ref[...] += dgk
        dv_ref[0] = dv_acc[WINDOW:, :]

    fixed = lambda h: (0, 0)
    per = lambda h: (h, 0, 0)
    wide = pl.BlockSpec((t, gw), lambda h: (0, h))
    head = pl.BlockSpec((1, t, HEAD_DIM), per)
    vec = pl.BlockSpec((1, HEAD_DIM), fixed)
    bias_spec = pl.BlockSpec((1, band, cols), per)
    return pl.pallas_call(
        body, name=name, grid=(kvh,),
        in_specs=[wide, head, head, vec, vec, pl.BlockSpec((1, 1, cols), per), bias_spec,
                  pl.BlockSpec((nblk, cols), lambda h: (h, 0)), wide],
        out_specs=[wide, head, head, vec, vec, pl.BlockSpec((1, group, LANES), per), bias_spec],
        out_shape=[_sds((t, kvh * gw), BF16), _sds((kvh, t, HEAD_DIM), F32), _sds((kvh, t, HEAD_DIM), F32),
                   _sds((1, HEAD_DIM), F32), _sds((1, HEAD_DIM), F32),
                   _sds((kvh, group, LANES), F32), _sds((kvh, band, cols), F32)],
        scratch_shapes=[pltpu.VMEM((t, gw), BF16), pltpu.VMEM((t + WINDOW, HEAD_DIM), BF16),
                        pltpu.VMEM((t + WINDOW, HEAD_DIM), BF16), pltpu.VMEM((t, gw), F32),
                        pltpu.VMEM((t + WINDOW, HEAD_DIM), F32), pltpu.VMEM((t + WINDOW, HEAD_DIM), F32),
                        pltpu.VMEM((1, cols), F32)],
        compiler_params=_params(1),
    )(qb, kh, vh, gq, gk, sinks, bias, lse, do)


def _local_step(x, target, p, comm):
    t, d = x.shape
    n_heads = d // HEAD_DIM
    kv_heads = n_heads // 8
    group = n_heads // kv_heads
    hw = n_heads * HEAD_DIM
    gate_col = 3 * hw // LANES
    kvw = kv_heads * HEAD_DIM
    grads = {}

    def mlp_fwd(tag, h, g, layer):
        w_up, = comm.weights([f"w_up{layer}"], h)
        u, hn, a = _norm_matmul(f"{tag}_up", h, g, w_up, relu2=True)
        w_down, = comm.weights([f"w_down{layer}"], a)
        return _matmul_res(f"{tag}_down", a, w_down, h), (h, g, u, hn, a, w_up, w_down)

    def mlp_bwd(tag, saved, layer, dy):
        h, g, u, hn, a, w_up, w_down = saved
        du = _matmul_nt(f"{tag}_du", dy, w_down, u=u)
        dw_down = _matmul_tn(f"{tag}_dwdown", a, dy)
        dw_up = _matmul_tn(f"{tag}_dwup", hn, du, col_blocks=w_up.shape[0])
        zero = comm.send_grads(tag, {f"w_down{layer}": dw_down, f"w_up{layer}": dw_up})
        return _matmul_nt_rmsbwd(f"{tag}_dh", du, w_up, h, g + zero, dy)

    w_in, = comm.weights(["w_in_a"], None)
    proj, xn1 = _norm_matmul("a_inproj", x, p["g_attn"][0], w_in, tn=640)
    b_pad = jnp.pad(p["b_f"], ((0, 0), (0, LANES - n_heads)))
    c = _gate_fwd("a_gate", proj, b_pad, n_heads, gate_col)
    crow = c[:, :n_heads].T.reshape(n_heads // 2, 2, t)
    o_a, lse_a = _fox_fwd("a_attn", proj, c, crow, p["gq_a"], p["gk_a"], n_heads)
    w_out_a, = comm.weights(["w_out_a"], o_a)
    h1 = _matmul_res("a_outproj", o_a, w_out_a, x)
    h2, mlp0 = mlp_fwd("mlp0", h1, p["g_mlp"][0], 0)

    w_kv, w_q_b = comm.weights(["w_kv", "w_q_b"], h2)
    kv, hn_kv = _norm_matmul("kv_proj", h2, p["g_kv"], w_kv, tn=2 * kvw)
    kh = kv[:, :kvw].reshape(t, kv_heads, HEAD_DIM).transpose(1, 0, 2)
    vh = kv[:, kvw:].reshape(t, kv_heads, HEAD_DIM).transpose(1, 0, 2)
    qb, hn_q = _norm_matmul("b_qproj", h2, p["g_attn"][1], w_q_b, tn=512)
    gqb, gkb = p["gq_b"], p["gk_b"].reshape(1, HEAD_DIM)
    bias = _bias_expand("b_bias", p["rel_bias"].T).reshape(kv_heads, group, 2 * WINDOW, WINDOW)
    bias = bias.transpose(0, 2, 1, 3).reshape(kv_heads, 2 * WINDOW, group * WINDOW)
    sink_rows = jnp.broadcast_to(p["sinks"].reshape(kv_heads, 1, group, 1), (kv_heads, 1, group, WINDOW)).reshape(kv_heads, 1, group * WINDOW)
    o_b, lse_b = _swa_fwd("b_attn", qb, kh, vh, gqb, gkb, sink_rows, bias, group)
    w_out_b, = comm.weights(["w_out_b"], o_b)
    h3 = _matmul_res("b_outproj", o_b, w_out_b, h2)
    y, mlp1 = mlp_fwd("mlp1", h3, p["g_mlp"][1], 1)
    dy, loss_tile = _loss_grad("loss", y, target)

    dh3, dg_mlp1 = mlp_bwd("mlp1", mlp1, 1, dy)
    do_b = _matmul_nt("b_do", dh3, w_out_b)
    dw_out_b = _matmul_tn("b_dwout", o_b, dh3)
    dqb, dkh, dvh, grads["gq_b"], dgk_b, dsink, dbias = _swa_bwd(
        "b_attn_bwd", qb, kh, vh, gqb, gkb, sink_rows, bias, lse_b, do_b, group)
    grads["gk_b"] = dgk_b.reshape(HEAD_DIM)
    grads["sinks"] = dsink[:, :, 0].reshape(1, n_heads)
    dbias = dbias.reshape(kv_heads, 2 * WINDOW, group, WINDOW).transpose(0, 2, 1, 3)
    grads["rel_bias"] = _bias_reduce("b_dbias", dbias.reshape(n_heads, WINDOW * 2 * WINDOW)).T
    dw_q_b = _matmul_tn("b_dwq", hn_q, dqb)
    dh2, dg_attn1 = _matmul_nt_rmsbwd("b_dhq", dqb, w_q_b, h2, p["g_attn"][1], dh3)
    dkv = jnp.concatenate([dkh.transpose(1, 0, 2).reshape(t, kvw), dvh.transpose(1, 0, 2).reshape(t, kvw)], axis=1)
    dw_kv = _matmul_tn("kv_dw", hn_kv, dkv)
    zero = comm.send_grads("attn_b", {"w_out_b": dw_out_b, "w_q_b": dw_q_b, "w_kv": dw_kv})
    dh2, dg_kv = _matmul_nt_rmsbwd("kv_dh", dkv, w_kv, h2, p["g_kv"] + zero, dh2)
    grads["g_kv"] = dg_kv.reshape(d)
    dh1, dg_mlp0 = mlp_bwd("mlp0", mlp0, 0, dh2)
    grads["g_mlp"] = jnp.concatenate([dg_mlp0, dg_mlp1], axis=0)

    do_a = _matmul_nt("a_do", dh1, w_out_a)
    dw_out_a = _matmul_tn("a_dwout", o_a, dh1)
    zero = comm.send_grads("attn_a_out", {"w_out_a": dw_out_a})
    dq, dk, dv, dc_rows, grads["gq_a"], grads["gk_a"] = _fox_bwd(
        "a_attn_bwd", proj, c, crow, p["gq_a"] + zero, p["gk_a"], lse_a, do_a, n_heads)
    dc = jnp.pad(dc_rows[:, :2, :].reshape(n_heads, t).T, ((0, 0), (0, LANES - n_heads)))
    dfl, db_f = _gate_bwd("a_gate_bwd", proj, b_pad, dc, n_heads, gate_col)
    grads["b_f"] = db_f[:, :n_heads]
    dproj = jnp.concatenate([dq, dk, dv, dfl], axis=1)
    dw_in = _matmul_tn("a_dwin", xn1, dproj, tn=640)
    zero = comm.send_grads("attn_a_in", {"w_in_a": dw_in})
    grad_x, dg_attn0 = _matmul_nt_rmsbwd("a_dx", dproj, w_in, x, p["g_attn"][0] + zero, dh1)
    grads["g_attn"] = jnp.concatenate([dg_attn0, dg_attn1], axis=0)
    return loss_tile, grad_x, grads


def _exchange(name, arrays, scatter, after):
    n = len(arrays)

    def body(*refs):
        src, out = refs[:n], refs[n + 1:2 * n + 1]
        send_sems, recv_sems, local_sems = refs[2 * n + 1:]
        x, y, c = lax.axis_index("x"), lax.axis_index("y"), lax.axis_index("c")
        me = 4 * x + 2 * y + c
        peers = []
        for k in range(1, N_DEV):
            px, py, pc = x ^ ((k >> 2) & 1), y ^ ((k >> 1) & 1), c ^ (k & 1)
            peers.append(((px, py, pc), 4 * px + 2 * py + pc))
        local, sends = [], []
        for a in range(n):
            mine = src[a].at[me] if scatter else src[a]
            cp = pltpu.make_async_copy(mine, out[a].at[me], local_sems.at[a])
            cp.start()
            local.append(cp)
        for k, (peer, peer_pos) in enumerate(peers):
            for a in range(n):
                cp = pltpu.make_async_remote_copy(
                    src_ref=src[a].at[peer_pos] if scatter else src[a], dst_ref=out[a].at[me],
                    send_sem=send_sems.at[a, k], recv_sem=recv_sems.at[a, k],
                    device_id=peer, device_id_type=pl.DeviceIdType.MESH)
                cp.start()
                sends.append(cp)
        for k, (peer, peer_pos) in enumerate(peers):
            for a in range(n):
                pltpu.make_async_remote_copy(
                    src_ref=out[a].at[peer_pos], dst_ref=out[a].at[peer_pos],
                    send_sem=send_sems.at[a, k], recv_sem=recv_sems.at[a, k],
                    device_id=peer, device_id_type=pl.DeviceIdType.MESH).wait_recv()
        for cp in sends:
            cp.wait_send()
        for cp in local:
            cp.wait()

    any_spec = pl.BlockSpec(memory_space=pl.ANY)
    out_shape = [_sds(a.shape if scatter else (N_DEV,) + a.shape, a.dtype) for a in arrays]
    return pl.pallas_call(
        body, name=name, in_specs=[any_spec] * (n + 1), out_specs=[any_spec] * n, out_shape=out_shape,
        scratch_shapes=[pltpu.SemaphoreType.DMA((n, N_DEV - 1)), pltpu.SemaphoreType.DMA((n, N_DEV - 1)),
                        pltpu.SemaphoreType.DMA((n,))],
    )(*arrays, after)


class _InFlight:
    def __init__(self, scatter, send_sems, recv_sems, srcs, lands, token):
        self.scatter, self.send_sems, self.recv_sems = scatter, send_sems, recv_sems
        self.srcs, self.lands, self.token = list(srcs), list(lands), token


def _mesh_peers():
    x, y, c = lax.axis_index("x"), lax.axis_index("y"), lax.axis_index("c")
    peers = []
    for k in range(1, N_DEV):
        px, py, pc = x ^ ((k >> 2) & 1), y ^ ((k >> 1) & 1), c ^ (k & 1)
        peers.append(((px, py, pc), 4 * px + 2 * py + pc))
    return 4 * x + 2 * y + c, peers


_HBM_SPEC = pl.BlockSpec(memory_space=pltpu.HBM)
_SEM_SPEC = pl.BlockSpec(memory_space=pltpu.SEMAPHORE)
_SIDE_EFFECT = pltpu.SideEffectType.DATAFLOW_SIDE_EFFECTING


def _exchange_start(name, arrays, scatter):
    n = len(arrays)
    me, _ = _mesh_peers()
    lands = []
    for a in arrays:
        own = lax.dynamic_index_in_dim(a, me, 0, keepdims=False) if scatter else a
        shape = a.shape if scatter else (N_DEV,) + a.shape
        lands.append(lax.dynamic_update_index_in_dim(lax.empty(shape, a.dtype), own, me, 0))

    def body(*refs):
        src, land = refs[:n], refs[n:2 * n]
        send_sems, recv_sems, token = refs[2 * n], refs[2 * n + 1], refs[-1]
        pos, peers = _mesh_peers()
        for a in range(n):
            for k, (peer, peer_pos) in enumerate(peers):
                pltpu.make_async_remote_copy(
                    src_ref=src[a].at[peer_pos] if scatter else src[a], dst_ref=land[a].at[pos],
                    send_sem=send_sems.at[a * (N_DEV - 1) + k], recv_sem=recv_sems.at[a * (N_DEV - 1) + k],
                    device_id=peer, device_id_type=pl.DeviceIdType.MESH).start()
        token[...] = jnp.zeros_like(token)

    operands = [pltpu.with_memory_space_constraint(a, pltpu.HBM) for a in list(arrays) + lands]
    outs = pl.pallas_call(
        body, name=name,
        out_shape=(pltpu.SemaphoreType.DMA((n * (N_DEV - 1),)), pltpu.SemaphoreType.DMA((n * (N_DEV - 1),)),
                   *[pltpu.HBM(a.shape, a.dtype) for a in operands], _sds((8, LANES), F32)),
        in_specs=[_HBM_SPEC] * (2 * n),
        out_specs=(_SEM_SPEC, _SEM_SPEC, *[_HBM_SPEC] * (2 * n), pl.BlockSpec(memory_space=pltpu.VMEM)),
        input_output_aliases={i: 2 + i for i in range(2 * n)},
        compiler_params=pltpu.CompilerParams(has_side_effects=_SIDE_EFFECT),
    )(*operands)
    return _InFlight(scatter, outs[0], outs[1], outs[2:2 + n], outs[2 + n:2 + 2 * n], outs[-1])


def _exchange_wait(name, flight, which, after):
    m = len(which)
    scatter = flight.scatter

    def body(*refs):
        src, land = refs[:m], refs[m:2 * m]
        send_sems, recv_sems = refs[2 * m], refs[2 * m + 1]
        _, peers = _mesh_peers()
        for i, a in enumerate(which):
            for k, (peer, peer_pos) in enumerate(peers):
                cp = pltpu.make_async_remote_copy(
                    src_ref=src[i].at[peer_pos] if scatter else src[i], dst_ref=land[i].at[peer_pos],
                    send_sem=send_sems.at[a * (N_DEV - 1) + k], recv_sem=recv_sems.at[a * (N_DEV - 1) + k],
                    device_id=peer, device_id_type=pl.DeviceIdType.MESH)
                cp.wait_send()
                cp.wait_recv()

    operands = [flight.srcs[a] for a in which] + [flight.lands[a] for a in which]
    outs = pl.pallas_call(
        body, name=name, out_shape=tuple(pltpu.HBM(a.shape, a.dtype) for a in operands),
        in_specs=[_HBM_SPEC] * (2 * m) + [_SEM_SPEC, _SEM_SPEC, pl.BlockSpec(memory_space=pl.ANY)],
        out_specs=tuple([_HBM_SPEC] * (2 * m)), input_output_aliases={i: i for i in range(2 * m)},
        compiler_params=pltpu.CompilerParams(has_side_effects=_SIDE_EFFECT),
    )(*operands, flight.send_sems, flight.recv_sems, after)
    return list(outs[m:])


def _adamw(name, parts, w, m, v, layer=None, into=None):
    r, c = w.shape[-2:]
    tr = r if r <= 256 else 256
    n_into = 0 if into is None else len(into)

    def body(p_ref, w_ref, m_ref, v_ref, *refs):
        g_ref, d_ref, mo_ref, vo_ref = refs[n_into:]
        g = p_ref[0].astype(F32)
        for dev in range(1, N_DEV):
            g = g + p_ref[dev].astype(F32)
        m_new = ADAM_B1 * m_ref[...] + (1.0 - ADAM_B1) * g
        v_new = ADAM_B2 * v_ref[...] + (1.0 - ADAM_B2) * jnp.square(g)
        m_hat = m_new / (1.0 - ADAM_B1 ** ADAM_STEP)
        v_hat = v_new / (1.0 - ADAM_B2 ** ADAM_STEP)
        g_ref[...] = g
        d_ref[...] = -ADAM_LR * (m_hat / (jnp.sqrt(v_hat) + ADAM_EPS) + ADAM_WD * w_ref[...])
        mo_ref[...] = m_new
        vo_ref[...] = v_new

    if layer is None:
        blk = pl.BlockSpec((tr, c), lambda i: (i, 0))
    else:
        blk = pl.BlockSpec((None, tr, c), lambda i: (layer, i, 0))
    return pl.pallas_call(
        body, name=name, grid=(r // tr,),
        in_specs=[pl.BlockSpec((N_DEV, tr, c), lambda i: (0, i, 0)), blk, blk, blk] + [pl.BlockSpec(memory_space=pl.ANY)] * n_into,
        out_specs=[blk] * 4, out_shape=[_sds(w.shape, F32)] * 4,
        input_output_aliases={4 + i: i for i in range(n_into)}, compiler_params=_params(1),
    )(parts, w, m, v, *(into or ()))


def _pack_small(tree):
    flat = jnp.concatenate([tree[k].reshape(-1) for k in SMALL])
    size = -(-flat.shape[0] // (8 * LANES)) * (8 * LANES)
    return jnp.pad(flat, (0, size - flat.shape[0])).reshape(-1, LANES)


def _unpack_small(packed, like):
    flat, out, off = packed.reshape(-1), {}, 0
    for k in SMALL:
        size = like[k].size
        out[k] = flat[off:off + size].reshape(like[k].shape)
        off += size
    return out


class _Comm:
    ORDER = ("w_in_a", "w_out_a", "w_up0", "w_down0", "w_kv", "w_q_b", "w_out_b", "w_up1", "w_down1")

    def __init__(self, shards, d, n_in):
        self.d, self.n_in = d, n_in
        self.flight = _exchange_start("gather_start", [shards[n].astype(BF16) for n in self.ORDER], scatter=False)
        self.sent = []

    def weights(self, names, after):
        which = [self.ORDER.index(n) for n in names]
        landed = _exchange_wait(f"gather_wait_{names[0]}", self.flight, which, self.flight.token if after is None else after)
        return [self._whole(n, g) for n, g in zip(names, landed)]

    def _whole(self, name, g):
        if name == "w_in_a":
            pad = -(-self.n_in // LANES) * LANES - self.n_in
            return jnp.pad(g.transpose(1, 0, 2).reshape(self.d, self.n_in), ((0, 0), (0, pad)))
        if name.startswith("w_up"):
            return g
        return g.reshape(-1, g.shape[-1])

    def _chunks(self, name, g):
        if name == "w_in_a":
            return g[:, :self.n_in].reshape(self.d, N_DEV, -1).transpose(1, 0, 2)
        if name.startswith("w_up"):
            return g
        return g.reshape(N_DEV, g.shape[0] // N_DEV, g.shape[1])

    def send_grads(self, tag, partials):
        names = list(partials)
        flight = _exchange_start(f"scatter_start_{tag}", [self._chunks(n, partials[n]) for n in names], scatter=True)
        self.sent.append((tag, flight, names))
        return flight.token[0, 0]

    def received(self, index, after):
        tag, flight, names = self.sent[index]
        landed = _exchange_wait(f"scatter_wait_{tag}", flight, list(range(len(names))), after)
        return dict(zip(names, landed))


def kernel(x, g_attn, g_mlp, w_in_a, b_f, gq_a, gk_a, w_out_a, g_kv, w_kv, gk_b, w_q_b, gq_b, sinks, rel_bias, w_out_b, w_up, w_down, loss_target, m_g_attn, m_g_mlp, m_w_in_a, m_b_f, m_gq_a, m_gk_a, m_w_out_a, m_g_kv, m_w_kv, m_gk_b, m_w_q_b, m_gq_b, m_sinks, m_rel_bias, m_w_out_b, m_w_up, m_w_down, v_g_attn, v_g_mlp, v_w_in_a, v_b_f, v_gq_a, v_gk_a, v_w_out_a, v_g_kv, v_w_kv, v_gk_b, v_w_q_b, v_gq_b, v_sinks, v_rel_bias, v_w_out_b, v_w_up, v_w_down):
    w = dict(g_attn=g_attn, g_mlp=g_mlp, w_in_a=w_in_a, b_f=b_f, gq_a=gq_a, gk_a=gk_a, w_out_a=w_out_a, g_kv=g_kv,
             w_kv=w_kv, gk_b=gk_b, w_q_b=w_q_b, gq_b=gq_b, sinks=sinks, rel_bias=rel_bias, w_out_b=w_out_b,
             w_up=w_up, w_down=w_down)
    mom = dict(g_attn=m_g_attn, g_mlp=m_g_mlp, w_in_a=m_w_in_a, b_f=m_b_f, gq_a=m_gq_a, gk_a=m_gk_a, w_out_a=m_w_out_a,
               g_kv=m_g_kv, w_kv=m_w_kv, gk_b=m_gk_b, w_q_b=m_w_q_b, gq_b=m_gq_b, sinks=m_sinks, rel_bias=m_rel_bias,
               w_out_b=m_w_out_b, w_up=m_w_up, w_down=m_w_down)
    var = dict(g_attn=v_g_attn, g_mlp=v_g_mlp, w_in_a=v_w_in_a, b_f=v_b_f, gq_a=v_gq_a, gk_a=v_gk_a, w_out_a=v_w_out_a,
               g_kv=v_g_kv, w_kv=v_w_kv, gk_b=v_gk_b, w_q_b=v_w_q_b, gq_b=v_gq_b, sinks=v_sinks, rel_bias=v_rel_bias,
               w_out_b=v_w_out_b, w_up=v_w_up, w_down=v_w_down)
    d = x.shape[2]
    where = {"w_in_a": ("w_in_a", 0), "w_out_a": ("w_out_a", 0), "w_kv": ("w_kv", None), "w_q_b": ("w_q_b", 0),
             "w_out_b": ("w_out_b", 0), "w_up0": ("w_up", 0), "w_up1": ("w_up", 1), "w_down0": ("w_down", 0),
             "w_down1": ("w_down", 1)}
    shards = {n: (w[k] if layer is None else w[k][layer]) for n, (k, layer) in where.items()}
    comm = _Comm(shards, d, w_in_a.shape[2] * N_DEV)
    loss_tile, grad_x, grads = _local_step(x[0], loss_target[0], {k: w[k] for k in SMALL}, comm)
    loss = lax.psum(loss_tile[0, 0], ("x", "y", "c"))

    res, after = {}, grad_x
    last = len(comm.sent) - 1
    for index in range(last + 1):
        if index == last:
            small_parts, = _exchange("gather_small_grads", [_pack_small(grads)], scatter=False, after=after)
            small = _adamw("adam_small", small_parts, _pack_small(w), _pack_small(mom), _pack_small(var))
            after = small[0]
        for n, parts in comm.received(index, after).items():
            k, layer = where[n]
            res[k] = _adamw(f"adam_{n}", parts, w[k], mom[k], var[k], layer, res.get(k))
            after = res[k][0]
    small = [_unpack_small(s, w) for s in small]
    for k in SMALL:
        res[k] = [s[k] for s in small]

    outs = [loss, grad_x[None]]
    for i in range(4):
        outs.extend(res[k][i] for k in WEIGHTS)
    return tuple(outs)
```

```python
import functools

import numpy as np
import jax
import jax.numpy as jnp
from jax import lax
from jax.experimental import pallas as pl
from jax.experimental.pallas import tpu as pltpu

F32 = jnp.float32
BF16 = jnp.bfloat16

N_DEV = 8
HEAD_DIM = 64
WINDOW = 128
N_BUCKETS = 32
REL_MAX_DIST = 128
NORM_EPS = 1e-6
NEG = -1e30
LANES = 128
VMEM_LIMIT = 56 * 1024 * 1024

ADAM_LR = 0.001
ADAM_B1 = 0.9
ADAM_B2 = 0.999
ADAM_EPS = 1e-08
ADAM_WD = 0.01
ADAM_STEP = 10

SMALL = ("g_attn", "g_mlp", "b_f", "gq_a", "gk_a", "g_kv", "gk_b", "gq_b", "sinks", "rel_bias")
BIG = ("w_in_a", "w_out_a", "w_kv", "w_q_b", "w_out_b", "w_up", "w_down")
WEIGHTS = ("g_attn", "g_mlp", "w_in_a", "b_f", "gq_a", "gk_a", "w_out_a", "g_kv", "w_kv", "gk_b",
           "w_q_b", "gq_b", "sinks", "rel_bias", "w_out_b", "w_up", "w_down")


def _params(n_grid):
    return pltpu.CompilerParams(dimension_semantics=("arbitrary",) * n_grid, vmem_limit_bytes=VMEM_LIMIT)


def _sds(shape, dtype):
    return jax.ShapeDtypeStruct(tuple(shape), dtype)


def _rms(x, g):
    return (x * lax.rsqrt(jnp.mean(x * x, axis=-1, keepdims=True) + NORM_EPS)) * g


def _dot_nt(a, b):
    return lax.dot_general(a, b, (((1,), (1,)), ((), ())), preferred_element_type=F32)


def _dot_tn(a, b):
    return lax.dot_general(a, b, (((0,), (0,)), ((), ())), preferred_element_type=F32)


def _dot(a, b):
    return jnp.dot(a, b, preferred_element_type=F32)


def _dot_exact(a, b):
    return jnp.dot(a, b, preferred_element_type=F32, precision=lax.Precision.HIGHEST)


def _norm_matmul(name, x, g, w, *, tn=None, relu2=False):
    t, d = x.shape
    blocked = w.ndim == 3
    if blocked:
        tn = w.shape[2]
        n = w.shape[0] * tn
        w_spec = pl.BlockSpec((None, d, tn), lambda i, j: (j, 0, 0))
    else:
        n = w.shape[1]
        w_spec = pl.BlockSpec((d, tn), lambda i, j: (0, j))
    tm = min(1024, t)

    def body(x_ref, g_ref, w_ref, y_ref, xn_ref, *a_ref):
        @pl.when(pl.program_id(1) == 0)
        def _():
            xn_ref[...] = _rms(x_ref[...], g_ref[...]).astype(BF16)

        y = _dot(xn_ref[...], w_ref[...])
        y_ref[...] = y
        if relu2:
            a_ref[0][...] = jnp.square(jnp.maximum(y, 0.0)).astype(BF16)

    out_shape = [_sds((t, n), F32), _sds((t, d), BF16)]
    out_specs = [pl.BlockSpec((tm, tn), lambda i, j: (i, j)), pl.BlockSpec((tm, d), lambda i, j: (i, 0))]
    if relu2:
        out_shape.append(_sds((t, n), BF16))
        out_specs.append(pl.BlockSpec((tm, tn), lambda i, j: (i, j)))
    return pl.pallas_call(
        body, name=name, grid=(t // tm, n // tn),
        in_specs=[pl.BlockSpec((tm, d), lambda i, j: (i, 0)), pl.BlockSpec((1, d), lambda i, j: (0, 0)), w_spec],
        out_specs=out_specs, out_shape=out_shape, compiler_params=_params(2),
    )(x, g.reshape(1, d), w)


def _matmul_res(name, a, w, res, *, tn=512):
    t, k = a.shape
    n = w.shape[1]
    tm = min(1024, t)

    def body(a_ref, w_ref, r_ref, o_ref):
        o_ref[...] = r_ref[...] + _dot(a_ref[...], w_ref[...])

    return pl.pallas_call(
        body, name=name, grid=(t // tm, n // tn),
        in_specs=[pl.BlockSpec((tm, k), lambda i, j: (i, 0)), pl.BlockSpec((k, tn), lambda i, j: (0, j)),
                  pl.BlockSpec((tm, tn), lambda i, j: (i, j))],
        out_specs=pl.BlockSpec((tm, tn), lambda i, j: (i, j)), out_shape=_sds((t, n), F32),
        compiler_params=_params(2),
    )(a, w, res)


def _matmul_nt(name, dy, w, *, u=None, tk=1024):
    t, n = dy.shape
    k = w.shape[0]
    tm = min(1024, t)

    def body(dy_ref, w_ref, *rest):
        o_ref = rest[-1]
        r = _dot_nt(dy_ref[...].astype(BF16), w_ref[...])
        if u is not None:
            r = r * (2.0 * jnp.maximum(rest[0][...], 0.0))
        o_ref[...] = r.astype(BF16)

    in_specs = [pl.BlockSpec((tm, n), lambda i, j: (i, 0)), pl.BlockSpec((tk, n), lambda i, j: (j, 0))]
    args = [dy, w]
    if u is not None:
        in_specs.append(pl.BlockSpec((tm, tk), lambda i, j: (i, j)))
        args.append(u)
    return pl.pallas_call(
        body, name=name, grid=(t // tm, k // tk), in_specs=in_specs,
        out_specs=pl.BlockSpec((tm, tk), lambda i, j: (i, j)), out_shape=_sds((t, k), BF16),
        compiler_params=_params(2),
    )(*args)


def _matmul_nt_rmsbwd(name, dy, w, x, g, dres):
    t, k = dy.shape
    blocked = w.ndim == 3
    d = w.shape[1] if blocked else w.shape[0]
    tm = min(512, t)

    def body(dy_ref, w_ref, x_ref, g_ref, r_ref, dx_ref, dg_ref):
        if blocked:
            kb = w.shape[2]
            dxn = _dot_nt(dy_ref[:, 0:kb].astype(BF16), w_ref[0])
            for j in range(1, w.shape[0]):
                dxn += _dot_nt(dy_ref[:, j * kb:(j + 1) * kb].astype(BF16), w_ref[j])
        else:
            dxn = _dot_nt(dy_ref[...].astype(BF16), w_ref[...])
        _, vjp = jax.vjp(_rms, x_ref[...], g_ref[...])
        dx, dg = vjp(dxn)
        dx_ref[...] = r_ref[...] + dx

        @pl.when(pl.program_id(0) == 0)
        def _():
            dg_ref[...] = jnp.zeros_like(dg_ref)

        dg_ref[...] += dg

    row = lambda i: (i, 0)
    fixed = lambda i: (0, 0)
    return pl.pallas_call(
        body, name=name, grid=(t // tm,),
        in_specs=[pl.BlockSpec((tm, k), row), pl.BlockSpec(w.shape, (lambda i: (0, 0, 0)) if blocked else fixed),
                  pl.BlockSpec((tm, d), row), pl.BlockSpec((1, d), fixed), pl.BlockSpec((tm, d), row)],
        out_specs=[pl.BlockSpec((tm, d), row), pl.BlockSpec((1, d), fixed)],
        out_shape=[_sds((t, d), F32), _sds((1, d), F32)], compiler_params=_params(1),
    )(dy, w, x, g.reshape(1, d), dres)


def _matmul_tn(name, a, b, *, tk=1024, tn=1024, col_blocks=None):
    t, k = a.shape
    n = b.shape[1]
    tk = min(tk, k)
    if col_blocks:
        tn = n // col_blocks
        out_spec, out_shape = pl.BlockSpec((None, tk, tn), lambda i, j: (j, i, 0)), _sds((col_blocks, k, tn), BF16)
    else:
        tn = min(tn, n)
        out_spec, out_shape = pl.BlockSpec((tk, tn), lambda i, j: (i, j)), _sds((k, n), BF16)

    def body(a_ref, b_ref, o_ref):
        o_ref[...] = _dot_tn(a_ref[...].astype(BF16), b_ref[...].astype(BF16)).astype(BF16)

    return pl.pallas_call(
        body, name=name, grid=(k // tk, n // tn),
        in_specs=[pl.BlockSpec((t, tk), lambda i, j: (0, i)), pl.BlockSpec((t, tn), lambda i, j: (0, j))],
        out_specs=out_spec, out_shape=out_shape, compiler_params=_params(2),
    )(a, b)


def _loss_grad(name, y, target):
    t, d = y.shape
    tm = min(512, t)

    def body(y_ref, t_ref, dy_ref, l_ref):
        e = y_ref[...] - t_ref[...]
        dy_ref[...] = e * (1.0 / d)

        @pl.when(pl.program_id(0) == 0)
        def _():
            l_ref[...] = jnp.zeros_like(l_ref)

        l_ref[...] += 0.5 * jnp.sum(jnp.mean(e * e, axis=-1))

    row = lambda i: (i, 0)
    return pl.pallas_call(
        body, name=name, grid=(t // tm,), in_specs=[pl.BlockSpec((tm, d), row), pl.BlockSpec((tm, d), row)],
        out_specs=[pl.BlockSpec((tm, d), row), pl.BlockSpec((8, LANES), lambda i: (0, 0))],
        out_shape=[_sds((t, d), F32), _sds((8, LANES), F32)], compiler_params=_params(1),
    )(y, target)


def _gate_fwd(name, proj, b_pad, n_heads, gate_col):
    t = proj.shape[0]
    tb = min(256, t)
    tri = jnp.asarray(np.tril(np.ones((tb, tb), np.float32)))

    def body(p_ref, b_ref, tri_ref, c_ref, carry):
        @pl.when(pl.program_id(0) == 0)
        def _():
            carry[...] = jnp.zeros_like(carry)

        lane = lax.broadcasted_iota(jnp.int32, (tb, LANES), 1)
        lf = jnp.where(lane < n_heads, jax.nn.log_sigmoid(p_ref[...] + b_ref[...]), 0.0)
        c = _dot_exact(tri_ref[...], lf) + carry[0:1, :]
        c_ref[...] = c
        carry[...] = jnp.broadcast_to(c[tb - 1:tb, :], carry.shape)

    return pl.pallas_call(
        body, name=name, grid=(t // tb,),
        in_specs=[pl.BlockSpec((tb, LANES), lambda i: (i, gate_col)), pl.BlockSpec((1, LANES), lambda i: (0, 0)),
                  pl.BlockSpec((tb, tb), lambda i: (0, 0))],
        out_specs=pl.BlockSpec((tb, LANES), lambda i: (i, 0)), out_shape=_sds((t, LANES), F32),
        scratch_shapes=[pltpu.VMEM((8, LANES), F32)], compiler_params=_params(1),
    )(proj, b_pad, tri)


def _gate_bwd(name, proj, b_pad, dc, n_heads, gate_col):
    t = proj.shape[0]
    tb = min(256, t)
    nb = t // tb
    triu = jnp.asarray(np.triu(np.ones((tb, tb), np.float32)))

    def body(p_ref, b_ref, dc_ref, tri_ref, df_ref, db_ref, carry):
        @pl.when(pl.program_id(0) == 0)
        def _():
            carry[...] = jnp.zeros_like(carry)
            db_ref[...] = jnp.zeros_like(db_ref)

        dcv = dc_ref[...]
        dlf = _dot_exact(tri_ref[...], dcv) + carry[0:1, :]
        carry[...] = jnp.broadcast_to(dlf[0:1, :], carry.shape)
        lane = lax.broadcasted_iota(jnp.int32, (tb, LANES), 1)
        z = p_ref[...] + b_ref[...]
        df = jnp.where(lane < n_heads, dlf / (1.0 + jnp.exp(z)), 0.0)
        df_ref[...] = df.astype(BF16)
        db_ref[...] += jnp.sum(df, axis=0, keepdims=True)

    return pl.pallas_call(
        body, name=name, grid=(nb,),
        in_specs=[pl.BlockSpec((tb, LANES), lambda i: (nb - 1 - i, gate_col)), pl.BlockSpec((1, LANES), lambda i: (0, 0)),
                  pl.BlockSpec((tb, LANES), lambda i: (nb - 1 - i, 0)), pl.BlockSpec((tb, tb), lambda i: (0, 0))],
        out_specs=[pl.BlockSpec((tb, LANES), lambda i: (nb - 1 - i, 0)), pl.BlockSpec((1, LANES), lambda i: (0, 0))],
        out_shape=[_sds((t, LANES), BF16), _sds((1, LANES), F32)],
        scratch_shapes=[pltpu.VMEM((8, LANES), F32)], compiler_params=_params(1),
    )(proj, b_pad, dc, triu)


def _qhead(qp, g):
    return _rms(qp, g) * (HEAD_DIM ** -0.5)


def _column(mat, idx):
    lane = lax.broadcasted_iota(jnp.int32, mat.shape, 1)
    return jnp.sum(jnp.where(lane == idx, mat, 0.0), axis=1, keepdims=True)


def _fox_scores(kk, qi, ckey, cq_i, i, bq):
    length = kk.shape[0]
    s = _dot_nt(kk, qi) + cq_i - ckey[:length]
    key = lax.broadcasted_iota(jnp.int32, (length, bq), 0)
    qry = lax.broadcasted_iota(jnp.int32, (length, bq), 1) + i * bq
    return jnp.where(key <= qry, s, NEG)


def _fox_fwd(name, proj, c, crow, gq, gk, n_heads):
    t = proj.shape[0]
    hw = n_heads * HEAD_DIM
    npair = n_heads // 2
    bq = min(256, t)
    nq = t // bq

    def body(q_ref, k_ref, v_ref, c_ref, crow_ref, gq_ref, gk_ref, o_ref, lse_ref):
        hp = pl.program_id(0)
        lse_ref[...] = jnp.zeros_like(lse_ref)
        outs = []
        for hh in range(2):
            sl = slice(hh * HEAD_DIM, (hh + 1) * HEAD_DIM)
            qn = _qhead(q_ref[:, sl], gq_ref[...]).astype(BF16)
            kn = _rms(k_ref[:, sl], gk_ref[...]).astype(BF16)
            v_t = v_ref[:, sl].T.astype(BF16)
            ckey = _column(c_ref[...], 2 * hp + hh)
            cq = crow_ref[0, hh:hh + 1, :]
            o_blocks = []
            for i in range(nq):
                cols = slice(i * bq, (i + 1) * bq)
                length = (i + 1) * bq
                s = _fox_scores(kn[:length], qn[cols], ckey, cq[:, cols], i, bq)
                m = jnp.max(s, axis=0, keepdims=True)
                p = jnp.exp(s - m)
                l = jnp.sum(p, axis=0, keepdims=True)
                o_blocks.append((_dot(v_t[:, :length], p.astype(BF16)) / l).T)
                lse_ref[0, hh:hh + 1, cols] = m + jnp.log(l)
            outs.append(jnp.concatenate(o_blocks, axis=0))
        o_ref[...] = jnp.concatenate(outs, axis=1).astype(BF16)

    col = lambda off: (lambda h: (0, off + h))
    fixed = lambda h: (0, 0)
    return pl.pallas_call(
        body, name=name, grid=(npair,),
        in_specs=[pl.BlockSpec((t, LANES), col(0)), pl.BlockSpec((t, LANES), col(npair)), pl.BlockSpec((t, LANES), col(2 * npair)),
                  pl.BlockSpec((t, LANES), fixed), pl.BlockSpec((1, 2, t), lambda h: (h, 0, 0)),
                  pl.BlockSpec((1, HEAD_DIM), fixed), pl.BlockSpec((1, HEAD_DIM), fixed)],
        out_specs=[pl.BlockSpec((t, LANES), col(0)), pl.BlockSpec((1, 8, t), lambda h: (h, 0, 0))],
        out_shape=[_sds((t, hw), BF16), _sds((npair, 8, t), F32)], compiler_params=_params(1),
    )(proj, proj, proj, c, crow, gq, gk)


def _fox_bwd(name, proj, c, crow, gq, gk, lse, do, n_heads):
    t = proj.shape[0]
    hw = n_heads * HEAD_DIM
    npair = n_heads // 2
    bq = min(256, t)
    nq = t // bq

    def body(q_ref, k_ref, v_ref, c_ref, crow_ref, gq_ref, gk_ref, lse_ref, do_ref,
             dq_ref, dk_ref, dv_ref, dc_ref, dgq_ref, dgk_ref, dk_acc, dv_acc, dc_acc):
        hp = pl.program_id(0)

        @pl.when(hp == 0)
        def _():
            dgq_ref[...] = jnp.zeros_like(dgq_ref)
            dgk_ref[...] = jnp.zeros_like(dgk_ref)

        lane = lax.broadcasted_iota(jnp.int32, (t, LANES), 1)
        dc_pair = jnp.zeros((t, LANES), F32)
        dqs, dks, dvs = [], [], []
        for hh in range(2):
            sl = slice(hh * HEAD_DIM, (hh + 1) * HEAD_DIM)
            qf, q_vjp = jax.vjp(_qhead, q_ref[:, sl], gq_ref[...])
            kf, k_vjp = jax.vjp(_rms, k_ref[:, sl], gk_ref[...])
            qn, kn, kn_t = qf.astype(BF16), kf.astype(BF16), kf.T.astype(BF16)
            vb = v_ref[:, sl].astype(BF16)
            dob = do_ref[:, sl]
            ckey = _column(c_ref[...], 2 * hp + hh)
            cq = crow_ref[0, hh:hh + 1, :]
            lse_h = lse_ref[0, hh:hh + 1, :]
            dk_acc[...] = jnp.zeros_like(dk_acc)
            dv_acc[...] = jnp.zeros_like(dv_acc)
            dc_acc[...] = jnp.zeros_like(dc_acc)
            dq_blocks = []
            for i in range(nq):
                cols = slice(i * bq, (i + 1) * bq)
                length = (i + 1) * bq
                qi, doi = qn[cols], dob[cols]
                s = _fox_scores(kn[:length], qi, ckey, cq[:, cols], i, bq)
                p = jnp.exp(s - lse_h[:, cols])
                dp = _dot_nt(vb[:length], doi)
                ds = p * (dp - jnp.sum(p * dp, axis=0, keepdims=True))
                dsb = ds.astype(BF16)
                dq_blocks.append(_dot(kn_t[:, :length], dsb).T)
                dk_acc[0:length, :] += _dot(dsb, qi)
                dv_acc[0:length, :] += _dot(p.astype(BF16), doi)
                part = ds[:, 0:LANES]
                for j in range(1, bq // LANES):
                    part = part + ds[:, j * LANES:(j + 1) * LANES]
                dc_acc[0:length, :] += part
            dqp, dgq = q_vjp(jnp.concatenate(dq_blocks, axis=0))
            dkp, dgk = k_vjp(dk_acc[...])
            dgq_ref[...] += dgq
            dgk_ref[...] += dgk
            dqs.append(dqp)
            dks.append(dkp)
            dvs.append(dv_acc[...])
            dc_pair = jnp.where(lane == hh, -jnp.sum(dc_acc[...], axis=1, keepdims=True), dc_pair)
        dq_ref[...] = jnp.concatenate(dqs, axis=1).astype(BF16)
        dk_ref[...] = jnp.concatenate(dks, axis=1).astype(BF16)
        dv_ref[...] = jnp.concatenate(dvs, axis=1).astype(BF16)
        dc_ref[...] = dc_pair

    col = lambda off: (lambda h: (0, off + h))
    fixed = lambda h: (0, 0)
    pair_blk = pl.BlockSpec((t, LANES), col(0))
    return pl.pallas_call(
        body, name=name, grid=(npair,),
        in_specs=[pl.BlockSpec((t, LANES), col(0)), pl.BlockSpec((t, LANES), col(npair)), pl.BlockSpec((t, LANES), col(2 * npair)),
                  pl.BlockSpec((t, LANES), fixed), pl.BlockSpec((1, 2, t), lambda h: (h, 0, 0)),
                  pl.BlockSpec((1, HEAD_DIM), fixed), pl.BlockSpec((1, HEAD_DIM), fixed),
                  pl.BlockSpec((1, 8, t), lambda h: (h, 0, 0)), pair_blk],
        out_specs=[pair_blk, pair_blk, pair_blk, pair_blk,
                   pl.BlockSpec((1, HEAD_DIM), fixed), pl.BlockSpec((1, HEAD_DIM), fixed)],
        out_shape=[_sds((t, hw), BF16), _sds((t, hw), BF16), _sds((t, hw), BF16), _sds((t, npair * LANES), F32),
                   _sds((1, HEAD_DIM), F32), _sds((1, HEAD_DIM), F32)],
        scratch_shapes=[pltpu.VMEM((t, HEAD_DIM), F32), pltpu.VMEM((t, HEAD_DIM), F32), pltpu.VMEM((t, LANES), F32)],
        compiler_params=_params(1),
    )(proj, proj, proj, c, crow, gq, gk, lse, do)


def _t5_bucket_table():
    dist = np.arange(WINDOW)[None, :] + WINDOW - np.arange(2 * WINDOW)[:, None]
    n = np.maximum(dist, 0)
    max_exact = N_BUCKETS // 2
    large = max_exact + (np.log(np.maximum(n, 1) / max_exact) / np.log(REL_MAX_DIST / max_exact)
                         * (N_BUCKETS - max_exact)).astype(np.int32)
    large = np.minimum(large, N_BUCKETS - 1)
    return np.where(n < max_exact, n, large).astype(np.int32).reshape(1, -1)


def _bias_expand(name, rel_bias_t):
    n_heads = rel_bias_t.shape[0]
    tbl = jnp.asarray(_t5_bucket_table())
    width = tbl.shape[1]

    def body(rb_ref, tbl_ref, o_ref):
        onehot = (lax.broadcasted_iota(jnp.int32, (N_BUCKETS, width), 0) == tbl_ref[...]).astype(F32)
        o_ref[...] = _dot_exact(rb_ref[...], onehot)

    return pl.pallas_call(body, name=name, out_shape=_sds((n_heads, width), F32), compiler_params=_params(0))(rel_bias_t, tbl)


def _bias_reduce(name, dbias):
    n_heads, width = dbias.shape
    tbl = jnp.asarray(_t5_bucket_table())

    def body(db_ref, tbl_ref, o_ref):
        onehot = (lax.broadcasted_iota(jnp.int32, (N_BUCKETS, width), 0) == tbl_ref[...]).astype(F32)
        o_ref[...] = lax.dot_general(db_ref[...], onehot, (((1,), (1,)), ((), ())), preferred_element_type=F32,
                                     precision=lax.Precision.HIGHEST)

    return pl.pallas_call(body, name=name, out_shape=_sds((n_heads, N_BUCKETS), F32), compiler_params=_params(0))(dbias, tbl)


def _swa_mask(n, group):
    j = lax.broadcasted_iota(jnp.int32, (2 * WINDOW, group * WINDOW), 0)
    i = lax.broadcasted_iota(jnp.int32, (2 * WINDOW, group * WINDOW), 1) & (WINDOW - 1)
    ok = (j > i) & (j <= i + WINDOW) & ((n > 0) | (j >= WINDOW))
    return jnp.where(ok, 0.0, NEG)


def _swa_stack(ref, start, group):
    return jnp.concatenate([ref[pl.ds(start, WINDOW), g * HEAD_DIM:(g + 1) * HEAD_DIM] for g in range(group)], axis=0)


def _swa_fwd(name, qb, kh, vh, gq, gk, sinks, bias, group):
    t = qb.shape[0]
    kvh = kh.shape[0]
    nblk = t // WINDOW
    gw = group * HEAD_DIM
    band = 2 * WINDOW
    cols = group * WINDOW

    def body(q_ref, k_ref, v_ref, gq_ref, gk_ref, sink_ref, bias_ref, o_ref, lse_ref, qs, kpad, vpad):
        for g in range(group):
            qs[:, g * HEAD_DIM:(g + 1) * HEAD_DIM] = _qhead(q_ref[:, g * HEAD_DIM:(g + 1) * HEAD_DIM], gq_ref[...]).astype(BF16)
        kpad[0:WINDOW, :] = jnp.zeros((WINDOW, HEAD_DIM), BF16)
        vpad[0:WINDOW, :] = jnp.zeros((WINDOW, HEAD_DIM), BF16)
        kpad[WINDOW:, :] = _rms(k_ref[0], gk_ref[...]).astype(BF16)
        vpad[WINDOW:, :] = v_ref[0].astype(BF16)
        sink = sink_ref[0]

        def block(n, carry):
            start = pl.multiple_of(n * WINDOW, WINDOW)
            kb = kpad[pl.ds(start, band), :]
            vb = vpad[pl.ds(start, band), :]
            s = _dot_nt(kb, _swa_stack(qs, start, group)) + bias_ref[0] + _swa_mask(n, group)
            m = jnp.maximum(jnp.max(s, axis=0, keepdims=True), sink)
            e = jnp.exp(s - m)
            l = jnp.sum(e, axis=0, keepdims=True) + jnp.exp(sink - m)
            o_t = _dot_tn(vb, e.astype(BF16)) / l
            for g in range(group):
                o_ref[pl.ds(start, WINDOW), g * HEAD_DIM:(g + 1) * HEAD_DIM] = o_t[:, g * WINDOW:(g + 1) * WINDOW].T.astype(BF16)
            lse_ref[pl.ds(n, 1), :] = m + jnp.log(l)
            return carry

        lax.fori_loop(0, nblk, block, 0)

    fixed = lambda h: (0, 0)
    per = lambda h: (h, 0, 0)
    return pl.pallas_call(
        body, name=name, grid=(kvh,),
        in_specs=[pl.BlockSpec((t, gw), lambda h: (0, h)), pl.BlockSpec((1, t, HEAD_DIM), per), pl.BlockSpec((1, t, HEAD_DIM), per),
                  pl.BlockSpec((1, HEAD_DIM), fixed), pl.BlockSpec((1, HEAD_DIM), fixed),
                  pl.BlockSpec((1, 1, cols), per), pl.BlockSpec((1, band, cols), per)],
        out_specs=[pl.BlockSpec((t, gw), lambda h: (0, h)), pl.BlockSpec((nblk, cols), lambda h: (h, 0))],
        out_shape=[_sds((t, kvh * gw), BF16), _sds((kvh * nblk, cols), F32)],
        scratch_shapes=[pltpu.VMEM((t, gw), BF16), pltpu.VMEM((t + WINDOW, HEAD_DIM), BF16),
                        pltpu.VMEM((t + WINDOW, HEAD_DIM), BF16)],
        compiler_params=_params(1),
    )(qb, kh, vh, gq, gk, sinks, bias)


def _swa_bwd(name, qb, kh, vh, gq, gk, sinks, bias, lse, do, group):
    t = qb.shape[0]
    kvh = kh.shape[0]
    nblk = t // WINDOW
    gw = group * HEAD_DIM
    band = 2 * WINDOW
    cols = group * WINDOW

    def body(q_ref, k_ref, v_ref, gq_ref, gk_ref, sink_ref, bias_ref, lse_ref, do_ref,
             dq_ref, dk_ref, dv_ref, dgq_ref, dgk_ref, dsink_ref, dbias_ref,
             qs, kpad, vpad, dqs, dk_acc, dv_acc, dsink_acc):
        @pl.when(pl.program_id(0) == 0)
        def _():
            dgq_ref[...] = jnp.zeros_like(dgq_ref)
            dgk_ref[...] = jnp.zeros_like(dgk_ref)

        for g in range(group):
            qs[:, g * HEAD_DIM:(g + 1) * HEAD_DIM] = _qhead(q_ref[:, g * HEAD_DIM:(g + 1) * HEAD_DIM], gq_ref[...]).astype(BF16)
        kpad[0:WINDOW, :] = jnp.zeros((WINDOW, HEAD_DIM), BF16)
        vpad[0:WINDOW, :] = jnp.zeros((WINDOW, HEAD_DIM), BF16)
        kpad[WINDOW:, :] = _rms(k_ref[0], gk_ref[...]).astype(BF16)
        vpad[WINDOW:, :] = v_ref[0].astype(BF16)
        dk_acc[...] = jnp.zeros_like(dk_acc)
        dv_acc[...] = jnp.zeros_like(dv_acc)
        dsink_acc[...] = jnp.zeros_like(dsink_acc)
        dbias_ref[...] = jnp.zeros_like(dbias_ref)
        sink = sink_ref[0]

        def block(n, carry):
            start = pl.multiple_of(n * WINDOW, WINDOW)
            kb = kpad[pl.ds(start, band), :]
            vb = vpad[pl.ds(start, band), :]
            q = _swa_stack(qs, start, group)
            dob = _swa_stack(do_ref, start, group)
            lse_n = lse_ref[pl.ds(n, 1), :]
            s = _dot_nt(kb, q) + bias_ref[0] + _swa_mask(n, group)
            p = jnp.exp(s - lse_n)
            dp = _dot_nt(vb, dob)
            dsum = jnp.sum(p * dp, axis=0, keepdims=True)
            ds = p * (dp - dsum)
            dsb = ds.astype(BF16)
            dsink_acc[...] -= jnp.exp(sink - lse_n) * dsum
            dbias_ref[0] += ds
            dq = _dot_tn(dsb, kb)
            for g in range(group):
                dqs[pl.ds(start, WINDOW), g * HEAD_DIM:(g + 1) * HEAD_DIM] = dq[g * WINDOW:(g + 1) * WINDOW]
            dk_acc[pl.ds(start, band), :] += _dot(dsb, q)
            dv_acc[pl.ds(start, band), :] += _dot(p.astype(BF16), dob)
            return carry

        lax.fori_loop(0, nblk, block, 0)
        for g in range(group):
            _, q_vjp = jax.vjp(_qhead, q_ref[:, g * HEAD_DIM:(g + 1) * HEAD_DIM], gq_ref[...])
            dqp, dgq = q_vjp(dqs[:, g * HEAD_DIM:(g + 1) * HEAD_DIM])
            dq_ref[:, g * HEAD_DIM:(g + 1) * HEAD_DIM] = dqp.astype(BF16)
            dgq_ref[...] += dgq
            dsink_g = jnp.sum(dsink_acc[:, g * WINDOW:(g + 1) * WINDOW], axis=1, keepdims=True)
            dsink_ref[0, g:g + 1, :] = jnp.broadcast_to(dsink_g, (1, LANES))
        _, k_vjp = jax.vjp(_rms, k_ref[0], gk_ref[...])
        dkp, dgk = k_vjp(dk_acc[WINDOW:, :])
        dk_ref[0] = dkp
        dgk_ref[...] += dgk
        dv_ref[0] = dv_acc[WINDOW:, :]

    fixed = lambda h: (0, 0)
    per = lambda h: (h, 0, 0)
    wide = pl.BlockSpec((t, gw), lambda h: (0, h))
    head = pl.BlockSpec((1, t, HEAD_DIM), per)
    vec = pl.BlockSpec((1, HEAD_DIM), fixed)
    bias_spec = pl.BlockSpec((1, band, cols), per)
    return pl.pallas_call(
        body, name=name, grid=(kvh,),
        in_specs=[wide, head, head, vec, vec, pl.BlockSpec((1, 1, cols), per), bias_spec,
                  pl.BlockSpec((nblk, cols), lambda h: (h, 0)), wide],
        out_specs=[wide, head, head, vec, vec, pl.BlockSpec((1, group, LANES), per), bias_spec],
        out_shape=[_sds((t, kvh * gw), BF16), _sds((kvh, t, HEAD_DIM), F32), _sds((kvh, t, HEAD_DIM), F32),
                   _sds((1, HEAD_DIM), F32), _sds((1, HEAD_DIM), F32),
                   _sds((kvh, group, LANES), F32), _sds((kvh, band, cols), F32)],
        scratch_shapes=[pltpu.VMEM((t, gw), BF16), pltpu.VMEM((t + WINDOW, HEAD_DIM), BF16),
                        pltpu.VMEM((t + WINDOW, HEAD_DIM), BF16), pltpu.VMEM((t, gw), F32),
                        pltpu.VMEM((t + WINDOW, HEAD_DIM), F32), pltpu.VMEM((t + WINDOW, HEAD_DIM), F32),
                        pltpu.VMEM((1, cols), F32)],
        compiler_params=_params(1),
    )(qb, kh, vh, gq, gk, sinks, bias, lse, do)


def _local_step(x, target, p, comm):
    t, d = x.shape
    n_heads = d // HEAD_DIM
    kv_heads = n_heads // 8
    group = n_heads // kv_heads
    hw = n_heads * HEAD_DIM
    gate_col = 3 * hw // LANES
    kvw = kv_heads * HEAD_DIM
    grads = {}

    def mlp_fwd(tag, h, g, layer):
        w_up, = comm.weights([f"w_up{layer}"], h)
        u, hn, a = _norm_matmul(f"{tag}_up", h, g, w_up, relu2=True)
        w_down, = comm.weights([f"w_down{layer}"], a)
        return _matmul_res(f"{tag}_down", a, w_down, h), (h, g, u, hn, a, w_up, w_down)

    def mlp_bwd(tag, saved, layer, dy):
        h, g, u, hn, a, w_up, w_down = saved
        du = _matmul_nt(f"{tag}_du", dy, w_down, u=u)
        dw_down = _matmul_tn(f"{tag}_dwdown", a, dy)
        dw_up = _matmul_tn(f"{tag}_dwup", hn, du, col_blocks=w_up.shape[0])
        zero = comm.send_grads(tag, {f"w_down{layer}": dw_down, f"w_up{layer}": dw_up})
        return _matmul_nt_rmsbwd(f"{tag}_dh", du, w_up, h, g + zero, dy)

    w_in, = comm.weights(["w_in_a"], None)
    proj, xn1 = _norm_matmul("a_inproj", x, p["g_attn"][0], w_in, tn=640)
    b_pad = jnp.pad(p["b_f"], ((0, 0), (0, LANES - n_heads)))
    c = _gate_fwd("a_gate", proj, b_pad, n_heads, gate_col)
    crow = c[:, :n_heads].T.reshape(n_heads // 2, 2, t)
    o_a, lse_a = _fox_fwd("a_attn", proj, c, crow, p["gq_a"], p["gk_a"], n_heads)
    w_out_a, = comm.weights(["w_out_a"], o_a)
    h1 = _matmul_res("a_outproj", o_a, w_out_a, x)
    h2, mlp0 = mlp_fwd("mlp0", h1, p["g_mlp"][0], 0)

    w_kv, w_q_b = comm.weights(["w_kv", "w_q_b"], h2)
    kv, hn_kv = _norm_matmul("kv_proj", h2, p["g_kv"], w_kv, tn=2 * kvw)
    kh = kv[:, :kvw].reshape(t, kv_heads, HEAD_DIM).transpose(1, 0, 2)
    vh = kv[:, kvw:].reshape(t, kv_heads, HEAD_DIM).transpose(1, 0, 2)
    qb, hn_q = _norm_matmul("b_qproj", h2, p["g_attn"][1], w_q_b, tn=512)
    gqb, gkb = p["gq_b"], p["gk_b"].reshape(1, HEAD_DIM)
    bias = _bias_expand("b_bias", p["rel_bias"].T).reshape(kv_heads, group, 2 * WINDOW, WINDOW)
    bias = bias.transpose(0, 2, 1, 3).reshape(kv_heads, 2 * WINDOW, group * WINDOW)
    sink_rows = jnp.broadcast_to(p["sinks"].reshape(kv_heads, 1, group, 1), (kv_heads, 1, group, WINDOW)).reshape(kv_heads, 1, group * WINDOW)
    o_b, lse_b = _swa_fwd("b_attn", qb, kh, vh, gqb, gkb, sink_rows, bias, group)
    w_out_b, = comm.weights(["w_out_b"], o_b)
    h3 = _matmul_res("b_outproj", o_b, w_out_b, h2)
    y, mlp1 = mlp_fwd("mlp1", h3, p["g_mlp"][1], 1)
    dy, loss_tile = _loss_grad("loss", y, target)

    dh3, dg_mlp1 = mlp_bwd("mlp1", mlp1, 1, dy)
    do_b = _matmul_nt("b_do", dh3, w_out_b)
    dw_out_b = _matmul_tn("b_dwout", o_b, dh3)
    dqb, dkh, dvh, grads["gq_b"], dgk_b, dsink, dbias = _swa_bwd(
        "b_attn_bwd", qb, kh, vh, gqb, gkb, sink_rows, bias, lse_b, do_b, group)
    grads["gk_b"] = dgk_b.reshape(HEAD_DIM)
    grads["sinks"] = dsink[:, :, 0].reshape(1, n_heads)
    dbias = dbias.reshape(kv_heads, 2 * WINDOW, group, WINDOW).transpose(0, 2, 1, 3)
    grads["rel_bias"] = _bias_reduce("b_dbias", dbias.reshape(n_heads, WINDOW * 2 * WINDOW)).T
    dw_q_b = _matmul_tn("b_dwq", hn_q, dqb)
    dh2, dg_attn1 = _matmul_nt_rmsbwd("b_dhq", dqb, w_q_b, h2, p["g_attn"][1], dh3)
    dkv = jnp.concatenate([dkh.transpose(1, 0, 2).reshape(t, kvw), dvh.transpose(1, 0, 2).reshape(t, kvw)], axis=1)
    dw_kv = _matmul_tn("kv_dw", hn_kv, dkv)
    zero = comm.send_grads("attn_b", {"w_out_b": dw_out_b, "w_q_b": dw_q_b, "w_kv": dw_kv})
    dh2, dg_kv = _matmul_nt_rmsbwd("kv_dh", dkv, w_kv, h2, p["g_kv"] + zero, dh2)
    grads["g_kv"] = dg_kv.reshape(d)
    dh1, dg_mlp0 = mlp_bwd("mlp0", mlp0, 0, dh2)
    grads["g_mlp"] = jnp.concatenate([dg_mlp0, dg_mlp1], axis=0)

    do_a = _matmul_nt("a_do", dh1, w_out_a)
    dw_out_a = _matmul_tn("a_dwout", o_a, dh1)
    zero = comm.send_grads("attn_a_out", {"w_out_a": dw_out_a})
    dq, dk, dv, dc_cols, grads["gq_a"], grads["gk_a"] = _fox_bwd(
        "a_attn_bwd", proj, c, crow, p["gq_a"] + zero, p["gk_a"], lse_a, do_a, n_heads)
    dc = jnp.pad(dc_cols.reshape(t, n_heads // 2, LANES)[:, :, :2].reshape(t, n_heads), ((0, 0), (0, LANES - n_heads)))
    dfl, db_f = _gate_bwd("a_gate_bwd", proj, b_pad, dc, n_heads, gate_col)
    grads["b_f"] = db_f[:, :n_heads]
    dproj = jnp.concatenate([dq, dk, dv, dfl], axis=1)
    dw_in = _matmul_tn("a_dwin", xn1, dproj, tn=640)
    zero = comm.send_grads("attn_a_in", {"w_in_a": dw_in})
    grad_x, dg_attn0 = _matmul_nt_rmsbwd("a_dx", dproj, w_in, x, p["g_attn"][0] + zero, dh1)
    grads["g_attn"] = jnp.concatenate([dg_attn0, dg_attn1], axis=0)
    return loss_tile, grad_x, grads


def _exchange(name, arrays, scatter, after):
    n = len(arrays)

    def body(*refs):
        src, out = refs[:n], refs[n + 1:2 * n + 1]
        send_sems, recv_sems, local_sems = refs[2 * n + 1:]
        x, y, c = lax.axis_index("x"), lax.axis_index("y"), lax.axis_index("c")
        me = 4 * x + 2 * y + c
        peers = []
        for k in range(1, N_DEV):
            px, py, pc = x ^ ((k >> 2) & 1), y ^ ((k >> 1) & 1), c ^ (k & 1)
            peers.append(((px, py, pc), 4 * px + 2 * py + pc))
        local, sends = [], []
        for a in range(n):
            mine = src[a].at[me] if scatter else src[a]
            cp = pltpu.make_async_copy(mine, out[a].at[me], local_sems.at[a])
            cp.start()
            local.append(cp)
        for k, (peer, peer_pos) in enumerate(peers):
            for a in range(n):
                cp = pltpu.make_async_remote_copy(
                    src_ref=src[a].at[peer_pos] if scatter else src[a], dst_ref=out[a].at[me],
                    send_sem=send_sems.at[a, k], recv_sem=recv_sems.at[a, k],
                    device_id=peer, device_id_type=pl.DeviceIdType.MESH)
                cp.start()
                sends.append(cp)
        for k, (peer, peer_pos) in enumerate(peers):
            for a in range(n):
                pltpu.make_async_remote_copy(
                    src_ref=out[a].at[peer_pos], dst_ref=out[a].at[peer_pos],
                    send_sem=send_sems.at[a, k], recv_sem=recv_sems.at[a, k],
                    device_id=peer, device_id_type=pl.DeviceIdType.MESH).wait_recv()
        for cp in sends:
            cp.wait_send()
        for cp in local:
            cp.wait()

    any_spec = pl.BlockSpec(memory_space=pl.ANY)
    out_shape = [_sds(a.shape if scatter else (N_DEV,) + a.shape, a.dtype) for a in arrays]
    return pl.pallas_call(
        body, name=name, in_specs=[any_spec] * (n + 1), out_specs=[any_spec] * n, out_shape=out_shape,
        scratch_shapes=[pltpu.SemaphoreType.DMA((n, N_DEV - 1)), pltpu.SemaphoreType.DMA((n, N_DEV - 1)),
                        pltpu.SemaphoreType.DMA((n,))],
    )(*arrays, after)


class _InFlight:
    def __init__(self, scatter, send_sems, recv_sems, srcs, lands, token):
        self.scatter, self.send_sems, self.recv_sems = scatter, send_sems, recv_sems
        self.srcs, self.lands, self.token = list(srcs), list(lands), token


def _mesh_peers():
    x, y, c = lax.axis_index("x"), lax.axis_index("y"), lax.axis_index("c")
    peers = []
    for k in range(1, N_DEV):
        px, py, pc = x ^ ((k >> 2) & 1), y ^ ((k >> 1) & 1), c ^ (k & 1)
        peers.append(((px, py, pc), 4 * px + 2 * py + pc))
    return 4 * x + 2 * y + c, peers


_HBM_SPEC = pl.BlockSpec(memory_space=pltpu.HBM)
_SEM_SPEC = pl.BlockSpec(memory_space=pltpu.SEMAPHORE)
_SIDE_EFFECT = pltpu.SideEffectType.DATAFLOW_SIDE_EFFECTING


def _exchange_start(name, arrays, scatter):
    n = len(arrays)
    me, _ = _mesh_peers()
    lands = []
    for a in arrays:
        own = lax.dynamic_index_in_dim(a, me, 0, keepdims=False) if scatter else a
        shape = a.shape if scatter else (N_DEV,) + a.shape
        lands.append(lax.dynamic_update_index_in_dim(lax.empty(shape, a.dtype), own, me, 0))

    def body(*refs):
        src, land = refs[:n], refs[n:2 * n]
        send_sems, recv_sems, token = refs[2 * n], refs[2 * n + 1], refs[-1]
        pos, peers = _mesh_peers()
        for a in range(n):
            for k, (peer, peer_pos) in enumerate(peers):
                pltpu.make_async_remote_copy(
                    src_ref=src[a].at[peer_pos] if scatter else src[a], dst_ref=land[a].at[pos],
                    send_sem=send_sems.at[a * (N_DEV - 1) + k], recv_sem=recv_sems.at[a * (N_DEV - 1) + k],
                    device_id=peer, device_id_type=pl.DeviceIdType.MESH).start()
        token[...] = jnp.zeros_like(token)

    operands = [pltpu.with_memory_space_constraint(a, pltpu.HBM) for a in list(arrays) + lands]
    outs = pl.pallas_call(
        body, name=name,
        out_shape=(pltpu.SemaphoreType.DMA((n * (N_DEV - 1),)), pltpu.SemaphoreType.DMA((n * (N_DEV - 1),)),
                   *[pltpu.HBM(a.shape, a.dtype) for a in operands], _sds((8, LANES), F32)),
        in_specs=[_HBM_SPEC] * (2 * n),
        out_specs=(_SEM_SPEC, _SEM_SPEC, *[_HBM_SPEC] * (2 * n), pl.BlockSpec(memory_space=pltpu.VMEM)),
        input_output_aliases={i: 2 + i for i in range(2 * n)},
        compiler_params=pltpu.CompilerParams(has_side_effects=_SIDE_EFFECT),
    )(*operands)
    return _InFlight(scatter, outs[0], outs[1], outs[2:2 + n], outs[2 + n:2 + 2 * n], outs[-1])


def _exchange_wait(name, flight, which, after):
    m = len(which)
    scatter = flight.scatter

    def body(*refs):
        src, land = refs[:m], refs[m:2 * m]
        send_sems, recv_sems = refs[2 * m], refs[2 * m + 1]
        _, peers = _mesh_peers()
        for i, a in enumerate(which):
            for k, (peer, peer_pos) in enumerate(peers):
                cp = pltpu.make_async_remote_copy(
                    src_ref=src[i].at[peer_pos] if scatter else src[i], dst_ref=land[i].at[peer_pos],
                    send_sem=send_sems.at[a * (N_DEV - 1) + k], recv_sem=recv_sems.at[a * (N_DEV - 1) + k],
                    device_id=peer, device_id_type=pl.DeviceIdType.MESH)
                cp.wait_send()
                cp.wait_recv()

    operands = [flight.srcs[a] for a in which] + [flight.lands[a] for a in which]
    outs = pl.pallas_call(
        body, name=name, out_shape=tuple(pltpu.HBM(a.shape, a.dtype) for a in operands),
        in_specs=[_HBM_SPEC] * (2 * m) + [_SEM_SPEC, _SEM_SPEC, pl.BlockSpec(memory_space=pl.ANY)],
        out_specs=tuple([_HBM_SPEC] * (2 * m)), input_output_aliases={i: i for i in range(2 * m)},
        compiler_params=pltpu.CompilerParams(has_side_effects=_SIDE_EFFECT),
    )(*operands, flight.send_sems, flight.recv_sems, after)
    return list(outs[m:])


def _adamw(name, parts, w, m, v, layer=None, into=None):
    r, c = w.shape[-2:]
    tr = r if r <= 256 else 256
    n_into = 0 if into is None else len(into)

    def body(p_ref, w_ref, m_ref, v_ref, *refs):
        g_ref, d_ref, mo_ref, vo_ref = refs[n_into:]
        g = p_ref[0].astype(F32)
        for dev in range(1, N_DEV):
            g = g + p_ref[dev].astype(F32)
        m_new = ADAM_B1 * m_ref[...] + (1.0 - ADAM_B1) * g
        v_new = ADAM_B2 * v_ref[...] + (1.0 - ADAM_B2) * jnp.square(g)
        m_hat = m_new / (1.0 - ADAM_B1 ** ADAM_STEP)
        v_hat = v_new / (1.0 - ADAM_B2 ** ADAM_STEP)
        g_ref[...] = g
        d_ref[...] = -ADAM_LR * (m_hat / (jnp.sqrt(v_hat) + ADAM_EPS) + ADAM_WD * w_ref[...])
        mo_ref[...] = m_new
        vo_ref[...] = v_new

    if layer is None:
        blk = pl.BlockSpec((tr, c), lambda i: (i, 0))
    else:
        blk = pl.BlockSpec((None, tr, c), lambda i: (layer, i, 0))
    return pl.pallas_call(
        body, name=name, grid=(r // tr,),
        in_specs=[pl.BlockSpec((N_DEV, tr, c), lambda i: (0, i, 0)), blk, blk, blk] + [pl.BlockSpec(memory_space=pl.ANY)] * n_into,
        out_specs=[blk] * 4, out_shape=[_sds(w.shape, F32)] * 4,
        input_output_aliases={4 + i: i for i in range(n_into)}, compiler_params=_params(1),
    )(parts, w, m, v, *(into or ()))


def _pack_small(tree, last):
    flat = jnp.concatenate([tree[k].reshape(-1) for k in SMALL] + [last])
    size = -(-flat.shape[0] // (8 * LANES)) * (8 * LANES)
    return jnp.pad(flat, (0, size - flat.shape[0])).reshape(-1, LANES)


def _unpack_small(packed, like):
    flat, out, off = packed.reshape(-1), {}, 0
    for k in SMALL:
        size = like[k].size
        out[k] = flat[off:off + size].reshape(like[k].shape)
        off += size
    return out


class _Comm:
    ORDER = ("w_in_a", "w_out_a", "w_up0", "w_down0", "w_kv", "w_q_b", "w_out_b", "w_up1", "w_down1")

    def __init__(self, shards, d, n_in):
        self.d, self.n_in = d, n_in
        self.flight = _exchange_start("gather_start", [shards[n].astype(BF16) for n in self.ORDER], scatter=False)
        self.sent = []

    def weights(self, names, after):
        which = [self.ORDER.index(n) for n in names]
        landed = _exchange_wait(f"gather_wait_{names[0]}", self.flight, which, self.flight.token if after is None else after)
        return [self._whole(n, g) for n, g in zip(names, landed)]

    def _whole(self, name, g):
        if name == "w_in_a":
            pad = -(-self.n_in // LANES) * LANES - self.n_in
            return jnp.pad(g.transpose(1, 0, 2).reshape(self.d, self.n_in), ((0, 0), (0, pad)))
        if name.startswith("w_up"):
            return g
        return g.reshape(-1, g.shape[-1])

    def _chunks(self, name, g):
        if name == "w_in_a":
            return g[:, :self.n_in].reshape(self.d, N_DEV, -1).transpose(1, 0, 2)
        if name.startswith("w_up"):
            return g
        return g.reshape(N_DEV, g.shape[0] // N_DEV, g.shape[1])

    def send_grads(self, tag, partials):
        names = list(partials)
        flight = _exchange_start(f"scatter_start_{tag}", [self._chunks(n, partials[n]) for n in names], scatter=True)
        self.sent.append((tag, flight, names))
        return flight.token[0, 0]

    def received(self, index, after):
        tag, flight, names = self.sent[index]
        landed = _exchange_wait(f"scatter_wait_{tag}", flight, list(range(len(names))), after)
        return dict(zip(names, landed))


def kernel(x, g_attn, g_mlp, w_in_a, b_f, gq_a, gk_a, w_out_a, g_kv, w_kv, gk_b, w_q_b, gq_b, sinks, rel_bias, w_out_b, w_up, w_down, loss_target, m_g_attn, m_g_mlp, m_w_in_a, m_b_f, m_gq_a, m_gk_a, m_w_out_a, m_g_kv, m_w_kv, m_gk_b, m_w_q_b, m_gq_b, m_sinks, m_rel_bias, m_w_out_b, m_w_up, m_w_down, v_g_attn, v_g_mlp, v_w_in_a, v_b_f, v_gq_a, v_gk_a, v_w_out_a, v_g_kv, v_w_kv, v_gk_b, v_w_q_b, v_gq_b, v_sinks, v_rel_bias, v_w_out_b, v_w_up, v_w_down):
    w = dict(g_attn=g_attn, g_mlp=g_mlp, w_in_a=w_in_a, b_f=b_f, gq_a=gq_a, gk_a=gk_a, w_out_a=w_out_a, g_kv=g_kv,
             w_kv=w_kv, gk_b=gk_b, w_q_b=w_q_b, gq_b=gq_b, sinks=sinks, rel_bias=rel_bias, w_out_b=w_out_b,
             w_up=w_up, w_down=w_down)
    mom = dict(g_attn=m_g_attn, g_mlp=m_g_mlp, w_in_a=m_w_in_a, b_f=m_b_f, gq_a=m_gq_a, gk_a=m_gk_a, w_out_a=m_w_out_a,
               g_kv=m_g_kv, w_kv=m_w_kv, gk_b=m_gk_b, w_q_b=m_w_q_b, gq_b=m_gq_b, sinks=m_sinks, rel_bias=m_rel_bias,
               w_out_b=m_w_out_b, w_up=m_w_up, w_down=m_w_down)
    var = dict(g_attn=v_g_attn, g_mlp=v_g_mlp, w_in_a=v_w_in_a, b_f=v_b_f, gq_a=v_gq_a, gk_a=v_gk_a, w_out_a=v_w_out_a,
               g_kv=v_g_kv, w_kv=v_w_kv, gk_b=v_gk_b, w_q_b=v_w_q_b, gq_b=v_gq_b, sinks=v_sinks, rel_bias=v_rel_bias,
               w_out_b=v_w_out_b, w_up=v_w_up, w_down=v_w_down)
    d = x.shape[2]
    where = {"w_in_a": ("w_in_a", 0), "w_out_a": ("w_out_a", 0), "w_kv": ("w_kv", None), "w_q_b": ("w_q_b", 0),
             "w_out_b": ("w_out_b", 0), "w_up0": ("w_up", 0), "w_up1": ("w_up", 1), "w_down0": ("w_down", 0),
             "w_down1": ("w_down", 1)}
    shards = {n: (w[k] if layer is None else w[k][layer]) for n, (k, layer) in where.items()}
    comm = _Comm(shards, d, w_in_a.shape[2] * N_DEV)
    loss_tile, grad_x, grads = _local_step(x[0], loss_target[0], {k: w[k] for k in SMALL}, comm)

    res, after = {}, grad_x
    last = len(comm.sent) - 1
    n_small = sum(w[k].size for k in SMALL)
    zero = jnp.zeros((1,), F32)
    for index in range(last + 1):
        if index == last:
            small_parts, = _exchange("gather_small_grads", [_pack_small(grads, loss_tile[0, 0:1])], scatter=False, after=after)
            small = _adamw("adam_small", small_parts, _pack_small(w, zero), _pack_small(mom, zero), _pack_small(var, zero))
            loss = small[0].reshape(-1)[n_small]
            after = small[0]
        for n, parts in comm.received(index, after).items():
            k, layer = where[n]
            res[k] = _adamw(f"adam_{n}", parts, w[k], mom[k], var[k], layer, res.get(k))
            after = res[k][0]
    small = [_unpack_small(s, w) for s in small]
    for k in SMALL:
        res[k] = [s[k] for s in small]

    outs = [loss, grad_x[None]]
    for i in range(4):
        outs.extend(res[k][i] for k in WEIGHTS)
    return tuple(outs)
```

```python
import functools

import numpy as np
import jax
import jax.numpy as jnp
from jax import lax
from jax.experimental import pallas as pl
from jax.experimental.pallas import tpu as pltpu

F32 = jnp.float32
BF16 = jnp.bfloat16

N_DEV = 8
HEAD_DIM = 64
WINDOW = 128
N_BUCKETS = 32
REL_MAX_DIST = 128
NORM_EPS = 1e-6
NEG = -1e30
LANES = 128
VMEM_LIMIT = 56 * 1024 * 1024

ADAM_LR = 0.001
ADAM_B1 = 0.9
ADAM_B2 = 0.999
ADAM_EPS = 1e-08
ADAM_WD = 0.01
ADAM_STEP = 10

SMALL = ("g_attn", "g_mlp", "b_f", "gq_a", "gk_a", "g_kv", "gk_b", "gq_b", "sinks", "rel_bias")
BIG = ("w_in_a", "w_out_a", "w_kv", "w_q_b", "w_out_b", "w_up", "w_down")
WEIGHTS = ("g_attn", "g_mlp", "w_in_a", "b_f", "gq_a", "gk_a", "w_out_a", "g_kv", "w_kv", "gk_b",
           "w_q_b", "gq_b", "sinks", "rel_bias", "w_out_b", "w_up", "w_down")


def _params(n_grid):
    return pltpu.CompilerParams(dimension_semantics=("arbitrary",) * n_grid, vmem_limit_bytes=VMEM_LIMIT)


def _sds(shape, dtype):
    return jax.ShapeDtypeStruct(tuple(shape), dtype)


def _rms(x, g):
    return (x * lax.rsqrt(jnp.mean(x * x, axis=-1, keepdims=True) + NORM_EPS)) * g


def _dot_nt(a, b):
    return lax.dot_general(a, b, (((1,), (1,)), ((), ())), preferred_element_type=F32)


def _dot_tn(a, b):
    return lax.dot_general(a, b, (((0,), (0,)), ((), ())), preferred_element_type=F32)


def _dot(a, b):
    return jnp.dot(a, b, preferred_element_type=F32)


def _dot_exact(a, b):
    return jnp.dot(a, b, preferred_element_type=F32, precision=lax.Precision.HIGHEST)


def _norm_matmul(name, x, g, w, *, tn=None, relu2=False):
    t, d = x.shape
    blocked = w.ndim == 3
    if blocked:
        tn = w.shape[2]
        n = w.shape[0] * tn
        w_spec = pl.BlockSpec((None, d, tn), lambda i, j: (j, 0, 0))
    else:
        n = w.shape[1]
        w_spec = pl.BlockSpec((d, tn), lambda i, j: (0, j))
    tm = min(1024, t)

    def body(x_ref, g_ref, w_ref, y_ref, xn_ref):
        @pl.when(pl.program_id(1) == 0)
        def _():
            xn_ref[...] = _rms(x_ref[...], g_ref[...]).astype(BF16)

        y = _dot(xn_ref[...], w_ref[...])
        y_ref[...] = jnp.square(jnp.maximum(y, 0.0)).astype(BF16) if relu2 else y

    out_shape = [_sds((t, n), BF16 if relu2 else F32), _sds((t, d), BF16)]
    out_specs = [pl.BlockSpec((tm, tn), lambda i, j: (i, j)), pl.BlockSpec((tm, d), lambda i, j: (i, 0))]
    return pl.pallas_call(
        body, name=name, grid=(t // tm, n // tn),
        in_specs=[pl.BlockSpec((tm, d), lambda i, j: (i, 0)), pl.BlockSpec((1, d), lambda i, j: (0, 0)), w_spec],
        out_specs=out_specs, out_shape=out_shape, compiler_params=_params(2),
    )(x, g.reshape(1, d), w)


def _matmul_res(name, a, w, res, *, tn=512):
    t, k = a.shape
    n = w.shape[1]
    tm = min(1024, t)

    def body(a_ref, w_ref, r_ref, o_ref):
        o_ref[...] = r_ref[...] + _dot(a_ref[...], w_ref[...])

    return pl.pallas_call(
        body, name=name, grid=(t // tm, n // tn),
        in_specs=[pl.BlockSpec((tm, k), lambda i, j: (i, 0)), pl.BlockSpec((k, tn), lambda i, j: (0, j)),
                  pl.BlockSpec((tm, tn), lambda i, j: (i, j))],
        out_specs=pl.BlockSpec((tm, tn), lambda i, j: (i, j)), out_shape=_sds((t, n), F32),
        compiler_params=_params(2),
    )(a, w, res)


def _matmul_nt(name, dy, w, *, a=None, tk=1024):
    t, n = dy.shape
    k = w.shape[0]
    tm = min(1024, t)

    def body(dy_ref, w_ref, *rest):
        o_ref = rest[-1]
        r = _dot_nt(dy_ref[...].astype(BF16), w_ref[...])
        if a is not None:
            r = r * (2.0 * jnp.sqrt(rest[0][...].astype(F32)))
        o_ref[...] = r.astype(BF16)

    in_specs = [pl.BlockSpec((tm, n), lambda i, j: (i, 0)), pl.BlockSpec((tk, n), lambda i, j: (j, 0))]
    args = [dy, w]
    if a is not None:
        in_specs.append(pl.BlockSpec((tm, tk), lambda i, j: (i, j)))
        args.append(a)
    return pl.pallas_call(
        body, name=name, grid=(t // tm, k // tk), in_specs=in_specs,
        out_specs=pl.BlockSpec((tm, tk), lambda i, j: (i, j)), out_shape=_sds((t, k), BF16),
        compiler_params=_params(2),
    )(*args)


def _matmul_nt_rmsbwd(name, dy, w, x, g, dres):
    t, k = dy.shape
    blocked = w.ndim == 3
    d = w.shape[1] if blocked else w.shape[0]
    tm = min(512, t)

    def body(dy_ref, w_ref, x_ref, g_ref, r_ref, dx_ref, dg_ref):
        if blocked:
            kb = w.shape[2]
            dxn = _dot_nt(dy_ref[:, 0:kb].astype(BF16), w_ref[0])
            for j in range(1, w.shape[0]):
                dxn += _dot_nt(dy_ref[:, j * kb:(j + 1) * kb].astype(BF16), w_ref[j])
        else:
            dxn = _dot_nt(dy_ref[...].astype(BF16), w_ref[...])
        _, vjp = jax.vjp(_rms, x_ref[...], g_ref[...])
        dx, dg = vjp(dxn)
        dx_ref[...] = r_ref[...] + dx

        @pl.when(pl.program_id(0) == 0)
        def _():
            dg_ref[...] = jnp.zeros_like(dg_ref)

        dg_ref[...] += dg

    row = lambda i: (i, 0)
    fixed = lambda i: (0, 0)
    return pl.pallas_call(
        body, name=name, grid=(t // tm,),
        in_specs=[pl.BlockSpec((tm, k), row), pl.BlockSpec(w.shape, (lambda i: (0, 0, 0)) if blocked else fixed),
                  pl.BlockSpec((tm, d), row), pl.BlockSpec((1, d), fixed), pl.BlockSpec((tm, d), row)],
        out_specs=[pl.BlockSpec((tm, d), row), pl.BlockSpec((1, d), fixed)],
        out_shape=[_sds((t, d), F32), _sds((1, d), F32)], compiler_params=_params(1),
    )(dy, w, x, g.reshape(1, d), dres)


def _matmul_tn(name, a, b, *, tk=1024, tn=1024, col_blocks=None):
    t, k = a.shape
    n = b.shape[1]
    tk = min(tk, k)
    if col_blocks:
        tn = n // col_blocks
        out_spec, out_shape = pl.BlockSpec((None, tk, tn), lambda i, j: (j, i, 0)), _sds((col_blocks, k, tn), BF16)
    else:
        tn = min(tn, n)
        out_spec, out_shape = pl.BlockSpec((tk, tn), lambda i, j: (i, j)), _sds((k, n), BF16)

    def body(a_ref, b_ref, o_ref):
        o_ref[...] = _dot_tn(a_ref[...].astype(BF16), b_ref[...].astype(BF16)).astype(BF16)

    return pl.pallas_call(
        body, name=name, grid=(k // tk, n // tn),
        in_specs=[pl.BlockSpec((t, tk), lambda i, j: (0, i)), pl.BlockSpec((t, tn), lambda i, j: (0, j))],
        out_specs=out_spec, out_shape=out_shape, compiler_params=_params(2),
    )(a, b)


def _loss_grad(name, y, target):
    t, d = y.shape
    tm = min(512, t)

    def body(y_ref, t_ref, dy_ref, l_ref):
        e = y_ref[...] - t_ref[...]
        dy_ref[...] = e * (1.0 / d)

        @pl.when(pl.program_id(0) == 0)
        def _():
            l_ref[...] = jnp.zeros_like(l_ref)

        l_ref[...] += 0.5 * jnp.sum(jnp.mean(e * e, axis=-1))

    row = lambda i: (i, 0)
    return pl.pallas_call(
        body, name=name, grid=(t // tm,), in_specs=[pl.BlockSpec((tm, d), row), pl.BlockSpec((tm, d), row)],
        out_specs=[pl.BlockSpec((tm, d), row), pl.BlockSpec((8, LANES), lambda i: (0, 0))],
        out_shape=[_sds((t, d), F32), _sds((8, LANES), F32)], compiler_params=_params(1),
    )(y, target)


def _gate_fwd(name, proj, b_pad, n_heads, gate_col):
    t = proj.shape[0]
    tb = min(256, t)
    tri = jnp.asarray(np.tril(np.ones((tb, tb), np.float32)))

    def body(p_ref, b_ref, tri_ref, c_ref, carry):
        @pl.when(pl.program_id(0) == 0)
        def _():
            carry[...] = jnp.zeros_like(carry)

        lane = lax.broadcasted_iota(jnp.int32, (tb, LANES), 1)
        lf = jnp.where(lane < n_heads, jax.nn.log_sigmoid(p_ref[...] + b_ref[...]), 0.0)
        c = _dot_exact(tri_ref[...], lf) + carry[0:1, :]
        c_ref[...] = c
        carry[...] = jnp.broadcast_to(c[tb - 1:tb, :], carry.shape)

    return pl.pallas_call(
        body, name=name, grid=(t // tb,),
        in_specs=[pl.BlockSpec((tb, LANES), lambda i: (i, gate_col)), pl.BlockSpec((1, LANES), lambda i: (0, 0)),
                  pl.BlockSpec((tb, tb), lambda i: (0, 0))],
        out_specs=pl.BlockSpec((tb, LANES), lambda i: (i, 0)), out_shape=_sds((t, LANES), F32),
        scratch_shapes=[pltpu.VMEM((8, LANES), F32)], compiler_params=_params(1),
    )(proj, b_pad, tri)


def _gate_bwd(name, proj, b_pad, dc, n_heads, gate_col):
    t = proj.shape[0]
    tb = min(256, t)
    nb = t // tb
    triu = jnp.asarray(np.triu(np.ones((tb, tb), np.float32)))

    def body(p_ref, b_ref, dc_ref, tri_ref, df_ref, db_ref, carry):
        @pl.when(pl.program_id(0) == 0)
        def _():
            carry[...] = jnp.zeros_like(carry)
            db_ref[...] = jnp.zeros_like(db_ref)

        dcv = dc_ref[...]
        dlf = _dot_exact(tri_ref[...], dcv) + carry[0:1, :]
        carry[...] = jnp.broadcast_to(dlf[0:1, :], carry.shape)
        lane = lax.broadcasted_iota(jnp.int32, (tb, LANES), 1)
        z = p_ref[...] + b_ref[...]
        df = jnp.where(lane < n_heads, dlf / (1.0 + jnp.exp(z)), 0.0)
        df_ref[...] = df.astype(BF16)
        db_ref[...] += jnp.sum(df, axis=0, keepdims=True)

    return pl.pallas_call(
        body, name=name, grid=(nb,),
        in_specs=[pl.BlockSpec((tb, LANES), lambda i: (nb - 1 - i, gate_col)), pl.BlockSpec((1, LANES), lambda i: (0, 0)),
                  pl.BlockSpec((tb, LANES), lambda i: (nb - 1 - i, 0)), pl.BlockSpec((tb, tb), lambda i: (0, 0))],
        out_specs=[pl.BlockSpec((tb, LANES), lambda i: (nb - 1 - i, 0)), pl.BlockSpec((1, LANES), lambda i: (0, 0))],
        out_shape=[_sds((t, LANES), BF16), _sds((1, LANES), F32)],
        scratch_shapes=[pltpu.VMEM((8, LANES), F32)], compiler_params=_params(1),
    )(proj, b_pad, dc, triu)


def _qhead(qp, g):
    return _rms(qp, g) * (HEAD_DIM ** -0.5)


def _column(mat, idx):
    lane = lax.broadcasted_iota(jnp.int32, mat.shape, 1)
    return jnp.sum(jnp.where(lane == idx, mat, 0.0), axis=1, keepdims=True)


def _fox_scores(kk, qi, ckey, cq_i, i, bq):
    length = kk.shape[0]
    s = _dot_nt(kk, qi) + cq_i - ckey[:length]
    key = lax.broadcasted_iota(jnp.int32, (length, bq), 0)
    qry = lax.broadcasted_iota(jnp.int32, (length, bq), 1) + i * bq
    return jnp.where(key <= qry, s, NEG)


def _fox_fwd(name, proj, c, crow, gq, gk, n_heads):
    t = proj.shape[0]
    hw = n_heads * HEAD_DIM
    npair = n_heads // 2
    bq = min(256, t)
    nq = t // bq

    def body(q_ref, k_ref, v_ref, c_ref, crow_ref, gq_ref, gk_ref, o_ref, lse_ref):
        hp = pl.program_id(0)
        lse_ref[...] = jnp.zeros_like(lse_ref)
        outs = []
        for hh in range(2):
            sl = slice(hh * HEAD_DIM, (hh + 1) * HEAD_DIM)
            qn = _qhead(q_ref[:, sl], gq_ref[...]).astype(BF16)
            kn = _rms(k_ref[:, sl], gk_ref[...]).astype(BF16)
            v_t = v_ref[:, sl].T.astype(BF16)
            ckey = _column(c_ref[...], 2 * hp + hh)
            cq = crow_ref[0, hh:hh + 1, :]
            o_blocks = []
            for i in range(nq):
                cols = slice(i * bq, (i + 1) * bq)
                length = (i + 1) * bq
                s = _fox_scores(kn[:length], qn[cols], ckey, cq[:, cols], i, bq)
                m = jnp.max(s, axis=0, keepdims=True)
                p = jnp.exp(s - m)
                l = jnp.sum(p, axis=0, keepdims=True)
                o_blocks.append((_dot(v_t[:, :length], p.astype(BF16)) / l).T)
                lse_ref[0, hh:hh + 1, cols] = m + jnp.log(l)
            outs.append(jnp.concatenate(o_blocks, axis=0))
        o_ref[...] = jnp.concatenate(outs, axis=1).astype(BF16)

    col = lambda off: (lambda h: (0, off + h))
    fixed = lambda h: (0, 0)
    return pl.pallas_call(
        body, name=name, grid=(npair,),
        in_specs=[pl.BlockSpec((t, LANES), col(0)), pl.BlockSpec((t, LANES), col(npair)), pl.BlockSpec((t, LANES), col(2 * npair)),
                  pl.BlockSpec((t, LANES), fixed), pl.BlockSpec((1, 2, t), lambda h: (h, 0, 0)),
                  pl.BlockSpec((1, HEAD_DIM), fixed), pl.BlockSpec((1, HEAD_DIM), fixed)],
        out_specs=[pl.BlockSpec((t, LANES), col(0)), pl.BlockSpec((1, 8, t), lambda h: (h, 0, 0))],
        out_shape=[_sds((t, hw), BF16), _sds((npair, 8, t), F32)], compiler_params=_params(1),
    )(proj, proj, proj, c, crow, gq, gk)


def _fox_bwd(name, proj, c, crow, gq, gk, lse, do, n_heads):
    t = proj.shape[0]
    hw = n_heads * HEAD_DIM
    npair = n_heads // 2
    bq = min(256, t)
    nq = t // bq

    def body(q_ref, k_ref, v_ref, c_ref, crow_ref, gq_ref, gk_ref, lse_ref, do_ref,
             dq_ref, dk_ref, dv_ref, dc_ref, dgq_ref, dgk_ref, dk_acc, dv_acc, dc_acc):
        hp = pl.program_id(0)

        @pl.when(hp == 0)
        def _():
            dgq_ref[...] = jnp.zeros_like(dgq_ref)
            dgk_ref[...] = jnp.zeros_like(dgk_ref)

        lane = lax.broadcasted_iota(jnp.int32, (t, LANES), 1)
        dc_pair = jnp.zeros((t, LANES), F32)
        dqs, dks, dvs = [], [], []
        for hh in range(2):
            sl = slice(hh * HEAD_DIM, (hh + 1) * HEAD_DIM)
            qf, q_vjp = jax.vjp(_qhead, q_ref[:, sl], gq_ref[...])
            kf, k_vjp = jax.vjp(_rms, k_ref[:, sl], gk_ref[...])
            qn, kn, kn_t = qf.astype(BF16), kf.astype(BF16), kf.T.astype(BF16)
            vb = v_ref[:, sl].astype(BF16)
            dob = do_ref[:, sl]
            ckey = _column(c_ref[...], 2 * hp + hh)
            cq = crow_ref[0, hh:hh + 1, :]
            lse_h = lse_ref[0, hh:hh + 1, :]
            dk_acc[...] = jnp.zeros_like(dk_acc)
            dv_acc[...] = jnp.zeros_like(dv_acc)
            dc_acc[...] = jnp.zeros_like(dc_acc)
            dq_blocks = []
            for i in range(nq):
                cols = slice(i * bq, (i + 1) * bq)
                length = (i + 1) * bq
                qi, doi = qn[cols], dob[cols]
                s = _fox_scores(kn[:length], qi, ckey, cq[:, cols], i, bq)
                p = jnp.exp(s - lse_h[:, cols])
                dp = _dot_nt(vb[:length], doi)
                ds = p * (dp - jnp.sum(p * dp, axis=0, keepdims=True))
                dsb = ds.astype(BF16)
                dq_blocks.append(_dot(kn_t[:, :length], dsb).T)
                dk_acc[0:length, :] += _dot(dsb, qi)
                dv_acc[0:length, :] += _dot(p.astype(BF16), doi)
                part = ds[:, 0:LANES]
                for j in range(1, bq // LANES):
                    part = part + ds[:, j * LANES:(j + 1) * LANES]
                dc_acc[0:length, :] += part
            dqp, dgq = q_vjp(jnp.concatenate(dq_blocks, axis=0))
            dkp, dgk = k_vjp(dk_acc[...])
            dgq_ref[...] += dgq
            dgk_ref[...] += dgk
            dqs.append(dqp)
            dks.append(dkp)
            dvs.append(dv_acc[...])
            dc_pair = jnp.where(lane == hh, -jnp.sum(dc_acc[...], axis=1, keepdims=True), dc_pair)
        dq_ref[...] = jnp.concatenate(dqs, axis=1).astype(BF16)
        dk_ref[...] = jnp.concatenate(dks, axis=1).astype(BF16)
        dv_ref[...] = jnp.concatenate(dvs, axis=1).astype(BF16)
        dc_ref[...] = dc_pair

    col = lambda off: (lambda h: (0, off + h))
    fixed = lambda h: (0, 0)
    pair_blk = pl.BlockSpec((t, LANES), col(0))
    return pl.pallas_call(
        body, name=name, grid=(npair,),
        in_specs=[pl.BlockSpec((t, LANES), col(0)), pl.BlockSpec((t, LANES), col(npair)), pl.BlockSpec((t, LANES), col(2 * npair)),
                  pl.BlockSpec((t, LANES), fixed), pl.BlockSpec((1, 2, t), lambda h: (h, 0, 0)),
                  pl.BlockSpec((1, HEAD_DIM), fixed), pl.BlockSpec((1, HEAD_DIM), fixed),
                  pl.BlockSpec((1, 8, t), lambda h: (h, 0, 0)), pair_blk],
        out_specs=[pair_blk, pair_blk, pair_blk, pair_blk,
                   pl.BlockSpec((1, HEAD_DIM), fixed), pl.BlockSpec((1, HEAD_DIM), fixed)],
        out_shape=[_sds((t, hw), BF16), _sds((t, hw), BF16), _sds((t, hw), BF16), _sds((t, npair * LANES), F32),
                   _sds((1, HEAD_DIM), F32), _sds((1, HEAD_DIM), F32)],
        scratch_shapes=[pltpu.VMEM((t, HEAD_DIM), F32), pltpu.VMEM((t, HEAD_DIM), F32), pltpu.VMEM((t, LANES), F32)],
        compiler_params=_params(1),
    )(proj, proj, proj, c, crow, gq, gk, lse, do)


def _t5_bucket_table():
    dist = np.arange(WINDOW)[None, :] + WINDOW - np.arange(2 * WINDOW)[:, None]
    n = np.maximum(dist, 0)
    max_exact = N_BUCKETS // 2
    large = max_exact + (np.log(np.maximum(n, 1) / max_exact) / np.log(REL_MAX_DIST / max_exact)
                         * (N_BUCKETS - max_exact)).astype(np.int32)
    large = np.minimum(large, N_BUCKETS - 1)
    return np.where(n < max_exact, n, large).astype(np.int32).reshape(1, -1)


def _bias_expand(name, rel_bias_t):
    n_heads = rel_bias_t.shape[0]
    tbl = jnp.asarray(_t5_bucket_table())
    width = tbl.shape[1]

    def body(rb_ref, tbl_ref, o_ref):
        onehot = (lax.broadcasted_iota(jnp.int32, (N_BUCKETS, width), 0) == tbl_ref[...]).astype(F32)
        o_ref[...] = _dot_exact(rb_ref[...], onehot)

    return pl.pallas_call(body, name=name, out_shape=_sds((n_heads, width), F32), compiler_params=_params(0))(rel_bias_t, tbl)


def _bias_reduce(name, dbias):
    n_heads, width = dbias.shape
    tbl = jnp.asarray(_t5_bucket_table())

    def body(db_ref, tbl_ref, o_ref):
        onehot = (lax.broadcasted_iota(jnp.int32, (N_BUCKETS, width), 0) == tbl_ref[...]).astype(F32)
        o_ref[...] = lax.dot_general(db_ref[...], onehot, (((1,), (1,)), ((), ())), preferred_element_type=F32,
                                     precision=lax.Precision.HIGHEST)

    return pl.pallas_call(body, name=name, out_shape=_sds((n_heads, N_BUCKETS), F32), compiler_params=_params(0))(dbias, tbl)


def _swa_mask(n, group):
    j = lax.broadcasted_iota(jnp.int32, (2 * WINDOW, group * WINDOW), 0)
    i = lax.broadcasted_iota(jnp.int32, (2 * WINDOW, group * WINDOW), 1) & (WINDOW - 1)
    ok = (j > i) & (j <= i + WINDOW) & ((n > 0) | (j >= WINDOW))
    return jnp.where(ok, 0.0, NEG)


def _swa_stack(ref, start, group):
    return jnp.concatenate([ref[pl.ds(start, WINDOW), g * HEAD_DIM:(g + 1) * HEAD_DIM] for g in range(group)], axis=0)


def _swa_fwd(name, qb, kh, vh, gq, gk, sinks, bias, group):
    t = qb.shape[0]
    kvh = kh.shape[0]
    nblk = t // WINDOW
    gw = group * HEAD_DIM
    band = 2 * WINDOW
    cols = group * WINDOW

    def body(q_ref, k_ref, v_ref, gq_ref, gk_ref, sink_ref, bias_ref, o_ref, lse_ref, qs, kpad, vpad):
        for g in range(group):
            qs[:, g * HEAD_DIM:(g + 1) * HEAD_DIM] = _qhead(q_ref[:, g * HEAD_DIM:(g + 1) * HEAD_DIM], gq_ref[...]).astype(BF16)
        kpad[0:WINDOW, :] = jnp.zeros((WINDOW, HEAD_DIM), BF16)
        vpad[0:WINDOW, :] = jnp.zeros((WINDOW, HEAD_DIM), BF16)
        kpad[WINDOW:, :] = _rms(k_ref[0], gk_ref[...]).astype(BF16)
        vpad[WINDOW:, :] = v_ref[0].astype(BF16)
        sink = sink_ref[0]

        def block(n, carry):
            start = pl.multiple_of(n * WINDOW, WINDOW)
            kb = kpad[pl.ds(start, band), :]
            vb = vpad[pl.ds(start, band), :]
            s = _dot_nt(kb, _swa_stack(qs, start, group)) + bias_ref[0] + _swa_mask(n, group)
            m = jnp.maximum(jnp.max(s, axis=0, keepdims=True), sink)
            e = jnp.exp(s - m)
            l = jnp.sum(e, axis=0, keepdims=True) + jnp.exp(sink - m)
            o_t = _dot_tn(vb, e.astype(BF16)) / l
            for g in range(group):
                o_ref[pl.ds(start, WINDOW), g * HEAD_DIM:(g + 1) * HEAD_DIM] = o_t[:, g * WINDOW:(g + 1) * WINDOW].T.astype(BF16)
            lse_ref[pl.ds(n, 1), :] = m + jnp.log(l)
            return carry

        lax.fori_loop(0, nblk, block, 0)

    fixed = lambda h: (0, 0)
    per = lambda h: (h, 0, 0)
    return pl.pallas_call(
        body, name=name, grid=(kvh,),
        in_specs=[pl.BlockSpec((t, gw), lambda h: (0, h)), pl.BlockSpec((1, t, HEAD_DIM), per), pl.BlockSpec((1, t, HEAD_DIM), per),
                  pl.BlockSpec((1, HEAD_DIM), fixed), pl.BlockSpec((1, HEAD_DIM), fixed),
                  pl.BlockSpec((1, 1, cols), per), pl.BlockSpec((1, band, cols), per)],
        out_specs=[pl.BlockSpec((t, gw), lambda h: (0, h)), pl.BlockSpec((nblk, cols), lambda h: (h, 0))],
        out_shape=[_sds((t, kvh * gw), BF16), _sds((kvh * nblk, cols), F32)],
        scratch_shapes=[pltpu.VMEM((t, gw), BF16), pltpu.VMEM((t + WINDOW, HEAD_DIM), BF16),
                        pltpu.VMEM((t + WINDOW, HEAD_DIM), BF16)],
        compiler_params=_params(1),
    )(qb, kh, vh, gq, gk, sinks, bias)


def _swa_bwd(name, qb, kh, vh, gq, gk, sinks, bias, lse, do, group):
    t = qb.shape[0]
    kvh = kh.shape[0]
    nblk = t // WINDOW
    gw = group * HEAD_DIM
    band = 2 * WINDOW
    cols = group * WINDOW

    def body(q_ref, k_ref, v_ref, gq_ref, gk_ref, sink_ref, bias_ref, lse_ref, do_ref,
             dq_ref, dk_ref, dv_ref, dgq_ref, dgk_ref, dsink_ref, dbias_ref,
             qs, kpad, vpad, dqs, dk_acc, dv_acc, dsink_acc):
        @pl.when(pl.program_id(0) == 0)
        def _():
            dgq_ref[...] = jnp.zeros_like(dgq_ref)
            dgk_ref[...] = jnp.zeros_like(dgk_ref)

        for g in range(group):
            qs[:, g * HEAD_DIM:(g + 1) * HEAD_DIM] = _qhead(q_ref[:, g * HEAD_DIM:(g + 1) * HEAD_DIM], gq_ref[...]).astype(BF16)
        kpad[0:WINDOW, :] = jnp.zeros((WINDOW, HEAD_DIM), BF16)
        vpad[0:WINDOW, :] = jnp.zeros((WINDOW, HEAD_DIM), BF16)
        kpad[WINDOW:, :] = _rms(k_ref[0], gk_ref[...]).astype(BF16)
        vpad[WINDOW:, :] = v_ref[0].astype(BF16)
        dk_acc[...] = jnp.zeros_like(dk_acc)
        dv_acc[...] = jnp.zeros_like(dv_acc)
        dsink_acc[...] = jnp.zeros_like(dsink_acc)
        dbias_ref[...] = jnp.zeros_like(dbias_ref)
        sink = sink_ref[0]

        def block(n, carry):
            start = pl.multiple_of(n * WINDOW, WINDOW)
            kb = kpad[pl.ds(start, band), :]
            vb = vpad[pl.ds(start, band), :]
            q = _swa_stack(qs, start, group)
            dob = _swa_stack(do_ref, start, group)
            lse_n = lse_ref[pl.ds(n, 1), :]
            s = _dot_nt(kb, q) + bias_ref[0] + _swa_mask(n, group)
            p = jnp.exp(s - lse_n)
            dp = _dot_nt(vb, dob)
            dsum = jnp.sum(p * dp, axis=0, keepdims=True)
            ds = p * (dp - dsum)
            dsb = ds.astype(BF16)
            dsink_acc[...] -= jnp.exp(sink - lse_n) * dsum
            dbias_ref[0] += ds
            dq = _dot_tn(dsb, kb)
            for g in range(group):
                dqs[pl.ds(start, WINDOW), g * HEAD_DIM:(g + 1) * HEAD_DIM] = dq[g * WINDOW:(g + 1) * WINDOW]
            dk_acc[pl.ds(start, band), :] += _dot(dsb, q)
            dv_acc[pl.ds(start, band), :] += _dot(p.astype(BF16), dob)
            return carry

        lax.fori_loop(0, nblk, block, 0)
        for g in range(group):
            _, q_vjp = jax.vjp(_qhead, q_ref[:, g * HEAD_DIM:(g + 1) * HEAD_DIM], gq_ref[...])
            dqp, dgq = q_vjp(dqs[:, g * HEAD_DIM:(g + 1) * HEAD_DIM])
            dq_ref[:, g * HEAD_DIM:(g + 1) * HEAD_DIM] = dqp.astype(BF16)
            dgq_ref[...] += dgq
            dsink_g = jnp.sum(dsink_acc[:, g * WINDOW:(g + 1) * WINDOW], axis=1, keepdims=True)
            dsink_ref[0, g:g + 1, :] = jnp.broadcast_to(dsink_g, (1, LANES))
        _, k_vjp = jax.vjp(_rms, k_ref[0], gk_ref[...])
        dkp, dgk = k_vjp(dk_acc[WINDOW:, :])
        dk_ref[0] = dkp
        dgk_ref[...] += dgk
        dv_ref[0] = dv_acc[WINDOW:, :]

    fixed = lambda h: (0, 0)
    per = lambda h: (h, 0, 0)
    wide = pl.BlockSpec((t, gw), lambda h: (0, h))
    head = pl.BlockSpec((1, t, HEAD_DIM), per)
    vec = pl.BlockSpec((1, HEAD_DIM), fixed)
    bias_spec = pl.BlockSpec((1, band, cols), per)
    return pl.pallas_call(
        body, name=name, grid=(kvh,),
        in_specs=[wide, head, head, vec, vec, pl.BlockSpec((1, 1, cols), per), bias_spec,
                  pl.BlockSpec((nblk, cols), lambda h: (h, 0)), wide],
        out_specs=[wide, head, head, vec, vec, pl.BlockSpec((1, group, LANES), per), bias_spec],
        out_shape=[_sds((t, kvh * gw), BF16), _sds((kvh, t, HEAD_DIM), F32), _sds((kvh, t, HEAD_DIM), F32),
                   _sds((1, HEAD_DIM), F32), _sds((1, HEAD_DIM), F32),
                   _sds((kvh, group, LANES), F32), _sds((kvh, band, cols), F32)],
        scratch_shapes=[pltpu.VMEM((t, gw), BF16), pltpu.VMEM((t + WINDOW, HEAD_DIM), BF16),
                        pltpu.VMEM((t + WINDOW, HEAD_DIM), BF16), pltpu.VMEM((t, gw), F32),
                        pltpu.VMEM((t + WINDOW, HEAD_DIM), F32), pltpu.VMEM((t + WINDOW, HEAD_DIM), F32),
                        pltpu.VMEM((1, cols), F32)],
        compiler_params=_params(1),
    )(qb, kh, vh, gq, gk, sinks, bias, lse, do)


def _local_step(x, target, p, comm):
    t, d = x.shape
    n_heads = d // HEAD_DIM
    kv_heads = n_heads // 8
    group = n_heads // kv_heads
    hw = n_heads * HEAD_DIM
    gate_col = 3 * hw // LANES
    kvw = kv_heads * HEAD_DIM
    grads = {}

    def mlp_fwd(tag, h, g, layer, then):
        w_up, = comm.weights([f"w_up{layer}"], h)
        a, hn = _norm_matmul(f"{tag}_up", h, g, w_up, relu2=True)
        if then:
            comm.prefetch(then, a)
        w_down, = comm.weights([f"w_down{layer}"], a)
        return _matmul_res(f"{tag}_down", a, w_down, h), (h, g, hn, a, w_up, w_down)

    def mlp_bwd(tag, saved, layer, dy):
        h, g, hn, a, w_up, w_down = saved
        du = _matmul_nt(f"{tag}_du", dy, w_down, a=a)
        dw_down = _matmul_tn(f"{tag}_dwdown", a, dy)
        dw_up = _matmul_tn(f"{tag}_dwup", hn, du, col_blocks=w_up.shape[0])
        zero = comm.send_grads(tag, {f"w_down{layer}": dw_down, f"w_up{layer}": dw_up})
        return _matmul_nt_rmsbwd(f"{tag}_dh", du, w_up, h, g + zero, dy)

    comm.prefetch(["w_in_a"], None)
    w_in, = comm.weights(["w_in_a"], None)
    proj, xn1 = _norm_matmul("a_inproj", x, p["g_attn"][0], w_in, tn=640)
    comm.prefetch(["w_out_a"], proj)
    b_pad = jnp.pad(p["b_f"], ((0, 0), (0, LANES - n_heads)))
    c = _gate_fwd("a_gate", proj, b_pad, n_heads, gate_col)
    crow = c[:, :n_heads].T.reshape(n_heads // 2, 2, t)
    o_a, lse_a = _fox_fwd("a_attn", proj, c, crow, p["gq_a"], p["gk_a"], n_heads)
    comm.prefetch(["w_up0", "w_down0"], o_a)
    w_out_a, = comm.weights(["w_out_a"], o_a)
    h1 = _matmul_res("a_outproj", o_a, w_out_a, x)
    h2, mlp0 = mlp_fwd("mlp0", h1, p["g_mlp"][0], 0, ["w_kv", "w_q_b", "w_out_b"])

    w_kv, w_q_b = comm.weights(["w_kv", "w_q_b"], h2)
    kv, hn_kv = _norm_matmul("kv_proj", h2, p["g_kv"], w_kv, tn=2 * kvw)
    kh = kv[:, :kvw].reshape(t, kv_heads, HEAD_DIM).transpose(1, 0, 2)
    vh = kv[:, kvw:].reshape(t, kv_heads, HEAD_DIM).transpose(1, 0, 2)
    qb, hn_q = _norm_matmul("b_qproj", h2, p["g_attn"][1], w_q_b, tn=512)
    comm.prefetch(["w_up1", "w_down1"], qb)
    gqb, gkb = p["gq_b"], p["gk_b"].reshape(1, HEAD_DIM)
    bias = _bias_expand("b_bias", p["rel_bias"].T).reshape(kv_heads, group, 2 * WINDOW, WINDOW)
    bias = bias.transpose(0, 2, 1, 3).reshape(kv_heads, 2 * WINDOW, group * WINDOW)
    sink_rows = jnp.broadcast_to(p["sinks"].reshape(kv_heads, 1, group, 1), (kv_heads, 1, group, WINDOW)).reshape(kv_heads, 1, group * WINDOW)
    o_b, lse_b = _swa_fwd("b_attn", qb, kh, vh, gqb, gkb, sink_rows, bias, group)
    w_out_b, = comm.weights(["w_out_b"], o_b)
    h3 = _matmul_res("b_outproj", o_b, w_out_b, h2)
    y, mlp1 = mlp_fwd("mlp1", h3, p["g_mlp"][1], 1, None)
    dy, loss_tile = _loss_grad("loss", y, target)

    dh3, dg_mlp1 = mlp_bwd("mlp1", mlp1, 1, dy)
    do_b = _matmul_nt("b_do", dh3, w_out_b)
    dw_out_b = _matmul_tn("b_dwout", o_b, dh3)
    dqb, dkh, dvh, grads["gq_b"], dgk_b, dsink, dbias = _swa_bwd(
        "b_attn_bwd", qb, kh, vh, gqb, gkb, sink_rows, bias, lse_b, do_b, group)
    grads["gk_b"] = dgk_b.reshape(HEAD_DIM)
    grads["sinks"] = dsink[:, :, 0].reshape(1, n_heads)
    dbias = dbias.reshape(kv_heads, 2 * WINDOW, group, WINDOW).transpose(0, 2, 1, 3)
    grads["rel_bias"] = _bias_reduce("b_dbias", dbias.reshape(n_heads, WINDOW * 2 * WINDOW)).T
    dw_q_b = _matmul_tn("b_dwq", hn_q, dqb)
    dh2, dg_attn1 = _matmul_nt_rmsbwd("b_dhq", dqb, w_q_b, h2, p["g_attn"][1], dh3)
    dkv = jnp.concatenate([dkh.transpose(1, 0, 2).reshape(t, kvw), dvh.transpose(1, 0, 2).reshape(t, kvw)], axis=1)
    dw_kv = _matmul_tn("kv_dw", hn_kv, dkv)
    zero = comm.send_grads("attn_b", {"w_out_b": dw_out_b, "w_q_b": dw_q_b, "w_kv": dw_kv})
    dh2, dg_kv = _matmul_nt_rmsbwd("kv_dh", dkv, w_kv, h2, p["g_kv"] + zero, dh2)
    grads["g_kv"] = dg_kv.reshape(d)
    dh1, dg_mlp0 = mlp_bwd("mlp0", mlp0, 0, dh2)
    grads["g_mlp"] = jnp.concatenate([dg_mlp0, dg_mlp1], axis=0)

    do_a = _matmul_nt("a_do", dh1, w_out_a)
    dw_out_a = _matmul_tn("a_dwout", o_a, dh1)
    zero = comm.send_grads("attn_a_out", {"w_out_a": dw_out_a})
    dq, dk, dv, dc_cols, grads["gq_a"], grads["gk_a"] = _fox_bwd(
        "a_attn_bwd", proj, c, crow, p["gq_a"] + zero, p["gk_a"], lse_a, do_a, n_heads)
    dc = jnp.pad(dc_cols.reshape(t, n_heads // 2, LANES)[:, :, :2].reshape(t, n_heads), ((0, 0), (0, LANES - n_heads)))
    dfl, db_f = _gate_bwd("a_gate_bwd", proj, b_pad, dc, n_heads, gate_col)
    grads["b_f"] = db_f[:, :n_heads]
    dproj = jnp.concatenate([dq, dk, dv, dfl], axis=1)
    dw_in = _matmul_tn("a_dwin", xn1, dproj, tn=640)
    zero = comm.send_grads("attn_a_in", {"w_in_a": dw_in})
    grad_x, dg_attn0 = _matmul_nt_rmsbwd("a_dx", dproj, w_in, x, p["g_attn"][0] + zero, dh1)
    grads["g_attn"] = jnp.concatenate([dg_attn0, dg_attn1], axis=0)
    return loss_tile, grad_x, grads


def _exchange(name, arrays, scatter, after):
    n = len(arrays)

    def body(*refs):
        src, out = refs[:n], refs[n + 1:2 * n + 1]
        send_sems, recv_sems, local_sems = refs[2 * n + 1:]
        x, y, c = lax.axis_index("x"), lax.axis_index("y"), lax.axis_index("c")
        me = 4 * x + 2 * y + c
        peers = []
        for k in range(1, N_DEV):
            px, py, pc = x ^ ((k >> 2) & 1), y ^ ((k >> 1) & 1), c ^ (k & 1)
            peers.append(((px, py, pc), 4 * px + 2 * py + pc))
        local, sends = [], []
        for a in range(n):
            mine = src[a].at[me] if scatter else src[a]
            cp = pltpu.make_async_copy(mine, out[a].at[me], local_sems.at[a])
            cp.start()
            local.append(cp)
        for k, (peer, peer_pos) in enumerate(peers):
            for a in range(n):
                cp = pltpu.make_async_remote_copy(
                    src_ref=src[a].at[peer_pos] if scatter else src[a], dst_ref=out[a].at[me],
                    send_sem=send_sems.at[a, k], recv_sem=recv_sems.at[a, k],
                    device_id=peer, device_id_type=pl.DeviceIdType.MESH)
                cp.start()
                sends.append(cp)
        for k, (peer, peer_pos) in enumerate(peers):
            for a in range(n):
                pltpu.make_async_remote_copy(
                    src_ref=out[a].at[peer_pos], dst_ref=out[a].at[peer_pos],
                    send_sem=send_sems.at[a, k], recv_sem=recv_sems.at[a, k],
                    device_id=peer, device_id_type=pl.DeviceIdType.MESH).wait_recv()
        for cp in sends:
            cp.wait_send()
        for cp in local:
            cp.wait()

    any_spec = pl.BlockSpec(memory_space=pl.ANY)
    out_shape = [_sds(a.shape if scatter else (N_DEV,) + a.shape, a.dtype) for a in arrays]
    return pl.pallas_call(
        body, name=name, in_specs=[any_spec] * (n + 1), out_specs=[any_spec] * n, out_shape=out_shape,
        scratch_shapes=[pltpu.SemaphoreType.DMA((n, N_DEV - 1)), pltpu.SemaphoreType.DMA((n, N_DEV - 1)),
                        pltpu.SemaphoreType.DMA((n,))],
    )(*arrays, after)


EVERYONE = (1, 2, 3, 4, 5, 6, 7)
SAME_CORE = (1, 2, 4, 6)
OTHER_CHIPS = (2, 4, 6)


class _InFlight:
    def __init__(self, scatter, ks, send_sems, recv_sems, srcs, lands, token):
        self.scatter, self.ks, self.send_sems, self.recv_sems = scatter, ks, send_sems, recv_sems
        self.srcs, self.lands, self.token = list(srcs), list(lands), token


def _mesh_peers(ks=EVERYONE):
    x, y, c = lax.axis_index("x"), lax.axis_index("y"), lax.axis_index("c")
    peers = []
    for k in ks:
        px, py, pc = x ^ ((k >> 2) & 1), y ^ ((k >> 1) & 1), c ^ (k & 1)
        peers.append(((px, py, pc), 4 * px + 2 * py + pc))
    return 4 * x + 2 * y + c, peers


_HBM_SPEC = pl.BlockSpec(memory_space=pltpu.HBM)
_SEM_SPEC = pl.BlockSpec(memory_space=pltpu.SEMAPHORE)
_SIDE_EFFECT = pltpu.SideEffectType.DATAFLOW_SIDE_EFFECTING


def _exchange_start(name, arrays, scatter, ks=EVERYONE):
    n = len(arrays)
    me, _ = _mesh_peers()
    lands = []
    for a in arrays:
        own = lax.dynamic_index_in_dim(a, me, 0, keepdims=False) if scatter else a
        shape = a.shape if scatter else (N_DEV,) + a.shape
        lands.append(lax.dynamic_update_index_in_dim(lax.empty(shape, a.dtype), own, me, 0))

    def body(*refs):
        src, land = refs[:n], refs[n:2 * n]
        send_sems, recv_sems, token = refs[2 * n], refs[2 * n + 1], refs[-1]
        pos, peers = _mesh_peers(ks)
        for a in range(n):
            for k, (peer, peer_pos) in enumerate(peers):
                pltpu.make_async_remote_copy(
                    src_ref=src[a].at[peer_pos] if scatter else src[a], dst_ref=land[a].at[pos],
                    send_sem=send_sems.at[a * len(ks) + k], recv_sem=recv_sems.at[a * len(ks) + k],
                    device_id=peer, device_id_type=pl.DeviceIdType.MESH).start()
        token[...] = jnp.zeros_like(token)

    operands = [pltpu.with_memory_space_constraint(a, pltpu.HBM) for a in list(arrays) + lands]
    outs = pl.pallas_call(
        body, name=name,
        out_shape=(pltpu.SemaphoreType.DMA((n * len(ks),)), pltpu.SemaphoreType.DMA((n * len(ks),)),
                   *[pltpu.HBM(a.shape, a.dtype) for a in operands], _sds((8, LANES), F32)),
        in_specs=[_HBM_SPEC] * (2 * n),
        out_specs=(_SEM_SPEC, _SEM_SPEC, *[_HBM_SPEC] * (2 * n), pl.BlockSpec(memory_space=pltpu.VMEM)),
        input_output_aliases={i: 2 + i for i in range(2 * n)},
        compiler_params=pltpu.CompilerParams(has_side_effects=_SIDE_EFFECT),
    )(*operands)
    return _InFlight(scatter, ks, outs[0], outs[1], outs[2:2 + n], outs[2 + n:2 + 2 * n], outs[-1])


def _exchange_wait(name, flight, which, after):
    m = len(which)
    scatter, ks = flight.scatter, flight.ks

    def body(*refs):
        src, land = refs[:m], refs[m:2 * m]
        send_sems, recv_sems = refs[2 * m], refs[2 * m + 1]
        _, peers = _mesh_peers(ks)
        for i, a in enumerate(which):
            for k, (peer, peer_pos) in enumerate(peers):
                cp = pltpu.make_async_remote_copy(
                    src_ref=src[i].at[peer_pos] if scatter else src[i], dst_ref=land[i].at[peer_pos],
                    send_sem=send_sems.at[a * len(ks) + k], recv_sem=recv_sems.at[a * len(ks) + k],
                    device_id=peer, device_id_type=pl.DeviceIdType.MESH)
                cp.wait_send()
                cp.wait_recv()

    operands = [flight.srcs[a] for a in which] + [flight.lands[a] for a in which]
    outs = pl.pallas_call(
        body, name=name, out_shape=tuple(pltpu.HBM(a.shape, a.dtype) for a in operands),
        in_specs=[_HBM_SPEC] * (2 * m) + [_SEM_SPEC, _SEM_SPEC, pl.BlockSpec(memory_space=pl.ANY)],
        out_specs=tuple([_HBM_SPEC] * (2 * m)), input_output_aliases={i: i for i in range(2 * m)},
        compiler_params=pltpu.CompilerParams(has_side_effects=_SIDE_EFFECT),
    )(*operands, flight.send_sems, flight.recv_sems, after)
    return list(outs[m:])


def _relay_start(name, lands):
    n = len(lands)

    def body(*refs):
        land, send_sems, recv_sems, token = refs[:n], refs[n], refs[n + 1], refs[-1]
        _, peers = _mesh_peers(OTHER_CHIPS)
        sibling = (lax.axis_index("x"), lax.axis_index("y"), 1 - lax.axis_index("c"))
        for a in range(n):
            for k, (_, peer_pos) in enumerate(peers):
                pltpu.make_async_remote_copy(
                    src_ref=land[a].at[peer_pos], dst_ref=land[a].at[peer_pos],
                    send_sem=send_sems.at[a * len(peers) + k], recv_sem=recv_sems.at[a * len(peers) + k],
                    device_id=sibling, device_id_type=pl.DeviceIdType.MESH).start()
        token[...] = jnp.zeros_like(token)

    count = n * len(OTHER_CHIPS)
    outs = pl.pallas_call(
        body, name=name,
        out_shape=(pltpu.SemaphoreType.DMA((count,)), pltpu.SemaphoreType.DMA((count,)),
                   *[pltpu.HBM(a.shape, a.dtype) for a in lands], _sds((8, LANES), F32)),
        in_specs=[_HBM_SPEC] * n,
        out_specs=(_SEM_SPEC, _SEM_SPEC, *[_HBM_SPEC] * n, pl.BlockSpec(memory_space=pltpu.VMEM)),
        input_output_aliases={i: 2 + i for i in range(n)},
        compiler_params=pltpu.CompilerParams(has_side_effects=_SIDE_EFFECT),
    )(*[pltpu.with_memory_space_constraint(a, pltpu.HBM) for a in lands])
    return _InFlight(False, OTHER_CHIPS, outs[0], outs[1], [], outs[2:2 + n], outs[-1])


def _relay_wait(name, flight, after):
    n = len(flight.lands)

    def body(*refs):
        land, send_sems, recv_sems = refs[:n], refs[n], refs[n + 1]
        _, peers = _mesh_peers(OTHER_CHIPS)
        sibling = (lax.axis_index("x"), lax.axis_index("y"), 1 - lax.axis_index("c"))
        for a in range(n):
            for k, (_, peer_pos) in enumerate(peers):
                cp = pltpu.make_async_remote_copy(
                    src_ref=land[a].at[peer_pos], dst_ref=land[a].at[peer_pos ^ 1],
                    send_sem=send_sems.at[a * len(peers) + k], recv_sem=recv_sems.at[a * len(peers) + k],
                    device_id=sibling, device_id_type=pl.DeviceIdType.MESH)
                cp.wait_send()
                cp.wait_recv()

    outs = pl.pallas_call(
        body, name=name, out_shape=tuple(pltpu.HBM(a.shape, a.dtype) for a in flight.lands),
        in_specs=[_HBM_SPEC] * n + [_SEM_SPEC, _SEM_SPEC, pl.BlockSpec(memory_space=pl.ANY)],
        out_specs=tuple([_HBM_SPEC] * n), input_output_aliases={i: i for i in range(n)},
        compiler_params=pltpu.CompilerParams(has_side_effects=_SIDE_EFFECT),
    )(*flight.lands, flight.send_sems, flight.recv_sems, after)
    return list(outs)


def _adamw(name, parts, w, m, v, layer=None, into=None):
    r, c = w.shape[-2:]
    tr = r if r <= 256 else 256
    n_into = 0 if into is None else len(into)

    def body(p_ref, w_ref, m_ref, v_ref, *refs):
        g_ref, d_ref, mo_ref, vo_ref = refs[n_into:]
        g = p_ref[0].astype(F32)
        for dev in range(1, N_DEV):
            g = g + p_ref[dev].astype(F32)
        m_new = ADAM_B1 * m_ref[...] + (1.0 - ADAM_B1) * g
        v_new = ADAM_B2 * v_ref[...] + (1.0 - ADAM_B2) * jnp.square(g)
        m_hat = m_new / (1.0 - ADAM_B1 ** ADAM_STEP)
        v_hat = v_new / (1.0 - ADAM_B2 ** ADAM_STEP)
        g_ref[...] = g
        d_ref[...] = -ADAM_LR * (m_hat / (jnp.sqrt(v_hat) + ADAM_EPS) + ADAM_WD * w_ref[...])
        mo_ref[...] = m_new
        vo_ref[...] = v_new

    if layer is None:
        blk = pl.BlockSpec((tr, c), lambda i: (i, 0))
    else:
        blk = pl.BlockSpec((None, tr, c), lambda i: (layer, i, 0))
    return pl.pallas_call(
        body, name=name, grid=(r // tr,),
        in_specs=[pl.BlockSpec((N_DEV, tr, c), lambda i: (0, i, 0)), blk, blk, blk] + [pl.BlockSpec(memory_space=pl.ANY)] * n_into,
        out_specs=[blk] * 4, out_shape=[_sds(w.shape, F32)] * 4,
        input_output_aliases={4 + i: i for i in range(n_into)}, compiler_params=_params(1),
    )(parts, w, m, v, *(into or ()))


def _pack_small(tree, last):
    flat = jnp.concatenate([tree[k].reshape(-1) for k in SMALL] + [last])
    size = -(-flat.shape[0] // (8 * LANES)) * (8 * LANES)
    return jnp.pad(flat, (0, size - flat.shape[0])).reshape(-1, LANES)


def _unpack_small(packed, like):
    flat, out, off = packed.reshape(-1), {}, 0
    for k in SMALL:
        size = like[k].size
        out[k] = flat[off:off + size].reshape(like[k].shape)
        off += size
    return out


class _Comm:
    ORDER = ("w_in_a", "w_out_a", "w_up0", "w_down0", "w_kv", "w_q_b", "w_out_b", "w_up1", "w_down1")

    def __init__(self, shards, d, n_in):
        self.d, self.n_in = d, n_in
        self.flight = _exchange_start("gather_start", [shards[n].astype(BF16) for n in self.ORDER], scatter=False, ks=SAME_CORE)
        self.relays, self.ready, self.sent = {}, {}, []

    def prefetch(self, names, after):
        which = [self.ORDER.index(n) for n in names]
        landed = _exchange_wait(f"gather_wait_{names[0]}", self.flight, which, self.flight.token if after is None else after)
        relay = _relay_start(f"gather_relay_{names[0]}", landed)
        for n in names:
            self.relays[n] = (relay, names)

    def weights(self, names, after):
        for n in names:
            if n not in self.ready:
                relay, group = self.relays[n]
                landed = _relay_wait(f"gather_relay_wait_{group[0]}", relay, relay.token if after is None else after)
                self.ready.update({m: self._whole(m, g) for m, g in zip(group, landed)})
        return [self.ready[n] for n in names]

    def _whole(self, name, g):
        if name == "w_in_a":
            pad = -(-self.n_in // LANES) * LANES - self.n_in
            return jnp.pad(g.transpose(1, 0, 2).reshape(self.d, self.n_in), ((0, 0), (0, pad)))
        if name.startswith("w_up"):
            return g
        return g.reshape(-1, g.shape[-1])

    def _chunks(self, name, g):
        if name == "w_in_a":
            return g[:, :self.n_in].reshape(self.d, N_DEV, -1).transpose(1, 0, 2)
        if name.startswith("w_up"):
            return g
        return g.reshape(N_DEV, g.shape[0] // N_DEV, g.shape[1])

    def send_grads(self, tag, partials):
        names = list(partials)
        flight = _exchange_start(f"scatter_start_{tag}", [self._chunks(n, partials[n]) for n in names], scatter=True)
        self.sent.append((tag, flight, names))
        return flight.token[0, 0]

    def received(self, index, after):
        tag, flight, names = self.sent[index]
        landed = _exchange_wait(f"scatter_wait_{tag}", flight, list(range(len(names))), after)
        return dict(zip(names, landed))


def kernel(x, g_attn, g_mlp, w_in_a, b_f, gq_a, gk_a, w_out_a, g_kv, w_kv, gk_b, w_q_b, gq_b, sinks, rel_bias, w_out_b, w_up, w_down, loss_target, m_g_attn, m_g_mlp, m_w_in_a, m_b_f, m_gq_a, m_gk_a, m_w_out_a, m_g_kv, m_w_kv, m_gk_b, m_w_q_b, m_gq_b, m_sinks, m_rel_bias, m_w_out_b, m_w_up, m_w_down, v_g_attn, v_g_mlp, v_w_in_a, v_b_f, v_gq_a, v_gk_a, v_w_out_a, v_g_kv, v_w_kv, v_gk_b, v_w_q_b, v_gq_b, v_sinks, v_rel_bias, v_w_out_b, v_w_up, v_w_down):
    w = dict(g_attn=g_attn, g_mlp=g_mlp, w_in_a=w_in_a, b_f=b_f, gq_a=gq_a, gk_a=gk_a, w_out_a=w_out_a, g_kv=g_kv,
             w_kv=w_kv, gk_b=gk_b, w_q_b=w_q_b, gq_b=gq_b, sinks=sinks, rel_bias=rel_bias, w_out_b=w_out_b,
             w_up=w_up, w_down=w_down)
    mom = dict(g_attn=m_g_attn, g_mlp=m_g_mlp, w_in_a=m_w_in_a, b_f=m_b_f, gq_a=m_gq_a, gk_a=m_gk_a, w_out_a=m_w_out_a,
               g_kv=m_g_kv, w_kv=m_w_kv, gk_b=m_gk_b, w_q_b=m_w_q_b, gq_b=m_gq_b, sinks=m_sinks, rel_bias=m_rel_bias,
               w_out_b=m_w_out_b, w_up=m_w_up, w_down=m_w_down)
    var = dict(g_attn=v_g_attn, g_mlp=v_g_mlp, w_in_a=v_w_in_a, b_f=v_b_f, gq_a=v_gq_a, gk_a=v_gk_a, w_out_a=v_w_out_a,
               g_kv=v_g_kv, w_kv=v_w_kv, gk_b=v_gk_b, w_q_b=v_w_q_b, gq_b=v_gq_b, sinks=v_sinks, rel_bias=v_rel_bias,
               w_out_b=v_w_out_b, w_up=v_w_up, w_down=v_w_down)
    d = x.shape[2]
    where = {"w_in_a": ("w_in_a", 0), "w_out_a": ("w_out_a", 0), "w_kv": ("w_kv", None), "w_q_b": ("w_q_b", 0),
             "w_out_b": ("w_out_b", 0), "w_up0": ("w_up", 0), "w_up1": ("w_up", 1), "w_down0": ("w_down", 0),
             "w_down1": ("w_down", 1)}
    shards = {n: (w[k] if layer is None else w[k][layer]) for n, (k, layer) in where.items()}
    comm = _Comm(shards, d, w_in_a.shape[2] * N_DEV)
    loss_tile, grad_x, grads = _local_step(x[0], loss_target[0], {k: w[k] for k in SMALL}, comm)

    res, after = {}, grad_x
    last = len(comm.sent) - 1
    n_small = sum(w[k].size for k in SMALL)
    zero = jnp.zeros((1,), F32)
    for index in range(last + 1):
        if index == last:
            small_parts, = _exchange("gather_small_grads", [_pack_small(grads, loss_tile[0, 0:1])], scatter=False, after=after)
            small = _adamw("adam_small", small_parts, _pack_small(w, zero), _pack_small(mom, zero), _pack_small(var, zero))
            loss = small[0].reshape(-1)[n_small]
            after = small[0]
        for n, parts in comm.received(index, after).items():
            k, layer = where[n]
            res[k] = _adamw(f"adam_{n}", parts, w[k], mom[k], var[k], layer, res.get(k))
            after = res[k][0]
    small = [_unpack_small(s, w) for s in small]
    for k in SMALL:
        res[k] = [s[k] for s in small]

    outs = [loss, grad_x[None]]
    for i in range(4):
        outs.extend(res[k][i] for k in WEIGHTS)
    return tuple(outs)
```

```python
import numpy as np
import jax
import jax.numpy as jnp
from jax import lax
from jax.experimental import pallas as pl
from jax.experimental.pallas import tpu as pltpu

F32 = jnp.float32
BF16 = jnp.bfloat16

N_DEV = 8
HEAD_DIM = 64
WINDOW = 128
N_BUCKETS = 32
REL_MAX_DIST = 128
NORM_EPS = 1e-6
NEG = -1e30
LANES = 128
VMEM_LIMIT = 56 * 1024 * 1024

ADAM_LR = 0.001
ADAM_B1 = 0.9
ADAM_B2 = 0.999
ADAM_EPS = 1e-08
ADAM_WD = 0.01
ADAM_STEP = 10

SMALL = ("g_attn", "g_mlp", "b_f", "gq_a", "gk_a", "g_kv", "gk_b", "gq_b", "sinks", "rel_bias")
WEIGHTS = ("g_attn", "g_mlp", "w_in_a", "b_f", "gq_a", "gk_a", "w_out_a", "g_kv", "w_kv", "gk_b",
           "w_q_b", "gq_b", "sinks", "rel_bias", "w_out_b", "w_up", "w_down")


def _params(n_grid):
    return pltpu.CompilerParams(dimension_semantics=("arbitrary",) * n_grid, vmem_limit_bytes=VMEM_LIMIT)


def _sds(shape, dtype):
    return jax.ShapeDtypeStruct(tuple(shape), dtype)


def _rms(x, g):
    return (x * lax.rsqrt(jnp.mean(x * x, axis=-1, keepdims=True) + NORM_EPS)) * g


def _dot_nt(a, b):
    return lax.dot_general(a, b, (((1,), (1,)), ((), ())), preferred_element_type=F32)


def _dot_tn(a, b):
    return lax.dot_general(a, b, (((0,), (0,)), ((), ())), preferred_element_type=F32)


def _dot(a, b):
    return jnp.dot(a, b, preferred_element_type=F32)


def _dot_exact(a, b):
    return jnp.dot(a, b, preferred_element_type=F32, precision=lax.Precision.HIGHEST)


def _norm_matmul(name, x, g, w, *, tn=None, relu2=False):
    t, d = x.shape
    blocked = w.ndim == 3
    if blocked:
        tn = w.shape[2]
        n = w.shape[0] * tn
        w_spec = pl.BlockSpec((None, d, tn), lambda i, j: (j, 0, 0))
    else:
        n = w.shape[1]
        w_spec = pl.BlockSpec((d, tn), lambda i, j: (0, j))
    tm = min(1024, t)

    def body(x_ref, g_ref, w_ref, y_ref, xn_ref):
        @pl.when(pl.program_id(1) == 0)
        def _():
            xn_ref[...] = _rms(x_ref[...], g_ref[...]).astype(BF16)

        y = _dot(xn_ref[...], w_ref[...])
        y_ref[...] = jnp.square(jnp.maximum(y, 0.0)).astype(BF16) if relu2 else y

    out_shape = [_sds((t, n), BF16 if relu2 else F32), _sds((t, d), BF16)]
    out_specs = [pl.BlockSpec((tm, tn), lambda i, j: (i, j)), pl.BlockSpec((tm, d), lambda i, j: (i, 0))]
    return pl.pallas_call(
        body, name=name, grid=(t // tm, n // tn),
        in_specs=[pl.BlockSpec((tm, d), lambda i, j: (i, 0)), pl.BlockSpec((1, d), lambda i, j: (0, 0)), w_spec],
        out_specs=out_specs, out_shape=out_shape, compiler_params=_params(2),
    )(x, g.reshape(1, d), w)


def _matmul_res(name, a, w, res, *, tn=512):
    t, k = a.shape
    n = w.shape[1]
    tm = min(1024, t)

    def body(a_ref, w_ref, r_ref, o_ref):
        o_ref[...] = r_ref[...] + _dot(a_ref[...], w_ref[...])

    return pl.pallas_call(
        body, name=name, grid=(t // tm, n // tn),
        in_specs=[pl.BlockSpec((tm, k), lambda i, j: (i, 0)), pl.BlockSpec((k, tn), lambda i, j: (0, j)),
                  pl.BlockSpec((tm, tn), lambda i, j: (i, j))],
        out_specs=pl.BlockSpec((tm, tn), lambda i, j: (i, j)), out_shape=_sds((t, n), F32),
        compiler_params=_params(2),
    )(a, w, res)


def _matmul_nt(name, dy, w, *, a=None, tk=1024):
    t, n = dy.shape
    k = w.shape[0]
    tm = min(1024, t)

    def body(dy_ref, w_ref, *rest):
        o_ref = rest[-1]
        r = _dot_nt(dy_ref[...].astype(BF16), w_ref[...])
        if a is not None:
            r = r * (2.0 * jnp.sqrt(rest[0][...].astype(F32)))
        o_ref[...] = r.astype(BF16)

    in_specs = [pl.BlockSpec((tm, n), lambda i, j: (i, 0)), pl.BlockSpec((tk, n), lambda i, j: (j, 0))]
    args = [dy, w]
    if a is not None:
        in_specs.append(pl.BlockSpec((tm, tk), lambda i, j: (i, j)))
        args.append(a)
    return pl.pallas_call(
        body, name=name, grid=(t // tm, k // tk), in_specs=in_specs,
        out_specs=pl.BlockSpec((tm, tk), lambda i, j: (i, j)), out_shape=_sds((t, k), BF16),
        compiler_params=_params(2),
    )(*args)


def _matmul_nt_rmsbwd(name, dy, w, x, g, dres):
    t, k = dy.shape
    blocked = w.ndim == 3
    d = w.shape[1] if blocked else w.shape[0]
    tm = min(512, t)

    def body(dy_ref, w_ref, x_ref, g_ref, r_ref, dx_ref, dg_ref):
        if blocked:
            kb = w.shape[2]
            dxn = _dot_nt(dy_ref[:, 0:kb].astype(BF16), w_ref[0])
            for j in range(1, w.shape[0]):
                dxn += _dot_nt(dy_ref[:, j * kb:(j + 1) * kb].astype(BF16), w_ref[j])
        else:
            dxn = _dot_nt(dy_ref[...].astype(BF16), w_ref[...])
        _, vjp = jax.vjp(_rms, x_ref[...], g_ref[...])
        dx, dg = vjp(dxn)
        dx_ref[...] = r_ref[...] + dx

        @pl.when(pl.program_id(0) == 0)
        def _():
            dg_ref[...] = jnp.zeros_like(dg_ref)

        dg_ref[...] += dg

    row = lambda i: (i, 0)
    fixed = lambda i: (0, 0)
    return pl.pallas_call(
        body, name=name, grid=(t // tm,),
        in_specs=[pl.BlockSpec((tm, k), row), pl.BlockSpec(w.shape, (lambda i: (0, 0, 0)) if blocked else fixed),
                  pl.BlockSpec((tm, d), row), pl.BlockSpec((1, d), fixed), pl.BlockSpec((tm, d), row)],
        out_specs=[pl.BlockSpec((tm, d), row), pl.BlockSpec((1, d), fixed)],
        out_shape=[_sds((t, d), F32), _sds((1, d), F32)], compiler_params=_params(1),
    )(dy, w, x, g.reshape(1, d), dres)


def _matmul_tn(name, a, b, *, tk=1024, tn=1024, col_blocks=None):
    t, k = a.shape
    n = b.shape[1]
    tk = min(tk, k)
    if col_blocks:
        tn = n // col_blocks
        out_spec, out_shape = pl.BlockSpec((None, tk, tn), lambda i, j: (j, i, 0)), _sds((col_blocks, k, tn), BF16)
    else:
        tn = min(tn, n)
        out_spec, out_shape = pl.BlockSpec((tk, tn), lambda i, j: (i, j)), _sds((k, n), BF16)

    def body(a_ref, b_ref, o_ref):
        o_ref[...] = _dot_tn(a_ref[...].astype(BF16), b_ref[...].astype(BF16)).astype(BF16)

    return pl.pallas_call(
        body, name=name, grid=(k // tk, n // tn),
        in_specs=[pl.BlockSpec((t, tk), lambda i, j: (0, i)), pl.BlockSpec((t, tn), lambda i, j: (0, j))],
        out_specs=out_spec, out_shape=out_shape, compiler_params=_params(2),
    )(a, b)


def _loss_grad(name, y, target):
    t, d = y.shape
    tm = min(512, t)

    def body(y_ref, t_ref, dy_ref, l_ref):
        e = y_ref[...] - t_ref[...]
        dy_ref[...] = e * (1.0 / d)

        @pl.when(pl.program_id(0) == 0)
        def _():
            l_ref[...] = jnp.zeros_like(l_ref)

        l_ref[...] += 0.5 * jnp.sum(jnp.mean(e * e, axis=-1))

    row = lambda i: (i, 0)
    return pl.pallas_call(
        body, name=name, grid=(t // tm,), in_specs=[pl.BlockSpec((tm, d), row), pl.BlockSpec((tm, d), row)],
        out_specs=[pl.BlockSpec((tm, d), row), pl.BlockSpec((8, LANES), lambda i: (0, 0))],
        out_shape=[_sds((t, d), F32), _sds((8, LANES), F32)], compiler_params=_params(1),
    )(y, target)


def _gate_fwd(name, proj, b_pad, n_heads, gate_col):
    t = proj.shape[0]
    tb = min(256, t)
    tri = jnp.asarray(np.tril(np.ones((tb, tb), np.float32)))

    def body(p_ref, b_ref, tri_ref, c_ref, carry):
        @pl.when(pl.program_id(0) == 0)
        def _():
            carry[...] = jnp.zeros_like(carry)

        lane = lax.broadcasted_iota(jnp.int32, (tb, LANES), 1)
        lf = jnp.where(lane < n_heads, jax.nn.log_sigmoid(p_ref[...] + b_ref[...]), 0.0)
        c = _dot_exact(tri_ref[...], lf) + carry[0:1, :]
        c_ref[...] = c
        carry[...] = jnp.broadcast_to(c[tb - 1:tb, :], carry.shape)

    return pl.pallas_call(
        body, name=name, grid=(t // tb,),
        in_specs=[pl.BlockSpec((tb, LANES), lambda i: (i, gate_col)), pl.BlockSpec((1, LANES), lambda i: (0, 0)),
                  pl.BlockSpec((tb, tb), lambda i: (0, 0))],
        out_specs=pl.BlockSpec((tb, LANES), lambda i: (i, 0)), out_shape=_sds((t, LANES), F32),
        scratch_shapes=[pltpu.VMEM((8, LANES), F32)], compiler_params=_params(1),
    )(proj, b_pad, tri)


def _gate_bwd(name, proj, b_pad, dc, n_heads, gate_col):
    t = proj.shape[0]
    tb = min(256, t)
    nb = t // tb
    triu = jnp.asarray(np.triu(np.ones((tb, tb), np.float32)))

    def body(p_ref, b_ref, dc_ref, tri_ref, df_ref, db_ref, carry):
        @pl.when(pl.program_id(0) == 0)
        def _():
            carry[...] = jnp.zeros_like(carry)
            db_ref[...] = jnp.zeros_like(db_ref)

        dcv = dc_ref[...]
        dlf = _dot_exact(tri_ref[...], dcv) + carry[0:1, :]
        carry[...] = jnp.broadcast_to(dlf[0:1, :], carry.shape)
        lane = lax.broadcasted_iota(jnp.int32, (tb, LANES), 1)
        z = p_ref[...] + b_ref[...]
        df = jnp.where(lane < n_heads, dlf / (1.0 + jnp.exp(z)), 0.0)
        df_ref[...] = df.astype(BF16)
        db_ref[...] += jnp.sum(df, axis=0, keepdims=True)

    return pl.pallas_call(
        body, name=name, grid=(nb,),
        in_specs=[pl.BlockSpec((tb, LANES), lambda i: (nb - 1 - i, gate_col)), pl.BlockSpec((1, LANES), lambda i: (0, 0)),
                  pl.BlockSpec((tb, LANES), lambda i: (nb - 1 - i, 0)), pl.BlockSpec((tb, tb), lambda i: (0, 0))],
        out_specs=[pl.BlockSpec((tb, LANES), lambda i: (nb - 1 - i, 0)), pl.BlockSpec((1, LANES), lambda i: (0, 0))],
        out_shape=[_sds((t, LANES), BF16), _sds((1, LANES), F32)],
        scratch_shapes=[pltpu.VMEM((8, LANES), F32)], compiler_params=_params(1),
    )(proj, b_pad, dc, triu)


def _qhead(qp, g):
    return _rms(qp, g) * (HEAD_DIM ** -0.5)


def _column(mat, idx):
    lane = lax.broadcasted_iota(jnp.int32, mat.shape, 1)
    return jnp.sum(jnp.where(lane == idx, mat, 0.0), axis=1, keepdims=True)


def _fox_scores(kk, qi, ckey, cq_i, i, bq):
    length = kk.shape[0]
    s = _dot_nt(kk, qi) + cq_i - ckey[:length]
    key = lax.broadcasted_iota(jnp.int32, (length, bq), 0)
    qry = lax.broadcasted_iota(jnp.int32, (length, bq), 1) + i * bq
    return jnp.where(key <= qry, s, NEG)


def _fox_fwd(name, proj, c, crow, gq, gk, n_heads):
    t = proj.shape[0]
    hw = n_heads * HEAD_DIM
    npair = n_heads // 2
    bq = min(256, t)
    nq = t // bq

    def body(q_ref, k_ref, v_ref, c_ref, crow_ref, gq_ref, gk_ref, o_ref, lse_ref):
        hp = pl.program_id(0)
        lse_ref[...] = jnp.zeros_like(lse_ref)
        outs = []
        for hh in range(2):
            sl = slice(hh * HEAD_DIM, (hh + 1) * HEAD_DIM)
            qn = _qhead(q_ref[:, sl], gq_ref[...]).astype(BF16)
            kn = _rms(k_ref[:, sl], gk_ref[...]).astype(BF16)
            v_t = v_ref[:, sl].T.astype(BF16)
            ckey = _column(c_ref[...], 2 * hp + hh)
            cq = crow_ref[0, hh:hh + 1, :]
            o_blocks = []
            for i in range(nq):
                cols = slice(i * bq, (i + 1) * bq)
                length = (i + 1) * bq
                s = _fox_scores(kn[:length], qn[cols], ckey, cq[:, cols], i, bq)
                m = jnp.max(s, axis=0, keepdims=True)
                p = jnp.exp(s - m)
                l = jnp.sum(p, axis=0, keepdims=True)
                o_blocks.append((_dot(v_t[:, :length], p.astype(BF16)) / l).T)
                lse_ref[0, hh:hh + 1, cols] = m + jnp.log(l)
            outs.append(jnp.concatenate(o_blocks, axis=0))
        o_ref[...] = jnp.concatenate(outs, axis=1).astype(BF16)

    col = lambda off: (lambda h: (0, off + h))
    fixed = lambda h: (0, 0)
    return pl.pallas_call(
        body, name=name, grid=(npair,),
        in_specs=[pl.BlockSpec((t, LANES), col(0)), pl.BlockSpec((t, LANES), col(npair)), pl.BlockSpec((t, LANES), col(2 * npair)),
                  pl.BlockSpec((t, LANES), fixed), pl.BlockSpec((1, 2, t), lambda h: (h, 0, 0)),
                  pl.BlockSpec((1, HEAD_DIM), fixed), pl.BlockSpec((1, HEAD_DIM), fixed)],
        out_specs=[pl.BlockSpec((t, LANES), col(0)), pl.BlockSpec((1, 8, t), lambda h: (h, 0, 0))],
        out_shape=[_sds((t, hw), BF16), _sds((npair, 8, t), F32)], compiler_params=_params(1),
    )(proj, proj, proj, c, crow, gq, gk)


def _fox_bwd(name, proj, c, crow, gq, gk, lse, do, n_heads):
    t = proj.shape[0]
    hw = n_heads * HEAD_DIM
    npair = n_heads // 2
    bq = min(256, t)
    nq = t // bq

    def body(q_ref, k_ref, v_ref, c_ref, crow_ref, gq_ref, gk_ref, lse_ref, do_ref,
             dq_ref, dk_ref, dv_ref, dc_ref, dgq_ref, dgk_ref, dk_acc, dv_acc, dc_acc):
        hp = pl.program_id(0)

        @pl.when(hp == 0)
        def _():
            dgq_ref[...] = jnp.zeros_like(dgq_ref)
            dgk_ref[...] = jnp.zeros_like(dgk_ref)

        lane = lax.broadcasted_iota(jnp.int32, (t, LANES), 1)
        dc_pair = jnp.zeros((t, LANES), F32)
        dqs, dks, dvs = [], [], []
        for hh in range(2):
            sl = slice(hh * HEAD_DIM, (hh + 1) * HEAD_DIM)
            qf, q_vjp = jax.vjp(_qhead, q_ref[:, sl], gq_ref[...])
            kf, k_vjp = jax.vjp(_rms, k_ref[:, sl], gk_ref[...])
            qn, kn, kn_t = qf.astype(BF16), kf.astype(BF16), kf.T.astype(BF16)
            vb = v_ref[:, sl].astype(BF16)
            dob = do_ref[:, sl]
            ckey = _column(c_ref[...], 2 * hp + hh)
            cq = crow_ref[0, hh:hh + 1, :]
            lse_h = lse_ref[0, hh:hh + 1, :]
            dk_acc[...] = jnp.zeros_like(dk_acc)
            dv_acc[...] = jnp.zeros_like(dv_acc)
            dc_acc[...] = jnp.zeros_like(dc_acc)
            dq_blocks = []
            for i in range(nq):
                cols = slice(i * bq, (i + 1) * bq)
                length = (i + 1) * bq
                qi, doi = qn[cols], dob[cols]
                s = _fox_scores(kn[:length], qi, ckey, cq[:, cols], i, bq)
                p = jnp.exp(s - lse_h[:, cols])
                dp = _dot_nt(vb[:length], doi)
                ds = p * (dp - jnp.sum(p * dp, axis=0, keepdims=True))
                dsb = ds.astype(BF16)
                dq_blocks.append(_dot(kn_t[:, :length], dsb).T)
                dk_acc[0:length, :] += _dot(dsb, qi)
                dv_acc[0:length, :] += _dot(p.astype(BF16), doi)
                part = ds[:, 0:LANES]
                for j in range(1, bq // LANES):
                    part = part + ds[:, j * LANES:(j + 1) * LANES]
                dc_acc[0:length, :] += part
            dqp, dgq = q_vjp(jnp.concatenate(dq_blocks, axis=0))
            dkp, dgk = k_vjp(dk_acc[...])
            dgq_ref[...] += dgq
            dgk_ref[...] += dgk
            dqs.append(dqp)
            dks.append(dkp)
            dvs.append(dv_acc[...])
            dc_pair = jnp.where(lane == hh, -jnp.sum(dc_acc[...], axis=1, keepdims=True), dc_pair)
        dq_ref[...] = jnp.concatenate(dqs, axis=1).astype(BF16)
        dk_ref[...] = jnp.concatenate(dks, axis=1).astype(BF16)
        dv_ref[...] = jnp.concatenate(dvs, axis=1).astype(BF16)
        dc_ref[...] = dc_pair

    col = lambda off: (lambda h: (0, off + h))
    fixed = lambda h: (0, 0)
    pair_blk = pl.BlockSpec((t, LANES), col(0))
    return pl.pallas_call(
        body, name=name, grid=(npair,),
        in_specs=[pl.BlockSpec((t, LANES), col(0)), pl.BlockSpec((t, LANES), col(npair)), pl.BlockSpec((t, LANES), col(2 * npair)),
                  pl.BlockSpec((t, LANES), fixed), pl.BlockSpec((1, 2, t), lambda h: (h, 0, 0)),
                  pl.BlockSpec((1, HEAD_DIM), fixed), pl.BlockSpec((1, HEAD_DIM), fixed),
                  pl.BlockSpec((1, 8, t), lambda h: (h, 0, 0)), pair_blk],
        out_specs=[pair_blk, pair_blk, pair_blk, pair_blk,
                   pl.BlockSpec((1, HEAD_DIM), fixed), pl.BlockSpec((1, HEAD_DIM), fixed)],
        out_shape=[_sds((t, hw), BF16), _sds((t, hw), BF16), _sds((t, hw), BF16), _sds((t, npair * LANES), F32),
                   _sds((1, HEAD_DIM), F32), _sds((1, HEAD_DIM), F32)],
        scratch_shapes=[pltpu.VMEM((t, HEAD_DIM), F32), pltpu.VMEM((t, HEAD_DIM), F32), pltpu.VMEM((t, LANES), F32)],
        compiler_params=_params(1),
    )(proj, proj, proj, c, crow, gq, gk, lse, do)


def _t5_bucket_table():
    dist = np.arange(WINDOW)[None, :] + WINDOW - np.arange(2 * WINDOW)[:, None]
    n = np.maximum(dist, 0)
    max_exact = N_BUCKETS // 2
    large = max_exact + (np.log(np.maximum(n, 1) / max_exact) / np.log(REL_MAX_DIST / max_exact)
                         * (N_BUCKETS - max_exact)).astype(np.int32)
    large = np.minimum(large, N_BUCKETS - 1)
    return np.where(n < max_exact, n, large).astype(np.int32).reshape(1, -1)


def _bias_expand(name, rel_bias_t):
    n_heads = rel_bias_t.shape[0]
    tbl = jnp.asarray(_t5_bucket_table())
    width = tbl.shape[1]

    def body(rb_ref, tbl_ref, o_ref):
        onehot = (lax.broadcasted_iota(jnp.int32, (N_BUCKETS, width), 0) == tbl_ref[...]).astype(F32)
        o_ref[...] = _dot_exact(rb_ref[...], onehot)

    return pl.pallas_call(body, name=name, out_shape=_sds((n_heads, width), F32), compiler_params=_params(0))(rel_bias_t, tbl)


def _bias_reduce(name, dbias):
    n_heads, width = dbias.shape
    tbl = jnp.asarray(_t5_bucket_table())

    def body(db_ref, tbl_ref, o_ref):
        onehot = (lax.broadcasted_iota(jnp.int32, (N_BUCKETS, width), 0) == tbl_ref[...]).astype(F32)
        o_ref[...] = lax.dot_general(db_ref[...], onehot, (((1,), (1,)), ((), ())), preferred_element_type=F32,
                                     precision=lax.Precision.HIGHEST)

    return pl.pallas_call(body, name=name, out_shape=_sds((n_heads, N_BUCKETS), F32), compiler_params=_params(0))(dbias, tbl)


def _swa_mask(n, group):
    j = lax.broadcasted_iota(jnp.int32, (2 * WINDOW, group * WINDOW), 0)
    i = lax.broadcasted_iota(jnp.int32, (2 * WINDOW, group * WINDOW), 1) & (WINDOW - 1)
    ok = (j > i) & (j <= i + WINDOW) & ((n > 0) | (j >= WINDOW))
    return jnp.where(ok, 0.0, NEG)


def _swa_stack(ref, start, group):
    return jnp.concatenate([ref[pl.ds(start, WINDOW), g * HEAD_DIM:(g + 1) * HEAD_DIM] for g in range(group)], axis=0)


def _swa_fwd(name, qb, kh, vh, gq, gk, sinks, bias, group):
    t = qb.shape[0]
    kvh = kh.shape[0]
    nblk = t // WINDOW
    gw = group * HEAD_DIM
    band = 2 * WINDOW
    cols = group * WINDOW

    def body(q_ref, k_ref, v_ref, gq_ref, gk_ref, sink_ref, bias_ref, o_ref, lse_ref, qs, kpad, vpad):
        for g in range(group):
            qs[:, g * HEAD_DIM:(g + 1) * HEAD_DIM] = _qhead(q_ref[:, g * HEAD_DIM:(g + 1) * HEAD_DIM], gq_ref[...]).astype(BF16)
        kpad[0:WINDOW, :] = jnp.zeros((WINDOW, HEAD_DIM), BF16)
        vpad[0:WINDOW, :] = jnp.zeros((WINDOW, HEAD_DIM), BF16)
        kpad[WINDOW:, :] = _rms(k_ref[0], gk_ref[...]).astype(BF16)
        vpad[WINDOW:, :] = v_ref[0].astype(BF16)
        sink = sink_ref[0]

        def block(n, carry):
            start = pl.multiple_of(n * WINDOW, WINDOW)
            kb = kpad[pl.ds(start, band), :]
            vb = vpad[pl.ds(start, band), :]
            s = _dot_nt(kb, _swa_stack(qs, start, group)) + bias_ref[0] + _swa_mask(n, group)
            m = jnp.maximum(jnp.max(s, axis=0, keepdims=True), sink)
            e = jnp.exp(s - m)
            l = jnp.sum(e, axis=0, keepdims=True) + jnp.exp(sink - m)
            o_t = _dot_tn(vb, e.astype(BF16)) / l
            for g in range(group):
                o_ref[pl.ds(start, WINDOW), g * HEAD_DIM:(g + 1) * HEAD_DIM] = o_t[:, g * WINDOW:(g + 1) * WINDOW].T.astype(BF16)
            lse_ref[pl.ds(n, 1), :] = m + jnp.log(l)
            return carry

        lax.fori_loop(0, nblk, block, 0)

    fixed = lambda h: (0, 0)
    per = lambda h: (h, 0, 0)
    return pl.pallas_call(
        body, name=name, grid=(kvh,),
        in_specs=[pl.BlockSpec((t, gw), lambda h: (0, h)), pl.BlockSpec((1, t, HEAD_DIM), per), pl.BlockSpec((1, t, HEAD_DIM), per),
                  pl.BlockSpec((1, HEAD_DIM), fixed), pl.BlockSpec((1, HEAD_DIM), fixed),
                  pl.BlockSpec((1, 1, cols), per), pl.BlockSpec((1, band, cols), per)],
        out_specs=[pl.BlockSpec((t, gw), lambda h: (0, h)), pl.BlockSpec((nblk, cols), lambda h: (h, 0))],
        out_shape=[_sds((t, kvh * gw), BF16), _sds((kvh * nblk, cols), F32)],
        scratch_shapes=[pltpu.VMEM((t, gw), BF16), pltpu.VMEM((t + WINDOW, HEAD_DIM), BF16),
                        pltpu.VMEM((t + WINDOW, HEAD_DIM), BF16)],
        compiler_params=_params(1),
    )(qb, kh, vh, gq, gk, sinks, bias)


def _swa_bwd(name, qb, kh, vh, gq, gk, sinks, bias, lse, do, group):
    t = qb.shape[0]
    kvh = kh.shape[0]
    nblk = t // WINDOW
    gw = group * HEAD_DIM
    band = 2 * WINDOW
    cols = group * WINDOW

    def body(q_ref, k_ref, v_ref, gq_ref, gk_ref, sink_ref, bias_ref, lse_ref, do_ref,
             dq_ref, dk_ref, dv_ref, dgq_ref, dgk_ref, dsink_ref, dbias_ref,
             qs, kpad, vpad, dqs, dk_acc, dv_acc, dsink_acc):
        @pl.when(pl.program_id(0) == 0)
        def _():
            dgq_ref[...] = jnp.zeros_like(dgq_ref)
            dgk_ref[...] = jnp.zeros_like(dgk_ref)

        for g in range(group):
            qs[:, g * HEAD_DIM:(g + 1) * HEAD_DIM] = _qhead(q_ref[:, g * HEAD_DIM:(g + 1) * HEAD_DIM], gq_ref[...]).astype(BF16)
        kpad[0:WINDOW, :] = jnp.zeros((WINDOW, HEAD_DIM), BF16)
        vpad[0:WINDOW, :] = jnp.zeros((WINDOW, HEAD_DIM), BF16)
        kpad[WINDOW:, :] = _rms(k_ref[0], gk_ref[...]).astype(BF16)
        vpad[WINDOW:, :] = v_ref[0].astype(BF16)
        dk_acc[...] = jnp.zeros_like(dk_acc)
        dv_acc[...] = jnp.zeros_like(dv_acc)
        dsink_acc[...] = jnp.zeros_like(dsink_acc)
        dbias_ref[...] = jnp.zeros_like(dbias_ref)
        sink = sink_ref[0]

        def block(n, carry):
            start = pl.multiple_of(n * WINDOW, WINDOW)
            kb = kpad[pl.ds(start, band), :]
            vb = vpad[pl.ds(start, band), :]
            q = _swa_stack(qs, start, group)
            dob = _swa_stack(do_ref, start, group)
            lse_n = lse_ref[pl.ds(n, 1), :]
            s = _dot_nt(kb, q) + bias_ref[0] + _swa_mask(n, group)
            p = jnp.exp(s - lse_n)
            dp = _dot_nt(vb, dob)
            dsum = jnp.sum(p * dp, axis=0, keepdims=True)
            ds = p * (dp - dsum)
            dsb = ds.astype(BF16)
            dsink_acc[...] -= jnp.exp(sink - lse_n) * dsum
            dbias_ref[0] += ds
            dq = _dot_tn(dsb, kb)
            for g in range(group):
                dqs[pl.ds(start, WINDOW), g * HEAD_DIM:(g + 1) * HEAD_DIM] = dq[g * WINDOW:(g + 1) * WINDOW]
            dk_acc[pl.ds(start, band), :] += _dot(dsb, q)
            dv_acc[pl.ds(start, band), :] += _dot(p.astype(BF16), dob)
            return carry

        lax.fori_loop(0, nblk, block, 0)
        for g in range(group):
            _, q_vjp = jax.vjp(_qhead, q_ref[:, g * HEAD_DIM:(g + 1) * HEAD_DIM], gq_ref[...])
            dqp, dgq = q_vjp(dqs[:, g * HEAD_DIM:(g + 1) * HEAD_DIM])
            dq_ref[:, g * HEAD_DIM:(g + 1) * HEAD_DIM] = dqp.astype(BF16)
            dgq_ref[...] += dgq
            dsink_g = jnp.sum(dsink_acc[:, g * WINDOW:(g + 1) * WINDOW], axis=1, keepdims=True)
            dsink_ref[0, g:g + 1, :] = jnp.broadcast_to(dsink_g, (1, LANES))
        _, k_vjp = jax.vjp(_rms, k_ref[0], gk_ref[...])
        dkp, dgk = k_vjp(dk_acc[WINDOW:, :])
        dk_ref[0] = dkp
        dgk_ref[...] += dgk
        dv_ref[0] = dv_acc[WINDOW:, :]

    fixed = lambda h: (0, 0)
    per = lambda h: (h, 0, 0)
    wide = pl.BlockSpec((t, gw), lambda h: (0, h))
    head = pl.BlockSpec((1, t, HEAD_DIM), per)
    vec = pl.BlockSpec((1, HEAD_DIM), fixed)
    bias_spec = pl.BlockSpec((1, band, cols), per)
    return pl.pallas_call(
        body, name=name, grid=(kvh,),
        in_specs=[wide, head, head, vec, vec, pl.BlockSpec((1, 1, cols), per), bias_spec,
                  pl.BlockSpec((nblk, cols), lambda h: (h, 0)), wide],
        out_specs=[wide, head, head, vec, vec, pl.BlockSpec((1, group, LANES), per), bias_spec],
        out_shape=[_sds((t, kvh * gw), BF16), _sds((kvh, t, HEAD_DIM), F32), _sds((kvh, t, HEAD_DIM), F32),
                   _sds((1, HEAD_DIM), F32), _sds((1, HEAD_DIM), F32),
                   _sds((kvh, group, LANES), F32), _sds((kvh, band, cols), F32)],
        scratch_shapes=[pltpu.VMEM((t, gw), BF16), pltpu.VMEM((t + WINDOW, HEAD_DIM), BF16),
                        pltpu.VMEM((t + WINDOW, HEAD_DIM), BF16), pltpu.VMEM((t, gw), F32),
                        pltpu.VMEM((t + WINDOW, HEAD_DIM), F32), pltpu.VMEM((t + WINDOW, HEAD_DIM), F32),
                        pltpu.VMEM((1, cols), F32)],
        compiler_params=_params(1),
    )(qb, kh, vh, gq, gk, sinks, bias, lse, do)


def _local_step(x, target, p, comm):
    t, d = x.shape
    n_heads = d // HEAD_DIM
    kv_heads = n_heads // 8
    group = n_heads // kv_heads
    hw = n_heads * HEAD_DIM
    gate_col = 3 * hw // LANES
    kvw = kv_heads * HEAD_DIM
    grads = {}

    def mlp_fwd(tag, h, g, layer):
        w_up, = comm.weights([f"w_up{layer}"], h)
        a, hn = _norm_matmul(f"{tag}_up", h, g, w_up, relu2=True)
        w_down, = comm.weights([f"w_down{layer}"], a)
        return _matmul_res(f"{tag}_down", a, w_down, h), (h, g, hn, a, w_up, w_down)

    def mlp_bwd(tag, saved, layer, dy):
        h, g, hn, a, w_up, w_down = saved
        du = _matmul_nt(f"{tag}_du", dy, w_down, a=a)
        dw_down = _matmul_tn(f"{tag}_dwdown", a, dy)
        dw_up = _matmul_tn(f"{tag}_dwup", hn, du, col_blocks=w_up.shape[0])
        zero = comm.send_grads(tag, {f"w_down{layer}": dw_down, f"w_up{layer}": dw_up})
        return _matmul_nt_rmsbwd(f"{tag}_dh", du, w_up, h, g + zero, dy)

    comm.prefetch(["w_in_a"], None)
    w_in, = comm.weights(["w_in_a"], None)
    proj, xn1 = _norm_matmul("a_inproj", x, p["g_attn"][0], w_in, tn=640)
    comm.prefetch(["w_out_a"], proj)
    b_pad = jnp.pad(p["b_f"], ((0, 0), (0, LANES - n_heads)))
    c = _gate_fwd("a_gate", proj, b_pad, n_heads, gate_col)
    comm.prefetch(["w_up0"], c)
    crow = c[:, :n_heads].T.reshape(n_heads // 2, 2, t)
    o_a, lse_a = _fox_fwd("a_attn", proj, c, crow, p["gq_a"], p["gk_a"], n_heads)
    comm.prefetch(["w_down0", "w_kv", "w_q_b", "w_out_b"], o_a)
    w_out_a, = comm.weights(["w_out_a"], o_a)
    h1 = _matmul_res("a_outproj", o_a, w_out_a, x)
    h2, mlp0 = mlp_fwd("mlp0", h1, p["g_mlp"][0], 0)

    comm.prefetch(["w_up1", "w_down1"], h2)
    w_kv, w_q_b = comm.weights(["w_kv", "w_q_b"], h2)
    kv, hn_kv = _norm_matmul("kv_proj", h2, p["g_kv"], w_kv, tn=2 * kvw)
    kh = kv[:, :kvw].reshape(t, kv_heads, HEAD_DIM).transpose(1, 0, 2)
    vh = kv[:, kvw:].reshape(t, kv_heads, HEAD_DIM).transpose(1, 0, 2)
    qb, hn_q = _norm_matmul("b_qproj", h2, p["g_attn"][1], w_q_b, tn=512)
    gqb, gkb = p["gq_b"], p["gk_b"].reshape(1, HEAD_DIM)
    bias = _bias_expand("b_bias", p["rel_bias"].T).reshape(kv_heads, group, 2 * WINDOW, WINDOW)
    bias = bias.transpose(0, 2, 1, 3).reshape(kv_heads, 2 * WINDOW, group * WINDOW)
    sink_rows = jnp.broadcast_to(p["sinks"].reshape(kv_heads, 1, group, 1), (kv_heads, 1, group, WINDOW)).reshape(kv_heads, 1, group * WINDOW)
    o_b, lse_b = _swa_fwd("b_attn", qb, kh, vh, gqb, gkb, sink_rows, bias, group)
    w_out_b, = comm.weights(["w_out_b"], o_b)
    h3 = _matmul_res("b_outproj", o_b, w_out_b, h2)
    y, mlp1 = mlp_fwd("mlp1", h3, p["g_mlp"][1], 1)
    dy, loss_tile = _loss_grad("loss", y, target)

    dh3, dg_mlp1 = mlp_bwd("mlp1", mlp1, 1, dy)
    do_b = _matmul_nt("b_do", dh3, w_out_b)
    dw_out_b = _matmul_tn("b_dwout", o_b, dh3)
    dqb, dkh, dvh, grads["gq_b"], dgk_b, dsink, dbias = _swa_bwd(
        "b_attn_bwd", qb, kh, vh, gqb, gkb, sink_rows, bias, lse_b, do_b, group)
    grads["gk_b"] = dgk_b.reshape(HEAD_DIM)
    grads["sinks"] = dsink[:, :, 0].reshape(1, n_heads)
    dbias = dbias.reshape(kv_heads, 2 * WINDOW, group, WINDOW).transpose(0, 2, 1, 3)
    grads["rel_bias"] = _bias_reduce("b_dbias", dbias.reshape(n_heads, WINDOW * 2 * WINDOW)).T
    dw_q_b = _matmul_tn("b_dwq", hn_q, dqb)
    dh2, dg_attn1 = _matmul_nt_rmsbwd("b_dhq", dqb, w_q_b, h2, p["g_attn"][1], dh3)
    dkv = jnp.concatenate([dkh.transpose(1, 0, 2).reshape(t, kvw), dvh.transpose(1, 0, 2).reshape(t, kvw)], axis=1)
    dw_kv = _matmul_tn("kv_dw", hn_kv, dkv)
    zero = comm.send_grads("attn_b", {"w_out_b": dw_out_b, "w_q_b": dw_q_b, "w_kv": dw_kv})
    dh2, dg_kv = _matmul_nt_rmsbwd("kv_dh", dkv, w_kv, h2, p["g_kv"] + zero, dh2)
    grads["g_kv"] = dg_kv.reshape(d)
    dh1, dg_mlp0 = mlp_bwd("mlp0", mlp0, 0, dh2)
    grads["g_mlp"] = jnp.concatenate([dg_mlp0, dg_mlp1], axis=0)

    do_a = _matmul_nt("a_do", dh1, w_out_a)
    dw_out_a = _matmul_tn("a_dwout", o_a, dh1)
    zero = comm.send_grads("attn_a_out", {"w_out_a": dw_out_a})
    dq, dk, dv, dc_cols, grads["gq_a"], grads["gk_a"] = _fox_bwd(
        "a_attn_bwd", proj, c, crow, p["gq_a"] + zero, p["gk_a"], lse_a, do_a, n_heads)
    dc = jnp.pad(dc_cols.reshape(t, n_heads // 2, LANES)[:, :, :2].reshape(t, n_heads), ((0, 0), (0, LANES - n_heads)))
    dfl, db_f = _gate_bwd("a_gate_bwd", proj, b_pad, dc, n_heads, gate_col)
    grads["b_f"] = db_f[:, :n_heads]
    dproj = jnp.concatenate([dq, dk, dv, dfl], axis=1)
    dw_in = _matmul_tn("a_dwin", xn1, dproj, tn=640)
    zero = comm.send_grads("attn_a_in", {"w_in_a": dw_in})
    grad_x, dg_attn0 = _matmul_nt_rmsbwd("a_dx", dproj, w_in, x, p["g_attn"][0] + zero, dh1)
    grads["g_attn"] = jnp.concatenate([dg_attn0, dg_attn1], axis=0)
    return loss_tile, grad_x, grads


EVERYONE = (1, 2, 3, 4, 5, 6, 7)
SAME_CORE = (1, 2, 4, 6)
OTHER_CHIPS = (2, 4, 6)


class _InFlight:
    def __init__(self, scatter, ks, send_sems, recv_sems, srcs, lands, token):
        self.scatter, self.ks, self.send_sems, self.recv_sems = scatter, ks, send_sems, recv_sems
        self.srcs, self.lands, self.token = list(srcs), list(lands), token


def _mesh_peers(ks=EVERYONE):
    x, y, c = lax.axis_index("x"), lax.axis_index("y"), lax.axis_index("c")
    peers = []
    for k in ks:
        px, py, pc = x ^ ((k >> 2) & 1), y ^ ((k >> 1) & 1), c ^ (k & 1)
        peers.append(((px, py, pc), 4 * px + 2 * py + pc))
    return 4 * x + 2 * y + c, peers


_HBM_SPEC = pl.BlockSpec(memory_space=pltpu.HBM)
_SEM_SPEC = pl.BlockSpec(memory_space=pltpu.SEMAPHORE)
_SIDE_EFFECT = pltpu.SideEffectType.DATAFLOW_SIDE_EFFECTING


def _exchange_start(name, arrays, scatter, ks=EVERYONE):
    n = len(arrays)
    me, _ = _mesh_peers()
    lands = []
    for a in arrays:
        own = lax.dynamic_index_in_dim(a, me, 0, keepdims=False) if scatter else a
        shape = a.shape if scatter else (N_DEV,) + a.shape
        lands.append(lax.dynamic_update_index_in_dim(lax.empty(shape, a.dtype), own, me, 0))

    def body(*refs):
        src, land = refs[:n], refs[n:2 * n]
        send_sems, recv_sems, token = refs[2 * n], refs[2 * n + 1], refs[-1]
        pos, peers = _mesh_peers(ks)
        for a in range(n):
            for k, (peer, peer_pos) in enumerate(peers):
                pltpu.make_async_remote_copy(
                    src_ref=src[a].at[peer_pos] if scatter else src[a], dst_ref=land[a].at[pos],
                    send_sem=send_sems.at[a * len(ks) + k], recv_sem=recv_sems.at[a * len(ks) + k],
                    device_id=peer, device_id_type=pl.DeviceIdType.MESH).start()
        token[...] = jnp.zeros_like(token)

    operands = [pltpu.with_memory_space_constraint(a, pltpu.HBM) for a in list(arrays) + lands]
    outs = pl.pallas_call(
        body, name=name,
        out_shape=(pltpu.SemaphoreType.DMA((n * len(ks),)), pltpu.SemaphoreType.DMA((n * len(ks),)),
                   *[pltpu.HBM(a.shape, a.dtype) for a in operands], _sds((8, LANES), F32)),
        in_specs=[_HBM_SPEC] * (2 * n),
        out_specs=(_SEM_SPEC, _SEM_SPEC, *[_HBM_SPEC] * (2 * n), pl.BlockSpec(memory_space=pltpu.VMEM)),
        input_output_aliases={i: 2 + i for i in range(2 * n)},
        compiler_params=pltpu.CompilerParams(has_side_effects=_SIDE_EFFECT),
    )(*operands)
    return _InFlight(scatter, ks, outs[0], outs[1], outs[2:2 + n], outs[2 + n:2 + 2 * n], outs[-1])


def _exchange_wait(name, flight, which, after):
    m = len(which)
    scatter, ks = flight.scatter, flight.ks

    def body(*refs):
        src, land = refs[:m], refs[m:2 * m]
        send_sems, recv_sems = refs[2 * m], refs[2 * m + 1]
        _, peers = _mesh_peers(ks)
        for i, a in enumerate(which):
            for k, (peer, peer_pos) in enumerate(peers):
                cp = pltpu.make_async_remote_copy(
                    src_ref=src[i].at[peer_pos] if scatter else src[i], dst_ref=land[i].at[peer_pos],
                    send_sem=send_sems.at[a * len(ks) + k], recv_sem=recv_sems.at[a * len(ks) + k],
                    device_id=peer, device_id_type=pl.DeviceIdType.MESH)
                cp.wait_send()
                cp.wait_recv()

    operands = [flight.srcs[a] for a in which] + [flight.lands[a] for a in which]
    outs = pl.pallas_call(
        body, name=name, out_shape=tuple(pltpu.HBM(a.shape, a.dtype) for a in operands),
        in_specs=[_HBM_SPEC] * (2 * m) + [_SEM_SPEC, _SEM_SPEC, pl.BlockSpec(memory_space=pl.ANY)],
        out_specs=tuple([_HBM_SPEC] * (2 * m)), input_output_aliases={i: i for i in range(2 * m)},
        compiler_params=pltpu.CompilerParams(has_side_effects=_SIDE_EFFECT),
    )(*operands, flight.send_sems, flight.recv_sems, after)
    return list(outs[m:])


def _relay_start(name, lands):
    n = len(lands)

    def body(*refs):
        land, send_sems, recv_sems, token = refs[:n], refs[n], refs[n + 1], refs[-1]
        _, peers = _mesh_peers(OTHER_CHIPS)
        sibling = (lax.axis_index("x"), lax.axis_index("y"), 1 - lax.axis_index("c"))
        for a in range(n):
            for k, (_, peer_pos) in enumerate(peers):
                pltpu.make_async_remote_copy(
                    src_ref=land[a].at[peer_pos], dst_ref=land[a].at[peer_pos],
                    send_sem=send_sems.at[a * len(peers) + k], recv_sem=recv_sems.at[a * len(peers) + k],
                    device_id=sibling, device_id_type=pl.DeviceIdType.MESH).start()
        token[...] = jnp.zeros_like(token)

    count = n * len(OTHER_CHIPS)
    outs = pl.pallas_call(
        body, name=name,
        out_shape=(pltpu.SemaphoreType.DMA((count,)), pltpu.SemaphoreType.DMA((count,)),
                   *[pltpu.HBM(a.shape, a.dtype) for a in lands], _sds((8, LANES), F32)),
        in_specs=[_HBM_SPEC] * n,
        out_specs=(_SEM_SPEC, _SEM_SPEC, *[_HBM_SPEC] * n, pl.BlockSpec(memory_space=pltpu.VMEM)),
        input_output_aliases={i: 2 + i for i in range(n)},
        compiler_params=pltpu.CompilerParams(has_side_effects=_SIDE_EFFECT),
    )(*[pltpu.with_memory_space_constraint(a, pltpu.HBM) for a in lands])
    return _InFlight(False, OTHER_CHIPS, outs[0], outs[1], [], outs[2:2 + n], outs[-1])


def _relay_wait(name, flight, after):
    n = len(flight.lands)

    def body(*refs):
        land, send_sems, recv_sems = refs[:n], refs[n], refs[n + 1]
        _, peers = _mesh_peers(OTHER_CHIPS)
        sibling = (lax.axis_index("x"), lax.axis_index("y"), 1 - lax.axis_index("c"))
        for a in range(n):
            for k, (_, peer_pos) in enumerate(peers):
                cp = pltpu.make_async_remote_copy(
                    src_ref=land[a].at[peer_pos], dst_ref=land[a].at[peer_pos ^ 1],
                    send_sem=send_sems.at[a * len(peers) + k], recv_sem=recv_sems.at[a * len(peers) + k],
                    device_id=sibling, device_id_type=pl.DeviceIdType.MESH)
                cp.wait_send()
                cp.wait_recv()

    outs = pl.pallas_call(
        body, name=name, out_shape=tuple(pltpu.HBM(a.shape, a.dtype) for a in flight.lands),
        in_specs=[_HBM_SPEC] * n + [_SEM_SPEC, _SEM_SPEC, pl.BlockSpec(memory_space=pl.ANY)],
        out_specs=tuple([_HBM_SPEC] * n), input_output_aliases={i: i for i in range(n)},
        compiler_params=pltpu.CompilerParams(has_side_effects=_SIDE_EFFECT),
    )(*flight.lands, flight.send_sems, flight.recv_sems, after)
    return list(outs)


def _adamw(name, parts, w, m, v, layer=None, into=None):
    r, c = w.shape[-2:]
    tr = r if r <= 256 else 256
    n_into = 0 if into is None else len(into)

    def body(p_ref, w_ref, m_ref, v_ref, *refs):
        g_ref, d_ref, mo_ref, vo_ref = refs[n_into:]
        g = p_ref[0].astype(F32)
        for dev in range(1, N_DEV):
            g = g + p_ref[dev].astype(F32)
        m_new = ADAM_B1 * m_ref[...] + (1.0 - ADAM_B1) * g
        v_new = ADAM_B2 * v_ref[...] + (1.0 - ADAM_B2) * jnp.square(g)
        m_hat = m_new / (1.0 - ADAM_B1 ** ADAM_STEP)
        v_hat = v_new / (1.0 - ADAM_B2 ** ADAM_STEP)
        g_ref[...] = g
        d_ref[...] = -ADAM_LR * (m_hat / (jnp.sqrt(v_hat) + ADAM_EPS) + ADAM_WD * w_ref[...])
        mo_ref[...] = m_new
        vo_ref[...] = v_new

    if layer is None:
        blk = pl.BlockSpec((tr, c), lambda i: (i, 0))
    else:
        blk = pl.BlockSpec((None, tr, c), lambda i: (layer, i, 0))
    return pl.pallas_call(
        body, name=name, grid=(r // tr,),
        in_specs=[pl.BlockSpec((N_DEV, tr, c), lambda i: (0, i, 0)), blk, blk, blk] + [pl.BlockSpec(memory_space=pl.ANY)] * n_into,
        out_specs=[blk] * 4, out_shape=[_sds(w.shape, F32)] * 4,
        input_output_aliases={4 + i: i for i in range(n_into)}, compiler_params=_params(1),
    )(parts, w, m, v, *(into or ()))


def _pack_small(tree, last):
    flat = jnp.concatenate([tree[k].reshape(-1) for k in SMALL] + [last])
    size = -(-flat.shape[0] // (8 * LANES)) * (8 * LANES)
    return jnp.pad(flat, (0, size - flat.shape[0])).reshape(-1, LANES)


def _unpack_small(packed, like):
    flat, out, off = packed.reshape(-1), {}, 0
    for k in SMALL:
        size = like[k].size
        out[k] = flat[off:off + size].reshape(like[k].shape)
        off += size
    return out


class _Comm:
    ORDER = ("w_in_a", "w_out_a", "w_up0", "w_down0", "w_kv", "w_q_b", "w_out_b", "w_up1", "w_down1")

    def __init__(self, shards, d, n_in):
        self.d, self.n_in = d, n_in
        self.flight = _exchange_start("gather_start", [shards[n].astype(BF16) for n in self.ORDER], scatter=False, ks=SAME_CORE)
        self.relays, self.ready, self.sent = {}, {}, []

    def prefetch(self, names, after):
        which = [self.ORDER.index(n) for n in names]
        landed = _exchange_wait(f"gather_wait_{names[0]}", self.flight, which, self.flight.token if after is None else after)
        relay = _relay_start(f"gather_relay_{names[0]}", landed)
        for n in names:
            self.relays[n] = (relay, names)

    def weights(self, names, after):
        for n in names:
            if n not in self.ready:
                relay, group = self.relays[n]
                landed = _relay_wait(f"gather_relay_wait_{group[0]}", relay, relay.token if after is None else after)
                self.ready.update({m: self._whole(m, g) for m, g in zip(group, landed)})
        return [self.ready[n] for n in names]

    def _whole(self, name, g):
        if name == "w_in_a":
            pad = -(-self.n_in // LANES) * LANES - self.n_in
            return jnp.pad(g.transpose(1, 0, 2).reshape(self.d, self.n_in), ((0, 0), (0, pad)))
        if name.startswith("w_up"):
            return g
        return g.reshape(-1, g.shape[-1])

    def _chunks(self, name, g):
        if name == "w_in_a":
            return g[:, :self.n_in].reshape(self.d, N_DEV, -1).transpose(1, 0, 2)
        if name.startswith("w_up"):
            return g
        return g.reshape(N_DEV, g.shape[0] // N_DEV, g.shape[1])

    def send_grads(self, tag, partials):
        names = list(partials)
        flight = _exchange_start(f"scatter_start_{tag}", [self._chunks(n, partials[n]) for n in names], scatter=True)
        self.sent.append((tag, flight, names))
        return flight.token[0, 0]

    def received(self, index, after):
        tag, flight, names = self.sent[index]
        landed = _exchange_wait(f"scatter_wait_{tag}", flight, list(range(len(names))), after)
        return dict(zip(names, landed))


def kernel(x, g_attn, g_mlp, w_in_a, b_f, gq_a, gk_a, w_out_a, g_kv, w_kv, gk_b, w_q_b, gq_b, sinks, rel_bias, w_out_b, w_up, w_down, loss_target, m_g_attn, m_g_mlp, m_w_in_a, m_b_f, m_gq_a, m_gk_a, m_w_out_a, m_g_kv, m_w_kv, m_gk_b, m_w_q_b, m_gq_b, m_sinks, m_rel_bias, m_w_out_b, m_w_up, m_w_down, v_g_attn, v_g_mlp, v_w_in_a, v_b_f, v_gq_a, v_gk_a, v_w_out_a, v_g_kv, v_w_kv, v_gk_b, v_w_q_b, v_gq_b, v_sinks, v_rel_bias, v_w_out_b, v_w_up, v_w_down):
    w = dict(g_attn=g_attn, g_mlp=g_mlp, w_in_a=w_in_a, b_f=b_f, gq_a=gq_a, gk_a=gk_a, w_out_a=w_out_a, g_kv=g_kv,
             w_kv=w_kv, gk_b=gk_b, w_q_b=w_q_b, gq_b=gq_b, sinks=sinks, rel_bias=rel_bias, w_out_b=w_out_b,
             w_up=w_up, w_down=w_down)
    mom = dict(g_attn=m_g_attn, g_mlp=m_g_mlp, w_in_a=m_w_in_a, b_f=m_b_f, gq_a=m_gq_a, gk_a=m_gk_a, w_out_a=m_w_out_a,
               g_kv=m_g_kv, w_kv=m_w_kv, gk_b=m_gk_b, w_q_b=m_w_q_b, gq_b=m_gq_b, sinks=m_sinks, rel_bias=m_rel_bias,
               w_out_b=m_w_out_b, w_up=m_w_up, w_down=m_w_down)
    var = dict(g_attn=v_g_attn, g_mlp=v_g_mlp, w_in_a=v_w_in_a, b_f=v_b_f, gq_a=v_gq_a, gk_a=v_gk_a, w_out_a=v_w_out_a,
               g_kv=v_g_kv, w_kv=v_w_kv, gk_b=v_gk_b, w_q_b=v_w_q_b, gq_b=v_gq_b, sinks=v_sinks, rel_bias=v_rel_bias,
               w_out_b=v_w_out_b, w_up=v_w_up, w_down=v_w_down)
    d = x.shape[2]
    where = {"w_in_a": ("w_in_a", 0), "w_out_a": ("w_out_a", 0), "w_kv": ("w_kv", None), "w_q_b": ("w_q_b", 0),
             "w_out_b": ("w_out_b", 0), "w_up0": ("w_up", 0), "w_up1": ("w_up", 1), "w_down0": ("w_down", 0),
             "w_down1": ("w_down", 1)}
    shards = {n: (w[k] if layer is None else w[k][layer]) for n, (k, layer) in where.items()}
    comm = _Comm(shards, d, w_in_a.shape[2] * N_DEV)
    loss_tile, grad_x, grads = _local_step(x[0], loss_target[0], {k: w[k] for k in SMALL}, comm)

    zero = jnp.zeros((1,), F32)
    small_flight = _exchange_start("gather_small_grads", [_pack_small(grads, loss_tile[0, 0:1])], scatter=False)
    res, after = {}, small_flight.token
    for index in range(len(comm.sent)):
        for n, parts in comm.received(index, after).items():
            k, layer = where[n]
            res[k] = _adamw(f"adam_{n}", parts, w[k], mom[k], var[k], layer, res.get(k))
            after = res[k][0]
    small_parts, = _exchange_wait("gather_small_wait", small_flight, [0], after)
    small = _adamw("adam_small", small_parts, _pack_small(w, zero), _pack_small(mom, zero), _pack_small(var, zero))
    loss = small[0].reshape(-1)[sum(w[k].size for k in SMALL)]
    small = [_unpack_small(s, w) for s in small]
    for k in SMALL:
        res[k] = [s[k] for s in small]

    outs = [loss, grad_x[None]]
    for i in range(4):
        outs.extend(res[k][i] for k in WEIGHTS)
    return tuple(outs)
```

```python
import numpy as np
import jax
import jax.numpy as jnp
from jax import lax
from jax.experimental import pallas as pl
from jax.experimental.pallas import tpu as pltpu

F32 = jnp.float32
BF16 = jnp.bfloat16

N_DEV = 8
HEAD_DIM = 64
WINDOW = 128
N_BUCKETS = 32
REL_MAX_DIST = 128
NORM_EPS = 1e-6
NEG = -1e30
LANES = 128
VMEM_LIMIT = 56 * 1024 * 1024

ADAM_LR = 0.001
ADAM_B1 = 0.9
ADAM_B2 = 0.999
ADAM_EPS = 1e-08
ADAM_WD = 0.01
ADAM_STEP = 10

SMALL = ("g_attn", "g_mlp", "b_f", "gq_a", "gk_a", "g_kv", "gk_b", "gq_b", "sinks", "rel_bias")
WEIGHTS = ("g_attn", "g_mlp", "w_in_a", "b_f", "gq_a", "gk_a", "w_out_a", "g_kv", "w_kv", "gk_b",
           "w_q_b", "gq_b", "sinks", "rel_bias", "w_out_b", "w_up", "w_down")


def _params(n_grid):
    return pltpu.CompilerParams(dimension_semantics=("arbitrary",) * n_grid, vmem_limit_bytes=VMEM_LIMIT)


def _sds(shape, dtype):
    return jax.ShapeDtypeStruct(tuple(shape), dtype)


def _rms(x, g):
    return (x * lax.rsqrt(jnp.mean(x * x, axis=-1, keepdims=True) + NORM_EPS)) * g


def _dot_nt(a, b):
    return lax.dot_general(a, b, (((1,), (1,)), ((), ())), preferred_element_type=F32)


def _dot_tn(a, b):
    return lax.dot_general(a, b, (((0,), (0,)), ((), ())), preferred_element_type=F32)


def _dot(a, b):
    return jnp.dot(a, b, preferred_element_type=F32)


def _dot_exact(a, b):
    return jnp.dot(a, b, preferred_element_type=F32, precision=lax.Precision.HIGHEST)


def _norm_matmul(name, x, g, w, *, tn=None, relu2=False):
    t, d = x.shape
    blocked = w.ndim == 3
    if blocked:
        tn = w.shape[2]
        n = w.shape[0] * tn
        w_spec = pl.BlockSpec((None, d, tn), lambda i, j: (j, 0, 0))
    else:
        n = w.shape[1]
        w_spec = pl.BlockSpec((d, tn), lambda i, j: (0, j))
    tm = min(1024, t)

    def body(x_ref, g_ref, w_ref, y_ref, xn_ref):
        @pl.when(pl.program_id(1) == 0)
        def _():
            xn_ref[...] = _rms(x_ref[...], g_ref[...]).astype(BF16)

        y = _dot(xn_ref[...], w_ref[...])
        y_ref[...] = jnp.square(jnp.maximum(y, 0.0)).astype(BF16) if relu2 else y

    out_shape = [_sds((t, n), BF16 if relu2 else F32), _sds((t, d), BF16)]
    out_specs = [pl.BlockSpec((tm, tn), lambda i, j: (i, j)), pl.BlockSpec((tm, d), lambda i, j: (i, 0))]
    return pl.pallas_call(
        body, name=name, grid=(t // tm, n // tn),
        in_specs=[pl.BlockSpec((tm, d), lambda i, j: (i, 0)), pl.BlockSpec((1, d), lambda i, j: (0, 0)), w_spec],
        out_specs=out_specs, out_shape=out_shape, compiler_params=_params(2),
    )(x, g.reshape(1, d), w)


def _matmul_res(name, a, w, res, *, tn=512, after=None):
    t, k = a.shape
    n = w.shape[1]
    tm = min(1024, t)

    def body(a_ref, w_ref, r_ref, *rest):
        rest[-1][...] = r_ref[...] + _dot(a_ref[...], w_ref[...])

    extra = [] if after is None else [after]
    return pl.pallas_call(
        body, name=name, grid=(t // tm, n // tn),
        in_specs=[pl.BlockSpec((tm, k), lambda i, j: (i, 0)), pl.BlockSpec((k, tn), lambda i, j: (0, j)),
                  pl.BlockSpec((tm, tn), lambda i, j: (i, j))] + [pl.BlockSpec((8, LANES), lambda i, j: (0, 0))] * len(extra),
        out_specs=pl.BlockSpec((tm, tn), lambda i, j: (i, j)), out_shape=_sds((t, n), F32),
        compiler_params=_params(2),
    )(a, w, res, *extra)


def _matmul_nt(name, dy, w, *, a=None, tk=1024):
    t, n = dy.shape
    k = w.shape[0]
    tm = min(1024, t)

    def body(dy_ref, w_ref, *rest):
        o_ref = rest[-1]
        r = _dot_nt(dy_ref[...].astype(BF16), w_ref[...])
        if a is not None:
            r = r * (2.0 * jnp.sqrt(rest[0][...].astype(F32)))
        o_ref[...] = r.astype(BF16)

    in_specs = [pl.BlockSpec((tm, n), lambda i, j: (i, 0)), pl.BlockSpec((tk, n), lambda i, j: (j, 0))]
    args = [dy, w]
    if a is not None:
        in_specs.append(pl.BlockSpec((tm, tk), lambda i, j: (i, j)))
        args.append(a)
    return pl.pallas_call(
        body, name=name, grid=(t // tm, k // tk), in_specs=in_specs,
        out_specs=pl.BlockSpec((tm, tk), lambda i, j: (i, j)), out_shape=_sds((t, k), BF16),
        compiler_params=_params(2),
    )(*args)


def _matmul_nt_rmsbwd(name, dy, w, x, g, dres):
    t, k = dy.shape
    blocked = w.ndim == 3
    d = w.shape[1] if blocked else w.shape[0]
    tm = min(512, t)

    def body(dy_ref, w_ref, x_ref, g_ref, r_ref, dx_ref, dg_ref):
        if blocked:
            kb = w.shape[2]
            dxn = _dot_nt(dy_ref[:, 0:kb].astype(BF16), w_ref[0])
            for j in range(1, w.shape[0]):
                dxn += _dot_nt(dy_ref[:, j * kb:(j + 1) * kb].astype(BF16), w_ref[j])
        else:
            dxn = _dot_nt(dy_ref[...].astype(BF16), w_ref[...])
        _, vjp = jax.vjp(_rms, x_ref[...], g_ref[...])
        dx, dg = vjp(dxn)
        dx_ref[...] = r_ref[...] + dx

        @pl.when(pl.program_id(0) == 0)
        def _():
            dg_ref[...] = jnp.zeros_like(dg_ref)

        dg_ref[...] += dg

    row = lambda i: (i, 0)
    fixed = lambda i: (0, 0)
    return pl.pallas_call(
        body, name=name, grid=(t // tm,),
        in_specs=[pl.BlockSpec((tm, k), row), pl.BlockSpec(w.shape, (lambda i: (0, 0, 0)) if blocked else fixed),
                  pl.BlockSpec((tm, d), row), pl.BlockSpec((1, d), fixed), pl.BlockSpec((tm, d), row)],
        out_specs=[pl.BlockSpec((tm, d), row), pl.BlockSpec((1, d), fixed)],
        out_shape=[_sds((t, d), F32), _sds((1, d), F32)], compiler_params=_params(1),
    )(dy, w, x, g.reshape(1, d), dres)


def _matmul_tn(name, a, b, *, tk=1024, tn=1024, col_blocks=None):
    t, k = a.shape
    n = b.shape[1]
    tk = min(tk, k)
    if col_blocks:
        tn = n // col_blocks
        out_spec, out_shape = pl.BlockSpec((None, tk, tn), lambda i, j: (j, i, 0)), _sds((col_blocks, k, tn), BF16)
    else:
        tn = min(tn, n)
        out_spec, out_shape = pl.BlockSpec((tk, tn), lambda i, j: (i, j)), _sds((k, n), BF16)

    def body(a_ref, b_ref, o_ref):
        o_ref[...] = _dot_tn(a_ref[...].astype(BF16), b_ref[...].astype(BF16)).astype(BF16)

    return pl.pallas_call(
        body, name=name, grid=(k // tk, n // tn),
        in_specs=[pl.BlockSpec((t, tk), lambda i, j: (0, i)), pl.BlockSpec((t, tn), lambda i, j: (0, j))],
        out_specs=out_spec, out_shape=out_shape, compiler_params=_params(2),
    )(a, b)


def _loss_grad(name, y, target):
    t, d = y.shape
    tm = min(512, t)

    def body(y_ref, t_ref, dy_ref, l_ref):
        e = y_ref[...] - t_ref[...]
        dy_ref[...] = e * (1.0 / d)

        @pl.when(pl.program_id(0) == 0)
        def _():
            l_ref[...] = jnp.zeros_like(l_ref)

        l_ref[...] += 0.5 * jnp.sum(jnp.mean(e * e, axis=-1))

    row = lambda i: (i, 0)
    return pl.pallas_call(
        body, name=name, grid=(t // tm,), in_specs=[pl.BlockSpec((tm, d), row), pl.BlockSpec((tm, d), row)],
        out_specs=[pl.BlockSpec((tm, d), row), pl.BlockSpec((8, LANES), lambda i: (0, 0))],
        out_shape=[_sds((t, d), F32), _sds((8, LANES), F32)], compiler_params=_params(1),
    )(y, target)


def _gate_fwd(name, proj, b_pad, n_heads, gate_col):
    t = proj.shape[0]
    tb = min(256, t)
    tri = jnp.asarray(np.tril(np.ones((tb, tb), np.float32)))

    def body(p_ref, b_ref, tri_ref, c_ref, carry):
        @pl.when(pl.program_id(0) == 0)
        def _():
            carry[...] = jnp.zeros_like(carry)

        lane = lax.broadcasted_iota(jnp.int32, (tb, LANES), 1)
        lf = jnp.where(lane < n_heads, jax.nn.log_sigmoid(p_ref[...] + b_ref[...]), 0.0)
        c = _dot_exact(tri_ref[...], lf) + carry[0:1, :]
        c_ref[...] = c
        carry[...] = jnp.broadcast_to(c[tb - 1:tb, :], carry.shape)

    return pl.pallas_call(
        body, name=name, grid=(t // tb,),
        in_specs=[pl.BlockSpec((tb, LANES), lambda i: (i, gate_col)), pl.BlockSpec((1, LANES), lambda i: (0, 0)),
                  pl.BlockSpec((tb, tb), lambda i: (0, 0))],
        out_specs=pl.BlockSpec((tb, LANES), lambda i: (i, 0)), out_shape=_sds((t, LANES), F32),
        scratch_shapes=[pltpu.VMEM((8, LANES), F32)], compiler_params=_params(1),
    )(proj, b_pad, tri)


def _gate_bwd(name, proj, b_pad, dc, n_heads, gate_col):
    t = proj.shape[0]
    tb = min(256, t)
    nb = t // tb
    triu = jnp.asarray(np.triu(np.ones((tb, tb), np.float32)))

    def body(p_ref, b_ref, dc_ref, tri_ref, df_ref, db_ref, carry):
        @pl.when(pl.program_id(0) == 0)
        def _():
            carry[...] = jnp.zeros_like(carry)
            db_ref[...] = jnp.zeros_like(db_ref)

        dcv = dc_ref[...]
        dlf = _dot_exact(tri_ref[...], dcv) + carry[0:1, :]
        carry[...] = jnp.broadcast_to(dlf[0:1, :], carry.shape)
        lane = lax.broadcasted_iota(jnp.int32, (tb, LANES), 1)
        z = p_ref[...] + b_ref[...]
        df = jnp.where(lane < n_heads, dlf / (1.0 + jnp.exp(z)), 0.0)
        df_ref[...] = df.astype(BF16)
        db_ref[...] += jnp.sum(df, axis=0, keepdims=True)

    return pl.pallas_call(
        body, name=name, grid=(nb,),
        in_specs=[pl.BlockSpec((tb, LANES), lambda i: (nb - 1 - i, gate_col)), pl.BlockSpec((1, LANES), lambda i: (0, 0)),
                  pl.BlockSpec((tb, LANES), lambda i: (nb - 1 - i, 0)), pl.BlockSpec((tb, tb), lambda i: (0, 0))],
        out_specs=[pl.BlockSpec((tb, LANES), lambda i: (nb - 1 - i, 0)), pl.BlockSpec((1, LANES), lambda i: (0, 0))],
        out_shape=[_sds((t, LANES), BF16), _sds((1, LANES), F32)],
        scratch_shapes=[pltpu.VMEM((8, LANES), F32)], compiler_params=_params(1),
    )(proj, b_pad, dc, triu)


def _qhead(qp, g):
    return _rms(qp, g) * (HEAD_DIM ** -0.5)


def _column(mat, idx):
    lane = lax.broadcasted_iota(jnp.int32, mat.shape, 1)
    return jnp.sum(jnp.where(lane == idx, mat, 0.0), axis=1, keepdims=True)


def _fox_scores(kk, qi, ckey, cq_i, i, bq):
    length = kk.shape[0]
    s = _dot_nt(kk, qi) + cq_i - ckey[:length]
    key = lax.broadcasted_iota(jnp.int32, (length, bq), 0)
    qry = lax.broadcasted_iota(jnp.int32, (length, bq), 1) + i * bq
    return jnp.where(key <= qry, s, NEG)


def _fox_fwd(name, proj, c, crow, gq, gk, n_heads):
    t = proj.shape[0]
    hw = n_heads * HEAD_DIM
    npair = n_heads // 2
    bq = min(256, t)
    nq = t // bq

    def body(q_ref, k_ref, v_ref, c_ref, crow_ref, gq_ref, gk_ref, o_ref, lse_ref):
        hp = pl.program_id(0)
        lse_ref[...] = jnp.zeros_like(lse_ref)
        outs = []
        for hh in range(2):
            sl = slice(hh * HEAD_DIM, (hh + 1) * HEAD_DIM)
            qn = _qhead(q_ref[:, sl], gq_ref[...]).astype(BF16)
            kn = _rms(k_ref[:, sl], gk_ref[...]).astype(BF16)
            v_t = v_ref[:, sl].T.astype(BF16)
            ckey = _column(c_ref[...], 2 * hp + hh)
            cq = crow_ref[0, hh:hh + 1, :]
            o_blocks = []
            for i in range(nq):
                cols = slice(i * bq, (i + 1) * bq)
                length = (i + 1) * bq
                s = _fox_scores(kn[:length], qn[cols], ckey, cq[:, cols], i, bq)
                m = jnp.max(s, axis=0, keepdims=True)
                p = jnp.exp(s - m)
                l = jnp.sum(p, axis=0, keepdims=True)
                o_blocks.append((_dot(v_t[:, :length], p.astype(BF16)) / l).T)
                lse_ref[0, hh:hh + 1, cols] = m + jnp.log(l)
            outs.append(jnp.concatenate(o_blocks, axis=0))
        o_ref[...] = jnp.concatenate(outs, axis=1).astype(BF16)

    col = lambda off: (lambda h: (0, off + h))
    fixed = lambda h: (0, 0)
    return pl.pallas_call(
        body, name=name, grid=(npair,),
        in_specs=[pl.BlockSpec((t, LANES), col(0)), pl.BlockSpec((t, LANES), col(npair)), pl.BlockSpec((t, LANES), col(2 * npair)),
                  pl.BlockSpec((t, LANES), fixed), pl.BlockSpec((1, 2, t), lambda h: (h, 0, 0)),
                  pl.BlockSpec((1, HEAD_DIM), fixed), pl.BlockSpec((1, HEAD_DIM), fixed)],
        out_specs=[pl.BlockSpec((t, LANES), col(0)), pl.BlockSpec((1, 8, t), lambda h: (h, 0, 0))],
        out_shape=[_sds((t, hw), BF16), _sds((npair, 8, t), F32)], compiler_params=_params(1),
    )(proj, proj, proj, c, crow, gq, gk)


def _fox_bwd(name, proj, c, crow, gq, gk, lse, do, n_heads):
    t = proj.shape[0]
    hw = n_heads * HEAD_DIM
    npair = n_heads // 2
    bq = min(256, t)
    nq = t // bq

    def body(q_ref, k_ref, v_ref, c_ref, crow_ref, gq_ref, gk_ref, lse_ref, do_ref,
             dq_ref, dk_ref, dv_ref, dc_ref, dgq_ref, dgk_ref, dk_acc, dv_acc, dc_acc):
        hp = pl.program_id(0)

        @pl.when(hp == 0)
        def _():
            dgq_ref[...] = jnp.zeros_like(dgq_ref)
            dgk_ref[...] = jnp.zeros_like(dgk_ref)

        lane = lax.broadcasted_iota(jnp.int32, (t, LANES), 1)
        dc_pair = jnp.zeros((t, LANES), F32)
        dqs, dks, dvs = [], [], []
        for hh in range(2):
            sl = slice(hh * HEAD_DIM, (hh + 1) * HEAD_DIM)
            qf, q_vjp = jax.vjp(_qhead, q_ref[:, sl], gq_ref[...])
            kf, k_vjp = jax.vjp(_rms, k_ref[:, sl], gk_ref[...])
            qn, kn, kn_t = qf.astype(BF16), kf.astype(BF16), kf.T.astype(BF16)
            vb = v_ref[:, sl].astype(BF16)
            dob = do_ref[:, sl]
            ckey = _column(c_ref[...], 2 * hp + hh)
            cq = crow_ref[0, hh:hh + 1, :]
            lse_h = lse_ref[0, hh:hh + 1, :]
            dk_acc[...] = jnp.zeros_like(dk_acc)
            dv_acc[...] = jnp.zeros_like(dv_acc)
            dc_acc[...] = jnp.zeros_like(dc_acc)
            dq_blocks = []
            for i in range(nq):
                cols = slice(i * bq, (i + 1) * bq)
                length = (i + 1) * bq
                qi, doi = qn[cols], dob[cols]
                s = _fox_scores(kn[:length], qi, ckey, cq[:, cols], i, bq)
                p = jnp.exp(s - lse_h[:, cols])
                dp = _dot_nt(vb[:length], doi)
                ds = p * (dp - jnp.sum(p * dp, axis=0, keepdims=True))
                dsb = ds.astype(BF16)
                dq_blocks.append(_dot(kn_t[:, :length], dsb).T)
                dk_acc[0:length, :] += _dot(dsb, qi)
                dv_acc[0:length, :] += _dot(p.astype(BF16), doi)
                part = ds[:, 0:LANES]
                for j in range(1, bq // LANES):
                    part = part + ds[:, j * LANES:(j + 1) * LANES]
                dc_acc[0:length, :] += part
            dqp, dgq = q_vjp(jnp.concatenate(dq_blocks, axis=0))
            dkp, dgk = k_vjp(dk_acc[...])
            dgq_ref[...] += dgq
            dgk_ref[...] += dgk
            dqs.append(dqp)
            dks.append(dkp)
            dvs.append(dv_acc[...])
            dc_pair = jnp.where(lane == hh, -jnp.sum(dc_acc[...], axis=1, keepdims=True), dc_pair)
        dq_ref[...] = jnp.concatenate(dqs, axis=1).astype(BF16)
        dk_ref[...] = jnp.concatenate(dks, axis=1).astype(BF16)
        dv_ref[...] = jnp.concatenate(dvs, axis=1).astype(BF16)
        dc_ref[...] = dc_pair

    col = lambda off: (lambda h: (0, off + h))
    fixed = lambda h: (0, 0)
    pair_blk = pl.BlockSpec((t, LANES), col(0))
    return pl.pallas_call(
        body, name=name, grid=(npair,),
        in_specs=[pl.BlockSpec((t, LANES), col(0)), pl.BlockSpec((t, LANES), col(npair)), pl.BlockSpec((t, LANES), col(2 * npair)),
                  pl.BlockSpec((t, LANES), fixed), pl.BlockSpec((1, 2, t), lambda h: (h, 0, 0)),
                  pl.BlockSpec((1, HEAD_DIM), fixed), pl.BlockSpec((1, HEAD_DIM), fixed),
                  pl.BlockSpec((1, 8, t), lambda h: (h, 0, 0)), pair_blk],
        out_specs=[pair_blk, pair_blk, pair_blk, pair_blk,
                   pl.BlockSpec((1, HEAD_DIM), fixed), pl.BlockSpec((1, HEAD_DIM), fixed)],
        out_shape=[_sds((t, hw), BF16), _sds((t, hw), BF16), _sds((t, hw), BF16), _sds((t, npair * LANES), F32),
                   _sds((1, HEAD_DIM), F32), _sds((1, HEAD_DIM), F32)],
        scratch_shapes=[pltpu.VMEM((t, HEAD_DIM), F32), pltpu.VMEM((t, HEAD_DIM), F32), pltpu.VMEM((t, LANES), F32)],
        compiler_params=_params(1),
    )(proj, proj, proj, c, crow, gq, gk, lse, do)


def _t5_bucket_table():
    dist = np.arange(WINDOW)[None, :] + WINDOW - np.arange(2 * WINDOW)[:, None]
    n = np.maximum(dist, 0)
    max_exact = N_BUCKETS // 2
    large = max_exact + (np.log(np.maximum(n, 1) / max_exact) / np.log(REL_MAX_DIST / max_exact)
                         * (N_BUCKETS - max_exact)).astype(np.int32)
    large = np.minimum(large, N_BUCKETS - 1)
    return np.where(n < max_exact, n, large).astype(np.int32).reshape(1, -1)


def _bias_expand(name, rel_bias_t):
    n_heads = rel_bias_t.shape[0]
    tbl = jnp.asarray(_t5_bucket_table())
    width = tbl.shape[1]

    def body(rb_ref, tbl_ref, o_ref):
        onehot = (lax.broadcasted_iota(jnp.int32, (N_BUCKETS, width), 0) == tbl_ref[...]).astype(F32)
        o_ref[...] = _dot_exact(rb_ref[...], onehot)

    return pl.pallas_call(body, name=name, out_shape=_sds((n_heads, width), F32), compiler_params=_params(0))(rel_bias_t, tbl)


def _bias_reduce(name, dbias):
    n_heads, width = dbias.shape
    tbl = jnp.asarray(_t5_bucket_table())

    def body(db_ref, tbl_ref, o_ref):
        onehot = (lax.broadcasted_iota(jnp.int32, (N_BUCKETS, width), 0) == tbl_ref[...]).astype(F32)
        o_ref[...] = lax.dot_general(db_ref[...], onehot, (((1,), (1,)), ((), ())), preferred_element_type=F32,
                                     precision=lax.Precision.HIGHEST)

    return pl.pallas_call(body, name=name, out_shape=_sds((n_heads, N_BUCKETS), F32), compiler_params=_params(0))(dbias, tbl)


def _swa_mask(n, group):
    j = lax.broadcasted_iota(jnp.int32, (2 * WINDOW, group * WINDOW), 0)
    i = lax.broadcasted_iota(jnp.int32, (2 * WINDOW, group * WINDOW), 1) & (WINDOW - 1)
    ok = (j > i) & (j <= i + WINDOW) & ((n > 0) | (j >= WINDOW))
    return jnp.where(ok, 0.0, NEG)


def _swa_stack(ref, start, group):
    return jnp.concatenate([ref[pl.ds(start, WINDOW), g * HEAD_DIM:(g + 1) * HEAD_DIM] for g in range(group)], axis=0)


def _swa_fwd(name, qb, kh, vh, gq, gk, sinks, bias, group):
    t = qb.shape[0]
    kvh = kh.shape[0]
    nblk = t // WINDOW
    gw = group * HEAD_DIM
    band = 2 * WINDOW
    cols = group * WINDOW

    def body(q_ref, k_ref, v_ref, gq_ref, gk_ref, sink_ref, bias_ref, o_ref, lse_ref, qs, kpad, vpad):
        for g in range(group):
            qs[:, g * HEAD_DIM:(g + 1) * HEAD_DIM] = _qhead(q_ref[:, g * HEAD_DIM:(g + 1) * HEAD_DIM], gq_ref[...]).astype(BF16)
        kpad[0:WINDOW, :] = jnp.zeros((WINDOW, HEAD_DIM), BF16)
        vpad[0:WINDOW, :] = jnp.zeros((WINDOW, HEAD_DIM), BF16)
        kpad[WINDOW:, :] = _rms(k_ref[0], gk_ref[...]).astype(BF16)
        vpad[WINDOW:, :] = v_ref[0].astype(BF16)
        sink = sink_ref[0]

        def block(n, carry):
            start = pl.multiple_of(n * WINDOW, WINDOW)
            kb = kpad[pl.ds(start, band), :]
            vb = vpad[pl.ds(start, band), :]
            s = _dot_nt(kb, _swa_stack(qs, start, group)) + bias_ref[0] + _swa_mask(n, group)
            m = jnp.maximum(jnp.max(s, axis=0, keepdims=True), sink)
            e = jnp.exp(s - m)
            l = jnp.sum(e, axis=0, keepdims=True) + jnp.exp(sink - m)
            o_t = _dot_tn(vb, e.astype(BF16)) / l
            for g in range(group):
                o_ref[pl.ds(start, WINDOW), g * HEAD_DIM:(g + 1) * HEAD_DIM] = o_t[:, g * WINDOW:(g + 1) * WINDOW].T.astype(BF16)
            lse_ref[pl.ds(n, 1), :] = m + jnp.log(l)
            return carry

        lax.fori_loop(0, nblk, block, 0)

    fixed = lambda h: (0, 0)
    per = lambda h: (h, 0, 0)
    return pl.pallas_call(
        body, name=name, grid=(kvh,),
        in_specs=[pl.BlockSpec((t, gw), lambda h: (0, h)), pl.BlockSpec((1, t, HEAD_DIM), per), pl.BlockSpec((1, t, HEAD_DIM), per),
                  pl.BlockSpec((1, HEAD_DIM), fixed), pl.BlockSpec((1, HEAD_DIM), fixed),
                  pl.BlockSpec((1, 1, cols), per), pl.BlockSpec((1, band, cols), per)],
        out_specs=[pl.BlockSpec((t, gw), lambda h: (0, h)), pl.BlockSpec((nblk, cols), lambda h: (h, 0))],
        out_shape=[_sds((t, kvh * gw), BF16), _sds((kvh * nblk, cols), F32)],
        scratch_shapes=[pltpu.VMEM((t, gw), BF16), pltpu.VMEM((t + WINDOW, HEAD_DIM), BF16),
                        pltpu.VMEM((t + WINDOW, HEAD_DIM), BF16)],
        compiler_params=_params(1),
    )(qb, kh, vh, gq, gk, sinks, bias)


def _swa_bwd(name, qb, kh, vh, gq, gk, sinks, bias, lse, do, group):
    t = qb.shape[0]
    kvh = kh.shape[0]
    nblk = t // WINDOW
    gw = group * HEAD_DIM
    band = 2 * WINDOW
    cols = group * WINDOW

    def body(q_ref, k_ref, v_ref, gq_ref, gk_ref, sink_ref, bias_ref, lse_ref, do_ref,
             dq_ref, dk_ref, dv_ref, dgq_ref, dgk_ref, dsink_ref, dbias_ref,
             qs, kpad, vpad, dqs, dk_acc, dv_acc, dsink_acc):
        @pl.when(pl.program_id(0) == 0)
        def _():
            dgq_ref[...] = jnp.zeros_like(dgq_ref)
            dgk_ref[...] = jnp.zeros_like(dgk_ref)

        for g in range(group):
            qs[:, g * HEAD_DIM:(g + 1) * HEAD_DIM] = _qhead(q_ref[:, g * HEAD_DIM:(g + 1) * HEAD_DIM], gq_ref[...]).astype(BF16)
        kpad[0:WINDOW, :] = jnp.zeros((WINDOW, HEAD_DIM), BF16)
        vpad[0:WINDOW, :] = jnp.zeros((WINDOW, HEAD_DIM), BF16)
        kpad[WINDOW:, :] = _rms(k_ref[0], gk_ref[...]).astype(BF16)
        vpad[WINDOW:, :] = v_ref[0].astype(BF16)
        dk_acc[...] = jnp.zeros_like(dk_acc)
        dv_acc[...] = jnp.zeros_like(dv_acc)
        dsink_acc[...] = jnp.zeros_like(dsink_acc)
        dbias_ref[...] = jnp.zeros_like(dbias_ref)
        sink = sink_ref[0]

        def block(n, carry):
            start = pl.multiple_of(n * WINDOW, WINDOW)
            kb = kpad[pl.ds(start, band), :]
            vb = vpad[pl.ds(start, band), :]
            q = _swa_stack(qs, start, group)
            dob = _swa_stack(do_ref, start, group)
            lse_n = lse_ref[pl.ds(n, 1), :]
            s = _dot_nt(kb, q) + bias_ref[0] + _swa_mask(n, group)
            p = jnp.exp(s - lse_n)
            dp = _dot_nt(vb, dob)
            dsum = jnp.sum(p * dp, axis=0, keepdims=True)
            ds = p * (dp - dsum)
            dsb = ds.astype(BF16)
            dsink_acc[...] -= jnp.exp(sink - lse_n) * dsum
            dbias_ref[0] += ds
            dq = _dot_tn(dsb, kb)
            for g in range(group):
                dqs[pl.ds(start, WINDOW), g * HEAD_DIM:(g + 1) * HEAD_DIM] = dq[g * WINDOW:(g + 1) * WINDOW]
            dk_acc[pl.ds(start, band), :] += _dot(dsb, q)
            dv_acc[pl.ds(start, band), :] += _dot(p.astype(BF16), dob)
            return carry

        lax.fori_loop(0, nblk, block, 0)
        for g in range(group):
            _, q_vjp = jax.vjp(_qhead, q_ref[:, g * HEAD_DIM:(g + 1) * HEAD_DIM], gq_ref[...])
            dqp, dgq = q_vjp(dqs[:, g * HEAD_DIM:(g + 1) * HEAD_DIM])
            dq_ref[:, g * HEAD_DIM:(g + 1) * HEAD_DIM] = dqp.astype(BF16)
            dgq_ref[...] += dgq
            dsink_g = jnp.sum(dsink_acc[:, g * WINDOW:(g + 1) * WINDOW], axis=1, keepdims=True)
            dsink_ref[0, g:g + 1, :] = jnp.broadcast_to(dsink_g, (1, LANES))
        _, k_vjp = jax.vjp(_rms, k_ref[0], gk_ref[...])
        dkp, dgk = k_vjp(dk_acc[WINDOW:, :])
        dk_ref[0] = dkp
        dgk_ref[...] += dgk
        dv_ref[0] = dv_acc[WINDOW:, :]

    fixed = lambda h: (0, 0)
    per = lambda h: (h, 0, 0)
    wide = pl.BlockSpec((t, gw), lambda h: (0, h))
    head = pl.BlockSpec((1, t, HEAD_DIM), per)
    vec = pl.BlockSpec((1, HEAD_DIM), fixed)
    bias_spec = pl.BlockSpec((1, band, cols), per)
    return pl.pallas_call(
        body, name=name, grid=(kvh,),
        in_specs=[wide, head, head, vec, vec, pl.BlockSpec((1, 1, cols), per), bias_spec,
                  pl.BlockSpec((nblk, cols), lambda h: (h, 0)), wide],
        out_specs=[wide, head, head, vec, vec, pl.BlockSpec((1, group, LANES), per), bias_spec],
        out_shape=[_sds((t, kvh * gw), BF16), _sds((kvh, t, HEAD_DIM), F32), _sds((kvh, t, HEAD_DIM), F32),
                   _sds((1, HEAD_DIM), F32), _sds((1, HEAD_DIM), F32),
                   _sds((kvh, group, LANES), F32), _sds((kvh, band, cols), F32)],
        scratch_shapes=[pltpu.VMEM((t, gw), BF16), pltpu.VMEM((t + WINDOW, HEAD_DIM), BF16),
                        pltpu.VMEM((t + WINDOW, HEAD_DIM), BF16), pltpu.VMEM((t, gw), F32),
                        pltpu.VMEM((t + WINDOW, HEAD_DIM), F32), pltpu.VMEM((t + WINDOW, HEAD_DIM), F32),
                        pltpu.VMEM((1, cols), F32)],
        compiler_params=_params(1),
    )(qb, kh, vh, gq, gk, sinks, bias, lse, do)


def _local_step(x, target, p, comm):
    t, d = x.shape
    n_heads = d // HEAD_DIM
    kv_heads = n_heads // 8
    group = n_heads // kv_heads
    hw = n_heads * HEAD_DIM
    gate_col = 3 * hw // LANES
    kvw = kv_heads * HEAD_DIM
    grads = {}

    def mlp_fwd(tag, h, g, layer):
        w_up, = comm.weights([f"w_up{layer}"], h)
        a, hn = _norm_matmul(f"{tag}_up", h, g, w_up, relu2=True)
        w_down, = comm.weights([f"w_down{layer}"], a)
        return _matmul_res(f"{tag}_down", a, w_down, h), (h, g, hn, a, w_up, w_down)

    def mlp_bwd(tag, saved, layer, dy):
        h, g, hn, a, w_up, w_down = saved
        du = _matmul_nt(f"{tag}_du", dy, w_down, a=a)
        dw_down = _matmul_tn(f"{tag}_dwdown", a, dy)
        dw_up = _matmul_tn(f"{tag}_dwup", hn, du, col_blocks=w_up.shape[0])
        zero = comm.send_grads(tag, {f"w_down{layer}": dw_down, f"w_up{layer}": dw_up})
        return _matmul_nt_rmsbwd(f"{tag}_dh", du, w_up, h, g + zero, dy)

    comm.prefetch(["w_in_a"], None)
    w_in, = comm.weights(["w_in_a"], None)
    proj, xn1 = _norm_matmul("a_inproj", x, p["g_attn"][0], w_in, tn=640)
    ahead = comm.prefetch(["w_out_a"], proj)
    b_pad = jnp.pad(p["b_f"], ((0, 0), (0, LANES - n_heads))) + ahead[0:1, :]
    c = _gate_fwd("a_gate", proj, b_pad, n_heads, gate_col)
    ahead = comm.prefetch(["w_up0"], c)
    crow = c[:, :n_heads].T.reshape(n_heads // 2, 2, t)
    o_a, lse_a = _fox_fwd("a_attn", proj, c, crow, p["gq_a"] + ahead[0:1, 0:HEAD_DIM], p["gk_a"], n_heads)
    ahead = comm.prefetch(["w_down0", "w_kv", "w_q_b", "w_out_b"], o_a)
    w_out_a, = comm.weights(["w_out_a"], o_a)
    h1 = _matmul_res("a_outproj", o_a, w_out_a, x, after=ahead)
    h2, mlp0 = mlp_fwd("mlp0", h1, p["g_mlp"][0], 0)

    ahead = comm.prefetch(["w_up1", "w_down1"], h2)
    w_kv, w_q_b = comm.weights(["w_kv", "w_q_b"], h2)
    kv, hn_kv = _norm_matmul("kv_proj", h2, p["g_kv"] + ahead[0, 0], w_kv, tn=2 * kvw)
    kh = kv[:, :kvw].reshape(t, kv_heads, HEAD_DIM).transpose(1, 0, 2)
    vh = kv[:, kvw:].reshape(t, kv_heads, HEAD_DIM).transpose(1, 0, 2)
    qb, hn_q = _norm_matmul("b_qproj", h2, p["g_attn"][1], w_q_b, tn=512)
    gqb, gkb = p["gq_b"], p["gk_b"].reshape(1, HEAD_DIM)
    bias = _bias_expand("b_bias", p["rel_bias"].T).reshape(kv_heads, group, 2 * WINDOW, WINDOW)
    bias = bias.transpose(0, 2, 1, 3).reshape(kv_heads, 2 * WINDOW, group * WINDOW)
    sink_rows = jnp.broadcast_to(p["sinks"].reshape(kv_heads, 1, group, 1), (kv_heads, 1, group, WINDOW)).reshape(kv_heads, 1, group * WINDOW)
    o_b, lse_b = _swa_fwd("b_attn", qb, kh, vh, gqb, gkb, sink_rows, bias, group)
    w_out_b, = comm.weights(["w_out_b"], o_b)
    h3 = _matmul_res("b_outproj", o_b, w_out_b, h2)
    y, mlp1 = mlp_fwd("mlp1", h3, p["g_mlp"][1], 1)
    dy, loss_tile = _loss_grad("loss", y, target)

    dh3, dg_mlp1 = mlp_bwd("mlp1", mlp1, 1, dy)
    do_b = _matmul_nt("b_do", dh3, w_out_b)
    dw_out_b = _matmul_tn("b_dwout", o_b, dh3)
    dqb, dkh, dvh, grads["gq_b"], dgk_b, dsink, dbias = _swa_bwd(
        "b_attn_bwd", qb, kh, vh, gqb, gkb, sink_rows, bias, lse_b, do_b, group)
    grads["gk_b"] = dgk_b
    grads["sinks"] = dsink[:, :, 0].reshape(1, n_heads)
    dbias = dbias.reshape(kv_heads, 2 * WINDOW, group, WINDOW).transpose(0, 2, 1, 3)
    grads["rel_bias"] = _bias_reduce("b_dbias", dbias.reshape(n_heads, WINDOW * 2 * WINDOW)).T
    dw_q_b = _matmul_tn("b_dwq", hn_q, dqb)
    dh2, dg_attn1 = _matmul_nt_rmsbwd("b_dhq", dqb, w_q_b, h2, p["g_attn"][1], dh3)
    dkv = jnp.concatenate([dkh.transpose(1, 0, 2).reshape(t, kvw), dvh.transpose(1, 0, 2).reshape(t, kvw)], axis=1)
    dw_kv = _matmul_tn("kv_dw", hn_kv, dkv)
    zero = comm.send_grads("attn_b", {"w_out_b": dw_out_b, "w_q_b": dw_q_b, "w_kv": dw_kv})
    dh2, dg_kv = _matmul_nt_rmsbwd("kv_dh", dkv, w_kv, h2, p["g_kv"] + zero, dh2)
    grads["g_kv"] = dg_kv
    dh1, dg_mlp0 = mlp_bwd("mlp0", mlp0, 0, dh2)
    grads["g_mlp"] = (dg_mlp0, dg_mlp1)

    do_a = _matmul_nt("a_do", dh1, w_out_a)
    dw_out_a = _matmul_tn("a_dwout", o_a, dh1)
    zero = comm.send_grads("attn_a_out", {"w_out_a": dw_out_a})
    dq, dk, dv, dc_cols, grads["gq_a"], grads["gk_a"] = _fox_bwd(
        "a_attn_bwd", proj, c, crow, p["gq_a"] + zero, p["gk_a"], lse_a, do_a, n_heads)
    dc = jnp.pad(dc_cols.reshape(t, n_heads // 2, LANES)[:, :, :2].reshape(t, n_heads), ((0, 0), (0, LANES - n_heads)))
    dfl, db_f = _gate_bwd("a_gate_bwd", proj, b_pad, dc, n_heads, gate_col)
    grads["b_f"] = db_f
    dproj = jnp.concatenate([dq, dk, dv, dfl], axis=1)
    dw_in = _matmul_tn("a_dwin", xn1, dproj, tn=640)
    zero = comm.send_grads("attn_a_in", {"w_in_a": dw_in})
    grad_x, dg_attn0 = _matmul_nt_rmsbwd("a_dx", dproj, w_in, x, p["g_attn"][0] + zero, dh1)
    grads["g_attn"] = (dg_attn0, dg_attn1)
    return loss_tile, grad_x, grads


EVERYONE = (1, 2, 3, 4, 5, 6, 7)
SAME_CORE = (1, 2, 4, 6)
OTHER_CHIPS = (2, 4, 6)


class _InFlight:
    def __init__(self, scatter, ks, send_sems, recv_sems, srcs, lands, token):
        self.scatter, self.ks, self.send_sems, self.recv_sems = scatter, ks, send_sems, recv_sems
        self.srcs, self.lands, self.token = list(srcs), list(lands), token


def _mesh_peers(ks=EVERYONE):
    x, y, c = lax.axis_index("x"), lax.axis_index("y"), lax.axis_index("c")
    peers = []
    for k in ks:
        px, py, pc = x ^ ((k >> 2) & 1), y ^ ((k >> 1) & 1), c ^ (k & 1)
        peers.append(((px, py, pc), 4 * px + 2 * py + pc))
    return 4 * x + 2 * y + c, peers


_HBM_SPEC = pl.BlockSpec(memory_space=pltpu.HBM)
_SEM_SPEC = pl.BlockSpec(memory_space=pltpu.SEMAPHORE)
_SIDE_EFFECT = pltpu.SideEffectType.DATAFLOW_SIDE_EFFECTING


def _exchange_start(name, arrays, scatter, ks=EVERYONE):
    n = len(arrays)
    me, _ = _mesh_peers()
    lands = []
    for a in arrays:
        own = lax.dynamic_index_in_dim(a, me, 0, keepdims=False) if scatter else a
        shape = a.shape if scatter else (N_DEV,) + a.shape
        lands.append(lax.dynamic_update_index_in_dim(lax.empty(shape, a.dtype), own, me, 0))

    def body(*refs):
        src, land = refs[:n], refs[n:2 * n]
        send_sems, recv_sems, token = refs[2 * n], refs[2 * n + 1], refs[-1]
        pos, peers = _mesh_peers(ks)
        for a in range(n):
            for k, (peer, peer_pos) in enumerate(peers):
                pltpu.make_async_remote_copy(
                    src_ref=src[a].at[peer_pos] if scatter else src[a], dst_ref=land[a].at[pos],
                    send_sem=send_sems.at[a * len(ks) + k], recv_sem=recv_sems.at[a * len(ks) + k],
                    device_id=peer, device_id_type=pl.DeviceIdType.MESH).start()
        token[...] = jnp.zeros_like(token)

    operands = [pltpu.with_memory_space_constraint(a, pltpu.HBM) for a in list(arrays) + lands]
    outs = pl.pallas_call(
        body, name=name,
        out_shape=(pltpu.SemaphoreType.DMA((n * len(ks),)), pltpu.SemaphoreType.DMA((n * len(ks),)),
                   *[pltpu.HBM(a.shape, a.dtype) for a in operands], _sds((8, LANES), F32)),
        in_specs=[_HBM_SPEC] * (2 * n),
        out_specs=(_SEM_SPEC, _SEM_SPEC, *[_HBM_SPEC] * (2 * n), pl.BlockSpec(memory_space=pltpu.VMEM)),
        input_output_aliases={i: 2 + i for i in range(2 * n)},
        compiler_params=pltpu.CompilerParams(has_side_effects=_SIDE_EFFECT),
    )(*operands)
    return _InFlight(scatter, ks, outs[0], outs[1], outs[2:2 + n], outs[2 + n:2 + 2 * n], outs[-1])


def _exchange_wait(name, flight, which, after):
    m = len(which)
    scatter, ks = flight.scatter, flight.ks

    def body(*refs):
        src, land = refs[:m], refs[m:2 * m]
        send_sems, recv_sems = refs[2 * m], refs[2 * m + 1]
        _, peers = _mesh_peers(ks)
        for i, a in enumerate(which):
            for k, (peer, peer_pos) in enumerate(peers):
                cp = pltpu.make_async_remote_copy(
                    src_ref=src[i].at[peer_pos] if scatter else src[i], dst_ref=land[i].at[peer_pos],
                    send_sem=send_sems.at[a * len(ks) + k], recv_sem=recv_sems.at[a * len(ks) + k],
                    device_id=peer, device_id_type=pl.DeviceIdType.MESH)
                cp.wait_send()
                cp.wait_recv()

    operands = [flight.srcs[a] for a in which] + [flight.lands[a] for a in which]
    outs = pl.pallas_call(
        body, name=name, out_shape=tuple(pltpu.HBM(a.shape, a.dtype) for a in operands),
        in_specs=[_HBM_SPEC] * (2 * m) + [_SEM_SPEC, _SEM_SPEC, pl.BlockSpec(memory_space=pl.ANY)],
        out_specs=tuple([_HBM_SPEC] * (2 * m)), input_output_aliases={i: i for i in range(2 * m)},
        compiler_params=pltpu.CompilerParams(has_side_effects=_SIDE_EFFECT),
    )(*operands, flight.send_sems, flight.recv_sems, after)
    return list(outs[m:])


def _relay_start(name, lands):
    n = len(lands)

    def body(*refs):
        land, send_sems, recv_sems, token = refs[:n], refs[n], refs[n + 1], refs[-1]
        _, peers = _mesh_peers(OTHER_CHIPS)
        sibling = (lax.axis_index("x"), lax.axis_index("y"), 1 - lax.axis_index("c"))
        for a in range(n):
            for k, (_, peer_pos) in enumerate(peers):
                pltpu.make_async_remote_copy(
                    src_ref=land[a].at[peer_pos], dst_ref=land[a].at[peer_pos],
                    send_sem=send_sems.at[a * len(peers) + k], recv_sem=recv_sems.at[a * len(peers) + k],
                    device_id=sibling, device_id_type=pl.DeviceIdType.MESH).start()
        token[...] = jnp.zeros_like(token)

    count = n * len(OTHER_CHIPS)
    outs = pl.pallas_call(
        body, name=name,
        out_shape=(pltpu.SemaphoreType.DMA((count,)), pltpu.SemaphoreType.DMA((count,)),
                   *[pltpu.HBM(a.shape, a.dtype) for a in lands], _sds((8, LANES), F32)),
        in_specs=[_HBM_SPEC] * n,
        out_specs=(_SEM_SPEC, _SEM_SPEC, *[_HBM_SPEC] * n, pl.BlockSpec(memory_space=pltpu.VMEM)),
        input_output_aliases={i: 2 + i for i in range(n)},
        compiler_params=pltpu.CompilerParams(has_side_effects=_SIDE_EFFECT),
    )(*[pltpu.with_memory_space_constraint(a, pltpu.HBM) for a in lands])
    return _InFlight(False, OTHER_CHIPS, outs[0], outs[1], [], outs[2:2 + n], outs[-1])


def _relay_wait(name, flight, after):
    n = len(flight.lands)

    def body(*refs):
        land, send_sems, recv_sems = refs[:n], refs[n], refs[n + 1]
        _, peers = _mesh_peers(OTHER_CHIPS)
        sibling = (lax.axis_index("x"), lax.axis_index("y"), 1 - lax.axis_index("c"))
        for a in range(n):
            for k, (_, peer_pos) in enumerate(peers):
                cp = pltpu.make_async_remote_copy(
                    src_ref=land[a].at[peer_pos], dst_ref=land[a].at[peer_pos ^ 1],
                    send_sem=send_sems.at[a * len(peers) + k], recv_sem=recv_sems.at[a * len(peers) + k],
                    device_id=sibling, device_id_type=pl.DeviceIdType.MESH)
                cp.wait_send()
                cp.wait_recv()

    outs = pl.pallas_call(
        body, name=name, out_shape=tuple(pltpu.HBM(a.shape, a.dtype) for a in flight.lands),
        in_specs=[_HBM_SPEC] * n + [_SEM_SPEC, _SEM_SPEC, pl.BlockSpec(memory_space=pl.ANY)],
        out_specs=tuple([_HBM_SPEC] * n), input_output_aliases={i: i for i in range(n)},
        compiler_params=pltpu.CompilerParams(has_side_effects=_SIDE_EFFECT),
    )(*flight.lands, flight.send_sems, flight.recv_sems, after)
    return list(outs)


def _sum_parts(p_ref):
    g = p_ref[0].astype(F32)
    for dev in range(1, N_DEV):
        g = g + p_ref[dev].astype(F32)
    return g


def _adam_update(g, w, m, v):
    m_new = ADAM_B1 * m + (1.0 - ADAM_B1) * g
    v_new = ADAM_B2 * v + (1.0 - ADAM_B2) * jnp.square(g)
    m_hat = m_new / (1.0 - ADAM_B1 ** ADAM_STEP)
    v_hat = v_new / (1.0 - ADAM_B2 ** ADAM_STEP)
    return -ADAM_LR * (m_hat / (jnp.sqrt(v_hat) + ADAM_EPS) + ADAM_WD * w), m_new, v_new


def _adamw(name, parts, w, m, v, layer=None, into=None):
    r, c = w.shape[-2:]
    tr = r if r <= 256 else 256
    n_into = 0 if into is None else len(into)

    def body(p_ref, w_ref, m_ref, v_ref, *refs):
        g_ref, d_ref, mo_ref, vo_ref = refs[n_into:]
        g = _sum_parts(p_ref)
        g_ref[...] = g
        d_ref[...], mo_ref[...], vo_ref[...] = _adam_update(g, w_ref[...], m_ref[...], v_ref[...])

    if layer is None:
        blk = pl.BlockSpec((tr, c), lambda i: (i, 0))
    else:
        blk = pl.BlockSpec((None, tr, c), lambda i: (layer, i, 0))
    return pl.pallas_call(
        body, name=name, grid=(r // tr,),
        in_specs=[pl.BlockSpec((N_DEV, tr, c), lambda i: (0, i, 0)), blk, blk, blk] + [pl.BlockSpec(memory_space=pl.ANY)] * n_into,
        out_specs=[blk] * 4, out_shape=[_sds(w.shape, F32)] * 4,
        input_output_aliases={4 + i: i for i in range(n_into)}, compiler_params=_params(1),
    )(parts, w, m, v, *(into or ()))


SMALL_PACK_ROWS = 16
LOSS_ROW = 11


def _small_rows(grads, loss_tile):
    return [(0, 1, grads["g_attn"][0]), (1, 1, grads["g_attn"][1]), (2, 1, grads["g_mlp"][0]), (3, 1, grads["g_mlp"][1]),
            (4, 1, grads["g_kv"]), (5, 1, grads["b_f"]), (6, 1, grads["gq_a"]), (7, 1, grads["gk_a"]), (8, 1, grads["gk_b"]),
            (9, 1, grads["gq_b"]), (10, 1, grads["sinks"]), (LOSS_ROW, 1, loss_tile)]


SMALL_ROWS = {"g_attn": (0, 2), "g_mlp": (2, 2), "g_kv": (4, 1), "b_f": (5, 1), "gq_a": (6, 1), "gk_a": (7, 1),
              "gk_b": (8, 1), "gq_b": (9, 1), "sinks": (10, 1)}


def _pack_small(name, pieces, d):
    def body(*refs):
        out = refs[-1]
        out[...] = jnp.zeros_like(out)
        for (row, rows, _), ref in zip(pieces, refs[:-1]):
            out[row:row + rows, 0:ref.shape[1]] = ref[0:rows, :]

    return pl.pallas_call(body, name=name, out_shape=_sds((SMALL_PACK_ROWS, d), F32), compiler_params=_params(0))(
        *[piece for _, _, piece in pieces])


def _adamw_small(name, parts, parts_rel_bias, w, m, v):
    def body(*refs):
        ins, outs = refs[2:2 + 3 * len(SMALL)], refs[2 + 3 * len(SMALL):]
        pack, rel = _sum_parts(refs[0]), _sum_parts(refs[1])
        for i, k in enumerate(SMALL):
            w_ref, m_ref, v_ref = ins[3 * i:3 * i + 3]
            if k == "rel_bias":
                g = rel
            else:
                row, rows = SMALL_ROWS[k]
                g = pack[row:row + rows, 0:w_ref.shape[1]]
            outs[4 * i][...] = g
            outs[4 * i + 1][...], outs[4 * i + 2][...], outs[4 * i + 3][...] = _adam_update(g, w_ref[...], m_ref[...], v_ref[...])
        outs[-1][...] = pack[LOSS_ROW:LOSS_ROW + 1, 0:LANES]

    operands = [parts, parts_rel_bias] + [t[k] for k in SMALL for t in (w, m, v)]
    out_shape = [_sds(w[k].shape, F32) for k in SMALL for _ in range(4)] + [_sds((1, LANES), F32)]
    outs = pl.pallas_call(body, name=name, out_shape=out_shape, compiler_params=_params(0))(*operands)
    return {k: outs[4 * i:4 * i + 4] for i, k in enumerate(SMALL)}, outs[-1]


class _Comm:
    ORDER = ("w_in_a", "w_out_a", "w_up0", "w_down0", "w_kv", "w_q_b", "w_out_b", "w_up1", "w_down1")

    def __init__(self, shards, d, n_in):
        self.d, self.n_in = d, n_in
        self.flight = _exchange_start("gather_start", [shards[n].astype(BF16) for n in self.ORDER], scatter=False, ks=SAME_CORE)
        self.relays, self.ready, self.sent = {}, {}, []

    def prefetch(self, names, after):
        which = [self.ORDER.index(n) for n in names]
        landed = _exchange_wait(f"gather_wait_{names[0]}", self.flight, which, self.flight.token if after is None else after)
        relay = _relay_start(f"gather_relay_{names[0]}", landed)
        for n in names:
            self.relays[n] = (relay, names)
        return relay.token

    def weights(self, names, after):
        for n in names:
            if n not in self.ready:
                relay, group = self.relays[n]
                landed = _relay_wait(f"gather_relay_wait_{group[0]}", relay, relay.token if after is None else after)
                self.ready.update({m: self._whole(m, g) for m, g in zip(group, landed)})
        return [self.ready[n] for n in names]

    def _whole(self, name, g):
        if name == "w_in_a":
            pad = -(-self.n_in // LANES) * LANES - self.n_in
            return jnp.pad(g.transpose(1, 0, 2).reshape(self.d, self.n_in), ((0, 0), (0, pad)))
        if name.startswith("w_up"):
            return g
        return g.reshape(-1, g.shape[-1])

    def _chunks(self, name, g):
        if name == "w_in_a":
            return g[:, :self.n_in].reshape(self.d, N_DEV, -1).transpose(1, 0, 2)
        if name.startswith("w_up"):
            return g
        return g.reshape(N_DEV, g.shape[0] // N_DEV, g.shape[1])

    def send_grads(self, tag, partials):
        names = list(partials)
        flight = _exchange_start(f"scatter_start_{tag}", [self._chunks(n, partials[n]) for n in names], scatter=True)
        self.sent.append((tag, flight, names))
        return flight.token[0, 0]

    def received(self, index, after):
        tag, flight, names = self.sent[index]
        landed = _exchange_wait(f"scatter_wait_{tag}", flight, list(range(len(names))), after)
        return dict(zip(names, landed))


def kernel(x, g_attn, g_mlp, w_in_a, b_f, gq_a, gk_a, w_out_a, g_kv, w_kv, gk_b, w_q_b, gq_b, sinks, rel_bias, w_out_b, w_up, w_down, loss_target, m_g_attn, m_g_mlp, m_w_in_a, m_b_f, m_gq_a, m_gk_a, m_w_out_a, m_g_kv, m_w_kv, m_gk_b, m_w_q_b, m_gq_b, m_sinks, m_rel_bias, m_w_out_b, m_w_up, m_w_down, v_g_attn, v_g_mlp, v_w_in_a, v_b_f, v_gq_a, v_gk_a, v_w_out_a, v_g_kv, v_w_kv, v_gk_b, v_w_q_b, v_gq_b, v_sinks, v_rel_bias, v_w_out_b, v_w_up, v_w_down):
    w = dict(g_attn=g_attn, g_mlp=g_mlp, w_in_a=w_in_a, b_f=b_f, gq_a=gq_a, gk_a=gk_a, w_out_a=w_out_a, g_kv=g_kv,
             w_kv=w_kv, gk_b=gk_b, w_q_b=w_q_b, gq_b=gq_b, sinks=sinks, rel_bias=rel_bias, w_out_b=w_out_b,
             w_up=w_up, w_down=w_down)
    mom = dict(g_attn=m_g_attn, g_mlp=m_g_mlp, w_in_a=m_w_in_a, b_f=m_b_f, gq_a=m_gq_a, gk_a=m_gk_a, w_out_a=m_w_out_a,
               g_kv=m_g_kv, w_kv=m_w_kv, gk_b=m_gk_b, w_q_b=m_w_q_b, gq_b=m_gq_b, sinks=m_sinks, rel_bias=m_rel_bias,
               w_out_b=m_w_out_b, w_up=m_w_up, w_down=m_w_down)
    var = dict(g_attn=v_g_attn, g_mlp=v_g_mlp, w_in_a=v_w_in_a, b_f=v_b_f, gq_a=v_gq_a, gk_a=v_gk_a, w_out_a=v_w_out_a,
               g_kv=v_g_kv, w_kv=v_w_kv, gk_b=v_gk_b, w_q_b=v_w_q_b, gq_b=v_gq_b, sinks=v_sinks, rel_bias=v_rel_bias,
               w_out_b=v_w_out_b, w_up=v_w_up, w_down=v_w_down)
    d = x.shape[2]
    where = {"w_in_a": ("w_in_a", 0), "w_out_a": ("w_out_a", 0), "w_kv": ("w_kv", None), "w_q_b": ("w_q_b", 0),
             "w_out_b": ("w_out_b", 0), "w_up0": ("w_up", 0), "w_up1": ("w_up", 1), "w_down0": ("w_down", 0),
             "w_down1": ("w_down", 1)}
    shards = {n: (w[k] if layer is None else w[k][layer]) for n, (k, layer) in where.items()}
    comm = _Comm(shards, d, w_in_a.shape[2] * N_DEV)
    loss_tile, grad_x, grads = _local_step(x[0], loss_target[0], {k: w[k] for k in SMALL}, comm)

    small_flight = _exchange_start(
        "gather_small_grads", [_pack_small("pack_small", _small_rows(grads, loss_tile), d), grads["rel_bias"]], scatter=False)
    res, after = {}, small_flight.token
    for index in range(len(comm.sent)):
        for n, parts in comm.received(index, after).items():
            k, layer = where[n]
            res[k] = _adamw(f"adam_{n}", parts, w[k], mom[k], var[k], layer, res.get(k))
            after = res[k][0]
    as_rows = lambda tree: {k: tree[k] if tree[k].ndim == 2 else tree[k].reshape(1, -1) for k in SMALL}
    small, loss_row = _adamw_small("adam_small", *_exchange_wait("gather_small_wait", small_flight, [0, 1], after),
                                   as_rows(w), as_rows(mom), as_rows(var))
    loss = loss_row[0, 0]
    for k in SMALL:
        res[k] = [a.reshape(w[k].shape) for a in small[k]]

    outs = [loss, grad_x[None]]
    for i in range(4):
        outs.extend(res[k][i] for k in WEIGHTS)
    return tuple(outs)
```

```python
import numpy as np
import jax
import jax.numpy as jnp
from jax import lax
from jax.experimental import pallas as pl
from jax.experimental.pallas import tpu as pltpu

F32 = jnp.float32
BF16 = jnp.bfloat16

N_DEV = 8
HEAD_DIM = 64
WINDOW = 128
N_BUCKETS = 32
REL_MAX_DIST = 128
NORM_EPS = 1e-6
NEG = -1e30
LANES = 128
VMEM_LIMIT = 56 * 1024 * 1024

ADAM_LR = 0.001
ADAM_B1 = 0.9
ADAM_B2 = 0.999
ADAM_EPS = 1e-08
ADAM_WD = 0.01
ADAM_STEP = 10

SMALL = ("g_attn", "g_mlp", "b_f", "gq_a", "gk_a", "g_kv", "gk_b", "gq_b", "sinks", "rel_bias")
WEIGHTS = ("g_attn", "g_mlp", "w_in_a", "b_f", "gq_a", "gk_a", "w_out_a", "g_kv", "w_kv", "gk_b",
           "w_q_b", "gq_b", "sinks", "rel_bias", "w_out_b", "w_up", "w_down")


def _params(n_grid):
    return pltpu.CompilerParams(dimension_semantics=("arbitrary",) * n_grid, vmem_limit_bytes=VMEM_LIMIT)


def _sds(shape, dtype):
    return jax.ShapeDtypeStruct(tuple(shape), dtype)


def _rms(x, g):
    return (x * lax.rsqrt(jnp.mean(x * x, axis=-1, keepdims=True) + NORM_EPS)) * g


def _dot_nt(a, b):
    return lax.dot_general(a, b, (((1,), (1,)), ((), ())), preferred_element_type=F32)


def _dot_tn(a, b):
    return lax.dot_general(a, b, (((0,), (0,)), ((), ())), preferred_element_type=F32)


def _dot(a, b):
    return jnp.dot(a, b, preferred_element_type=F32)


def _dot_exact(a, b):
    return jnp.dot(a, b, preferred_element_type=F32, precision=lax.Precision.HIGHEST)


def _norm_matmul(name, x, g, w, *, tn=None, relu2=False):
    t, d = x.shape
    blocked = w.ndim == 3
    if blocked:
        tn = w.shape[2]
        n = w.shape[0] * tn
        w_spec = pl.BlockSpec((None, d, tn), lambda i, j: (j, 0, 0))
    else:
        n = w.shape[1]
        w_spec = pl.BlockSpec((d, tn), lambda i, j: (0, j))
    tm = min(1024, t)

    def body(x_ref, g_ref, w_ref, y_ref, xn_ref):
        @pl.when(pl.program_id(1) == 0)
        def _():
            xn_ref[...] = _rms(x_ref[...], g_ref[...]).astype(BF16)

        y = _dot(xn_ref[...], w_ref[...])
        y_ref[...] = jnp.square(jnp.maximum(y, 0.0)).astype(BF16) if relu2 else y

    out_shape = [_sds((t, n), BF16 if relu2 else F32), _sds((t, d), BF16)]
    out_specs = [pl.BlockSpec((tm, tn), lambda i, j: (i, j)), pl.BlockSpec((tm, d), lambda i, j: (i, 0))]
    return pl.pallas_call(
        body, name=name, grid=(t // tm, n // tn),
        in_specs=[pl.BlockSpec((tm, d), lambda i, j: (i, 0)), pl.BlockSpec((1, d), lambda i, j: (0, 0)), w_spec],
        out_specs=out_specs, out_shape=out_shape, compiler_params=_params(2),
    )(x, g.reshape(1, d), w)


def _matmul_res(name, a, w, res, *, tn=512, after=None):
    t, k = a.shape
    n = w.shape[1]
    tm = min(1024, t)

    def body(a_ref, w_ref, r_ref, *rest):
        rest[-1][...] = r_ref[...] + _dot(a_ref[...], w_ref[...])

    extra = [] if after is None else [after]
    return pl.pallas_call(
        body, name=name, grid=(t // tm, n // tn),
        in_specs=[pl.BlockSpec((tm, k), lambda i, j: (i, 0)), pl.BlockSpec((k, tn), lambda i, j: (0, j)),
                  pl.BlockSpec((tm, tn), lambda i, j: (i, j))] + [pl.BlockSpec((8, LANES), lambda i, j: (0, 0))] * len(extra),
        out_specs=pl.BlockSpec((tm, tn), lambda i, j: (i, j)), out_shape=_sds((t, n), F32),
        compiler_params=_params(2),
    )(a, w, res, *extra)


def _matmul_nt(name, dy, w, *, a=None, tk=1024):
    t, n = dy.shape
    k = w.shape[0]
    tm = min(1024, t)

    def body(dy_ref, w_ref, *rest):
        o_ref = rest[-1]
        r = _dot_nt(dy_ref[...].astype(BF16), w_ref[...])
        if a is not None:
            r = r * (2.0 * jnp.sqrt(rest[0][...].astype(F32)))
        o_ref[...] = r.astype(BF16)

    in_specs = [pl.BlockSpec((tm, n), lambda i, j: (i, 0)), pl.BlockSpec((tk, n), lambda i, j: (j, 0))]
    args = [dy, w]
    if a is not None:
        in_specs.append(pl.BlockSpec((tm, tk), lambda i, j: (i, j)))
        args.append(a)
    return pl.pallas_call(
        body, name=name, grid=(t // tm, k // tk), in_specs=in_specs,
        out_specs=pl.BlockSpec((tm, tk), lambda i, j: (i, j)), out_shape=_sds((t, k), BF16),
        compiler_params=_params(2),
    )(*args)


def _matmul_nt_rmsbwd(name, dy, w, x, g, dres):
    t, k = dy.shape
    blocked = w.ndim == 3
    d = w.shape[1] if blocked else w.shape[0]
    tm = min(512, t)

    def body(dy_ref, w_ref, x_ref, g_ref, r_ref, dx_ref, dg_ref):
        if blocked:
            kb = w.shape[2]
            dxn = _dot_nt(dy_ref[:, 0:kb].astype(BF16), w_ref[0])
            for j in range(1, w.shape[0]):
                dxn += _dot_nt(dy_ref[:, j * kb:(j + 1) * kb].astype(BF16), w_ref[j])
        else:
            dxn = _dot_nt(dy_ref[...].astype(BF16), w_ref[...])
        _, vjp = jax.vjp(_rms, x_ref[...], g_ref[...])
        dx, dg = vjp(dxn)
        dx_ref[...] = r_ref[...] + dx

        @pl.when(pl.program_id(0) == 0)
        def _():
            dg_ref[...] = jnp.zeros_like(dg_ref)

        dg_ref[...] += dg

    row = lambda i: (i, 0)
    fixed = lambda i: (0, 0)
    return pl.pallas_call(
        body, name=name, grid=(t // tm,),
        in_specs=[pl.BlockSpec((tm, k), row), pl.BlockSpec(w.shape, (lambda i: (0, 0, 0)) if blocked else fixed),
                  pl.BlockSpec((tm, d), row), pl.BlockSpec((1, d), fixed), pl.BlockSpec((tm, d), row)],
        out_specs=[pl.BlockSpec((tm, d), row), pl.BlockSpec((1, d), fixed)],
        out_shape=[_sds((t, d), F32), _sds((1, d), F32)], compiler_params=_params(1),
    )(dy, w, x, g.reshape(1, d), dres)


def _matmul_tn(name, a, b, *, tk=1024, tn=1024, col_blocks=None):
    t, k = a.shape
    n = b.shape[1]
    tk = min(tk, k)
    if col_blocks:
        tn = n // col_blocks
        out_spec, out_shape = pl.BlockSpec((None, tk, tn), lambda i, j: (j, i, 0)), _sds((col_blocks, k, tn), BF16)
    else:
        tn = min(tn, n)
        out_spec, out_shape = pl.BlockSpec((tk, tn), lambda i, j: (i, j)), _sds((k, n), BF16)

    def body(a_ref, b_ref, o_ref):
        o_ref[...] = _dot_tn(a_ref[...].astype(BF16), b_ref[...].astype(BF16)).astype(BF16)

    return pl.pallas_call(
        body, name=name, grid=(k // tk, n // tn),
        in_specs=[pl.BlockSpec((t, tk), lambda i, j: (0, i)), pl.BlockSpec((t, tn), lambda i, j: (0, j))],
        out_specs=out_spec, out_shape=out_shape, compiler_params=_params(2),
    )(a, b)


def _join_col_blocks(name, blocks, width):
    b, r, c = blocks.shape
    tr = min(256, r)

    def body(g_ref, o_ref):
        o_ref[...] = jnp.zeros_like(o_ref)
        for j in range(b):
            o_ref[:, c * j:c * (j + 1)] = g_ref[j]

    return pl.pallas_call(
        body, name=name, grid=(r // tr,), in_specs=[pl.BlockSpec((b, tr, c), lambda i: (0, i, 0))],
        out_specs=pl.BlockSpec((tr, width), lambda i: (i, 0)), out_shape=_sds((r, width), blocks.dtype),
        compiler_params=_params(1),
    )(blocks)


def _split_col_blocks(name, mat, b, c):
    r, width = mat.shape
    tr = min(256, r)

    def body(w_ref, o_ref):
        for j in range(b):
            o_ref[j] = w_ref[:, c * j:c * (j + 1)]

    return pl.pallas_call(
        body, name=name, grid=(r // tr,), in_specs=[pl.BlockSpec((tr, width), lambda i: (i, 0))],
        out_specs=pl.BlockSpec((b, tr, c), lambda i: (0, i, 0)), out_shape=_sds((b, r, c), mat.dtype),
        compiler_params=_params(1),
    )(mat)


def _loss_grad(name, y, target):
    t, d = y.shape
    tm = min(512, t)

    def body(y_ref, t_ref, dy_ref, l_ref):
        e = y_ref[...] - t_ref[...]
        dy_ref[...] = e * (1.0 / d)

        @pl.when(pl.program_id(0) == 0)
        def _():
            l_ref[...] = jnp.zeros_like(l_ref)

        l_ref[...] += 0.5 * jnp.sum(jnp.mean(e * e, axis=-1))

    row = lambda i: (i, 0)
    return pl.pallas_call(
        body, name=name, grid=(t // tm,), in_specs=[pl.BlockSpec((tm, d), row), pl.BlockSpec((tm, d), row)],
        out_specs=[pl.BlockSpec((tm, d), row), pl.BlockSpec((8, LANES), lambda i: (0, 0))],
        out_shape=[_sds((t, d), F32), _sds((8, LANES), F32)], compiler_params=_params(1),
    )(y, target)


def _gate_fwd(name, proj, b_pad, n_heads, gate_col):
    t = proj.shape[0]
    tb = min(256, t)
    tri = jnp.asarray(np.tril(np.ones((tb, tb), np.float32)))

    def body(p_ref, b_ref, tri_ref, c_ref, carry):
        @pl.when(pl.program_id(0) == 0)
        def _():
            carry[...] = jnp.zeros_like(carry)

        lane = lax.broadcasted_iota(jnp.int32, (tb, LANES), 1)
        lf = jnp.where(lane < n_heads, jax.nn.log_sigmoid(p_ref[...] + b_ref[...]), 0.0)
        c = _dot_exact(tri_ref[...], lf) + carry[0:1, :]
        c_ref[...] = c
        carry[...] = jnp.broadcast_to(c[tb - 1:tb, :], carry.shape)

    return pl.pallas_call(
        body, name=name, grid=(t // tb,),
        in_specs=[pl.BlockSpec((tb, LANES), lambda i: (i, gate_col)), pl.BlockSpec((1, LANES), lambda i: (0, 0)),
                  pl.BlockSpec((tb, tb), lambda i: (0, 0))],
        out_specs=pl.BlockSpec((tb, LANES), lambda i: (i, 0)), out_shape=_sds((t, LANES), F32),
        scratch_shapes=[pltpu.VMEM((8, LANES), F32)], compiler_params=_params(1),
    )(proj, b_pad, tri)


def _gate_bwd(name, proj, b_pad, dc, n_heads, gate_col):
    t = proj.shape[0]
    tb = min(256, t)
    nb = t // tb
    triu = jnp.asarray(np.triu(np.ones((tb, tb), np.float32)))

    def body(p_ref, b_ref, dc_ref, tri_ref, df_ref, db_ref, carry):
        @pl.when(pl.program_id(0) == 0)
        def _():
            carry[...] = jnp.zeros_like(carry)
            db_ref[...] = jnp.zeros_like(db_ref)

        dcv = dc_ref[...]
        dlf = _dot_exact(tri_ref[...], dcv) + carry[0:1, :]
        carry[...] = jnp.broadcast_to(dlf[0:1, :], carry.shape)
        lane = lax.broadcasted_iota(jnp.int32, (tb, LANES), 1)
        z = p_ref[...] + b_ref[...]
        df = jnp.where(lane < n_heads, dlf / (1.0 + jnp.exp(z)), 0.0)
        df_ref[...] = df.astype(BF16)
        db_ref[...] += jnp.sum(df, axis=0, keepdims=True)

    return pl.pallas_call(
        body, name=name, grid=(nb,),
        in_specs=[pl.BlockSpec((tb, LANES), lambda i: (nb - 1 - i, gate_col)), pl.BlockSpec((1, LANES), lambda i: (0, 0)),
                  pl.BlockSpec((tb, LANES), lambda i: (nb - 1 - i, 0)), pl.BlockSpec((tb, tb), lambda i: (0, 0))],
        out_specs=[pl.BlockSpec((tb, LANES), lambda i: (nb - 1 - i, 0)), pl.BlockSpec((1, LANES), lambda i: (0, 0))],
        out_shape=[_sds((t, LANES), BF16), _sds((1, LANES), F32)],
        scratch_shapes=[pltpu.VMEM((8, LANES), F32)], compiler_params=_params(1),
    )(proj, b_pad, dc, triu)


def _qhead(qp, g):
    return _rms(qp, g) * (HEAD_DIM ** -0.5)


def _column(mat, idx):
    lane = lax.broadcasted_iota(jnp.int32, mat.shape, 1)
    return jnp.sum(jnp.where(lane == idx, mat, 0.0), axis=1, keepdims=True)


def _fox_scores(kk, qi, ckey, cq_i, i, bq):
    length = kk.shape[0]
    s = _dot_nt(kk, qi) + cq_i - ckey[:length]
    key = lax.broadcasted_iota(jnp.int32, (length, bq), 0)
    qry = lax.broadcasted_iota(jnp.int32, (length, bq), 1) + i * bq
    return jnp.where(key <= qry, s, NEG)


def _fox_fwd(name, proj, c, crow, gq, gk, n_heads):
    t = proj.shape[0]
    hw = n_heads * HEAD_DIM
    npair = n_heads // 2
    bq = min(256, t)
    nq = t // bq

    def body(q_ref, k_ref, v_ref, c_ref, crow_ref, gq_ref, gk_ref, o_ref, lse_ref):
        hp = pl.program_id(0)
        lse_ref[...] = jnp.zeros_like(lse_ref)
        outs = []
        for hh in range(2):
            sl = slice(hh * HEAD_DIM, (hh + 1) * HEAD_DIM)
            qn = _qhead(q_ref[:, sl], gq_ref[...]).astype(BF16)
            kn = _rms(k_ref[:, sl], gk_ref[...]).astype(BF16)
            v_t = v_ref[:, sl].T.astype(BF16)
            ckey = _column(c_ref[...], 2 * hp + hh)
            cq = crow_ref[0, hh:hh + 1, :]
            o_blocks = []
            for i in range(nq):
                cols = slice(i * bq, (i + 1) * bq)
                length = (i + 1) * bq
                s = _fox_scores(kn[:length], qn[cols], ckey, cq[:, cols], i, bq)
                m = jnp.max(s, axis=0, keepdims=True)
                p = jnp.exp(s - m)
                l = jnp.sum(p, axis=0, keepdims=True)
                o_blocks.append((_dot(v_t[:, :length], p.astype(BF16)) / l).T)
                lse_ref[0, hh:hh + 1, cols] = m + jnp.log(l)
            outs.append(jnp.concatenate(o_blocks, axis=0))
        o_ref[...] = jnp.concatenate(outs, axis=1).astype(BF16)

    col = lambda off: (lambda h: (0, off + h))
    fixed = lambda h: (0, 0)
    return pl.pallas_call(
        body, name=name, grid=(npair,),
        in_specs=[pl.BlockSpec((t, LANES), col(0)), pl.BlockSpec((t, LANES), col(npair)), pl.BlockSpec((t, LANES), col(2 * npair)),
                  pl.BlockSpec((t, LANES), fixed), pl.BlockSpec((1, 2, t), lambda h: (h, 0, 0)),
                  pl.BlockSpec((1, HEAD_DIM), fixed), pl.BlockSpec((1, HEAD_DIM), fixed)],
        out_specs=[pl.BlockSpec((t, LANES), col(0)), pl.BlockSpec((1, 8, t), lambda h: (h, 0, 0))],
        out_shape=[_sds((t, hw), BF16), _sds((npair, 8, t), F32)], compiler_params=_params(1),
    )(proj, proj, proj, c, crow, gq, gk)


def _fox_bwd(name, proj, c, crow, gq, gk, lse, do, n_heads):
    t = proj.shape[0]
    hw = n_heads * HEAD_DIM
    npair = n_heads // 2
    bq = min(256, t)
    nq = t // bq

    def body(q_ref, k_ref, v_ref, c_ref, crow_ref, gq_ref, gk_ref, lse_ref, do_ref,
             dq_ref, dk_ref, dv_ref, dc_ref, dgq_ref, dgk_ref, dk_acc, dv_acc, dc_acc):
        hp = pl.program_id(0)

        @pl.when(hp == 0)
        def _():
            dgq_ref[...] = jnp.zeros_like(dgq_ref)
            dgk_ref[...] = jnp.zeros_like(dgk_ref)

        lane = lax.broadcasted_iota(jnp.int32, (t, LANES), 1)
        dc_pair = jnp.zeros((t, LANES), F32)
        dqs, dks, dvs = [], [], []
        for hh in range(2):
            sl = slice(hh * HEAD_DIM, (hh + 1) * HEAD_DIM)
            qf, q_vjp = jax.vjp(_qhead, q_ref[:, sl], gq_ref[...])
            kf, k_vjp = jax.vjp(_rms, k_ref[:, sl], gk_ref[...])
            qn, kn, kn_t = qf.astype(BF16), kf.astype(BF16), kf.T.astype(BF16)
            vb = v_ref[:, sl].astype(BF16)
            dob = do_ref[:, sl]
            ckey = _column(c_ref[...], 2 * hp + hh)
            cq = crow_ref[0, hh:hh + 1, :]
            lse_h = lse_ref[0, hh:hh + 1, :]
            dk_acc[...] = jnp.zeros_like(dk_acc)
            dv_acc[...] = jnp.zeros_like(dv_acc)
            dc_acc[...] = jnp.zeros_like(dc_acc)
            dq_blocks = []
            for i in range(nq):
                cols = slice(i * bq, (i + 1) * bq)
                length = (i + 1) * bq
                qi, doi = qn[cols], dob[cols]
                s = _fox_scores(kn[:length], qi, ckey, cq[:, cols], i, bq)
                p = jnp.exp(s - lse_h[:, cols])
                dp = _dot_nt(vb[:length], doi)
                ds = p * (dp - jnp.sum(p * dp, axis=0, keepdims=True))
                dsb = ds.astype(BF16)
                dq_blocks.append(_dot(kn_t[:, :length], dsb).T)
                dk_acc[0:length, :] += _dot(dsb, qi)
                dv_acc[0:length, :] += _dot(p.astype(BF16), doi)
                part = ds[:, 0:LANES]
                for j in range(1, bq // LANES):
                    part = part + ds[:, j * LANES:(j + 1) * LANES]
                dc_acc[0:length, :] += part
            dqp, dgq = q_vjp(jnp.concatenate(dq_blocks, axis=0))
            dkp, dgk = k_vjp(dk_acc[...])
            dgq_ref[...] += dgq
            dgk_ref[...] += dgk
            dqs.append(dqp)
            dks.append(dkp)
            dvs.append(dv_acc[...])
            dc_pair = jnp.where(lane == hh, -jnp.sum(dc_acc[...], axis=1, keepdims=True), dc_pair)
        dq_ref[...] = jnp.concatenate(dqs, axis=1).astype(BF16)
        dk_ref[...] = jnp.concatenate(dks, axis=1).astype(BF16)
        dv_ref[...] = jnp.concatenate(dvs, axis=1).astype(BF16)
        dc_ref[...] = dc_pair

    col = lambda off: (lambda h: (0, off + h))
    fixed = lambda h: (0, 0)
    pair_blk = pl.BlockSpec((t, LANES), col(0))
    return pl.pallas_call(
        body, name=name, grid=(npair,),
        in_specs=[pl.BlockSpec((t, LANES), col(0)), pl.BlockSpec((t, LANES), col(npair)), pl.BlockSpec((t, LANES), col(2 * npair)),
                  pl.BlockSpec((t, LANES), fixed), pl.BlockSpec((1, 2, t), lambda h: (h, 0, 0)),
                  pl.BlockSpec((1, HEAD_DIM), fixed), pl.BlockSpec((1, HEAD_DIM), fixed),
                  pl.BlockSpec((1, 8, t), lambda h: (h, 0, 0)), pair_blk],
        out_specs=[pair_blk, pair_blk, pair_blk, pair_blk,
                   pl.BlockSpec((1, HEAD_DIM), fixed), pl.BlockSpec((1, HEAD_DIM), fixed)],
        out_shape=[_sds((t, hw), BF16), _sds((t, hw), BF16), _sds((t, hw), BF16), _sds((t, npair * LANES), F32),
                   _sds((1, HEAD_DIM), F32), _sds((1, HEAD_DIM), F32)],
        scratch_shapes=[pltpu.VMEM((t, HEAD_DIM), F32), pltpu.VMEM((t, HEAD_DIM), F32), pltpu.VMEM((t, LANES), F32)],
        compiler_params=_params(1),
    )(proj, proj, proj, c, crow, gq, gk, lse, do)


def _t5_bucket_table():
    dist = np.arange(WINDOW)[None, :] + WINDOW - np.arange(2 * WINDOW)[:, None]
    n = np.maximum(dist, 0)
    max_exact = N_BUCKETS // 2
    large = max_exact + (np.log(np.maximum(n, 1) / max_exact) / np.log(REL_MAX_DIST / max_exact)
                         * (N_BUCKETS - max_exact)).astype(np.int32)
    large = np.minimum(large, N_BUCKETS - 1)
    return np.where(n < max_exact, n, large).astype(np.int32).reshape(1, -1)


def _bias_expand(name, rel_bias_t):
    n_heads = rel_bias_t.shape[0]
    tbl = jnp.asarray(_t5_bucket_table())
    width = tbl.shape[1]

    def body(rb_ref, tbl_ref, o_ref):
        onehot = (lax.broadcasted_iota(jnp.int32, (N_BUCKETS, width), 0) == tbl_ref[...]).astype(F32)
        o_ref[...] = _dot_exact(rb_ref[...], onehot)

    return pl.pallas_call(body, name=name, out_shape=_sds((n_heads, width), F32), compiler_params=_params(0))(rel_bias_t, tbl)


def _bias_reduce(name, dbias):
    n_heads, width = dbias.shape
    tbl = jnp.asarray(_t5_bucket_table())

    def body(db_ref, tbl_ref, o_ref):
        onehot = (lax.broadcasted_iota(jnp.int32, (N_BUCKETS, width), 0) == tbl_ref[...]).astype(F32)
        o_ref[...] = lax.dot_general(db_ref[...], onehot, (((1,), (1,)), ((), ())), preferred_element_type=F32,
                                     precision=lax.Precision.HIGHEST)

    return pl.pallas_call(body, name=name, out_shape=_sds((n_heads, N_BUCKETS), F32), compiler_params=_params(0))(dbias, tbl)


def _swa_mask(n, group):
    j = lax.broadcasted_iota(jnp.int32, (2 * WINDOW, group * WINDOW), 0)
    i = lax.broadcasted_iota(jnp.int32, (2 * WINDOW, group * WINDOW), 1) & (WINDOW - 1)
    ok = (j > i) & (j <= i + WINDOW) & ((n > 0) | (j >= WINDOW))
    return jnp.where(ok, 0.0, NEG)


def _swa_stack(ref, start, group):
    return jnp.concatenate([ref[pl.ds(start, WINDOW), g * HEAD_DIM:(g + 1) * HEAD_DIM] for g in range(group)], axis=0)


def _swa_fwd(name, qb, kh, vh, gq, gk, sinks, bias, group):
    t = qb.shape[0]
    kvh = kh.shape[0]
    nblk = t // WINDOW
    gw = group * HEAD_DIM
    band = 2 * WINDOW
    cols = group * WINDOW

    def body(q_ref, k_ref, v_ref, gq_ref, gk_ref, sink_ref, bias_ref, o_ref, lse_ref, qs, kpad, vpad):
        for g in range(group):
            qs[:, g * HEAD_DIM:(g + 1) * HEAD_DIM] = _qhead(q_ref[:, g * HEAD_DIM:(g + 1) * HEAD_DIM], gq_ref[...]).astype(BF16)
        kpad[0:WINDOW, :] = jnp.zeros((WINDOW, HEAD_DIM), BF16)
        vpad[0:WINDOW, :] = jnp.zeros((WINDOW, HEAD_DIM), BF16)
        kpad[WINDOW:, :] = _rms(k_ref[0], gk_ref[...]).astype(BF16)
        vpad[WINDOW:, :] = v_ref[0].astype(BF16)
        sink = sink_ref[0]

        def block(n, carry):
            start = pl.multiple_of(n * WINDOW, WINDOW)
            kb = kpad[pl.ds(start, band), :]
            vb = vpad[pl.ds(start, band), :]
            s = _dot_nt(kb, _swa_stack(qs, start, group)) + bias_ref[0] + _swa_mask(n, group)
            m = jnp.maximum(jnp.max(s, axis=0, keepdims=True), sink)
            e = jnp.exp(s - m)
            l = jnp.sum(e, axis=0, keepdims=True) + jnp.exp(sink - m)
            o_t = _dot_tn(vb, e.astype(BF16)) / l
            for g in range(group):
                o_ref[pl.ds(start, WINDOW), g * HEAD_DIM:(g + 1) * HEAD_DIM] = o_t[:, g * WINDOW:(g + 1) * WINDOW].T.astype(BF16)
            lse_ref[pl.ds(n, 1), :] = m + jnp.log(l)
            return carry

        lax.fori_loop(0, nblk, block, 0)

    fixed = lambda h: (0, 0)
    per = lambda h: (h, 0, 0)
    return pl.pallas_call(
        body, name=name, grid=(kvh,),
        in_specs=[pl.BlockSpec((t, gw), lambda h: (0, h)), pl.BlockSpec((1, t, HEAD_DIM), per), pl.BlockSpec((1, t, HEAD_DIM), per),
                  pl.BlockSpec((1, HEAD_DIM), fixed), pl.BlockSpec((1, HEAD_DIM), fixed),
                  pl.BlockSpec((1, 1, cols), per), pl.BlockSpec((1, band, cols), per)],
        out_specs=[pl.BlockSpec((t, gw), lambda h: (0, h)), pl.BlockSpec((nblk, cols), lambda h: (h, 0))],
        out_shape=[_sds((t, kvh * gw), BF16), _sds((kvh * nblk, cols), F32)],
        scratch_shapes=[pltpu.VMEM((t, gw), BF16), pltpu.VMEM((t + WINDOW, HEAD_DIM), BF16),
                        pltpu.VMEM((t + WINDOW, HEAD_DIM), BF16)],
        compiler_params=_params(1),
    )(qb, kh, vh, gq, gk, sinks, bias)


def _swa_bwd(name, qb, kh, vh, gq, gk, sinks, bias, lse, do, group):
    t = qb.shape[0]
    kvh = kh.shape[0]
    nblk = t // WINDOW
    gw = group * HEAD_DIM
    band = 2 * WINDOW
    cols = group * WINDOW

    def body(q_ref, k_ref, v_ref, gq_ref, gk_ref, sink_ref, bias_ref, lse_ref, do_ref,
             dq_ref, dk_ref, dv_ref, dgq_ref, dgk_ref, dsink_ref, dbias_ref,
             qs, kpad, vpad, dqs, dk_acc, dv_acc, dsink_acc):
        @pl.when(pl.program_id(0) == 0)
        def _():
            dgq_ref[...] = jnp.zeros_like(dgq_ref)
            dgk_ref[...] = jnp.zeros_like(dgk_ref)

        for g in range(group):
            qs[:, g * HEAD_DIM:(g + 1) * HEAD_DIM] = _qhead(q_ref[:, g * HEAD_DIM:(g + 1) * HEAD_DIM], gq_ref[...]).astype(BF16)
        kpad[0:WINDOW, :] = jnp.zeros((WINDOW, HEAD_DIM), BF16)
        vpad[0:WINDOW, :] = jnp.zeros((WINDOW, HEAD_DIM), BF16)
        kpad[WINDOW:, :] = _rms(k_ref[0], gk_ref[...]).astype(BF16)
        vpad[WINDOW:, :] = v_ref[0].astype(BF16)
        dk_acc[...] = jnp.zeros_like(dk_acc)
        dv_acc[...] = jnp.zeros_like(dv_acc)
        dsink_acc[...] = jnp.zeros_like(dsink_acc)
        dbias_ref[...] = jnp.zeros_like(dbias_ref)
        sink = sink_ref[0]

        def block(n, carry):
            start = pl.multiple_of(n * WINDOW, WINDOW)
            kb = kpad[pl.ds(start, band), :]
            vb = vpad[pl.ds(start, band), :]
            q = _swa_stack(qs, start, group)
            dob = _swa_stack(do_ref, start, group)
            lse_n = lse_ref[pl.ds(n, 1), :]
            s = _dot_nt(kb, q) + bias_ref[0] + _swa_mask(n, group)
            p = jnp.exp(s - lse_n)
            dp = _dot_nt(vb, dob)
            dsum = jnp.sum(p * dp, axis=0, keepdims=True)
            ds = p * (dp - dsum)
            dsb = ds.astype(BF16)
            dsink_acc[...] -= jnp.exp(sink - lse_n) * dsum
            dbias_ref[0] += ds
            dq = _dot_tn(dsb, kb)
            for g in range(group):
                dqs[pl.ds(start, WINDOW), g * HEAD_DIM:(g + 1) * HEAD_DIM] = dq[g * WINDOW:(g + 1) * WINDOW]
            dk_acc[pl.ds(start, band), :] += _dot(dsb, q)
            dv_acc[pl.ds(start, band), :] += _dot(p.astype(BF16), dob)
            return carry

        lax.fori_loop(0, nblk, block, 0)
        for g in range(group):
            _, q_vjp = jax.vjp(_qhead, q_ref[:, g * HEAD_DIM:(g + 1) * HEAD_DIM], gq_ref[...])
            dqp, dgq = q_vjp(dqs[:, g * HEAD_DIM:(g + 1) * HEAD_DIM])
            dq_ref[:, g * HEAD_DIM:(g + 1) * HEAD_DIM] = dqp.astype(BF16)
            dgq_ref[...] += dgq
            dsink_g = jnp.sum(dsink_acc[:, g * WINDOW:(g + 1) * WINDOW], axis=1, keepdims=True)
            dsink_ref[0, g:g + 1, :] = jnp.broadcast_to(dsink_g, (1, LANES))
        _, k_vjp = jax.vjp(_rms, k_ref[0], gk_ref[...])
        dkp, dgk = k_vjp(dk_acc[WINDOW:, :])
        dk_ref[0] = dkp
        dgk_ref[...] += dgk
        dv_ref[0] = dv_acc[WINDOW:, :]

    fixed = lambda h: (0, 0)
    per = lambda h: (h, 0, 0)
    wide = pl.BlockSpec((t, gw), lambda h: (0, h))
    head = pl.BlockSpec((1, t, HEAD_DIM), per)
    vec = pl.BlockSpec((1, HEAD_DIM), fixed)
    bias_spec = pl.BlockSpec((1, band, cols), per)
    return pl.pallas_call(
        body, name=name, grid=(kvh,),
        in_specs=[wide, head, head, vec, vec, pl.BlockSpec((1, 1, cols), per), bias_spec,
                  pl.BlockSpec((nblk, cols), lambda h: (h, 0)), wide],
        out_specs=[wide, head, head, vec, vec, pl.BlockSpec((1, group, LANES), per), bias_spec],
        out_shape=[_sds((t, kvh * gw), BF16), _sds((kvh, t, HEAD_DIM), F32), _sds((kvh, t, HEAD_DIM), F32),
                   _sds((1, HEAD_DIM), F32), _sds((1, HEAD_DIM), F32),
                   _sds((kvh, group, LANES), F32), _sds((kvh, band, cols), F32)],
        scratch_shapes=[pltpu.VMEM((t, gw), BF16), pltpu.VMEM((t + WINDOW, HEAD_DIM), BF16),
                        pltpu.VMEM((t + WINDOW, HEAD_DIM), BF16), pltpu.VMEM((t, gw), F32),
                        pltpu.VMEM((t + WINDOW, HEAD_DIM), F32), pltpu.VMEM((t + WINDOW, HEAD_DIM), F32),
                        pltpu.VMEM((1, cols), F32)],
        compiler_params=_params(1),
    )(qb, kh, vh, gq, gk, sinks, bias, lse, do)


def _local_step(x, target, p, comm):
    t, d = x.shape
    n_heads = d // HEAD_DIM
    kv_heads = n_heads // 8
    group = n_heads // kv_heads
    hw = n_heads * HEAD_DIM
    gate_col = 3 * hw // LANES
    kvw = kv_heads * HEAD_DIM
    grads = {}

    def mlp_fwd(tag, h, g, layer):
        w_up, = comm.weights([f"w_up{layer}"], h)
        a, hn = _norm_matmul(f"{tag}_up", h, g, w_up, relu2=True)
        w_down, = comm.weights([f"w_down{layer}"], a)
        return _matmul_res(f"{tag}_down", a, w_down, h), (h, g, hn, a, w_up, w_down)

    def mlp_bwd(tag, saved, layer, dy):
        h, g, hn, a, w_up, w_down = saved
        du = _matmul_nt(f"{tag}_du", dy, w_down, a=a)
        dw_down = _matmul_tn(f"{tag}_dwdown", a, dy)
        dw_up = _matmul_tn(f"{tag}_dwup", hn, du, col_blocks=w_up.shape[0])
        zero = comm.send_grads(tag, {f"w_down{layer}": dw_down, f"w_up{layer}": dw_up})
        return _matmul_nt_rmsbwd(f"{tag}_dh", du, w_up, h, g + zero, dy)

    comm.prefetch(["w_in_a"], None)
    w_in, = comm.weights(["w_in_a"], None)
    proj, xn1 = _norm_matmul("a_inproj", x, p["g_attn"][0], w_in, tn=640)
    ahead = comm.prefetch(["w_out_a"], proj)
    b_pad = jnp.pad(p["b_f"], ((0, 0), (0, LANES - n_heads))) + ahead[0:1, :]
    c = _gate_fwd("a_gate", proj, b_pad, n_heads, gate_col)
    crow = c[:, :n_heads].T.reshape(n_heads // 2, 2, t)
    o_a, lse_a = _fox_fwd("a_attn", proj, c, crow, p["gq_a"], p["gk_a"], n_heads)
    ahead = comm.prefetch(["w_up0", "w_down0", "w_kv", "w_q_b", "w_out_b"], o_a)
    w_out_a, = comm.weights(["w_out_a"], o_a)
    h1 = _matmul_res("a_outproj", o_a, w_out_a, x, after=ahead)
    h2, mlp0 = mlp_fwd("mlp0", h1, p["g_mlp"][0], 0)

    ahead = comm.prefetch(["w_up1", "w_down1"], h2)
    w_kv, w_q_b = comm.weights(["w_kv", "w_q_b"], h2)
    kv, hn_kv = _norm_matmul("kv_proj", h2, p["g_kv"] + ahead[0, 0], w_kv, tn=2 * kvw)
    kh = kv[:, :kvw].reshape(t, kv_heads, HEAD_DIM).transpose(1, 0, 2)
    vh = kv[:, kvw:].reshape(t, kv_heads, HEAD_DIM).transpose(1, 0, 2)
    qb, hn_q = _norm_matmul("b_qproj", h2, p["g_attn"][1], w_q_b, tn=512)
    gqb, gkb = p["gq_b"], p["gk_b"].reshape(1, HEAD_DIM)
    bias = _bias_expand("b_bias", p["rel_bias"].T).reshape(kv_heads, group, 2 * WINDOW, WINDOW)
    bias = bias.transpose(0, 2, 1, 3).reshape(kv_heads, 2 * WINDOW, group * WINDOW)
    sink_rows = jnp.broadcast_to(p["sinks"].reshape(kv_heads, 1, group, 1), (kv_heads, 1, group, WINDOW)).reshape(kv_heads, 1, group * WINDOW)
    o_b, lse_b = _swa_fwd("b_attn", qb, kh, vh, gqb, gkb, sink_rows, bias, group)
    w_out_b, = comm.weights(["w_out_b"], o_b)
    h3 = _matmul_res("b_outproj", o_b, w_out_b, h2)
    y, mlp1 = mlp_fwd("mlp1", h3, p["g_mlp"][1], 1)
    dy, loss_tile = _loss_grad("loss", y, target)

    dh3, dg_mlp1 = mlp_bwd("mlp1", mlp1, 1, dy)
    do_b = _matmul_nt("b_do", dh3, w_out_b)
    dw_out_b = _matmul_tn("b_dwout", o_b, dh3)
    dqb, dkh, dvh, grads["gq_b"], dgk_b, dsink, dbias = _swa_bwd(
        "b_attn_bwd", qb, kh, vh, gqb, gkb, sink_rows, bias, lse_b, do_b, group)
    grads["gk_b"] = dgk_b
    grads["sinks"] = dsink[:, :, 0].reshape(1, n_heads)
    dbias = dbias.reshape(kv_heads, 2 * WINDOW, group, WINDOW).transpose(0, 2, 1, 3)
    grads["rel_bias"] = _bias_reduce("b_dbias", dbias.reshape(n_heads, WINDOW * 2 * WINDOW)).T
    dw_q_b = _matmul_tn("b_dwq", hn_q, dqb)
    dh2, dg_attn1 = _matmul_nt_rmsbwd("b_dhq", dqb, w_q_b, h2, p["g_attn"][1], dh3)
    dkv = jnp.concatenate([dkh.transpose(1, 0, 2).reshape(t, kvw), dvh.transpose(1, 0, 2).reshape(t, kvw)], axis=1)
    dw_kv = _matmul_tn("kv_dw", hn_kv, dkv)
    zero = comm.send_grads("attn_b", {"w_out_b": dw_out_b, "w_q_b": dw_q_b, "w_kv": dw_kv})
    dh2, dg_kv = _matmul_nt_rmsbwd("kv_dh", dkv, w_kv, h2, p["g_kv"] + zero, dh2)
    grads["g_kv"] = dg_kv
    dh1, dg_mlp0 = mlp_bwd("mlp0", mlp0, 0, dh2)
    grads["g_mlp"] = (dg_mlp0, dg_mlp1)

    do_a = _matmul_nt("a_do", dh1, w_out_a)
    dw_out_a = _matmul_tn("a_dwout", o_a, dh1)
    zero = comm.send_grads("attn_a_out", {"w_out_a": dw_out_a})
    dq, dk, dv, dc_cols, grads["gq_a"], grads["gk_a"] = _fox_bwd(
        "a_attn_bwd", proj, c, crow, p["gq_a"] + zero, p["gk_a"], lse_a, do_a, n_heads)
    dc = jnp.pad(dc_cols.reshape(t, n_heads // 2, LANES)[:, :, :2].reshape(t, n_heads), ((0, 0), (0, LANES - n_heads)))
    dfl, db_f = _gate_bwd("a_gate_bwd", proj, b_pad, dc, n_heads, gate_col)
    grads["b_f"] = db_f
    dproj = jnp.concatenate([dq, dk, dv, dfl], axis=1)
    dw_in = _matmul_tn("a_dwin", xn1, dproj, tn=640)
    zero = comm.send_grads("attn_a_in", {"w_in_a": dw_in})
    grad_x, dg_attn0 = _matmul_nt_rmsbwd("a_dx", dproj, w_in, x, p["g_attn"][0] + zero, dh1)
    grads["g_attn"] = (dg_attn0, dg_attn1)
    return loss_tile, grad_x, grads


EVERYONE = (1, 2, 3, 4, 5, 6, 7)
SAME_CORE = (1, 2, 4, 6)
OTHER_CHIPS = (2, 4, 6)


class _InFlight:
    def __init__(self, scatter, ks, send_sems, recv_sems, srcs, lands, token):
        self.scatter, self.ks, self.send_sems, self.recv_sems = scatter, ks, send_sems, recv_sems
        self.srcs, self.lands, self.token = list(srcs), list(lands), token


def _mesh_peers(ks=EVERYONE):
    x, y, c = lax.axis_index("x"), lax.axis_index("y"), lax.axis_index("c")
    peers = []
    for k in ks:
        px, py, pc = x ^ ((k >> 2) & 1), y ^ ((k >> 1) & 1), c ^ (k & 1)
        peers.append(((px, py, pc), 4 * px + 2 * py + pc))
    return 4 * x + 2 * y + c, peers


_HBM_SPEC = pl.BlockSpec(memory_space=pltpu.HBM)
_SEM_SPEC = pl.BlockSpec(memory_space=pltpu.SEMAPHORE)
_SIDE_EFFECT = pltpu.SideEffectType.DATAFLOW_SIDE_EFFECTING


def _exchange_start(name, arrays, scatter, ks=EVERYONE):
    n = len(arrays)
    me, _ = _mesh_peers()
    lands = []
    for a in arrays:
        own = lax.dynamic_index_in_dim(a, me, 0, keepdims=False) if scatter else a
        shape = a.shape if scatter else (N_DEV,) + a.shape
        lands.append(lax.dynamic_update_index_in_dim(lax.empty(shape, a.dtype), own, me, 0))

    def body(*refs):
        src, land = refs[:n], refs[n:2 * n]
        send_sems, recv_sems, token = refs[2 * n], refs[2 * n + 1], refs[-1]
        pos, peers = _mesh_peers(ks)
        for a in range(n):
            for k, (peer, peer_pos) in enumerate(peers):
                pltpu.make_async_remote_copy(
                    src_ref=src[a].at[peer_pos] if scatter else src[a], dst_ref=land[a].at[pos],
                    send_sem=send_sems.at[a * len(ks) + k], recv_sem=recv_sems.at[a * len(ks) + k],
                    device_id=peer, device_id_type=pl.DeviceIdType.MESH).start()
        token[...] = jnp.zeros_like(token)

    operands = [pltpu.with_memory_space_constraint(a, pltpu.HBM) for a in list(arrays) + lands]
    outs = pl.pallas_call(
        body, name=name,
        out_shape=(pltpu.SemaphoreType.DMA((n * len(ks),)), pltpu.SemaphoreType.DMA((n * len(ks),)),
                   *[pltpu.HBM(a.shape, a.dtype) for a in operands], _sds((8, LANES), F32)),
        in_specs=[_HBM_SPEC] * (2 * n),
        out_specs=(_SEM_SPEC, _SEM_SPEC, *[_HBM_SPEC] * (2 * n), pl.BlockSpec(memory_space=pltpu.VMEM)),
        input_output_aliases={i: 2 + i for i in range(2 * n)},
        compiler_params=pltpu.CompilerParams(has_side_effects=_SIDE_EFFECT),
    )(*operands)
    return _InFlight(scatter, ks, outs[0], outs[1], outs[2:2 + n], outs[2 + n:2 + 2 * n], outs[-1])


def _exchange_wait(name, flight, which, after):
    m = len(which)
    scatter, ks = flight.scatter, flight.ks

    def body(*refs):
        src, land = refs[:m], refs[m:2 * m]
        send_sems, recv_sems = refs[2 * m], refs[2 * m + 1]
        _, peers = _mesh_peers(ks)
        for i, a in enumerate(which):
            for k, (peer, peer_pos) in enumerate(peers):
                cp = pltpu.make_async_remote_copy(
                    src_ref=src[i].at[peer_pos] if scatter else src[i], dst_ref=land[i].at[peer_pos],
                    send_sem=send_sems.at[a * len(ks) + k], recv_sem=recv_sems.at[a * len(ks) + k],
                    device_id=peer, device_id_type=pl.DeviceIdType.MESH)
                cp.wait_send()
                cp.wait_recv()

    operands = [flight.srcs[a] for a in which] + [flight.lands[a] for a in which]
    outs = pl.pallas_call(
        body, name=name, out_shape=tuple(pltpu.HBM(a.shape, a.dtype) for a in operands),
        in_specs=[_HBM_SPEC] * (2 * m) + [_SEM_SPEC, _SEM_SPEC, pl.BlockSpec(memory_space=pl.ANY)],
        out_specs=tuple([_HBM_SPEC] * (2 * m)), input_output_aliases={i: i for i in range(2 * m)},
        compiler_params=pltpu.CompilerParams(has_side_effects=_SIDE_EFFECT),
    )(*operands, flight.send_sems, flight.recv_sems, after)
    return list(outs[m:])


def _relay_start(name, lands):
    n = len(lands)

    def body(*refs):
        land, send_sems, recv_sems, token = refs[:n], refs[n], refs[n + 1], refs[-1]
        _, peers = _mesh_peers(OTHER_CHIPS)
        sibling = (lax.axis_index("x"), lax.axis_index("y"), 1 - lax.axis_index("c"))
        for a in range(n):
            for k, (_, peer_pos) in enumerate(peers):
                pltpu.make_async_remote_copy(
                    src_ref=land[a].at[peer_pos], dst_ref=land[a].at[peer_pos],
                    send_sem=send_sems.at[a * len(peers) + k], recv_sem=recv_sems.at[a * len(peers) + k],
                    device_id=sibling, device_id_type=pl.DeviceIdType.MESH).start()
        token[...] = jnp.zeros_like(token)

    count = n * len(OTHER_CHIPS)
    outs = pl.pallas_call(
        body, name=name,
        out_shape=(pltpu.SemaphoreType.DMA((count,)), pltpu.SemaphoreType.DMA((count,)),
                   *[pltpu.HBM(a.shape, a.dtype) for a in lands], _sds((8, LANES), F32)),
        in_specs=[_HBM_SPEC] * n,
        out_specs=(_SEM_SPEC, _SEM_SPEC, *[_HBM_SPEC] * n, pl.BlockSpec(memory_space=pltpu.VMEM)),
        input_output_aliases={i: 2 + i for i in range(n)},
        compiler_params=pltpu.CompilerParams(has_side_effects=_SIDE_EFFECT),
    )(*[pltpu.with_memory_space_constraint(a, pltpu.HBM) for a in lands])
    return _InFlight(False, OTHER_CHIPS, outs[0], outs[1], [], outs[2:2 + n], outs[-1])


def _relay_wait(name, flight, which, after):
    m = len(which)

    def body(*refs):
        land, send_sems, recv_sems = refs[:m], refs[m], refs[m + 1]
        _, peers = _mesh_peers(OTHER_CHIPS)
        sibling = (lax.axis_index("x"), lax.axis_index("y"), 1 - lax.axis_index("c"))
        for i, a in enumerate(which):
            for k, (_, peer_pos) in enumerate(peers):
                cp = pltpu.make_async_remote_copy(
                    src_ref=land[i].at[peer_pos], dst_ref=land[i].at[peer_pos ^ 1],
                    send_sem=send_sems.at[a * len(peers) + k], recv_sem=recv_sems.at[a * len(peers) + k],
                    device_id=sibling, device_id_type=pl.DeviceIdType.MESH)
                cp.wait_send()
                cp.wait_recv()

    operands = [flight.lands[a] for a in which]
    outs = pl.pallas_call(
        body, name=name, out_shape=tuple(pltpu.HBM(a.shape, a.dtype) for a in operands),
        in_specs=[_HBM_SPEC] * m + [_SEM_SPEC, _SEM_SPEC, pl.BlockSpec(memory_space=pl.ANY)],
        out_specs=tuple([_HBM_SPEC] * m), input_output_aliases={i: i for i in range(m)},
        compiler_params=pltpu.CompilerParams(has_side_effects=_SIDE_EFFECT),
    )(*operands, flight.send_sems, flight.recv_sems, after)
    return list(outs)


def _sum_parts(p_ref):
    g = p_ref[0].astype(F32)
    for dev in range(1, N_DEV):
        g = g + p_ref[dev].astype(F32)
    return g


def _adam_update(g, w, m, v):
    m_new = ADAM_B1 * m + (1.0 - ADAM_B1) * g
    v_new = ADAM_B2 * v + (1.0 - ADAM_B2) * jnp.square(g)
    m_hat = m_new / (1.0 - ADAM_B1 ** ADAM_STEP)
    v_hat = v_new / (1.0 - ADAM_B2 ** ADAM_STEP)
    return -ADAM_LR * (m_hat / (jnp.sqrt(v_hat) + ADAM_EPS) + ADAM_WD * w), m_new, v_new


def _adamw(name, parts, w, m, v, layer=None, into=None):
    r, c = w.shape[-2:]
    tr = r if r <= 256 else 256
    n_into = 0 if into is None else len(into)

    def body(p_ref, w_ref, m_ref, v_ref, *refs):
        g_ref, d_ref, mo_ref, vo_ref = refs[n_into:]
        g = _sum_parts(p_ref)
        g_ref[...] = g
        d_ref[...], mo_ref[...], vo_ref[...] = _adam_update(g, w_ref[...], m_ref[...], v_ref[...])

    if layer is None:
        blk = pl.BlockSpec((tr, c), lambda i: (i, 0))
    else:
        blk = pl.BlockSpec((None, tr, c), lambda i: (layer, i, 0))
    return pl.pallas_call(
        body, name=name, grid=(r // tr,),
        in_specs=[pl.BlockSpec((N_DEV, tr, c), lambda i: (0, i, 0)), blk, blk, blk] + [pl.BlockSpec(memory_space=pl.ANY)] * n_into,
        out_specs=[blk] * 4, out_shape=[_sds(w.shape, F32)] * 4,
        input_output_aliases={4 + i: i for i in range(n_into)}, compiler_params=_params(1),
    )(parts, w, m, v, *(into or ()))


SMALL_PACK_ROWS = 16
LOSS_ROW = 11


def _small_rows(grads, loss_tile):
    return [(0, 1, grads["g_attn"][0]), (1, 1, grads["g_attn"][1]), (2, 1, grads["g_mlp"][0]), (3, 1, grads["g_mlp"][1]),
            (4, 1, grads["g_kv"]), (5, 1, grads["b_f"]), (6, 1, grads["gq_a"]), (7, 1, grads["gk_a"]), (8, 1, grads["gk_b"]),
            (9, 1, grads["gq_b"]), (10, 1, grads["sinks"]), (LOSS_ROW, 1, loss_tile)]


SMALL_ROWS = {"g_attn": (0, 2), "g_mlp": (2, 2), "g_kv": (4, 1), "b_f": (5, 1), "gq_a": (6, 1), "gk_a": (7, 1),
              "gk_b": (8, 1), "gq_b": (9, 1), "sinks": (10, 1)}


def _pack_small(name, pieces, d):
    def body(*refs):
        out = refs[-1]
        out[...] = jnp.zeros_like(out)
        for (row, rows, _), ref in zip(pieces, refs[:-1]):
            out[row:row + rows, 0:ref.shape[1]] = ref[0:rows, :]

    return pl.pallas_call(body, name=name, out_shape=_sds((SMALL_PACK_ROWS, d), F32), compiler_params=_params(0))(
        *[piece for _, _, piece in pieces])


def _adamw_small(name, parts, parts_rel_bias, w, m, v):
    def body(*refs):
        ins, outs = refs[2:2 + 3 * len(SMALL)], refs[2 + 3 * len(SMALL):]
        pack, rel = _sum_parts(refs[0]), _sum_parts(refs[1])
        for i, k in enumerate(SMALL):
            w_ref, m_ref, v_ref = ins[3 * i:3 * i + 3]
            if k == "rel_bias":
                g = rel
            else:
                row, rows = SMALL_ROWS[k]
                g = pack[row:row + rows, 0:w_ref.shape[1]]
            outs[4 * i][...] = g
            outs[4 * i + 1][...], outs[4 * i + 2][...], outs[4 * i + 3][...] = _adam_update(g, w_ref[...], m_ref[...], v_ref[...])
        outs[-1][...] = pack[LOSS_ROW:LOSS_ROW + 1, 0:LANES]

    operands = [parts, parts_rel_bias] + [t[k] for k in SMALL for t in (w, m, v)]
    out_shape = [_sds(w[k].shape, F32) for k in SMALL for _ in range(4)] + [_sds((1, LANES), F32)]
    outs = pl.pallas_call(body, name=name, out_shape=out_shape, compiler_params=_params(0))(*operands)
    return {k: outs[4 * i:4 * i + 4] for i, k in enumerate(SMALL)}, outs[-1]


class _Comm:
    ORDER = ("w_in_a", "w_out_a", "w_up0", "w_down0", "w_kv", "w_q_b", "w_out_b", "w_up1", "w_down1")

    def __init__(self, shards, d, n_in):
        self.d, self.n_in = d, n_in
        self.flight = _exchange_start("gather_start", [shards[n].astype(BF16) for n in self.ORDER], scatter=False, ks=SAME_CORE)
        self.relays, self.sent = {}, []

    def prefetch(self, names, after):
        which = [self.ORDER.index(n) for n in names]
        landed = _exchange_wait(f"gather_wait_{names[0]}", self.flight, which, self.flight.token if after is None else after)
        relay = _relay_start(f"gather_relay_{names[0]}", landed)
        for n in names:
            self.relays[n] = (relay, names)
        return relay.token

    def weights(self, names, after):
        relay, group = self.relays[names[0]]
        landed = _relay_wait(f"gather_relay_wait_{names[0]}", relay, [group.index(n) for n in names],
                             relay.token if after is None else after)
        return [self._whole(n, g) for n, g in zip(names, landed)]

    def _whole(self, name, g):
        if name == "w_in_a":
            return _join_col_blocks("w_in_join", g, -(-self.n_in // LANES) * LANES)
        if name.startswith("w_up"):
            return g
        return g.reshape(-1, g.shape[-1])

    def _chunks(self, name, g):
        if name == "w_in_a":
            return _split_col_blocks("dw_in_split", g, N_DEV, self.n_in // N_DEV)
        if name.startswith("w_up"):
            return g
        return g.reshape(N_DEV, g.shape[0] // N_DEV, g.shape[1])

    def send_grads(self, tag, partials):
        names = list(partials)
        flight = _exchange_start(f"scatter_start_{tag}", [self._chunks(n, partials[n]) for n in names], scatter=True)
        self.sent.append((tag, flight, names))
        return flight.token[0, 0]

    def received(self, index, after):
        tag, flight, names = self.sent[index]
        landed = _exchange_wait(f"scatter_wait_{tag}", flight, list(range(len(names))), after)
        return dict(zip(names, landed))


def kernel(x, g_attn, g_mlp, w_in_a, b_f, gq_a, gk_a, w_out_a, g_kv, w_kv, gk_b, w_q_b, gq_b, sinks, rel_bias, w_out_b, w_up, w_down, loss_target, m_g_attn, m_g_mlp, m_w_in_a, m_b_f, m_gq_a, m_gk_a, m_w_out_a, m_g_kv, m_w_kv, m_gk_b, m_w_q_b, m_gq_b, m_sinks, m_rel_bias, m_w_out_b, m_w_up, m_w_down, v_g_attn, v_g_mlp, v_w_in_a, v_b_f, v_gq_a, v_gk_a, v_w_out_a, v_g_kv, v_w_kv, v_gk_b, v_w_q_b, v_gq_b, v_sinks, v_rel_bias, v_w_out_b, v_w_up, v_w_down):
    w = dict(g_attn=g_attn, g_mlp=g_mlp, w_in_a=w_in_a, b_f=b_f, gq_a=gq_a, gk_a=gk_a, w_out_a=w_out_a, g_kv=g_kv,
             w_kv=w_kv, gk_b=gk_b, w_q_b=w_q_b, gq_b=gq_b, sinks=sinks, rel_bias=rel_bias, w_out_b=w_out_b,
             w_up=w_up, w_down=w_down)
    mom = dict(g_attn=m_g_attn, g_mlp=m_g_mlp, w_in_a=m_w_in_a, b_f=m_b_f, gq_a=m_gq_a, gk_a=m_gk_a, w_out_a=m_w_out_a,
               g_kv=m_g_kv, w_kv=m_w_kv, gk_b=m_gk_b, w_q_b=m_w_q_b, gq_b=m_gq_b, sinks=m_sinks, rel_bias=m_rel_bias,
               w_out_b=m_w_out_b, w_up=m_w_up, w_down=m_w_down)
    var = dict(g_attn=v_g_attn, g_mlp=v_g_mlp, w_in_a=v_w_in_a, b_f=v_b_f, gq_a=v_gq_a, gk_a=v_gk_a, w_out_a=v_w_out_a,
               g_kv=v_g_kv, w_kv=v_w_kv, gk_b=v_gk_b, w_q_b=v_w_q_b, gq_b=v_gq_b, sinks=v_sinks, rel_bias=v_rel_bias,
               w_out_b=v_w_out_b, w_up=v_w_up, w_down=v_w_down)
    d = x.shape[2]
    where = {"w_in_a": ("w_in_a", 0), "w_out_a": ("w_out_a", 0), "w_kv": ("w_kv", None), "w_q_b": ("w_q_b", 0),
             "w_out_b": ("w_out_b", 0), "w_up0": ("w_up", 0), "w_up1": ("w_up", 1), "w_down0": ("w_down", 0),
             "w_down1": ("w_down", 1)}
    shards = {n: (w[k] if layer is None else w[k][layer]) for n, (k, layer) in where.items()}
    comm = _Comm(shards, d, w_in_a.shape[2] * N_DEV)
    loss_tile, grad_x, grads = _local_step(x[0], loss_target[0], {k: w[k] for k in SMALL}, comm)

    small_flight = _exchange_start(
        "gather_small_grads", [_pack_small("pack_small", _small_rows(grads, loss_tile), d), grads["rel_bias"]], scatter=False)
    res, after = {}, small_flight.token
    for index in range(len(comm.sent)):
        for n, parts in comm.received(index, after).items():
            k, layer = where[n]
            res[k] = _adamw(f"adam_{n}", parts, w[k], mom[k], var[k], layer, res.get(k))
            after = res[k][0]
    as_rows = lambda tree: {k: tree[k] if tree[k].ndim == 2 else tree[k].reshape(1, -1) for k in SMALL}
    small, loss_row = _adamw_small("adam_small", *_exchange_wait("gather_small_wait", small_flight, [0, 1], after),
                                   as_rows(w), as_rows(mom), as_rows(var))
    loss = loss_row[0, 0]
    for k in SMALL:
        res[k] = [a.reshape(w[k].shape) for a in small[k]]

    outs = [loss, grad_x[None]]
    for i in range(4):
        outs.extend(res[k][i] for k in WEIGHTS)
    return tuple(outs)
```

```python
import numpy as np
import jax
import jax.numpy as jnp
from jax import lax
from jax.experimental import pallas as pl
from jax.experimental.pallas import tpu as pltpu

F32 = jnp.float32
BF16 = jnp.bfloat16

N_DEV = 8
HEAD_DIM = 64
WINDOW = 128
N_BUCKETS = 32
REL_MAX_DIST = 128
NORM_EPS = 1e-6
NEG = -1e30
LANES = 128
VMEM_LIMIT = 56 * 1024 * 1024

ADAM_LR = 0.001
ADAM_B1 = 0.9
ADAM_B2 = 0.999
ADAM_EPS = 1e-08
ADAM_WD = 0.01
ADAM_STEP = 10

SMALL = ("g_attn", "g_mlp", "b_f", "gq_a", "gk_a", "g_kv", "gk_b", "gq_b", "sinks", "rel_bias")
WEIGHTS = ("g_attn", "g_mlp", "w_in_a", "b_f", "gq_a", "gk_a", "w_out_a", "g_kv", "w_kv", "gk_b",
           "w_q_b", "gq_b", "sinks", "rel_bias", "w_out_b", "w_up", "w_down")


def _params(n_grid):
    return pltpu.CompilerParams(dimension_semantics=("arbitrary",) * n_grid, vmem_limit_bytes=VMEM_LIMIT)


def _sds(shape, dtype):
    return jax.ShapeDtypeStruct(tuple(shape), dtype)


def _rms(x, g):
    return (x * lax.rsqrt(jnp.mean(x * x, axis=-1, keepdims=True) + NORM_EPS)) * g


def _dot_nt(a, b):
    return lax.dot_general(a, b, (((1,), (1,)), ((), ())), preferred_element_type=F32)


def _dot_tn(a, b):
    return lax.dot_general(a, b, (((0,), (0,)), ((), ())), preferred_element_type=F32)


def _dot(a, b):
    return jnp.dot(a, b, preferred_element_type=F32)


def _dot_exact(a, b):
    return jnp.dot(a, b, preferred_element_type=F32, precision=lax.Precision.HIGHEST)


def _norm_matmul(name, x, g, w, *, tn=None, relu2=False):
    t, d = x.shape
    blocked = w.ndim == 3
    if blocked:
        tn = w.shape[2]
        n = w.shape[0] * tn
        w_spec = pl.BlockSpec((None, d, tn), lambda i, j: (j, 0, 0))
    else:
        n = w.shape[1]
        w_spec = pl.BlockSpec((d, tn), lambda i, j: (0, j))
    tm = min(1024, t)

    def body(x_ref, g_ref, w_ref, y_ref, xn_ref):
        @pl.when(pl.program_id(1) == 0)
        def _():
            xn_ref[...] = _rms(x_ref[...], g_ref[...]).astype(BF16)

        y = _dot(xn_ref[...], w_ref[...])
        y_ref[...] = jnp.square(jnp.maximum(y, 0.0)).astype(BF16) if relu2 else y

    out_shape = [_sds((t, n), BF16 if relu2 else F32), _sds((t, d), BF16)]
    out_specs = [pl.BlockSpec((tm, tn), lambda i, j: (i, j)), pl.BlockSpec((tm, d), lambda i, j: (i, 0))]
    return pl.pallas_call(
        body, name=name, grid=(t // tm, n // tn),
        in_specs=[pl.BlockSpec((tm, d), lambda i, j: (i, 0)), pl.BlockSpec((1, d), lambda i, j: (0, 0)), w_spec],
        out_specs=out_specs, out_shape=out_shape, compiler_params=_params(2),
    )(x, g.reshape(1, d), w)


def _matmul_res(name, a, w, res, *, tn=512, after=None):
    t, k = a.shape
    n = w.shape[1]
    tm = min(1024, t)

    def body(a_ref, w_ref, r_ref, *rest):
        rest[-1][...] = r_ref[...] + _dot(a_ref[...], w_ref[...])

    extra = [] if after is None else [after]
    return pl.pallas_call(
        body, name=name, grid=(t // tm, n // tn),
        in_specs=[pl.BlockSpec((tm, k), lambda i, j: (i, 0)), pl.BlockSpec((k, tn), lambda i, j: (0, j)),
                  pl.BlockSpec((tm, tn), lambda i, j: (i, j))] + [pl.BlockSpec((8, LANES), lambda i, j: (0, 0))] * len(extra),
        out_specs=pl.BlockSpec((tm, tn), lambda i, j: (i, j)), out_shape=_sds((t, n), F32),
        compiler_params=_params(2),
    )(a, w, res, *extra)


def _matmul_nt(name, dy, w, *, a=None, tk=1024):
    t, n = dy.shape
    k = w.shape[0]
    tm = min(1024, t)

    def body(dy_ref, w_ref, *rest):
        o_ref = rest[-1]
        r = _dot_nt(dy_ref[...].astype(BF16), w_ref[...])
        if a is not None:
            r = r * (2.0 * jnp.sqrt(rest[0][...].astype(F32)))
        o_ref[...] = r.astype(BF16)

    in_specs = [pl.BlockSpec((tm, n), lambda i, j: (i, 0)), pl.BlockSpec((tk, n), lambda i, j: (j, 0))]
    args = [dy, w]
    if a is not None:
        in_specs.append(pl.BlockSpec((tm, tk), lambda i, j: (i, j)))
        args.append(a)
    return pl.pallas_call(
        body, name=name, grid=(t // tm, k // tk), in_specs=in_specs,
        out_specs=pl.BlockSpec((tm, tk), lambda i, j: (i, j)), out_shape=_sds((t, k), BF16),
        compiler_params=_params(2),
    )(*args)


def _matmul_nt_rmsbwd(name, dy, w, x, g, dres):
    t, k = dy.shape
    blocked = w.ndim == 3
    d = w.shape[1] if blocked else w.shape[0]
    tm = min(512, t)

    def body(dy_ref, w_ref, x_ref, g_ref, r_ref, dx_ref, dg_ref):
        if blocked:
            kb = w.shape[2]
            dxn = _dot_nt(dy_ref[:, 0:kb].astype(BF16), w_ref[0])
            for j in range(1, w.shape[0]):
                dxn += _dot_nt(dy_ref[:, j * kb:(j + 1) * kb].astype(BF16), w_ref[j])
        else:
            dxn = _dot_nt(dy_ref[...].astype(BF16), w_ref[...])
        _, vjp = jax.vjp(_rms, x_ref[...], g_ref[...])
        dx, dg = vjp(dxn)
        dx_ref[...] = r_ref[...] + dx

        @pl.when(pl.program_id(0) == 0)
        def _():
            dg_ref[...] = jnp.zeros_like(dg_ref)

        dg_ref[...] += dg

    row = lambda i: (i, 0)
    fixed = lambda i: (0, 0)
    return pl.pallas_call(
        body, name=name, grid=(t // tm,),
        in_specs=[pl.BlockSpec((tm, k), row), pl.BlockSpec(w.shape, (lambda i: (0, 0, 0)) if blocked else fixed),
                  pl.BlockSpec((tm, d), row), pl.BlockSpec((1, d), fixed), pl.BlockSpec((tm, d), row)],
        out_specs=[pl.BlockSpec((tm, d), row), pl.BlockSpec((1, d), fixed)],
        out_shape=[_sds((t, d), F32), _sds((1, d), F32)], compiler_params=_params(1),
    )(dy, w, x, g.reshape(1, d), dres)


def _matmul_tn(name, a, b, *, tk=1024, tn=1024, col_blocks=None):
    t, k = a.shape
    n = b.shape[1]
    tk = min(tk, k)
    if col_blocks:
        tn = n // col_blocks
        out_spec, out_shape = pl.BlockSpec((None, tk, tn), lambda i, j: (j, i, 0)), _sds((col_blocks, k, tn), BF16)
    else:
        tn = min(tn, n)
        out_spec, out_shape = pl.BlockSpec((tk, tn), lambda i, j: (i, j)), _sds((k, n), BF16)

    def body(a_ref, b_ref, o_ref):
        o_ref[...] = _dot_tn(a_ref[...].astype(BF16), b_ref[...].astype(BF16)).astype(BF16)

    return pl.pallas_call(
        body, name=name, grid=(k // tk, n // tn),
        in_specs=[pl.BlockSpec((t, tk), lambda i, j: (0, i)), pl.BlockSpec((t, tn), lambda i, j: (0, j))],
        out_specs=out_spec, out_shape=out_shape, compiler_params=_params(2),
    )(a, b)


def _join_col_blocks(name, blocks, width):
    b, r, c = blocks.shape
    tr = min(256, r)

    def body(g_ref, o_ref):
        o_ref[...] = jnp.zeros_like(o_ref)
        for j in range(b):
            o_ref[:, c * j:c * (j + 1)] = g_ref[j]

    return pl.pallas_call(
        body, name=name, grid=(r // tr,), in_specs=[pl.BlockSpec((b, tr, c), lambda i: (0, i, 0))],
        out_specs=pl.BlockSpec((tr, width), lambda i: (i, 0)), out_shape=_sds((r, width), blocks.dtype),
        compiler_params=_params(1),
    )(blocks)


def _split_col_blocks(name, mat, b, c):
    r, width = mat.shape
    tr = min(256, r)

    def body(w_ref, o_ref):
        for j in range(b):
            o_ref[j] = w_ref[:, c * j:c * (j + 1)]

    return pl.pallas_call(
        body, name=name, grid=(r // tr,), in_specs=[pl.BlockSpec((tr, width), lambda i: (i, 0))],
        out_specs=pl.BlockSpec((b, tr, c), lambda i: (0, i, 0)), out_shape=_sds((b, r, c), mat.dtype),
        compiler_params=_params(1),
    )(mat)


def _matmul_res_loss(name, a, w, res, target, *, tn=512):
    t, k = a.shape
    n = w.shape[1]
    tm = min(1024, t)

    def body(a_ref, w_ref, r_ref, t_ref, dy_ref, l_ref):
        e = r_ref[...] + _dot(a_ref[...], w_ref[...]) - t_ref[...]
        dy_ref[...] = e * (1.0 / n)

        @pl.when((pl.program_id(0) == 0) & (pl.program_id(1) == 0))
        def _():
            l_ref[...] = jnp.zeros_like(l_ref)

        l_ref[...] += (0.5 / n) * jnp.sum(e * e)

    tile = pl.BlockSpec((tm, tn), lambda i, j: (i, j))
    return pl.pallas_call(
        body, name=name, grid=(t // tm, n // tn),
        in_specs=[pl.BlockSpec((tm, k), lambda i, j: (i, 0)), pl.BlockSpec((k, tn), lambda i, j: (0, j)), tile, tile],
        out_specs=[tile, pl.BlockSpec((8, LANES), lambda i, j: (0, 0))],
        out_shape=[_sds((t, n), F32), _sds((8, LANES), F32)], compiler_params=_params(2),
    )(a, w, res, target)


def _gate_fwd(name, proj, b_pad, n_heads, gate_col):
    t = proj.shape[0]
    tb = min(256, t)
    tri = jnp.asarray(np.tril(np.ones((tb, tb), np.float32)))

    def body(p_ref, b_ref, tri_ref, c_ref, carry):
        @pl.when(pl.program_id(0) == 0)
        def _():
            carry[...] = jnp.zeros_like(carry)

        lane = lax.broadcasted_iota(jnp.int32, (tb, LANES), 1)
        lf = jnp.where(lane < n_heads, jax.nn.log_sigmoid(p_ref[...] + b_ref[...]), 0.0)
        c = _dot_exact(tri_ref[...], lf) + carry[0:1, :]
        c_ref[...] = c
        carry[...] = jnp.broadcast_to(c[tb - 1:tb, :], carry.shape)

    return pl.pallas_call(
        body, name=name, grid=(t // tb,),
        in_specs=[pl.BlockSpec((tb, LANES), lambda i: (i, gate_col)), pl.BlockSpec((1, LANES), lambda i: (0, 0)),
                  pl.BlockSpec((tb, tb), lambda i: (0, 0))],
        out_specs=pl.BlockSpec((tb, LANES), lambda i: (i, 0)), out_shape=_sds((t, LANES), F32),
        scratch_shapes=[pltpu.VMEM((8, LANES), F32)], compiler_params=_params(1),
    )(proj, b_pad, tri)


def _gate_bwd(name, proj, b_pad, dc, n_heads, gate_col):
    t = proj.shape[0]
    tb = min(256, t)
    nb = t // tb
    triu = jnp.asarray(np.triu(np.ones((tb, tb), np.float32)))

    def body(p_ref, b_ref, dc_ref, tri_ref, df_ref, db_ref, carry):
        @pl.when(pl.program_id(0) == 0)
        def _():
            carry[...] = jnp.zeros_like(carry)
            db_ref[...] = jnp.zeros_like(db_ref)

        dcv = dc_ref[...]
        dlf = _dot_exact(tri_ref[...], dcv) + carry[0:1, :]
        carry[...] = jnp.broadcast_to(dlf[0:1, :], carry.shape)
        lane = lax.broadcasted_iota(jnp.int32, (tb, LANES), 1)
        z = p_ref[...] + b_ref[...]
        df = jnp.where(lane < n_heads, dlf / (1.0 + jnp.exp(z)), 0.0)
        df_ref[...] = df.astype(BF16)
        db_ref[...] += jnp.sum(df, axis=0, keepdims=True)

    return pl.pallas_call(
        body, name=name, grid=(nb,),
        in_specs=[pl.BlockSpec((tb, LANES), lambda i: (nb - 1 - i, gate_col)), pl.BlockSpec((1, LANES), lambda i: (0, 0)),
                  pl.BlockSpec((tb, LANES), lambda i: (nb - 1 - i, 0)), pl.BlockSpec((tb, tb), lambda i: (0, 0))],
        out_specs=[pl.BlockSpec((tb, LANES), lambda i: (nb - 1 - i, 0)), pl.BlockSpec((1, LANES), lambda i: (0, 0))],
        out_shape=[_sds((t, LANES), BF16), _sds((1, LANES), F32)],
        scratch_shapes=[pltpu.VMEM((8, LANES), F32)], compiler_params=_params(1),
    )(proj, b_pad, dc, triu)


def _qhead(qp, g):
    return _rms(qp, g) * (HEAD_DIM ** -0.5)


def _column(mat, idx):
    lane = lax.broadcasted_iota(jnp.int32, mat.shape, 1)
    return jnp.sum(jnp.where(lane == idx, mat, 0.0), axis=1, keepdims=True)


def _fox_scores(kk, qi, ckey, cq_i, i, bq):
    length = kk.shape[0]
    s = _dot_nt(kk, qi) + cq_i - ckey[:length]
    key = lax.broadcasted_iota(jnp.int32, (length, bq), 0)
    qry = lax.broadcasted_iota(jnp.int32, (length, bq), 1) + i * bq
    return jnp.where(key <= qry, s, NEG)


def _fox_fwd(name, proj, c, crow, gq, gk, n_heads):
    t = proj.shape[0]
    hw = n_heads * HEAD_DIM
    npair = n_heads // 2
    bq = min(256, t)
    nq = t // bq

    def body(q_ref, k_ref, v_ref, c_ref, crow_ref, gq_ref, gk_ref, o_ref, lse_ref):
        hp = pl.program_id(0)
        lse_ref[...] = jnp.zeros_like(lse_ref)
        outs = []
        for hh in range(2):
            sl = slice(hh * HEAD_DIM, (hh + 1) * HEAD_DIM)
            qn = _qhead(q_ref[:, sl], gq_ref[...]).astype(BF16)
            kn = _rms(k_ref[:, sl], gk_ref[...]).astype(BF16)
            v_t = v_ref[:, sl].T.astype(BF16)
            ckey = _column(c_ref[...], 2 * hp + hh)
            cq = crow_ref[0, hh:hh + 1, :]
            o_blocks = []
            for i in range(nq):
                cols = slice(i * bq, (i + 1) * bq)
                length = (i + 1) * bq
                s = _fox_scores(kn[:length], qn[cols], ckey, cq[:, cols], i, bq)
                m = jnp.max(s, axis=0, keepdims=True)
                p = jnp.exp(s - m)
                l = jnp.sum(p, axis=0, keepdims=True)
                o_blocks.append((_dot(v_t[:, :length], p.astype(BF16)) / l).T)
                lse_ref[0, hh:hh + 1, cols] = m + jnp.log(l)
            outs.append(jnp.concatenate(o_blocks, axis=0))
        o_ref[...] = jnp.concatenate(outs, axis=1).astype(BF16)

    col = lambda off: (lambda h: (0, off + h))
    fixed = lambda h: (0, 0)
    return pl.pallas_call(
        body, name=name, grid=(npair,),
        in_specs=[pl.BlockSpec((t, LANES), col(0)), pl.BlockSpec((t, LANES), col(npair)), pl.BlockSpec((t, LANES), col(2 * npair)),
                  pl.BlockSpec((t, LANES), fixed), pl.BlockSpec((1, 2, t), lambda h: (h, 0, 0)),
                  pl.BlockSpec((1, HEAD_DIM), fixed), pl.BlockSpec((1, HEAD_DIM), fixed)],
        out_specs=[pl.BlockSpec((t, LANES), col(0)), pl.BlockSpec((1, 8, t), lambda h: (h, 0, 0))],
        out_shape=[_sds((t, hw), BF16), _sds((npair, 8, t), F32)], compiler_params=_params(1),
    )(proj, proj, proj, c, crow, gq, gk)


def _fox_bwd(name, proj, c, crow, gq, gk, lse, do, n_heads):
    t = proj.shape[0]
    hw = n_heads * HEAD_DIM
    npair = n_heads // 2
    bq = min(256, t)
    nq = t // bq

    def body(q_ref, k_ref, v_ref, c_ref, crow_ref, gq_ref, gk_ref, lse_ref, do_ref,
             dq_ref, dk_ref, dv_ref, dc_ref, dgq_ref, dgk_ref, dk_acc, dv_acc, dc_acc):
        hp = pl.program_id(0)

        @pl.when(hp == 0)
        def _():
            dgq_ref[...] = jnp.zeros_like(dgq_ref)
            dgk_ref[...] = jnp.zeros_like(dgk_ref)

        lane = lax.broadcasted_iota(jnp.int32, (t, LANES), 1)
        dc_pair = jnp.zeros((t, LANES), F32)
        dqs, dks, dvs = [], [], []
        for hh in range(2):
            sl = slice(hh * HEAD_DIM, (hh + 1) * HEAD_DIM)
            qf, q_vjp = jax.vjp(_qhead, q_ref[:, sl], gq_ref[...])
            kf, k_vjp = jax.vjp(_rms, k_ref[:, sl], gk_ref[...])
            qn, kn, kn_t = qf.astype(BF16), kf.astype(BF16), kf.T.astype(BF16)
            vb = v_ref[:, sl].astype(BF16)
            dob = do_ref[:, sl]
            ckey = _column(c_ref[...], 2 * hp + hh)
            cq = crow_ref[0, hh:hh + 1, :]
            lse_h = lse_ref[0, hh:hh + 1, :]
            dk_acc[...] = jnp.zeros_like(dk_acc)
            dv_acc[...] = jnp.zeros_like(dv_acc)
            dc_acc[...] = jnp.zeros_like(dc_acc)
            dq_blocks = []
            for i in range(nq):
                cols = slice(i * bq, (i + 1) * bq)
                length = (i + 1) * bq
                qi, doi = qn[cols], dob[cols]
                s = _fox_scores(kn[:length], qi, ckey, cq[:, cols], i, bq)
                p = jnp.exp(s - lse_h[:, cols])
                dp = _dot_nt(vb[:length], doi)
                ds = p * (dp - jnp.sum(p * dp, axis=0, keepdims=True))
                dsb = ds.astype(BF16)
                dq_blocks.append(_dot(kn_t[:, :length], dsb).T)
                dk_acc[0:length, :] += _dot(dsb, qi)
                dv_acc[0:length, :] += _dot(p.astype(BF16), doi)
                part = ds[:, 0:LANES]
                for j in range(1, bq // LANES):
                    part = part + ds[:, j * LANES:(j + 1) * LANES]
                dc_acc[0:length, :] += part
            dqp, dgq = q_vjp(jnp.concatenate(dq_blocks, axis=0))
            dkp, dgk = k_vjp(dk_acc[...])
            dgq_ref[...] += dgq
            dgk_ref[...] += dgk
            dqs.append(dqp)
            dks.append(dkp)
            dvs.append(dv_acc[...])
            dc_pair = jnp.where(lane == hh, -jnp.sum(dc_acc[...], axis=1, keepdims=True), dc_pair)
        dq_ref[...] = jnp.concatenate(dqs, axis=1).astype(BF16)
        dk_ref[...] = jnp.concatenate(dks, axis=1).astype(BF16)
        dv_ref[...] = jnp.concatenate(dvs, axis=1).astype(BF16)
        dc_ref[...] = dc_pair

    col = lambda off: (lambda h: (0, off + h))
    fixed = lambda h: (0, 0)
    pair_blk = pl.BlockSpec((t, LANES), col(0))
    return pl.pallas_call(
        body, name=name, grid=(npair,),
        in_specs=[pl.BlockSpec((t, LANES), col(0)), pl.BlockSpec((t, LANES), col(npair)), pl.BlockSpec((t, LANES), col(2 * npair)),
                  pl.BlockSpec((t, LANES), fixed), pl.BlockSpec((1, 2, t), lambda h: (h, 0, 0)),
                  pl.BlockSpec((1, HEAD_DIM), fixed), pl.BlockSpec((1, HEAD_DIM), fixed),
                  pl.BlockSpec((1, 8, t), lambda h: (h, 0, 0)), pair_blk],
        out_specs=[pair_blk, pair_blk, pair_blk, pair_blk,
                   pl.BlockSpec((1, HEAD_DIM), fixed), pl.BlockSpec((1, HEAD_DIM), fixed)],
        out_shape=[_sds((t, hw), BF16), _sds((t, hw), BF16), _sds((t, hw), BF16), _sds((t, npair * LANES), F32),
                   _sds((1, HEAD_DIM), F32), _sds((1, HEAD_DIM), F32)],
        scratch_shapes=[pltpu.VMEM((t, HEAD_DIM), F32), pltpu.VMEM((t, HEAD_DIM), F32), pltpu.VMEM((t, LANES), F32)],
        compiler_params=_params(1),
    )(proj, proj, proj, c, crow, gq, gk, lse, do)


def _t5_bucket_table():
    dist = np.arange(WINDOW)[None, :] + WINDOW - np.arange(2 * WINDOW)[:, None]
    n = np.maximum(dist, 0)
    max_exact = N_BUCKETS // 2
    large = max_exact + (np.log(np.maximum(n, 1) / max_exact) / np.log(REL_MAX_DIST / max_exact)
                         * (N_BUCKETS - max_exact)).astype(np.int32)
    large = np.minimum(large, N_BUCKETS - 1)
    return np.where(n < max_exact, n, large).astype(np.int32).reshape(1, -1)


def _bias_expand(name, rel_bias_t):
    n_heads = rel_bias_t.shape[0]
    tbl = jnp.asarray(_t5_bucket_table())
    width = tbl.shape[1]

    def body(rb_ref, tbl_ref, o_ref):
        onehot = (lax.broadcasted_iota(jnp.int32, (N_BUCKETS, width), 0) == tbl_ref[...]).astype(F32)
        o_ref[...] = _dot_exact(rb_ref[...], onehot)

    return pl.pallas_call(body, name=name, out_shape=_sds((n_heads, width), F32), compiler_params=_params(0))(rel_bias_t, tbl)


def _bias_reduce(name, dbias):
    n_heads, width = dbias.shape
    tbl = jnp.asarray(_t5_bucket_table())

    def body(db_ref, tbl_ref, o_ref):
        onehot = (lax.broadcasted_iota(jnp.int32, (N_BUCKETS, width), 0) == tbl_ref[...]).astype(F32)
        o_ref[...] = lax.dot_general(db_ref[...], onehot, (((1,), (1,)), ((), ())), preferred_element_type=F32,
                                     precision=lax.Precision.HIGHEST)

    return pl.pallas_call(body, name=name, out_shape=_sds((n_heads, N_BUCKETS), F32), compiler_params=_params(0))(dbias, tbl)


def _swa_mask(n, group):
    j = lax.broadcasted_iota(jnp.int32, (2 * WINDOW, group * WINDOW), 0)
    i = lax.broadcasted_iota(jnp.int32, (2 * WINDOW, group * WINDOW), 1) & (WINDOW - 1)
    ok = (j > i) & (j <= i + WINDOW) & ((n > 0) | (j >= WINDOW))
    return jnp.where(ok, 0.0, NEG)


def _swa_stack(ref, start, group):
    return jnp.concatenate([ref[pl.ds(start, WINDOW), g * HEAD_DIM:(g + 1) * HEAD_DIM] for g in range(group)], axis=0)


def _swa_fwd(name, qb, kh, vh, gq, gk, sinks, bias, group):
    t = qb.shape[0]
    kvh = kh.shape[0]
    nblk = t // WINDOW
    gw = group * HEAD_DIM
    band = 2 * WINDOW
    cols = group * WINDOW

    def body(q_ref, k_ref, v_ref, gq_ref, gk_ref, sink_ref, bias_ref, o_ref, lse_ref, qs, kpad, vpad):
        for g in range(group):
            qs[:, g * HEAD_DIM:(g + 1) * HEAD_DIM] = _qhead(q_ref[:, g * HEAD_DIM:(g + 1) * HEAD_DIM], gq_ref[...]).astype(BF16)
        kpad[0:WINDOW, :] = jnp.zeros((WINDOW, HEAD_DIM), BF16)
        vpad[0:WINDOW, :] = jnp.zeros((WINDOW, HEAD_DIM), BF16)
        kpad[WINDOW:, :] = _rms(k_ref[0], gk_ref[...]).astype(BF16)
        vpad[WINDOW:, :] = v_ref[0].astype(BF16)
        sink = sink_ref[0]

        def block(n, carry):
            start = pl.multiple_of(n * WINDOW, WINDOW)
            kb = kpad[pl.ds(start, band), :]
            vb = vpad[pl.ds(start, band), :]
            s = _dot_nt(kb, _swa_stack(qs, start, group)) + bias_ref[0] + _swa_mask(n, group)
            m = jnp.maximum(jnp.max(s, axis=0, keepdims=True), sink)
            e = jnp.exp(s - m)
            l = jnp.sum(e, axis=0, keepdims=True) + jnp.exp(sink - m)
            o_t = _dot_tn(vb, e.astype(BF16)) / l
            for g in range(group):
                o_ref[pl.ds(start, WINDOW), g * HEAD_DIM:(g + 1) * HEAD_DIM] = o_t[:, g * WINDOW:(g + 1) * WINDOW].T.astype(BF16)
            lse_ref[pl.ds(n, 1), :] = m + jnp.log(l)
            return carry

        lax.fori_loop(0, nblk, block, 0)

    fixed = lambda h: (0, 0)
    per = lambda h: (h, 0, 0)
    return pl.pallas_call(
        body, name=name, grid=(kvh,),
        in_specs=[pl.BlockSpec((t, gw), lambda h: (0, h)), pl.BlockSpec((1, t, HEAD_DIM), per), pl.BlockSpec((1, t, HEAD_DIM), per),
                  pl.BlockSpec((1, HEAD_DIM), fixed), pl.BlockSpec((1, HEAD_DIM), fixed),
                  pl.BlockSpec((1, 1, cols), per), pl.BlockSpec((1, band, cols), per)],
        out_specs=[pl.BlockSpec((t, gw), lambda h: (0, h)), pl.BlockSpec((nblk, cols), lambda h: (h, 0))],
        out_shape=[_sds((t, kvh * gw), BF16), _sds((kvh * nblk, cols), F32)],
        scratch_shapes=[pltpu.VMEM((t, gw), BF16), pltpu.VMEM((t + WINDOW, HEAD_DIM), BF16),
                        pltpu.VMEM((t + WINDOW, HEAD_DIM), BF16)],
        compiler_params=_params(1),
    )(qb, kh, vh, gq, gk, sinks, bias)


def _swa_bwd(name, qb, kh, vh, gq, gk, sinks, bias, lse, do, group):
    t = qb.shape[0]
    kvh = kh.shape[0]
    nblk = t // WINDOW
    gw = group * HEAD_DIM
    band = 2 * WINDOW
    cols = group * WINDOW

    def body(q_ref, k_ref, v_ref, gq_ref, gk_ref, sink_ref, bias_ref, lse_ref, do_ref,
             dq_ref, dk_ref, dv_ref, dgq_ref, dgk_ref, dsink_ref, dbias_ref,
             qs, kpad, vpad, dqs, dk_acc, dv_acc, dsink_acc):
        @pl.when(pl.program_id(0) == 0)
        def _():
            dgq_ref[...] = jnp.zeros_like(dgq_ref)
            dgk_ref[...] = jnp.zeros_like(dgk_ref)

        for g in range(group):
            qs[:, g * HEAD_DIM:(g + 1) * HEAD_DIM] = _qhead(q_ref[:, g * HEAD_DIM:(g + 1) * HEAD_DIM], gq_ref[...]).astype(BF16)
        kpad[0:WINDOW, :] = jnp.zeros((WINDOW, HEAD_DIM), BF16)
        vpad[0:WINDOW, :] = jnp.zeros((WINDOW, HEAD_DIM), BF16)
        kpad[WINDOW:, :] = _rms(k_ref[0], gk_ref[...]).astype(BF16)
        vpad[WINDOW:, :] = v_ref[0].astype(BF16)
        dk_acc[...] = jnp.zeros_like(dk_acc)
        dv_acc[...] = jnp.zeros_like(dv_acc)
        dsink_acc[...] = jnp.zeros_like(dsink_acc)
        dbias_ref[...] = jnp.zeros_like(dbias_ref)
        sink = sink_ref[0]

        def block(n, carry):
            start = pl.multiple_of(n * WINDOW, WINDOW)
            kb = kpad[pl.ds(start, band), :]
            vb = vpad[pl.ds(start, band), :]
            q = _swa_stack(qs, start, group)
            dob = _swa_stack(do_ref, start, group)
            lse_n = lse_ref[pl.ds(n, 1), :]
            s = _dot_nt(kb, q) + bias_ref[0] + _swa_mask(n, group)
            p = jnp.exp(s - lse_n)
            dp = _dot_nt(vb, dob)
            dsum = jnp.sum(p * dp, axis=0, keepdims=True)
            ds = p * (dp - dsum)
            dsb = ds.astype(BF16)
            dsink_acc[...] -= jnp.exp(sink - lse_n) * dsum
            dbias_ref[0] += ds
            dq = _dot_tn(dsb, kb)
            for g in range(group):
                dqs[pl.ds(start, WINDOW), g * HEAD_DIM:(g + 1) * HEAD_DIM] = dq[g * WINDOW:(g + 1) * WINDOW]
            dk_acc[pl.ds(start, band), :] += _dot(dsb, q)
            dv_acc[pl.ds(start, band), :] += _dot(p.astype(BF16), dob)
            return carry

        lax.fori_loop(0, nblk, block, 0)
        for g in range(group):
            _, q_vjp = jax.vjp(_qhead, q_ref[:, g * HEAD_DIM:(g + 1) * HEAD_DIM], gq_ref[...])
            dqp, dgq = q_vjp(dqs[:, g * HEAD_DIM:(g + 1) * HEAD_DIM])
            dq_ref[:, g * HEAD_DIM:(g + 1) * HEAD_DIM] = dqp.astype(BF16)
            dgq_ref[...] += dgq
            dsink_g = jnp.sum(dsink_acc[:, g * WINDOW:(g + 1) * WINDOW], axis=1, keepdims=True)
            dsink_ref[0, g:g + 1, :] = jnp.broadcast_to(dsink_g, (1, LANES))
        _, k_vjp = jax.vjp(_rms, k_ref[0], gk_ref[...])
        dkp, dgk = k_vjp(dk_acc[WINDOW:, :])
        dk_ref[0] = dkp
        dgk_ref[...] += dgk
        dv_ref[0] = dv_acc[WINDOW:, :]

    fixed = lambda h: (0, 0)
    per = lambda h: (h, 0, 0)
    wide = pl.BlockSpec((t, gw), lambda h: (0, h))
    head = pl.BlockSpec((1, t, HEAD_DIM), per)
    vec = pl.BlockSpec((1, HEAD_DIM), fixed)
    bias_spec = pl.BlockSpec((1, band, cols), per)
    return pl.pallas_call(
        body, name=name, grid=(kvh,),
        in_specs=[wide, head, head, vec, vec, pl.BlockSpec((1, 1, cols), per), bias_spec,
                  pl.BlockSpec((nblk, cols), lambda h: (h, 0)), wide],
        out_specs=[wide, head, head, vec, vec, pl.BlockSpec((1, group, LANES), per), bias_spec],
        out_shape=[_sds((t, kvh * gw), BF16), _sds((kvh, t, HEAD_DIM), F32), _sds((kvh, t, HEAD_DIM), F32),
                   _sds((1, HEAD_DIM), F32), _sds((1, HEAD_DIM), F32),
                   _sds((kvh, group, LANES), F32), _sds((kvh, band, cols), F32)],
        scratch_shapes=[pltpu.VMEM((t, gw), BF16), pltpu.VMEM((t + WINDOW, HEAD_DIM), BF16),
                        pltpu.VMEM((t + WINDOW, HEAD_DIM), BF16), pltpu.VMEM((t, gw), F32),
                        pltpu.VMEM((t + WINDOW, HEAD_DIM), F32), pltpu.VMEM((t + WINDOW, HEAD_DIM), F32),
                        pltpu.VMEM((1, cols), F32)],
        compiler_params=_params(1),
    )(qb, kh, vh, gq, gk, sinks, bias, lse, do)


def _local_step(x, target, p, comm):
    t, d = x.shape
    n_heads = d // HEAD_DIM
    kv_heads = n_heads // 8
    group = n_heads // kv_heads
    hw = n_heads * HEAD_DIM
    gate_col = 3 * hw // LANES
    kvw = kv_heads * HEAD_DIM
    grads = {}

    def mlp_fwd(tag, h, g, layer, last=False):
        w_up, = comm.weights([f"w_up{layer}"], h)
        a, hn = _norm_matmul(f"{tag}_up", h, g, w_up, relu2=True)
        w_down, = comm.weights([f"w_down{layer}"], a)
        out = _matmul_res_loss(f"{tag}_down", a, w_down, h, target) if last else _matmul_res(f"{tag}_down", a, w_down, h)
        return out, (h, g, hn, a, w_up, w_down)

    def mlp_bwd(tag, saved, layer, dy):
        h, g, hn, a, w_up, w_down = saved
        du = _matmul_nt(f"{tag}_du", dy, w_down, a=a)
        dw_down = _matmul_tn(f"{tag}_dwdown", a, dy)
        dw_up = _matmul_tn(f"{tag}_dwup", hn, du, col_blocks=w_up.shape[0])
        zero = comm.send_grads(tag, {f"w_down{layer}": dw_down, f"w_up{layer}": dw_up})
        return _matmul_nt_rmsbwd(f"{tag}_dh", du, w_up, h, g + zero, dy)

    comm.prefetch(["w_in_a"], None)
    w_in, = comm.weights(["w_in_a"], None)
    proj, xn1 = _norm_matmul("a_inproj", x, p["g_attn"][0], w_in, tn=640)
    ahead = comm.prefetch(["w_out_a"], proj)
    b_pad = jnp.pad(p["b_f"], ((0, 0), (0, LANES - n_heads))) + ahead[0:1, :]
    c = _gate_fwd("a_gate", proj, b_pad, n_heads, gate_col)
    crow = c[:, :n_heads].T.reshape(n_heads // 2, 2, t)
    o_a, lse_a = _fox_fwd("a_attn", proj, c, crow, p["gq_a"], p["gk_a"], n_heads)
    ahead = comm.prefetch(["w_up0", "w_down0", "w_kv", "w_q_b", "w_out_b"], o_a)
    w_out_a, = comm.weights(["w_out_a"], o_a)
    h1 = _matmul_res("a_outproj", o_a, w_out_a, x, after=ahead)
    h2, mlp0 = mlp_fwd("mlp0", h1, p["g_mlp"][0], 0)

    ahead = comm.prefetch(["w_up1", "w_down1"], h2)
    w_kv, w_q_b = comm.weights(["w_kv", "w_q_b"], h2)
    kv, hn_kv = _norm_matmul("kv_proj", h2, p["g_kv"] + ahead[0, 0], w_kv, tn=2 * kvw)
    kh = kv[:, :kvw].reshape(t, kv_heads, HEAD_DIM).transpose(1, 0, 2)
    vh = kv[:, kvw:].reshape(t, kv_heads, HEAD_DIM).transpose(1, 0, 2)
    qb, hn_q = _norm_matmul("b_qproj", h2, p["g_attn"][1], w_q_b, tn=512)
    gqb, gkb = p["gq_b"], p["gk_b"].reshape(1, HEAD_DIM)
    bias = _bias_expand("b_bias", p["rel_bias"].T).reshape(kv_heads, group, 2 * WINDOW, WINDOW)
    bias = bias.transpose(0, 2, 1, 3).reshape(kv_heads, 2 * WINDOW, group * WINDOW)
    sink_rows = jnp.broadcast_to(p["sinks"].reshape(kv_heads, 1, group, 1), (kv_heads, 1, group, WINDOW)).reshape(kv_heads, 1, group * WINDOW)
    o_b, lse_b = _swa_fwd("b_attn", qb, kh, vh, gqb, gkb, sink_rows, bias, group)
    w_out_b, = comm.weights(["w_out_b"], o_b)
    h3 = _matmul_res("b_outproj", o_b, w_out_b, h2)
    (dy, loss_tile), mlp1 = mlp_fwd("mlp1", h3, p["g_mlp"][1], 1, last=True)

    dh3, dg_mlp1 = mlp_bwd("mlp1", mlp1, 1, dy)
    do_b = _matmul_nt("b_do", dh3, w_out_b)
    dw_out_b = _matmul_tn("b_dwout", o_b, dh3)
    dqb, dkh, dvh, grads["gq_b"], dgk_b, dsink, dbias = _swa_bwd(
        "b_attn_bwd", qb, kh, vh, gqb, gkb, sink_rows, bias, lse_b, do_b, group)
    grads["gk_b"] = dgk_b
    grads["sinks"] = dsink[:, :, 0].reshape(1, n_heads)
    dbias = dbias.reshape(kv_heads, 2 * WINDOW, group, WINDOW).transpose(0, 2, 1, 3)
    grads["rel_bias"] = _bias_reduce("b_dbias", dbias.reshape(n_heads, WINDOW * 2 * WINDOW)).T
    dw_q_b = _matmul_tn("b_dwq", hn_q, dqb)
    dh2, dg_attn1 = _matmul_nt_rmsbwd("b_dhq", dqb, w_q_b, h2, p["g_attn"][1], dh3)
    dkv = jnp.concatenate([dkh.transpose(1, 0, 2).reshape(t, kvw), dvh.transpose(1, 0, 2).reshape(t, kvw)], axis=1)
    dw_kv = _matmul_tn("kv_dw", hn_kv, dkv)
    zero = comm.send_grads("attn_b", {"w_out_b": dw_out_b, "w_q_b": dw_q_b, "w_kv": dw_kv})
    dh2, dg_kv = _matmul_nt_rmsbwd("kv_dh", dkv, w_kv, h2, p["g_kv"] + zero, dh2)
    grads["g_kv"] = dg_kv
    dh1, dg_mlp0 = mlp_bwd("mlp0", mlp0, 0, dh2)
    grads["g_mlp"] = (dg_mlp0, dg_mlp1)

    do_a = _matmul_nt("a_do", dh1, w_out_a)
    dw_out_a = _matmul_tn("a_dwout", o_a, dh1)
    zero = comm.send_grads("attn_a_out", {"w_out_a": dw_out_a})
    dq, dk, dv, dc_cols, grads["gq_a"], grads["gk_a"] = _fox_bwd(
        "a_attn_bwd", proj, c, crow, p["gq_a"] + zero, p["gk_a"], lse_a, do_a, n_heads)
    dc = jnp.pad(dc_cols.reshape(t, n_heads // 2, LANES)[:, :, :2].reshape(t, n_heads), ((0, 0), (0, LANES - n_heads)))
    dfl, db_f = _gate_bwd("a_gate_bwd", proj, b_pad, dc, n_heads, gate_col)
    grads["b_f"] = db_f
    dproj = jnp.concatenate([dq, dk, dv, dfl], axis=1)
    dw_in = _matmul_tn("a_dwin", xn1, dproj, tn=640)
    zero = comm.send_grads("attn_a_in", {"w_in_a": dw_in})
    grad_x, dg_attn0 = _matmul_nt_rmsbwd("a_dx", dproj, w_in, x, p["g_attn"][0] + zero, dh1)
    grads["g_attn"] = (dg_attn0, dg_attn1)
    return loss_tile, grad_x, grads


EVERYONE = (1, 2, 3, 4, 5, 6, 7)
SAME_CORE = (1, 2, 4, 6)
OTHER_CHIPS = (2, 4, 6)


class _InFlight:
    def __init__(self, scatter, ks, send_sems, recv_sems, srcs, lands, token):
        self.scatter, self.ks, self.send_sems, self.recv_sems = scatter, ks, send_sems, recv_sems
        self.srcs, self.lands, self.token = list(srcs), list(lands), token


def _mesh_peers(ks=EVERYONE):
    x, y, c = lax.axis_index("x"), lax.axis_index("y"), lax.axis_index("c")
    peers = []
    for k in ks:
        px, py, pc = x ^ ((k >> 2) & 1), y ^ ((k >> 1) & 1), c ^ (k & 1)
        peers.append(((px, py, pc), 4 * px + 2 * py + pc))
    return 4 * x + 2 * y + c, peers


_HBM_SPEC = pl.BlockSpec(memory_space=pltpu.HBM)
_SEM_SPEC = pl.BlockSpec(memory_space=pltpu.SEMAPHORE)
_SIDE_EFFECT = pltpu.SideEffectType.DATAFLOW_SIDE_EFFECTING


def _exchange_start(name, arrays, scatter, ks=EVERYONE):
    n = len(arrays)
    lands = [lax.empty(a.shape if scatter else (N_DEV,) + a.shape, a.dtype) for a in arrays]

    def body(*refs):
        src, land = refs[:n], refs[n:2 * n]
        send_sems, recv_sems, token, own_sems = refs[2 * n], refs[2 * n + 1], refs[-2], refs[-1]
        pos, peers = _mesh_peers(ks)
        own = [pltpu.make_async_copy(src[a].at[pos] if scatter else src[a], land[a].at[pos], own_sems.at[a]) for a in range(n)]
        for cp in own:
            cp.start()
        for a in range(n):
            for k, (peer, peer_pos) in enumerate(peers):
                pltpu.make_async_remote_copy(
                    src_ref=src[a].at[peer_pos] if scatter else src[a], dst_ref=land[a].at[pos],
                    send_sem=send_sems.at[a * len(ks) + k], recv_sem=recv_sems.at[a * len(ks) + k],
                    device_id=peer, device_id_type=pl.DeviceIdType.MESH).start()
        token[...] = jnp.zeros_like(token)
        for cp in own:
            cp.wait()

    operands = [pltpu.with_memory_space_constraint(a, pltpu.HBM) for a in list(arrays) + lands]
    outs = pl.pallas_call(
        body, name=name,
        out_shape=(pltpu.SemaphoreType.DMA((n * len(ks),)), pltpu.SemaphoreType.DMA((n * len(ks),)),
                   *[pltpu.HBM(a.shape, a.dtype) for a in operands], _sds((8, LANES), F32)),
        in_specs=[_HBM_SPEC] * (2 * n),
        out_specs=(_SEM_SPEC, _SEM_SPEC, *[_HBM_SPEC] * (2 * n), pl.BlockSpec(memory_space=pltpu.VMEM)),
        input_output_aliases={i: 2 + i for i in range(2 * n)},
        scratch_shapes=[pltpu.SemaphoreType.DMA((n,))],
        compiler_params=pltpu.CompilerParams(has_side_effects=_SIDE_EFFECT),
    )(*operands)
    return _InFlight(scatter, ks, outs[0], outs[1], outs[2:2 + n], outs[2 + n:2 + 2 * n], outs[-1])


def _exchange_wait(name, flight, which, after):
    m = len(which)
    scatter, ks = flight.scatter, flight.ks

    def body(*refs):
        src, land = refs[:m], refs[m:2 * m]
        send_sems, recv_sems = refs[2 * m], refs[2 * m + 1]
        _, peers = _mesh_peers(ks)
        for i, a in enumerate(which):
            for k, (peer, peer_pos) in enumerate(peers):
                cp = pltpu.make_async_remote_copy(
                    src_ref=src[i].at[peer_pos] if scatter else src[i], dst_ref=land[i].at[peer_pos],
                    send_sem=send_sems.at[a * len(ks) + k], recv_sem=recv_sems.at[a * len(ks) + k],
                    device_id=peer, device_id_type=pl.DeviceIdType.MESH)
                cp.wait_send()
                cp.wait_recv()

    operands = [flight.srcs[a] for a in which] + [flight.lands[a] for a in which]
    outs = pl.pallas_call(
        body, name=name, out_shape=tuple(pltpu.HBM(a.shape, a.dtype) for a in operands),
        in_specs=[_HBM_SPEC] * (2 * m) + [_SEM_SPEC, _SEM_SPEC, pl.BlockSpec(memory_space=pl.ANY)],
        out_specs=tuple([_HBM_SPEC] * (2 * m)), input_output_aliases={i: i for i in range(2 * m)},
        compiler_params=pltpu.CompilerParams(has_side_effects=_SIDE_EFFECT),
    )(*operands, flight.send_sems, flight.recv_sems, after)
    return list(outs[m:])


def _relay_start(name, lands):
    n = len(lands)

    def body(*refs):
        land, send_sems, recv_sems, token = refs[:n], refs[n], refs[n + 1], refs[-1]
        _, peers = _mesh_peers(OTHER_CHIPS)
        sibling = (lax.axis_index("x"), lax.axis_index("y"), 1 - lax.axis_index("c"))
        for a in range(n):
            for k, (_, peer_pos) in enumerate(peers):
                pltpu.make_async_remote_copy(
                    src_ref=land[a].at[peer_pos], dst_ref=land[a].at[peer_pos],
                    send_sem=send_sems.at[a * len(peers) + k], recv_sem=recv_sems.at[a * len(peers) + k],
                    device_id=sibling, device_id_type=pl.DeviceIdType.MESH).start()
        token[...] = jnp.zeros_like(token)

    count = n * len(OTHER_CHIPS)
    outs = pl.pallas_call(
        body, name=name,
        out_shape=(pltpu.SemaphoreType.DMA((count,)), pltpu.SemaphoreType.DMA((count,)),
                   *[pltpu.HBM(a.shape, a.dtype) for a in lands], _sds((8, LANES), F32)),
        in_specs=[_HBM_SPEC] * n,
        out_specs=(_SEM_SPEC, _SEM_SPEC, *[_HBM_SPEC] * n, pl.BlockSpec(memory_space=pltpu.VMEM)),
        input_output_aliases={i: 2 + i for i in range(n)},
        compiler_params=pltpu.CompilerParams(has_side_effects=_SIDE_EFFECT),
    )(*[pltpu.with_memory_space_constraint(a, pltpu.HBM) for a in lands])
    return _InFlight(False, OTHER_CHIPS, outs[0], outs[1], [], outs[2:2 + n], outs[-1])


def _relay_wait(name, flight, which, after):
    m = len(which)

    def body(*refs):
        land, send_sems, recv_sems = refs[:m], refs[m], refs[m + 1]
        _, peers = _mesh_peers(OTHER_CHIPS)
        sibling = (lax.axis_index("x"), lax.axis_index("y"), 1 - lax.axis_index("c"))
        for i, a in enumerate(which):
            for k, (_, peer_pos) in enumerate(peers):
                cp = pltpu.make_async_remote_copy(
                    src_ref=land[i].at[peer_pos], dst_ref=land[i].at[peer_pos ^ 1],
                    send_sem=send_sems.at[a * len(peers) + k], recv_sem=recv_sems.at[a * len(peers) + k],
                    device_id=sibling, device_id_type=pl.DeviceIdType.MESH)
                cp.wait_send()
                cp.wait_recv()

    operands = [flight.lands[a] for a in which]
    outs = pl.pallas_call(
        body, name=name, out_shape=tuple(pltpu.HBM(a.shape, a.dtype) for a in operands),
        in_specs=[_HBM_SPEC] * m + [_SEM_SPEC, _SEM_SPEC, pl.BlockSpec(memory_space=pl.ANY)],
        out_specs=tuple([_HBM_SPEC] * m), input_output_aliases={i: i for i in range(m)},
        compiler_params=pltpu.CompilerParams(has_side_effects=_SIDE_EFFECT),
    )(*operands, flight.send_sems, flight.recv_sems, after)
    return list(outs)


def _sum_parts(p_ref):
    g = p_ref[0].astype(F32)
    for dev in range(1, N_DEV):
        g = g + p_ref[dev].astype(F32)
    return g


def _adam_update(g, w, m, v):
    m_new = ADAM_B1 * m + (1.0 - ADAM_B1) * g
    v_new = ADAM_B2 * v + (1.0 - ADAM_B2) * jnp.square(g)
    m_hat = m_new / (1.0 - ADAM_B1 ** ADAM_STEP)
    v_hat = v_new / (1.0 - ADAM_B2 ** ADAM_STEP)
    return -ADAM_LR * (m_hat / (jnp.sqrt(v_hat) + ADAM_EPS) + ADAM_WD * w), m_new, v_new


def _adamw(name, parts, w, m, v, layer=None, into=None):
    r, c = w.shape[-2:]
    tr = r if r <= 256 else 256
    n_into = 0 if into is None else len(into)

    def body(p_ref, w_ref, m_ref, v_ref, *refs):
        g_ref, d_ref, mo_ref, vo_ref = refs[n_into:]
        g = _sum_parts(p_ref)
        g_ref[...] = g
        d_ref[...], mo_ref[...], vo_ref[...] = _adam_update(g, w_ref[...], m_ref[...], v_ref[...])

    if layer is None:
        blk = pl.BlockSpec((tr, c), lambda i: (i, 0))
    else:
        blk = pl.BlockSpec((None, tr, c), lambda i: (layer, i, 0))
    return pl.pallas_call(
        body, name=name, grid=(r // tr,),
        in_specs=[pl.BlockSpec((N_DEV, tr, c), lambda i: (0, i, 0)), blk, blk, blk] + [pl.BlockSpec(memory_space=pl.ANY)] * n_into,
        out_specs=[blk] * 4, out_shape=[_sds(w.shape, F32)] * 4,
        input_output_aliases={4 + i: i for i in range(n_into)}, compiler_params=_params(1),
    )(parts, w, m, v, *(into or ()))


SMALL_PACK_ROWS = 16
LOSS_ROW = 11


def _small_rows(grads, loss_tile):
    return [(0, 1, grads["g_attn"][0]), (1, 1, grads["g_attn"][1]), (2, 1, grads["g_mlp"][0]), (3, 1, grads["g_mlp"][1]),
            (4, 1, grads["g_kv"]), (5, 1, grads["b_f"]), (6, 1, grads["gq_a"]), (7, 1, grads["gk_a"]), (8, 1, grads["gk_b"]),
            (9, 1, grads["gq_b"]), (10, 1, grads["sinks"]), (LOSS_ROW, 1, loss_tile)]


SMALL_ROWS = {"g_attn": (0, 2), "g_mlp": (2, 2), "g_kv": (4, 1), "b_f": (5, 1), "gq_a": (6, 1), "gk_a": (7, 1),
              "gk_b": (8, 1), "gq_b": (9, 1), "sinks": (10, 1)}


def _pack_small(name, pieces, d):
    def body(*refs):
        out = refs[-1]
        out[...] = jnp.zeros_like(out)
        for (row, rows, _), ref in zip(pieces, refs[:-1]):
            out[row:row + rows, 0:ref.shape[1]] = ref[0:rows, :]

    return pl.pallas_call(body, name=name, out_shape=_sds((SMALL_PACK_ROWS, d), F32), compiler_params=_params(0))(
        *[piece for _, _, piece in pieces])


def _adamw_small(name, parts, parts_rel_bias, w, m, v):
    def body(*refs):
        ins, outs = refs[2:2 + 3 * len(SMALL)], refs[2 + 3 * len(SMALL):]
        pack, rel = _sum_parts(refs[0]), _sum_parts(refs[1])
        for i, k in enumerate(SMALL):
            w_ref, m_ref, v_ref = ins[3 * i:3 * i + 3]
            if k == "rel_bias":
                g = rel
            else:
                row, rows = SMALL_ROWS[k]
                g = pack[row:row + rows, 0:w_ref.shape[1]]
            outs[4 * i][...] = g
            outs[4 * i + 1][...], outs[4 * i + 2][...], outs[4 * i + 3][...] = _adam_update(g, w_ref[...], m_ref[...], v_ref[...])
        outs[-1][...] = pack[LOSS_ROW:LOSS_ROW + 1, 0:LANES]

    operands = [parts, parts_rel_bias] + [t[k] for k in SMALL for t in (w, m, v)]
    out_shape = [_sds(w[k].shape, F32) for k in SMALL for _ in range(4)] + [_sds((1, LANES), F32)]
    outs = pl.pallas_call(body, name=name, out_shape=out_shape, compiler_params=_params(0))(*operands)
    return {k: outs[4 * i:4 * i + 4] for i, k in enumerate(SMALL)}, outs[-1]


class _Comm:
    ORDER = ("w_in_a", "w_out_a", "w_up0", "w_down0", "w_kv", "w_q_b", "w_out_b", "w_up1", "w_down1")

    def __init__(self, shards, d, n_in):
        self.d, self.n_in = d, n_in
        self.flight = _exchange_start("gather_start", [shards[n].astype(BF16) for n in self.ORDER], scatter=False, ks=SAME_CORE)
        self.relays, self.sent = {}, []

    def prefetch(self, names, after):
        which = [self.ORDER.index(n) for n in names]
        landed = _exchange_wait(f"gather_wait_{names[0]}", self.flight, which, self.flight.token if after is None else after)
        relay = _relay_start(f"gather_relay_{names[0]}", landed)
        for n in names:
            self.relays[n] = (relay, names)
        return relay.token

    def weights(self, names, after):
        relay, group = self.relays[names[0]]
        landed = _relay_wait(f"gather_relay_wait_{names[0]}", relay, [group.index(n) for n in names],
                             relay.token if after is None else after)
        return [self._whole(n, g) for n, g in zip(names, landed)]

    def _whole(self, name, g):
        if name == "w_in_a":
            return _join_col_blocks("w_in_join", g, -(-self.n_in // LANES) * LANES)
        if name.startswith("w_up"):
            return g
        return g.reshape(-1, g.shape[-1])

    def _chunks(self, name, g):
        if name == "w_in_a":
            return _split_col_blocks("dw_in_split", g, N_DEV, self.n_in // N_DEV)
        if name.startswith("w_up"):
            return g
        return g.reshape(N_DEV, g.shape[0] // N_DEV, g.shape[1])

    def send_grads(self, tag, partials):
        names = list(partials)
        flight = _exchange_start(f"scatter_start_{tag}", [self._chunks(n, partials[n]) for n in names], scatter=True)
        self.sent.append((tag, flight, names))
        return flight.token[0, 0]

    def received(self, index, after):
        tag, flight, names = self.sent[index]
        landed = _exchange_wait(f"scatter_wait_{tag}", flight, list(range(len(names))), after)
        return dict(zip(names, landed))


def kernel(x, g_attn, g_mlp, w_in_a, b_f, gq_a, gk_a, w_out_a, g_kv, w_kv, gk_b, w_q_b, gq_b, sinks, rel_bias, w_out_b, w_up, w_down, loss_target, m_g_attn, m_g_mlp, m_w_in_a, m_b_f, m_gq_a, m_gk_a, m_w_out_a, m_g_kv, m_w_kv, m_gk_b, m_w_q_b, m_gq_b, m_sinks, m_rel_bias, m_w_out_b, m_w_up, m_w_down, v_g_attn, v_g_mlp, v_w_in_a, v_b_f, v_gq_a, v_gk_a, v_w_out_a, v_g_kv, v_w_kv, v_gk_b, v_w_q_b, v_gq_b, v_sinks, v_rel_bias, v_w_out_b, v_w_up, v_w_down):
    w = dict(g_attn=g_attn, g_mlp=g_mlp, w_in_a=w_in_a, b_f=b_f, gq_a=gq_a, gk_a=gk_a, w_out_a=w_out_a, g_kv=g_kv,
             w_kv=w_kv, gk_b=gk_b, w_q_b=w_q_b, gq_b=gq_b, sinks=sinks, rel_bias=rel_bias, w_out_b=w_out_b,
             w_up=w_up, w_down=w_down)
    mom = dict(g_attn=m_g_attn, g_mlp=m_g_mlp, w_in_a=m_w_in_a, b_f=m_b_f, gq_a=m_gq_a, gk_a=m_gk_a, w_out_a=m_w_out_a,
               g_kv=m_g_kv, w_kv=m_w_kv, gk_b=m_gk_b, w_q_b=m_w_q_b, gq_b=m_gq_b, sinks=m_sinks, rel_bias=m_rel_bias,
               w_out_b=m_w_out_b, w_up=m_w_up, w_down=m_w_down)
    var = dict(g_attn=v_g_attn, g_mlp=v_g_mlp, w_in_a=v_w_in_a, b_f=v_b_f, gq_a=v_gq_a, gk_a=v_gk_a, w_out_a=v_w_out_a,
               g_kv=v_g_kv, w_kv=v_w_kv, gk_b=v_gk_b, w_q_b=v_w_q_b, gq_b=v_gq_b, sinks=v_sinks, rel_bias=v_rel_bias,
               w_out_b=v_w_out_b, w_up=v_w_up, w_down=v_w_down)
    d = x.shape[2]
    where = {"w_in_a": ("w_in_a", 0), "w_out_a": ("w_out_a", 0), "w_kv": ("w_kv", None), "w_q_b": ("w_q_b", 0),
             "w_out_b": ("w_out_b", 0), "w_up0": ("w_up", 0), "w_up1": ("w_up", 1), "w_down0": ("w_down", 0),
             "w_down1": ("w_down", 1)}
    shards = {n: (w[k] if layer is None else w[k][layer]) for n, (k, layer) in where.items()}
    comm = _Comm(shards, d, w_in_a.shape[2] * N_DEV)
    loss_tile, grad_x, grads = _local_step(x[0], loss_target[0], {k: w[k] for k in SMALL}, comm)

    small_flight = _exchange_start(
        "gather_small_grads", [_pack_small("pack_small", _small_rows(grads, loss_tile), d), grads["rel_bias"]], scatter=False)
    res, after = {}, small_flight.token
    for index in range(len(comm.sent)):
        for n, parts in comm.received(index, after).items():
            k, layer = where[n]
            res[k] = _adamw(f"adam_{n}", parts, w[k], mom[k], var[k], layer, res.get(k))
            after = res[k][0]
    as_rows = lambda tree: {k: tree[k] if tree[k].ndim == 2 else tree[k].reshape(1, -1) for k in SMALL}
    small, loss_row = _adamw_small("adam_small", *_exchange_wait("gather_small_wait", small_flight, [0, 1], after),
                                   as_rows(w), as_rows(mom), as_rows(var))
    loss = loss_row[0, 0]
    for k in SMALL:
        res[k] = [a.reshape(w[k].shape) for a in small[k]]

    outs = [loss, grad_x[None]]
    for i in range(4):
        outs.extend(res[k][i] for k in WEIGHTS)
    return tuple(outs)
```

```python
import numpy as np
import jax
import jax.numpy as jnp
from jax import lax
from jax.experimental import pallas as pl
from jax.experimental.pallas import tpu as pltpu

F32 = jnp.float32
BF16 = jnp.bfloat16

N_DEV = 8
HEAD_DIM = 64
WINDOW = 128
N_BUCKETS = 32
REL_MAX_DIST = 128
NORM_EPS = 1e-6
NEG = -1e30
LANES = 128
VMEM_LIMIT = 56 * 1024 * 1024

ADAM_LR = 0.001
ADAM_B1 = 0.9
ADAM_B2 = 0.999
ADAM_EPS = 1e-08
ADAM_WD = 0.01
ADAM_STEP = 10

SMALL = ("g_attn", "g_mlp", "b_f", "gq_a", "gk_a", "g_kv", "gk_b", "gq_b", "sinks", "rel_bias")
WEIGHTS = ("g_attn", "g_mlp", "w_in_a", "b_f", "gq_a", "gk_a", "w_out_a", "g_kv", "w_kv", "gk_b",
           "w_q_b", "gq_b", "sinks", "rel_bias", "w_out_b", "w_up", "w_down")


def _params(n_grid):
    return pltpu.CompilerParams(dimension_semantics=("arbitrary",) * n_grid, vmem_limit_bytes=VMEM_LIMIT)


def _sds(shape, dtype):
    return jax.ShapeDtypeStruct(tuple(shape), dtype)


def _rms(x, g):
    return (x * lax.rsqrt(jnp.mean(x * x, axis=-1, keepdims=True) + NORM_EPS)) * g


def _dot_nt(a, b):
    return lax.dot_general(a, b, (((1,), (1,)), ((), ())), preferred_element_type=F32)


def _dot_tn(a, b):
    return lax.dot_general(a, b, (((0,), (0,)), ((), ())), preferred_element_type=F32)


def _dot(a, b):
    return jnp.dot(a, b, preferred_element_type=F32)


def _dot_exact(a, b):
    return jnp.dot(a, b, preferred_element_type=F32, precision=lax.Precision.HIGHEST)


def _norm_matmul(name, x, g, w, *, tn=None, relu2=False, w_rows=False):
    t, d = x.shape
    blocked = w.ndim == 3
    if blocked:
        tn = w.shape[2]
        n = w.shape[0] * tn
        w_spec = pl.BlockSpec((None, d, tn), lambda i, j: (j, 0, 0))
    elif w_rows:
        n = w.shape[0]
        w_spec = pl.BlockSpec((tn, d), lambda i, j: (j, 0))
    else:
        n = w.shape[1]
        w_spec = pl.BlockSpec((d, tn), lambda i, j: (0, j))
    tm = min(1024, t)

    def body(x_ref, g_ref, w_ref, y_ref, xn_ref):
        @pl.when(pl.program_id(1) == 0)
        def _():
            xn_ref[...] = _rms(x_ref[...], g_ref[...]).astype(BF16)

        y = _dot_nt(xn_ref[...], w_ref[...]) if w_rows else _dot(xn_ref[...], w_ref[...])
        y_ref[...] = jnp.square(jnp.maximum(y, 0.0)).astype(BF16) if relu2 else y

    out_shape = [_sds((t, n), BF16 if relu2 else F32), _sds((t, d), BF16)]
    out_specs = [pl.BlockSpec((tm, tn), lambda i, j: (i, j)), pl.BlockSpec((tm, d), lambda i, j: (i, 0))]
    return pl.pallas_call(
        body, name=name, grid=(t // tm, n // tn),
        in_specs=[pl.BlockSpec((tm, d), lambda i, j: (i, 0)), pl.BlockSpec((1, d), lambda i, j: (0, 0)), w_spec],
        out_specs=out_specs, out_shape=out_shape, compiler_params=_params(2),
    )(x, g.reshape(1, d), w)


def _matmul_res(name, a, w, res, *, tn=512, after=None):
    t, k = a.shape
    n = w.shape[1]
    tm = min(1024, t)

    def body(a_ref, w_ref, r_ref, *rest):
        rest[-1][...] = r_ref[...] + _dot(a_ref[...], w_ref[...])

    extra = [] if after is None else [after]
    return pl.pallas_call(
        body, name=name, grid=(t // tm, n // tn),
        in_specs=[pl.BlockSpec((tm, k), lambda i, j: (i, 0)), pl.BlockSpec((k, tn), lambda i, j: (0, j)),
                  pl.BlockSpec((tm, tn), lambda i, j: (i, j))] + [pl.BlockSpec((8, LANES), lambda i, j: (0, 0))] * len(extra),
        out_specs=pl.BlockSpec((tm, tn), lambda i, j: (i, j)), out_shape=_sds((t, n), F32),
        compiler_params=_params(2),
    )(a, w, res, *extra)


def _matmul_nt(name, dy, w, *, a=None, tk=1024):
    t, n = dy.shape
    k = w.shape[0]
    tm = min(1024, t)

    def body(dy_ref, w_ref, *rest):
        o_ref = rest[-1]
        r = _dot_nt(dy_ref[...].astype(BF16), w_ref[...])
        if a is not None:
            r = r * (2.0 * jnp.sqrt(rest[0][...].astype(F32)))
        o_ref[...] = r.astype(BF16)

    in_specs = [pl.BlockSpec((tm, n), lambda i, j: (i, 0)), pl.BlockSpec((tk, n), lambda i, j: (j, 0))]
    args = [dy, w]
    if a is not None:
        in_specs.append(pl.BlockSpec((tm, tk), lambda i, j: (i, j)))
        args.append(a)
    return pl.pallas_call(
        body, name=name, grid=(t // tm, k // tk), in_specs=in_specs,
        out_specs=pl.BlockSpec((tm, tk), lambda i, j: (i, j)), out_shape=_sds((t, k), BF16),
        compiler_params=_params(2),
    )(*args)


def _matmul_nt_rmsbwd(name, dy, w, x, g, dres, *, w_rows=False):
    t, k = dy.shape
    blocked = w.ndim == 3
    d = w.shape[1] if blocked or w_rows else w.shape[0]
    tm = min(512, t)

    def body(dy_ref, w_ref, x_ref, g_ref, r_ref, dx_ref, dg_ref):
        if blocked:
            kb = w.shape[2]
            dxn = _dot_nt(dy_ref[:, 0:kb].astype(BF16), w_ref[0])
            for j in range(1, w.shape[0]):
                dxn += _dot_nt(dy_ref[:, j * kb:(j + 1) * kb].astype(BF16), w_ref[j])
        elif w_rows:
            dxn = _dot(dy_ref[...].astype(BF16), w_ref[...])
        else:
            dxn = _dot_nt(dy_ref[...].astype(BF16), w_ref[...])
        _, vjp = jax.vjp(_rms, x_ref[...], g_ref[...])
        dx, dg = vjp(dxn)
        dx_ref[...] = r_ref[...] + dx

        @pl.when(pl.program_id(0) == 0)
        def _():
            dg_ref[...] = jnp.zeros_like(dg_ref)

        dg_ref[...] += dg

    row = lambda i: (i, 0)
    fixed = lambda i: (0, 0)
    return pl.pallas_call(
        body, name=name, grid=(t // tm,),
        in_specs=[pl.BlockSpec((tm, k), row), pl.BlockSpec(w.shape, (lambda i: (0, 0, 0)) if blocked else fixed),
                  pl.BlockSpec((tm, d), row), pl.BlockSpec((1, d), fixed), pl.BlockSpec((tm, d), row)],
        out_specs=[pl.BlockSpec((tm, d), row), pl.BlockSpec((1, d), fixed)],
        out_shape=[_sds((t, d), F32), _sds((1, d), F32)], compiler_params=_params(1),
    )(dy, w, x, g.reshape(1, d), dres)


def _matmul_tn(name, a, b, *, tk=1024, tn=1024, col_blocks=None):
    t, k = a.shape
    n = b.shape[1]
    tk = min(tk, k)
    if col_blocks:
        tn = n // col_blocks
        out_spec, out_shape = pl.BlockSpec((None, tk, tn), lambda i, j: (j, i, 0)), _sds((col_blocks, k, tn), BF16)
    else:
        tn = min(tn, n)
        out_spec, out_shape = pl.BlockSpec((tk, tn), lambda i, j: (i, j)), _sds((k, n), BF16)

    def body(a_ref, b_ref, o_ref):
        o_ref[...] = _dot_tn(a_ref[...].astype(BF16), b_ref[...].astype(BF16)).astype(BF16)

    return pl.pallas_call(
        body, name=name, grid=(k // tk, n // tn),
        in_specs=[pl.BlockSpec((t, tk), lambda i, j: (0, i)), pl.BlockSpec((t, tn), lambda i, j: (0, j))],
        out_specs=out_spec, out_shape=out_shape, compiler_params=_params(2),
    )(a, b)


def _join_row_blocks(name, blocks, rows):
    b, r, c = blocks.shape
    tc = min(256, c)

    def body(g_ref, o_ref):
        o_ref[...] = jnp.zeros_like(o_ref)
        for j in range(b):
            o_ref[r * j:r * (j + 1), :] = g_ref[j]

    return pl.pallas_call(
        body, name=name, grid=(c // tc,), in_specs=[pl.BlockSpec((b, r, tc), lambda i: (0, 0, i))],
        out_specs=pl.BlockSpec((rows, tc), lambda i: (0, i)), out_shape=_sds((rows, c), blocks.dtype),
        compiler_params=_params(1),
    )(blocks)


def _split_row_blocks(name, mat, b, r):
    rows, c = mat.shape
    tc = min(256, c)

    def body(w_ref, o_ref):
        for j in range(b):
            o_ref[j] = w_ref[r * j:r * (j + 1), :]

    return pl.pallas_call(
        body, name=name, grid=(c // tc,), in_specs=[pl.BlockSpec((rows, tc), lambda i: (0, i))],
        out_specs=pl.BlockSpec((b, r, tc), lambda i: (0, 0, i)), out_shape=_sds((b, r, c), mat.dtype),
        compiler_params=_params(1),
    )(mat)


def _matmul_res_loss(name, a, w, res, target, *, tn=512):
    t, k = a.shape
    n = w.shape[1]
    tm = min(1024, t)

    def body(a_ref, w_ref, r_ref, t_ref, dy_ref, l_ref):
        e = r_ref[...] + _dot(a_ref[...], w_ref[...]) - t_ref[...]
        dy_ref[...] = e * (1.0 / n)

        @pl.when((pl.program_id(0) == 0) & (pl.program_id(1) == 0))
        def _():
            l_ref[...] = jnp.zeros_like(l_ref)

        l_ref[...] += (0.5 / n) * jnp.sum(e * e)

    tile = pl.BlockSpec((tm, tn), lambda i, j: (i, j))
    return pl.pallas_call(
        body, name=name, grid=(t // tm, n // tn),
        in_specs=[pl.BlockSpec((tm, k), lambda i, j: (i, 0)), pl.BlockSpec((k, tn), lambda i, j: (0, j)), tile, tile],
        out_specs=[tile, pl.BlockSpec((8, LANES), lambda i, j: (0, 0))],
        out_shape=[_sds((t, n), F32), _sds((8, LANES), F32)], compiler_params=_params(2),
    )(a, w, res, target)


def _gate_fwd(name, proj, b_pad, n_heads, gate_col):
    t = proj.shape[0]
    tb = min(256, t)
    tri = jnp.asarray(np.tril(np.ones((tb, tb), np.float32)))

    def body(p_ref, b_ref, tri_ref, c_ref, carry):
        @pl.when(pl.program_id(0) == 0)
        def _():
            carry[...] = jnp.zeros_like(carry)

        lane = lax.broadcasted_iota(jnp.int32, (tb, LANES), 1)
        lf = jnp.where(lane < n_heads, jax.nn.log_sigmoid(p_ref[...] + b_ref[...]), 0.0)
        c = _dot_exact(tri_ref[...], lf) + carry[0:1, :]
        c_ref[...] = c
        carry[...] = jnp.broadcast_to(c[tb - 1:tb, :], carry.shape)

    return pl.pallas_call(
        body, name=name, grid=(t // tb,),
        in_specs=[pl.BlockSpec((tb, LANES), lambda i: (i, gate_col)), pl.BlockSpec((1, LANES), lambda i: (0, 0)),
                  pl.BlockSpec((tb, tb), lambda i: (0, 0))],
        out_specs=pl.BlockSpec((tb, LANES), lambda i: (i, 0)), out_shape=_sds((t, LANES), F32),
        scratch_shapes=[pltpu.VMEM((8, LANES), F32)], compiler_params=_params(1),
    )(proj, b_pad, tri)


def _gate_bwd(name, proj, b_pad, dc, n_heads, gate_col):
    t = proj.shape[0]
    tb = min(256, t)
    nb = t // tb
    triu = jnp.asarray(np.triu(np.ones((tb, tb), np.float32)))

    def body(p_ref, b_ref, dc_ref, tri_ref, df_ref, db_ref, carry):
        @pl.when(pl.program_id(0) == 0)
        def _():
            carry[...] = jnp.zeros_like(carry)
            db_ref[...] = jnp.zeros_like(db_ref)

        dcv = dc_ref[...]
        dlf = _dot_exact(tri_ref[...], dcv) + carry[0:1, :]
        carry[...] = jnp.broadcast_to(dlf[0:1, :], carry.shape)
        lane = lax.broadcasted_iota(jnp.int32, (tb, LANES), 1)
        z = p_ref[...] + b_ref[...]
        df = jnp.where(lane < n_heads, dlf / (1.0 + jnp.exp(z)), 0.0)
        df_ref[...] = df.astype(BF16)
        db_ref[...] += jnp.sum(df, axis=0, keepdims=True)

    return pl.pallas_call(
        body, name=name, grid=(nb,),
        in_specs=[pl.BlockSpec((tb, LANES), lambda i: (nb - 1 - i, gate_col)), pl.BlockSpec((1, LANES), lambda i: (0, 0)),
                  pl.BlockSpec((tb, LANES), lambda i: (nb - 1 - i, 0)), pl.BlockSpec((tb, tb), lambda i: (0, 0))],
        out_specs=[pl.BlockSpec((tb, LANES), lambda i: (nb - 1 - i, 0)), pl.BlockSpec((1, LANES), lambda i: (0, 0))],
        out_shape=[_sds((t, LANES), BF16), _sds((1, LANES), F32)],
        scratch_shapes=[pltpu.VMEM((8, LANES), F32)], compiler_params=_params(1),
    )(proj, b_pad, dc, triu)


def _qhead(qp, g):
    return _rms(qp, g) * (HEAD_DIM ** -0.5)


def _column(mat, idx):
    lane = lax.broadcasted_iota(jnp.int32, mat.shape, 1)
    return jnp.sum(jnp.where(lane == idx, mat, 0.0), axis=1, keepdims=True)


def _fox_scores(kk, qi, ckey, cq_i, i, bq):
    length = kk.shape[0]
    s = _dot_nt(kk, qi) + cq_i - ckey[:length]
    key = lax.broadcasted_iota(jnp.int32, (length, bq), 0)
    qry = lax.broadcasted_iota(jnp.int32, (length, bq), 1) + i * bq
    return jnp.where(key <= qry, s, NEG)


def _fox_fwd(name, proj, c, crow, gq, gk, n_heads):
    t = proj.shape[0]
    hw = n_heads * HEAD_DIM
    npair = n_heads // 2
    bq = min(256, t)
    nq = t // bq

    def body(q_ref, k_ref, v_ref, c_ref, crow_ref, gq_ref, gk_ref, o_ref, lse_ref):
        hp = pl.program_id(0)
        lse_ref[...] = jnp.zeros_like(lse_ref)
        outs = []
        for hh in range(2):
            sl = slice(hh * HEAD_DIM, (hh + 1) * HEAD_DIM)
            qn = _qhead(q_ref[:, sl], gq_ref[...]).astype(BF16)
            kn = _rms(k_ref[:, sl], gk_ref[...]).astype(BF16)
            v_t = v_ref[:, sl].T.astype(BF16)
            ckey = _column(c_ref[...], 2 * hp + hh)
            cq = crow_ref[0, hh:hh + 1, :]
            o_blocks = []
            for i in range(nq):
                cols = slice(i * bq, (i + 1) * bq)
                length = (i + 1) * bq
                s = _fox_scores(kn[:length], qn[cols], ckey, cq[:, cols], i, bq)
                m = jnp.max(s, axis=0, keepdims=True)
                p = jnp.exp(s - m)
                l = jnp.sum(p, axis=0, keepdims=True)
                o_blocks.append((_dot(v_t[:, :length], p.astype(BF16)) / l).T)
                lse_ref[0, hh:hh + 1, cols] = m + jnp.log(l)
            outs.append(jnp.concatenate(o_blocks, axis=0))
        o_ref[...] = jnp.concatenate(outs, axis=1).astype(BF16)

    col = lambda off: (lambda h: (0, off + h))
    fixed = lambda h: (0, 0)
    return pl.pallas_call(
        body, name=name, grid=(npair,),
        in_specs=[pl.BlockSpec((t, LANES), col(0)), pl.BlockSpec((t, LANES), col(npair)), pl.BlockSpec((t, LANES), col(2 * npair)),
                  pl.BlockSpec((t, LANES), fixed), pl.BlockSpec((1, 2, t), lambda h: (h, 0, 0)),
                  pl.BlockSpec((1, HEAD_DIM), fixed), pl.BlockSpec((1, HEAD_DIM), fixed)],
        out_specs=[pl.BlockSpec((t, LANES), col(0)), pl.BlockSpec((1, 8, t), lambda h: (h, 0, 0))],
        out_shape=[_sds((t, hw), BF16), _sds((npair, 8, t), F32)], compiler_params=_params(1),
    )(proj, proj, proj, c, crow, gq, gk)


def _fox_bwd(name, proj, c, crow, gq, gk, lse, do, n_heads):
    t = proj.shape[0]
    hw = n_heads * HEAD_DIM
    npair = n_heads // 2
    bq = min(256, t)
    nq = t // bq

    def body(q_ref, k_ref, v_ref, c_ref, crow_ref, gq_ref, gk_ref, lse_ref, do_ref,
             dq_ref, dk_ref, dv_ref, dc_ref, dgq_ref, dgk_ref, dk_acc, dv_acc, dc_acc):
        hp = pl.program_id(0)

        @pl.when(hp == 0)
        def _():
            dgq_ref[...] = jnp.zeros_like(dgq_ref)
            dgk_ref[...] = jnp.zeros_like(dgk_ref)

        lane = lax.broadcasted_iota(jnp.int32, (t, LANES), 1)
        dc_pair = jnp.zeros((t, LANES), F32)
        dqs, dks, dvs = [], [], []
        for hh in range(2):
            sl = slice(hh * HEAD_DIM, (hh + 1) * HEAD_DIM)
            qf, q_vjp = jax.vjp(_qhead, q_ref[:, sl], gq_ref[...])
            kf, k_vjp = jax.vjp(_rms, k_ref[:, sl], gk_ref[...])
            qn, kn, kn_t = qf.astype(BF16), kf.astype(BF16), kf.T.astype(BF16)
            vb = v_ref[:, sl].astype(BF16)
            dob = do_ref[:, sl]
            ckey = _column(c_ref[...], 2 * hp + hh)
            cq = crow_ref[0, hh:hh + 1, :]
            lse_h = lse_ref[0, hh:hh + 1, :]
            dk_acc[...] = jnp.zeros_like(dk_acc)
            dv_acc[...] = jnp.zeros_like(dv_acc)
            dc_acc[...] = jnp.zeros_like(dc_acc)
            dq_blocks = []
            for i in range(nq):
                cols = slice(i * bq, (i + 1) * bq)
                length = (i + 1) * bq
                qi, doi = qn[cols], dob[cols]
                s = _fox_scores(kn[:length], qi, ckey, cq[:, cols], i, bq)
                p = jnp.exp(s - lse_h[:, cols])
                dp = _dot_nt(vb[:length], doi)
                ds = p * (dp - jnp.sum(p * dp, axis=0, keepdims=True))
                dsb = ds.astype(BF16)
                dq_blocks.append(_dot(kn_t[:, :length], dsb).T)
                dk_acc[0:length, :] += _dot(dsb, qi)
                dv_acc[0:length, :] += _dot(p.astype(BF16), doi)
                part = ds[:, 0:LANES]
                for j in range(1, bq // LANES):
                    part = part + ds[:, j * LANES:(j + 1) * LANES]
                dc_acc[0:length, :] += part
            dqp, dgq = q_vjp(jnp.concatenate(dq_blocks, axis=0))
            dkp, dgk = k_vjp(dk_acc[...])
            dgq_ref[...] += dgq
            dgk_ref[...] += dgk
            dqs.append(dqp)
            dks.append(dkp)
            dvs.append(dv_acc[...])
            dc_pair = jnp.where(lane == hh, -jnp.sum(dc_acc[...], axis=1, keepdims=True), dc_pair)
        dq_ref[...] = jnp.concatenate(dqs, axis=1).astype(BF16)
        dk_ref[...] = jnp.concatenate(dks, axis=1).astype(BF16)
        dv_ref[...] = jnp.concatenate(dvs, axis=1).astype(BF16)
        dc_ref[...] = dc_pair

    col = lambda off: (lambda h: (0, off + h))
    fixed = lambda h: (0, 0)
    pair_blk = pl.BlockSpec((t, LANES), col(0))
    return pl.pallas_call(
        body, name=name, grid=(npair,),
        in_specs=[pl.BlockSpec((t, LANES), col(0)), pl.BlockSpec((t, LANES), col(npair)), pl.BlockSpec((t, LANES), col(2 * npair)),
                  pl.BlockSpec((t, LANES), fixed), pl.BlockSpec((1, 2, t), lambda h: (h, 0, 0)),
                  pl.BlockSpec((1, HEAD_DIM), fixed), pl.BlockSpec((1, HEAD_DIM), fixed),
                  pl.BlockSpec((1, 8, t), lambda h: (h, 0, 0)), pair_blk],
        out_specs=[pair_blk, pair_blk, pair_blk, pair_blk,
                   pl.BlockSpec((1, HEAD_DIM), fixed), pl.BlockSpec((1, HEAD_DIM), fixed)],
        out_shape=[_sds((t, hw), BF16), _sds((t, hw), BF16), _sds((t, hw), BF16), _sds((t, npair * LANES), F32),
                   _sds((1, HEAD_DIM), F32), _sds((1, HEAD_DIM), F32)],
        scratch_shapes=[pltpu.VMEM((t, HEAD_DIM), F32), pltpu.VMEM((t, HEAD_DIM), F32), pltpu.VMEM((t, LANES), F32)],
        compiler_params=_params(1),
    )(proj, proj, proj, c, crow, gq, gk, lse, do)


def _t5_bucket_table():
    dist = np.arange(WINDOW)[None, :] + WINDOW - np.arange(2 * WINDOW)[:, None]
    n = np.maximum(dist, 0)
    max_exact = N_BUCKETS // 2
    large = max_exact + (np.log(np.maximum(n, 1) / max_exact) / np.log(REL_MAX_DIST / max_exact)
                         * (N_BUCKETS - max_exact)).astype(np.int32)
    large = np.minimum(large, N_BUCKETS - 1)
    return np.where(n < max_exact, n, large).astype(np.int32).reshape(1, -1)


def _bias_expand(name, rel_bias_t):
    n_heads = rel_bias_t.shape[0]
    tbl = jnp.asarray(_t5_bucket_table())
    width = tbl.shape[1]

    def body(rb_ref, tbl_ref, o_ref):
        onehot = (lax.broadcasted_iota(jnp.int32, (N_BUCKETS, width), 0) == tbl_ref[...]).astype(F32)
        o_ref[...] = _dot_exact(rb_ref[...], onehot)

    return pl.pallas_call(body, name=name, out_shape=_sds((n_heads, width), F32), compiler_params=_params(0))(rel_bias_t, tbl)


def _bias_reduce(name, dbias):
    n_heads, width = dbias.shape
    tbl = jnp.asarray(_t5_bucket_table())

    def body(db_ref, tbl_ref, o_ref):
        onehot = (lax.broadcasted_iota(jnp.int32, (N_BUCKETS, width), 0) == tbl_ref[...]).astype(F32)
        o_ref[...] = lax.dot_general(db_ref[...], onehot, (((1,), (1,)), ((), ())), preferred_element_type=F32,
                                     precision=lax.Precision.HIGHEST)

    return pl.pallas_call(body, name=name, out_shape=_sds((n_heads, N_BUCKETS), F32), compiler_params=_params(0))(dbias, tbl)


def _swa_mask(n, group):
    j = lax.broadcasted_iota(jnp.int32, (2 * WINDOW, group * WINDOW), 0)
    i = lax.broadcasted_iota(jnp.int32, (2 * WINDOW, group * WINDOW), 1) & (WINDOW - 1)
    ok = (j > i) & (j <= i + WINDOW) & ((n > 0) | (j >= WINDOW))
    return jnp.where(ok, 0.0, NEG)


def _swa_stack(ref, start, group):
    return jnp.concatenate([ref[pl.ds(start, WINDOW), g * HEAD_DIM:(g + 1) * HEAD_DIM] for g in range(group)], axis=0)


def _swa_fwd(name, qb, kh, vh, gq, gk, sinks, bias, group):
    t = qb.shape[0]
    kvh = kh.shape[0]
    nblk = t // WINDOW
    gw = group * HEAD_DIM
    band = 2 * WINDOW
    cols = group * WINDOW

    def body(q_ref, k_ref, v_ref, gq_ref, gk_ref, sink_ref, bias_ref, o_ref, lse_ref, qs, kpad, vpad):
        for g in range(group):
            qs[:, g * HEAD_DIM:(g + 1) * HEAD_DIM] = _qhead(q_ref[:, g * HEAD_DIM:(g + 1) * HEAD_DIM], gq_ref[...]).astype(BF16)
        kpad[0:WINDOW, :] = jnp.zeros((WINDOW, HEAD_DIM), BF16)
        vpad[0:WINDOW, :] = jnp.zeros((WINDOW, HEAD_DIM), BF16)
        kpad[WINDOW:, :] = _rms(k_ref[0], gk_ref[...]).astype(BF16)
        vpad[WINDOW:, :] = v_ref[0].astype(BF16)
        sink = sink_ref[0]

        def block(n, carry):
            start = pl.multiple_of(n * WINDOW, WINDOW)
            kb = kpad[pl.ds(start, band), :]
            vb = vpad[pl.ds(start, band), :]
            s = _dot_nt(kb, _swa_stack(qs, start, group)) + bias_ref[0] + _swa_mask(n, group)
            m = jnp.maximum(jnp.max(s, axis=0, keepdims=True), sink)
            e = jnp.exp(s - m)
            l = jnp.sum(e, axis=0, keepdims=True) + jnp.exp(sink - m)
            o_t = _dot_tn(vb, e.astype(BF16)) / l
            for g in range(group):
                o_ref[pl.ds(start, WINDOW), g * HEAD_DIM:(g + 1) * HEAD_DIM] = o_t[:, g * WINDOW:(g + 1) * WINDOW].T.astype(BF16)
            lse_ref[pl.ds(n, 1), :] = m + jnp.log(l)
            return carry

        lax.fori_loop(0, nblk, block, 0)

    fixed = lambda h: (0, 0)
    per = lambda h: (h, 0, 0)
    return pl.pallas_call(
        body, name=name, grid=(kvh,),
        in_specs=[pl.BlockSpec((t, gw), lambda h: (0, h)), pl.BlockSpec((1, t, HEAD_DIM), per), pl.BlockSpec((1, t, HEAD_DIM), per),
                  pl.BlockSpec((1, HEAD_DIM), fixed), pl.BlockSpec((1, HEAD_DIM), fixed),
                  pl.BlockSpec((1, 1, cols), per), pl.BlockSpec((1, band, cols), per)],
        out_specs=[pl.BlockSpec((t, gw), lambda h: (0, h)), pl.BlockSpec((nblk, cols), lambda h: (h, 0))],
        out_shape=[_sds((t, kvh * gw), BF16), _sds((kvh * nblk, cols), F32)],
        scratch_shapes=[pltpu.VMEM((t, gw), BF16), pltpu.VMEM((t + WINDOW, HEAD_DIM), BF16),
                        pltpu.VMEM((t + WINDOW, HEAD_DIM), BF16)],
        compiler_params=_params(1),
    )(qb, kh, vh, gq, gk, sinks, bias)


def _swa_bwd(name, qb, kh, vh, gq, gk, sinks, bias, lse, do, group):
    t = qb.shape[0]
    kvh = kh.shape[0]
    nblk = t // WINDOW
    gw = group * HEAD_DIM
    band = 2 * WINDOW
    cols = group * WINDOW

    def body(q_ref, k_ref, v_ref, gq_ref, gk_ref, sink_ref, bias_ref, lse_ref, do_ref,
             dq_ref, dk_ref, dv_ref, dgq_ref, dgk_ref, dsink_ref, dbias_ref,
             qs, kpad, vpad, dqs, dk_acc, dv_acc, dsink_acc):
        @pl.when(pl.program_id(0) == 0)
        def _():
            dgq_ref[...] = jnp.zeros_like(dgq_ref)
            dgk_ref[...] = jnp.zeros_like(dgk_ref)

        for g in range(group):
            qs[:, g * HEAD_DIM:(g + 1) * HEAD_DIM] = _qhead(q_ref[:, g * HEAD_DIM:(g + 1) * HEAD_DIM], gq_ref[...]).astype(BF16)
        kpad[0:WINDOW, :] = jnp.zeros((WINDOW, HEAD_DIM), BF16)
        vpad[0:WINDOW, :] = jnp.zeros((WINDOW, HEAD_DIM), BF16)
        kpad[WINDOW:, :] = _rms(k_ref[0], gk_ref[...]).astype(BF16)
        vpad[WINDOW:, :] = v_ref[0].astype(BF16)
        dk_acc[...] = jnp.zeros_like(dk_acc)
        dv_acc[...] = jnp.zeros_like(dv_acc)
        dsink_acc[...] = jnp.zeros_like(dsink_acc)
        dbias_ref[...] = jnp.zeros_like(dbias_ref)
        sink = sink_ref[0]

        def block(n, carry):
            start = pl.multiple_of(n * WINDOW, WINDOW)
            kb = kpad[pl.ds(start, band), :]
            vb = vpad[pl.ds(start, band), :]
            q = _swa_stack(qs, start, group)
            dob = _swa_stack(do_ref, start, group)
            lse_n = lse_ref[pl.ds(n, 1), :]
            s = _dot_nt(kb, q) + bias_ref[0] + _swa_mask(n, group)
            p = jnp.exp(s - lse_n)
            dp = _dot_nt(vb, dob)
            dsum = jnp.sum(p * dp, axis=0, keepdims=True)
            ds = p * (dp - dsum)
            dsb = ds.astype(BF16)
            dsink_acc[...] -= jnp.exp(sink - lse_n) * dsum
            dbias_ref[0] += ds
            dq = _dot_tn(dsb, kb)
            for g in range(group):
                dqs[pl.ds(start, WINDOW), g * HEAD_DIM:(g + 1) * HEAD_DIM] = dq[g * WINDOW:(g + 1) * WINDOW]
            dk_acc[pl.ds(start, band), :] += _dot(dsb, q)
            dv_acc[pl.ds(start, band), :] += _dot(p.astype(BF16), dob)
            return carry

        lax.fori_loop(0, nblk, block, 0)
        for g in range(group):
            _, q_vjp = jax.vjp(_qhead, q_ref[:, g * HEAD_DIM:(g + 1) * HEAD_DIM], gq_ref[...])
            dqp, dgq = q_vjp(dqs[:, g * HEAD_DIM:(g + 1) * HEAD_DIM])
            dq_ref[:, g * HEAD_DIM:(g + 1) * HEAD_DIM] = dqp.astype(BF16)
            dgq_ref[...] += dgq
            dsink_g = jnp.sum(dsink_acc[:, g * WINDOW:(g + 1) * WINDOW], axis=1, keepdims=True)
            dsink_ref[0, g:g + 1, :] = jnp.broadcast_to(dsink_g, (1, LANES))
        _, k_vjp = jax.vjp(_rms, k_ref[0], gk_ref[...])
        dkp, dgk = k_vjp(dk_acc[WINDOW:, :])
        dk_ref[0] = dkp
        dgk_ref[...] += dgk
        dv_ref[0] = dv_acc[WINDOW:, :]

    fixed = lambda h: (0, 0)
    per = lambda h: (h, 0, 0)
    wide = pl.BlockSpec((t, gw), lambda h: (0, h))
    head = pl.BlockSpec((1, t, HEAD_DIM), per)
    vec = pl.BlockSpec((1, HEAD_DIM), fixed)
    bias_spec = pl.BlockSpec((1, band, cols), per)
    return pl.pallas_call(
        body, name=name, grid=(kvh,),
        in_specs=[wide, head, head, vec, vec, pl.BlockSpec((1, 1, cols), per), bias_spec,
                  pl.BlockSpec((nblk, cols), lambda h: (h, 0)), wide],
        out_specs=[wide, head, head, vec, vec, pl.BlockSpec((1, group, LANES), per), bias_spec],
        out_shape=[_sds((t, kvh * gw), BF16), _sds((kvh, t, HEAD_DIM), F32), _sds((kvh, t, HEAD_DIM), F32),
                   _sds((1, HEAD_DIM), F32), _sds((1, HEAD_DIM), F32),
                   _sds((kvh, group, LANES), F32), _sds((kvh, band, cols), F32)],
        scratch_shapes=[pltpu.VMEM((t, gw), BF16), pltpu.VMEM((t + WINDOW, HEAD_DIM), BF16),
                        pltpu.VMEM((t + WINDOW, HEAD_DIM), BF16), pltpu.VMEM((t, gw), F32),
                        pltpu.VMEM((t + WINDOW, HEAD_DIM), F32), pltpu.VMEM((t + WINDOW, HEAD_DIM), F32),
                        pltpu.VMEM((1, cols), F32)],
        compiler_params=_params(1),
    )(qb, kh, vh, gq, gk, sinks, bias, lse, do)


def _local_step(x, target, p, comm):
    t, d = x.shape
    n_heads = d // HEAD_DIM
    kv_heads = n_heads // 8
    group = n_heads // kv_heads
    hw = n_heads * HEAD_DIM
    gate_col = 3 * hw // LANES
    kvw = kv_heads * HEAD_DIM
    grads = {}

    def mlp_fwd(tag, h, g, layer, last=False):
        w_up, = comm.weights([f"w_up{layer}"], h)
        a, hn = _norm_matmul(f"{tag}_up", h, g, w_up, relu2=True)
        w_down, = comm.weights([f"w_down{layer}"], a)
        out = _matmul_res_loss(f"{tag}_down", a, w_down, h, target) if last else _matmul_res(f"{tag}_down", a, w_down, h)
        return out, (h, g, hn, a, w_up, w_down)

    def mlp_bwd(tag, saved, layer, dy):
        h, g, hn, a, w_up, w_down = saved
        du = _matmul_nt(f"{tag}_du", dy, w_down, a=a)
        dw_down = _matmul_tn(f"{tag}_dwdown", a, dy)
        dw_up = _matmul_tn(f"{tag}_dwup", hn, du, col_blocks=w_up.shape[0])
        zero = comm.send_grads(tag, {f"w_down{layer}": dw_down, f"w_up{layer}": dw_up})
        return _matmul_nt_rmsbwd(f"{tag}_dh", du, w_up, h, g + zero, dy)

    comm.prefetch(["w_in_a"], None)
    w_in, = comm.weights(["w_in_a"], None)
    proj, xn1 = _norm_matmul("a_inproj", x, p["g_attn"][0], w_in, tn=640, w_rows=True)
    ahead = comm.prefetch(["w_out_a"], proj)
    b_pad = jnp.pad(p["b_f"], ((0, 0), (0, LANES - n_heads))) + ahead[0:1, :]
    c = _gate_fwd("a_gate", proj, b_pad, n_heads, gate_col)
    crow = c[:, :n_heads].T.reshape(n_heads // 2, 2, t)
    o_a, lse_a = _fox_fwd("a_attn", proj, c, crow, p["gq_a"], p["gk_a"], n_heads)
    ahead = comm.prefetch(["w_up0", "w_down0", "w_kv", "w_q_b", "w_out_b"], o_a)
    w_out_a, = comm.weights(["w_out_a"], o_a)
    h1 = _matmul_res("a_outproj", o_a, w_out_a, x, after=ahead)
    h2, mlp0 = mlp_fwd("mlp0", h1, p["g_mlp"][0], 0)

    ahead = comm.prefetch(["w_up1", "w_down1"], h2)
    w_kv, w_q_b = comm.weights(["w_kv", "w_q_b"], h2)
    kv, hn_kv = _norm_matmul("kv_proj", h2, p["g_kv"] + ahead[0, 0], w_kv, tn=2 * kvw)
    kh = kv[:, :kvw].reshape(t, kv_heads, HEAD_DIM).transpose(1, 0, 2)
    vh = kv[:, kvw:].reshape(t, kv_heads, HEAD_DIM).transpose(1, 0, 2)
    qb, hn_q = _norm_matmul("b_qproj", h2, p["g_attn"][1], w_q_b, tn=512)
    gqb, gkb = p["gq_b"], p["gk_b"].reshape(1, HEAD_DIM)
    bias = _bias_expand("b_bias", p["rel_bias"].T).reshape(kv_heads, group, 2 * WINDOW, WINDOW)
    bias = bias.transpose(0, 2, 1, 3).reshape(kv_heads, 2 * WINDOW, group * WINDOW)
    sink_rows = jnp.broadcast_to(p["sinks"].reshape(kv_heads, 1, group, 1), (kv_heads, 1, group, WINDOW)).reshape(kv_heads, 1, group * WINDOW)
    o_b, lse_b = _swa_fwd("b_attn", qb, kh, vh, gqb, gkb, sink_rows, bias, group)
    w_out_b, = comm.weights(["w_out_b"], o_b)
    h3 = _matmul_res("b_outproj", o_b, w_out_b, h2)
    (dy, loss_tile), mlp1 = mlp_fwd("mlp1", h3, p["g_mlp"][1], 1, last=True)

    dh3, dg_mlp1 = mlp_bwd("mlp1", mlp1, 1, dy)
    do_b = _matmul_nt("b_do", dh3, w_out_b)
    dw_out_b = _matmul_tn("b_dwout", o_b, dh3)
    dqb, dkh, dvh, grads["gq_b"], dgk_b, dsink, dbias = _swa_bwd(
        "b_attn_bwd", qb, kh, vh, gqb, gkb, sink_rows, bias, lse_b, do_b, group)
    grads["gk_b"] = dgk_b
    grads["sinks"] = dsink[:, :, 0].reshape(1, n_heads)
    dbias = dbias.reshape(kv_heads, 2 * WINDOW, group, WINDOW).transpose(0, 2, 1, 3)
    grads["rel_bias"] = _bias_reduce("b_dbias", dbias.reshape(n_heads, WINDOW * 2 * WINDOW)).T
    dw_q_b = _matmul_tn("b_dwq", hn_q, dqb)
    dh2, dg_attn1 = _matmul_nt_rmsbwd("b_dhq", dqb, w_q_b, h2, p["g_attn"][1], dh3)
    dkv = jnp.concatenate([dkh.transpose(1, 0, 2).reshape(t, kvw), dvh.transpose(1, 0, 2).reshape(t, kvw)], axis=1)
    dw_kv = _matmul_tn("kv_dw", hn_kv, dkv)
    zero = comm.send_grads("attn_b", {"w_out_b": dw_out_b, "w_q_b": dw_q_b, "w_kv": dw_kv})
    dh2, dg_kv = _matmul_nt_rmsbwd("kv_dh", dkv, w_kv, h2, p["g_kv"] + zero, dh2)
    grads["g_kv"] = dg_kv
    dh1, dg_mlp0 = mlp_bwd("mlp0", mlp0, 0, dh2)
    grads["g_mlp"] = (dg_mlp0, dg_mlp1)

    do_a = _matmul_nt("a_do", dh1, w_out_a)
    dw_out_a = _matmul_tn("a_dwout", o_a, dh1)
    zero = comm.send_grads("attn_a_out", {"w_out_a": dw_out_a})
    dq, dk, dv, dc_cols, grads["gq_a"], grads["gk_a"] = _fox_bwd(
        "a_attn_bwd", proj, c, crow, p["gq_a"] + zero, p["gk_a"], lse_a, do_a, n_heads)
    dc = jnp.pad(dc_cols.reshape(t, n_heads // 2, LANES)[:, :, :2].reshape(t, n_heads), ((0, 0), (0, LANES - n_heads)))
    dfl, db_f = _gate_bwd("a_gate_bwd", proj, b_pad, dc, n_heads, gate_col)
    grads["b_f"] = db_f
    dproj = jnp.concatenate([dq, dk, dv, dfl], axis=1)
    dw_in = _matmul_tn("a_dwin", dproj, xn1, tk=640)
    zero = comm.send_grads("attn_a_in", {"w_in_a": dw_in})
    grad_x, dg_attn0 = _matmul_nt_rmsbwd("a_dx", dproj, w_in, x, p["g_attn"][0] + zero, dh1, w_rows=True)
    grads["g_attn"] = (dg_attn0, dg_attn1)
    return loss_tile, grad_x, grads


EVERYONE = (1, 2, 3, 4, 5, 6, 7)
SAME_CORE = (1, 2, 4, 6)
OTHER_CHIPS = (2, 4, 6)


class _InFlight:
    def __init__(self, scatter, ks, send_sems, recv_sems, srcs, lands, token):
        self.scatter, self.ks, self.send_sems, self.recv_sems = scatter, ks, send_sems, recv_sems
        self.srcs, self.lands, self.token = list(srcs), list(lands), token


def _mesh_peers(ks=EVERYONE):
    x, y, c = lax.axis_index("x"), lax.axis_index("y"), lax.axis_index("c")
    peers = []
    for k in ks:
        px, py, pc = x ^ ((k >> 2) & 1), y ^ ((k >> 1) & 1), c ^ (k & 1)
        peers.append(((px, py, pc), 4 * px + 2 * py + pc))
    return 4 * x + 2 * y + c, peers


_HBM_SPEC = pl.BlockSpec(memory_space=pltpu.HBM)
_SEM_SPEC = pl.BlockSpec(memory_space=pltpu.SEMAPHORE)
_SIDE_EFFECT = pltpu.SideEffectType.DATAFLOW_SIDE_EFFECTING


def _exchange_start(name, arrays, scatter, ks=EVERYONE):
    n = len(arrays)
    me, _ = _mesh_peers()
    lands = []
    for a in arrays:
        own = lax.dynamic_index_in_dim(a, me, 0, keepdims=False) if scatter else a
        shape = a.shape if scatter else (N_DEV,) + a.shape
        lands.append(lax.dynamic_update_index_in_dim(lax.empty(shape, a.dtype), own, me, 0))

    def body(*refs):
        src, land = refs[:n], refs[n:2 * n]
        send_sems, recv_sems, token = refs[2 * n], refs[2 * n + 1], refs[-1]
        pos, peers = _mesh_peers(ks)
        for a in range(n):
            for k, (peer, peer_pos) in enumerate(peers):
                pltpu.make_async_remote_copy(
                    src_ref=src[a].at[peer_pos] if scatter else src[a], dst_ref=land[a].at[pos],
                    send_sem=send_sems.at[a * len(ks) + k], recv_sem=recv_sems.at[a * len(ks) + k],
                    device_id=peer, device_id_type=pl.DeviceIdType.MESH).start()
        token[...] = jnp.zeros_like(token)

    operands = [pltpu.with_memory_space_constraint(a, pltpu.HBM) for a in list(arrays) + lands]
    outs = pl.pallas_call(
        body, name=name,
        out_shape=(pltpu.SemaphoreType.DMA((n * len(ks),)), pltpu.SemaphoreType.DMA((n * len(ks),)),
                   *[pltpu.HBM(a.shape, a.dtype) for a in operands], _sds((8, LANES), F32)),
        in_specs=[_HBM_SPEC] * (2 * n),
        out_specs=(_SEM_SPEC, _SEM_SPEC, *[_HBM_SPEC] * (2 * n), pl.BlockSpec(memory_space=pltpu.VMEM)),
        input_output_aliases={i: 2 + i for i in range(2 * n)},
        compiler_params=pltpu.CompilerParams(has_side_effects=_SIDE_EFFECT),
    )(*operands)
    return _InFlight(scatter, ks, outs[0], outs[1], outs[2:2 + n], outs[2 + n:2 + 2 * n], outs[-1])


def _exchange_wait(name, flight, which, after):
    m = len(which)
    scatter, ks = flight.scatter, flight.ks

    def body(*refs):
        src, land = refs[:m], refs[m:2 * m]
        send_sems, recv_sems = refs[2 * m], refs[2 * m + 1]
        _, peers = _mesh_peers(ks)
        for i, a in enumerate(which):
            for k, (peer, peer_pos) in enumerate(peers):
                cp = pltpu.make_async_remote_copy(
                    src_ref=src[i].at[peer_pos] if scatter else src[i], dst_ref=land[i].at[peer_pos],
                    send_sem=send_sems.at[a * len(ks) + k], recv_sem=recv_sems.at[a * len(ks) + k],
                    device_id=peer, device_id_type=pl.DeviceIdType.MESH)
                cp.wait_send()
                cp.wait_recv()

    operands = [flight.srcs[a] for a in which] + [flight.lands[a] for a in which]
    outs = pl.pallas_call(
        body, name=name, out_shape=tuple(pltpu.HBM(a.shape, a.dtype) for a in operands),
        in_specs=[_HBM_SPEC] * (2 * m) + [_SEM_SPEC, _SEM_SPEC, pl.BlockSpec(memory_space=pl.ANY)],
        out_specs=tuple([_HBM_SPEC] * (2 * m)), input_output_aliases={i: i for i in range(2 * m)},
        compiler_params=pltpu.CompilerParams(has_side_effects=_SIDE_EFFECT),
    )(*operands, flight.send_sems, flight.recv_sems, after)
    return list(outs[m:])


def _relay_start(name, lands):
    n = len(lands)

    def body(*refs):
        land, send_sems, recv_sems, token = refs[:n], refs[n], refs[n + 1], refs[-1]
        _, peers = _mesh_peers(OTHER_CHIPS)
        sibling = (lax.axis_index("x"), lax.axis_index("y"), 1 - lax.axis_index("c"))
        for a in range(n):
            for k, (_, peer_pos) in enumerate(peers):
                pltpu.make_async_remote_copy(
                    src_ref=land[a].at[peer_pos], dst_ref=land[a].at[peer_pos],
                    send_sem=send_sems.at[a * len(peers) + k], recv_sem=recv_sems.at[a * len(peers) + k],
                    device_id=sibling, device_id_type=pl.DeviceIdType.MESH).start()
        token[...] = jnp.zeros_like(token)

    count = n * len(OTHER_CHIPS)
    outs = pl.pallas_call(
        body, name=name,
        out_shape=(pltpu.SemaphoreType.DMA((count,)), pltpu.SemaphoreType.DMA((count,)),
                   *[pltpu.HBM(a.shape, a.dtype) for a in lands], _sds((8, LANES), F32)),
        in_specs=[_HBM_SPEC] * n,
        out_specs=(_SEM_SPEC, _SEM_SPEC, *[_HBM_SPEC] * n, pl.BlockSpec(memory_space=pltpu.VMEM)),
        input_output_aliases={i: 2 + i for i in range(n)},
        compiler_params=pltpu.CompilerParams(has_side_effects=_SIDE_EFFECT),
    )(*[pltpu.with_memory_space_constraint(a, pltpu.HBM) for a in lands])
    return _InFlight(False, OTHER_CHIPS, outs[0], outs[1], [], outs[2:2 + n], outs[-1])


def _relay_wait(name, flight, which, after):
    m = len(which)

    def body(*refs):
        land, send_sems, recv_sems = refs[:m], refs[m], refs[m + 1]
        _, peers = _mesh_peers(OTHER_CHIPS)
        sibling = (lax.axis_index("x"), lax.axis_index("y"), 1 - lax.axis_index("c"))
        for i, a in enumerate(which):
            for k, (_, peer_pos) in enumerate(peers):
                cp = pltpu.make_async_remote_copy(
                    src_ref=land[i].at[peer_pos], dst_ref=land[i].at[peer_pos ^ 1],
                    send_sem=send_sems.at[a * len(peers) + k], recv_sem=recv_sems.at[a * len(peers) + k],
                    device_id=sibling, device_id_type=pl.DeviceIdType.MESH)
                cp.wait_send()
                cp.wait_recv()

    operands = [flight.lands[a] for a in which]
    outs = pl.pallas_call(
        body, name=name, out_shape=tuple(pltpu.HBM(a.shape, a.dtype) for a in operands),
        in_specs=[_HBM_SPEC] * m + [_SEM_SPEC, _SEM_SPEC, pl.BlockSpec(memory_space=pl.ANY)],
        out_specs=tuple([_HBM_SPEC] * m), input_output_aliases={i: i for i in range(m)},
        compiler_params=pltpu.CompilerParams(has_side_effects=_SIDE_EFFECT),
    )(*operands, flight.send_sems, flight.recv_sems, after)
    return list(outs)


def _sum_parts(p_ref):
    g = p_ref[0].astype(F32)
    for dev in range(1, N_DEV):
        g = g + p_ref[dev].astype(F32)
    return g


def _adam_update(g, w, m, v):
    m_new = ADAM_B1 * m + (1.0 - ADAM_B1) * g
    v_new = ADAM_B2 * v + (1.0 - ADAM_B2) * jnp.square(g)
    m_hat = m_new / (1.0 - ADAM_B1 ** ADAM_STEP)
    v_hat = v_new / (1.0 - ADAM_B2 ** ADAM_STEP)
    return -ADAM_LR * (m_hat / (jnp.sqrt(v_hat) + ADAM_EPS) + ADAM_WD * w), m_new, v_new


def _adamw(name, parts, w, m, v, layer=None, into=None):
    r, c = w.shape[-2:]
    tr = 256 if r % 256 == 0 else r
    n_into = 0 if into is None else len(into)

    def body(p_ref, w_ref, m_ref, v_ref, *refs):
        g_ref, d_ref, mo_ref, vo_ref = refs[n_into:]
        g = _sum_parts(p_ref)
        g_ref[...] = g
        d_ref[...], mo_ref[...], vo_ref[...] = _adam_update(g, w_ref[...], m_ref[...], v_ref[...])

    if layer is None:
        blk = pl.BlockSpec((tr, c), lambda i: (i, 0))
    else:
        blk = pl.BlockSpec((None, tr, c), lambda i: (layer, i, 0))
    return pl.pallas_call(
        body, name=name, grid=(r // tr,),
        in_specs=[pl.BlockSpec((N_DEV, tr, c), lambda i: (0, i, 0)), blk, blk, blk] + [pl.BlockSpec(memory_space=pl.ANY)] * n_into,
        out_specs=[blk] * 4, out_shape=[_sds(w.shape, F32)] * 4,
        input_output_aliases={4 + i: i for i in range(n_into)}, compiler_params=_params(1),
    )(parts, w, m, v, *(into or ()))


SMALL_PACK_ROWS = 16
LOSS_ROW = 11


def _small_rows(grads, loss_tile):
    return [(0, 1, grads["g_attn"][0]), (1, 1, grads["g_attn"][1]), (2, 1, grads["g_mlp"][0]), (3, 1, grads["g_mlp"][1]),
            (4, 1, grads["g_kv"]), (5, 1, grads["b_f"]), (6, 1, grads["gq_a"]), (7, 1, grads["gk_a"]), (8, 1, grads["gk_b"]),
            (9, 1, grads["gq_b"]), (10, 1, grads["sinks"]), (LOSS_ROW, 1, loss_tile)]


SMALL_ROWS = {"g_attn": (0, 2), "g_mlp": (2, 2), "g_kv": (4, 1), "b_f": (5, 1), "gq_a": (6, 1), "gk_a": (7, 1),
              "gk_b": (8, 1), "gq_b": (9, 1), "sinks": (10, 1)}


def _pack_small(name, pieces, d):
    def body(*refs):
        out = refs[-1]
        out[...] = jnp.zeros_like(out)
        for (row, rows, _), ref in zip(pieces, refs[:-1]):
            out[row:row + rows, 0:ref.shape[1]] = ref[0:rows, :]

    return pl.pallas_call(body, name=name, out_shape=_sds((SMALL_PACK_ROWS, d), F32), compiler_params=_params(0))(
        *[piece for _, _, piece in pieces])


def _adamw_small(name, parts, parts_rel_bias, w, m, v):
    def body(*refs):
        ins, outs = refs[2:2 + 3 * len(SMALL)], refs[2 + 3 * len(SMALL):]
        pack, rel = _sum_parts(refs[0]), _sum_parts(refs[1])
        for i, k in enumerate(SMALL):
            w_ref, m_ref, v_ref = ins[3 * i:3 * i + 3]
            if k == "rel_bias":
                g = rel
            else:
                row, rows = SMALL_ROWS[k]
                g = pack[row:row + rows, 0:w_ref.shape[1]]
            outs[4 * i][...] = g
            outs[4 * i + 1][...], outs[4 * i + 2][...], outs[4 * i + 3][...] = _adam_update(g, w_ref[...], m_ref[...], v_ref[...])
        outs[-1][...] = pack[LOSS_ROW:LOSS_ROW + 1, 0:LANES]

    operands = [parts, parts_rel_bias] + [t[k] for k in SMALL for t in (w, m, v)]
    out_shape = [_sds(w[k].shape, F32) for k in SMALL for _ in range(4)] + [_sds((1, LANES), F32)]
    outs = pl.pallas_call(body, name=name, out_shape=out_shape, compiler_params=_params(0))(*operands)
    return {k: outs[4 * i:4 * i + 4] for i, k in enumerate(SMALL)}, outs[-1]


class _Comm:
    ORDER = ("w_in_a", "w_out_a", "w_up0", "w_down0", "w_kv", "w_q_b", "w_out_b", "w_up1", "w_down1")

    def __init__(self, shards, d, n_in):
        self.d, self.n_in = d, n_in
        self.flight = _exchange_start("gather_start", [shards[n].astype(BF16) for n in self.ORDER], scatter=False, ks=SAME_CORE)
        self.relays, self.sent = {}, []

    def prefetch(self, names, after):
        which = [self.ORDER.index(n) for n in names]
        landed = _exchange_wait(f"gather_wait_{names[0]}", self.flight, which, self.flight.token if after is None else after)
        relay = _relay_start(f"gather_relay_{names[0]}", landed)
        for n in names:
            self.relays[n] = (relay, names)
        return relay.token

    def weights(self, names, after):
        relay, group = self.relays[names[0]]
        landed = _relay_wait(f"gather_relay_wait_{names[0]}", relay, [group.index(n) for n in names],
                             relay.token if after is None else after)
        return [self._whole(n, g) for n, g in zip(names, landed)]

    def _whole(self, name, g):
        if name == "w_in_a":
            return _join_row_blocks("w_in_join", g, -(-self.n_in // LANES) * LANES)
        if name.startswith("w_up"):
            return g
        return g.reshape(-1, g.shape[-1])

    def _chunks(self, name, g):
        if name == "w_in_a":
            return _split_row_blocks("dw_in_split", g, N_DEV, self.n_in // N_DEV)
        if name.startswith("w_up"):
            return g
        return g.reshape(N_DEV, g.shape[0] // N_DEV, g.shape[1])

    def send_grads(self, tag, partials):
        names = list(partials)
        flight = _exchange_start(f"scatter_start_{tag}", [self._chunks(n, partials[n]) for n in names], scatter=True)
        self.sent.append((tag, flight, names))
        return flight.token[0, 0]

    def received(self, index, after):
        tag, flight, names = self.sent[index]
        landed = _exchange_wait(f"scatter_wait_{tag}", flight, list(range(len(names))), after)
        return dict(zip(names, landed))


def kernel(x, g_attn, g_mlp, w_in_a, b_f, gq_a, gk_a, w_out_a, g_kv, w_kv, gk_b, w_q_b, gq_b, sinks, rel_bias, w_out_b, w_up, w_down, loss_target, m_g_attn, m_g_mlp, m_w_in_a, m_b_f, m_gq_a, m_gk_a, m_w_out_a, m_g_kv, m_w_kv, m_gk_b, m_w_q_b, m_gq_b, m_sinks, m_rel_bias, m_w_out_b, m_w_up, m_w_down, v_g_attn, v_g_mlp, v_w_in_a, v_b_f, v_gq_a, v_gk_a, v_w_out_a, v_g_kv, v_w_kv, v_gk_b, v_w_q_b, v_gq_b, v_sinks, v_rel_bias, v_w_out_b, v_w_up, v_w_down):
    w = dict(g_attn=g_attn, g_mlp=g_mlp, w_in_a=w_in_a, b_f=b_f, gq_a=gq_a, gk_a=gk_a, w_out_a=w_out_a, g_kv=g_kv,
             w_kv=w_kv, gk_b=gk_b, w_q_b=w_q_b, gq_b=gq_b, sinks=sinks, rel_bias=rel_bias, w_out_b=w_out_b,
             w_up=w_up, w_down=w_down)
    mom = dict(g_attn=m_g_attn, g_mlp=m_g_mlp, w_in_a=m_w_in_a, b_f=m_b_f, gq_a=m_gq_a, gk_a=m_gk_a, w_out_a=m_w_out_a,
               g_kv=m_g_kv, w_kv=m_w_kv, gk_b=m_gk_b, w_q_b=m_w_q_b, gq_b=m_gq_b, sinks=m_sinks, rel_bias=m_rel_bias,
               w_out_b=m_w_out_b, w_up=m_w_up, w_down=m_w_down)
    var = dict(g_attn=v_g_attn, g_mlp=v_g_mlp, w_in_a=v_w_in_a, b_f=v_b_f, gq_a=v_gq_a, gk_a=v_gk_a, w_out_a=v_w_out_a,
               g_kv=v_g_kv, w_kv=v_w_kv, gk_b=v_gk_b, w_q_b=v_w_q_b, gq_b=v_gq_b, sinks=v_sinks, rel_bias=v_rel_bias,
               w_out_b=v_w_out_b, w_up=v_w_up, w_down=v_w_down)
    d = x.shape[2]
    where = {"w_in_a": ("w_in_a", 0), "w_out_a": ("w_out_a", 0), "w_kv": ("w_kv", None), "w_q_b": ("w_q_b", 0),
             "w_out_b": ("w_out_b", 0), "w_up0": ("w_up", 0), "w_up1": ("w_up", 1), "w_down0": ("w_down", 0),
             "w_down1": ("w_down", 1)}
    flip = lambda tree: {**tree, "w_in_a": jnp.swapaxes(tree["w_in_a"], 1, 2)}
    w, mom, var = flip(w), flip(mom), flip(var)
    shards = {n: (w[k] if layer is None else w[k][layer]) for n, (k, layer) in where.items()}
    comm = _Comm(shards, d, w_in_a.shape[2] * N_DEV)
    loss_tile, grad_x, grads = _local_step(x[0], loss_target[0], {k: w[k] for k in SMALL}, comm)

    small_flight = _exchange_start(
        "gather_small_grads", [_pack_small("pack_small", _small_rows(grads, loss_tile), d), grads["rel_bias"]], scatter=False)
    res, after = {}, small_flight.token
    for index in range(len(comm.sent)):
        for n, parts in comm.received(index, after).items():
            k, layer = where[n]
            res[k] = _adamw(f"adam_{n}", parts, w[k], mom[k], var[k], layer, res.get(k))
            after = res[k][0]
    as_rows = lambda tree: {k: tree[k] if tree[k].ndim == 2 else tree[k].reshape(1, -1) for k in SMALL}
    small, loss_row = _adamw_small("adam_small", *_exchange_wait("gather_small_wait", small_flight, [0, 1], after),
                                   as_rows(w), as_rows(mom), as_rows(var))
    loss = loss_row[0, 0]
    for k in SMALL:
        res[k] = [a.reshape(w[k].shape) for a in small[k]]
    res["w_in_a"] = [jnp.swapaxes(a, 1, 2) for a in res["w_in_a"]]

    outs = [loss, grad_x[None]]
    for i in range(4):
        outs.extend(res[k][i] for k in WEIGHTS)
    return tuple(outs)
```

```python
import numpy as np
import jax
import jax.numpy as jnp
from jax import lax
from jax.experimental import pallas as pl
from jax.experimental.pallas import tpu as pltpu

F32 = jnp.float32
BF16 = jnp.bfloat16

N_DEV = 8
HEAD_DIM = 64
WINDOW = 128
N_BUCKETS = 32
REL_MAX_DIST = 128
NORM_EPS = 1e-6
NEG = -1e30
LANES = 128
VMEM_LIMIT = 56 * 1024 * 1024

ADAM_LR = 0.001
ADAM_B1 = 0.9
ADAM_B2 = 0.999
ADAM_EPS = 1e-08
ADAM_WD = 0.01
ADAM_STEP = 10

SMALL = ("g_attn", "g_mlp", "b_f", "gq_a", "gk_a", "g_kv", "gk_b", "gq_b", "sinks", "rel_bias")
WEIGHTS = ("g_attn", "g_mlp", "w_in_a", "b_f", "gq_a", "gk_a", "w_out_a", "g_kv", "w_kv", "gk_b",
           "w_q_b", "gq_b", "sinks", "rel_bias", "w_out_b", "w_up", "w_down")


def _params(n_grid):
    return pltpu.CompilerParams(dimension_semantics=("arbitrary",) * n_grid, vmem_limit_bytes=VMEM_LIMIT)


def _sds(shape, dtype):
    return jax.ShapeDtypeStruct(tuple(shape), dtype)


def _rms(x, g):
    return (x * lax.rsqrt(jnp.mean(x * x, axis=-1, keepdims=True) + NORM_EPS)) * g


def _dot_nt(a, b):
    return lax.dot_general(a, b, (((1,), (1,)), ((), ())), preferred_element_type=F32)


def _dot_tn(a, b):
    return lax.dot_general(a, b, (((0,), (0,)), ((), ())), preferred_element_type=F32)


def _dot(a, b):
    return jnp.dot(a, b, preferred_element_type=F32)


def _dot_exact(a, b):
    return jnp.dot(a, b, preferred_element_type=F32, precision=lax.Precision.HIGHEST)


def _norm_matmul(name, x, g, w, *, tn=None, relu2=False, w_rows=False):
    t, d = x.shape
    blocked = w.ndim == 3
    if blocked:
        tn = w.shape[2]
        n = w.shape[0] * tn
        w_spec = pl.BlockSpec((None, d, tn), lambda i, j: (j, 0, 0))
    elif w_rows:
        n = w.shape[0]
        w_spec = pl.BlockSpec((tn, d), lambda i, j: (j, 0))
    else:
        n = w.shape[1]
        w_spec = pl.BlockSpec((d, tn), lambda i, j: (0, j))
    tm = min(1024, t)

    def body(x_ref, g_ref, w_ref, y_ref, xn_ref):
        @pl.when(pl.program_id(1) == 0)
        def _():
            xn_ref[...] = _rms(x_ref[...], g_ref[...]).astype(BF16)

        y = _dot_nt(xn_ref[...], w_ref[...]) if w_rows else _dot(xn_ref[...], w_ref[...])
        y_ref[...] = jnp.square(jnp.maximum(y, 0.0)).astype(BF16) if relu2 else y

    out_shape = [_sds((t, n), BF16 if relu2 else F32), _sds((t, d), BF16)]
    out_specs = [pl.BlockSpec((tm, tn), lambda i, j: (i, j)), pl.BlockSpec((tm, d), lambda i, j: (i, 0))]
    return pl.pallas_call(
        body, name=name, grid=(t // tm, n // tn),
        in_specs=[pl.BlockSpec((tm, d), lambda i, j: (i, 0)), pl.BlockSpec((1, d), lambda i, j: (0, 0)), w_spec],
        out_specs=out_specs, out_shape=out_shape, compiler_params=_params(2),
    )(x, g.reshape(1, d), w)


def _matmul_res(name, a, w, res, *, tn=512, after=None):
    t, k = a.shape
    n = w.shape[1]
    tm = min(1024, t)

    def body(a_ref, w_ref, r_ref, *rest):
        rest[-1][...] = r_ref[...] + _dot(a_ref[...], w_ref[...])

    extra = [] if after is None else [after]
    return pl.pallas_call(
        body, name=name, grid=(t // tm, n // tn),
        in_specs=[pl.BlockSpec((tm, k), lambda i, j: (i, 0)), pl.BlockSpec((k, tn), lambda i, j: (0, j)),
                  pl.BlockSpec((tm, tn), lambda i, j: (i, j))] + [pl.BlockSpec((8, LANES), lambda i, j: (0, 0))] * len(extra),
        out_specs=pl.BlockSpec((tm, tn), lambda i, j: (i, j)), out_shape=_sds((t, n), F32),
        compiler_params=_params(2),
    )(a, w, res, *extra)


def _matmul_nt(name, dy, w, *, a=None, tk=1024):
    t, n = dy.shape
    k = w.shape[0]
    tm = min(1024, t)

    def body(dy_ref, w_ref, *rest):
        o_ref = rest[-1]
        r = _dot_nt(dy_ref[...].astype(BF16), w_ref[...])
        if a is not None:
            r = r * (2.0 * jnp.sqrt(rest[0][...].astype(F32)))
        o_ref[...] = r.astype(BF16)

    in_specs = [pl.BlockSpec((tm, n), lambda i, j: (i, 0)), pl.BlockSpec((tk, n), lambda i, j: (j, 0))]
    args = [dy, w]
    if a is not None:
        in_specs.append(pl.BlockSpec((tm, tk), lambda i, j: (i, j)))
        args.append(a)
    return pl.pallas_call(
        body, name=name, grid=(t // tm, k // tk), in_specs=in_specs,
        out_specs=pl.BlockSpec((tm, tk), lambda i, j: (i, j)), out_shape=_sds((t, k), BF16),
        compiler_params=_params(2),
    )(*args)


def _matmul_nt_rmsbwd(name, dy, w, x, g, dres, *, w_rows=False):
    t, k = dy.shape
    blocked = w.ndim == 3
    d = w.shape[1] if blocked or w_rows else w.shape[0]
    tm = min(512, t)

    def body(dy_ref, w_ref, x_ref, g_ref, r_ref, dx_ref, dg_ref):
        if blocked:
            kb = w.shape[2]
            dxn = _dot_nt(dy_ref[:, 0:kb].astype(BF16), w_ref[0])
            for j in range(1, w.shape[0]):
                dxn += _dot_nt(dy_ref[:, j * kb:(j + 1) * kb].astype(BF16), w_ref[j])
        elif w_rows:
            dxn = _dot(dy_ref[...].astype(BF16), w_ref[...])
        else:
            dxn = _dot_nt(dy_ref[...].astype(BF16), w_ref[...])
        _, vjp = jax.vjp(_rms, x_ref[...], g_ref[...])
        dx, dg = vjp(dxn)
        dx_ref[...] = r_ref[...] + dx

        @pl.when(pl.program_id(0) == 0)
        def _():
            dg_ref[...] = jnp.zeros_like(dg_ref)

        dg_ref[...] += dg

    row = lambda i: (i, 0)
    fixed = lambda i: (0, 0)
    return pl.pallas_call(
        body, name=name, grid=(t // tm,),
        in_specs=[pl.BlockSpec((tm, k), row), pl.BlockSpec(w.shape, (lambda i: (0, 0, 0)) if blocked else fixed),
                  pl.BlockSpec((tm, d), row), pl.BlockSpec((1, d), fixed), pl.BlockSpec((tm, d), row)],
        out_specs=[pl.BlockSpec((tm, d), row), pl.BlockSpec((1, d), fixed)],
        out_shape=[_sds((t, d), F32), _sds((1, d), F32)], compiler_params=_params(1),
    )(dy, w, x, g.reshape(1, d), dres)


def _matmul_tn(name, a, b, *, tk=1024, tn=1024, col_blocks=None):
    t, k = a.shape
    n = b.shape[1]
    tk = min(tk, k)
    if col_blocks:
        tn = n // col_blocks
        out_spec, out_shape = pl.BlockSpec((None, tk, tn), lambda i, j: (j, i, 0)), _sds((col_blocks, k, tn), BF16)
    else:
        tn = min(tn, n)
        out_spec, out_shape = pl.BlockSpec((tk, tn), lambda i, j: (i, j)), _sds((k, n), BF16)

    def body(a_ref, b_ref, o_ref):
        o_ref[...] = _dot_tn(a_ref[...].astype(BF16), b_ref[...].astype(BF16)).astype(BF16)

    return pl.pallas_call(
        body, name=name, grid=(k // tk, n // tn),
        in_specs=[pl.BlockSpec((t, tk), lambda i, j: (0, i)), pl.BlockSpec((t, tn), lambda i, j: (0, j))],
        out_specs=out_spec, out_shape=out_shape, compiler_params=_params(2),
    )(a, b)


def _join_row_blocks(name, blocks, rows):
    b, r, c = blocks.shape
    tc = min(256, c)

    def body(g_ref, o_ref):
        o_ref[...] = jnp.zeros_like(o_ref)
        for j in range(b):
            o_ref[r * j:r * (j + 1), :] = g_ref[j]

    return pl.pallas_call(
        body, name=name, grid=(c // tc,), in_specs=[pl.BlockSpec((b, r, tc), lambda i: (0, 0, i))],
        out_specs=pl.BlockSpec((rows, tc), lambda i: (0, i)), out_shape=_sds((rows, c), blocks.dtype),
        compiler_params=_params(1),
    )(blocks)


def _split_row_blocks(name, mat, b, r):
    rows, c = mat.shape
    tc = min(256, c)

    def body(w_ref, o_ref):
        for j in range(b):
            o_ref[j] = w_ref[r * j:r * (j + 1), :]

    return pl.pallas_call(
        body, name=name, grid=(c // tc,), in_specs=[pl.BlockSpec((rows, tc), lambda i: (0, i))],
        out_specs=pl.BlockSpec((b, r, tc), lambda i: (0, 0, i)), out_shape=_sds((b, r, c), mat.dtype),
        compiler_params=_params(1),
    )(mat)


def _matmul_res_loss(name, a, w, res, target, *, tn=512):
    t, k = a.shape
    n = w.shape[1]
    tm = min(1024, t)

    def body(a_ref, w_ref, r_ref, t_ref, dy_ref, l_ref):
        e = r_ref[...] + _dot(a_ref[...], w_ref[...]) - t_ref[...]
        dy_ref[...] = e * (1.0 / n)

        @pl.when((pl.program_id(0) == 0) & (pl.program_id(1) == 0))
        def _():
            l_ref[...] = jnp.zeros_like(l_ref)

        l_ref[...] += (0.5 / n) * jnp.sum(e * e)

    tile = pl.BlockSpec((tm, tn), lambda i, j: (i, j))
    return pl.pallas_call(
        body, name=name, grid=(t // tm, n // tn),
        in_specs=[pl.BlockSpec((tm, k), lambda i, j: (i, 0)), pl.BlockSpec((k, tn), lambda i, j: (0, j)), tile, tile],
        out_specs=[tile, pl.BlockSpec((8, LANES), lambda i, j: (0, 0))],
        out_shape=[_sds((t, n), F32), _sds((8, LANES), F32)], compiler_params=_params(2),
    )(a, w, res, target)


def _gate_fwd(name, proj, b_pad, n_heads, gate_col):
    t = proj.shape[0]
    tb = min(256, t)
    tri = jnp.asarray(np.tril(np.ones((tb, tb), np.float32)))

    def body(p_ref, b_ref, tri_ref, c_ref, carry):
        @pl.when(pl.program_id(0) == 0)
        def _():
            carry[...] = jnp.zeros_like(carry)

        lane = lax.broadcasted_iota(jnp.int32, (tb, LANES), 1)
        lf = jnp.where(lane < n_heads, jax.nn.log_sigmoid(p_ref[...] + b_ref[...]), 0.0)
        c = _dot_exact(tri_ref[...], lf) + carry[0:1, :]
        c_ref[...] = c
        carry[...] = jnp.broadcast_to(c[tb - 1:tb, :], carry.shape)

    return pl.pallas_call(
        body, name=name, grid=(t // tb,),
        in_specs=[pl.BlockSpec((tb, LANES), lambda i: (i, gate_col)), pl.BlockSpec((1, LANES), lambda i: (0, 0)),
                  pl.BlockSpec((tb, tb), lambda i: (0, 0))],
        out_specs=pl.BlockSpec((tb, LANES), lambda i: (i, 0)), out_shape=_sds((t, LANES), F32),
        scratch_shapes=[pltpu.VMEM((8, LANES), F32)], compiler_params=_params(1),
    )(proj, b_pad, tri)


def _gate_bwd(name, proj, b_pad, dc, n_heads, gate_col):
    t = proj.shape[0]
    tb = min(256, t)
    nb = t // tb
    triu = jnp.asarray(np.triu(np.ones((tb, tb), np.float32)))

    def body(p_ref, b_ref, dc_ref, tri_ref, df_ref, db_ref, carry):
        @pl.when(pl.program_id(0) == 0)
        def _():
            carry[...] = jnp.zeros_like(carry)
            db_ref[...] = jnp.zeros_like(db_ref)

        dcv = dc_ref[...]
        dlf = _dot_exact(tri_ref[...], dcv) + carry[0:1, :]
        carry[...] = jnp.broadcast_to(dlf[0:1, :], carry.shape)
        lane = lax.broadcasted_iota(jnp.int32, (tb, LANES), 1)
        z = p_ref[...] + b_ref[...]
        df = jnp.where(lane < n_heads, dlf / (1.0 + jnp.exp(z)), 0.0)
        df_ref[...] = df.astype(BF16)
        db_ref[...] += jnp.sum(df, axis=0, keepdims=True)

    return pl.pallas_call(
        body, name=name, grid=(nb,),
        in_specs=[pl.BlockSpec((tb, LANES), lambda i: (nb - 1 - i, gate_col)), pl.BlockSpec((1, LANES), lambda i: (0, 0)),
                  pl.BlockSpec((tb, LANES), lambda i: (nb - 1 - i, 0)), pl.BlockSpec((tb, tb), lambda i: (0, 0))],
        out_specs=[pl.BlockSpec((tb, LANES), lambda i: (nb - 1 - i, 0)), pl.BlockSpec((1, LANES), lambda i: (0, 0))],
        out_shape=[_sds((t, LANES), BF16), _sds((1, LANES), F32)],
        scratch_shapes=[pltpu.VMEM((8, LANES), F32)], compiler_params=_params(1),
    )(proj, b_pad, dc, triu)


def _qhead(qp, g):
    return _rms(qp, g) * (HEAD_DIM ** -0.5)


def _column(mat, idx):
    lane = lax.broadcasted_iota(jnp.int32, mat.shape, 1)
    return jnp.sum(jnp.where(lane == idx, mat, 0.0), axis=1, keepdims=True)


def _fox_scores(kk, qi, ckey, cq_i, i, bq):
    length = kk.shape[0]
    s = _dot_nt(kk, qi) + cq_i - ckey[:length]
    key = lax.broadcasted_iota(jnp.int32, (length, bq), 0)
    qry = lax.broadcasted_iota(jnp.int32, (length, bq), 1) + i * bq
    return jnp.where(key <= qry, s, NEG)


def _fox_fwd(name, proj, c, crow, gq, gk, n_heads):
    t = proj.shape[0]
    hw = n_heads * HEAD_DIM
    npair = n_heads // 2
    bq = min(256, t)
    nq = t // bq

    def body(q_ref, k_ref, v_ref, c_ref, crow_ref, gq_ref, gk_ref, o_ref, lse_ref):
        hp = pl.program_id(0)
        lse_ref[...] = jnp.zeros_like(lse_ref)
        outs = []
        for hh in range(2):
            sl = slice(hh * HEAD_DIM, (hh + 1) * HEAD_DIM)
            qn = _qhead(q_ref[:, sl], gq_ref[...]).astype(BF16)
            kn = _rms(k_ref[:, sl], gk_ref[...]).astype(BF16)
            v_t = v_ref[:, sl].T.astype(BF16)
            ckey = _column(c_ref[...], 2 * hp + hh)
            cq = crow_ref[0, hh:hh + 1, :]
            o_blocks = []
            for i in range(nq):
                cols = slice(i * bq, (i + 1) * bq)
                length = (i + 1) * bq
                s = _fox_scores(kn[:length], qn[cols], ckey, cq[:, cols], i, bq)
                m = jnp.max(s, axis=0, keepdims=True)
                p = jnp.exp(s - m)
                l = jnp.sum(p, axis=0, keepdims=True)
                o_blocks.append((_dot(v_t[:, :length], p.astype(BF16)) / l).T)
                lse_ref[0, hh:hh + 1, cols] = m + jnp.log(l)
            outs.append(jnp.concatenate(o_blocks, axis=0))
        o_ref[...] = jnp.concatenate(outs, axis=1).astype(BF16)

    col = lambda off: (lambda h: (0, off + h))
    fixed = lambda h: (0, 0)
    return pl.pallas_call(
        body, name=name, grid=(npair,),
        in_specs=[pl.BlockSpec((t, LANES), col(0)), pl.BlockSpec((t, LANES), col(npair)), pl.BlockSpec((t, LANES), col(2 * npair)),
                  pl.BlockSpec((t, LANES), fixed), pl.BlockSpec((1, 2, t), lambda h: (h, 0, 0)),
                  pl.BlockSpec((1, HEAD_DIM), fixed), pl.BlockSpec((1, HEAD_DIM), fixed)],
        out_specs=[pl.BlockSpec((t, LANES), col(0)), pl.BlockSpec((1, 8, t), lambda h: (h, 0, 0))],
        out_shape=[_sds((t, hw), BF16), _sds((npair, 8, t), F32)], compiler_params=_params(1),
    )(proj, proj, proj, c, crow, gq, gk)


def _fox_bwd(name, proj, c, crow, gq, gk, lse, do, n_heads):
    t = proj.shape[0]
    hw = n_heads * HEAD_DIM
    npair = n_heads // 2
    bq = min(256, t)
    nq = t // bq

    def body(q_ref, k_ref, v_ref, c_ref, crow_ref, gq_ref, gk_ref, lse_ref, do_ref,
             dq_ref, dk_ref, dv_ref, dc_ref, dgq_ref, dgk_ref, dk_acc, dv_acc, dc_acc):
        hp = pl.program_id(0)

        @pl.when(hp == 0)
        def _():
            dgq_ref[...] = jnp.zeros_like(dgq_ref)
            dgk_ref[...] = jnp.zeros_like(dgk_ref)
            dc_ref[...] = jnp.zeros_like(dc_ref)

        lane = lax.broadcasted_iota(jnp.int32, (t, LANES), 1)
        dqs, dks, dvs = [], [], []
        for hh in range(2):
            sl = slice(hh * HEAD_DIM, (hh + 1) * HEAD_DIM)
            qf, q_vjp = jax.vjp(_qhead, q_ref[:, sl], gq_ref[...])
            kf, k_vjp = jax.vjp(_rms, k_ref[:, sl], gk_ref[...])
            qn, kn, kn_t = qf.astype(BF16), kf.astype(BF16), kf.T.astype(BF16)
            vb = v_ref[:, sl].astype(BF16)
            dob = do_ref[:, sl]
            ckey = _column(c_ref[...], 2 * hp + hh)
            cq = crow_ref[0, hh:hh + 1, :]
            lse_h = lse_ref[0, hh:hh + 1, :]
            dk_acc[...] = jnp.zeros_like(dk_acc)
            dv_acc[...] = jnp.zeros_like(dv_acc)
            dc_acc[...] = jnp.zeros_like(dc_acc)
            dq_blocks = []
            for i in range(nq):
                cols = slice(i * bq, (i + 1) * bq)
                length = (i + 1) * bq
                qi, doi = qn[cols], dob[cols]
                s = _fox_scores(kn[:length], qi, ckey, cq[:, cols], i, bq)
                p = jnp.exp(s - lse_h[:, cols])
                dp = _dot_nt(vb[:length], doi)
                ds = p * (dp - jnp.sum(p * dp, axis=0, keepdims=True))
                dsb = ds.astype(BF16)
                dq_blocks.append(_dot(kn_t[:, :length], dsb).T)
                dk_acc[0:length, :] += _dot(dsb, qi)
                dv_acc[0:length, :] += _dot(p.astype(BF16), doi)
                part = ds[:, 0:LANES]
                for j in range(1, bq // LANES):
                    part = part + ds[:, j * LANES:(j + 1) * LANES]
                dc_acc[0:length, :] += part
            dqp, dgq = q_vjp(jnp.concatenate(dq_blocks, axis=0))
            dkp, dgk = k_vjp(dk_acc[...])
            dgq_ref[...] += dgq
            dgk_ref[...] += dgk
            dqs.append(dqp)
            dks.append(dkp)
            dvs.append(dv_acc[...])
            dc_ref[...] = jnp.where(lane == 2 * hp + hh, -jnp.sum(dc_acc[...], axis=1, keepdims=True), dc_ref[...])
        dq_ref[...] = jnp.concatenate(dqs, axis=1).astype(BF16)
        dk_ref[...] = jnp.concatenate(dks, axis=1).astype(BF16)
        dv_ref[...] = jnp.concatenate(dvs, axis=1).astype(BF16)

    col = lambda off: (lambda h: (0, off + h))
    fixed = lambda h: (0, 0)
    pair_blk = pl.BlockSpec((t, LANES), col(0))
    return pl.pallas_call(
        body, name=name, grid=(npair,),
        in_specs=[pl.BlockSpec((t, LANES), col(0)), pl.BlockSpec((t, LANES), col(npair)), pl.BlockSpec((t, LANES), col(2 * npair)),
                  pl.BlockSpec((t, LANES), fixed), pl.BlockSpec((1, 2, t), lambda h: (h, 0, 0)),
                  pl.BlockSpec((1, HEAD_DIM), fixed), pl.BlockSpec((1, HEAD_DIM), fixed),
                  pl.BlockSpec((1, 8, t), lambda h: (h, 0, 0)), pair_blk],
        out_specs=[pair_blk, pair_blk, pair_blk, pl.BlockSpec((t, LANES), fixed),
                   pl.BlockSpec((1, HEAD_DIM), fixed), pl.BlockSpec((1, HEAD_DIM), fixed)],
        out_shape=[_sds((t, hw), BF16), _sds((t, hw), BF16), _sds((t, hw), BF16), _sds((t, LANES), F32),
                   _sds((1, HEAD_DIM), F32), _sds((1, HEAD_DIM), F32)],
        scratch_shapes=[pltpu.VMEM((t, HEAD_DIM), F32), pltpu.VMEM((t, HEAD_DIM), F32), pltpu.VMEM((t, LANES), F32)],
        compiler_params=_params(1),
    )(proj, proj, proj, c, crow, gq, gk, lse, do)


def _t5_bucket_table():
    dist = np.arange(WINDOW)[None, :] + WINDOW - np.arange(2 * WINDOW)[:, None]
    n = np.maximum(dist, 0)
    max_exact = N_BUCKETS // 2
    large = max_exact + (np.log(np.maximum(n, 1) / max_exact) / np.log(REL_MAX_DIST / max_exact)
                         * (N_BUCKETS - max_exact)).astype(np.int32)
    large = np.minimum(large, N_BUCKETS - 1)
    return np.where(n < max_exact, n, large).astype(np.int32).reshape(1, -1)


def _bias_expand(name, rel_bias_t):
    n_heads = rel_bias_t.shape[0]
    tbl = jnp.asarray(_t5_bucket_table())
    width = tbl.shape[1]

    def body(rb_ref, tbl_ref, o_ref):
        onehot = (lax.broadcasted_iota(jnp.int32, (N_BUCKETS, width), 0) == tbl_ref[...]).astype(F32)
        o_ref[...] = _dot_exact(rb_ref[...], onehot)

    return pl.pallas_call(body, name=name, out_shape=_sds((n_heads, width), F32), compiler_params=_params(0))(rel_bias_t, tbl)


def _bias_reduce(name, dbias):
    n_heads, width = dbias.shape
    tbl = jnp.asarray(_t5_bucket_table())

    def body(db_ref, tbl_ref, o_ref):
        onehot = (lax.broadcasted_iota(jnp.int32, (N_BUCKETS, width), 0) == tbl_ref[...]).astype(F32)
        o_ref[...] = lax.dot_general(db_ref[...], onehot, (((1,), (1,)), ((), ())), preferred_element_type=F32,
                                     precision=lax.Precision.HIGHEST)

    return pl.pallas_call(body, name=name, out_shape=_sds((n_heads, N_BUCKETS), F32), compiler_params=_params(0))(dbias, tbl)


def _swa_mask(n, group):
    j = lax.broadcasted_iota(jnp.int32, (2 * WINDOW, group * WINDOW), 0)
    i = lax.broadcasted_iota(jnp.int32, (2 * WINDOW, group * WINDOW), 1) & (WINDOW - 1)
    ok = (j > i) & (j <= i + WINDOW) & ((n > 0) | (j >= WINDOW))
    return jnp.where(ok, 0.0, NEG)


def _swa_stack(ref, start, group):
    return jnp.concatenate([ref[pl.ds(start, WINDOW), g * HEAD_DIM:(g + 1) * HEAD_DIM] for g in range(group)], axis=0)


def _swa_fwd(name, qb, kh, vh, gq, gk, sinks, bias, group):
    t = qb.shape[0]
    kvh = kh.shape[0]
    nblk = t // WINDOW
    gw = group * HEAD_DIM
    band = 2 * WINDOW
    cols = group * WINDOW

    def body(q_ref, k_ref, v_ref, gq_ref, gk_ref, sink_ref, bias_ref, o_ref, lse_ref, qs, kpad, vpad):
        for g in range(group):
            qs[:, g * HEAD_DIM:(g + 1) * HEAD_DIM] = _qhead(q_ref[:, g * HEAD_DIM:(g + 1) * HEAD_DIM], gq_ref[...]).astype(BF16)
        kpad[0:WINDOW, :] = jnp.zeros((WINDOW, HEAD_DIM), BF16)
        vpad[0:WINDOW, :] = jnp.zeros((WINDOW, HEAD_DIM), BF16)
        kpad[WINDOW:, :] = _rms(k_ref[0], gk_ref[...]).astype(BF16)
        vpad[WINDOW:, :] = v_ref[0].astype(BF16)
        sink = sink_ref[0]

        def block(n, carry):
            start = pl.multiple_of(n * WINDOW, WINDOW)
            kb = kpad[pl.ds(start, band), :]
            vb = vpad[pl.ds(start, band), :]
            s = _dot_nt(kb, _swa_stack(qs, start, group)) + bias_ref[0] + _swa_mask(n, group)
            m = jnp.maximum(jnp.max(s, axis=0, keepdims=True), sink)
            e = jnp.exp(s - m)
            l = jnp.sum(e, axis=0, keepdims=True) + jnp.exp(sink - m)
            o_t = _dot_tn(vb, e.astype(BF16)) / l
            for g in range(group):
                o_ref[pl.ds(start, WINDOW), g * HEAD_DIM:(g + 1) * HEAD_DIM] = o_t[:, g * WINDOW:(g + 1) * WINDOW].T.astype(BF16)
            lse_ref[pl.ds(n, 1), :] = m + jnp.log(l)
            return carry

        lax.fori_loop(0, nblk, block, 0)

    fixed = lambda h: (0, 0)
    per = lambda h: (h, 0, 0)
    return pl.pallas_call(
        body, name=name, grid=(kvh,),
        in_specs=[pl.BlockSpec((t, gw), lambda h: (0, h)), pl.BlockSpec((1, t, HEAD_DIM), per), pl.BlockSpec((1, t, HEAD_DIM), per),
                  pl.BlockSpec((1, HEAD_DIM), fixed), pl.BlockSpec((1, HEAD_DIM), fixed),
                  pl.BlockSpec((1, 1, cols), per), pl.BlockSpec((1, band, cols), per)],
        out_specs=[pl.BlockSpec((t, gw), lambda h: (0, h)), pl.BlockSpec((nblk, cols), lambda h: (h, 0))],
        out_shape=[_sds((t, kvh * gw), BF16), _sds((kvh * nblk, cols), F32)],
        scratch_shapes=[pltpu.VMEM((t, gw), BF16), pltpu.VMEM((t + WINDOW, HEAD_DIM), BF16),
                        pltpu.VMEM((t + WINDOW, HEAD_DIM), BF16)],
        compiler_params=_params(1),
    )(qb, kh, vh, gq, gk, sinks, bias)


def _swa_bwd(name, qb, kh, vh, gq, gk, sinks, bias, lse, do, group):
    t = qb.shape[0]
    kvh = kh.shape[0]
    nblk = t // WINDOW
    gw = group * HEAD_DIM
    band = 2 * WINDOW
    cols = group * WINDOW

    def body(q_ref, k_ref, v_ref, gq_ref, gk_ref, sink_ref, bias_ref, lse_ref, do_ref,
             dq_ref, dk_ref, dv_ref, dgq_ref, dgk_ref, dsink_ref, dbias_ref,
             qs, kpad, vpad, dqs, dk_acc, dv_acc, dsink_acc):
        @pl.when(pl.program_id(0) == 0)
        def _():
            dgq_ref[...] = jnp.zeros_like(dgq_ref)
            dgk_ref[...] = jnp.zeros_like(dgk_ref)

        for g in range(group):
            qs[:, g * HEAD_DIM:(g + 1) * HEAD_DIM] = _qhead(q_ref[:, g * HEAD_DIM:(g + 1) * HEAD_DIM], gq_ref[...]).astype(BF16)
        kpad[0:WINDOW, :] = jnp.zeros((WINDOW, HEAD_DIM), BF16)
        vpad[0:WINDOW, :] = jnp.zeros((WINDOW, HEAD_DIM), BF16)
        kpad[WINDOW:, :] = _rms(k_ref[0], gk_ref[...]).astype(BF16)
        vpad[WINDOW:, :] = v_ref[0].astype(BF16)
        dk_acc[...] = jnp.zeros_like(dk_acc)
        dv_acc[...] = jnp.zeros_like(dv_acc)
        dsink_acc[...] = jnp.zeros_like(dsink_acc)
        dbias_ref[...] = jnp.zeros_like(dbias_ref)
        sink = sink_ref[0]

        def block(n, carry):
            start = pl.multiple_of(n * WINDOW, WINDOW)
            kb = kpad[pl.ds(start, band), :]
            vb = vpad[pl.ds(start, band), :]
            q = _swa_stack(qs, start, group)
            dob = _swa_stack(do_ref, start, group)
            lse_n = lse_ref[pl.ds(n, 1), :]
            s = _dot_nt(kb, q) + bias_ref[0] + _swa_mask(n, group)
            p = jnp.exp(s - lse_n)
            dp = _dot_nt(vb, dob)
            dsum = jnp.sum(p * dp, axis=0, keepdims=True)
            ds = p * (dp - dsum)
            dsb = ds.astype(BF16)
            dsink_acc[...] -= jnp.exp(sink - lse_n) * dsum
            dbias_ref[0] += ds
            dq = _dot_tn(dsb, kb)
            for g in range(group):
                dqs[pl.ds(start, WINDOW), g * HEAD_DIM:(g + 1) * HEAD_DIM] = dq[g * WINDOW:(g + 1) * WINDOW]
            dk_acc[pl.ds(start, band), :] += _dot(dsb, q)
            dv_acc[pl.ds(start, band), :] += _dot(p.astype(BF16), dob)
            return carry

        lax.fori_loop(0, nblk, block, 0)
        for g in range(group):
            _, q_vjp = jax.vjp(_qhead, q_ref[:, g * HEAD_DIM:(g + 1) * HEAD_DIM], gq_ref[...])
            dqp, dgq = q_vjp(dqs[:, g * HEAD_DIM:(g + 1) * HEAD_DIM])
            dq_ref[:, g * HEAD_DIM:(g + 1) * HEAD_DIM] = dqp.astype(BF16)
            dgq_ref[...] += dgq
            dsink_g = jnp.sum(dsink_acc[:, g * WINDOW:(g + 1) * WINDOW], axis=1, keepdims=True)
            dsink_ref[0, g:g + 1, :] = jnp.broadcast_to(dsink_g, (1, LANES))
        _, k_vjp = jax.vjp(_rms, k_ref[0], gk_ref[...])
        dkp, dgk = k_vjp(dk_acc[WINDOW:, :])
        dk_ref[0] = dkp
        dgk_ref[...] += dgk
        dv_ref[0] = dv_acc[WINDOW:, :]

    fixed = lambda h: (0, 0)
    per = lambda h: (h, 0, 0)
    wide = pl.BlockSpec((t, gw), lambda h: (0, h))
    head = pl.BlockSpec((1, t, HEAD_DIM), per)
    vec = pl.BlockSpec((1, HEAD_DIM), fixed)
    bias_spec = pl.BlockSpec((1, band, cols), per)
    return pl.pallas_call(
        body, name=name, grid=(kvh,),
        in_specs=[wide, head, head, vec, vec, pl.BlockSpec((1, 1, cols), per), bias_spec,
                  pl.BlockSpec((nblk, cols), lambda h: (h, 0)), wide],
        out_specs=[wide, head, head, vec, vec, pl.BlockSpec((1, group, LANES), per), bias_spec],
        out_shape=[_sds((t, kvh * gw), BF16), _sds((kvh, t, HEAD_DIM), F32), _sds((kvh, t, HEAD_DIM), F32),
                   _sds((1, HEAD_DIM), F32), _sds((1, HEAD_DIM), F32),
                   _sds((kvh, group, LANES), F32), _sds((kvh, band, cols), F32)],
        scratch_shapes=[pltpu.VMEM((t, gw), BF16), pltpu.VMEM((t + WINDOW, HEAD_DIM), BF16),
                        pltpu.VMEM((t + WINDOW, HEAD_DIM), BF16), pltpu.VMEM((t, gw), F32),
                        pltpu.VMEM((t + WINDOW, HEAD_DIM), F32), pltpu.VMEM((t + WINDOW, HEAD_DIM), F32),
                        pltpu.VMEM((1, cols), F32)],
        compiler_params=_params(1),
    )(qb, kh, vh, gq, gk, sinks, bias, lse, do)


def _local_step(x, target, p, comm):
    t, d = x.shape
    n_heads = d // HEAD_DIM
    kv_heads = n_heads // 8
    group = n_heads // kv_heads
    hw = n_heads * HEAD_DIM
    gate_col = 3 * hw // LANES
    kvw = kv_heads * HEAD_DIM
    grads = {}

    def mlp_fwd(tag, h, g, layer, last=False):
        w_up, = comm.weights([f"w_up{layer}"], h)
        a, hn = _norm_matmul(f"{tag}_up", h, g, w_up, relu2=True)
        w_down, = comm.weights([f"w_down{layer}"], a)
        out = _matmul_res_loss(f"{tag}_down", a, w_down, h, target) if last else _matmul_res(f"{tag}_down", a, w_down, h)
        return out, (h, g, hn, a, w_up, w_down)

    def mlp_bwd(tag, saved, layer, dy):
        h, g, hn, a, w_up, w_down = saved
        du = _matmul_nt(f"{tag}_du", dy, w_down, a=a)
        dw_down = _matmul_tn(f"{tag}_dwdown", a, dy)
        dw_up = _matmul_tn(f"{tag}_dwup", hn, du, col_blocks=w_up.shape[0])
        zero = comm.send_grads(tag, {f"w_down{layer}": dw_down, f"w_up{layer}": dw_up})
        return _matmul_nt_rmsbwd(f"{tag}_dh", du, w_up, h, g + zero, dy)

    comm.prefetch(["w_in_a"], None)
    w_in, = comm.weights(["w_in_a"], None)
    proj, xn1 = _norm_matmul("a_inproj", x, p["g_attn"][0], w_in, tn=640, w_rows=True)
    ahead = comm.prefetch(["w_out_a"], proj)
    b_pad = jnp.pad(p["b_f"], ((0, 0), (0, LANES - n_heads))) + ahead[0:1, :]
    c = _gate_fwd("a_gate", proj, b_pad, n_heads, gate_col)
    crow = c[:, :n_heads].T.reshape(n_heads // 2, 2, t)
    o_a, lse_a = _fox_fwd("a_attn", proj, c, crow, p["gq_a"], p["gk_a"], n_heads)
    ahead = comm.prefetch(["w_up0", "w_down0", "w_kv", "w_q_b", "w_out_b"], o_a)
    w_out_a, = comm.weights(["w_out_a"], o_a)
    h1 = _matmul_res("a_outproj", o_a, w_out_a, x, after=ahead)
    h2, mlp0 = mlp_fwd("mlp0", h1, p["g_mlp"][0], 0)

    ahead = comm.prefetch(["w_up1", "w_down1"], h2)
    w_kv, w_q_b = comm.weights(["w_kv", "w_q_b"], h2)
    kv, hn_kv = _norm_matmul("kv_proj", h2, p["g_kv"] + ahead[0, 0], w_kv, tn=2 * kvw)
    kh = kv[:, :kvw].reshape(t, kv_heads, HEAD_DIM).transpose(1, 0, 2)
    vh = kv[:, kvw:].reshape(t, kv_heads, HEAD_DIM).transpose(1, 0, 2)
    qb, hn_q = _norm_matmul("b_qproj", h2, p["g_attn"][1], w_q_b, tn=512)
    gqb, gkb = p["gq_b"], p["gk_b"].reshape(1, HEAD_DIM)
    bias = _bias_expand("b_bias", p["rel_bias"].T).reshape(kv_heads, group, 2 * WINDOW, WINDOW)
    bias = bias.transpose(0, 2, 1, 3).reshape(kv_heads, 2 * WINDOW, group * WINDOW)
    sink_rows = jnp.broadcast_to(p["sinks"].reshape(kv_heads, 1, group, 1), (kv_heads, 1, group, WINDOW)).reshape(kv_heads, 1, group * WINDOW)
    o_b, lse_b = _swa_fwd("b_attn", qb, kh, vh, gqb, gkb, sink_rows, bias, group)
    w_out_b, = comm.weights(["w_out_b"], o_b)
    h3 = _matmul_res("b_outproj", o_b, w_out_b, h2)
    (dy, loss_tile), mlp1 = mlp_fwd("mlp1", h3, p["g_mlp"][1], 1, last=True)

    dh3, dg_mlp1 = mlp_bwd("mlp1", mlp1, 1, dy)
    do_b = _matmul_nt("b_do", dh3, w_out_b)
    dw_out_b = _matmul_tn("b_dwout", o_b, dh3)
    dqb, dkh, dvh, grads["gq_b"], dgk_b, dsink, dbias = _swa_bwd(
        "b_attn_bwd", qb, kh, vh, gqb, gkb, sink_rows, bias, lse_b, do_b, group)
    grads["gk_b"] = dgk_b
    grads["sinks"] = dsink[:, :, 0].reshape(1, n_heads)
    dbias = dbias.reshape(kv_heads, 2 * WINDOW, group, WINDOW).transpose(0, 2, 1, 3)
    grads["rel_bias"] = _bias_reduce("b_dbias", dbias.reshape(n_heads, WINDOW * 2 * WINDOW)).T
    dw_q_b = _matmul_tn("b_dwq", hn_q, dqb)
    dh2, dg_attn1 = _matmul_nt_rmsbwd("b_dhq", dqb, w_q_b, h2, p["g_attn"][1], dh3)
    dkv = jnp.concatenate([dkh.transpose(1, 0, 2).reshape(t, kvw), dvh.transpose(1, 0, 2).reshape(t, kvw)], axis=1)
    dw_kv = _matmul_tn("kv_dw", hn_kv, dkv)
    zero = comm.send_grads("attn_b", {"w_out_b": dw_out_b, "w_q_b": dw_q_b, "w_kv": dw_kv})
    dh2, dg_kv = _matmul_nt_rmsbwd("kv_dh", dkv, w_kv, h2, p["g_kv"] + zero, dh2)
    grads["g_kv"] = dg_kv
    dh1, dg_mlp0 = mlp_bwd("mlp0", mlp0, 0, dh2)
    grads["g_mlp"] = (dg_mlp0, dg_mlp1)

    do_a = _matmul_nt("a_do", dh1, w_out_a)
    dw_out_a = _matmul_tn("a_dwout", o_a, dh1)
    zero = comm.send_grads("attn_a_out", {"w_out_a": dw_out_a})
    dq, dk, dv, dc, grads["gq_a"], grads["gk_a"] = _fox_bwd(
        "a_attn_bwd", proj, c, crow, p["gq_a"] + zero, p["gk_a"], lse_a, do_a, n_heads)
    dfl, db_f = _gate_bwd("a_gate_bwd", proj, b_pad, dc, n_heads, gate_col)
    grads["b_f"] = db_f
    dproj = jnp.concatenate([dq, dk, dv, dfl], axis=1)
    dw_in = _matmul_tn("a_dwin", dproj, xn1, tk=640)
    zero = comm.send_grads("attn_a_in", {"w_in_a": dw_in})
    grad_x, dg_attn0 = _matmul_nt_rmsbwd("a_dx", dproj, w_in, x, p["g_attn"][0] + zero, dh1, w_rows=True)
    grads["g_attn"] = (dg_attn0, dg_attn1)
    return loss_tile, grad_x, grads


EVERYONE = (1, 2, 3, 4, 5, 6, 7)
SAME_CORE = (1, 2, 4, 6)
OTHER_CHIPS = (2, 4, 6)


class _InFlight:
    def __init__(self, scatter, ks, send_sems, recv_sems, srcs, lands, token):
        self.scatter, self.ks, self.send_sems, self.recv_sems = scatter, ks, send_sems, recv_sems
        self.srcs, self.lands, self.token = list(srcs), list(lands), token


def _mesh_peers(ks=EVERYONE):
    x, y, c = lax.axis_index("x"), lax.axis_index("y"), lax.axis_index("c")
    peers = []
    for k in ks:
        px, py, pc = x ^ ((k >> 2) & 1), y ^ ((k >> 1) & 1), c ^ (k & 1)
        peers.append(((px, py, pc), 4 * px + 2 * py + pc))
    return 4 * x + 2 * y + c, peers


_HBM_SPEC = pl.BlockSpec(memory_space=pltpu.HBM)
_SEM_SPEC = pl.BlockSpec(memory_space=pltpu.SEMAPHORE)
_SIDE_EFFECT = pltpu.SideEffectType.DATAFLOW_SIDE_EFFECTING


def _exchange_start(name, arrays, scatter, ks=EVERYONE):
    n = len(arrays)
    me, _ = _mesh_peers()
    lands = []
    for a in arrays:
        own = lax.dynamic_index_in_dim(a, me, 0, keepdims=False) if scatter else a
        shape = a.shape if scatter else (N_DEV,) + a.shape
        lands.append(lax.dynamic_update_index_in_dim(lax.empty(shape, a.dtype), own, me, 0))

    def body(*refs):
        src, land = refs[:n], refs[n:2 * n]
        send_sems, recv_sems, token = refs[2 * n], refs[2 * n + 1], refs[-1]
        pos, peers = _mesh_peers(ks)
        for a in range(n):
            for k, (peer, peer_pos) in enumerate(peers):
                pltpu.make_async_remote_copy(
                    src_ref=src[a].at[peer_pos] if scatter else src[a], dst_ref=land[a].at[pos],
                    send_sem=send_sems.at[a * len(ks) + k], recv_sem=recv_sems.at[a * len(ks) + k],
                    device_id=peer, device_id_type=pl.DeviceIdType.MESH).start()
        token[...] = jnp.zeros_like(token)

    operands = [pltpu.with_memory_space_constraint(a, pltpu.HBM) for a in list(arrays) + lands]
    outs = pl.pallas_call(
        body, name=name,
        out_shape=(pltpu.SemaphoreType.DMA((n * len(ks),)), pltpu.SemaphoreType.DMA((n * len(ks),)),
                   *[pltpu.HBM(a.shape, a.dtype) for a in operands], _sds((8, LANES), F32)),
        in_specs=[_HBM_SPEC] * (2 * n),
        out_specs=(_SEM_SPEC, _SEM_SPEC, *[_HBM_SPEC] * (2 * n), pl.BlockSpec(memory_space=pltpu.VMEM)),
        input_output_aliases={i: 2 + i for i in range(2 * n)},
        compiler_params=pltpu.CompilerParams(has_side_effects=_SIDE_EFFECT),
    )(*operands)
    return _InFlight(scatter, ks, outs[0], outs[1], outs[2:2 + n], outs[2 + n:2 + 2 * n], outs[-1])


def _exchange_wait(name, flight, which, after):
    m = len(which)
    scatter, ks = flight.scatter, flight.ks

    def body(*refs):
        src, land = refs[:m], refs[m:2 * m]
        send_sems, recv_sems = refs[2 * m], refs[2 * m + 1]
        _, peers = _mesh_peers(ks)
        for i, a in enumerate(which):
            for k, (peer, peer_pos) in enumerate(peers):
                cp = pltpu.make_async_remote_copy(
                    src_ref=src[i].at[peer_pos] if scatter else src[i], dst_ref=land[i].at[peer_pos],
                    send_sem=send_sems.at[a * len(ks) + k], recv_sem=recv_sems.at[a * len(ks) + k],
                    device_id=peer, device_id_type=pl.DeviceIdType.MESH)
                cp.wait_send()
                cp.wait_recv()

    operands = [flight.srcs[a] for a in which] + [flight.lands[a] for a in which]
    outs = pl.pallas_call(
        body, name=name, out_shape=tuple(pltpu.HBM(a.shape, a.dtype) for a in operands),
        in_specs=[_HBM_SPEC] * (2 * m) + [_SEM_SPEC, _SEM_SPEC, pl.BlockSpec(memory_space=pl.ANY)],
        out_specs=tuple([_HBM_SPEC] * (2 * m)), input_output_aliases={i: i for i in range(2 * m)},
        compiler_params=pltpu.CompilerParams(has_side_effects=_SIDE_EFFECT),
    )(*operands, flight.send_sems, flight.recv_sems, after)
    return list(outs[m:])


def _relay_start(name, lands):
    n = len(lands)

    def body(*refs):
        land, send_sems, recv_sems, token = refs[:n], refs[n], refs[n + 1], refs[-1]
        _, peers = _mesh_peers(OTHER_CHIPS)
        sibling = (lax.axis_index("x"), lax.axis_index("y"), 1 - lax.axis_index("c"))
        for a in range(n):
            for k, (_, peer_pos) in enumerate(peers):
                pltpu.make_async_remote_copy(
                    src_ref=land[a].at[peer_pos], dst_ref=land[a].at[peer_pos],
                    send_sem=send_sems.at[a * len(peers) + k], recv_sem=recv_sems.at[a * len(peers) + k],
                    device_id=sibling, device_id_type=pl.DeviceIdType.MESH).start()
        token[...] = jnp.zeros_like(token)

    count = n * len(OTHER_CHIPS)
    outs = pl.pallas_call(
        body, name=name,
        out_shape=(pltpu.SemaphoreType.DMA((count,)), pltpu.SemaphoreType.DMA((count,)),
                   *[pltpu.HBM(a.shape, a.dtype) for a in lands], _sds((8, LANES), F32)),
        in_specs=[_HBM_SPEC] * n,
        out_specs=(_SEM_SPEC, _SEM_SPEC, *[_HBM_SPEC] * n, pl.BlockSpec(memory_space=pltpu.VMEM)),
        input_output_aliases={i: 2 + i for i in range(n)},
        compiler_params=pltpu.CompilerParams(has_side_effects=_SIDE_EFFECT),
    )(*[pltpu.with_memory_space_constraint(a, pltpu.HBM) for a in lands])
    return _InFlight(False, OTHER_CHIPS, outs[0], outs[1], [], outs[2:2 + n], outs[-1])


def _relay_wait(name, flight, which, after):
    m = len(which)

    def body(*refs):
        land, send_sems, recv_sems = refs[:m], refs[m], refs[m + 1]
        _, peers = _mesh_peers(OTHER_CHIPS)
        sibling = (lax.axis_index("x"), lax.axis_index("y"), 1 - lax.axis_index("c"))
        for i, a in enumerate(which):
            for k, (_, peer_pos) in enumerate(peers):
                cp = pltpu.make_async_remote_copy(
                    src_ref=land[i].at[peer_pos], dst_ref=land[i].at[peer_pos ^ 1],
                    send_sem=send_sems.at[a * len(peers) + k], recv_sem=recv_sems.at[a * len(peers) + k],
                    device_id=sibling, device_id_type=pl.DeviceIdType.MESH)
                cp.wait_send()
                cp.wait_recv()

    operands = [flight.lands[a] for a in which]
    outs = pl.pallas_call(
        body, name=name, out_shape=tuple(pltpu.HBM(a.shape, a.dtype) for a in operands),
        in_specs=[_HBM_SPEC] * m + [_SEM_SPEC, _SEM_SPEC, pl.BlockSpec(memory_space=pl.ANY)],
        out_specs=tuple([_HBM_SPEC] * m), input_output_aliases={i: i for i in range(m)},
        compiler_params=pltpu.CompilerParams(has_side_effects=_SIDE_EFFECT),
    )(*operands, flight.send_sems, flight.recv_sems, after)
    return list(outs)


def _sum_parts(p_ref):
    g = p_ref[0].astype(F32)
    for dev in range(1, N_DEV):
        g = g + p_ref[dev].astype(F32)
    return g


def _adam_update(g, w, m, v):
    m_new = ADAM_B1 * m + (1.0 - ADAM_B1) * g
    v_new = ADAM_B2 * v + (1.0 - ADAM_B2) * jnp.square(g)
    m_hat = m_new / (1.0 - ADAM_B1 ** ADAM_STEP)
    v_hat = v_new / (1.0 - ADAM_B2 ** ADAM_STEP)
    return -ADAM_LR * (m_hat / (jnp.sqrt(v_hat) + ADAM_EPS) + ADAM_WD * w), m_new, v_new


def _adamw(name, parts, w, m, v, layer=None, into=None):
    r, c = w.shape[-2:]
    tr = 256 if r % 256 == 0 else r
    n_into = 0 if into is None else len(into)

    def body(p_ref, w_ref, m_ref, v_ref, *refs):
        g_ref, d_ref, mo_ref, vo_ref = refs[n_into:]
        g = _sum_parts(p_ref)
        g_ref[...] = g
        d_ref[...], mo_ref[...], vo_ref[...] = _adam_update(g, w_ref[...], m_ref[...], v_ref[...])

    if layer is None:
        blk = pl.BlockSpec((tr, c), lambda i: (i, 0))
    else:
        blk = pl.BlockSpec((None, tr, c), lambda i: (layer, i, 0))
    return pl.pallas_call(
        body, name=name, grid=(r // tr,),
        in_specs=[pl.BlockSpec((N_DEV, tr, c), lambda i: (0, i, 0)), blk, blk, blk] + [pl.BlockSpec(memory_space=pl.ANY)] * n_into,
        out_specs=[blk] * 4, out_shape=[_sds(w.shape, F32)] * 4,
        input_output_aliases={4 + i: i for i in range(n_into)}, compiler_params=_params(1),
    )(parts, w, m, v, *(into or ()))


SMALL_PACK_ROWS = 16
LOSS_ROW = 11


def _small_rows(grads, loss_tile):
    return [(0, 1, grads["g_attn"][0]), (1, 1, grads["g_attn"][1]), (2, 1, grads["g_mlp"][0]), (3, 1, grads["g_mlp"][1]),
            (4, 1, grads["g_kv"]), (5, 1, grads["b_f"]), (6, 1, grads["gq_a"]), (7, 1, grads["gk_a"]), (8, 1, grads["gk_b"]),
            (9, 1, grads["gq_b"]), (10, 1, grads["sinks"]), (LOSS_ROW, 1, loss_tile)]


SMALL_ROWS = {"g_attn": (0, 2), "g_mlp": (2, 2), "g_kv": (4, 1), "b_f": (5, 1), "gq_a": (6, 1), "gk_a": (7, 1),
              "gk_b": (8, 1), "gq_b": (9, 1), "sinks": (10, 1)}


def _pack_small(name, pieces, d):
    def body(*refs):
        out = refs[-1]
        out[...] = jnp.zeros_like(out)
        for (row, rows, _), ref in zip(pieces, refs[:-1]):
            out[row:row + rows, 0:ref.shape[1]] = ref[0:rows, :]

    return pl.pallas_call(body, name=name, out_shape=_sds((SMALL_PACK_ROWS, d), F32), compiler_params=_params(0))(
        *[piece for _, _, piece in pieces])


def _adamw_small(name, parts, parts_rel_bias, w, m, v):
    def body(*refs):
        ins, outs = refs[2:2 + 3 * len(SMALL)], refs[2 + 3 * len(SMALL):]
        pack, rel = _sum_parts(refs[0]), _sum_parts(refs[1])
        for i, k in enumerate(SMALL):
            w_ref, m_ref, v_ref = ins[3 * i:3 * i + 3]
            if k == "rel_bias":
                g = rel
            else:
                row, rows = SMALL_ROWS[k]
                g = pack[row:row + rows, 0:w_ref.shape[1]]
            outs[4 * i][...] = g
            outs[4 * i + 1][...], outs[4 * i + 2][...], outs[4 * i + 3][...] = _adam_update(g, w_ref[...], m_ref[...], v_ref[...])
        outs[-1][...] = pack[LOSS_ROW:LOSS_ROW + 1, 0:LANES]

    operands = [parts, parts_rel_bias] + [t[k] for k in SMALL for t in (w, m, v)]
    out_shape = [_sds(w[k].shape, F32) for k in SMALL for _ in range(4)] + [_sds((1, LANES), F32)]
    outs = pl.pallas_call(body, name=name, out_shape=out_shape, compiler_params=_params(0))(*operands)
    return {k: outs[4 * i:4 * i + 4] for i, k in enumerate(SMALL)}, outs[-1]


class _Comm:
    ORDER = ("w_in_a", "w_out_a", "w_up0", "w_down0", "w_kv", "w_q_b", "w_out_b", "w_up1", "w_down1")

    def __init__(self, shards, d, n_in):
        self.d, self.n_in = d, n_in
        self.flight = _exchange_start("gather_start", [shards[n].astype(BF16) for n in self.ORDER], scatter=False, ks=SAME_CORE)
        self.relays, self.sent = {}, []

    def prefetch(self, names, after):
        which = [self.ORDER.index(n) for n in names]
        landed = _exchange_wait(f"gather_wait_{names[0]}", self.flight, which, self.flight.token if after is None else after)
        relay = _relay_start(f"gather_relay_{names[0]}", landed)
        for n in names:
            self.relays[n] = (relay, names)
        return relay.token

    def weights(self, names, after):
        relay, group = self.relays[names[0]]
        landed = _relay_wait(f"gather_relay_wait_{names[0]}", relay, [group.index(n) for n in names],
                             relay.token if after is None else after)
        return [self._whole(n, g) for n, g in zip(names, landed)]

    def _whole(self, name, g):
        if name == "w_in_a":
            return _join_row_blocks("w_in_join", g, -(-self.n_in // LANES) * LANES)
        if name.startswith("w_up"):
            return g
        return g.reshape(-1, g.shape[-1])

    def _chunks(self, name, g):
        if name == "w_in_a":
            return _split_row_blocks("dw_in_split", g, N_DEV, self.n_in // N_DEV)
        if name.startswith("w_up"):
            return g
        return g.reshape(N_DEV, g.shape[0] // N_DEV, g.shape[1])

    def send_grads(self, tag, partials):
        names = list(partials)
        flight = _exchange_start(f"scatter_start_{tag}", [self._chunks(n, partials[n]) for n in names], scatter=True)
        self.sent.append((tag, flight, names))
        return flight.token[0, 0]

    def received(self, index, after):
        tag, flight, names = self.sent[index]
        landed = _exchange_wait(f"scatter_wait_{tag}", flight, list(range(len(names))), after)
        return dict(zip(names, landed))


def kernel(x, g_attn, g_mlp, w_in_a, b_f, gq_a, gk_a, w_out_a, g_kv, w_kv, gk_b, w_q_b, gq_b, sinks, rel_bias, w_out_b, w_up, w_down, loss_target, m_g_attn, m_g_mlp, m_w_in_a, m_b_f, m_gq_a, m_gk_a, m_w_out_a, m_g_kv, m_w_kv, m_gk_b, m_w_q_b, m_gq_b, m_sinks, m_rel_bias, m_w_out_b, m_w_up, m_w_down, v_g_attn, v_g_mlp, v_w_in_a, v_b_f, v_gq_a, v_gk_a, v_w_out_a, v_g_kv, v_w_kv, v_gk_b, v_w_q_b, v_gq_b, v_sinks, v_rel_bias, v_w_out_b, v_w_up, v_w_down):
    w = dict(g_attn=g_attn, g_mlp=g_mlp, w_in_a=w_in_a, b_f=b_f, gq_a=gq_a, gk_a=gk_a, w_out_a=w_out_a, g_kv=g_kv,
             w_kv=w_kv, gk_b=gk_b, w_q_b=w_q_b, gq_b=gq_b, sinks=sinks, rel_bias=rel_bias, w_out_b=w_out_b,
             w_up=w_up, w_down=w_down)
    mom = dict(g_attn=m_g_attn, g_mlp=m_g_mlp, w_in_a=m_w_in_a, b_f=m_b_f, gq_a=m_gq_a, gk_a=m_gk_a, w_out_a=m_w_out_a,
               g_kv=m_g_kv, w_kv=m_w_kv, gk_b=m_gk_b, w_q_b=m_w_q_b, gq_b=m_gq_b, sinks=m_sinks, rel_bias=m_rel_bias,
               w_out_b=m_w_out_b, w_up=m_w_up, w_down=m_w_down)
    var = dict(g_attn=v_g_attn, g_mlp=v_g_mlp, w_in_a=v_w_in_a, b_f=v_b_f, gq_a=v_gq_a, gk_a=v_gk_a, w_out_a=v_w_out_a,
               g_kv=v_g_kv, w_kv=v_w_kv, gk_b=v_gk_b, w_q_b=v_w_q_b, gq_b=v_gq_b, sinks=v_sinks, rel_bias=v_rel_bias,
               w_out_b=v_w_out_b, w_up=v_w_up, w_down=v_w_down)
    d = x.shape[2]
    where = {"w_in_a": ("w_in_a", 0), "w_out_a": ("w_out_a", 0), "w_kv": ("w_kv", None), "w_q_b": ("w_q_b", 0),
             "w_out_b": ("w_out_b", 0), "w_up0": ("w_up", 0), "w_up1": ("w_up", 1), "w_down0": ("w_down", 0),
             "w_down1": ("w_down", 1)}
    flip = lambda tree: {**tree, "w_in_a": jnp.swapaxes(tree["w_in_a"], 1, 2)}
    w, mom, var = flip(w), flip(mom), flip(var)
    shards = {n: (w[k] if layer is None else w[k][layer]) for n, (k, layer) in where.items()}
    comm = _Comm(shards, d, w_in_a.shape[2] * N_DEV)
    loss_tile, grad_x, grads = _local_step(x[0], loss_target[0], {k: w[k] for k in SMALL}, comm)

    small_flight = _exchange_start(
        "gather_small_grads", [_pack_small("pack_small", _small_rows(grads, loss_tile), d), grads["rel_bias"]], scatter=False)
    res, after = {}, small_flight.token
    for index in range(len(comm.sent)):
        for n, parts in comm.received(index, after).items():
            k, layer = where[n]
            res[k] = _adamw(f"adam_{n}", parts, w[k], mom[k], var[k], layer, res.get(k))
            after = res[k][0]
    as_rows = lambda tree: {k: tree[k] if tree[k].ndim == 2 else tree[k].reshape(1, -1) for k in SMALL}
    small, loss_row = _adamw_small("adam_small", *_exchange_wait("gather_small_wait", small_flight, [0, 1], after),
                                   as_rows(w), as_rows(mom), as_rows(var))
    loss = loss_row[0, 0]
    for k in SMALL:
        res[k] = [a.reshape(w[k].shape) for a in small[k]]
    res["w_in_a"] = [jnp.swapaxes(a, 1, 2) for a in res["w_in_a"]]

    outs = [loss, grad_x[None]]
    for i in range(4):
        outs.extend(res[k][i] for k in WEIGHTS)
    return tuple(outs)
```

```python
import numpy as np
import jax
import jax.numpy as jnp
from jax import lax
from jax.experimental import pallas as pl
from jax.experimental.pallas import tpu as pltpu

F32 = jnp.float32
BF16 = jnp.bfloat16

N_DEV = 8
HEAD_DIM = 64
WINDOW = 128
N_BUCKETS = 32
REL_MAX_DIST = 128
NORM_EPS = 1e-6
NEG = -1e30
LANES = 128
VMEM_LIMIT = 56 * 1024 * 1024

ADAM_LR = 0.001
ADAM_B1 = 0.9
ADAM_B2 = 0.999
ADAM_EPS = 1e-08
ADAM_WD = 0.01
ADAM_STEP = 10

SMALL = ("g_attn", "g_mlp", "b_f", "gq_a", "gk_a", "g_kv", "gk_b", "gq_b", "sinks", "rel_bias")
WEIGHTS = ("g_attn", "g_mlp", "w_in_a", "b_f", "gq_a", "gk_a", "w_out_a", "g_kv", "w_kv", "gk_b",
           "w_q_b", "gq_b", "sinks", "rel_bias", "w_out_b", "w_up", "w_down")


def _params(n_grid):
    return pltpu.CompilerParams(dimension_semantics=("arbitrary",) * n_grid, vmem_limit_bytes=VMEM_LIMIT)


def _sds(shape, dtype):
    return jax.ShapeDtypeStruct(tuple(shape), dtype)


def _rms(x, g):
    return (x * lax.rsqrt(jnp.mean(x * x, axis=-1, keepdims=True) + NORM_EPS)) * g


def _dot_nt(a, b):
    return lax.dot_general(a, b, (((1,), (1,)), ((), ())), preferred_element_type=F32)


def _dot_tn(a, b):
    return lax.dot_general(a, b, (((0,), (0,)), ((), ())), preferred_element_type=F32)


def _dot(a, b):
    return jnp.dot(a, b, preferred_element_type=F32)


def _dot_exact(a, b):
    return jnp.dot(a, b, preferred_element_type=F32, precision=lax.Precision.HIGHEST)


def _norm_matmul(name, x, g, w, *, tn=None, relu2=False, w_rows=False):
    t, d = x.shape
    blocked = w.ndim == 3
    if blocked:
        tn = w.shape[2]
        n = w.shape[0] * tn
        w_spec = pl.BlockSpec((None, d, tn), lambda i, j: (j, 0, 0))
    elif w_rows:
        n = w.shape[0]
        w_spec = pl.BlockSpec((tn, d), lambda i, j: (j, 0))
    else:
        n = w.shape[1]
        w_spec = pl.BlockSpec((d, tn), lambda i, j: (0, j))
    tm = min(1024, t)

    def body(x_ref, g_ref, w_ref, y_ref, xn_ref):
        @pl.when(pl.program_id(1) == 0)
        def _():
            xn_ref[...] = _rms(x_ref[...], g_ref[...]).astype(BF16)

        y = _dot_nt(xn_ref[...], w_ref[...]) if w_rows else _dot(xn_ref[...], w_ref[...])
        y_ref[...] = jnp.square(jnp.maximum(y, 0.0)).astype(BF16) if relu2 else y

    out_shape = [_sds((t, n), BF16 if relu2 else F32), _sds((t, d), BF16)]
    out_specs = [pl.BlockSpec((tm, tn), lambda i, j: (i, j)), pl.BlockSpec((tm, d), lambda i, j: (i, 0))]
    return pl.pallas_call(
        body, name=name, grid=(t // tm, n // tn),
        in_specs=[pl.BlockSpec((tm, d), lambda i, j: (i, 0)), pl.BlockSpec((1, d), lambda i, j: (0, 0)), w_spec],
        out_specs=out_specs, out_shape=out_shape, compiler_params=_params(2),
    )(x, g.reshape(1, d), w)


def _matmul_res(name, a, w, res, *, tn=512, after=None):
    t, k = a.shape
    n = w.shape[1]
    tm = min(1024, t)

    def body(a_ref, w_ref, r_ref, *rest):
        rest[-1][...] = r_ref[...] + _dot(a_ref[...], w_ref[...])

    extra = [] if after is None else [after]
    return pl.pallas_call(
        body, name=name, grid=(t // tm, n // tn),
        in_specs=[pl.BlockSpec((tm, k), lambda i, j: (i, 0)), pl.BlockSpec((k, tn), lambda i, j: (0, j)),
                  pl.BlockSpec((tm, tn), lambda i, j: (i, j))] + [pl.BlockSpec((8, LANES), lambda i, j: (0, 0))] * len(extra),
        out_specs=pl.BlockSpec((tm, tn), lambda i, j: (i, j)), out_shape=_sds((t, n), F32),
        compiler_params=_params(2),
    )(a, w, res, *extra)


def _matmul_nt(name, dy, w, *, a=None, tk=1024):
    t, n = dy.shape
    k = w.shape[0]
    tm = min(1024, t)

    def body(dy_ref, w_ref, *rest):
        o_ref = rest[-1]
        r = _dot_nt(dy_ref[...].astype(BF16), w_ref[...])
        if a is not None:
            r = r * (2.0 * jnp.sqrt(rest[0][...].astype(F32)))
        o_ref[...] = r.astype(BF16)

    in_specs = [pl.BlockSpec((tm, n), lambda i, j: (i, 0)), pl.BlockSpec((tk, n), lambda i, j: (j, 0))]
    args = [dy, w]
    if a is not None:
        in_specs.append(pl.BlockSpec((tm, tk), lambda i, j: (i, j)))
        args.append(a)
    return pl.pallas_call(
        body, name=name, grid=(t // tm, k // tk), in_specs=in_specs,
        out_specs=pl.BlockSpec((tm, tk), lambda i, j: (i, j)), out_shape=_sds((t, k), BF16),
        compiler_params=_params(2),
    )(*args)


def _matmul_nt_rmsbwd(name, dy, w, x, g, dres, *, w_rows=False):
    t, k = dy.shape
    blocked = w.ndim == 3
    d = w.shape[1] if blocked or w_rows else w.shape[0]
    tm = min(512, t)

    def body(dy_ref, w_ref, x_ref, g_ref, r_ref, dx_ref, dg_ref):
        if blocked:
            kb = w.shape[2]
            dxn = _dot_nt(dy_ref[:, 0:kb].astype(BF16), w_ref[0])
            for j in range(1, w.shape[0]):
                dxn += _dot_nt(dy_ref[:, j * kb:(j + 1) * kb].astype(BF16), w_ref[j])
        elif w_rows:
            dxn = _dot(dy_ref[...].astype(BF16), w_ref[...])
        else:
            dxn = _dot_nt(dy_ref[...].astype(BF16), w_ref[...])
        _, vjp = jax.vjp(_rms, x_ref[...], g_ref[...])
        dx, dg = vjp(dxn)
        dx_ref[...] = r_ref[...] + dx

        @pl.when(pl.program_id(0) == 0)
        def _():
            dg_ref[...] = jnp.zeros_like(dg_ref)

        dg_ref[...] += dg

    row = lambda i: (i, 0)
    fixed = lambda i: (0, 0)
    return pl.pallas_call(
        body, name=name, grid=(t // tm,),
        in_specs=[pl.BlockSpec((tm, k), row), pl.BlockSpec(w.shape, (lambda i: (0, 0, 0)) if blocked else fixed),
                  pl.BlockSpec((tm, d), row), pl.BlockSpec((1, d), fixed), pl.BlockSpec((tm, d), row)],
        out_specs=[pl.BlockSpec((tm, d), row), pl.BlockSpec((1, d), fixed)],
        out_shape=[_sds((t, d), F32), _sds((1, d), F32)], compiler_params=_params(1),
    )(dy, w, x, g.reshape(1, d), dres)


def _matmul_tn(name, a, b, *, tk=1024, tn=1024, col_blocks=None):
    t, k = a.shape
    n = b.shape[1]
    tk = min(tk, k)
    if col_blocks:
        tn = n // col_blocks
        out_spec, out_shape = pl.BlockSpec((None, tk, tn), lambda i, j: (j, i, 0)), _sds((col_blocks, k, tn), BF16)
    else:
        tn = min(tn, n)
        out_spec, out_shape = pl.BlockSpec((tk, tn), lambda i, j: (i, j)), _sds((k, n), BF16)

    def body(a_ref, b_ref, o_ref):
        o_ref[...] = _dot_tn(a_ref[...].astype(BF16), b_ref[...].astype(BF16)).astype(BF16)

    return pl.pallas_call(
        body, name=name, grid=(k // tk, n // tn),
        in_specs=[pl.BlockSpec((t, tk), lambda i, j: (0, i)), pl.BlockSpec((t, tn), lambda i, j: (0, j))],
        out_specs=out_spec, out_shape=out_shape, compiler_params=_params(2),
    )(a, b)


def _join_row_blocks(name, blocks, rows):
    b, r, c = blocks.shape
    tc = min(256, c)

    def body(g_ref, o_ref):
        o_ref[...] = jnp.zeros_like(o_ref)
        for j in range(b):
            o_ref[r * j:r * (j + 1), :] = g_ref[j]

    return pl.pallas_call(
        body, name=name, grid=(c // tc,), in_specs=[pl.BlockSpec((b, r, tc), lambda i: (0, 0, i))],
        out_specs=pl.BlockSpec((rows, tc), lambda i: (0, i)), out_shape=_sds((rows, c), blocks.dtype),
        compiler_params=_params(1),
    )(blocks)


def _split_row_blocks(name, mat, b, r):
    rows, c = mat.shape
    tc = min(256, c)

    def body(w_ref, o_ref):
        for j in range(b):
            o_ref[j] = w_ref[r * j:r * (j + 1), :]

    return pl.pallas_call(
        body, name=name, grid=(c // tc,), in_specs=[pl.BlockSpec((rows, tc), lambda i: (0, i))],
        out_specs=pl.BlockSpec((b, r, tc), lambda i: (0, 0, i)), out_shape=_sds((b, r, c), mat.dtype),
        compiler_params=_params(1),
    )(mat)


def _matmul_res_loss(name, a, w, res, target, *, tn=512):
    t, k = a.shape
    n = w.shape[1]
    tm = min(1024, t)

    def body(a_ref, w_ref, r_ref, t_ref, dy_ref, l_ref):
        e = r_ref[...] + _dot(a_ref[...], w_ref[...]) - t_ref[...]
        dy_ref[...] = e * (1.0 / n)

        @pl.when((pl.program_id(0) == 0) & (pl.program_id(1) == 0))
        def _():
            l_ref[...] = jnp.zeros_like(l_ref)

        l_ref[...] += (0.5 / n) * jnp.sum(e * e)

    tile = pl.BlockSpec((tm, tn), lambda i, j: (i, j))
    return pl.pallas_call(
        body, name=name, grid=(t // tm, n // tn),
        in_specs=[pl.BlockSpec((tm, k), lambda i, j: (i, 0)), pl.BlockSpec((k, tn), lambda i, j: (0, j)), tile, tile],
        out_specs=[tile, pl.BlockSpec((8, LANES), lambda i, j: (0, 0))],
        out_shape=[_sds((t, n), F32), _sds((8, LANES), F32)], compiler_params=_params(2),
    )(a, w, res, target)


def _gate_fwd(name, proj, b_pad, n_heads, gate_col):
    t = proj.shape[0]
    tb = min(256, t)
    tri = jnp.asarray(np.tril(np.ones((tb, tb), np.float32)))

    def body(p_ref, b_ref, tri_ref, c_ref, carry):
        @pl.when(pl.program_id(0) == 0)
        def _():
            carry[...] = jnp.zeros_like(carry)

        lane = lax.broadcasted_iota(jnp.int32, (tb, LANES), 1)
        lf = jnp.where(lane < n_heads, jax.nn.log_sigmoid(p_ref[...] + b_ref[...]), 0.0)
        c = _dot_exact(tri_ref[...], lf) + carry[0:1, :]
        c_ref[...] = c
        carry[...] = jnp.broadcast_to(c[tb - 1:tb, :], carry.shape)

    return pl.pallas_call(
        body, name=name, grid=(t // tb,),
        in_specs=[pl.BlockSpec((tb, LANES), lambda i: (i, gate_col)), pl.BlockSpec((1, LANES), lambda i: (0, 0)),
                  pl.BlockSpec((tb, tb), lambda i: (0, 0))],
        out_specs=pl.BlockSpec((tb, LANES), lambda i: (i, 0)), out_shape=_sds((t, LANES), F32),
        scratch_shapes=[pltpu.VMEM((8, LANES), F32)], compiler_params=_params(1),
    )(proj, b_pad, tri)


def _gate_bwd(name, proj, b_pad, dc, n_heads, gate_col):
    t = proj.shape[0]
    tb = min(256, t)
    nb = t // tb
    triu = jnp.asarray(np.triu(np.ones((tb, tb), np.float32)))

    def body(p_ref, b_ref, dc_ref, tri_ref, df_ref, db_ref, carry):
        @pl.when(pl.program_id(0) == 0)
        def _():
            carry[...] = jnp.zeros_like(carry)
            db_ref[...] = jnp.zeros_like(db_ref)

        dcv = dc_ref[...]
        dlf = _dot_exact(tri_ref[...], dcv) + carry[0:1, :]
        carry[...] = jnp.broadcast_to(dlf[0:1, :], carry.shape)
        lane = lax.broadcasted_iota(jnp.int32, (tb, LANES), 1)
        z = p_ref[...] + b_ref[...]
        df = jnp.where(lane < n_heads, dlf / (1.0 + jnp.exp(z)), 0.0)
        df_ref[...] = df.astype(BF16)
        db_ref[...] += jnp.sum(df, axis=0, keepdims=True)

    return pl.pallas_call(
        body, name=name, grid=(nb,),
        in_specs=[pl.BlockSpec((tb, LANES), lambda i: (nb - 1 - i, gate_col)), pl.BlockSpec((1, LANES), lambda i: (0, 0)),
                  pl.BlockSpec((tb, LANES), lambda i: (nb - 1 - i, 0)), pl.BlockSpec((tb, tb), lambda i: (0, 0))],
        out_specs=[pl.BlockSpec((tb, LANES), lambda i: (nb - 1 - i, 0)), pl.BlockSpec((1, LANES), lambda i: (0, 0))],
        out_shape=[_sds((t, LANES), BF16), _sds((1, LANES), F32)],
        scratch_shapes=[pltpu.VMEM((8, LANES), F32)], compiler_params=_params(1),
    )(proj, b_pad, dc, triu)


def _qhead(qp, g):
    return _rms(qp, g) * (HEAD_DIM ** -0.5)


def _column(mat, idx):
    lane = lax.broadcasted_iota(jnp.int32, mat.shape, 1)
    return jnp.sum(jnp.where(lane == idx, mat, 0.0), axis=1, keepdims=True)


def _fox_scores(kk, qi, ckey, cq_i, i, bq):
    length = kk.shape[0]
    s = _dot_nt(kk, qi) + cq_i - ckey[:length]
    key = lax.broadcasted_iota(jnp.int32, (length, bq), 0)
    qry = lax.broadcasted_iota(jnp.int32, (length, bq), 1) + i * bq
    return jnp.where(key <= qry, s, NEG)


def _fox_fwd(name, proj, c, crow, gq, gk, n_heads):
    t = proj.shape[0]
    hw = n_heads * HEAD_DIM
    npair = n_heads // 2
    bq = min(512, t)
    nq = t // bq

    def body(q_ref, k_ref, v_ref, c_ref, crow_ref, gq_ref, gk_ref, o_ref, lse_ref):
        hp = pl.program_id(0)
        lse_ref[...] = jnp.zeros_like(lse_ref)
        outs = []
        for hh in range(2):
            sl = slice(hh * HEAD_DIM, (hh + 1) * HEAD_DIM)
            qn = _qhead(q_ref[:, sl], gq_ref[...]).astype(BF16)
            kn = _rms(k_ref[:, sl], gk_ref[...]).astype(BF16)
            v_t = v_ref[:, sl].T.astype(BF16)
            ckey = _column(c_ref[...], 2 * hp + hh)
            cq = crow_ref[0, hh:hh + 1, :]
            o_blocks = []
            for i in range(nq):
                cols = slice(i * bq, (i + 1) * bq)
                length = (i + 1) * bq
                s = _fox_scores(kn[:length], qn[cols], ckey, cq[:, cols], i, bq)
                m = jnp.max(s, axis=0, keepdims=True)
                p = jnp.exp(s - m)
                l = jnp.sum(p, axis=0, keepdims=True)
                o_blocks.append((_dot(v_t[:, :length], p.astype(BF16)) / l).T)
                lse_ref[0, hh:hh + 1, cols] = m + jnp.log(l)
            outs.append(jnp.concatenate(o_blocks, axis=0))
        o_ref[...] = jnp.concatenate(outs, axis=1).astype(BF16)

    col = lambda off: (lambda h: (0, off + h))
    fixed = lambda h: (0, 0)
    return pl.pallas_call(
        body, name=name, grid=(npair,),
        in_specs=[pl.BlockSpec((t, LANES), col(0)), pl.BlockSpec((t, LANES), col(npair)), pl.BlockSpec((t, LANES), col(2 * npair)),
                  pl.BlockSpec((t, LANES), fixed), pl.BlockSpec((1, 2, t), lambda h: (h, 0, 0)),
                  pl.BlockSpec((1, HEAD_DIM), fixed), pl.BlockSpec((1, HEAD_DIM), fixed)],
        out_specs=[pl.BlockSpec((t, LANES), col(0)), pl.BlockSpec((1, 8, t), lambda h: (h, 0, 0))],
        out_shape=[_sds((t, hw), BF16), _sds((npair, 8, t), F32)], compiler_params=_params(1),
    )(proj, proj, proj, c, crow, gq, gk)


def _fox_bwd(name, proj, c, crow, gq, gk, lse, do, n_heads):
    t = proj.shape[0]
    hw = n_heads * HEAD_DIM
    npair = n_heads // 2
    bq = min(256, t)
    nq = t // bq

    def body(q_ref, k_ref, v_ref, c_ref, crow_ref, gq_ref, gk_ref, lse_ref, do_ref,
             dq_ref, dk_ref, dv_ref, dc_ref, dgq_ref, dgk_ref, dk_acc, dv_acc, dc_acc):
        hp = pl.program_id(0)

        @pl.when(hp == 0)
        def _():
            dgq_ref[...] = jnp.zeros_like(dgq_ref)
            dgk_ref[...] = jnp.zeros_like(dgk_ref)
            dc_ref[...] = jnp.zeros_like(dc_ref)

        lane = lax.broadcasted_iota(jnp.int32, (t, LANES), 1)
        dqs, dks, dvs = [], [], []
        for hh in range(2):
            sl = slice(hh * HEAD_DIM, (hh + 1) * HEAD_DIM)
            qf, q_vjp = jax.vjp(_qhead, q_ref[:, sl], gq_ref[...])
            kf, k_vjp = jax.vjp(_rms, k_ref[:, sl], gk_ref[...])
            qn, kn, kn_t = qf.astype(BF16), kf.astype(BF16), kf.T.astype(BF16)
            vb = v_ref[:, sl].astype(BF16)
            dob = do_ref[:, sl]
            ckey = _column(c_ref[...], 2 * hp + hh)
            cq = crow_ref[0, hh:hh + 1, :]
            lse_h = lse_ref[0, hh:hh + 1, :]
            dk_acc[...] = jnp.zeros_like(dk_acc)
            dv_acc[...] = jnp.zeros_like(dv_acc)
            dc_acc[...] = jnp.zeros_like(dc_acc)
            dq_blocks = []
            for i in range(nq):
                cols = slice(i * bq, (i + 1) * bq)
                length = (i + 1) * bq
                qi, doi = qn[cols], dob[cols]
                s = _fox_scores(kn[:length], qi, ckey, cq[:, cols], i, bq)
                p = jnp.exp(s - lse_h[:, cols])
                dp = _dot_nt(vb[:length], doi)
                ds = p * (dp - jnp.sum(p * dp, axis=0, keepdims=True))
                dsb = ds.astype(BF16)
                dq_blocks.append(_dot(kn_t[:, :length], dsb).T)
                dk_acc[0:length, :] += _dot(dsb, qi)
                dv_acc[0:length, :] += _dot(p.astype(BF16), doi)
                part = ds[:, 0:LANES]
                for j in range(1, bq // LANES):
                    part = part + ds[:, j * LANES:(j + 1) * LANES]
                dc_acc[0:length, :] += part
            dqp, dgq = q_vjp(jnp.concatenate(dq_blocks, axis=0))
            dkp, dgk = k_vjp(dk_acc[...])
            dgq_ref[...] += dgq
            dgk_ref[...] += dgk
            dqs.append(dqp)
            dks.append(dkp)
            dvs.append(dv_acc[...])
            dc_ref[...] = jnp.where(lane == 2 * hp + hh, -jnp.sum(dc_acc[...], axis=1, keepdims=True), dc_ref[...])
        dq_ref[...] = jnp.concatenate(dqs, axis=1).astype(BF16)
        dk_ref[...] = jnp.concatenate(dks, axis=1).astype(BF16)
        dv_ref[...] = jnp.concatenate(dvs, axis=1).astype(BF16)

    col = lambda off: (lambda h: (0, off + h))
    fixed = lambda h: (0, 0)
    pair_blk = pl.BlockSpec((t, LANES), col(0))
    return pl.pallas_call(
        body, name=name, grid=(npair,),
        in_specs=[pl.BlockSpec((t, LANES), col(0)), pl.BlockSpec((t, LANES), col(npair)), pl.BlockSpec((t, LANES), col(2 * npair)),
                  pl.BlockSpec((t, LANES), fixed), pl.BlockSpec((1, 2, t), lambda h: (h, 0, 0)),
                  pl.BlockSpec((1, HEAD_DIM), fixed), pl.BlockSpec((1, HEAD_DIM), fixed),
                  pl.BlockSpec((1, 8, t), lambda h: (h, 0, 0)), pair_blk],
        out_specs=[pair_blk, pair_blk, pair_blk, pl.BlockSpec((t, LANES), fixed),
                   pl.BlockSpec((1, HEAD_DIM), fixed), pl.BlockSpec((1, HEAD_DIM), fixed)],
        out_shape=[_sds((t, hw), BF16), _sds((t, hw), BF16), _sds((t, hw), BF16), _sds((t, LANES), F32),
                   _sds((1, HEAD_DIM), F32), _sds((1, HEAD_DIM), F32)],
        scratch_shapes=[pltpu.VMEM((t, HEAD_DIM), F32), pltpu.VMEM((t, HEAD_DIM), F32), pltpu.VMEM((t, LANES), F32)],
        compiler_params=_params(1),
    )(proj, proj, proj, c, crow, gq, gk, lse, do)


def _t5_bucket_table():
    dist = np.arange(WINDOW)[None, :] + WINDOW - np.arange(2 * WINDOW)[:, None]
    n = np.maximum(dist, 0)
    max_exact = N_BUCKETS // 2
    large = max_exact + (np.log(np.maximum(n, 1) / max_exact) / np.log(REL_MAX_DIST / max_exact)
                         * (N_BUCKETS - max_exact)).astype(np.int32)
    large = np.minimum(large, N_BUCKETS - 1)
    return np.where(n < max_exact, n, large).astype(np.int32).reshape(1, -1)


def _bias_expand(name, rel_bias_t):
    n_heads = rel_bias_t.shape[0]
    tbl = jnp.asarray(_t5_bucket_table())
    width = tbl.shape[1]

    def body(rb_ref, tbl_ref, o_ref):
        onehot = (lax.broadcasted_iota(jnp.int32, (N_BUCKETS, width), 0) == tbl_ref[...]).astype(F32)
        o_ref[...] = _dot_exact(rb_ref[...], onehot)

    return pl.pallas_call(body, name=name, out_shape=_sds((n_heads, width), F32), compiler_params=_params(0))(rel_bias_t, tbl)


def _bias_reduce(name, dbias):
    n_heads, width = dbias.shape
    tbl = jnp.asarray(_t5_bucket_table())

    def body(db_ref, tbl_ref, o_ref):
        onehot = (lax.broadcasted_iota(jnp.int32, (N_BUCKETS, width), 0) == tbl_ref[...]).astype(F32)
        o_ref[...] = lax.dot_general(db_ref[...], onehot, (((1,), (1,)), ((), ())), preferred_element_type=F32,
                                     precision=lax.Precision.HIGHEST)

    return pl.pallas_call(body, name=name, out_shape=_sds((n_heads, N_BUCKETS), F32), compiler_params=_params(0))(dbias, tbl)


def _swa_mask(n, group):
    j = lax.broadcasted_iota(jnp.int32, (2 * WINDOW, group * WINDOW), 0)
    i = lax.broadcasted_iota(jnp.int32, (2 * WINDOW, group * WINDOW), 1) & (WINDOW - 1)
    ok = (j > i) & (j <= i + WINDOW) & ((n > 0) | (j >= WINDOW))
    return jnp.where(ok, 0.0, NEG)


def _swa_stack(ref, start, group):
    return jnp.concatenate([ref[pl.ds(start, WINDOW), g * HEAD_DIM:(g + 1) * HEAD_DIM] for g in range(group)], axis=0)


def _kv_head(ref, n_kv):
    out = ref[:, 0:HEAD_DIM]
    for h in range(1, n_kv):
        out = jnp.where(pl.program_id(0) == h, ref[:, h * HEAD_DIM:(h + 1) * HEAD_DIM], out)
    return out


def _swa_fwd(name, qb, kv, gq, gk, sinks, bias, group):
    t = qb.shape[0]
    kvh = kv.shape[1] // (2 * HEAD_DIM)
    nblk = t // WINDOW
    gw = group * HEAD_DIM
    band = 2 * WINDOW
    cols = group * WINDOW

    def body(q_ref, k_ref, v_ref, gq_ref, gk_ref, sink_ref, bias_ref, o_ref, lse_ref, qs, kpad, vpad):
        for g in range(group):
            qs[:, g * HEAD_DIM:(g + 1) * HEAD_DIM] = _qhead(q_ref[:, g * HEAD_DIM:(g + 1) * HEAD_DIM], gq_ref[...]).astype(BF16)
        kpad[0:WINDOW, :] = jnp.zeros((WINDOW, HEAD_DIM), BF16)
        vpad[0:WINDOW, :] = jnp.zeros((WINDOW, HEAD_DIM), BF16)
        kpad[WINDOW:, :] = _rms(_kv_head(k_ref, kvh), gk_ref[...]).astype(BF16)
        vpad[WINDOW:, :] = _kv_head(v_ref, kvh).astype(BF16)
        sink = sink_ref[0]

        def block(n, carry):
            start = pl.multiple_of(n * WINDOW, WINDOW)
            kb = kpad[pl.ds(start, band), :]
            vb = vpad[pl.ds(start, band), :]
            s = _dot_nt(kb, _swa_stack(qs, start, group)) + bias_ref[0] + _swa_mask(n, group)
            m = jnp.maximum(jnp.max(s, axis=0, keepdims=True), sink)
            e = jnp.exp(s - m)
            l = jnp.sum(e, axis=0, keepdims=True) + jnp.exp(sink - m)
            o_t = _dot_tn(vb, e.astype(BF16)) / l
            for g in range(group):
                o_ref[pl.ds(start, WINDOW), g * HEAD_DIM:(g + 1) * HEAD_DIM] = o_t[:, g * WINDOW:(g + 1) * WINDOW].T.astype(BF16)
            lse_ref[pl.ds(n, 1), :] = m + jnp.log(l)
            return carry

        lax.fori_loop(0, nblk, block, 0)

    fixed = lambda h: (0, 0)
    per = lambda h: (h, 0, 0)
    return pl.pallas_call(
        body, name=name, grid=(kvh,),
        in_specs=[pl.BlockSpec((t, gw), lambda h: (0, h)), pl.BlockSpec((t, kvh * HEAD_DIM), lambda h: (0, 0)),
                  pl.BlockSpec((t, kvh * HEAD_DIM), lambda h: (0, 1)),
                  pl.BlockSpec((1, HEAD_DIM), fixed), pl.BlockSpec((1, HEAD_DIM), fixed),
                  pl.BlockSpec((1, 1, cols), per), pl.BlockSpec((1, band, cols), per)],
        out_specs=[pl.BlockSpec((t, gw), lambda h: (0, h)), pl.BlockSpec((nblk, cols), lambda h: (h, 0))],
        out_shape=[_sds((t, kvh * gw), BF16), _sds((kvh * nblk, cols), F32)],
        scratch_shapes=[pltpu.VMEM((t, gw), BF16), pltpu.VMEM((t + WINDOW, HEAD_DIM), BF16),
                        pltpu.VMEM((t + WINDOW, HEAD_DIM), BF16)],
        compiler_params=_params(1),
    )(qb, kv, kv, gq, gk, sinks, bias)


def _swa_bwd(name, qb, kv, gq, gk, sinks, bias, lse, do, group):
    t = qb.shape[0]
    kvh = kv.shape[1] // (2 * HEAD_DIM)
    kvw = kvh * HEAD_DIM
    nblk = t // WINDOW
    gw = group * HEAD_DIM
    band = 2 * WINDOW
    cols = group * WINDOW

    def body(q_ref, k_ref, v_ref, gq_ref, gk_ref, sink_ref, bias_ref, lse_ref, do_ref,
             dq_ref, dkv_ref, dgq_ref, dgk_ref, dsink_ref, dbias_ref,
             qs, kpad, vpad, dqs, dk_acc, dv_acc, dsink_acc):
        @pl.when(pl.program_id(0) == 0)
        def _():
            dgq_ref[...] = jnp.zeros_like(dgq_ref)
            dgk_ref[...] = jnp.zeros_like(dgk_ref)
            dkv_ref[...] = jnp.zeros_like(dkv_ref)

        for g in range(group):
            qs[:, g * HEAD_DIM:(g + 1) * HEAD_DIM] = _qhead(q_ref[:, g * HEAD_DIM:(g + 1) * HEAD_DIM], gq_ref[...]).astype(BF16)
        kpad[0:WINDOW, :] = jnp.zeros((WINDOW, HEAD_DIM), BF16)
        vpad[0:WINDOW, :] = jnp.zeros((WINDOW, HEAD_DIM), BF16)
        kpad[WINDOW:, :] = _rms(_kv_head(k_ref, kvh), gk_ref[...]).astype(BF16)
        vpad[WINDOW:, :] = _kv_head(v_ref, kvh).astype(BF16)
        dk_acc[...] = jnp.zeros_like(dk_acc)
        dv_acc[...] = jnp.zeros_like(dv_acc)
        dsink_acc[...] = jnp.zeros_like(dsink_acc)
        dbias_ref[...] = jnp.zeros_like(dbias_ref)
        sink = sink_ref[0]

        def block(n, carry):
            start = pl.multiple_of(n * WINDOW, WINDOW)
            kb = kpad[pl.ds(start, band), :]
            vb = vpad[pl.ds(start, band), :]
            q = _swa_stack(qs, start, group)
            dob = _swa_stack(do_ref, start, group)
            lse_n = lse_ref[pl.ds(n, 1), :]
            s = _dot_nt(kb, q) + bias_ref[0] + _swa_mask(n, group)
            p = jnp.exp(s - lse_n)
            dp = _dot_nt(vb, dob)
            dsum = jnp.sum(p * dp, axis=0, keepdims=True)
            ds = p * (dp - dsum)
            dsb = ds.astype(BF16)
            dsink_acc[...] -= jnp.exp(sink - lse_n) * dsum
            dbias_ref[0] += ds
            dq = _dot_tn(dsb, kb)
            for g in range(group):
                dqs[pl.ds(start, WINDOW), g * HEAD_DIM:(g + 1) * HEAD_DIM] = dq[g * WINDOW:(g + 1) * WINDOW]
            dk_acc[pl.ds(start, band), :] += _dot(dsb, q)
            dv_acc[pl.ds(start, band), :] += _dot(p.astype(BF16), dob)
            return carry

        lax.fori_loop(0, nblk, block, 0)
        for g in range(group):
            _, q_vjp = jax.vjp(_qhead, q_ref[:, g * HEAD_DIM:(g + 1) * HEAD_DIM], gq_ref[...])
            dqp, dgq = q_vjp(dqs[:, g * HEAD_DIM:(g + 1) * HEAD_DIM])
            dq_ref[:, g * HEAD_DIM:(g + 1) * HEAD_DIM] = dqp.astype(BF16)
            dgq_ref[...] += dgq
            dsink_g = jnp.sum(dsink_acc[:, g * WINDOW:(g + 1) * WINDOW], axis=1, keepdims=True)
            dsink_ref[0, g:g + 1, :] = jnp.broadcast_to(dsink_g, (1, LANES))
        _, k_vjp = jax.vjp(_rms, _kv_head(k_ref, kvh), gk_ref[...])
        dkp, dgk = k_vjp(dk_acc[WINDOW:, :])
        dgk_ref[...] += dgk
        mine = lax.broadcasted_iota(jnp.int32, (t, kvw), 1) // HEAD_DIM == pl.program_id(0)
        dkv_ref[:, 0:kvw] = jnp.where(mine, jnp.concatenate([dkp] * kvh, axis=1), dkv_ref[:, 0:kvw])
        dkv_ref[:, kvw:] = jnp.where(mine, jnp.concatenate([dv_acc[WINDOW:, :]] * kvh, axis=1), dkv_ref[:, kvw:])

    fixed = lambda h: (0, 0)
    per = lambda h: (h, 0, 0)
    wide = pl.BlockSpec((t, gw), lambda h: (0, h))
    vec = pl.BlockSpec((1, HEAD_DIM), fixed)
    bias_spec = pl.BlockSpec((1, band, cols), per)
    return pl.pallas_call(
        body, name=name, grid=(kvh,),
        in_specs=[wide, pl.BlockSpec((t, kvw), lambda h: (0, 0)), pl.BlockSpec((t, kvw), lambda h: (0, 1)), vec, vec,
                  pl.BlockSpec((1, 1, cols), per), bias_spec, pl.BlockSpec((nblk, cols), lambda h: (h, 0)), wide],
        out_specs=[wide, pl.BlockSpec((t, 2 * kvw), fixed), vec, vec, pl.BlockSpec((1, group, LANES), per), bias_spec],
        out_shape=[_sds((t, kvh * gw), BF16), _sds((t, 2 * kvw), F32),
                   _sds((1, HEAD_DIM), F32), _sds((1, HEAD_DIM), F32),
                   _sds((kvh, group, LANES), F32), _sds((kvh, band, cols), F32)],
        scratch_shapes=[pltpu.VMEM((t, gw), BF16), pltpu.VMEM((t + WINDOW, HEAD_DIM), BF16),
                        pltpu.VMEM((t + WINDOW, HEAD_DIM), BF16), pltpu.VMEM((t, gw), F32),
                        pltpu.VMEM((t + WINDOW, HEAD_DIM), F32), pltpu.VMEM((t + WINDOW, HEAD_DIM), F32),
                        pltpu.VMEM((1, cols), F32)],
        compiler_params=_params(1),
    )(qb, kv, kv, gq, gk, sinks, bias, lse, do)


def _local_step(x, target, p, comm):
    t, d = x.shape
    n_heads = d // HEAD_DIM
    kv_heads = n_heads // 8
    group = n_heads // kv_heads
    hw = n_heads * HEAD_DIM
    gate_col = 3 * hw // LANES
    kvw = kv_heads * HEAD_DIM
    grads = {}

    def mlp_fwd(tag, h, g, layer, last=False):
        w_up, = comm.weights([f"w_up{layer}"], h)
        a, hn = _norm_matmul(f"{tag}_up", h, g, w_up, relu2=True)
        w_down, = comm.weights([f"w_down{layer}"], a)
        out = _matmul_res_loss(f"{tag}_down", a, w_down, h, target) if last else _matmul_res(f"{tag}_down", a, w_down, h)
        return out, (h, g, hn, a, w_up, w_down)

    def mlp_bwd(tag, saved, layer, dy):
        h, g, hn, a, w_up, w_down = saved
        du = _matmul_nt(f"{tag}_du", dy, w_down, a=a)
        dw_down = _matmul_tn(f"{tag}_dwdown", a, dy)
        dw_up = _matmul_tn(f"{tag}_dwup", hn, du, col_blocks=w_up.shape[0])
        zero = comm.send_grads(tag, {f"w_down{layer}": dw_down, f"w_up{layer}": dw_up})
        return _matmul_nt_rmsbwd(f"{tag}_dh", du, w_up, h, g + zero, dy)

    comm.prefetch(["w_in_a"], None)
    w_in, = comm.weights(["w_in_a"], None)
    proj, xn1 = _norm_matmul("a_inproj", x, p["g_attn"][0], w_in, tn=640, w_rows=True)
    ahead = comm.prefetch(["w_out_a"], proj)
    b_pad = jnp.pad(p["b_f"], ((0, 0), (0, LANES - n_heads))) + ahead[0:1, :]
    c = _gate_fwd("a_gate", proj, b_pad, n_heads, gate_col)
    crow = c[:, :n_heads].T.reshape(n_heads // 2, 2, t)
    o_a, lse_a = _fox_fwd("a_attn", proj, c, crow, p["gq_a"], p["gk_a"], n_heads)
    ahead = comm.prefetch(["w_up0", "w_down0", "w_kv", "w_q_b", "w_out_b"], o_a)
    w_out_a, = comm.weights(["w_out_a"], o_a)
    h1 = _matmul_res("a_outproj", o_a, w_out_a, x, after=ahead)
    h2, mlp0 = mlp_fwd("mlp0", h1, p["g_mlp"][0], 0)

    ahead = comm.prefetch(["w_up1", "w_down1"], h2)
    w_kv, w_q_b = comm.weights(["w_kv", "w_q_b"], h2)
    kv, hn_kv = _norm_matmul("kv_proj", h2, p["g_kv"] + ahead[0, 0], w_kv, tn=2 * kvw)
    qb, hn_q = _norm_matmul("b_qproj", h2, p["g_attn"][1], w_q_b, tn=512)
    gqb, gkb = p["gq_b"], p["gk_b"].reshape(1, HEAD_DIM)
    bias = _bias_expand("b_bias", p["rel_bias"].T).reshape(kv_heads, group, 2 * WINDOW, WINDOW)
    bias = bias.transpose(0, 2, 1, 3).reshape(kv_heads, 2 * WINDOW, group * WINDOW)
    sink_rows = jnp.broadcast_to(p["sinks"].reshape(kv_heads, 1, group, 1), (kv_heads, 1, group, WINDOW)).reshape(kv_heads, 1, group * WINDOW)
    o_b, lse_b = _swa_fwd("b_attn", qb, kv, gqb, gkb, sink_rows, bias, group)
    w_out_b, = comm.weights(["w_out_b"], o_b)
    h3 = _matmul_res("b_outproj", o_b, w_out_b, h2)
    (dy, loss_tile), mlp1 = mlp_fwd("mlp1", h3, p["g_mlp"][1], 1, last=True)

    dh3, dg_mlp1 = mlp_bwd("mlp1", mlp1, 1, dy)
    do_b = _matmul_nt("b_do", dh3, w_out_b)
    dw_out_b = _matmul_tn("b_dwout", o_b, dh3)
    dqb, dkv, grads["gq_b"], dgk_b, dsink, dbias = _swa_bwd(
        "b_attn_bwd", qb, kv, gqb, gkb, sink_rows, bias, lse_b, do_b, group)
    grads["gk_b"] = dgk_b
    grads["sinks"] = dsink[:, :, 0].reshape(1, n_heads)
    dbias = dbias.reshape(kv_heads, 2 * WINDOW, group, WINDOW).transpose(0, 2, 1, 3)
    grads["rel_bias"] = _bias_reduce("b_dbias", dbias.reshape(n_heads, WINDOW * 2 * WINDOW)).T
    dw_q_b = _matmul_tn("b_dwq", hn_q, dqb)
    dh2, dg_attn1 = _matmul_nt_rmsbwd("b_dhq", dqb, w_q_b, h2, p["g_attn"][1], dh3)
    dw_kv = _matmul_tn("kv_dw", hn_kv, dkv)
    zero = comm.send_grads("attn_b", {"w_out_b": dw_out_b, "w_q_b": dw_q_b, "w_kv": dw_kv})
    dh2, dg_kv = _matmul_nt_rmsbwd("kv_dh", dkv, w_kv, h2, p["g_kv"] + zero, dh2)
    grads["g_kv"] = dg_kv
    dh1, dg_mlp0 = mlp_bwd("mlp0", mlp0, 0, dh2)
    grads["g_mlp"] = (dg_mlp0, dg_mlp1)

    do_a = _matmul_nt("a_do", dh1, w_out_a)
    dw_out_a = _matmul_tn("a_dwout", o_a, dh1)
    zero = comm.send_grads("attn_a_out", {"w_out_a": dw_out_a})
    dq, dk, dv, dc, grads["gq_a"], grads["gk_a"] = _fox_bwd(
        "a_attn_bwd", proj, c, crow, p["gq_a"] + zero, p["gk_a"], lse_a, do_a, n_heads)
    dfl, db_f = _gate_bwd("a_gate_bwd", proj, b_pad, dc, n_heads, gate_col)
    grads["b_f"] = db_f
    dproj = jnp.concatenate([dq, dk, dv, dfl], axis=1)
    dw_in = _matmul_tn("a_dwin", dproj, xn1, tk=640)
    zero = comm.send_grads("attn_a_in", {"w_in_a": dw_in})
    grad_x, dg_attn0 = _matmul_nt_rmsbwd("a_dx", dproj, w_in, x, p["g_attn"][0] + zero, dh1, w_rows=True)
    grads["g_attn"] = (dg_attn0, dg_attn1)
    return loss_tile, grad_x, grads


EVERYONE = (1, 2, 3, 4, 5, 6, 7)
SAME_CORE = (1, 2, 4, 6)
OTHER_CHIPS = (2, 4, 6)


class _InFlight:
    def __init__(self, scatter, ks, send_sems, recv_sems, srcs, lands, token):
        self.scatter, self.ks, self.send_sems, self.recv_sems = scatter, ks, send_sems, recv_sems
        self.srcs, self.lands, self.token = list(srcs), list(lands), token


def _mesh_peers(ks=EVERYONE):
    x, y, c = lax.axis_index("x"), lax.axis_index("y"), lax.axis_index("c")
    peers = []
    for k in ks:
        px, py, pc = x ^ ((k >> 2) & 1), y ^ ((k >> 1) & 1), c ^ (k & 1)
        peers.append(((px, py, pc), 4 * px + 2 * py + pc))
    return 4 * x + 2 * y + c, peers


_HBM_SPEC = pl.BlockSpec(memory_space=pltpu.HBM)
_SEM_SPEC = pl.BlockSpec(memory_space=pltpu.SEMAPHORE)
_SIDE_EFFECT = pltpu.SideEffectType.DATAFLOW_SIDE_EFFECTING


def _exchange_start(name, arrays, scatter, ks=EVERYONE):
    n = len(arrays)
    me, _ = _mesh_peers()
    lands = []
    for a in arrays:
        own = lax.dynamic_index_in_dim(a, me, 0, keepdims=False) if scatter else a
        shape = a.shape if scatter else (N_DEV,) + a.shape
        lands.append(lax.dynamic_update_index_in_dim(lax.empty(shape, a.dtype), own, me, 0))

    def body(*refs):
        src, land = refs[:n], refs[n:2 * n]
        send_sems, recv_sems, token = refs[2 * n], refs[2 * n + 1], refs[-1]
        pos, peers = _mesh_peers(ks)
        for a in range(n):
            for k, (peer, peer_pos) in enumerate(peers):
                pltpu.make_async_remote_copy(
                    src_ref=src[a].at[peer_pos] if scatter else src[a], dst_ref=land[a].at[pos],
                    send_sem=send_sems.at[a * len(ks) + k], recv_sem=recv_sems.at[a * len(ks) + k],
                    device_id=peer, device_id_type=pl.DeviceIdType.MESH).start()
        token[...] = jnp.zeros_like(token)

    operands = [pltpu.with_memory_space_constraint(a, pltpu.HBM) for a in list(arrays) + lands]
    outs = pl.pallas_call(
        body, name=name,
        out_shape=(pltpu.SemaphoreType.DMA((n * len(ks),)), pltpu.SemaphoreType.DMA((n * len(ks),)),
                   *[pltpu.HBM(a.shape, a.dtype) for a in operands], _sds((8, LANES), F32)),
        in_specs=[_HBM_SPEC] * (2 * n),
        out_specs=(_SEM_SPEC, _SEM_SPEC, *[_HBM_SPEC] * (2 * n), pl.BlockSpec(memory_space=pltpu.VMEM)),
        input_output_aliases={i: 2 + i for i in range(2 * n)},
        compiler_params=pltpu.CompilerParams(has_side_effects=_SIDE_EFFECT),
    )(*operands)
    return _InFlight(scatter, ks, outs[0], outs[1], outs[2:2 + n], outs[2 + n:2 + 2 * n], outs[-1])


def _exchange_wait(name, flight, which, after):
    m = len(which)
    scatter, ks = flight.scatter, flight.ks

    def body(*refs):
        src, land = refs[:m], refs[m:2 * m]
        send_sems, recv_sems = refs[2 * m], refs[2 * m + 1]
        _, peers = _mesh_peers(ks)
        for i, a in enumerate(which):
            for k, (peer, peer_pos) in enumerate(peers):
                cp = pltpu.make_async_remote_copy(
                    src_ref=src[i].at[peer_pos] if scatter else src[i], dst_ref=land[i].at[peer_pos],
                    send_sem=send_sems.at[a * len(ks) + k], recv_sem=recv_sems.at[a * len(ks) + k],
                    device_id=peer, device_id_type=pl.DeviceIdType.MESH)
                cp.wait_send()
                cp.wait_recv()

    operands = [flight.srcs[a] for a in which] + [flight.lands[a] for a in which]
    outs = pl.pallas_call(
        body, name=name, out_shape=tuple(pltpu.HBM(a.shape, a.dtype) for a in operands),
        in_specs=[_HBM_SPEC] * (2 * m) + [_SEM_SPEC, _SEM_SPEC, pl.BlockSpec(memory_space=pl.ANY)],
        out_specs=tuple([_HBM_SPEC] * (2 * m)), input_output_aliases={i: i for i in range(2 * m)},
        compiler_params=pltpu.CompilerParams(has_side_effects=_SIDE_EFFECT),
    )(*operands, flight.send_sems, flight.recv_sems, after)
    return list(outs[m:])


def _relay_start(name, lands):
    n = len(lands)

    def body(*refs):
        land, send_sems, recv_sems, token = refs[:n], refs[n], refs[n + 1], refs[-1]
        _, peers = _mesh_peers(OTHER_CHIPS)
        sibling = (lax.axis_index("x"), lax.axis_index("y"), 1 - lax.axis_index("c"))
        for a in range(n):
            for k, (_, peer_pos) in enumerate(peers):
                pltpu.make_async_remote_copy(
                    src_ref=land[a].at[peer_pos], dst_ref=land[a].at[peer_pos],
                    send_sem=send_sems.at[a * len(peers) + k], recv_sem=recv_sems.at[a * len(peers) + k],
                    device_id=sibling, device_id_type=pl.DeviceIdType.MESH).start()
        token[...] = jnp.zeros_like(token)

    count = n * len(OTHER_CHIPS)
    outs = pl.pallas_call(
        body, name=name,
        out_shape=(pltpu.SemaphoreType.DMA((count,)), pltpu.SemaphoreType.DMA((count,)),
                   *[pltpu.HBM(a.shape, a.dtype) for a in lands], _sds((8, LANES), F32)),
        in_specs=[_HBM_SPEC] * n,
        out_specs=(_SEM_SPEC, _SEM_SPEC, *[_HBM_SPEC] * n, pl.BlockSpec(memory_space=pltpu.VMEM)),
        input_output_aliases={i: 2 + i for i in range(n)},
        compiler_params=pltpu.CompilerParams(has_side_effects=_SIDE_EFFECT),
    )(*[pltpu.with_memory_space_constraint(a, pltpu.HBM) for a in lands])
    return _InFlight(False, OTHER_CHIPS, outs[0], outs[1], [], outs[2:2 + n], outs[-1])


def _relay_wait(name, flight, which, after):
    m = len(which)

    def body(*refs):
        land, send_sems, recv_sems = refs[:m], refs[m], refs[m + 1]
        _, peers = _mesh_peers(OTHER_CHIPS)
        sibling = (lax.axis_index("x"), lax.axis_index("y"), 1 - lax.axis_index("c"))
        for i, a in enumerate(which):
            for k, (_, peer_pos) in enumerate(peers):
                cp = pltpu.make_async_remote_copy(
                    src_ref=land[i].at[peer_pos], dst_ref=land[i].at[peer_pos ^ 1],
                    send_sem=send_sems.at[a * len(peers) + k], recv_sem=recv_sems.at[a * len(peers) + k],
                    device_id=sibling, device_id_type=pl.DeviceIdType.MESH)
                cp.wait_send()
                cp.wait_recv()

    operands = [flight.lands[a] for a in which]
    outs = pl.pallas_call(
        body, name=name, out_shape=tuple(pltpu.HBM(a.shape, a.dtype) for a in operands),
        in_specs=[_HBM_SPEC] * m + [_SEM_SPEC, _SEM_SPEC, pl.BlockSpec(memory_space=pl.ANY)],
        out_specs=tuple([_HBM_SPEC] * m), input_output_aliases={i: i for i in range(m)},
        compiler_params=pltpu.CompilerParams(has_side_effects=_SIDE_EFFECT),
    )(*operands, flight.send_sems, flight.recv_sems, after)
    return list(outs)


def _sum_parts(p_ref):
    g = p_ref[0].astype(F32)
    for dev in range(1, N_DEV):
        g = g + p_ref[dev].astype(F32)
    return g


def _adam_update(g, w, m, v):
    m_new = ADAM_B1 * m + (1.0 - ADAM_B1) * g
    v_new = ADAM_B2 * v + (1.0 - ADAM_B2) * jnp.square(g)
    m_hat = m_new / (1.0 - ADAM_B1 ** ADAM_STEP)
    v_hat = v_new / (1.0 - ADAM_B2 ** ADAM_STEP)
    return -ADAM_LR * (m_hat / (jnp.sqrt(v_hat) + ADAM_EPS) + ADAM_WD * w), m_new, v_new


def _adamw(name, parts, w, m, v, layer=None, into=None):
    r, c = w.shape[-2:]
    tr = 256 if r % 256 == 0 else r
    n_into = 0 if into is None else len(into)

    def body(p_ref, w_ref, m_ref, v_ref, *refs):
        g_ref, d_ref, mo_ref, vo_ref = refs[n_into:]
        g = _sum_parts(p_ref)
        g_ref[...] = g
        d_ref[...], mo_ref[...], vo_ref[...] = _adam_update(g, w_ref[...], m_ref[...], v_ref[...])

    if layer is None:
        blk = pl.BlockSpec((tr, c), lambda i: (i, 0))
    else:
        blk = pl.BlockSpec((None, tr, c), lambda i: (layer, i, 0))
    return pl.pallas_call(
        body, name=name, grid=(r // tr,),
        in_specs=[pl.BlockSpec((N_DEV, tr, c), lambda i: (0, i, 0)), blk, blk, blk] + [pl.BlockSpec(memory_space=pl.ANY)] * n_into,
        out_specs=[blk] * 4, out_shape=[_sds(w.shape, F32)] * 4,
        input_output_aliases={4 + i: i for i in range(n_into)}, compiler_params=_params(1),
    )(parts, w, m, v, *(into or ()))


SMALL_PACK_ROWS = 16
LOSS_ROW = 11


def _small_rows(grads, loss_tile):
    return [(0, 1, grads["g_attn"][0]), (1, 1, grads["g_attn"][1]), (2, 1, grads["g_mlp"][0]), (3, 1, grads["g_mlp"][1]),
            (4, 1, grads["g_kv"]), (5, 1, grads["b_f"]), (6, 1, grads["gq_a"]), (7, 1, grads["gk_a"]), (8, 1, grads["gk_b"]),
            (9, 1, grads["gq_b"]), (10, 1, grads["sinks"]), (LOSS_ROW, 1, loss_tile)]


SMALL_ROWS = {"g_attn": (0, 2), "g_mlp": (2, 2), "g_kv": (4, 1), "b_f": (5, 1), "gq_a": (6, 1), "gk_a": (7, 1),
              "gk_b": (8, 1), "gq_b": (9, 1), "sinks": (10, 1)}


def _pack_small(name, pieces, d):
    def body(*refs):
        out = refs[-1]
        out[...] = jnp.zeros_like(out)
        for (row, rows, _), ref in zip(pieces, refs[:-1]):
            out[row:row + rows, 0:ref.shape[1]] = ref[0:rows, :]

    return pl.pallas_call(body, name=name, out_shape=_sds((SMALL_PACK_ROWS, d), F32), compiler_params=_params(0))(
        *[piece for _, _, piece in pieces])


def _adamw_small(name, parts, parts_rel_bias, w, m, v):
    def body(*refs):
        ins, outs = refs[2:2 + 3 * len(SMALL)], refs[2 + 3 * len(SMALL):]
        pack, rel = _sum_parts(refs[0]), _sum_parts(refs[1])
        for i, k in enumerate(SMALL):
            w_ref, m_ref, v_ref = ins[3 * i:3 * i + 3]
            if k == "rel_bias":
                g = rel
            else:
                row, rows = SMALL_ROWS[k]
                g = pack[row:row + rows, 0:w_ref.shape[1]]
            outs[4 * i][...] = g
            outs[4 * i + 1][...], outs[4 * i + 2][...], outs[4 * i + 3][...] = _adam_update(g, w_ref[...], m_ref[...], v_ref[...])
        outs[-1][...] = pack[LOSS_ROW:LOSS_ROW + 1, 0:LANES]

    operands = [parts, parts_rel_bias] + [t[k] for k in SMALL for t in (w, m, v)]
    out_shape = [_sds(w[k].shape, F32) for k in SMALL for _ in range(4)] + [_sds((1, LANES), F32)]
    outs = pl.pallas_call(body, name=name, out_shape=out_shape, compiler_params=_params(0))(*operands)
    return {k: outs[4 * i:4 * i + 4] for i, k in enumerate(SMALL)}, outs[-1]


class _Comm:
    ORDER = ("w_in_a", "w_out_a", "w_up0", "w_down0", "w_kv", "w_q_b", "w_out_b", "w_up1", "w_down1")

    def __init__(self, shards, d, n_in):
        self.d, self.n_in = d, n_in
        self.flight = _exchange_start("gather_start", [shards[n].astype(BF16) for n in self.ORDER], scatter=False, ks=SAME_CORE)
        self.relays, self.sent = {}, []

    def prefetch(self, names, after):
        which = [self.ORDER.index(n) for n in names]
        landed = _exchange_wait(f"gather_wait_{names[0]}", self.flight, which, self.flight.token if after is None else after)
        relay = _relay_start(f"gather_relay_{names[0]}", landed)
        for n in names:
            self.relays[n] = (relay, names)
        return relay.token

    def weights(self, names, after):
        relay, group = self.relays[names[0]]
        landed = _relay_wait(f"gather_relay_wait_{names[0]}", relay, [group.index(n) for n in names],
                             relay.token if after is None else after)
        return [self._whole(n, g) for n, g in zip(names, landed)]

    def _whole(self, name, g):
        if name == "w_in_a":
            return _join_row_blocks("w_in_join", g, -(-self.n_in // LANES) * LANES)
        if name.startswith("w_up"):
            return g
        return g.reshape(-1, g.shape[-1])

    def _chunks(self, name, g):
        if name == "w_in_a":
            return _split_row_blocks("dw_in_split", g, N_DEV, self.n_in // N_DEV)
        if name.startswith("w_up"):
            return g
        return g.reshape(N_DEV, g.shape[0] // N_DEV, g.shape[1])

    def send_grads(self, tag, partials):
        names = list(partials)
        flight = _exchange_start(f"scatter_start_{tag}", [self._chunks(n, partials[n]) for n in names], scatter=True)
        self.sent.append((tag, flight, names))
        return flight.token[0, 0]

    def received(self, index, after):
        tag, flight, names = self.sent[index]
        landed = _exchange_wait(f"scatter_wait_{tag}", flight, list(range(len(names))), after)
        return dict(zip(names, landed))


def kernel(x, g_attn, g_mlp, w_in_a, b_f, gq_a, gk_a, w_out_a, g_kv, w_kv, gk_b, w_q_b, gq_b, sinks, rel_bias, w_out_b, w_up, w_down, loss_target, m_g_attn, m_g_mlp, m_w_in_a, m_b_f, m_gq_a, m_gk_a, m_w_out_a, m_g_kv, m_w_kv, m_gk_b, m_w_q_b, m_gq_b, m_sinks, m_rel_bias, m_w_out_b, m_w_up, m_w_down, v_g_attn, v_g_mlp, v_w_in_a, v_b_f, v_gq_a, v_gk_a, v_w_out_a, v_g_kv, v_w_kv, v_gk_b, v_w_q_b, v_gq_b, v_sinks, v_rel_bias, v_w_out_b, v_w_up, v_w_down):
    w = dict(g_attn=g_attn, g_mlp=g_mlp, w_in_a=w_in_a, b_f=b_f, gq_a=gq_a, gk_a=gk_a, w_out_a=w_out_a, g_kv=g_kv,
             w_kv=w_kv, gk_b=gk_b, w_q_b=w_q_b, gq_b=gq_b, sinks=sinks, rel_bias=rel_bias, w_out_b=w_out_b,
             w_up=w_up, w_down=w_down)
    mom = dict(g_attn=m_g_attn, g_mlp=m_g_mlp, w_in_a=m_w_in_a, b_f=m_b_f, gq_a=m_gq_a, gk_a=m_gk_a, w_out_a=m_w_out_a,
               g_kv=m_g_kv, w_kv=m_w_kv, gk_b=m_gk_b, w_q_b=m_w_q_b, gq_b=m_gq_b, sinks=m_sinks, rel_bias=m_rel_bias,
               w_out_b=m_w_out_b, w_up=m_w_up, w_down=m_w_down)
    var = dict(g_attn=v_g_attn, g_mlp=v_g_mlp, w_in_a=v_w_in_a, b_f=v_b_f, gq_a=v_gq_a, gk_a=v_gk_a, w_out_a=v_w_out_a,
               g_kv=v_g_kv, w_kv=v_w_kv, gk_b=v_gk_b, w_q_b=v_w_q_b, gq_b=v_gq_b, sinks=v_sinks, rel_bias=v_rel_bias,
               w_out_b=v_w_out_b, w_up=v_w_up, w_down=v_w_down)
    d = x.shape[2]
    where = {"w_in_a": ("w_in_a", 0), "w_out_a": ("w_out_a", 0), "w_kv": ("w_kv", None), "w_q_b": ("w_q_b", 0),
             "w_out_b": ("w_out_b", 0), "w_up0": ("w_up", 0), "w_up1": ("w_up", 1), "w_down0": ("w_down", 0),
             "w_down1": ("w_down", 1)}
    flip = lambda tree: {**tree, "w_in_a": jnp.swapaxes(tree["w_in_a"], 1, 2)}
    w, mom, var = flip(w), flip(mom), flip(var)
    shards = {n: (w[k] if layer is None else w[k][layer]) for n, (k, layer) in where.items()}
    comm = _Comm(shards, d, w_in_a.shape[2] * N_DEV)
    loss_tile, grad_x, grads = _local_step(x[0], loss_target[0], {k: w[k] for k in SMALL}, comm)

    small_flight = _exchange_start(
        "gather_small_grads", [_pack_small("pack_small", _small_rows(grads, loss_tile), d), grads["rel_bias"]], scatter=False)
    res, after = {}, small_flight.token
    for index in range(len(comm.sent)):
        for n, parts in comm.received(index, after).items():
            k, layer = where[n]
            res[k] = _adamw(f"adam_{n}", parts, w[k], mom[k], var[k], layer, res.get(k))
            after = res[k][0]
    as_rows = lambda tree: {k: tree[k] if tree[k].ndim == 2 else tree[k].reshape(1, -1) for k in SMALL}
    small, loss_row = _adamw_small("adam_small", *_exchange_wait("gather_small_wait", small_flight, [0, 1], after),
                                   as_rows(w), as_rows(mom), as_rows(var))
    loss = loss_row[0, 0]
    for k in SMALL:
        res[k] = [a.reshape(w[k].shape) for a in small[k]]
    res["w_in_a"] = [jnp.swapaxes(a, 1, 2) for a in res["w_in_a"]]

    outs = [loss, grad_x[None]]
    for i in range(4):
        outs.extend(res[k][i] for k in WEIGHTS)
    return tuple(outs)
```

```python
import numpy as np
import jax
import jax.numpy as jnp
from jax import lax
from jax.experimental import pallas as pl
from jax.experimental.pallas import tpu as pltpu

F32 = jnp.float32
BF16 = jnp.bfloat16

N_DEV = 8
HEAD_DIM = 64
WINDOW = 128
N_BUCKETS = 32
REL_MAX_DIST = 128
NORM_EPS = 1e-6
NEG = -1e30
LANES = 128
VMEM_LIMIT = 56 * 1024 * 1024

ADAM_LR = 0.001
ADAM_B1 = 0.9
ADAM_B2 = 0.999
ADAM_EPS = 1e-08
ADAM_WD = 0.01
ADAM_STEP = 10

SMALL = ("g_attn", "g_mlp", "b_f", "gq_a", "gk_a", "g_kv", "gk_b", "gq_b", "sinks", "rel_bias")
WEIGHTS = ("g_attn", "g_mlp", "w_in_a", "b_f", "gq_a", "gk_a", "w_out_a", "g_kv", "w_kv", "gk_b",
           "w_q_b", "gq_b", "sinks", "rel_bias", "w_out_b", "w_up", "w_down")


def _params(n_grid):
    return pltpu.CompilerParams(dimension_semantics=("arbitrary",) * n_grid, vmem_limit_bytes=VMEM_LIMIT)


def _sds(shape, dtype):
    return jax.ShapeDtypeStruct(tuple(shape), dtype)


def _after_operand(after):
    if after is None:
        return [], []
    return [pl.BlockSpec((8, LANES), lambda *_: (0, 0))], [after]


def _rms(x, g):
    return (x * lax.rsqrt(jnp.mean(x * x, axis=-1, keepdims=True) + NORM_EPS)) * g


def _dot_nt(a, b):
    return lax.dot_general(a, b, (((1,), (1,)), ((), ())), preferred_element_type=F32)


def _dot_tn(a, b):
    return lax.dot_general(a, b, (((0,), (0,)), ((), ())), preferred_element_type=F32)


def _dot(a, b):
    return jnp.dot(a, b, preferred_element_type=F32)


def _dot_exact(a, b):
    return jnp.dot(a, b, preferred_element_type=F32, precision=lax.Precision.HIGHEST)


def _norm_matmul(name, x, g, w, *, tn=None, relu2=False, w_rows=False, after=None):
    t, d = x.shape
    blocked = w.ndim == 3
    if blocked:
        tn = w.shape[2]
        n = w.shape[0] * tn
        w_spec = pl.BlockSpec((None, d, tn), lambda i, j: (j, 0, 0))
    elif w_rows:
        n = w.shape[0]
        w_spec = pl.BlockSpec((tn, d), lambda i, j: (j, 0))
    else:
        n = w.shape[1]
        w_spec = pl.BlockSpec((d, tn), lambda i, j: (0, j))
    tm = min(1024, t)

    def body(x_ref, g_ref, w_ref, *rest):
        y_ref, xn_ref = rest[-2:]

        @pl.when(pl.program_id(1) == 0)
        def _():
            xn_ref[...] = _rms(x_ref[...], g_ref[...]).astype(BF16)

        y = _dot_nt(xn_ref[...], w_ref[...]) if w_rows else _dot(xn_ref[...], w_ref[...])
        y_ref[...] = jnp.square(jnp.maximum(y, 0.0)).astype(BF16) if relu2 else y

    extra_specs, extra = _after_operand(after)
    out_shape = [_sds((t, n), BF16 if relu2 else F32), _sds((t, d), BF16)]
    out_specs = [pl.BlockSpec((tm, tn), lambda i, j: (i, j)), pl.BlockSpec((tm, d), lambda i, j: (i, 0))]
    return pl.pallas_call(
        body, name=name, grid=(t // tm, n // tn),
        in_specs=[pl.BlockSpec((tm, d), lambda i, j: (i, 0)), pl.BlockSpec((1, d), lambda i, j: (0, 0)), w_spec] + extra_specs,
        out_specs=out_specs, out_shape=out_shape, compiler_params=_params(2),
    )(x, g.reshape(1, d), w, *extra)


def _matmul_res(name, a, w, res, *, tn=512, after=None):
    t, k = a.shape
    n = w.shape[1]
    tm = min(1024, t)

    def body(a_ref, w_ref, r_ref, *rest):
        rest[-1][...] = r_ref[...] + _dot(a_ref[...], w_ref[...])

    extra_specs, extra = _after_operand(after)
    return pl.pallas_call(
        body, name=name, grid=(t // tm, n // tn),
        in_specs=[pl.BlockSpec((tm, k), lambda i, j: (i, 0)), pl.BlockSpec((k, tn), lambda i, j: (0, j)),
                  pl.BlockSpec((tm, tn), lambda i, j: (i, j))] + extra_specs,
        out_specs=pl.BlockSpec((tm, tn), lambda i, j: (i, j)), out_shape=_sds((t, n), F32),
        compiler_params=_params(2),
    )(a, w, res, *extra)


def _matmul_nt(name, dy, w, *, a=None, tk=1024):
    t, n = dy.shape
    k = w.shape[0]
    tm = min(1024, t)

    def body(dy_ref, w_ref, *rest):
        o_ref = rest[-1]
        r = _dot_nt(dy_ref[...].astype(BF16), w_ref[...])
        if a is not None:
            r = r * (2.0 * jnp.sqrt(rest[0][...].astype(F32)))
        o_ref[...] = r.astype(BF16)

    in_specs = [pl.BlockSpec((tm, n), lambda i, j: (i, 0)), pl.BlockSpec((tk, n), lambda i, j: (j, 0))]
    args = [dy, w]
    if a is not None:
        in_specs.append(pl.BlockSpec((tm, tk), lambda i, j: (i, j)))
        args.append(a)
    return pl.pallas_call(
        body, name=name, grid=(t // tm, k // tk), in_specs=in_specs,
        out_specs=pl.BlockSpec((tm, tk), lambda i, j: (i, j)), out_shape=_sds((t, k), BF16),
        compiler_params=_params(2),
    )(*args)


def _matmul_nt_rmsbwd(name, dy, w, x, g, dres, *, w_rows=False, after=None):
    t, k = dy.shape
    blocked = w.ndim == 3
    d = w.shape[1] if blocked or w_rows else w.shape[0]
    tm = min(512, t)

    def body(dy_ref, w_ref, x_ref, g_ref, r_ref, *rest):
        dx_ref, dg_ref = rest[-2:]
        if blocked:
            kb = w.shape[2]
            dxn = _dot_nt(dy_ref[:, 0:kb].astype(BF16), w_ref[0])
            for j in range(1, w.shape[0]):
                dxn += _dot_nt(dy_ref[:, j * kb:(j + 1) * kb].astype(BF16), w_ref[j])
        elif w_rows:
            dxn = _dot(dy_ref[...].astype(BF16), w_ref[...])
        else:
            dxn = _dot_nt(dy_ref[...].astype(BF16), w_ref[...])
        _, vjp = jax.vjp(_rms, x_ref[...], g_ref[...])
        dx, dg = vjp(dxn)
        dx_ref[...] = r_ref[...] + dx

        @pl.when(pl.program_id(0) == 0)
        def _():
            dg_ref[...] = jnp.zeros_like(dg_ref)

        dg_ref[...] += dg

    row = lambda i: (i, 0)
    fixed = lambda i: (0, 0)
    extra_specs, extra = _after_operand(after)
    return pl.pallas_call(
        body, name=name, grid=(t // tm,),
        in_specs=[pl.BlockSpec((tm, k), row), pl.BlockSpec(w.shape, (lambda i: (0, 0, 0)) if blocked else fixed),
                  pl.BlockSpec((tm, d), row), pl.BlockSpec((1, d), fixed), pl.BlockSpec((tm, d), row)] + extra_specs,
        out_specs=[pl.BlockSpec((tm, d), row), pl.BlockSpec((1, d), fixed)],
        out_shape=[_sds((t, d), F32), _sds((1, d), F32)], compiler_params=_params(1),
    )(dy, w, x, g.reshape(1, d), dres, *extra)


def _matmul_tn(name, a, b, *, tk=1024, tn=1024, col_blocks=None):
    t, k = a.shape
    n = b.shape[1]
    tk = min(tk, k)
    if col_blocks:
        tn = n // col_blocks
        out_spec, out_shape = pl.BlockSpec((None, tk, tn), lambda i, j: (j, i, 0)), _sds((col_blocks, k, tn), BF16)
    else:
        tn = min(tn, n)
        out_spec, out_shape = pl.BlockSpec((tk, tn), lambda i, j: (i, j)), _sds((k, n), BF16)

    def body(a_ref, b_ref, o_ref):
        o_ref[...] = _dot_tn(a_ref[...].astype(BF16), b_ref[...].astype(BF16)).astype(BF16)

    return pl.pallas_call(
        body, name=name, grid=(k // tk, n // tn),
        in_specs=[pl.BlockSpec((t, tk), lambda i, j: (0, i)), pl.BlockSpec((t, tn), lambda i, j: (0, j))],
        out_specs=out_spec, out_shape=out_shape, compiler_params=_params(2),
    )(a, b)


def _join_row_blocks(name, blocks, rows):
    b, r, c = blocks.shape
    tc = min(256, c)

    def body(g_ref, o_ref):
        o_ref[...] = jnp.zeros_like(o_ref)
        for j in range(b):
            o_ref[r * j:r * (j + 1), :] = g_ref[j]

    return pl.pallas_call(
        body, name=name, grid=(c // tc,), in_specs=[pl.BlockSpec((b, r, tc), lambda i: (0, 0, i))],
        out_specs=pl.BlockSpec((rows, tc), lambda i: (0, i)), out_shape=_sds((rows, c), blocks.dtype),
        compiler_params=_params(1),
    )(blocks)


def _split_row_blocks(name, mat, b, r):
    rows, c = mat.shape
    tc = min(256, c)

    def body(w_ref, o_ref):
        for j in range(b):
            o_ref[j] = w_ref[r * j:r * (j + 1), :]

    return pl.pallas_call(
        body, name=name, grid=(c // tc,), in_specs=[pl.BlockSpec((rows, tc), lambda i: (0, i))],
        out_specs=pl.BlockSpec((b, r, tc), lambda i: (0, 0, i)), out_shape=_sds((b, r, c), mat.dtype),
        compiler_params=_params(1),
    )(mat)


def _matmul_res_loss(name, a, w, res, target, *, tn=512):
    t, k = a.shape
    n = w.shape[1]
    tm = min(1024, t)

    def body(a_ref, w_ref, r_ref, t_ref, dy_ref, l_ref):
        e = r_ref[...] + _dot(a_ref[...], w_ref[...]) - t_ref[...]
        dy_ref[...] = e * (1.0 / n)

        @pl.when((pl.program_id(0) == 0) & (pl.program_id(1) == 0))
        def _():
            l_ref[...] = jnp.zeros_like(l_ref)

        l_ref[...] += (0.5 / n) * jnp.sum(e * e)

    tile = pl.BlockSpec((tm, tn), lambda i, j: (i, j))
    return pl.pallas_call(
        body, name=name, grid=(t // tm, n // tn),
        in_specs=[pl.BlockSpec((tm, k), lambda i, j: (i, 0)), pl.BlockSpec((k, tn), lambda i, j: (0, j)), tile, tile],
        out_specs=[tile, pl.BlockSpec((8, LANES), lambda i, j: (0, 0))],
        out_shape=[_sds((t, n), F32), _sds((8, LANES), F32)], compiler_params=_params(2),
    )(a, w, res, target)


def _gate_fwd(name, proj, b_pad, n_heads, gate_col, after=None):
    t = proj.shape[0]
    tb = min(256, t)
    tri = jnp.asarray(np.tril(np.ones((tb, tb), np.float32)))
    extra_specs, extra = _after_operand(after)

    def body(p_ref, b_ref, tri_ref, *rest):
        c_ref, carry = rest[-2:]

        @pl.when(pl.program_id(0) == 0)
        def _():
            carry[...] = jnp.zeros_like(carry)

        lane = lax.broadcasted_iota(jnp.int32, (tb, LANES), 1)
        lf = jnp.where(lane < n_heads, jax.nn.log_sigmoid(p_ref[...] + b_ref[...]), 0.0)
        c = _dot_exact(tri_ref[...], lf) + carry[0:1, :]
        c_ref[...] = c
        carry[...] = jnp.broadcast_to(c[tb - 1:tb, :], carry.shape)

    return pl.pallas_call(
        body, name=name, grid=(t // tb,),
        in_specs=[pl.BlockSpec((tb, LANES), lambda i: (i, gate_col)), pl.BlockSpec((1, LANES), lambda i: (0, 0)),
                  pl.BlockSpec((tb, tb), lambda i: (0, 0))] + extra_specs,
        out_specs=pl.BlockSpec((tb, LANES), lambda i: (i, 0)), out_shape=_sds((t, LANES), F32),
        scratch_shapes=[pltpu.VMEM((8, LANES), F32)], compiler_params=_params(1),
    )(proj, b_pad, tri, *extra)


def _gate_bwd(name, proj, b_pad, dc, n_heads, gate_col):
    t = proj.shape[0]
    tb = min(256, t)
    nb = t // tb
    triu = jnp.asarray(np.triu(np.ones((tb, tb), np.float32)))

    def body(p_ref, b_ref, dc_ref, tri_ref, df_ref, db_ref, carry):
        @pl.when(pl.program_id(0) == 0)
        def _():
            carry[...] = jnp.zeros_like(carry)
            db_ref[...] = jnp.zeros_like(db_ref)

        dcv = dc_ref[...]
        dlf = _dot_exact(tri_ref[...], dcv) + carry[0:1, :]
        carry[...] = jnp.broadcast_to(dlf[0:1, :], carry.shape)
        lane = lax.broadcasted_iota(jnp.int32, (tb, LANES), 1)
        z = p_ref[...] + b_ref[...]
        df = jnp.where(lane < n_heads, dlf / (1.0 + jnp.exp(z)), 0.0)
        df_ref[...] = df.astype(BF16)
        db_ref[...] += jnp.sum(df, axis=0, keepdims=True)

    return pl.pallas_call(
        body, name=name, grid=(nb,),
        in_specs=[pl.BlockSpec((tb, LANES), lambda i: (nb - 1 - i, gate_col)), pl.BlockSpec((1, LANES), lambda i: (0, 0)),
                  pl.BlockSpec((tb, LANES), lambda i: (nb - 1 - i, 0)), pl.BlockSpec((tb, tb), lambda i: (0, 0))],
        out_specs=[pl.BlockSpec((tb, LANES), lambda i: (nb - 1 - i, 0)), pl.BlockSpec((1, LANES), lambda i: (0, 0))],
        out_shape=[_sds((t, LANES), BF16), _sds((1, LANES), F32)],
        scratch_shapes=[pltpu.VMEM((8, LANES), F32)], compiler_params=_params(1),
    )(proj, b_pad, dc, triu)


def _qhead(qp, g):
    return _rms(qp, g) * (HEAD_DIM ** -0.5)


def _column(mat, idx):
    lane = lax.broadcasted_iota(jnp.int32, mat.shape, 1)
    return jnp.sum(jnp.where(lane == idx, mat, 0.0), axis=1, keepdims=True)


def _fox_scores(kk, qi, ckey, cq_i, i, bq):
    length = kk.shape[0]
    s = _dot_nt(kk, qi) + cq_i - ckey[:length]
    key = lax.broadcasted_iota(jnp.int32, (length, bq), 0)
    qry = lax.broadcasted_iota(jnp.int32, (length, bq), 1) + i * bq
    return jnp.where(key <= qry, s, NEG)


def _fox_fwd(name, proj, c, crow, gq, gk, n_heads):
    t = proj.shape[0]
    hw = n_heads * HEAD_DIM
    npair = n_heads // 2
    bq = min(512, t)
    nq = t // bq

    def body(q_ref, k_ref, v_ref, c_ref, crow_ref, gq_ref, gk_ref, o_ref, lse_ref):
        hp = pl.program_id(0)
        lse_ref[...] = jnp.zeros_like(lse_ref)
        outs = []
        for hh in range(2):
            sl = slice(hh * HEAD_DIM, (hh + 1) * HEAD_DIM)
            qn = _qhead(q_ref[:, sl], gq_ref[...]).astype(BF16)
            kn = _rms(k_ref[:, sl], gk_ref[...]).astype(BF16)
            v_t = v_ref[:, sl].T.astype(BF16)
            ckey = _column(c_ref[...], 2 * hp + hh)
            cq = crow_ref[0, hh:hh + 1, :]
            o_blocks = []
            for i in range(nq):
                cols = slice(i * bq, (i + 1) * bq)
                length = (i + 1) * bq
                s = _fox_scores(kn[:length], qn[cols], ckey, cq[:, cols], i, bq)
                m = jnp.max(s, axis=0, keepdims=True)
                p = jnp.exp(s - m)
                l = jnp.sum(p, axis=0, keepdims=True)
                o_blocks.append((_dot(v_t[:, :length], p.astype(BF16)) / l).T)
                lse_ref[0, hh:hh + 1, cols] = m + jnp.log(l)
            outs.append(jnp.concatenate(o_blocks, axis=0))
        o_ref[...] = jnp.concatenate(outs, axis=1).astype(BF16)

    col = lambda off: (lambda h: (0, off + h))
    fixed = lambda h: (0, 0)
    return pl.pallas_call(
        body, name=name, grid=(npair,),
        in_specs=[pl.BlockSpec((t, LANES), col(0)), pl.BlockSpec((t, LANES), col(npair)), pl.BlockSpec((t, LANES), col(2 * npair)),
                  pl.BlockSpec((t, LANES), fixed), pl.BlockSpec((1, 2, t), lambda h: (h, 0, 0)),
                  pl.BlockSpec((1, HEAD_DIM), fixed), pl.BlockSpec((1, HEAD_DIM), fixed)],
        out_specs=[pl.BlockSpec((t, LANES), col(0)), pl.BlockSpec((1, 8, t), lambda h: (h, 0, 0))],
        out_shape=[_sds((t, hw), BF16), _sds((npair, 8, t), F32)], compiler_params=_params(1),
    )(proj, proj, proj, c, crow, gq, gk)


def _fox_bwd(name, proj, c, crow, gq, gk, lse, do, n_heads, after=None):
    t = proj.shape[0]
    hw = n_heads * HEAD_DIM
    npair = n_heads // 2
    bq = min(256, t)
    nq = t // bq

    def body(q_ref, k_ref, v_ref, c_ref, crow_ref, gq_ref, gk_ref, lse_ref, do_ref, *rest):
        dq_ref, dk_ref, dv_ref, dc_ref, dgq_ref, dgk_ref, dk_acc, dv_acc, dc_acc = rest[-9:]
        hp = pl.program_id(0)

        @pl.when(hp == 0)
        def _():
            dgq_ref[...] = jnp.zeros_like(dgq_ref)
            dgk_ref[...] = jnp.zeros_like(dgk_ref)
            dc_ref[...] = jnp.zeros_like(dc_ref)

        lane = lax.broadcasted_iota(jnp.int32, (t, LANES), 1)
        dqs, dks, dvs = [], [], []
        for hh in range(2):
            sl = slice(hh * HEAD_DIM, (hh + 1) * HEAD_DIM)
            qf, q_vjp = jax.vjp(_qhead, q_ref[:, sl], gq_ref[...])
            kf, k_vjp = jax.vjp(_rms, k_ref[:, sl], gk_ref[...])
            qn, kn, kn_t = qf.astype(BF16), kf.astype(BF16), kf.T.astype(BF16)
            vb = v_ref[:, sl].astype(BF16)
            dob = do_ref[:, sl]
            ckey = _column(c_ref[...], 2 * hp + hh)
            cq = crow_ref[0, hh:hh + 1, :]
            lse_h = lse_ref[0, hh:hh + 1, :]
            dk_acc[...] = jnp.zeros_like(dk_acc)
            dv_acc[...] = jnp.zeros_like(dv_acc)
            dc_acc[...] = jnp.zeros_like(dc_acc)
            dq_blocks = []
            for i in range(nq):
                cols = slice(i * bq, (i + 1) * bq)
                length = (i + 1) * bq
                qi, doi = qn[cols], dob[cols]
                s = _fox_scores(kn[:length], qi, ckey, cq[:, cols], i, bq)
                p = jnp.exp(s - lse_h[:, cols])
                dp = _dot_nt(vb[:length], doi)
                ds = p * (dp - jnp.sum(p * dp, axis=0, keepdims=True))
                dsb = ds.astype(BF16)
                dq_blocks.append(_dot(kn_t[:, :length], dsb).T)
                dk_acc[0:length, :] += _dot(dsb, qi)
                dv_acc[0:length, :] += _dot(p.astype(BF16), doi)
                part = ds[:, 0:LANES]
                for j in range(1, bq // LANES):
                    part = part + ds[:, j * LANES:(j + 1) * LANES]
                dc_acc[0:length, :] += part
            dqp, dgq = q_vjp(jnp.concatenate(dq_blocks, axis=0))
            dkp, dgk = k_vjp(dk_acc[...])
            dgq_ref[...] += dgq
            dgk_ref[...] += dgk
            dqs.append(dqp)
            dks.append(dkp)
            dvs.append(dv_acc[...])
            dc_ref[...] = jnp.where(lane == 2 * hp + hh, -jnp.sum(dc_acc[...], axis=1, keepdims=True), dc_ref[...])
        dq_ref[...] = jnp.concatenate(dqs, axis=1).astype(BF16)
        dk_ref[...] = jnp.concatenate(dks, axis=1).astype(BF16)
        dv_ref[...] = jnp.concatenate(dvs, axis=1).astype(BF16)

    col = lambda off: (lambda h: (0, off + h))
    fixed = lambda h: (0, 0)
    pair_blk = pl.BlockSpec((t, LANES), col(0))
    extra_specs, extra = _after_operand(after)
    return pl.pallas_call(
        body, name=name, grid=(npair,),
        in_specs=[pl.BlockSpec((t, LANES), col(0)), pl.BlockSpec((t, LANES), col(npair)), pl.BlockSpec((t, LANES), col(2 * npair)),
                  pl.BlockSpec((t, LANES), fixed), pl.BlockSpec((1, 2, t), lambda h: (h, 0, 0)),
                  pl.BlockSpec((1, HEAD_DIM), fixed), pl.BlockSpec((1, HEAD_DIM), fixed),
                  pl.BlockSpec((1, 8, t), lambda h: (h, 0, 0)), pair_blk] + extra_specs,
        out_specs=[pair_blk, pair_blk, pair_blk, pl.BlockSpec((t, LANES), fixed),
                   pl.BlockSpec((1, HEAD_DIM), fixed), pl.BlockSpec((1, HEAD_DIM), fixed)],
        out_shape=[_sds((t, hw), BF16), _sds((t, hw), BF16), _sds((t, hw), BF16), _sds((t, LANES), F32),
                   _sds((1, HEAD_DIM), F32), _sds((1, HEAD_DIM), F32)],
        scratch_shapes=[pltpu.VMEM((t, HEAD_DIM), F32), pltpu.VMEM((t, HEAD_DIM), F32), pltpu.VMEM((t, LANES), F32)],
        compiler_params=_params(1),
    )(proj, proj, proj, c, crow, gq, gk, lse, do, *extra)


def _t5_bucket_table():
    dist = np.arange(WINDOW)[None, :] + WINDOW - np.arange(2 * WINDOW)[:, None]
    n = np.maximum(dist, 0)
    max_exact = N_BUCKETS // 2
    large = max_exact + (np.log(np.maximum(n, 1) / max_exact) / np.log(REL_MAX_DIST / max_exact)
                         * (N_BUCKETS - max_exact)).astype(np.int32)
    large = np.minimum(large, N_BUCKETS - 1)
    return np.where(n < max_exact, n, large).astype(np.int32).reshape(1, -1)


def _bias_expand(name, rel_bias_t):
    n_heads = rel_bias_t.shape[0]
    tbl = jnp.asarray(_t5_bucket_table())
    width = tbl.shape[1]

    def body(rb_ref, tbl_ref, o_ref):
        onehot = (lax.broadcasted_iota(jnp.int32, (N_BUCKETS, width), 0) == tbl_ref[...]).astype(F32)
        o_ref[...] = _dot_exact(rb_ref[...], onehot)

    return pl.pallas_call(body, name=name, out_shape=_sds((n_heads, width), F32), compiler_params=_params(0))(rel_bias_t, tbl)


def _bias_reduce(name, dbias):
    n_heads, width = dbias.shape
    tbl = jnp.asarray(_t5_bucket_table())

    def body(db_ref, tbl_ref, o_ref):
        onehot = (lax.broadcasted_iota(jnp.int32, (N_BUCKETS, width), 0) == tbl_ref[...]).astype(F32)
        o_ref[...] = lax.dot_general(db_ref[...], onehot, (((1,), (1,)), ((), ())), preferred_element_type=F32,
                                     precision=lax.Precision.HIGHEST)

    return pl.pallas_call(body, name=name, out_shape=_sds((n_heads, N_BUCKETS), F32), compiler_params=_params(0))(dbias, tbl)


def _swa_mask(n, group):
    j = lax.broadcasted_iota(jnp.int32, (2 * WINDOW, group * WINDOW), 0)
    i = lax.broadcasted_iota(jnp.int32, (2 * WINDOW, group * WINDOW), 1) & (WINDOW - 1)
    ok = (j > i) & (j <= i + WINDOW) & ((n > 0) | (j >= WINDOW))
    return jnp.where(ok, 0.0, NEG)


def _swa_stack(ref, start, group):
    return jnp.concatenate([ref[pl.ds(start, WINDOW), g * HEAD_DIM:(g + 1) * HEAD_DIM] for g in range(group)], axis=0)


def _kv_head(ref, n_kv):
    out = ref[:, 0:HEAD_DIM]
    for h in range(1, n_kv):
        out = jnp.where(pl.program_id(0) == h, ref[:, h * HEAD_DIM:(h + 1) * HEAD_DIM], out)
    return out


def _swa_fwd(name, qb, kv, gq, gk, sinks, bias, group):
    t = qb.shape[0]
    kvh = kv.shape[1] // (2 * HEAD_DIM)
    nblk = t // WINDOW
    gw = group * HEAD_DIM
    band = 2 * WINDOW
    cols = group * WINDOW

    def body(q_ref, k_ref, v_ref, gq_ref, gk_ref, sink_ref, bias_ref, o_ref, lse_ref, qs, kpad, vpad):
        for g in range(group):
            qs[:, g * HEAD_DIM:(g + 1) * HEAD_DIM] = _qhead(q_ref[:, g * HEAD_DIM:(g + 1) * HEAD_DIM], gq_ref[...]).astype(BF16)
        kpad[0:WINDOW, :] = jnp.zeros((WINDOW, HEAD_DIM), BF16)
        vpad[0:WINDOW, :] = jnp.zeros((WINDOW, HEAD_DIM), BF16)
        kpad[WINDOW:, :] = _rms(_kv_head(k_ref, kvh), gk_ref[...]).astype(BF16)
        vpad[WINDOW:, :] = _kv_head(v_ref, kvh).astype(BF16)
        sink = sink_ref[0]

        def block(n, carry):
            start = pl.multiple_of(n * WINDOW, WINDOW)
            kb = kpad[pl.ds(start, band), :]
            vb = vpad[pl.ds(start, band), :]
            s = _dot_nt(kb, _swa_stack(qs, start, group)) + bias_ref[0] + _swa_mask(n, group)
            m = jnp.maximum(jnp.max(s, axis=0, keepdims=True), sink)
            e = jnp.exp(s - m)
            l = jnp.sum(e, axis=0, keepdims=True) + jnp.exp(sink - m)
            o_t = _dot_tn(vb, e.astype(BF16)) / l
            for g in range(group):
                o_ref[pl.ds(start, WINDOW), g * HEAD_DIM:(g + 1) * HEAD_DIM] = o_t[:, g * WINDOW:(g + 1) * WINDOW].T.astype(BF16)
            lse_ref[pl.ds(n, 1), :] = m + jnp.log(l)
            return carry

        lax.fori_loop(0, nblk, block, 0)

    fixed = lambda h: (0, 0)
    per = lambda h: (h, 0, 0)
    return pl.pallas_call(
        body, name=name, grid=(kvh,),
        in_specs=[pl.BlockSpec((t, gw), lambda h: (0, h)), pl.BlockSpec((t, kvh * HEAD_DIM), lambda h: (0, 0)),
                  pl.BlockSpec((t, kvh * HEAD_DIM), lambda h: (0, 1)),
                  pl.BlockSpec((1, HEAD_DIM), fixed), pl.BlockSpec((1, HEAD_DIM), fixed),
                  pl.BlockSpec((1, 1, cols), per), pl.BlockSpec((1, band, cols), per)],
        out_specs=[pl.BlockSpec((t, gw), lambda h: (0, h)), pl.BlockSpec((nblk, cols), lambda h: (h, 0))],
        out_shape=[_sds((t, kvh * gw), BF16), _sds((kvh * nblk, cols), F32)],
        scratch_shapes=[pltpu.VMEM((t, gw), BF16), pltpu.VMEM((t + WINDOW, HEAD_DIM), BF16),
                        pltpu.VMEM((t + WINDOW, HEAD_DIM), BF16)],
        compiler_params=_params(1),
    )(qb, kv, kv, gq, gk, sinks, bias)


def _swa_bwd(name, qb, kv, gq, gk, sinks, bias, lse, do, group):
    t = qb.shape[0]
    kvh = kv.shape[1] // (2 * HEAD_DIM)
    kvw = kvh * HEAD_DIM
    nblk = t // WINDOW
    gw = group * HEAD_DIM
    band = 2 * WINDOW
    cols = group * WINDOW

    def body(q_ref, k_ref, v_ref, gq_ref, gk_ref, sink_ref, bias_ref, lse_ref, do_ref,
             dq_ref, dkv_ref, dgq_ref, dgk_ref, dsink_ref, dbias_ref,
             qs, kpad, vpad, dqs, dk_acc, dv_acc, dsink_acc):
        @pl.when(pl.program_id(0) == 0)
        def _():
            dgq_ref[...] = jnp.zeros_like(dgq_ref)
            dgk_ref[...] = jnp.zeros_like(dgk_ref)
            dkv_ref[...] = jnp.zeros_like(dkv_ref)

        for g in range(group):
            qs[:, g * HEAD_DIM:(g + 1) * HEAD_DIM] = _qhead(q_ref[:, g * HEAD_DIM:(g + 1) * HEAD_DIM], gq_ref[...]).astype(BF16)
        kpad[0:WINDOW, :] = jnp.zeros((WINDOW, HEAD_DIM), BF16)
        vpad[0:WINDOW, :] = jnp.zeros((WINDOW, HEAD_DIM), BF16)
        kpad[WINDOW:, :] = _rms(_kv_head(k_ref, kvh), gk_ref[...]).astype(BF16)
        vpad[WINDOW:, :] = _kv_head(v_ref, kvh).astype(BF16)
        dk_acc[...] = jnp.zeros_like(dk_acc)
        dv_acc[...] = jnp.zeros_like(dv_acc)
        dsink_acc[...] = jnp.zeros_like(dsink_acc)
        dbias_ref[...] = jnp.zeros_like(dbias_ref)
        sink = sink_ref[0]

        def block(n, carry):
            start = pl.multiple_of(n * WINDOW, WINDOW)
            kb = kpad[pl.ds(start, band), :]
            vb = vpad[pl.ds(start, band), :]
            q = _swa_stack(qs, start, group)
            dob = _swa_stack(do_ref, start, group)
            lse_n = lse_ref[pl.ds(n, 1), :]
            s = _dot_nt(kb, q) + bias_ref[0] + _swa_mask(n, group)
            p = jnp.exp(s - lse_n)
            dp = _dot_nt(vb, dob)
            dsum = jnp.sum(p * dp, axis=0, keepdims=True)
            ds = p * (dp - dsum)
            dsb = ds.astype(BF16)
            dsink_acc[...] -= jnp.exp(sink - lse_n) * dsum
            dbias_ref[0] += ds
            dq = _dot_tn(dsb, kb)
            for g in range(group):
                dqs[pl.ds(start, WINDOW), g * HEAD_DIM:(g + 1) * HEAD_DIM] = dq[g * WINDOW:(g + 1) * WINDOW]
            dk_acc[pl.ds(start, band), :] += _dot(dsb, q)
            dv_acc[pl.ds(start, band), :] += _dot(p.astype(BF16), dob)
            return carry

        lax.fori_loop(0, nblk, block, 0)
        for g in range(group):
            _, q_vjp = jax.vjp(_qhead, q_ref[:, g * HEAD_DIM:(g + 1) * HEAD_DIM], gq_ref[...])
            dqp, dgq = q_vjp(dqs[:, g * HEAD_DIM:(g + 1) * HEAD_DIM])
            dq_ref[:, g * HEAD_DIM:(g + 1) * HEAD_DIM] = dqp.astype(BF16)
            dgq_ref[...] += dgq
            dsink_g = jnp.sum(dsink_acc[:, g * WINDOW:(g + 1) * WINDOW], axis=1, keepdims=True)
            dsink_ref[0, g:g + 1, :] = jnp.broadcast_to(dsink_g, (1, LANES))
        _, k_vjp = jax.vjp(_rms, _kv_head(k_ref, kvh), gk_ref[...])
        dkp, dgk = k_vjp(dk_acc[WINDOW:, :])
        dgk_ref[...] += dgk
        mine = lax.broadcasted_iota(jnp.int32, (t, kvw), 1) // HEAD_DIM == pl.program_id(0)
        dkv_ref[:, 0:kvw] = jnp.where(mine, jnp.concatenate([dkp] * kvh, axis=1), dkv_ref[:, 0:kvw])
        dkv_ref[:, kvw:] = jnp.where(mine, jnp.concatenate([dv_acc[WINDOW:, :]] * kvh, axis=1), dkv_ref[:, kvw:])

    fixed = lambda h: (0, 0)
    per = lambda h: (h, 0, 0)
    wide = pl.BlockSpec((t, gw), lambda h: (0, h))
    vec = pl.BlockSpec((1, HEAD_DIM), fixed)
    bias_spec = pl.BlockSpec((1, band, cols), per)
    return pl.pallas_call(
        body, name=name, grid=(kvh,),
        in_specs=[wide, pl.BlockSpec((t, kvw), lambda h: (0, 0)), pl.BlockSpec((t, kvw), lambda h: (0, 1)), vec, vec,
                  pl.BlockSpec((1, 1, cols), per), bias_spec, pl.BlockSpec((nblk, cols), lambda h: (h, 0)), wide],
        out_specs=[wide, pl.BlockSpec((t, 2 * kvw), fixed), vec, vec, pl.BlockSpec((1, group, LANES), per), bias_spec],
        out_shape=[_sds((t, kvh * gw), BF16), _sds((t, 2 * kvw), F32),
                   _sds((1, HEAD_DIM), F32), _sds((1, HEAD_DIM), F32),
                   _sds((kvh, group, LANES), F32), _sds((kvh, band, cols), F32)],
        scratch_shapes=[pltpu.VMEM((t, gw), BF16), pltpu.VMEM((t + WINDOW, HEAD_DIM), BF16),
                        pltpu.VMEM((t + WINDOW, HEAD_DIM), BF16), pltpu.VMEM((t, gw), F32),
                        pltpu.VMEM((t + WINDOW, HEAD_DIM), F32), pltpu.VMEM((t + WINDOW, HEAD_DIM), F32),
                        pltpu.VMEM((1, cols), F32)],
        compiler_params=_params(1),
    )(qb, kv, kv, gq, gk, sinks, bias, lse, do)


def _local_step(x, target, p, comm):
    t, d = x.shape
    n_heads = d // HEAD_DIM
    kv_heads = n_heads // 8
    group = n_heads // kv_heads
    hw = n_heads * HEAD_DIM
    gate_col = 3 * hw // LANES
    kvw = kv_heads * HEAD_DIM
    grads = {}

    def mlp_fwd(tag, h, g, layer, last=False):
        w_up, = comm.weights([f"w_up{layer}"], h)
        a, hn = _norm_matmul(f"{tag}_up", h, g, w_up, relu2=True)
        w_down, = comm.weights([f"w_down{layer}"], a)
        out = _matmul_res_loss(f"{tag}_down", a, w_down, h, target) if last else _matmul_res(f"{tag}_down", a, w_down, h)
        return out, (h, g, hn, a, w_up, w_down)

    def mlp_bwd(tag, saved, layer, dy):
        h, g, hn, a, w_up, w_down = saved
        du = _matmul_nt(f"{tag}_du", dy, w_down, a=a)
        dw_down = _matmul_tn(f"{tag}_dwdown", a, dy)
        dw_up = _matmul_tn(f"{tag}_dwup", hn, du, col_blocks=w_up.shape[0])
        sent = comm.send_grads(tag, {f"w_down{layer}": dw_down, f"w_up{layer}": dw_up})
        return _matmul_nt_rmsbwd(f"{tag}_dh", du, w_up, h, g, dy, after=sent)

    comm.prefetch(["w_in_a"], None)
    w_in, = comm.weights(["w_in_a"], None)
    proj, xn1 = _norm_matmul("a_inproj", x, p["g_attn"][0], w_in, tn=640, w_rows=True)
    ahead = comm.prefetch(["w_out_a"], proj)
    b_pad = jnp.pad(p["b_f"], ((0, 0), (0, LANES - n_heads)))
    c = _gate_fwd("a_gate", proj, b_pad, n_heads, gate_col, after=ahead)
    crow = c[:, :n_heads].T.reshape(n_heads // 2, 2, t)
    o_a, lse_a = _fox_fwd("a_attn", proj, c, crow, p["gq_a"], p["gk_a"], n_heads)
    ahead = comm.prefetch(["w_up0", "w_down0", "w_kv", "w_q_b", "w_out_b"], o_a)
    w_out_a, = comm.weights(["w_out_a"], o_a)
    h1 = _matmul_res("a_outproj", o_a, w_out_a, x, after=ahead)
    h2, mlp0 = mlp_fwd("mlp0", h1, p["g_mlp"][0], 0)

    ahead = comm.prefetch(["w_up1", "w_down1"], h2)
    w_kv, w_q_b = comm.weights(["w_kv", "w_q_b"], h2)
    kv, hn_kv = _norm_matmul("kv_proj", h2, p["g_kv"], w_kv, tn=2 * kvw, after=ahead)
    qb, hn_q = _norm_matmul("b_qproj", h2, p["g_attn"][1], w_q_b, tn=512)
    gqb, gkb = p["gq_b"], p["gk_b"].reshape(1, HEAD_DIM)
    bias = _bias_expand("b_bias", p["rel_bias"].T).reshape(kv_heads, group, 2 * WINDOW, WINDOW)
    bias = bias.transpose(0, 2, 1, 3).reshape(kv_heads, 2 * WINDOW, group * WINDOW)
    sink_rows = jnp.broadcast_to(p["sinks"].reshape(kv_heads, 1, group, 1), (kv_heads, 1, group, WINDOW)).reshape(kv_heads, 1, group * WINDOW)
    o_b, lse_b = _swa_fwd("b_attn", qb, kv, gqb, gkb, sink_rows, bias, group)
    w_out_b, = comm.weights(["w_out_b"], o_b)
    h3 = _matmul_res("b_outproj", o_b, w_out_b, h2)
    (dy, loss_tile), mlp1 = mlp_fwd("mlp1", h3, p["g_mlp"][1], 1, last=True)

    dh3, dg_mlp1 = mlp_bwd("mlp1", mlp1, 1, dy)
    do_b = _matmul_nt("b_do", dh3, w_out_b)
    dw_out_b = _matmul_tn("b_dwout", o_b, dh3)
    dqb, dkv, grads["gq_b"], dgk_b, dsink, dbias = _swa_bwd(
        "b_attn_bwd", qb, kv, gqb, gkb, sink_rows, bias, lse_b, do_b, group)
    grads["gk_b"] = dgk_b
    grads["sinks"] = dsink[:, :, 0].reshape(1, n_heads)
    dbias = dbias.reshape(kv_heads, 2 * WINDOW, group, WINDOW).transpose(0, 2, 1, 3)
    grads["rel_bias"] = _bias_reduce("b_dbias", dbias.reshape(n_heads, WINDOW * 2 * WINDOW)).T
    dw_q_b = _matmul_tn("b_dwq", hn_q, dqb)
    dh2, dg_attn1 = _matmul_nt_rmsbwd("b_dhq", dqb, w_q_b, h2, p["g_attn"][1], dh3)
    dw_kv = _matmul_tn("kv_dw", hn_kv, dkv)
    sent = comm.send_grads("attn_b", {"w_out_b": dw_out_b, "w_q_b": dw_q_b, "w_kv": dw_kv})
    dh2, dg_kv = _matmul_nt_rmsbwd("kv_dh", dkv, w_kv, h2, p["g_kv"], dh2, after=sent)
    grads["g_kv"] = dg_kv
    dh1, dg_mlp0 = mlp_bwd("mlp0", mlp0, 0, dh2)
    grads["g_mlp"] = (dg_mlp0, dg_mlp1)

    do_a = _matmul_nt("a_do", dh1, w_out_a)
    dw_out_a = _matmul_tn("a_dwout", o_a, dh1)
    sent = comm.send_grads("attn_a_out", {"w_out_a": dw_out_a})
    dq, dk, dv, dc, grads["gq_a"], grads["gk_a"] = _fox_bwd(
        "a_attn_bwd", proj, c, crow, p["gq_a"], p["gk_a"], lse_a, do_a, n_heads, after=sent)
    dfl, db_f = _gate_bwd("a_gate_bwd", proj, b_pad, dc, n_heads, gate_col)
    grads["b_f"] = db_f
    dproj = jnp.concatenate([dq, dk, dv, dfl], axis=1)
    dw_in = _matmul_tn("a_dwin", dproj, xn1, tk=640)
    sent = comm.send_grads("attn_a_in", {"w_in_a": dw_in})
    grad_x, dg_attn0 = _matmul_nt_rmsbwd("a_dx", dproj, w_in, x, p["g_attn"][0], dh1, w_rows=True, after=sent)
    grads["g_attn"] = (dg_attn0, dg_attn1)
    return loss_tile, grad_x, grads


EVERYONE = (1, 2, 3, 4, 5, 6, 7)
SAME_CORE = (1, 2, 4, 6)
OTHER_CHIPS = (2, 4, 6)


class _InFlight:
    def __init__(self, scatter, ks, send_sems, recv_sems, srcs, lands, token):
        self.scatter, self.ks, self.send_sems, self.recv_sems = scatter, ks, send_sems, recv_sems
        self.srcs, self.lands, self.token = list(srcs), list(lands), token


def _mesh_peers(ks=EVERYONE):
    x, y, c = lax.axis_index("x"), lax.axis_index("y"), lax.axis_index("c")
    peers = []
    for k in ks:
        px, py, pc = x ^ ((k >> 2) & 1), y ^ ((k >> 1) & 1), c ^ (k & 1)
        peers.append(((px, py, pc), 4 * px + 2 * py + pc))
    return 4 * x + 2 * y + c, peers


_HBM_SPEC = pl.BlockSpec(memory_space=pltpu.HBM)
_SEM_SPEC = pl.BlockSpec(memory_space=pltpu.SEMAPHORE)
_SIDE_EFFECT = pltpu.SideEffectType.DATAFLOW_SIDE_EFFECTING


def _exchange_start(name, arrays, scatter, ks=EVERYONE):
    n = len(arrays)
    me, _ = _mesh_peers()
    lands = []
    for a in arrays:
        own = lax.dynamic_index_in_dim(a, me, 0, keepdims=False) if scatter else a
        shape = a.shape if scatter else (N_DEV,) + a.shape
        lands.append(lax.dynamic_update_index_in_dim(lax.empty(shape, a.dtype), own, me, 0))

    def body(*refs):
        src, land = refs[:n], refs[n:2 * n]
        send_sems, recv_sems, token = refs[2 * n], refs[2 * n + 1], refs[-1]
        pos, peers = _mesh_peers(ks)
        for a in range(n):
            for k, (peer, peer_pos) in enumerate(peers):
                pltpu.make_async_remote_copy(
                    src_ref=src[a].at[peer_pos] if scatter else src[a], dst_ref=land[a].at[pos],
                    send_sem=send_sems.at[a * len(ks) + k], recv_sem=recv_sems.at[a * len(ks) + k],
                    device_id=peer, device_id_type=pl.DeviceIdType.MESH).start()
        token[...] = jnp.zeros_like(token)

    operands = [pltpu.with_memory_space_constraint(a, pltpu.HBM) for a in list(arrays) + lands]
    outs = pl.pallas_call(
        body, name=name,
        out_shape=(pltpu.SemaphoreType.DMA((n * len(ks),)), pltpu.SemaphoreType.DMA((n * len(ks),)),
                   *[pltpu.HBM(a.shape, a.dtype) for a in operands], _sds((8, LANES), F32)),
        in_specs=[_HBM_SPEC] * (2 * n),
        out_specs=(_SEM_SPEC, _SEM_SPEC, *[_HBM_SPEC] * (2 * n), pl.BlockSpec(memory_space=pltpu.VMEM)),
        input_output_aliases={i: 2 + i for i in range(2 * n)},
        compiler_params=pltpu.CompilerParams(has_side_effects=_SIDE_EFFECT),
    )(*operands)
    return _InFlight(scatter, ks, outs[0], outs[1], outs[2:2 + n], outs[2 + n:2 + 2 * n], outs[-1])


def _exchange_wait(name, flight, which, after):
    m = len(which)
    scatter, ks = flight.scatter, flight.ks

    def body(*refs):
        src, land = refs[:m], refs[m:2 * m]
        send_sems, recv_sems = refs[2 * m], refs[2 * m + 1]
        _, peers = _mesh_peers(ks)
        for i, a in enumerate(which):
            for k, (peer, peer_pos) in enumerate(peers):
                cp = pltpu.make_async_remote_copy(
                    src_ref=src[i].at[peer_pos] if scatter else src[i], dst_ref=land[i].at[peer_pos],
                    send_sem=send_sems.at[a * len(ks) + k], recv_sem=recv_sems.at[a * len(ks) + k],
                    device_id=peer, device_id_type=pl.DeviceIdType.MESH)
                cp.wait_send()
                cp.wait_recv()

    operands = [flight.srcs[a] for a in which] + [flight.lands[a] for a in which]
    outs = pl.pallas_call(
        body, name=name, out_shape=tuple(pltpu.HBM(a.shape, a.dtype) for a in operands),
        in_specs=[_HBM_SPEC] * (2 * m) + [_SEM_SPEC, _SEM_SPEC, pl.BlockSpec(memory_space=pl.ANY)],
        out_specs=tuple([_HBM_SPEC] * (2 * m)), input_output_aliases={i: i for i in range(2 * m)},
        compiler_params=pltpu.CompilerParams(has_side_effects=_SIDE_EFFECT),
    )(*operands, flight.send_sems, flight.recv_sems, after)
    return list(outs[m:])


def _relay_start(name, lands):
    n = len(lands)

    def body(*refs):
        land, send_sems, recv_sems, token = refs[:n], refs[n], refs[n + 1], refs[-1]
        _, peers = _mesh_peers(OTHER_CHIPS)
        sibling = (lax.axis_index("x"), lax.axis_index("y"), 1 - lax.axis_index("c"))
        for a in range(n):
            for k, (_, peer_pos) in enumerate(peers):
                pltpu.make_async_remote_copy(
                    src_ref=land[a].at[peer_pos], dst_ref=land[a].at[peer_pos],
                    send_sem=send_sems.at[a * len(peers) + k], recv_sem=recv_sems.at[a * len(peers) + k],
                    device_id=sibling, device_id_type=pl.DeviceIdType.MESH).start()
        token[...] = jnp.zeros_like(token)

    count = n * len(OTHER_CHIPS)
    outs = pl.pallas_call(
        body, name=name,
        out_shape=(pltpu.SemaphoreType.DMA((count,)), pltpu.SemaphoreType.DMA((count,)),
                   *[pltpu.HBM(a.shape, a.dtype) for a in lands], _sds((8, LANES), F32)),
        in_specs=[_HBM_SPEC] * n,
        out_specs=(_SEM_SPEC, _SEM_SPEC, *[_HBM_SPEC] * n, pl.BlockSpec(memory_space=pltpu.VMEM)),
        input_output_aliases={i: 2 + i for i in range(n)},
        compiler_params=pltpu.CompilerParams(has_side_effects=_SIDE_EFFECT),
    )(*[pltpu.with_memory_space_constraint(a, pltpu.HBM) for a in lands])
    return _InFlight(False, OTHER_CHIPS, outs[0], outs[1], [], outs[2:2 + n], outs[-1])


def _relay_wait(name, flight, which, after):
    m = len(which)

    def body(*refs):
        land, send_sems, recv_sems = refs[:m], refs[m], refs[m + 1]
        _, peers = _mesh_peers(OTHER_CHIPS)
        sibling = (lax.axis_index("x"), lax.axis_index("y"), 1 - lax.axis_index("c"))
        for i, a in enumerate(which):
            for k, (_, peer_pos) in enumerate(peers):
                cp = pltpu.make_async_remote_copy(
                    src_ref=land[i].at[peer_pos], dst_ref=land[i].at[peer_pos ^ 1],
                    send_sem=send_sems.at[a * len(peers) + k], recv_sem=recv_sems.at[a * len(peers) + k],
                    device_id=sibling, device_id_type=pl.DeviceIdType.MESH)
                cp.wait_send()
                cp.wait_recv()

    operands = [flight.lands[a] for a in which]
    outs = pl.pallas_call(
        body, name=name, out_shape=tuple(pltpu.HBM(a.shape, a.dtype) for a in operands),
        in_specs=[_HBM_SPEC] * m + [_SEM_SPEC, _SEM_SPEC, pl.BlockSpec(memory_space=pl.ANY)],
        out_specs=tuple([_HBM_SPEC] * m), input_output_aliases={i: i for i in range(m)},
        compiler_params=pltpu.CompilerParams(has_side_effects=_SIDE_EFFECT),
    )(*operands, flight.send_sems, flight.recv_sems, after)
    return list(outs)


def _sum_parts(p_ref):
    g = p_ref[0].astype(F32)
    for dev in range(1, N_DEV):
        g = g + p_ref[dev].astype(F32)
    return g


def _adam_update(g, w, m, v):
    m_new = ADAM_B1 * m + (1.0 - ADAM_B1) * g
    v_new = ADAM_B2 * v + (1.0 - ADAM_B2) * jnp.square(g)
    m_hat = m_new / (1.0 - ADAM_B1 ** ADAM_STEP)
    v_hat = v_new / (1.0 - ADAM_B2 ** ADAM_STEP)
    return -ADAM_LR * (m_hat / (jnp.sqrt(v_hat) + ADAM_EPS) + ADAM_WD * w), m_new, v_new


def _adamw(name, parts, w, m, v, layer=None, into=None):
    r, c = w.shape[-2:]
    tr = 256 if r % 256 == 0 else r
    n_into = 0 if into is None else len(into)

    def body(p_ref, w_ref, m_ref, v_ref, *refs):
        g_ref, d_ref, mo_ref, vo_ref = refs[n_into:]
        g = _sum_parts(p_ref)
        g_ref[...] = g
        d_ref[...], mo_ref[...], vo_ref[...] = _adam_update(g, w_ref[...], m_ref[...], v_ref[...])

    if layer is None:
        blk = pl.BlockSpec((tr, c), lambda i: (i, 0))
    else:
        blk = pl.BlockSpec((None, tr, c), lambda i: (layer, i, 0))
    return pl.pallas_call(
        body, name=name, grid=(r // tr,),
        in_specs=[pl.BlockSpec((N_DEV, tr, c), lambda i: (0, i, 0)), blk, blk, blk] + [pl.BlockSpec(memory_space=pl.ANY)] * n_into,
        out_specs=[blk] * 4, out_shape=[_sds(w.shape, F32)] * 4,
        input_output_aliases={4 + i: i for i in range(n_into)}, compiler_params=_params(1),
    )(parts, w, m, v, *(into or ()))


SMALL_PACK_ROWS = 16
LOSS_ROW = 11


def _small_rows(grads, loss_tile):
    return [(0, 1, grads["g_attn"][0]), (1, 1, grads["g_attn"][1]), (2, 1, grads["g_mlp"][0]), (3, 1, grads["g_mlp"][1]),
            (4, 1, grads["g_kv"]), (5, 1, grads["b_f"]), (6, 1, grads["gq_a"]), (7, 1, grads["gk_a"]), (8, 1, grads["gk_b"]),
            (9, 1, grads["gq_b"]), (10, 1, grads["sinks"]), (LOSS_ROW, 1, loss_tile)]


SMALL_ROWS = {"g_attn": (0, 2), "g_mlp": (2, 2), "g_kv": (4, 1), "b_f": (5, 1), "gq_a": (6, 1), "gk_a": (7, 1),
              "gk_b": (8, 1), "gq_b": (9, 1), "sinks": (10, 1)}


def _pack_small(name, pieces, d):
    def body(*refs):
        out = refs[-1]
        out[...] = jnp.zeros_like(out)
        for (row, rows, _), ref in zip(pieces, refs[:-1]):
            out[row:row + rows, 0:ref.shape[1]] = ref[0:rows, :]

    return pl.pallas_call(body, name=name, out_shape=_sds((SMALL_PACK_ROWS, d), F32), compiler_params=_params(0))(
        *[piece for _, _, piece in pieces])


def _adamw_small(name, parts, parts_rel_bias, w, m, v):
    def body(*refs):
        ins, outs = refs[2:2 + 3 * len(SMALL)], refs[2 + 3 * len(SMALL):]
        pack, rel = _sum_parts(refs[0]), _sum_parts(refs[1])
        for i, k in enumerate(SMALL):
            w_ref, m_ref, v_ref = ins[3 * i:3 * i + 3]
            if k == "rel_bias":
                g = rel
            else:
                row, rows = SMALL_ROWS[k]
                g = pack[row:row + rows, 0:w_ref.shape[1]]
            outs[4 * i][...] = g
            outs[4 * i + 1][...], outs[4 * i + 2][...], outs[4 * i + 3][...] = _adam_update(g, w_ref[...], m_ref[...], v_ref[...])
        outs[-1][...] = pack[LOSS_ROW:LOSS_ROW + 1, 0:LANES]

    operands = [parts, parts_rel_bias] + [t[k] for k in SMALL for t in (w, m, v)]
    out_shape = [_sds(w[k].shape, F32) for k in SMALL for _ in range(4)] + [_sds((1, LANES), F32)]
    outs = pl.pallas_call(body, name=name, out_shape=out_shape, compiler_params=_params(0))(*operands)
    return {k: outs[4 * i:4 * i + 4] for i, k in enumerate(SMALL)}, outs[-1]


class _Comm:
    ORDER = ("w_in_a", "w_out_a", "w_up0", "w_down0", "w_kv", "w_q_b", "w_out_b", "w_up1", "w_down1")

    def __init__(self, shards, d, n_in):
        self.d, self.n_in = d, n_in
        self.flight = _exchange_start("gather_start", [shards[n].astype(BF16) for n in self.ORDER], scatter=False, ks=SAME_CORE)
        self.relays, self.sent = {}, []

    def prefetch(self, names, after):
        which = [self.ORDER.index(n) for n in names]
        landed = _exchange_wait(f"gather_wait_{names[0]}", self.flight, which, self.flight.token if after is None else after)
        relay = _relay_start(f"gather_relay_{names[0]}", landed)
        for n in names:
            self.relays[n] = (relay, names)
        return relay.token

    def weights(self, names, after):
        relay, group = self.relays[names[0]]
        landed = _relay_wait(f"gather_relay_wait_{names[0]}", relay, [group.index(n) for n in names],
                             relay.token if after is None else after)
        return [self._whole(n, g) for n, g in zip(names, landed)]

    def _whole(self, name, g):
        if name == "w_in_a":
            return _join_row_blocks("w_in_join", g, -(-self.n_in // LANES) * LANES)
        if name.startswith("w_up"):
            return g
        return g.reshape(-1, g.shape[-1])

    def _chunks(self, name, g):
        if name == "w_in_a":
            return _split_row_blocks("dw_in_split", g, N_DEV, self.n_in // N_DEV)
        if name.startswith("w_up"):
            return g
        return g.reshape(N_DEV, g.shape[0] // N_DEV, g.shape[1])

    def send_grads(self, tag, partials):
        names = list(partials)
        flight = _exchange_start(f"scatter_start_{tag}", [self._chunks(n, partials[n]) for n in names], scatter=True)
        self.sent.append((tag, flight, names))
        return flight.token

    def received(self, index, after):
        tag, flight, names = self.sent[index]
        landed = _exchange_wait(f"scatter_wait_{tag}", flight, list(range(len(names))), after)
        return dict(zip(names, landed))


def kernel(x, g_attn, g_mlp, w_in_a, b_f, gq_a, gk_a, w_out_a, g_kv, w_kv, gk_b, w_q_b, gq_b, sinks, rel_bias, w_out_b, w_up, w_down, loss_target, m_g_attn, m_g_mlp, m_w_in_a, m_b_f, m_gq_a, m_gk_a, m_w_out_a, m_g_kv, m_w_kv, m_gk_b, m_w_q_b, m_gq_b, m_sinks, m_rel_bias, m_w_out_b, m_w_up, m_w_down, v_g_attn, v_g_mlp, v_w_in_a, v_b_f, v_gq_a, v_gk_a, v_w_out_a, v_g_kv, v_w_kv, v_gk_b, v_w_q_b, v_gq_b, v_sinks, v_rel_bias, v_w_out_b, v_w_up, v_w_down):
    w = dict(g_attn=g_attn, g_mlp=g_mlp, w_in_a=w_in_a, b_f=b_f, gq_a=gq_a, gk_a=gk_a, w_out_a=w_out_a, g_kv=g_kv,
             w_kv=w_kv, gk_b=gk_b, w_q_b=w_q_b, gq_b=gq_b, sinks=sinks, rel_bias=rel_bias, w_out_b=w_out_b,
             w_up=w_up, w_down=w_down)
    mom = dict(g_attn=m_g_attn, g_mlp=m_g_mlp, w_in_a=m_w_in_a, b_f=m_b_f, gq_a=m_gq_a, gk_a=m_gk_a, w_out_a=m_w_out_a,
               g_kv=m_g_kv, w_kv=m_w_kv, gk_b=m_gk_b, w_q_b=m_w_q_b, gq_b=m_gq_b, sinks=m_sinks, rel_bias=m_rel_bias,
               w_out_b=m_w_out_b, w_up=m_w_up, w_down=m_w_down)
    var = dict(g_attn=v_g_attn, g_mlp=v_g_mlp, w_in_a=v_w_in_a, b_f=v_b_f, gq_a=v_gq_a, gk_a=v_gk_a, w_out_a=v_w_out_a,
               g_kv=v_g_kv, w_kv=v_w_kv, gk_b=v_gk_b, w_q_b=v_w_q_b, gq_b=v_gq_b, sinks=v_sinks, rel_bias=v_rel_bias,
               w_out_b=v_w_out_b, w_up=v_w_up, w_down=v_w_down)
    d = x.shape[2]
    where = {"w_in_a": ("w_in_a", 0), "w_out_a": ("w_out_a", 0), "w_kv": ("w_kv", None), "w_q_b": ("w_q_b", 0),
             "w_out_b": ("w_out_b", 0), "w_up0": ("w_up", 0), "w_up1": ("w_up", 1), "w_down0": ("w_down", 0),
             "w_down1": ("w_down", 1)}
    flip = lambda tree: {**tree, "w_in_a": jnp.swapaxes(tree["w_in_a"], 1, 2)}
    w, mom, var = flip(w), flip(mom), flip(var)
    shards = {n: (w[k] if layer is None else w[k][layer]) for n, (k, layer) in where.items()}
    comm = _Comm(shards, d, w_in_a.shape[2] * N_DEV)
    loss_tile, grad_x, grads = _local_step(x[0], loss_target[0], {k: w[k] for k in SMALL}, comm)

    small_flight = _exchange_start(
        "gather_small_grads", [_pack_small("pack_small", _small_rows(grads, loss_tile), d), grads["rel_bias"]], scatter=False)
    res, after = {}, small_flight.token
    for index in range(len(comm.sent)):
        for n, parts in comm.received(index, after).items():
            k, layer = where[n]
            res[k] = _adamw(f"adam_{n}", parts, w[k], mom[k], var[k], layer, res.get(k))
            after = res[k][0]
    as_rows = lambda tree: {k: tree[k] if tree[k].ndim == 2 else tree[k].reshape(1, -1) for k in SMALL}
    small, loss_row = _adamw_small("adam_small", *_exchange_wait("gather_small_wait", small_flight, [0, 1], after),
                                   as_rows(w), as_rows(mom), as_rows(var))
    loss = loss_row[0, 0]
    for k in SMALL:
        res[k] = [a.reshape(w[k].shape) for a in small[k]]
    res["w_in_a"] = [jnp.swapaxes(a, 1, 2) for a in res["w_in_a"]]

    outs = [loss, grad_x[None]]
    for i in range(4):
        outs.extend(res[k][i] for k in WEIGHTS)
    return tuple(outs)
```

```python
import numpy as np
import jax
import jax.numpy as jnp
from jax import lax
from jax.experimental import pallas as pl
from jax.experimental.pallas import tpu as pltpu

F32 = jnp.float32
BF16 = jnp.bfloat16

N_DEV = 8
HEAD_DIM = 64
WINDOW = 128
N_BUCKETS = 32
REL_MAX_DIST = 128
NORM_EPS = 1e-6
NEG = -1e30
LANES = 128
VMEM_LIMIT = 56 * 1024 * 1024

ADAM_LR = 0.001
ADAM_B1 = 0.9
ADAM_B2 = 0.999
ADAM_EPS = 1e-08
ADAM_WD = 0.01
ADAM_STEP = 10

SMALL = ("g_attn", "g_mlp", "b_f", "gq_a", "gk_a", "g_kv", "gk_b", "gq_b", "sinks", "rel_bias")
WEIGHTS = ("g_attn", "g_mlp", "w_in_a", "b_f", "gq_a", "gk_a", "w_out_a", "g_kv", "w_kv", "gk_b",
           "w_q_b", "gq_b", "sinks", "rel_bias", "w_out_b", "w_up", "w_down")


def _params(n_grid):
    return pltpu.CompilerParams(dimension_semantics=("arbitrary",) * n_grid, vmem_limit_bytes=VMEM_LIMIT)


def _sds(shape, dtype):
    return jax.ShapeDtypeStruct(tuple(shape), dtype)


def _after_operand(after):
    if after is None:
        return [], []
    return [pl.BlockSpec((8, LANES), lambda *_: (0, 0))], [after]


def _rms(x, g):
    return (x * lax.rsqrt(jnp.mean(x * x, axis=-1, keepdims=True) + NORM_EPS)) * g


def _dot_nt(a, b):
    return lax.dot_general(a, b, (((1,), (1,)), ((), ())), preferred_element_type=F32)


def _dot_tn(a, b):
    return lax.dot_general(a, b, (((0,), (0,)), ((), ())), preferred_element_type=F32)


def _dot(a, b):
    return jnp.dot(a, b, preferred_element_type=F32)


def _dot_exact(a, b):
    return jnp.dot(a, b, preferred_element_type=F32, precision=lax.Precision.HIGHEST)


def _norm_matmul(name, x, g, w, *, tn=None, relu2=False, w_rows=False, after=None):
    t, d = x.shape
    blocked = w.ndim == 3
    if blocked:
        tn = w.shape[2]
        n = w.shape[0] * tn
        w_spec = pl.BlockSpec((None, d, tn), lambda i, j: (j, 0, 0))
    elif w_rows:
        n = w.shape[0]
        w_spec = pl.BlockSpec((tn, d), lambda i, j: (j, 0))
    else:
        n = w.shape[1]
        w_spec = pl.BlockSpec((d, tn), lambda i, j: (0, j))
    tm = min(1024, t)

    def body(x_ref, g_ref, w_ref, *rest):
        y_ref, xn_ref = rest[-2:]

        @pl.when(pl.program_id(1) == 0)
        def _():
            xn_ref[...] = _rms(x_ref[...], g_ref[...]).astype(BF16)

        y = _dot_nt(xn_ref[...], w_ref[...]) if w_rows else _dot(xn_ref[...], w_ref[...])
        y_ref[...] = jnp.square(jnp.maximum(y, 0.0)).astype(BF16) if relu2 else y

    extra_specs, extra = _after_operand(after)
    out_shape = [_sds((t, n), BF16 if relu2 else F32), _sds((t, d), BF16)]
    out_specs = [pl.BlockSpec((tm, tn), lambda i, j: (i, j)), pl.BlockSpec((tm, d), lambda i, j: (i, 0))]
    return pl.pallas_call(
        body, name=name, grid=(t // tm, n // tn),
        in_specs=[pl.BlockSpec((tm, d), lambda i, j: (i, 0)), pl.BlockSpec((1, d), lambda i, j: (0, 0)), w_spec] + extra_specs,
        out_specs=out_specs, out_shape=out_shape, compiler_params=_params(2),
    )(x, g.reshape(1, d), w, *extra)


def _matmul_res(name, a, w, res, *, tn=512, after=None):
    t, k = a.shape
    n = w.shape[1]
    tm = min(1024, t)

    def body(a_ref, w_ref, r_ref, *rest):
        rest[-1][...] = r_ref[...] + _dot(a_ref[...], w_ref[...])

    extra_specs, extra = _after_operand(after)
    return pl.pallas_call(
        body, name=name, grid=(t // tm, n // tn),
        in_specs=[pl.BlockSpec((tm, k), lambda i, j: (i, 0)), pl.BlockSpec((k, tn), lambda i, j: (0, j)),
                  pl.BlockSpec((tm, tn), lambda i, j: (i, j))] + extra_specs,
        out_specs=pl.BlockSpec((tm, tn), lambda i, j: (i, j)), out_shape=_sds((t, n), F32),
        compiler_params=_params(2),
    )(a, w, res, *extra)


def _matmul_nt(name, dy, w, *, a=None, tk=1024):
    t, n = dy.shape
    k = w.shape[0]
    tm = min(1024, t)

    def body(dy_ref, w_ref, *rest):
        o_ref = rest[-1]
        r = _dot_nt(dy_ref[...].astype(BF16), w_ref[...])
        if a is not None:
            r = r * (2.0 * jnp.sqrt(rest[0][...].astype(F32)))
        o_ref[...] = r.astype(BF16)

    in_specs = [pl.BlockSpec((tm, n), lambda i, j: (i, 0)), pl.BlockSpec((tk, n), lambda i, j: (j, 0))]
    args = [dy, w]
    if a is not None:
        in_specs.append(pl.BlockSpec((tm, tk), lambda i, j: (i, j)))
        args.append(a)
    return pl.pallas_call(
        body, name=name, grid=(t // tm, k // tk), in_specs=in_specs,
        out_specs=pl.BlockSpec((tm, tk), lambda i, j: (i, j)), out_shape=_sds((t, k), BF16),
        compiler_params=_params(2),
    )(*args)


def _matmul_nt_rmsbwd(name, dy, w, x, g, dres, *, w_rows=False, after=None):
    t, k = dy.shape
    blocked = w.ndim == 3
    d = w.shape[1] if blocked or w_rows else w.shape[0]
    tm = min(512, t)

    def body(dy_ref, w_ref, x_ref, g_ref, r_ref, *rest):
        dx_ref, dg_ref = rest[-2:]
        if blocked:
            kb = w.shape[2]
            dxn = _dot_nt(dy_ref[:, 0:kb].astype(BF16), w_ref[0])
            for j in range(1, w.shape[0]):
                dxn += _dot_nt(dy_ref[:, j * kb:(j + 1) * kb].astype(BF16), w_ref[j])
        elif w_rows:
            dxn = _dot(dy_ref[...].astype(BF16), w_ref[...])
        else:
            dxn = _dot_nt(dy_ref[...].astype(BF16), w_ref[...])
        _, vjp = jax.vjp(_rms, x_ref[...], g_ref[...])
        dx, dg = vjp(dxn)
        dx_ref[...] = r_ref[...] + dx

        @pl.when(pl.program_id(0) == 0)
        def _():
            dg_ref[...] = jnp.zeros_like(dg_ref)

        dg_ref[...] += dg

    row = lambda i: (i, 0)
    fixed = lambda i: (0, 0)
    extra_specs, extra = _after_operand(after)
    return pl.pallas_call(
        body, name=name, grid=(t // tm,),
        in_specs=[pl.BlockSpec((tm, k), row), pl.BlockSpec(w.shape, (lambda i: (0, 0, 0)) if blocked else fixed),
                  pl.BlockSpec((tm, d), row), pl.BlockSpec((1, d), fixed), pl.BlockSpec((tm, d), row)] + extra_specs,
        out_specs=[pl.BlockSpec((tm, d), row), pl.BlockSpec((1, d), fixed)],
        out_shape=[_sds((t, d), F32), _sds((1, d), F32)], compiler_params=_params(1),
    )(dy, w, x, g.reshape(1, d), dres, *extra)


def _matmul_tn(name, a, b, *, tk=1024, tn=1024, col_blocks=None):
    t, k = a.shape
    n = b.shape[1]
    tk = min(tk, k)
    if col_blocks:
        tn = n // col_blocks
        out_spec, out_shape = pl.BlockSpec((None, tk, tn), lambda i, j: (j, i, 0)), _sds((col_blocks, k, tn), BF16)
    else:
        tn = min(tn, n)
        out_spec, out_shape = pl.BlockSpec((tk, tn), lambda i, j: (i, j)), _sds((k, n), BF16)

    def body(a_ref, b_ref, o_ref):
        o_ref[...] = _dot_tn(a_ref[...].astype(BF16), b_ref[...].astype(BF16)).astype(BF16)

    return pl.pallas_call(
        body, name=name, grid=(k // tk, n // tn),
        in_specs=[pl.BlockSpec((t, tk), lambda i, j: (0, i)), pl.BlockSpec((t, tn), lambda i, j: (0, j))],
        out_specs=out_spec, out_shape=out_shape, compiler_params=_params(2),
    )(a, b)


def _join_row_blocks(name, blocks, rows):
    b, r, c = blocks.shape
    tc = min(256, c)

    def body(g_ref, o_ref):
        o_ref[...] = jnp.zeros_like(o_ref)
        for j in range(b):
            o_ref[r * j:r * (j + 1), :] = g_ref[j]

    return pl.pallas_call(
        body, name=name, grid=(c // tc,), in_specs=[pl.BlockSpec((b, r, tc), lambda i: (0, 0, i))],
        out_specs=pl.BlockSpec((rows, tc), lambda i: (0, i)), out_shape=_sds((rows, c), blocks.dtype),
        compiler_params=_params(1),
    )(blocks)


def _split_row_blocks(name, mat, b, r):
    rows, c = mat.shape
    tc = min(256, c)

    def body(w_ref, o_ref):
        for j in range(b):
            o_ref[j] = w_ref[r * j:r * (j + 1), :]

    return pl.pallas_call(
        body, name=name, grid=(c // tc,), in_specs=[pl.BlockSpec((rows, tc), lambda i: (0, i))],
        out_specs=pl.BlockSpec((b, r, tc), lambda i: (0, 0, i)), out_shape=_sds((b, r, c), mat.dtype),
        compiler_params=_params(1),
    )(mat)


def _matmul_res_loss(name, a, w, res, target, *, tn=512):
    t, k = a.shape
    n = w.shape[1]
    tm = min(1024, t)

    def body(a_ref, w_ref, r_ref, t_ref, dy_ref, l_ref):
        e = r_ref[...] + _dot(a_ref[...], w_ref[...]) - t_ref[...]
        dy_ref[...] = e * (1.0 / n)

        @pl.when((pl.program_id(0) == 0) & (pl.program_id(1) == 0))
        def _():
            l_ref[...] = jnp.zeros_like(l_ref)

        l_ref[...] += (0.5 / n) * jnp.sum(e * e)

    tile = pl.BlockSpec((tm, tn), lambda i, j: (i, j))
    return pl.pallas_call(
        body, name=name, grid=(t // tm, n // tn),
        in_specs=[pl.BlockSpec((tm, k), lambda i, j: (i, 0)), pl.BlockSpec((k, tn), lambda i, j: (0, j)), tile, tile],
        out_specs=[tile, pl.BlockSpec((8, LANES), lambda i, j: (0, 0))],
        out_shape=[_sds((t, n), F32), _sds((8, LANES), F32)], compiler_params=_params(2),
    )(a, w, res, target)


def _gate_fwd(name, proj, b_pad, n_heads, gate_col, after=None):
    t = proj.shape[0]
    tb = min(256, t)
    tri = jnp.asarray(np.tril(np.ones((tb, tb), np.float32)))
    extra_specs, extra = _after_operand(after)

    def body(p_ref, b_ref, tri_ref, *rest):
        c_ref, carry = rest[-2:]

        @pl.when(pl.program_id(0) == 0)
        def _():
            carry[...] = jnp.zeros_like(carry)

        lane = lax.broadcasted_iota(jnp.int32, (tb, LANES), 1)
        lf = jnp.where(lane < n_heads, jax.nn.log_sigmoid(p_ref[...] + b_ref[...]), 0.0)
        c = _dot_exact(tri_ref[...], lf) + carry[0:1, :]
        c_ref[...] = c
        carry[...] = jnp.broadcast_to(c[tb - 1:tb, :], carry.shape)

    return pl.pallas_call(
        body, name=name, grid=(t // tb,),
        in_specs=[pl.BlockSpec((tb, LANES), lambda i: (i, gate_col)), pl.BlockSpec((1, LANES), lambda i: (0, 0)),
                  pl.BlockSpec((tb, tb), lambda i: (0, 0))] + extra_specs,
        out_specs=pl.BlockSpec((tb, LANES), lambda i: (i, 0)), out_shape=_sds((t, LANES), F32),
        scratch_shapes=[pltpu.VMEM((8, LANES), F32)], compiler_params=_params(1),
    )(proj, b_pad, tri, *extra)


def _gate_bwd(name, proj, b_pad, dc, n_heads, gate_col):
    t = proj.shape[0]
    tb = min(256, t)
    nb = t // tb
    triu = jnp.asarray(np.triu(np.ones((tb, tb), np.float32)))

    def body(p_ref, b_ref, dc_ref, tri_ref, df_ref, db_ref, carry):
        @pl.when(pl.program_id(0) == 0)
        def _():
            carry[...] = jnp.zeros_like(carry)
            db_ref[...] = jnp.zeros_like(db_ref)

        dcv = dc_ref[...]
        dlf = _dot_exact(tri_ref[...], dcv) + carry[0:1, :]
        carry[...] = jnp.broadcast_to(dlf[0:1, :], carry.shape)
        lane = lax.broadcasted_iota(jnp.int32, (tb, LANES), 1)
        z = p_ref[...] + b_ref[...]
        df = jnp.where(lane < n_heads, dlf / (1.0 + jnp.exp(z)), 0.0)
        df_ref[...] = df.astype(BF16)
        db_ref[...] += jnp.sum(df, axis=0, keepdims=True)

    return pl.pallas_call(
        body, name=name, grid=(nb,),
        in_specs=[pl.BlockSpec((tb, LANES), lambda i: (nb - 1 - i, gate_col)), pl.BlockSpec((1, LANES), lambda i: (0, 0)),
                  pl.BlockSpec((tb, LANES), lambda i: (nb - 1 - i, 0)), pl.BlockSpec((tb, tb), lambda i: (0, 0))],
        out_specs=[pl.BlockSpec((tb, LANES), lambda i: (nb - 1 - i, 0)), pl.BlockSpec((1, LANES), lambda i: (0, 0))],
        out_shape=[_sds((t, LANES), BF16), _sds((1, LANES), F32)],
        scratch_shapes=[pltpu.VMEM((8, LANES), F32)], compiler_params=_params(1),
    )(proj, b_pad, dc, triu)


def _qhead(qp, g):
    return _rms(qp, g) * (HEAD_DIM ** -0.5)


def _column(mat, idx):
    lane = lax.broadcasted_iota(jnp.int32, mat.shape, 1)
    return jnp.sum(jnp.where(lane == idx, mat, 0.0), axis=1, keepdims=True)


def _fox_scores(kk, qi, ckey, cq_i, i, bq):
    length = kk.shape[0]
    s = _dot_nt(kk, qi) + cq_i - ckey[:length]
    key = lax.broadcasted_iota(jnp.int32, (length, bq), 0)
    qry = lax.broadcasted_iota(jnp.int32, (length, bq), 1) + i * bq
    return jnp.where(key <= qry, s, NEG)


def _fox_fwd(name, proj, c, crow, gq, gk, n_heads):
    t = proj.shape[0]
    hw = n_heads * HEAD_DIM
    npair = n_heads // 2
    bq = min(512, t)
    nq = t // bq

    def body(q_ref, k_ref, v_ref, c_ref, crow_ref, gq_ref, gk_ref, o_ref, lse_ref):
        hp = pl.program_id(0)
        lse_ref[...] = jnp.zeros_like(lse_ref)
        outs = []
        for hh in range(2):
            sl = slice(hh * HEAD_DIM, (hh + 1) * HEAD_DIM)
            qn = _qhead(q_ref[:, sl], gq_ref[...]).astype(BF16)
            kn = _rms(k_ref[:, sl], gk_ref[...]).astype(BF16)
            v_t = v_ref[:, sl].T.astype(BF16)
            ckey = _column(c_ref[...], 2 * hp + hh)
            cq = crow_ref[0, hh:hh + 1, :]
            o_blocks = []
            for i in range(nq):
                cols = slice(i * bq, (i + 1) * bq)
                length = (i + 1) * bq
                s = _fox_scores(kn[:length], qn[cols], ckey, cq[:, cols], i, bq)
                m = jnp.max(s, axis=0, keepdims=True)
                p = jnp.exp(s - m)
                l = jnp.sum(p, axis=0, keepdims=True)
                o_blocks.append((_dot(v_t[:, :length], p.astype(BF16)) / l).T)
                lse_ref[0, hh:hh + 1, cols] = m + jnp.log(l)
            outs.append(jnp.concatenate(o_blocks, axis=0))
        o_ref[...] = jnp.concatenate(outs, axis=1).astype(BF16)

    col = lambda off: (lambda h: (0, off + h))
    fixed = lambda h: (0, 0)
    return pl.pallas_call(
        body, name=name, grid=(npair,),
        in_specs=[pl.BlockSpec((t, LANES), col(0)), pl.BlockSpec((t, LANES), col(npair)), pl.BlockSpec((t, LANES), col(2 * npair)),
                  pl.BlockSpec((t, LANES), fixed), pl.BlockSpec((1, 2, t), lambda h: (h, 0, 0)),
                  pl.BlockSpec((1, HEAD_DIM), fixed), pl.BlockSpec((1, HEAD_DIM), fixed)],
        out_specs=[pl.BlockSpec((t, LANES), col(0)), pl.BlockSpec((1, 8, t), lambda h: (h, 0, 0))],
        out_shape=[_sds((t, hw), BF16), _sds((npair, 8, t), F32)], compiler_params=_params(1),
    )(proj, proj, proj, c, crow, gq, gk)


def _fox_bwd(name, proj, c, crow, gq, gk, lse, do, n_heads, after=None):
    t = proj.shape[0]
    hw = n_heads * HEAD_DIM
    npair = n_heads // 2
    bq = min(256, t)
    nq = t // bq

    def body(q_ref, k_ref, v_ref, c_ref, crow_ref, gq_ref, gk_ref, lse_ref, do_ref, *rest):
        dq_ref, dk_ref, dv_ref, dc_ref, dgq_ref, dgk_ref, dk_acc, dv_acc, dc_acc = rest[-9:]
        hp = pl.program_id(0)

        @pl.when(hp == 0)
        def _():
            dgq_ref[...] = jnp.zeros_like(dgq_ref)
            dgk_ref[...] = jnp.zeros_like(dgk_ref)
            dc_ref[...] = jnp.zeros_like(dc_ref)

        lane = lax.broadcasted_iota(jnp.int32, (t, LANES), 1)
        dqs, dks, dvs = [], [], []
        for hh in range(2):
            sl = slice(hh * HEAD_DIM, (hh + 1) * HEAD_DIM)
            qf, q_vjp = jax.vjp(_qhead, q_ref[:, sl], gq_ref[...])
            kf, k_vjp = jax.vjp(_rms, k_ref[:, sl], gk_ref[...])
            qn, kn, kn_t = qf.astype(BF16), kf.astype(BF16), kf.T.astype(BF16)
            vb = v_ref[:, sl].astype(BF16)
            dob = do_ref[:, sl]
            ckey = _column(c_ref[...], 2 * hp + hh)
            cq = crow_ref[0, hh:hh + 1, :]
            lse_h = lse_ref[0, hh:hh + 1, :]
            dk_acc[...] = jnp.zeros_like(dk_acc)
            dv_acc[...] = jnp.zeros_like(dv_acc)
            dc_acc[...] = jnp.zeros_like(dc_acc)
            dq_blocks = []
            for i in range(nq):
                cols = slice(i * bq, (i + 1) * bq)
                length = (i + 1) * bq
                qi, doi = qn[cols], dob[cols]
                s = _fox_scores(kn[:length], qi, ckey, cq[:, cols], i, bq)
                p = jnp.exp(s - lse_h[:, cols])
                dp = _dot_nt(vb[:length], doi)
                ds = p * (dp - jnp.sum(p * dp, axis=0, keepdims=True))
                dsb = ds.astype(BF16)
                dq_blocks.append(_dot(kn_t[:, :length], dsb).T)
                dk_acc[0:length, :] += _dot(dsb, qi)
                dv_acc[0:length, :] += _dot(p.astype(BF16), doi)
                part = ds[:, 0:LANES]
                for j in range(1, bq // LANES):
                    part = part + ds[:, j * LANES:(j + 1) * LANES]
                dc_acc[0:length, :] += part
            dqp, dgq = q_vjp(jnp.concatenate(dq_blocks, axis=0))
            dkp, dgk = k_vjp(dk_acc[...])
            dgq_ref[...] += dgq
            dgk_ref[...] += dgk
            dqs.append(dqp)
            dks.append(dkp)
            dvs.append(dv_acc[...])
            dc_ref[...] = jnp.where(lane == 2 * hp + hh, -jnp.sum(dc_acc[...], axis=1, keepdims=True), dc_ref[...])
        dq_ref[...] = jnp.concatenate(dqs, axis=1).astype(BF16)
        dk_ref[...] = jnp.concatenate(dks, axis=1).astype(BF16)
        dv_ref[...] = jnp.concatenate(dvs, axis=1).astype(BF16)

    col = lambda off: (lambda h: (0, off + h))
    fixed = lambda h: (0, 0)
    pair_blk = pl.BlockSpec((t, LANES), col(0))
    extra_specs, extra = _after_operand(after)
    return pl.pallas_call(
        body, name=name, grid=(npair,),
        in_specs=[pl.BlockSpec((t, LANES), col(0)), pl.BlockSpec((t, LANES), col(npair)), pl.BlockSpec((t, LANES), col(2 * npair)),
                  pl.BlockSpec((t, LANES), fixed), pl.BlockSpec((1, 2, t), lambda h: (h, 0, 0)),
                  pl.BlockSpec((1, HEAD_DIM), fixed), pl.BlockSpec((1, HEAD_DIM), fixed),
                  pl.BlockSpec((1, 8, t), lambda h: (h, 0, 0)), pair_blk] + extra_specs,
        out_specs=[pair_blk, pair_blk, pair_blk, pl.BlockSpec((t, LANES), fixed),
                   pl.BlockSpec((1, HEAD_DIM), fixed), pl.BlockSpec((1, HEAD_DIM), fixed)],
        out_shape=[_sds((t, hw), BF16), _sds((t, hw), BF16), _sds((t, hw), BF16), _sds((t, LANES), F32),
                   _sds((1, HEAD_DIM), F32), _sds((1, HEAD_DIM), F32)],
        scratch_shapes=[pltpu.VMEM((t, HEAD_DIM), F32), pltpu.VMEM((t, HEAD_DIM), F32), pltpu.VMEM((t, LANES), F32)],
        compiler_params=_params(1),
    )(proj, proj, proj, c, crow, gq, gk, lse, do, *extra)


def _t5_bucket_table():
    dist = np.arange(WINDOW)[None, :] + WINDOW - np.arange(2 * WINDOW)[:, None]
    n = np.maximum(dist, 0)
    max_exact = N_BUCKETS // 2
    large = max_exact + (np.log(np.maximum(n, 1) / max_exact) / np.log(REL_MAX_DIST / max_exact)
                         * (N_BUCKETS - max_exact)).astype(np.int32)
    large = np.minimum(large, N_BUCKETS - 1)
    return np.where(n < max_exact, n, large).astype(np.int32).reshape(1, -1)


def _bias_expand(name, rel_bias_t):
    n_heads = rel_bias_t.shape[0]
    tbl = jnp.asarray(_t5_bucket_table())
    width = tbl.shape[1]

    def body(rb_ref, tbl_ref, o_ref):
        onehot = (lax.broadcasted_iota(jnp.int32, (N_BUCKETS, width), 0) == tbl_ref[...]).astype(F32)
        o_ref[...] = _dot_exact(rb_ref[...], onehot)

    return pl.pallas_call(body, name=name, out_shape=_sds((n_heads, width), F32), compiler_params=_params(0))(rel_bias_t, tbl)


def _bias_reduce(name, dbias):
    n_heads, width = dbias.shape
    tbl = jnp.asarray(_t5_bucket_table())

    def body(db_ref, tbl_ref, o_ref):
        onehot = (lax.broadcasted_iota(jnp.int32, (N_BUCKETS, width), 0) == tbl_ref[...]).astype(F32)
        o_ref[...] = lax.dot_general(db_ref[...], onehot, (((1,), (1,)), ((), ())), preferred_element_type=F32,
                                     precision=lax.Precision.HIGHEST)

    return pl.pallas_call(body, name=name, out_shape=_sds((n_heads, N_BUCKETS), F32), compiler_params=_params(0))(dbias, tbl)


def _swa_mask(n, group):
    j = lax.broadcasted_iota(jnp.int32, (2 * WINDOW, group * WINDOW), 0)
    i = lax.broadcasted_iota(jnp.int32, (2 * WINDOW, group * WINDOW), 1) & (WINDOW - 1)
    ok = (j > i) & (j <= i + WINDOW) & ((n > 0) | (j >= WINDOW))
    return jnp.where(ok, 0.0, NEG)


def _swa_stack(ref, start, group):
    return jnp.concatenate([ref[pl.ds(start, WINDOW), g * HEAD_DIM:(g + 1) * HEAD_DIM] for g in range(group)], axis=0)


def _kv_head(ref, n_kv):
    out = ref[:, 0:HEAD_DIM]
    for h in range(1, n_kv):
        out = jnp.where(pl.program_id(0) == h, ref[:, h * HEAD_DIM:(h + 1) * HEAD_DIM], out)
    return out


def _swa_fwd(name, qb, kv, gq, gk, sinks, bias, group):
    t = qb.shape[0]
    kvh = kv.shape[1] // (2 * HEAD_DIM)
    nblk = t // WINDOW
    gw = group * HEAD_DIM
    band = 2 * WINDOW
    cols = group * WINDOW

    def body(q_ref, k_ref, v_ref, gq_ref, gk_ref, sink_ref, bias_ref, o_ref, lse_ref, qs, kpad, vpad):
        for g in range(group):
            qs[:, g * HEAD_DIM:(g + 1) * HEAD_DIM] = _qhead(q_ref[:, g * HEAD_DIM:(g + 1) * HEAD_DIM], gq_ref[...]).astype(BF16)
        kpad[0:WINDOW, :] = jnp.zeros((WINDOW, HEAD_DIM), BF16)
        vpad[0:WINDOW, :] = jnp.zeros((WINDOW, HEAD_DIM), BF16)
        kpad[WINDOW:, :] = _rms(_kv_head(k_ref, kvh), gk_ref[...]).astype(BF16)
        vpad[WINDOW:, :] = _kv_head(v_ref, kvh).astype(BF16)
        sink = sink_ref[0]

        def block(n, carry):
            start = pl.multiple_of(n * WINDOW, WINDOW)
            kb = kpad[pl.ds(start, band), :]
            vb = vpad[pl.ds(start, band), :]
            s = _dot_nt(kb, _swa_stack(qs, start, group)) + bias_ref[0] + _swa_mask(n, group)
            m = jnp.maximum(jnp.max(s, axis=0, keepdims=True), sink)
            e = jnp.exp(s - m)
            l = jnp.sum(e, axis=0, keepdims=True) + jnp.exp(sink - m)
            o_t = _dot_tn(vb, e.astype(BF16)) / l
            for g in range(group):
                o_ref[pl.ds(start, WINDOW), g * HEAD_DIM:(g + 1) * HEAD_DIM] = o_t[:, g * WINDOW:(g + 1) * WINDOW].T.astype(BF16)
            lse_ref[pl.ds(n, 1), :] = m + jnp.log(l)
            return carry

        lax.fori_loop(0, nblk, block, 0)

    fixed = lambda h: (0, 0)
    per = lambda h: (h, 0, 0)
    return pl.pallas_call(
        body, name=name, grid=(kvh,),
        in_specs=[pl.BlockSpec((t, gw), lambda h: (0, h)), pl.BlockSpec((t, kvh * HEAD_DIM), lambda h: (0, 0)),
                  pl.BlockSpec((t, kvh * HEAD_DIM), lambda h: (0, 1)),
                  pl.BlockSpec((1, HEAD_DIM), fixed), pl.BlockSpec((1, HEAD_DIM), fixed),
                  pl.BlockSpec((1, 1, cols), per), pl.BlockSpec((1, band, cols), per)],
        out_specs=[pl.BlockSpec((t, gw), lambda h: (0, h)), pl.BlockSpec((nblk, cols), lambda h: (h, 0))],
        out_shape=[_sds((t, kvh * gw), BF16), _sds((kvh * nblk, cols), F32)],
        scratch_shapes=[pltpu.VMEM((t, gw), BF16), pltpu.VMEM((t + WINDOW, HEAD_DIM), BF16),
                        pltpu.VMEM((t + WINDOW, HEAD_DIM), BF16)],
        compiler_params=_params(1),
    )(qb, kv, kv, gq, gk, sinks, bias)


def _swa_bwd(name, qb, kv, gq, gk, sinks, bias, lse, do, group):
    t = qb.shape[0]
    kvh = kv.shape[1] // (2 * HEAD_DIM)
    kvw = kvh * HEAD_DIM
    nblk = t // WINDOW
    gw = group * HEAD_DIM
    band = 2 * WINDOW
    cols = group * WINDOW

    def body(q_ref, k_ref, v_ref, gq_ref, gk_ref, sink_ref, bias_ref, lse_ref, do_ref,
             dq_ref, dkv_ref, dgq_ref, dgk_ref, dsink_ref, dbias_ref,
             qs, kpad, vpad, dqs, dk_acc, dv_acc, dsink_acc):
        @pl.when(pl.program_id(0) == 0)
        def _():
            dgq_ref[...] = jnp.zeros_like(dgq_ref)
            dgk_ref[...] = jnp.zeros_like(dgk_ref)
            dkv_ref[...] = jnp.zeros_like(dkv_ref)

        for g in range(group):
            qs[:, g * HEAD_DIM:(g + 1) * HEAD_DIM] = _qhead(q_ref[:, g * HEAD_DIM:(g + 1) * HEAD_DIM], gq_ref[...]).astype(BF16)
        kpad[0:WINDOW, :] = jnp.zeros((WINDOW, HEAD_DIM), BF16)
        vpad[0:WINDOW, :] = jnp.zeros((WINDOW, HEAD_DIM), BF16)
        kpad[WINDOW:, :] = _rms(_kv_head(k_ref, kvh), gk_ref[...]).astype(BF16)
        vpad[WINDOW:, :] = _kv_head(v_ref, kvh).astype(BF16)
        dk_acc[...] = jnp.zeros_like(dk_acc)
        dv_acc[...] = jnp.zeros_like(dv_acc)
        dsink_acc[...] = jnp.zeros_like(dsink_acc)
        dbias_ref[...] = jnp.zeros_like(dbias_ref)
        sink = sink_ref[0]

        def block(n, carry):
            start = pl.multiple_of(n * WINDOW, WINDOW)
            kb = kpad[pl.ds(start, band), :]
            vb = vpad[pl.ds(start, band), :]
            q = _swa_stack(qs, start, group)
            dob = _swa_stack(do_ref, start, group)
            lse_n = lse_ref[pl.ds(n, 1), :]
            s = _dot_nt(kb, q) + bias_ref[0] + _swa_mask(n, group)
            p = jnp.exp(s - lse_n)
            dp = _dot_nt(vb, dob)
            dsum = jnp.sum(p * dp, axis=0, keepdims=True)
            ds = p * (dp - dsum)
            dsb = ds.astype(BF16)
            dsink_acc[...] -= jnp.exp(sink - lse_n) * dsum
            dbias_ref[0] += ds
            dq = _dot_tn(dsb, kb)
            for g in range(group):
                dqs[pl.ds(start, WINDOW), g * HEAD_DIM:(g + 1) * HEAD_DIM] = dq[g * WINDOW:(g + 1) * WINDOW]
            dk_acc[pl.ds(start, band), :] += _dot(dsb, q)
            dv_acc[pl.ds(start, band), :] += _dot(p.astype(BF16), dob)
            return carry

        lax.fori_loop(0, nblk, block, 0)
        for g in range(group):
            _, q_vjp = jax.vjp(_qhead, q_ref[:, g * HEAD_DIM:(g + 1) * HEAD_DIM], gq_ref[...])
            dqp, dgq = q_vjp(dqs[:, g * HEAD_DIM:(g + 1) * HEAD_DIM])
            dq_ref[:, g * HEAD_DIM:(g + 1) * HEAD_DIM] = dqp.astype(BF16)
            dgq_ref[...] += dgq
            dsink_g = jnp.sum(dsink_acc[:, g * WINDOW:(g + 1) * WINDOW], axis=1, keepdims=True)
            dsink_ref[0, g:g + 1, :] = jnp.broadcast_to(dsink_g, (1, LANES))
        _, k_vjp = jax.vjp(_rms, _kv_head(k_ref, kvh), gk_ref[...])
        dkp, dgk = k_vjp(dk_acc[WINDOW:, :])
        dgk_ref[...] += dgk
        mine = lax.broadcasted_iota(jnp.int32, (t, kvw), 1) // HEAD_DIM == pl.program_id(0)
        dkv_ref[:, 0:kvw] = jnp.where(mine, jnp.concatenate([dkp] * kvh, axis=1), dkv_ref[:, 0:kvw])
        dkv_ref[:, kvw:] = jnp.where(mine, jnp.concatenate([dv_acc[WINDOW:, :]] * kvh, axis=1), dkv_ref[:, kvw:])

    fixed = lambda h: (0, 0)
    per = lambda h: (h, 0, 0)
    wide = pl.BlockSpec((t, gw), lambda h: (0, h))
    vec = pl.BlockSpec((1, HEAD_DIM), fixed)
    bias_spec = pl.BlockSpec((1, band, cols), per)
    return pl.pallas_call(
        body, name=name, grid=(kvh,),
        in_specs=[wide, pl.BlockSpec((t, kvw), lambda h: (0, 0)), pl.BlockSpec((t, kvw), lambda h: (0, 1)), vec, vec,
                  pl.BlockSpec((1, 1, cols), per), bias_spec, pl.BlockSpec((nblk, cols), lambda h: (h, 0)), wide],
        out_specs=[wide, pl.BlockSpec((t, 2 * kvw), fixed), vec, vec, pl.BlockSpec((1, group, LANES), per), bias_spec],
        out_shape=[_sds((t, kvh * gw), BF16), _sds((t, 2 * kvw), F32),
                   _sds((1, HEAD_DIM), F32), _sds((1, HEAD_DIM), F32),
                   _sds((kvh, group, LANES), F32), _sds((kvh, band, cols), F32)],
        scratch_shapes=[pltpu.VMEM((t, gw), BF16), pltpu.VMEM((t + WINDOW, HEAD_DIM), BF16),
                        pltpu.VMEM((t + WINDOW, HEAD_DIM), BF16), pltpu.VMEM((t, gw), F32),
                        pltpu.VMEM((t + WINDOW, HEAD_DIM), F32), pltpu.VMEM((t + WINDOW, HEAD_DIM), F32),
                        pltpu.VMEM((1, cols), F32)],
        compiler_params=_params(1),
    )(qb, kv, kv, gq, gk, sinks, bias, lse, do)


def _local_step(x, target, p, comm):
    t, d = x.shape
    n_heads = d // HEAD_DIM
    kv_heads = n_heads // 8
    group = n_heads // kv_heads
    hw = n_heads * HEAD_DIM
    gate_col = 3 * hw // LANES
    kvw = kv_heads * HEAD_DIM
    grads = {}

    def mlp_fwd(tag, h, g, layer, last=False):
        w_up, = comm.weights([f"w_up{layer}"], h)
        a, hn = _norm_matmul(f"{tag}_up", h, g, w_up, relu2=True)
        w_down, = comm.weights([f"w_down{layer}"], a)
        out = _matmul_res_loss(f"{tag}_down", a, w_down, h, target) if last else _matmul_res(f"{tag}_down", a, w_down, h)
        return out, (h, g, hn, a, w_up, w_down)

    def mlp_bwd(tag, saved, layer, dy):
        h, g, hn, a, w_up, w_down = saved
        du = _matmul_nt(f"{tag}_du", dy, w_down, a=a)
        dw_down = _matmul_tn(f"{tag}_dwdown", a, dy)
        dw_up = _matmul_tn(f"{tag}_dwup", hn, du, col_blocks=w_up.shape[0])
        sent = comm.send_grads(tag, {f"w_down{layer}": dw_down, f"w_up{layer}": dw_up})
        return _matmul_nt_rmsbwd(f"{tag}_dh", du, w_up, h, g, dy, after=sent)

    bias = _bias_expand("b_bias", p["rel_bias"].T).reshape(kv_heads, group, 2 * WINDOW, WINDOW)
    bias = bias.transpose(0, 2, 1, 3).reshape(kv_heads, 2 * WINDOW, group * WINDOW)
    comm.prefetch(["w_in_a"], bias)
    w_in, = comm.weights(["w_in_a"], None)
    proj, xn1 = _norm_matmul("a_inproj", x, p["g_attn"][0], w_in, tn=640, w_rows=True)
    ahead = comm.prefetch(["w_out_a"], proj)
    b_pad = jnp.pad(p["b_f"], ((0, 0), (0, LANES - n_heads)))
    c = _gate_fwd("a_gate", proj, b_pad, n_heads, gate_col, after=ahead)
    crow = c[:, :n_heads].T.reshape(n_heads // 2, 2, t)
    o_a, lse_a = _fox_fwd("a_attn", proj, c, crow, p["gq_a"], p["gk_a"], n_heads)
    ahead = comm.prefetch(["w_up0", "w_down0", "w_kv", "w_q_b", "w_out_b"], o_a)
    w_out_a, = comm.weights(["w_out_a"], o_a)
    h1 = _matmul_res("a_outproj", o_a, w_out_a, x, after=ahead)
    h2, mlp0 = mlp_fwd("mlp0", h1, p["g_mlp"][0], 0)

    ahead = comm.prefetch(["w_up1", "w_down1"], h2)
    w_kv, w_q_b = comm.weights(["w_kv", "w_q_b"], h2)
    kv, hn_kv = _norm_matmul("kv_proj", h2, p["g_kv"], w_kv, tn=2 * kvw, after=ahead)
    qb, hn_q = _norm_matmul("b_qproj", h2, p["g_attn"][1], w_q_b, tn=512)
    gqb, gkb = p["gq_b"], p["gk_b"].reshape(1, HEAD_DIM)
    sink_rows = jnp.broadcast_to(p["sinks"].reshape(kv_heads, 1, group, 1), (kv_heads, 1, group, WINDOW)).reshape(kv_heads, 1, group * WINDOW)
    o_b, lse_b = _swa_fwd("b_attn", qb, kv, gqb, gkb, sink_rows, bias, group)
    w_out_b, = comm.weights(["w_out_b"], o_b)
    h3 = _matmul_res("b_outproj", o_b, w_out_b, h2)
    (dy, loss_tile), mlp1 = mlp_fwd("mlp1", h3, p["g_mlp"][1], 1, last=True)

    dh3, dg_mlp1 = mlp_bwd("mlp1", mlp1, 1, dy)
    do_b = _matmul_nt("b_do", dh3, w_out_b)
    dw_out_b = _matmul_tn("b_dwout", o_b, dh3)
    dqb, dkv, grads["gq_b"], dgk_b, dsink, dbias = _swa_bwd(
        "b_attn_bwd", qb, kv, gqb, gkb, sink_rows, bias, lse_b, do_b, group)
    grads["gk_b"] = dgk_b
    grads["sinks"] = dsink[:, :, 0].reshape(1, n_heads)
    dbias = dbias.reshape(kv_heads, 2 * WINDOW, group, WINDOW).transpose(0, 2, 1, 3)
    grads["rel_bias"] = _bias_reduce("b_dbias", dbias.reshape(n_heads, WINDOW * 2 * WINDOW)).T
    dw_q_b = _matmul_tn("b_dwq", hn_q, dqb)
    dh2, dg_attn1 = _matmul_nt_rmsbwd("b_dhq", dqb, w_q_b, h2, p["g_attn"][1], dh3)
    dw_kv = _matmul_tn("kv_dw", hn_kv, dkv)
    sent = comm.send_grads("attn_b", {"w_out_b": dw_out_b, "w_q_b": dw_q_b, "w_kv": dw_kv})
    dh2, dg_kv = _matmul_nt_rmsbwd("kv_dh", dkv, w_kv, h2, p["g_kv"], dh2, after=sent)
    grads["g_kv"] = dg_kv
    dh1, dg_mlp0 = mlp_bwd("mlp0", mlp0, 0, dh2)
    grads["g_mlp"] = (dg_mlp0, dg_mlp1)

    do_a = _matmul_nt("a_do", dh1, w_out_a)
    dw_out_a = _matmul_tn("a_dwout", o_a, dh1)
    sent = comm.send_grads("attn_a_out", {"w_out_a": dw_out_a})
    dq, dk, dv, dc, grads["gq_a"], grads["gk_a"] = _fox_bwd(
        "a_attn_bwd", proj, c, crow, p["gq_a"], p["gk_a"], lse_a, do_a, n_heads, after=sent)
    dfl, db_f = _gate_bwd("a_gate_bwd", proj, b_pad, dc, n_heads, gate_col)
    grads["b_f"] = db_f
    dproj = jnp.concatenate([dq, dk, dv, dfl], axis=1)
    dw_in = _matmul_tn("a_dwin", dproj, xn1, tk=640)
    sent = comm.send_grads("attn_a_in", {"w_in_a": dw_in})
    grad_x, dg_attn0 = _matmul_nt_rmsbwd("a_dx", dproj, w_in, x, p["g_attn"][0], dh1, w_rows=True, after=sent)
    grads["g_attn"] = (dg_attn0, dg_attn1)
    return loss_tile, grad_x, grads


EVERYONE = (1, 2, 3, 4, 5, 6, 7)
SAME_CORE = (1, 2, 4, 6)
OTHER_CHIPS = (2, 4, 6)


class _InFlight:
    def __init__(self, scatter, ks, send_sems, recv_sems, srcs, lands, token):
        self.scatter, self.ks, self.send_sems, self.recv_sems = scatter, ks, send_sems, recv_sems
        self.srcs, self.lands, self.token = list(srcs), list(lands), token


def _mesh_peers(ks=EVERYONE):
    x, y, c = lax.axis_index("x"), lax.axis_index("y"), lax.axis_index("c")
    peers = []
    for k in ks:
        px, py, pc = x ^ ((k >> 2) & 1), y ^ ((k >> 1) & 1), c ^ (k & 1)
        peers.append(((px, py, pc), 4 * px + 2 * py + pc))
    return 4 * x + 2 * y + c, peers


_HBM_SPEC = pl.BlockSpec(memory_space=pltpu.HBM)
_SEM_SPEC = pl.BlockSpec(memory_space=pltpu.SEMAPHORE)
_SIDE_EFFECT = pltpu.SideEffectType.DATAFLOW_SIDE_EFFECTING


def _exchange_start(name, arrays, scatter, ks=EVERYONE):
    n = len(arrays)
    me, _ = _mesh_peers()
    lands = []
    for a in arrays:
        own = lax.dynamic_index_in_dim(a, me, 0, keepdims=False) if scatter else a
        shape = a.shape if scatter else (N_DEV,) + a.shape
        lands.append(lax.dynamic_update_index_in_dim(lax.empty(shape, a.dtype), own, me, 0))

    def body(*refs):
        src, land = refs[:n], refs[n:2 * n]
        send_sems, recv_sems, token = refs[2 * n], refs[2 * n + 1], refs[-1]
        pos, peers = _mesh_peers(ks)
        for a in range(n):
            for k, (peer, peer_pos) in enumerate(peers):
                pltpu.make_async_remote_copy(
                    src_ref=src[a].at[peer_pos] if scatter else src[a], dst_ref=land[a].at[pos],
                    send_sem=send_sems.at[a * len(ks) + k], recv_sem=recv_sems.at[a * len(ks) + k],
                    device_id=peer, device_id_type=pl.DeviceIdType.MESH).start()
        token[...] = jnp.zeros_like(token)

    operands = [pltpu.with_memory_space_constraint(a, pltpu.HBM) for a in list(arrays) + lands]
    outs = pl.pallas_call(
        body, name=name,
        out_shape=(pltpu.SemaphoreType.DMA((n * len(ks),)), pltpu.SemaphoreType.DMA((n * len(ks),)),
                   *[pltpu.HBM(a.shape, a.dtype) for a in operands], _sds((8, LANES), F32)),
        in_specs=[_HBM_SPEC] * (2 * n),
        out_specs=(_SEM_SPEC, _SEM_SPEC, *[_HBM_SPEC] * (2 * n), pl.BlockSpec(memory_space=pltpu.VMEM)),
        input_output_aliases={i: 2 + i for i in range(2 * n)},
        compiler_params=pltpu.CompilerParams(has_side_effects=_SIDE_EFFECT),
    )(*operands)
    return _InFlight(scatter, ks, outs[0], outs[1], outs[2:2 + n], outs[2 + n:2 + 2 * n], outs[-1])


def _exchange_wait(name, flight, which, after):
    m = len(which)
    scatter, ks = flight.scatter, flight.ks

    def body(*refs):
        src, land = refs[:m], refs[m:2 * m]
        send_sems, recv_sems = refs[2 * m], refs[2 * m + 1]
        _, peers = _mesh_peers(ks)
        for i, a in enumerate(which):
            for k, (peer, peer_pos) in enumerate(peers):
                cp = pltpu.make_async_remote_copy(
                    src_ref=src[i].at[peer_pos] if scatter else src[i], dst_ref=land[i].at[peer_pos],
                    send_sem=send_sems.at[a * len(ks) + k], recv_sem=recv_sems.at[a * len(ks) + k],
                    device_id=peer, device_id_type=pl.DeviceIdType.MESH)
                cp.wait_send()
                cp.wait_recv()

    operands = [flight.srcs[a] for a in which] + [flight.lands[a] for a in which]
    outs = pl.pallas_call(
        body, name=name, out_shape=tuple(pltpu.HBM(a.shape, a.dtype) for a in operands),
        in_specs=[_HBM_SPEC] * (2 * m) + [_SEM_SPEC, _SEM_SPEC, pl.BlockSpec(memory_space=pl.ANY)],
        out_specs=tuple([_HBM_SPEC] * (2 * m)), input_output_aliases={i: i for i in range(2 * m)},
        compiler_params=pltpu.CompilerParams(has_side_effects=_SIDE_EFFECT),
    )(*operands, flight.send_sems, flight.recv_sems, after)
    return list(outs[m:])


def _relay_start(name, lands):
    n = len(lands)

    def body(*refs):
        land, send_sems, recv_sems, token = refs[:n], refs[n], refs[n + 1], refs[-1]
        _, peers = _mesh_peers(OTHER_CHIPS)
        sibling = (lax.axis_index("x"), lax.axis_index("y"), 1 - lax.axis_index("c"))
        for a in range(n):
            for k, (_, peer_pos) in enumerate(peers):
                pltpu.make_async_remote_copy(
                    src_ref=land[a].at[peer_pos], dst_ref=land[a].at[peer_pos],
                    send_sem=send_sems.at[a * len(peers) + k], recv_sem=recv_sems.at[a * len(peers) + k],
                    device_id=sibling, device_id_type=pl.DeviceIdType.MESH).start()
        token[...] = jnp.zeros_like(token)

    count = n * len(OTHER_CHIPS)
    outs = pl.pallas_call(
        body, name=name,
        out_shape=(pltpu.SemaphoreType.DMA((count,)), pltpu.SemaphoreType.DMA((count,)),
                   *[pltpu.HBM(a.shape, a.dtype) for a in lands], _sds((8, LANES), F32)),
        in_specs=[_HBM_SPEC] * n,
        out_specs=(_SEM_SPEC, _SEM_SPEC, *[_HBM_SPEC] * n, pl.BlockSpec(memory_space=pltpu.VMEM)),
        input_output_aliases={i: 2 + i for i in range(n)},
        compiler_params=pltpu.CompilerParams(has_side_effects=_SIDE_EFFECT),
    )(*[pltpu.with_memory_space_constraint(a, pltpu.HBM) for a in lands])
    return _InFlight(False, OTHER_CHIPS, outs[0], outs[1], [], outs[2:2 + n], outs[-1])


def _relay_wait(name, flight, which, after):
    m = len(which)

    def body(*refs):
        land, send_sems, recv_sems = refs[:m], refs[m], refs[m + 1]
        _, peers = _mesh_peers(OTHER_CHIPS)
        sibling = (lax.axis_index("x"), lax.axis_index("y"), 1 - lax.axis_index("c"))
        for i, a in enumerate(which):
            for k, (_, peer_pos) in enumerate(peers):
                cp = pltpu.make_async_remote_copy(
                    src_ref=land[i].at[peer_pos], dst_ref=land[i].at[peer_pos ^ 1],
                    send_sem=send_sems.at[a * len(peers) + k], recv_sem=recv_sems.at[a * len(peers) + k],
                    device_id=sibling, device_id_type=pl.DeviceIdType.MESH)
                cp.wait_send()
                cp.wait_recv()

    operands = [flight.lands[a] for a in which]
    outs = pl.pallas_call(
        body, name=name, out_shape=tuple(pltpu.HBM(a.shape, a.dtype) for a in operands),
        in_specs=[_HBM_SPEC] * m + [_SEM_SPEC, _SEM_SPEC, pl.BlockSpec(memory_space=pl.ANY)],
        out_specs=tuple([_HBM_SPEC] * m), input_output_aliases={i: i for i in range(m)},
        compiler_params=pltpu.CompilerParams(has_side_effects=_SIDE_EFFECT),
    )(*operands, flight.send_sems, flight.recv_sems, after)
    return list(outs)


def _sum_parts(p_ref):
    g = p_ref[0].astype(F32)
    for dev in range(1, N_DEV):
        g = g + p_ref[dev].astype(F32)
    return g


def _adam_update(g, w, m, v):
    m_new = ADAM_B1 * m + (1.0 - ADAM_B1) * g
    v_new = ADAM_B2 * v + (1.0 - ADAM_B2) * jnp.square(g)
    m_hat = m_new / (1.0 - ADAM_B1 ** ADAM_STEP)
    v_hat = v_new / (1.0 - ADAM_B2 ** ADAM_STEP)
    return -ADAM_LR * (m_hat / (jnp.sqrt(v_hat) + ADAM_EPS) + ADAM_WD * w), m_new, v_new


def _adamw(name, parts, w, m, v, layer=None, into=None):
    r, c = w.shape[-2:]
    tr = 256 if r % 256 == 0 else r
    n_into = 0 if into is None else len(into)

    def body(p_ref, w_ref, m_ref, v_ref, *refs):
        g_ref, d_ref, mo_ref, vo_ref = refs[n_into:]
        g = _sum_parts(p_ref)
        g_ref[...] = g
        d_ref[...], mo_ref[...], vo_ref[...] = _adam_update(g, w_ref[...], m_ref[...], v_ref[...])

    if layer is None:
        blk = pl.BlockSpec((tr, c), lambda i: (i, 0))
    else:
        blk = pl.BlockSpec((None, tr, c), lambda i: (layer, i, 0))
    return pl.pallas_call(
        body, name=name, grid=(r // tr,),
        in_specs=[pl.BlockSpec((N_DEV, tr, c), lambda i: (0, i, 0)), blk, blk, blk] + [pl.BlockSpec(memory_space=pl.ANY)] * n_into,
        out_specs=[blk] * 4, out_shape=[_sds(w.shape, F32)] * 4,
        input_output_aliases={4 + i: i for i in range(n_into)}, compiler_params=_params(1),
    )(parts, w, m, v, *(into or ()))


SMALL_PACK_ROWS = 16
LOSS_ROW = 11


def _small_rows(grads, loss_tile):
    return [(0, 1, grads["g_attn"][0]), (1, 1, grads["g_attn"][1]), (2, 1, grads["g_mlp"][0]), (3, 1, grads["g_mlp"][1]),
            (4, 1, grads["g_kv"]), (5, 1, grads["b_f"]), (6, 1, grads["gq_a"]), (7, 1, grads["gk_a"]), (8, 1, grads["gk_b"]),
            (9, 1, grads["gq_b"]), (10, 1, grads["sinks"]), (LOSS_ROW, 1, loss_tile)]


SMALL_ROWS = {"g_attn": (0, 2), "g_mlp": (2, 2), "g_kv": (4, 1), "b_f": (5, 1), "gq_a": (6, 1), "gk_a": (7, 1),
              "gk_b": (8, 1), "gq_b": (9, 1), "sinks": (10, 1)}


def _pack_small(name, pieces, d):
    def body(*refs):
        out = refs[-1]
        out[...] = jnp.zeros_like(out)
        for (row, rows, _), ref in zip(pieces, refs[:-1]):
            out[row:row + rows, 0:ref.shape[1]] = ref[0:rows, :]

    return pl.pallas_call(body, name=name, out_shape=_sds((SMALL_PACK_ROWS, d), F32), compiler_params=_params(0))(
        *[piece for _, _, piece in pieces])


def _adamw_small(name, parts, parts_rel_bias, w, m, v):
    def body(*refs):
        ins, outs = refs[2:2 + 3 * len(SMALL)], refs[2 + 3 * len(SMALL):]
        pack, rel = _sum_parts(refs[0]), _sum_parts(refs[1])
        for i, k in enumerate(SMALL):
            w_ref, m_ref, v_ref = ins[3 * i:3 * i + 3]
            if k == "rel_bias":
                g = rel
            else:
                row, rows = SMALL_ROWS[k]
                g = pack[row:row + rows, 0:w_ref.shape[1]]
            outs[4 * i][...] = g
            outs[4 * i + 1][...], outs[4 * i + 2][...], outs[4 * i + 3][...] = _adam_update(g, w_ref[...], m_ref[...], v_ref[...])
        outs[-1][...] = pack[LOSS_ROW:LOSS_ROW + 1, 0:LANES]

    operands = [parts, parts_rel_bias] + [t[k] for k in SMALL for t in (w, m, v)]
    out_shape = [_sds(w[k].shape, F32) for k in SMALL for _ in range(4)] + [_sds((1, LANES), F32)]
    outs = pl.pallas_call(body, name=name, out_shape=out_shape, compiler_params=_params(0))(*operands)
    return {k: outs[4 * i:4 * i + 4] for i, k in enumerate(SMALL)}, outs[-1]


class _Comm:
    ORDER = ("w_in_a", "w_out_a", "w_up0", "w_down0", "w_kv", "w_q_b", "w_out_b", "w_up1", "w_down1")

    def __init__(self, shards, d, n_in):
        self.d, self.n_in = d, n_in
        self.flights = {}
        for tag, names in (("first", self.ORDER[:1]), ("rest", self.ORDER[1:])):
            flight = _exchange_start(f"gather_start_{tag}", [shards[n].astype(BF16) for n in names], scatter=False, ks=SAME_CORE)
            self.flights.update({n: (flight, i) for i, n in enumerate(names)})
        self.relays, self.sent = {}, []

    def prefetch(self, names, after):
        flight = self.flights[names[0]][0]
        which = [self.flights[n][1] for n in names]
        landed = _exchange_wait(f"gather_wait_{names[0]}", flight, which, flight.token if after is None else after)
        relay = _relay_start(f"gather_relay_{names[0]}", landed)
        for n in names:
            self.relays[n] = (relay, names)
        return relay.token

    def weights(self, names, after):
        relay, group = self.relays[names[0]]
        landed = _relay_wait(f"gather_relay_wait_{names[0]}", relay, [group.index(n) for n in names],
                             relay.token if after is None else after)
        return [self._whole(n, g) for n, g in zip(names, landed)]

    def _whole(self, name, g):
        if name == "w_in_a":
            return _join_row_blocks("w_in_join", g, -(-self.n_in // LANES) * LANES)
        if name.startswith("w_up"):
            return g
        return g.reshape(-1, g.shape[-1])

    def _chunks(self, name, g):
        if name == "w_in_a":
            return _split_row_blocks("dw_in_split", g, N_DEV, self.n_in // N_DEV)
        if name.startswith("w_up"):
            return g
        return g.reshape(N_DEV, g.shape[0] // N_DEV, g.shape[1])

    def send_grads(self, tag, partials):
        names = list(partials)
        flight = _exchange_start(f"scatter_start_{tag}", [self._chunks(n, partials[n]) for n in names], scatter=True)
        self.sent.append((tag, flight, names))
        return flight.token

    def received(self, index, after):
        tag, flight, names = self.sent[index]
        landed = _exchange_wait(f"scatter_wait_{tag}", flight, list(range(len(names))), after)
        return dict(zip(names, landed))


def kernel(x, g_attn, g_mlp, w_in_a, b_f, gq_a, gk_a, w_out_a, g_kv, w_kv, gk_b, w_q_b, gq_b, sinks, rel_bias, w_out_b, w_up, w_down, loss_target, m_g_attn, m_g_mlp, m_w_in_a, m_b_f, m_gq_a, m_gk_a, m_w_out_a, m_g_kv, m_w_kv, m_gk_b, m_w_q_b, m_gq_b, m_sinks, m_rel_bias, m_w_out_b, m_w_up, m_w_down, v_g_attn, v_g_mlp, v_w_in_a, v_b_f, v_gq_a, v_gk_a, v_w_out_a, v_g_kv, v_w_kv, v_gk_b, v_w_q_b, v_gq_b, v_sinks, v_rel_bias, v_w_out_b, v_w_up, v_w_down):
    w = dict(g_attn=g_attn, g_mlp=g_mlp, w_in_a=w_in_a, b_f=b_f, gq_a=gq_a, gk_a=gk_a, w_out_a=w_out_a, g_kv=g_kv,
             w_kv=w_kv, gk_b=gk_b, w_q_b=w_q_b, gq_b=gq_b, sinks=sinks, rel_bias=rel_bias, w_out_b=w_out_b,
             w_up=w_up, w_down=w_down)
    mom = dict(g_attn=m_g_attn, g_mlp=m_g_mlp, w_in_a=m_w_in_a, b_f=m_b_f, gq_a=m_gq_a, gk_a=m_gk_a, w_out_a=m_w_out_a,
               g_kv=m_g_kv, w_kv=m_w_kv, gk_b=m_gk_b, w_q_b=m_w_q_b, gq_b=m_gq_b, sinks=m_sinks, rel_bias=m_rel_bias,
               w_out_b=m_w_out_b, w_up=m_w_up, w_down=m_w_down)
    var = dict(g_attn=v_g_attn, g_mlp=v_g_mlp, w_in_a=v_w_in_a, b_f=v_b_f, gq_a=v_gq_a, gk_a=v_gk_a, w_out_a=v_w_out_a,
               g_kv=v_g_kv, w_kv=v_w_kv, gk_b=v_gk_b, w_q_b=v_w_q_b, gq_b=v_gq_b, sinks=v_sinks, rel_bias=v_rel_bias,
               w_out_b=v_w_out_b, w_up=v_w_up, w_down=v_w_down)
    d = x.shape[2]
    where = {"w_in_a": ("w_in_a", 0), "w_out_a": ("w_out_a", 0), "w_kv": ("w_kv", None), "w_q_b": ("w_q_b", 0),
             "w_out_b": ("w_out_b", 0), "w_up0": ("w_up", 0), "w_up1": ("w_up", 1), "w_down0": ("w_down", 0),
             "w_down1": ("w_down", 1)}
    flip = lambda tree: {**tree, "w_in_a": jnp.swapaxes(tree["w_in_a"], 1, 2)}
    w, mom, var = flip(w), flip(mom), flip(var)
    shards = {n: (w[k] if layer is None else w[k][layer]) for n, (k, layer) in where.items()}
    comm = _Comm(shards, d, w_in_a.shape[2] * N_DEV)
    loss_tile, grad_x, grads = _local_step(x[0], loss_target[0], {k: w[k] for k in SMALL}, comm)

    small_flight = _exchange_start(
        "gather_small_grads", [_pack_small("pack_small", _small_rows(grads, loss_tile), d), grads["rel_bias"]], scatter=False)
    res, after = {}, small_flight.token
    for index in range(len(comm.sent)):
        for n, parts in comm.received(index, after).items():
            k, layer = where[n]
            res[k] = _adamw(f"adam_{n}", parts, w[k], mom[k], var[k], layer, res.get(k))
            after = res[k][0]
    as_rows = lambda tree: {k: tree[k] if tree[k].ndim == 2 else tree[k].reshape(1, -1) for k in SMALL}
    small, loss_row = _adamw_small("adam_small", *_exchange_wait("gather_small_wait", small_flight, [0, 1], after),
                                   as_rows(w), as_rows(mom), as_rows(var))
    loss = loss_row[0, 0]
    for k in SMALL:
        res[k] = [a.reshape(w[k].shape) for a in small[k]]
    res["w_in_a"] = [jnp.swapaxes(a, 1, 2) for a in res["w_in_a"]]

    outs = [loss, grad_x[None]]
    for i in range(4):
        outs.extend(res[k][i] for k in WEIGHTS)
    return tuple(outs)
```

```python
import numpy as np
import jax
import jax.numpy as jnp
from jax import lax
from jax.experimental import pallas as pl
from jax.experimental.pallas import tpu as pltpu

F32 = jnp.float32
BF16 = jnp.bfloat16

N_DEV = 8
HEAD_DIM = 64
WINDOW = 128
N_BUCKETS = 32
REL_MAX_DIST = 128
NORM_EPS = 1e-6
NEG = -1e30
LANES = 128
VMEM_LIMIT = 56 * 1024 * 1024

ADAM_LR = 0.001
ADAM_B1 = 0.9
ADAM_B2 = 0.999
ADAM_EPS = 1e-08
ADAM_WD = 0.01
ADAM_STEP = 10

SMALL = ("g_attn", "g_mlp", "b_f", "gq_a", "gk_a", "g_kv", "gk_b", "gq_b", "sinks", "rel_bias")
WEIGHTS = ("g_attn", "g_mlp", "w_in_a", "b_f", "gq_a", "gk_a", "w_out_a", "g_kv", "w_kv", "gk_b",
           "w_q_b", "gq_b", "sinks", "rel_bias", "w_out_b", "w_up", "w_down")


def _params(n_grid):
    return pltpu.CompilerParams(dimension_semantics=("arbitrary",) * n_grid, vmem_limit_bytes=VMEM_LIMIT)


def _sds(shape, dtype):
    return jax.ShapeDtypeStruct(tuple(shape), dtype)


def _after_operand(after):
    if after is None:
        return [], []
    return [pl.BlockSpec((8, LANES), lambda *_: (0, 0))], [after]


def _rms(x, g):
    return (x * lax.rsqrt(jnp.mean(x * x, axis=-1, keepdims=True) + NORM_EPS)) * g


def _dot_nt(a, b):
    return lax.dot_general(a, b, (((1,), (1,)), ((), ())), preferred_element_type=F32)


def _dot_tn(a, b):
    return lax.dot_general(a, b, (((0,), (0,)), ((), ())), preferred_element_type=F32)


def _dot(a, b):
    return jnp.dot(a, b, preferred_element_type=F32)


def _dot_exact(a, b):
    return jnp.dot(a, b, preferred_element_type=F32, precision=lax.Precision.HIGHEST)


def _norm_matmul(name, x, g, w, *, tn=None, relu2=False, w_rows=False, after=None):
    t, d = x.shape
    blocked = w.ndim == 3
    per_step = 2 if blocked else 1
    if blocked:
        tn = per_step * w.shape[2]
        n = w.shape[0] * w.shape[2]
        w_spec = pl.BlockSpec((per_step, d, w.shape[2]), lambda i, j: (j, 0, 0))
    elif w_rows:
        n = w.shape[0]
        w_spec = pl.BlockSpec((tn, d), lambda i, j: (j, 0))
    else:
        n = w.shape[1]
        w_spec = pl.BlockSpec((d, tn), lambda i, j: (0, j))
    tm = min(1024, t)

    def body(x_ref, g_ref, w_ref, *rest):
        y_ref, xn_ref = rest[-2:]

        @pl.when(pl.program_id(1) == 0)
        def _():
            xn_ref[...] = _rms(x_ref[...], g_ref[...]).astype(BF16)

        for b in range(per_step):
            cols = slice(b * (tn // per_step), (b + 1) * (tn // per_step)) if blocked else slice(None)
            wb = w_ref[b] if blocked else w_ref[...]
            y = _dot_nt(xn_ref[...], wb) if w_rows else _dot(xn_ref[...], wb)
            y_ref[:, cols] = jnp.square(jnp.maximum(y, 0.0)).astype(BF16) if relu2 else y

    extra_specs, extra = _after_operand(after)
    out_shape = [_sds((t, n), BF16 if relu2 else F32), _sds((t, d), BF16)]
    out_specs = [pl.BlockSpec((tm, tn), lambda i, j: (i, j)), pl.BlockSpec((tm, d), lambda i, j: (i, 0))]
    return pl.pallas_call(
        body, name=name, grid=(t // tm, n // tn),
        in_specs=[pl.BlockSpec((tm, d), lambda i, j: (i, 0)), pl.BlockSpec((1, d), lambda i, j: (0, 0)), w_spec] + extra_specs,
        out_specs=out_specs, out_shape=out_shape, compiler_params=_params(2),
    )(x, g.reshape(1, d), w, *extra)


def _matmul_res(name, a, w, res, *, tn=512, after=None):
    t, k = a.shape
    n = w.shape[1]
    tm = min(1024, t)

    def body(a_ref, w_ref, r_ref, *rest):
        rest[-1][...] = r_ref[...] + _dot(a_ref[...], w_ref[...])

    extra_specs, extra = _after_operand(after)
    return pl.pallas_call(
        body, name=name, grid=(t // tm, n // tn),
        in_specs=[pl.BlockSpec((tm, k), lambda i, j: (i, 0)), pl.BlockSpec((k, tn), lambda i, j: (0, j)),
                  pl.BlockSpec((tm, tn), lambda i, j: (i, j))] + extra_specs,
        out_specs=pl.BlockSpec((tm, tn), lambda i, j: (i, j)), out_shape=_sds((t, n), F32),
        compiler_params=_params(2),
    )(a, w, res, *extra)


def _matmul_nt(name, dy, w, *, a=None, tk=1024):
    t, n = dy.shape
    k = w.shape[0]
    tm = min(1024, t)

    def body(dy_ref, w_ref, *rest):
        o_ref = rest[-1]
        r = _dot_nt(dy_ref[...].astype(BF16), w_ref[...])
        if a is not None:
            r = r * (2.0 * jnp.sqrt(rest[0][...].astype(F32)))
        o_ref[...] = r.astype(BF16)

    in_specs = [pl.BlockSpec((tm, n), lambda i, j: (i, 0)), pl.BlockSpec((tk, n), lambda i, j: (j, 0))]
    args = [dy, w]
    if a is not None:
        in_specs.append(pl.BlockSpec((tm, tk), lambda i, j: (i, j)))
        args.append(a)
    return pl.pallas_call(
        body, name=name, grid=(t // tm, k // tk), in_specs=in_specs,
        out_specs=pl.BlockSpec((tm, tk), lambda i, j: (i, j)), out_shape=_sds((t, k), BF16),
        compiler_params=_params(2),
    )(*args)


def _matmul_nt_rmsbwd(name, dy, w, x, g, dres, *, w_rows=False, after=None):
    t, k = dy.shape
    blocked = w.ndim == 3
    d = w.shape[1] if blocked or w_rows else w.shape[0]
    tm = min(512, t)

    def body(dy_ref, w_ref, x_ref, g_ref, r_ref, *rest):
        dx_ref, dg_ref = rest[-2:]
        if blocked:
            kb = w.shape[2]
            dxn = _dot_nt(dy_ref[:, 0:kb].astype(BF16), w_ref[0])
            for j in range(1, w.shape[0]):
                dxn += _dot_nt(dy_ref[:, j * kb:(j + 1) * kb].astype(BF16), w_ref[j])
        elif w_rows:
            dxn = _dot(dy_ref[...].astype(BF16), w_ref[...])
        else:
            dxn = _dot_nt(dy_ref[...].astype(BF16), w_ref[...])
        _, vjp = jax.vjp(_rms, x_ref[...], g_ref[...])
        dx, dg = vjp(dxn)
        dx_ref[...] = r_ref[...] + dx

        @pl.when(pl.program_id(0) == 0)
        def _():
            dg_ref[...] = jnp.zeros_like(dg_ref)

        dg_ref[...] += dg

    row = lambda i: (i, 0)
    fixed = lambda i: (0, 0)
    extra_specs, extra = _after_operand(after)
    return pl.pallas_call(
        body, name=name, grid=(t // tm,),
        in_specs=[pl.BlockSpec((tm, k), row), pl.BlockSpec(w.shape, (lambda i: (0, 0, 0)) if blocked else fixed),
                  pl.BlockSpec((tm, d), row), pl.BlockSpec((1, d), fixed), pl.BlockSpec((tm, d), row)] + extra_specs,
        out_specs=[pl.BlockSpec((tm, d), row), pl.BlockSpec((1, d), fixed)],
        out_shape=[_sds((t, d), F32), _sds((1, d), F32)], compiler_params=_params(1),
    )(dy, w, x, g.reshape(1, d), dres, *extra)


def _matmul_tn(name, a, b, *, tk=1024, tn=1024, col_blocks=None):
    t, k = a.shape
    n = b.shape[1]
    tk = min(tk, k)
    if col_blocks:
        tn = n // col_blocks
        out_spec, out_shape = pl.BlockSpec((None, tk, tn), lambda i, j: (j, i, 0)), _sds((col_blocks, k, tn), BF16)
    else:
        tn = min(tn, n)
        out_spec, out_shape = pl.BlockSpec((tk, tn), lambda i, j: (i, j)), _sds((k, n), BF16)

    def body(a_ref, b_ref, o_ref):
        o_ref[...] = _dot_tn(a_ref[...].astype(BF16), b_ref[...].astype(BF16)).astype(BF16)

    return pl.pallas_call(
        body, name=name, grid=(k // tk, n // tn),
        in_specs=[pl.BlockSpec((t, tk), lambda i, j: (0, i)), pl.BlockSpec((t, tn), lambda i, j: (0, j))],
        out_specs=out_spec, out_shape=out_shape, compiler_params=_params(2),
    )(a, b)


def _join_row_blocks(name, blocks, rows):
    b, r, c = blocks.shape
    tc = min(256, c)

    def body(g_ref, o_ref):
        o_ref[...] = jnp.zeros_like(o_ref)
        for j in range(b):
            o_ref[r * j:r * (j + 1), :] = g_ref[j]

    return pl.pallas_call(
        body, name=name, grid=(c // tc,), in_specs=[pl.BlockSpec((b, r, tc), lambda i: (0, 0, i))],
        out_specs=pl.BlockSpec((rows, tc), lambda i: (0, i)), out_shape=_sds((rows, c), blocks.dtype),
        compiler_params=_params(1),
    )(blocks)


def _split_row_blocks(name, mat, b, r):
    rows, c = mat.shape
    tc = min(256, c)

    def body(w_ref, o_ref):
        for j in range(b):
            o_ref[j] = w_ref[r * j:r * (j + 1), :]

    return pl.pallas_call(
        body, name=name, grid=(c // tc,), in_specs=[pl.BlockSpec((rows, tc), lambda i: (0, i))],
        out_specs=pl.BlockSpec((b, r, tc), lambda i: (0, 0, i)), out_shape=_sds((b, r, c), mat.dtype),
        compiler_params=_params(1),
    )(mat)


def _matmul_res_loss(name, a, w, res, target, *, tn=512):
    t, k = a.shape
    n = w.shape[1]
    tm = min(1024, t)

    def body(a_ref, w_ref, r_ref, t_ref, dy_ref, l_ref):
        e = r_ref[...] + _dot(a_ref[...], w_ref[...]) - t_ref[...]
        dy_ref[...] = e * (1.0 / n)

        @pl.when((pl.program_id(0) == 0) & (pl.program_id(1) == 0))
        def _():
            l_ref[...] = jnp.zeros_like(l_ref)

        l_ref[...] += (0.5 / n) * jnp.sum(e * e)

    tile = pl.BlockSpec((tm, tn), lambda i, j: (i, j))
    return pl.pallas_call(
        body, name=name, grid=(t // tm, n // tn),
        in_specs=[pl.BlockSpec((tm, k), lambda i, j: (i, 0)), pl.BlockSpec((k, tn), lambda i, j: (0, j)), tile, tile],
        out_specs=[tile, pl.BlockSpec((8, LANES), lambda i, j: (0, 0))],
        out_shape=[_sds((t, n), F32), _sds((8, LANES), F32)], compiler_params=_params(2),
    )(a, w, res, target)


def _gate_fwd(name, proj, b_pad, n_heads, gate_col, after=None):
    t = proj.shape[0]
    tb = min(256, t)
    tri = jnp.asarray(np.tril(np.ones((tb, tb), np.float32)))
    extra_specs, extra = _after_operand(after)

    def body(p_ref, b_ref, tri_ref, *rest):
        c_ref, carry = rest[-2:]

        @pl.when(pl.program_id(0) == 0)
        def _():
            carry[...] = jnp.zeros_like(carry)

        lane = lax.broadcasted_iota(jnp.int32, (tb, LANES), 1)
        lf = jnp.where(lane < n_heads, jax.nn.log_sigmoid(p_ref[...] + b_ref[...]), 0.0)
        c = _dot_exact(tri_ref[...], lf) + carry[0:1, :]
        c_ref[...] = c
        carry[...] = jnp.broadcast_to(c[tb - 1:tb, :], carry.shape)

    return pl.pallas_call(
        body, name=name, grid=(t // tb,),
        in_specs=[pl.BlockSpec((tb, LANES), lambda i: (i, gate_col)), pl.BlockSpec((1, LANES), lambda i: (0, 0)),
                  pl.BlockSpec((tb, tb), lambda i: (0, 0))] + extra_specs,
        out_specs=pl.BlockSpec((tb, LANES), lambda i: (i, 0)), out_shape=_sds((t, LANES), F32),
        scratch_shapes=[pltpu.VMEM((8, LANES), F32)], compiler_params=_params(1),
    )(proj, b_pad, tri, *extra)


def _gate_bwd(name, proj, b_pad, dc, n_heads, gate_col):
    t = proj.shape[0]
    tb = min(256, t)
    nb = t // tb
    triu = jnp.asarray(np.triu(np.ones((tb, tb), np.float32)))

    def body(p_ref, b_ref, dc_ref, tri_ref, df_ref, db_ref, carry):
        @pl.when(pl.program_id(0) == 0)
        def _():
            carry[...] = jnp.zeros_like(carry)
            db_ref[...] = jnp.zeros_like(db_ref)

        dcv = dc_ref[...]
        dlf = _dot_exact(tri_ref[...], dcv) + carry[0:1, :]
        carry[...] = jnp.broadcast_to(dlf[0:1, :], carry.shape)
        lane = lax.broadcasted_iota(jnp.int32, (tb, LANES), 1)
        z = p_ref[...] + b_ref[...]
        df = jnp.where(lane < n_heads, dlf / (1.0 + jnp.exp(z)), 0.0)
        df_ref[...] = df.astype(BF16)
        db_ref[...] += jnp.sum(df, axis=0, keepdims=True)

    return pl.pallas_call(
        body, name=name, grid=(nb,),
        in_specs=[pl.BlockSpec((tb, LANES), lambda i: (nb - 1 - i, gate_col)), pl.BlockSpec((1, LANES), lambda i: (0, 0)),
                  pl.BlockSpec((tb, LANES), lambda i: (nb - 1 - i, 0)), pl.BlockSpec((tb, tb), lambda i: (0, 0))],
        out_specs=[pl.BlockSpec((tb, LANES), lambda i: (nb - 1 - i, 0)), pl.BlockSpec((1, LANES), lambda i: (0, 0))],
        out_shape=[_sds((t, LANES), BF16), _sds((1, LANES), F32)],
        scratch_shapes=[pltpu.VMEM((8, LANES), F32)], compiler_params=_params(1),
    )(proj, b_pad, dc, triu)


def _qhead(qp, g):
    return _rms(qp, g) * (HEAD_DIM ** -0.5)


def _column(mat, idx):
    lane = lax.broadcasted_iota(jnp.int32, mat.shape, 1)
    return jnp.sum(jnp.where(lane == idx, mat, 0.0), axis=1, keepdims=True)


def _fox_scores(kk, qi, ckey, cq_i, i, bq):
    length = kk.shape[0]
    s = _dot_nt(kk, qi) + cq_i - ckey[:length]
    key = lax.broadcasted_iota(jnp.int32, (length, bq), 0)
    qry = lax.broadcasted_iota(jnp.int32, (length, bq), 1) + i * bq
    return jnp.where(key <= qry, s, NEG)


def _fox_fwd(name, proj, c, crow, gq, gk, n_heads):
    t = proj.shape[0]
    hw = n_heads * HEAD_DIM
    npair = n_heads // 2
    bq = min(512, t)
    nq = t // bq

    def body(q_ref, k_ref, v_ref, c_ref, crow_ref, gq_ref, gk_ref, o_ref, lse_ref):
        hp = pl.program_id(0)
        lse_ref[...] = jnp.zeros_like(lse_ref)
        outs = []
        for hh in range(2):
            sl = slice(hh * HEAD_DIM, (hh + 1) * HEAD_DIM)
            qn = _qhead(q_ref[:, sl], gq_ref[...]).astype(BF16)
            kn = _rms(k_ref[:, sl], gk_ref[...]).astype(BF16)
            v_t = v_ref[:, sl].T.astype(BF16)
            ckey = _column(c_ref[...], 2 * hp + hh)
            cq = crow_ref[0, hh:hh + 1, :]
            o_blocks = []
            for i in range(nq):
                cols = slice(i * bq, (i + 1) * bq)
                length = (i + 1) * bq
                s = _fox_scores(kn[:length], qn[cols], ckey, cq[:, cols], i, bq)
                m = jnp.max(s, axis=0, keepdims=True)
                p = jnp.exp(s - m)
                l = jnp.sum(p, axis=0, keepdims=True)
                o_blocks.append((_dot(v_t[:, :length], p.astype(BF16)) / l).T)
                lse_ref[0, hh:hh + 1, cols] = m + jnp.log(l)
            outs.append(jnp.concatenate(o_blocks, axis=0))
        o_ref[...] = jnp.concatenate(outs, axis=1).astype(BF16)

    col = lambda off: (lambda h: (0, off + h))
    fixed = lambda h: (0, 0)
    return pl.pallas_call(
        body, name=name, grid=(npair,),
        in_specs=[pl.BlockSpec((t, LANES), col(0)), pl.BlockSpec((t, LANES), col(npair)), pl.BlockSpec((t, LANES), col(2 * npair)),
                  pl.BlockSpec((t, LANES), fixed), pl.BlockSpec((1, 2, t), lambda h: (h, 0, 0)),
                  pl.BlockSpec((1, HEAD_DIM), fixed), pl.BlockSpec((1, HEAD_DIM), fixed)],
        out_specs=[pl.BlockSpec((t, LANES), col(0)), pl.BlockSpec((1, 8, t), lambda h: (h, 0, 0))],
        out_shape=[_sds((t, hw), BF16), _sds((npair, 8, t), F32)], compiler_params=_params(1),
    )(proj, proj, proj, c, crow, gq, gk)


def _fox_bwd(name, proj, c, crow, gq, gk, lse, do, n_heads, after=None):
    t = proj.shape[0]
    hw = n_heads * HEAD_DIM
    npair = n_heads // 2
    bq = min(256, t)
    nq = t // bq

    def body(q_ref, k_ref, v_ref, c_ref, crow_ref, gq_ref, gk_ref, lse_ref, do_ref, *rest):
        dq_ref, dk_ref, dv_ref, dc_ref, dgq_ref, dgk_ref, dk_acc, dv_acc, dc_acc = rest[-9:]
        hp = pl.program_id(0)

        @pl.when(hp == 0)
        def _():
            dgq_ref[...] = jnp.zeros_like(dgq_ref)
            dgk_ref[...] = jnp.zeros_like(dgk_ref)
            dc_ref[...] = jnp.zeros_like(dc_ref)

        lane = lax.broadcasted_iota(jnp.int32, (t, LANES), 1)
        dqs, dks, dvs = [], [], []
        for hh in range(2):
            sl = slice(hh * HEAD_DIM, (hh + 1) * HEAD_DIM)
            qf, q_vjp = jax.vjp(_qhead, q_ref[:, sl], gq_ref[...])
            kf, k_vjp = jax.vjp(_rms, k_ref[:, sl], gk_ref[...])
            qn, kn, kn_t = qf.astype(BF16), kf.astype(BF16), kf.T.astype(BF16)
            vb = v_ref[:, sl].astype(BF16)
            dob = do_ref[:, sl]
            ckey = _column(c_ref[...], 2 * hp + hh)
            cq = crow_ref[0, hh:hh + 1, :]
            lse_h = lse_ref[0, hh:hh + 1, :]
            dk_acc[...] = jnp.zeros_like(dk_acc)
            dv_acc[...] = jnp.zeros_like(dv_acc)
            dc_acc[...] = jnp.zeros_like(dc_acc)
            dq_blocks = []
            for i in range(nq):
                cols = slice(i * bq, (i + 1) * bq)
                length = (i + 1) * bq
                qi, doi = qn[cols], dob[cols]
                s = _fox_scores(kn[:length], qi, ckey, cq[:, cols], i, bq)
                p = jnp.exp(s - lse_h[:, cols])
                dp = _dot_nt(vb[:length], doi)
                ds = p * (dp - jnp.sum(p * dp, axis=0, keepdims=True))
                dsb = ds.astype(BF16)
                dq_blocks.append(_dot(kn_t[:, :length], dsb).T)
                dk_acc[0:length, :] += _dot(dsb, qi)
                dv_acc[0:length, :] += _dot(p.astype(BF16), doi)
                part = ds[:, 0:LANES]
                for j in range(1, bq // LANES):
                    part = part + ds[:, j * LANES:(j + 1) * LANES]
                dc_acc[0:length, :] += part
            dqp, dgq = q_vjp(jnp.concatenate(dq_blocks, axis=0))
            dkp, dgk = k_vjp(dk_acc[...])
            dgq_ref[...] += dgq
            dgk_ref[...] += dgk
            dqs.append(dqp)
            dks.append(dkp)
            dvs.append(dv_acc[...])
            dc_ref[...] = jnp.where(lane == 2 * hp + hh, -jnp.sum(dc_acc[...], axis=1, keepdims=True), dc_ref[...])
        dq_ref[...] = jnp.concatenate(dqs, axis=1).astype(BF16)
        dk_ref[...] = jnp.concatenate(dks, axis=1).astype(BF16)
        dv_ref[...] = jnp.concatenate(dvs, axis=1).astype(BF16)

    col = lambda off: (lambda h: (0, off + h))
    fixed = lambda h: (0, 0)
    pair_blk = pl.BlockSpec((t, LANES), col(0))
    extra_specs, extra = _after_operand(after)
    return pl.pallas_call(
        body, name=name, grid=(npair,),
        in_specs=[pl.BlockSpec((t, LANES), col(0)), pl.BlockSpec((t, LANES), col(npair)), pl.BlockSpec((t, LANES), col(2 * npair)),
                  pl.BlockSpec((t, LANES), fixed), pl.BlockSpec((1, 2, t), lambda h: (h, 0, 0)),
                  pl.BlockSpec((1, HEAD_DIM), fixed), pl.BlockSpec((1, HEAD_DIM), fixed),
                  pl.BlockSpec((1, 8, t), lambda h: (h, 0, 0)), pair_blk] + extra_specs,
        out_specs=[pair_blk, pair_blk, pair_blk, pl.BlockSpec((t, LANES), fixed),
                   pl.BlockSpec((1, HEAD_DIM), fixed), pl.BlockSpec((1, HEAD_DIM), fixed)],
        out_shape=[_sds((t, hw), BF16), _sds((t, hw), BF16), _sds((t, hw), BF16), _sds((t, LANES), F32),
                   _sds((1, HEAD_DIM), F32), _sds((1, HEAD_DIM), F32)],
        scratch_shapes=[pltpu.VMEM((t, HEAD_DIM), F32), pltpu.VMEM((t, HEAD_DIM), F32), pltpu.VMEM((t, LANES), F32)],
        compiler_params=_params(1),
    )(proj, proj, proj, c, crow, gq, gk, lse, do, *extra)


def _t5_bucket_table():
    dist = np.arange(WINDOW)[None, :] + WINDOW - np.arange(2 * WINDOW)[:, None]
    n = np.maximum(dist, 0)
    max_exact = N_BUCKETS // 2
    large = max_exact + (np.log(np.maximum(n, 1) / max_exact) / np.log(REL_MAX_DIST / max_exact)
                         * (N_BUCKETS - max_exact)).astype(np.int32)
    large = np.minimum(large, N_BUCKETS - 1)
    return np.where(n < max_exact, n, large).astype(np.int32).reshape(1, -1)


def _bias_expand(name, rel_bias_t):
    n_heads = rel_bias_t.shape[0]
    tbl = jnp.asarray(_t5_bucket_table())
    width = tbl.shape[1]

    def body(rb_ref, tbl_ref, o_ref):
        onehot = (lax.broadcasted_iota(jnp.int32, (N_BUCKETS, width), 0) == tbl_ref[...]).astype(F32)
        o_ref[...] = _dot_exact(rb_ref[...], onehot)

    return pl.pallas_call(body, name=name, out_shape=_sds((n_heads, width), F32), compiler_params=_params(0))(rel_bias_t, tbl)


def _bias_reduce(name, dbias):
    n_heads, width = dbias.shape
    tbl = jnp.asarray(_t5_bucket_table())

    def body(db_ref, tbl_ref, o_ref):
        onehot = (lax.broadcasted_iota(jnp.int32, (N_BUCKETS, width), 0) == tbl_ref[...]).astype(F32)
        o_ref[...] = lax.dot_general(db_ref[...], onehot, (((1,), (1,)), ((), ())), preferred_element_type=F32,
                                     precision=lax.Precision.HIGHEST)

    return pl.pallas_call(body, name=name, out_shape=_sds((n_heads, N_BUCKETS), F32), compiler_params=_params(0))(dbias, tbl)


def _swa_mask(n, group):
    j = lax.broadcasted_iota(jnp.int32, (2 * WINDOW, group * WINDOW), 0)
    i = lax.broadcasted_iota(jnp.int32, (2 * WINDOW, group * WINDOW), 1) & (WINDOW - 1)
    ok = (j > i) & (j <= i + WINDOW) & ((n > 0) | (j >= WINDOW))
    return jnp.where(ok, 0.0, NEG)


def _swa_stack(ref, start, group):
    return jnp.concatenate([ref[pl.ds(start, WINDOW), g * HEAD_DIM:(g + 1) * HEAD_DIM] for g in range(group)], axis=0)


def _kv_head(ref, n_kv):
    out = ref[:, 0:HEAD_DIM]
    for h in range(1, n_kv):
        out = jnp.where(pl.program_id(0) == h, ref[:, h * HEAD_DIM:(h + 1) * HEAD_DIM], out)
    return out


def _swa_fwd(name, qb, kv, gq, gk, sinks, bias, group):
    t = qb.shape[0]
    kvh = kv.shape[1] // (2 * HEAD_DIM)
    nblk = t // WINDOW
    gw = group * HEAD_DIM
    band = 2 * WINDOW
    cols = group * WINDOW

    def body(q_ref, k_ref, v_ref, gq_ref, gk_ref, sink_ref, bias_ref, o_ref, lse_ref, qs, kpad, vpad):
        for g in range(group):
            qs[:, g * HEAD_DIM:(g + 1) * HEAD_DIM] = _qhead(q_ref[:, g * HEAD_DIM:(g + 1) * HEAD_DIM], gq_ref[...]).astype(BF16)
        kpad[0:WINDOW, :] = jnp.zeros((WINDOW, HEAD_DIM), BF16)
        vpad[0:WINDOW, :] = jnp.zeros((WINDOW, HEAD_DIM), BF16)
        kpad[WINDOW:, :] = _rms(_kv_head(k_ref, kvh), gk_ref[...]).astype(BF16)
        vpad[WINDOW:, :] = _kv_head(v_ref, kvh).astype(BF16)
        sink = sink_ref[0]

        def block(n, carry):
            start = pl.multiple_of(n * WINDOW, WINDOW)
            kb = kpad[pl.ds(start, band), :]
            vb = vpad[pl.ds(start, band), :]
            s = _dot_nt(kb, _swa_stack(qs, start, group)) + bias_ref[0] + _swa_mask(n, group)
            m = jnp.maximum(jnp.max(s, axis=0, keepdims=True), sink)
            e = jnp.exp(s - m)
            l = jnp.sum(e, axis=0, keepdims=True) + jnp.exp(sink - m)
            o_t = _dot_tn(vb, e.astype(BF16)) / l
            for g in range(group):
                o_ref[pl.ds(start, WINDOW), g * HEAD_DIM:(g + 1) * HEAD_DIM] = o_t[:, g * WINDOW:(g + 1) * WINDOW].T.astype(BF16)
            lse_ref[pl.ds(n, 1), :] = m + jnp.log(l)
            return carry

        lax.fori_loop(0, nblk, block, 0)

    fixed = lambda h: (0, 0)
    per = lambda h: (h, 0, 0)
    return pl.pallas_call(
        body, name=name, grid=(kvh,),
        in_specs=[pl.BlockSpec((t, gw), lambda h: (0, h)), pl.BlockSpec((t, kvh * HEAD_DIM), lambda h: (0, 0)),
                  pl.BlockSpec((t, kvh * HEAD_DIM), lambda h: (0, 1)),
                  pl.BlockSpec((1, HEAD_DIM), fixed), pl.BlockSpec((1, HEAD_DIM), fixed),
                  pl.BlockSpec((1, 1, cols), per), pl.BlockSpec((1, band, cols), per)],
        out_specs=[pl.BlockSpec((t, gw), lambda h: (0, h)), pl.BlockSpec((nblk, cols), lambda h: (h, 0))],
        out_shape=[_sds((t, kvh * gw), BF16), _sds((kvh * nblk, cols), F32)],
        scratch_shapes=[pltpu.VMEM((t, gw), BF16), pltpu.VMEM((t + WINDOW, HEAD_DIM), BF16),
                        pltpu.VMEM((t + WINDOW, HEAD_DIM), BF16)],
        compiler_params=_params(1),
    )(qb, kv, kv, gq, gk, sinks, bias)


def _swa_bwd(name, qb, kv, gq, gk, sinks, bias, lse, do, group):
    t = qb.shape[0]
    kvh = kv.shape[1] // (2 * HEAD_DIM)
    kvw = kvh * HEAD_DIM
    nblk = t // WINDOW
    gw = group * HEAD_DIM
    band = 2 * WINDOW
    cols = group * WINDOW

    def body(q_ref, k_ref, v_ref, gq_ref, gk_ref, sink_ref, bias_ref, lse_ref, do_ref,
             dq_ref, dkv_ref, dgq_ref, dgk_ref, dsink_ref, dbias_ref,
             qs, kpad, vpad, dqs, dk_acc, dv_acc, dsink_acc):
        @pl.when(pl.program_id(0) == 0)
        def _():
            dgq_ref[...] = jnp.zeros_like(dgq_ref)
            dgk_ref[...] = jnp.zeros_like(dgk_ref)
            dkv_ref[...] = jnp.zeros_like(dkv_ref)

        for g in range(group):
            qs[:, g * HEAD_DIM:(g + 1) * HEAD_DIM] = _qhead(q_ref[:, g * HEAD_DIM:(g + 1) * HEAD_DIM], gq_ref[...]).astype(BF16)
        kpad[0:WINDOW, :] = jnp.zeros((WINDOW, HEAD_DIM), BF16)
        vpad[0:WINDOW, :] = jnp.zeros((WINDOW, HEAD_DIM), BF16)
        kpad[WINDOW:, :] = _rms(_kv_head(k_ref, kvh), gk_ref[...]).astype(BF16)
        vpad[WINDOW:, :] = _kv_head(v_ref, kvh).astype(BF16)
        dk_acc[...] = jnp.zeros_like(dk_acc)
        dv_acc[...] = jnp.zeros_like(dv_acc)
        dsink_acc[...] = jnp.zeros_like(dsink_acc)
        dbias_ref[...] = jnp.zeros_like(dbias_ref)
        sink = sink_ref[0]

        def block(n, carry):
            start = pl.multiple_of(n * WINDOW, WINDOW)
            kb = kpad[pl.ds(start, band), :]
            vb = vpad[pl.ds(start, band), :]
            q = _swa_stack(qs, start, group)
            dob = _swa_stack(do_ref, start, group)
            lse_n = lse_ref[pl.ds(n, 1), :]
            s = _dot_nt(kb, q) + bias_ref[0] + _swa_mask(n, group)
            p = jnp.exp(s - lse_n)
            dp = _dot_nt(vb, dob)
            dsum = jnp.sum(p * dp, axis=0, keepdims=True)
            ds = p * (dp - dsum)
            dsb = ds.astype(BF16)
            dsink_acc[...] -= jnp.exp(sink - lse_n) * dsum
            dbias_ref[0] += ds
            dq = _dot_tn(dsb, kb)
            for g in range(group):
                dqs[pl.ds(start, WINDOW), g * HEAD_DIM:(g + 1) * HEAD_DIM] = dq[g * WINDOW:(g + 1) * WINDOW]
            dk_acc[pl.ds(start, band), :] += _dot(dsb, q)
            dv_acc[pl.ds(start, band), :] += _dot(p.astype(BF16), dob)
            return carry

        lax.fori_loop(0, nblk, block, 0)
        for g in range(group):
            _, q_vjp = jax.vjp(_qhead, q_ref[:, g * HEAD_DIM:(g + 1) * HEAD_DIM], gq_ref[...])
            dqp, dgq = q_vjp(dqs[:, g * HEAD_DIM:(g + 1) * HEAD_DIM])
            dq_ref[:, g * HEAD_DIM:(g + 1) * HEAD_DIM] = dqp.astype(BF16)
            dgq_ref[...] += dgq
            dsink_g = jnp.sum(dsink_acc[:, g * WINDOW:(g + 1) * WINDOW], axis=1, keepdims=True)
            dsink_ref[0, g:g + 1, :] = jnp.broadcast_to(dsink_g, (1, LANES))
        _, k_vjp = jax.vjp(_rms, _kv_head(k_ref, kvh), gk_ref[...])
        dkp, dgk = k_vjp(dk_acc[WINDOW:, :])
        dgk_ref[...] += dgk
        mine = lax.broadcasted_iota(jnp.int32, (t, kvw), 1) // HEAD_DIM == pl.program_id(0)
        dkv_ref[:, 0:kvw] = jnp.where(mine, jnp.concatenate([dkp] * kvh, axis=1), dkv_ref[:, 0:kvw])
        dkv_ref[:, kvw:] = jnp.where(mine, jnp.concatenate([dv_acc[WINDOW:, :]] * kvh, axis=1), dkv_ref[:, kvw:])

    fixed = lambda h: (0, 0)
    per = lambda h: (h, 0, 0)
    wide = pl.BlockSpec((t, gw), lambda h: (0, h))
    vec = pl.BlockSpec((1, HEAD_DIM), fixed)
    bias_spec = pl.BlockSpec((1, band, cols), per)
    return pl.pallas_call(
        body, name=name, grid=(kvh,),
        in_specs=[wide, pl.BlockSpec((t, kvw), lambda h: (0, 0)), pl.BlockSpec((t, kvw), lambda h: (0, 1)), vec, vec,
                  pl.BlockSpec((1, 1, cols), per), bias_spec, pl.BlockSpec((nblk, cols), lambda h: (h, 0)), wide],
        out_specs=[wide, pl.BlockSpec((t, 2 * kvw), fixed), vec, vec, pl.BlockSpec((1, group, LANES), per), bias_spec],
        out_shape=[_sds((t, kvh * gw), BF16), _sds((t, 2 * kvw), F32),
                   _sds((1, HEAD_DIM), F32), _sds((1, HEAD_DIM), F32),
                   _sds((kvh, group, LANES), F32), _sds((kvh, band, cols), F32)],
        scratch_shapes=[pltpu.VMEM((t, gw), BF16), pltpu.VMEM((t + WINDOW, HEAD_DIM), BF16),
                        pltpu.VMEM((t + WINDOW, HEAD_DIM), BF16), pltpu.VMEM((t, gw), F32),
                        pltpu.VMEM((t + WINDOW, HEAD_DIM), F32), pltpu.VMEM((t + WINDOW, HEAD_DIM), F32),
                        pltpu.VMEM((1, cols), F32)],
        compiler_params=_params(1),
    )(qb, kv, kv, gq, gk, sinks, bias, lse, do)


def _local_step(x, target, p, comm):
    t, d = x.shape
    n_heads = d // HEAD_DIM
    kv_heads = n_heads // 8
    group = n_heads // kv_heads
    hw = n_heads * HEAD_DIM
    gate_col = 3 * hw // LANES
    kvw = kv_heads * HEAD_DIM
    grads = {}

    def mlp_fwd(tag, h, g, layer, last=False):
        w_up, = comm.weights([f"w_up{layer}"], h)
        a, hn = _norm_matmul(f"{tag}_up", h, g, w_up, relu2=True)
        w_down, = comm.weights([f"w_down{layer}"], a)
        out = _matmul_res_loss(f"{tag}_down", a, w_down, h, target) if last else _matmul_res(f"{tag}_down", a, w_down, h)
        return out, (h, g, hn, a, w_up, w_down)

    def mlp_bwd(tag, saved, layer, dy):
        h, g, hn, a, w_up, w_down = saved
        du = _matmul_nt(f"{tag}_du", dy, w_down, a=a)
        dw_down = _matmul_tn(f"{tag}_dwdown", a, dy)
        dw_up = _matmul_tn(f"{tag}_dwup", hn, du, col_blocks=w_up.shape[0])
        sent = comm.send_grads(tag, {f"w_down{layer}": dw_down, f"w_up{layer}": dw_up})
        return _matmul_nt_rmsbwd(f"{tag}_dh", du, w_up, h, g, dy, after=sent)

    bias = _bias_expand("b_bias", p["rel_bias"].T).reshape(kv_heads, group, 2 * WINDOW, WINDOW)
    bias = bias.transpose(0, 2, 1, 3).reshape(kv_heads, 2 * WINDOW, group * WINDOW)
    comm.prefetch(["w_in_a"], bias)
    w_in, = comm.weights(["w_in_a"], None)
    proj, xn1 = _norm_matmul("a_inproj", x, p["g_attn"][0], w_in, tn=640, w_rows=True)
    ahead = comm.prefetch(["w_out_a"], proj)
    b_pad = jnp.pad(p["b_f"], ((0, 0), (0, LANES - n_heads)))
    c = _gate_fwd("a_gate", proj, b_pad, n_heads, gate_col, after=ahead)
    crow = c[:, :n_heads].T.reshape(n_heads // 2, 2, t)
    o_a, lse_a = _fox_fwd("a_attn", proj, c, crow, p["gq_a"], p["gk_a"], n_heads)
    ahead = comm.prefetch(["w_up0", "w_down0", "w_kv", "w_q_b", "w_out_b"], o_a)
    w_out_a, = comm.weights(["w_out_a"], o_a)
    h1 = _matmul_res("a_outproj", o_a, w_out_a, x, after=ahead)
    h2, mlp0 = mlp_fwd("mlp0", h1, p["g_mlp"][0], 0)

    ahead = comm.prefetch(["w_up1", "w_down1"], h2)
    w_kv, w_q_b = comm.weights(["w_kv", "w_q_b"], h2)
    kv, hn_kv = _norm_matmul("kv_proj", h2, p["g_kv"], w_kv, tn=2 * kvw, after=ahead)
    qb, hn_q = _norm_matmul("b_qproj", h2, p["g_attn"][1], w_q_b, tn=512)
    gqb, gkb = p["gq_b"], p["gk_b"].reshape(1, HEAD_DIM)
    sink_rows = jnp.broadcast_to(p["sinks"].reshape(kv_heads, 1, group, 1), (kv_heads, 1, group, WINDOW)).reshape(kv_heads, 1, group * WINDOW)
    o_b, lse_b = _swa_fwd("b_attn", qb, kv, gqb, gkb, sink_rows, bias, group)
    w_out_b, = comm.weights(["w_out_b"], o_b)
    h3 = _matmul_res("b_outproj", o_b, w_out_b, h2)
    (dy, loss_tile), mlp1 = mlp_fwd("mlp1", h3, p["g_mlp"][1], 1, last=True)

    dh3, dg_mlp1 = mlp_bwd("mlp1", mlp1, 1, dy)
    do_b = _matmul_nt("b_do", dh3, w_out_b)
    dw_out_b = _matmul_tn("b_dwout", o_b, dh3)
    dqb, dkv, grads["gq_b"], dgk_b, dsink, dbias = _swa_bwd(
        "b_attn_bwd", qb, kv, gqb, gkb, sink_rows, bias, lse_b, do_b, group)
    grads["gk_b"] = dgk_b
    grads["sinks"] = dsink[:, :, 0].reshape(1, n_heads)
    dbias = dbias.reshape(kv_heads, 2 * WINDOW, group, WINDOW).transpose(0, 2, 1, 3)
    grads["rel_bias"] = _bias_reduce("b_dbias", dbias.reshape(n_heads, WINDOW * 2 * WINDOW)).T
    dw_q_b = _matmul_tn("b_dwq", hn_q, dqb)
    dh2, dg_attn1 = _matmul_nt_rmsbwd("b_dhq", dqb, w_q_b, h2, p["g_attn"][1], dh3)
    dw_kv = _matmul_tn("kv_dw", hn_kv, dkv)
    sent = comm.send_grads("attn_b", {"w_out_b": dw_out_b, "w_q_b": dw_q_b, "w_kv": dw_kv})
    dh2, dg_kv = _matmul_nt_rmsbwd("kv_dh", dkv, w_kv, h2, p["g_kv"], dh2, after=sent)
    grads["g_kv"] = dg_kv
    dh1, dg_mlp0 = mlp_bwd("mlp0", mlp0, 0, dh2)
    grads["g_mlp"] = (dg_mlp0, dg_mlp1)

    do_a = _matmul_nt("a_do", dh1, w_out_a)
    dw_out_a = _matmul_tn("a_dwout", o_a, dh1)
    sent = comm.send_grads("attn_a_out", {"w_out_a": dw_out_a})
    dq, dk, dv, dc, grads["gq_a"], grads["gk_a"] = _fox_bwd(
        "a_attn_bwd", proj, c, crow, p["gq_a"], p["gk_a"], lse_a, do_a, n_heads, after=sent)
    dfl, db_f = _gate_bwd("a_gate_bwd", proj, b_pad, dc, n_heads, gate_col)
    grads["b_f"] = db_f
    dproj = jnp.concatenate([dq, dk, dv, dfl], axis=1)
    dw_in = _matmul_tn("a_dwin", dproj, xn1, tk=640)
    sent = comm.send_grads("attn_a_in", {"w_in_a": dw_in})
    grad_x, dg_attn0 = _matmul_nt_rmsbwd("a_dx", dproj, w_in, x, p["g_attn"][0], dh1, w_rows=True, after=sent)
    grads["g_attn"] = (dg_attn0, dg_attn1)
    return loss_tile, grad_x, grads


EVERYONE = (1, 2, 3, 4, 5, 6, 7)
SAME_CORE = (1, 2, 4, 6)
OTHER_CHIPS = (2, 4, 6)


class _InFlight:
    def __init__(self, scatter, ks, send_sems, recv_sems, srcs, lands, token):
        self.scatter, self.ks, self.send_sems, self.recv_sems = scatter, ks, send_sems, recv_sems
        self.srcs, self.lands, self.token = list(srcs), list(lands), token


def _mesh_peers(ks=EVERYONE):
    x, y, c = lax.axis_index("x"), lax.axis_index("y"), lax.axis_index("c")
    peers = []
    for k in ks:
        px, py, pc = x ^ ((k >> 2) & 1), y ^ ((k >> 1) & 1), c ^ (k & 1)
        peers.append(((px, py, pc), 4 * px + 2 * py + pc))
    return 4 * x + 2 * y + c, peers


_HBM_SPEC = pl.BlockSpec(memory_space=pltpu.HBM)
_SEM_SPEC = pl.BlockSpec(memory_space=pltpu.SEMAPHORE)
_SIDE_EFFECT = pltpu.SideEffectType.DATAFLOW_SIDE_EFFECTING


def _exchange_start(name, arrays, scatter, ks=EVERYONE):
    n = len(arrays)
    me, _ = _mesh_peers()
    lands = []
    for a in arrays:
        own = lax.dynamic_index_in_dim(a, me, 0, keepdims=False) if scatter else a
        shape = a.shape if scatter else (N_DEV,) + a.shape
        lands.append(lax.dynamic_update_index_in_dim(lax.empty(shape, a.dtype), own, me, 0))

    def body(*refs):
        src, land = refs[:n], refs[n:2 * n]
        send_sems, recv_sems, token = refs[2 * n], refs[2 * n + 1], refs[-1]
        pos, peers = _mesh_peers(ks)
        for a in range(n):
            for k, (peer, peer_pos) in enumerate(peers):
                pltpu.make_async_remote_copy(
                    src_ref=src[a].at[peer_pos] if scatter else src[a], dst_ref=land[a].at[pos],
                    send_sem=send_sems.at[a * len(ks) + k], recv_sem=recv_sems.at[a * len(ks) + k],
                    device_id=peer, device_id_type=pl.DeviceIdType.MESH).start()
        token[...] = jnp.zeros_like(token)

    operands = [pltpu.with_memory_space_constraint(a, pltpu.HBM) for a in list(arrays) + lands]
    outs = pl.pallas_call(
        body, name=name,
        out_shape=(pltpu.SemaphoreType.DMA((n * len(ks),)), pltpu.SemaphoreType.DMA((n * len(ks),)),
                   *[pltpu.HBM(a.shape, a.dtype) for a in operands], _sds((8, LANES), F32)),
        in_specs=[_HBM_SPEC] * (2 * n),
        out_specs=(_SEM_SPEC, _SEM_SPEC, *[_HBM_SPEC] * (2 * n), pl.BlockSpec(memory_space=pltpu.VMEM)),
        input_output_aliases={i: 2 + i for i in range(2 * n)},
        compiler_params=pltpu.CompilerParams(has_side_effects=_SIDE_EFFECT),
    )(*operands)
    return _InFlight(scatter, ks, outs[0], outs[1], outs[2:2 + n], outs[2 + n:2 + 2 * n], outs[-1])


def _exchange_wait(name, flight, which, after):
    m = len(which)
    scatter, ks = flight.scatter, flight.ks

    def body(*refs):
        src, land = refs[:m], refs[m:2 * m]
        send_sems, recv_sems = refs[2 * m], refs[2 * m + 1]
        _, peers = _mesh_peers(ks)
        for i, a in enumerate(which):
            for k, (peer, peer_pos) in enumerate(peers):
                cp = pltpu.make_async_remote_copy(
                    src_ref=src[i].at[peer_pos] if scatter else src[i], dst_ref=land[i].at[peer_pos],
                    send_sem=send_sems.at[a * len(ks) + k], recv_sem=recv_sems.at[a * len(ks) + k],
                    device_id=peer, device_id_type=pl.DeviceIdType.MESH)
                cp.wait_send()
                cp.wait_recv()

    operands = [flight.srcs[a] for a in which] + [flight.lands[a] for a in which]
    outs = pl.pallas_call(
        body, name=name, out_shape=tuple(pltpu.HBM(a.shape, a.dtype) for a in operands),
        in_specs=[_HBM_SPEC] * (2 * m) + [_SEM_SPEC, _SEM_SPEC, pl.BlockSpec(memory_space=pl.ANY)],
        out_specs=tuple([_HBM_SPEC] * (2 * m)), input_output_aliases={i: i for i in range(2 * m)},
        compiler_params=pltpu.CompilerParams(has_side_effects=_SIDE_EFFECT),
    )(*operands, flight.send_sems, flight.recv_sems, after)
    return list(outs[m:])


def _relay_start(name, lands):
    n = len(lands)

    def body(*refs):
        land, send_sems, recv_sems, token = refs[:n], refs[n], refs[n + 1], refs[-1]
        _, peers = _mesh_peers(OTHER_CHIPS)
        sibling = (lax.axis_index("x"), lax.axis_index("y"), 1 - lax.axis_index("c"))
        for a in range(n):
            for k, (_, peer_pos) in enumerate(peers):
                pltpu.make_async_remote_copy(
                    src_ref=land[a].at[peer_pos], dst_ref=land[a].at[peer_pos],
                    send_sem=send_sems.at[a * len(peers) + k], recv_sem=recv_sems.at[a * len(peers) + k],
                    device_id=sibling, device_id_type=pl.DeviceIdType.MESH).start()
        token[...] = jnp.zeros_like(token)

    count = n * len(OTHER_CHIPS)
    outs = pl.pallas_call(
        body, name=name,
        out_shape=(pltpu.SemaphoreType.DMA((count,)), pltpu.SemaphoreType.DMA((count,)),
                   *[pltpu.HBM(a.shape, a.dtype) for a in lands], _sds((8, LANES), F32)),
        in_specs=[_HBM_SPEC] * n,
        out_specs=(_SEM_SPEC, _SEM_SPEC, *[_HBM_SPEC] * n, pl.BlockSpec(memory_space=pltpu.VMEM)),
        input_output_aliases={i: 2 + i for i in range(n)},
        compiler_params=pltpu.CompilerParams(has_side_effects=_SIDE_EFFECT),
    )(*[pltpu.with_memory_space_constraint(a, pltpu.HBM) for a in lands])
    return _InFlight(False, OTHER_CHIPS, outs[0], outs[1], [], outs[2:2 + n], outs[-1])


def _relay_wait(name, flight, which, after):
    m = len(which)

    def body(*refs):
        land, send_sems, recv_sems = refs[:m], refs[m], refs[m + 1]
        _, peers = _mesh_peers(OTHER_CHIPS)
        sibling = (lax.axis_index("x"), lax.axis_index("y"), 1 - lax.axis_index("c"))
        for i, a in enumerate(which):
            for k, (_, peer_pos) in enumerate(peers):
                cp = pltpu.make_async_remote_copy(
                    src_ref=land[i].at[peer_pos], dst_ref=land[i].at[peer_pos ^ 1],
                    send_sem=send_sems.at[a * len(peers) + k], recv_sem=recv_sems.at[a * len(peers) + k],
                    device_id=sibling, device_id_type=pl.DeviceIdType.MESH)
                cp.wait_send()
                cp.wait_recv()

    operands = [flight.lands[a] for a in which]
    outs = pl.pallas_call(
        body, name=name, out_shape=tuple(pltpu.HBM(a.shape, a.dtype) for a in operands),
        in_specs=[_HBM_SPEC] * m + [_SEM_SPEC, _SEM_SPEC, pl.BlockSpec(memory_space=pl.ANY)],
        out_specs=tuple([_HBM_SPEC] * m), input_output_aliases={i: i for i in range(m)},
        compiler_params=pltpu.CompilerParams(has_side_effects=_SIDE_EFFECT),
    )(*operands, flight.send_sems, flight.recv_sems, after)
    return list(outs)


def _sum_parts(p_ref):
    g = p_ref[0].astype(F32)
    for dev in range(1, N_DEV):
        g = g + p_ref[dev].astype(F32)
    return g


def _adam_update(g, w, m, v):
    m_new = ADAM_B1 * m + (1.0 - ADAM_B1) * g
    v_new = ADAM_B2 * v + (1.0 - ADAM_B2) * jnp.square(g)
    m_hat = m_new / (1.0 - ADAM_B1 ** ADAM_STEP)
    v_hat = v_new / (1.0 - ADAM_B2 ** ADAM_STEP)
    return -ADAM_LR * (m_hat / (jnp.sqrt(v_hat) + ADAM_EPS) + ADAM_WD * w), m_new, v_new


def _adamw(name, parts, w, m, v, layer=None, into=None):
    r, c = w.shape[-2:]
    tr = 256 if r % 256 == 0 else r
    n_into = 0 if into is None else len(into)

    def body(p_ref, w_ref, m_ref, v_ref, *refs):
        g_ref, d_ref, mo_ref, vo_ref = refs[n_into:]
        g = _sum_parts(p_ref)
        g_ref[...] = g
        d_ref[...], mo_ref[...], vo_ref[...] = _adam_update(g, w_ref[...], m_ref[...], v_ref[...])

    if layer is None:
        blk = pl.BlockSpec((tr, c), lambda i: (i, 0))
    else:
        blk = pl.BlockSpec((None, tr, c), lambda i: (layer, i, 0))
    return pl.pallas_call(
        body, name=name, grid=(r // tr,),
        in_specs=[pl.BlockSpec((N_DEV, tr, c), lambda i: (0, i, 0)), blk, blk, blk] + [pl.BlockSpec(memory_space=pl.ANY)] * n_into,
        out_specs=[blk] * 4, out_shape=[_sds(w.shape, F32)] * 4,
        input_output_aliases={4 + i: i for i in range(n_into)}, compiler_params=_params(1),
    )(parts, w, m, v, *(into or ()))


SMALL_PACK_ROWS = 16
LOSS_ROW = 11


def _small_rows(grads, loss_tile):
    return [(0, 1, grads["g_attn"][0]), (1, 1, grads["g_attn"][1]), (2, 1, grads["g_mlp"][0]), (3, 1, grads["g_mlp"][1]),
            (4, 1, grads["g_kv"]), (5, 1, grads["b_f"]), (6, 1, grads["gq_a"]), (7, 1, grads["gk_a"]), (8, 1, grads["gk_b"]),
            (9, 1, grads["gq_b"]), (10, 1, grads["sinks"]), (LOSS_ROW, 1, loss_tile)]


SMALL_ROWS = {"g_attn": (0, 2), "g_mlp": (2, 2), "g_kv": (4, 1), "b_f": (5, 1), "gq_a": (6, 1), "gk_a": (7, 1),
              "gk_b": (8, 1), "gq_b": (9, 1), "sinks": (10, 1)}


def _pack_small(name, pieces, d):
    def body(*refs):
        out = refs[-1]
        out[...] = jnp.zeros_like(out)
        for (row, rows, _), ref in zip(pieces, refs[:-1]):
            out[row:row + rows, 0:ref.shape[1]] = ref[0:rows, :]

    return pl.pallas_call(body, name=name, out_shape=_sds((SMALL_PACK_ROWS, d), F32), compiler_params=_params(0))(
        *[piece for _, _, piece in pieces])


def _adamw_small(name, parts, parts_rel_bias, w, m, v):
    def body(*refs):
        ins, outs = refs[2:2 + 3 * len(SMALL)], refs[2 + 3 * len(SMALL):]
        pack, rel = _sum_parts(refs[0]), _sum_parts(refs[1])
        for i, k in enumerate(SMALL):
            w_ref, m_ref, v_ref = ins[3 * i:3 * i + 3]
            if k == "rel_bias":
                g = rel
            else:
                row, rows = SMALL_ROWS[k]
                g = pack[row:row + rows, 0:w_ref.shape[1]]
            outs[4 * i][...] = g
            outs[4 * i + 1][...], outs[4 * i + 2][...], outs[4 * i + 3][...] = _adam_update(g, w_ref[...], m_ref[...], v_ref[...])
        outs[-1][...] = pack[LOSS_ROW:LOSS_ROW + 1, 0:LANES]

    operands = [parts, parts_rel_bias] + [t[k] for k in SMALL for t in (w, m, v)]
    out_shape = [_sds(w[k].shape, F32) for k in SMALL for _ in range(4)] + [_sds((1, LANES), F32)]
    outs = pl.pallas_call(body, name=name, out_shape=out_shape, compiler_params=_params(0))(*operands)
    return {k: outs[4 * i:4 * i + 4] for i, k in enumerate(SMALL)}, outs[-1]


class _Comm:
    ORDER = ("w_in_a", "w_out_a", "w_up0", "w_down0", "w_kv", "w_q_b", "w_out_b", "w_up1", "w_down1")

    def __init__(self, shards, d, n_in):
        self.d, self.n_in = d, n_in
        self.flight = _exchange_start("gather_start", [shards[n].astype(BF16) for n in self.ORDER], scatter=False, ks=SAME_CORE)
        self.relays, self.sent = {}, []

    def prefetch(self, names, after):
        which = [self.ORDER.index(n) for n in names]
        landed = _exchange_wait(f"gather_wait_{names[0]}", self.flight, which, self.flight.token if after is None else after)
        relay = _relay_start(f"gather_relay_{names[0]}", landed)
        for n in names:
            self.relays[n] = (relay, names)
        return relay.token

    def weights(self, names, after):
        relay, group = self.relays[names[0]]
        landed = _relay_wait(f"gather_relay_wait_{names[0]}", relay, [group.index(n) for n in names],
                             relay.token if after is None else after)
        return [self._whole(n, g) for n, g in zip(names, landed)]

    def _whole(self, name, g):
        if name == "w_in_a":
            return _join_row_blocks("w_in_join", g, -(-self.n_in // LANES) * LANES)
        if name.startswith("w_up"):
            return g
        return g.reshape(-1, g.shape[-1])

    def _chunks(self, name, g):
        if name == "w_in_a":
            return _split_row_blocks("dw_in_split", g, N_DEV, self.n_in // N_DEV)
        if name.startswith("w_up"):
            return g
        return g.reshape(N_DEV, g.shape[0] // N_DEV, g.shape[1])

    def send_grads(self, tag, partials):
        names = list(partials)
        flight = _exchange_start(f"scatter_start_{tag}", [self._chunks(n, partials[n]) for n in names], scatter=True)
        self.sent.append((tag, flight, names))
        return flight.token

    def received(self, index, after):
        tag, flight, names = self.sent[index]
        landed = _exchange_wait(f"scatter_wait_{tag}", flight, list(range(len(names))), after)
        return dict(zip(names, landed))


def kernel(x, g_attn, g_mlp, w_in_a, b_f, gq_a, gk_a, w_out_a, g_kv, w_kv, gk_b, w_q_b, gq_b, sinks, rel_bias, w_out_b, w_up, w_down, loss_target, m_g_attn, m_g_mlp, m_w_in_a, m_b_f, m_gq_a, m_gk_a, m_w_out_a, m_g_kv, m_w_kv, m_gk_b, m_w_q_b, m_gq_b, m_sinks, m_rel_bias, m_w_out_b, m_w_up, m_w_down, v_g_attn, v_g_mlp, v_w_in_a, v_b_f, v_gq_a, v_gk_a, v_w_out_a, v_g_kv, v_w_kv, v_gk_b, v_w_q_b, v_gq_b, v_sinks, v_rel_bias, v_w_out_b, v_w_up, v_w_down):
    w = dict(g_attn=g_attn, g_mlp=g_mlp, w_in_a=w_in_a, b_f=b_f, gq_a=gq_a, gk_a=gk_a, w_out_a=w_out_a, g_kv=g_kv,
             w_kv=w_kv, gk_b=gk_b, w_q_b=w_q_b, gq_b=gq_b, sinks=sinks, rel_bias=rel_bias, w_out_b=w_out_b,
             w_up=w_up, w_down=w_down)
    mom = dict(g_attn=m_g_attn, g_mlp=m_g_mlp, w_in_a=m_w_in_a, b_f=m_b_f, gq_a=m_gq_a, gk_a=m_gk_a, w_out_a=m_w_out_a,
               g_kv=m_g_kv, w_kv=m_w_kv, gk_b=m_gk_b, w_q_b=m_w_q_b, gq_b=m_gq_b, sinks=m_sinks, rel_bias=m_rel_bias,
               w_out_b=m_w_out_b, w_up=m_w_up, w_down=m_w_down)
    var = dict(g_attn=v_g_attn, g_mlp=v_g_mlp, w_in_a=v_w_in_a, b_f=v_b_f, gq_a=v_gq_a, gk_a=v_gk_a, w_out_a=v_w_out_a,
               g_kv=v_g_kv, w_kv=v_w_kv, gk_b=v_gk_b, w_q_b=v_w_q_b, gq_b=v_gq_b, sinks=v_sinks, rel_bias=v_rel_bias,
               w_out_b=v_w_out_b, w_up=v_w_up, w_down=v_w_down)
    d = x.shape[2]
    where = {"w_in_a": ("w_in_a", 0), "w_out_a": ("w_out_a", 0), "w_kv": ("w_kv", None), "w_q_b": ("w_q_b", 0),
             "w_out_b": ("w_out_b", 0), "w_up0": ("w_up", 0), "w_up1": ("w_up", 1), "w_down0": ("w_down", 0),
             "w_down1": ("w_down", 1)}
    flip = lambda tree: {**tree, "w_in_a": jnp.swapaxes(tree["w_in_a"], 1, 2)}
    w, mom, var = flip(w), flip(mom), flip(var)
    shards = {n: (w[k] if layer is None else w[k][layer]) for n, (k, layer) in where.items()}
    comm = _Comm(shards, d, w_in_a.shape[2] * N_DEV)
    loss_tile, grad_x, grads = _local_step(x[0], loss_target[0], {k: w[k] for k in SMALL}, comm)

    small_flight = _exchange_start(
        "gather_small_grads", [_pack_small("pack_small", _small_rows(grads, loss_tile), d), grads["rel_bias"]], scatter=False)
    res, after = {}, small_flight.token
    for index in range(len(comm.sent)):
        for n, parts in comm.received(index, after).items():
            k, layer = where[n]
            res[k] = _adamw(f"adam_{n}", parts, w[k], mom[k], var[k], layer, res.get(k))
            after = res[k][0]
    as_rows = lambda tree: {k: tree[k] if tree[k].ndim == 2 else tree[k].reshape(1, -1) for k in SMALL}
    small, loss_row = _adamw_small("adam_small", *_exchange_wait("gather_small_wait", small_flight, [0, 1], after),
                                   as_rows(w), as_rows(mom), as_rows(var))
    loss = loss_row[0, 0]
    for k in SMALL:
        res[k] = [a.reshape(w[k].shape) for a in small[k]]
    res["w_in_a"] = [jnp.swapaxes(a, 1, 2) for a in res["w_in_a"]]

    outs = [loss, grad_x[None]]
    for i in range(4):
        outs.extend(res[k][i] for k in WEIGHTS)
    return tuple(outs)
```

```python
import numpy as np
import jax
import jax.numpy as jnp
from jax import lax
from jax.experimental import pallas as pl
from jax.experimental.pallas import tpu as pltpu

F32 = jnp.float32
BF16 = jnp.bfloat16

N_DEV = 8
HEAD_DIM = 64
WINDOW = 128
N_BUCKETS = 32
REL_MAX_DIST = 128
NORM_EPS = 1e-6
NEG = -1e30
LANES = 128
VMEM_LIMIT = 56 * 1024 * 1024

ADAM_LR = 0.001
ADAM_B1 = 0.9
ADAM_B2 = 0.999
ADAM_EPS = 1e-08
ADAM_WD = 0.01
ADAM_STEP = 10

SMALL = ("g_attn", "g_mlp", "b_f", "gq_a", "gk_a", "g_kv", "gk_b", "gq_b", "sinks", "rel_bias")
WEIGHTS = ("g_attn", "g_mlp", "w_in_a", "b_f", "gq_a", "gk_a", "w_out_a", "g_kv", "w_kv", "gk_b",
           "w_q_b", "gq_b", "sinks", "rel_bias", "w_out_b", "w_up", "w_down")


def _params(n_grid):
    return pltpu.CompilerParams(dimension_semantics=("arbitrary",) * n_grid, vmem_limit_bytes=VMEM_LIMIT)


def _sds(shape, dtype):
    return jax.ShapeDtypeStruct(tuple(shape), dtype)


def _after_operand(after):
    if after is None:
        return [], []
    return [pl.BlockSpec((8, LANES), lambda *_: (0, 0))], [after]


def _rms(x, g):
    return (x * lax.rsqrt(jnp.mean(x * x, axis=-1, keepdims=True) + NORM_EPS)) * g


def _dot_nt(a, b):
    return lax.dot_general(a, b, (((1,), (1,)), ((), ())), preferred_element_type=F32)


def _dot_tn(a, b):
    return lax.dot_general(a, b, (((0,), (0,)), ((), ())), preferred_element_type=F32)


def _dot(a, b):
    return jnp.dot(a, b, preferred_element_type=F32)


def _dot_exact(a, b):
    return jnp.dot(a, b, preferred_element_type=F32, precision=lax.Precision.HIGHEST)


def _norm_matmul(name, x, g, w, *, tn=None, relu2=False, w_rows=False, after=None):
    t, d = x.shape
    blocked = w.ndim == 3
    per_step = 2 if blocked else 1
    if blocked:
        tn = per_step * w.shape[2]
        n = w.shape[0] * w.shape[2]
        w_spec = pl.BlockSpec((per_step, d, w.shape[2]), lambda i, j: (j, 0, 0))
    elif w_rows:
        n = w.shape[0]
        w_spec = pl.BlockSpec((tn, d), lambda i, j: (j, 0))
    else:
        n = w.shape[1]
        w_spec = pl.BlockSpec((d, tn), lambda i, j: (0, j))
    tm = min(1024, t)

    def body(x_ref, g_ref, w_ref, *rest):
        y_ref, xn_ref = rest[-2:]

        @pl.when(pl.program_id(1) == 0)
        def _():
            xn_ref[...] = _rms(x_ref[...], g_ref[...]).astype(BF16)

        for b in range(per_step):
            cols = slice(b * (tn // per_step), (b + 1) * (tn // per_step)) if blocked else slice(None)
            wb = w_ref[b] if blocked else w_ref[...]
            y = _dot_nt(xn_ref[...], wb) if w_rows else _dot(xn_ref[...], wb)
            y_ref[:, cols] = jnp.square(jnp.maximum(y, 0.0)).astype(BF16) if relu2 else y

    extra_specs, extra = _after_operand(after)
    out_shape = [_sds((t, n), BF16 if relu2 else F32), _sds((t, d), BF16)]
    out_specs = [pl.BlockSpec((tm, tn), lambda i, j: (i, j)), pl.BlockSpec((tm, d), lambda i, j: (i, 0))]
    return pl.pallas_call(
        body, name=name, grid=(t // tm, n // tn),
        in_specs=[pl.BlockSpec((tm, d), lambda i, j: (i, 0)), pl.BlockSpec((1, d), lambda i, j: (0, 0)), w_spec] + extra_specs,
        out_specs=out_specs, out_shape=out_shape, compiler_params=_params(2),
    )(x, g.reshape(1, d), w, *extra)


def _matmul_res(name, a, w, res, *, tn=512, after=None):
    t, k = a.shape
    n = w.shape[1]
    tm = min(1024, t)

    def body(a_ref, w_ref, r_ref, *rest):
        rest[-1][...] = r_ref[...] + _dot(a_ref[...], w_ref[...])

    extra_specs, extra = _after_operand(after)
    return pl.pallas_call(
        body, name=name, grid=(t // tm, n // tn),
        in_specs=[pl.BlockSpec((tm, k), lambda i, j: (i, 0)), pl.BlockSpec((k, tn), lambda i, j: (0, j)),
                  pl.BlockSpec((tm, tn), lambda i, j: (i, j))] + extra_specs,
        out_specs=pl.BlockSpec((tm, tn), lambda i, j: (i, j)), out_shape=_sds((t, n), F32),
        compiler_params=_params(2),
    )(a, w, res, *extra)


def _matmul_nt(name, dy, w, *, a=None, tk=1024):
    t, n = dy.shape
    k = w.shape[0]
    tm = min(1024, t)

    def body(dy_ref, w_ref, *rest):
        o_ref = rest[-1]
        r = _dot_nt(dy_ref[...].astype(BF16), w_ref[...])
        if a is not None:
            r = r * (2.0 * jnp.sqrt(rest[0][...].astype(F32)))
        o_ref[...] = r.astype(BF16)

    in_specs = [pl.BlockSpec((tm, n), lambda i, j: (i, 0)), pl.BlockSpec((tk, n), lambda i, j: (j, 0))]
    args = [dy, w]
    if a is not None:
        in_specs.append(pl.BlockSpec((tm, tk), lambda i, j: (i, j)))
        args.append(a)
    return pl.pallas_call(
        body, name=name, grid=(t // tm, k // tk), in_specs=in_specs,
        out_specs=pl.BlockSpec((tm, tk), lambda i, j: (i, j)), out_shape=_sds((t, k), BF16),
        compiler_params=_params(2),
    )(*args)


def _matmul_nt_rmsbwd(name, dy, w, x, g, dres, *, w_rows=False, after=None):
    t, k = dy.shape
    blocked = w.ndim == 3
    d = w.shape[1] if blocked or w_rows else w.shape[0]
    tm = min(512, t)

    def body(dy_ref, w_ref, x_ref, g_ref, r_ref, *rest):
        dx_ref, dg_ref = rest[-2:]
        if blocked:
            kb = w.shape[2]
            dxn = _dot_nt(dy_ref[:, 0:kb].astype(BF16), w_ref[0])
            for j in range(1, w.shape[0]):
                dxn += _dot_nt(dy_ref[:, j * kb:(j + 1) * kb].astype(BF16), w_ref[j])
        elif w_rows:
            dxn = _dot(dy_ref[...].astype(BF16), w_ref[...])
        else:
            dxn = _dot_nt(dy_ref[...].astype(BF16), w_ref[...])
        _, vjp = jax.vjp(_rms, x_ref[...], g_ref[...])
        dx, dg = vjp(dxn)
        dx_ref[...] = r_ref[...] + dx

        @pl.when(pl.program_id(0) == 0)
        def _():
            dg_ref[...] = jnp.zeros_like(dg_ref)

        dg_ref[...] += dg

    row = lambda i: (i, 0)
    fixed = lambda i: (0, 0)
    extra_specs, extra = _after_operand(after)
    return pl.pallas_call(
        body, name=name, grid=(t // tm,),
        in_specs=[pl.BlockSpec((tm, k), row), pl.BlockSpec(w.shape, (lambda i: (0, 0, 0)) if blocked else fixed),
                  pl.BlockSpec((tm, d), row), pl.BlockSpec((1, d), fixed), pl.BlockSpec((tm, d), row)] + extra_specs,
        out_specs=[pl.BlockSpec((tm, d), row), pl.BlockSpec((1, d), fixed)],
        out_shape=[_sds((t, d), F32), _sds((1, d), F32)], compiler_params=_params(1),
    )(dy, w, x, g.reshape(1, d), dres, *extra)


def _matmul_tn(name, a, b, *, tk=1024, tn=1024, col_blocks=None):
    t, k = a.shape
    n = b.shape[1]
    tk = min(tk, k)
    if col_blocks:
        tn = n // col_blocks
        out_spec, out_shape = pl.BlockSpec((None, tk, tn), lambda i, j: (j, i, 0)), _sds((col_blocks, k, tn), BF16)
    else:
        tn = min(tn, n)
        out_spec, out_shape = pl.BlockSpec((tk, tn), lambda i, j: (i, j)), _sds((k, n), BF16)

    def body(a_ref, b_ref, o_ref):
        o_ref[...] = _dot_tn(a_ref[...].astype(BF16), b_ref[...].astype(BF16)).astype(BF16)

    return pl.pallas_call(
        body, name=name, grid=(k // tk, n // tn),
        in_specs=[pl.BlockSpec((t, tk), lambda i, j: (0, i)), pl.BlockSpec((t, tn), lambda i, j: (0, j))],
        out_specs=out_spec, out_shape=out_shape, compiler_params=_params(2),
    )(a, b)


def _join_row_blocks(name, blocks, rows):
    b, r, c = blocks.shape
    tc = min(256, c)

    def body(g_ref, o_ref):
        o_ref[...] = jnp.zeros_like(o_ref)
        for j in range(b):
            o_ref[r * j:r * (j + 1), :] = g_ref[j]

    return pl.pallas_call(
        body, name=name, grid=(c // tc,), in_specs=[pl.BlockSpec((b, r, tc), lambda i: (0, 0, i))],
        out_specs=pl.BlockSpec((rows, tc), lambda i: (0, i)), out_shape=_sds((rows, c), blocks.dtype),
        compiler_params=_params(1),
    )(blocks)


def _split_row_blocks(name, mat, b, r):
    rows, c = mat.shape
    tc = min(256, c)

    def body(w_ref, o_ref):
        for j in range(b):
            o_ref[j] = w_ref[r * j:r * (j + 1), :]

    return pl.pallas_call(
        body, name=name, grid=(c // tc,), in_specs=[pl.BlockSpec((rows, tc), lambda i: (0, i))],
        out_specs=pl.BlockSpec((b, r, tc), lambda i: (0, 0, i)), out_shape=_sds((b, r, c), mat.dtype),
        compiler_params=_params(1),
    )(mat)


def _matmul_res_loss(name, a, w, res, target, *, tn=512):
    t, k = a.shape
    n = w.shape[1]
    tm = min(1024, t)

    def body(a_ref, w_ref, r_ref, t_ref, dy_ref, l_ref):
        e = r_ref[...] + _dot(a_ref[...], w_ref[...]) - t_ref[...]
        dy_ref[...] = e * (1.0 / n)

        @pl.when((pl.program_id(0) == 0) & (pl.program_id(1) == 0))
        def _():
            l_ref[...] = jnp.zeros_like(l_ref)

        l_ref[...] += (0.5 / n) * jnp.sum(e * e)

    tile = pl.BlockSpec((tm, tn), lambda i, j: (i, j))
    return pl.pallas_call(
        body, name=name, grid=(t // tm, n // tn),
        in_specs=[pl.BlockSpec((tm, k), lambda i, j: (i, 0)), pl.BlockSpec((k, tn), lambda i, j: (0, j)), tile, tile],
        out_specs=[tile, pl.BlockSpec((8, LANES), lambda i, j: (0, 0))],
        out_shape=[_sds((t, n), F32), _sds((8, LANES), F32)], compiler_params=_params(2),
    )(a, w, res, target)


def _gate_fwd(name, proj, b_pad, n_heads, gate_col, after=None):
    t = proj.shape[0]
    tb = min(256, t)
    tri = jnp.asarray(np.tril(np.ones((tb, tb), np.float32)))
    extra_specs, extra = _after_operand(after)

    def body(p_ref, b_ref, tri_ref, *rest):
        c_ref, carry = rest[-2:]

        @pl.when(pl.program_id(0) == 0)
        def _():
            carry[...] = jnp.zeros_like(carry)

        lane = lax.broadcasted_iota(jnp.int32, (tb, LANES), 1)
        lf = jnp.where(lane < n_heads, jax.nn.log_sigmoid(p_ref[...] + b_ref[...]), 0.0)
        c = _dot_exact(tri_ref[...], lf) + carry[0:1, :]
        c_ref[...] = c
        carry[...] = jnp.broadcast_to(c[tb - 1:tb, :], carry.shape)

    return pl.pallas_call(
        body, name=name, grid=(t // tb,),
        in_specs=[pl.BlockSpec((tb, LANES), lambda i: (i, gate_col)), pl.BlockSpec((1, LANES), lambda i: (0, 0)),
                  pl.BlockSpec((tb, tb), lambda i: (0, 0))] + extra_specs,
        out_specs=pl.BlockSpec((tb, LANES), lambda i: (i, 0)), out_shape=_sds((t, LANES), F32),
        scratch_shapes=[pltpu.VMEM((8, LANES), F32)], compiler_params=_params(1),
    )(proj, b_pad, tri, *extra)


def _gate_bwd(name, proj, b_pad, dc, n_heads, gate_col):
    t = proj.shape[0]
    tb = min(256, t)
    nb = t // tb
    triu = jnp.asarray(np.triu(np.ones((tb, tb), np.float32)))

    def body(p_ref, b_ref, dc_ref, tri_ref, df_ref, db_ref, carry):
        @pl.when(pl.program_id(0) == 0)
        def _():
            carry[...] = jnp.zeros_like(carry)
            db_ref[...] = jnp.zeros_like(db_ref)

        dcv = dc_ref[...]
        dlf = _dot_exact(tri_ref[...], dcv) + carry[0:1, :]
        carry[...] = jnp.broadcast_to(dlf[0:1, :], carry.shape)
        lane = lax.broadcasted_iota(jnp.int32, (tb, LANES), 1)
        z = p_ref[...] + b_ref[...]
        df = jnp.where(lane < n_heads, dlf / (1.0 + jnp.exp(z)), 0.0)
        df_ref[...] = df.astype(BF16)
        db_ref[...] += jnp.sum(df, axis=0, keepdims=True)

    return pl.pallas_call(
        body, name=name, grid=(nb,),
        in_specs=[pl.BlockSpec((tb, LANES), lambda i: (nb - 1 - i, gate_col)), pl.BlockSpec((1, LANES), lambda i: (0, 0)),
                  pl.BlockSpec((tb, LANES), lambda i: (nb - 1 - i, 0)), pl.BlockSpec((tb, tb), lambda i: (0, 0))],
        out_specs=[pl.BlockSpec((tb, LANES), lambda i: (nb - 1 - i, 0)), pl.BlockSpec((1, LANES), lambda i: (0, 0))],
        out_shape=[_sds((t, LANES), BF16), _sds((1, LANES), F32)],
        scratch_shapes=[pltpu.VMEM((8, LANES), F32)], compiler_params=_params(1),
    )(proj, b_pad, dc, triu)


def _qhead(qp, g):
    return _rms(qp, g) * (HEAD_DIM ** -0.5)


def _column(mat, idx):
    lane = lax.broadcasted_iota(jnp.int32, mat.shape, 1)
    return jnp.sum(jnp.where(lane == idx, mat, 0.0), axis=1, keepdims=True)


def _fox_scores(kk, qi, ckey, cq_i, i, bq):
    length = kk.shape[0]
    s = _dot_nt(kk, qi) + cq_i - ckey[:length]
    key = lax.broadcasted_iota(jnp.int32, (length, bq), 0)
    qry = lax.broadcasted_iota(jnp.int32, (length, bq), 1) + i * bq
    return jnp.where(key <= qry, s, NEG)


def _fox_fwd(name, proj, c, crow, gq, gk, n_heads):
    t = proj.shape[0]
    hw = n_heads * HEAD_DIM
    npair = n_heads // 2
    bq = min(512, t)
    nq = t // bq

    def body(q_ref, k_ref, v_ref, c_ref, crow_ref, gq_ref, gk_ref, o_ref, lse_ref):
        hp = pl.program_id(0)
        lse_ref[...] = jnp.zeros_like(lse_ref)
        outs = []
        for hh in range(2):
            sl = slice(hh * HEAD_DIM, (hh + 1) * HEAD_DIM)
            qn = _qhead(q_ref[:, sl], gq_ref[...]).astype(BF16)
            kn = _rms(k_ref[:, sl], gk_ref[...]).astype(BF16)
            v_t = v_ref[:, sl].T.astype(BF16)
            ckey = _column(c_ref[...], 2 * hp + hh)
            cq = crow_ref[0, hh:hh + 1, :]
            o_blocks = []
            for i in range(nq):
                cols = slice(i * bq, (i + 1) * bq)
                length = (i + 1) * bq
                s = _fox_scores(kn[:length], qn[cols], ckey, cq[:, cols], i, bq)
                m = jnp.max(s, axis=0, keepdims=True)
                p = jnp.exp(s - m)
                l = jnp.sum(p, axis=0, keepdims=True)
                o_blocks.append((_dot(v_t[:, :length], p.astype(BF16)) / l).T)
                lse_ref[0, hh:hh + 1, cols] = m + jnp.log(l)
            outs.append(jnp.concatenate(o_blocks, axis=0))
        o_ref[...] = jnp.concatenate(outs, axis=1).astype(BF16)

    col = lambda off: (lambda h: (0, off + h))
    fixed = lambda h: (0, 0)
    return pl.pallas_call(
        body, name=name, grid=(npair,),
        in_specs=[pl.BlockSpec((t, LANES), col(0)), pl.BlockSpec((t, LANES), col(npair)), pl.BlockSpec((t, LANES), col(2 * npair)),
                  pl.BlockSpec((t, LANES), fixed), pl.BlockSpec((1, 2, t), lambda h: (h, 0, 0)),
                  pl.BlockSpec((1, HEAD_DIM), fixed), pl.BlockSpec((1, HEAD_DIM), fixed)],
        out_specs=[pl.BlockSpec((t, LANES), col(0)), pl.BlockSpec((1, 8, t), lambda h: (h, 0, 0))],
        out_shape=[_sds((t, hw), BF16), _sds((npair, 8, t), F32)], compiler_params=_params(1),
    )(proj, proj, proj, c, crow, gq, gk)


def _fox_bwd(name, proj, c, crow, gq, gk, lse, do, n_heads, after=None):
    t = proj.shape[0]
    hw = n_heads * HEAD_DIM
    npair = n_heads // 2
    bq = min(256, t)
    nq = t // bq

    def body(q_ref, k_ref, v_ref, c_ref, crow_ref, gq_ref, gk_ref, lse_ref, do_ref, *rest):
        dq_ref, dk_ref, dv_ref, dc_ref, dgq_ref, dgk_ref, dk_acc, dv_acc, dc_acc = rest[-9:]
        hp = pl.program_id(0)

        @pl.when(hp == 0)
        def _():
            dgq_ref[...] = jnp.zeros_like(dgq_ref)
            dgk_ref[...] = jnp.zeros_like(dgk_ref)
            dc_ref[...] = jnp.zeros_like(dc_ref)

        lane = lax.broadcasted_iota(jnp.int32, (t, LANES), 1)
        dqs, dks, dvs = [], [], []
        for hh in range(2):
            sl = slice(hh * HEAD_DIM, (hh + 1) * HEAD_DIM)
            qf, q_vjp = jax.vjp(_qhead, q_ref[:, sl], gq_ref[...])
            kf, k_vjp = jax.vjp(_rms, k_ref[:, sl], gk_ref[...])
            qn, kn, kn_t = qf.astype(BF16), kf.astype(BF16), kf.T.astype(BF16)
            vb = v_ref[:, sl].astype(BF16)
            dob = do_ref[:, sl]
            ckey = _column(c_ref[...], 2 * hp + hh)
            cq = crow_ref[0, hh:hh + 1, :]
            lse_h = lse_ref[0, hh:hh + 1, :]
            dk_acc[...] = jnp.zeros_like(dk_acc)
            dv_acc[...] = jnp.zeros_like(dv_acc)
            dc_acc[...] = jnp.zeros_like(dc_acc)
            dq_blocks = []
            for i in range(nq):
                cols = slice(i * bq, (i + 1) * bq)
                length = (i + 1) * bq
                qi, doi = qn[cols], dob[cols]
                s = _fox_scores(kn[:length], qi, ckey, cq[:, cols], i, bq)
                p = jnp.exp(s - lse_h[:, cols])
                dp = _dot_nt(vb[:length], doi)
                ds = p * (dp - jnp.sum(p * dp, axis=0, keepdims=True))
                dsb = ds.astype(BF16)
                dq_blocks.append(_dot(kn_t[:, :length], dsb).T)
                dk_acc[0:length, :] += _dot(dsb, qi)
                dv_acc[0:length, :] += _dot(p.astype(BF16), doi)
                part = ds[:, 0:LANES]
                for j in range(1, bq // LANES):
                    part = part + ds[:, j * LANES:(j + 1) * LANES]
                dc_acc[0:length, :] += part
            dqp, dgq = q_vjp(jnp.concatenate(dq_blocks, axis=0))
            dkp, dgk = k_vjp(dk_acc[...])
            dgq_ref[...] += dgq
            dgk_ref[...] += dgk
            dqs.append(dqp)
            dks.append(dkp)
            dvs.append(dv_acc[...])
            dc_ref[...] = jnp.where(lane == 2 * hp + hh, -jnp.sum(dc_acc[...], axis=1, keepdims=True), dc_ref[...])
        dq_ref[...] = jnp.concatenate(dqs, axis=1).astype(BF16)
        dk_ref[...] = jnp.concatenate(dks, axis=1).astype(BF16)
        dv_ref[...] = jnp.concatenate(dvs, axis=1).astype(BF16)

    col = lambda off: (lambda h: (0, off + h))
    fixed = lambda h: (0, 0)
    pair_blk = pl.BlockSpec((t, LANES), col(0))
    extra_specs, extra = _after_operand(after)
    return pl.pallas_call(
        body, name=name, grid=(npair,),
        in_specs=[pl.BlockSpec((t, LANES), col(0)), pl.BlockSpec((t, LANES), col(npair)), pl.BlockSpec((t, LANES), col(2 * npair)),
                  pl.BlockSpec((t, LANES), fixed), pl.BlockSpec((1, 2, t), lambda h: (h, 0, 0)),
                  pl.BlockSpec((1, HEAD_DIM), fixed), pl.BlockSpec((1, HEAD_DIM), fixed),
                  pl.BlockSpec((1, 8, t), lambda h: (h, 0, 0)), pair_blk] + extra_specs,
        out_specs=[pair_blk, pair_blk, pair_blk, pl.BlockSpec((t, LANES), fixed),
                   pl.BlockSpec((1, HEAD_DIM), fixed), pl.BlockSpec((1, HEAD_DIM), fixed)],
        out_shape=[_sds((t, hw), BF16), _sds((t, hw), BF16), _sds((t, hw), BF16), _sds((t, LANES), F32),
                   _sds((1, HEAD_DIM), F32), _sds((1, HEAD_DIM), F32)],
        scratch_shapes=[pltpu.VMEM((t, HEAD_DIM), F32), pltpu.VMEM((t, HEAD_DIM), F32), pltpu.VMEM((t, LANES), F32)],
        compiler_params=_params(1),
    )(proj, proj, proj, c, crow, gq, gk, lse, do, *extra)


def _t5_bucket_table():
    dist = np.arange(WINDOW)[None, :] + WINDOW - np.arange(2 * WINDOW)[:, None]
    n = np.maximum(dist, 0)
    max_exact = N_BUCKETS // 2
    large = max_exact + (np.log(np.maximum(n, 1) / max_exact) / np.log(REL_MAX_DIST / max_exact)
                         * (N_BUCKETS - max_exact)).astype(np.int32)
    large = np.minimum(large, N_BUCKETS - 1)
    return np.where(n < max_exact, n, large).astype(np.int32).reshape(1, -1)


def _bias_expand(name, rel_bias_t):
    n_heads = rel_bias_t.shape[0]
    tbl = jnp.asarray(_t5_bucket_table())
    width = tbl.shape[1]

    def body(rb_ref, tbl_ref, o_ref):
        onehot = (lax.broadcasted_iota(jnp.int32, (N_BUCKETS, width), 0) == tbl_ref[...]).astype(F32)
        o_ref[...] = _dot_exact(rb_ref[...], onehot)

    return pl.pallas_call(body, name=name, out_shape=_sds((n_heads, width), F32), compiler_params=_params(0))(rel_bias_t, tbl)


def _bias_reduce(name, dbias):
    n_heads, width = dbias.shape
    tbl = jnp.asarray(_t5_bucket_table())

    def body(db_ref, tbl_ref, o_ref):
        onehot = (lax.broadcasted_iota(jnp.int32, (N_BUCKETS, width), 0) == tbl_ref[...]).astype(F32)
        o_ref[...] = lax.dot_general(db_ref[...], onehot, (((1,), (1,)), ((), ())), preferred_element_type=F32,
                                     precision=lax.Precision.HIGHEST)

    return pl.pallas_call(body, name=name, out_shape=_sds((n_heads, N_BUCKETS), F32), compiler_params=_params(0))(dbias, tbl)


def _swa_mask(n, group):
    j = lax.broadcasted_iota(jnp.int32, (2 * WINDOW, group * WINDOW), 0)
    i = lax.broadcasted_iota(jnp.int32, (2 * WINDOW, group * WINDOW), 1) & (WINDOW - 1)
    ok = (j > i) & (j <= i + WINDOW) & ((n > 0) | (j >= WINDOW))
    return jnp.where(ok, 0.0, NEG)


def _swa_stack(ref, start, group):
    return jnp.concatenate([ref[pl.ds(start, WINDOW), g * HEAD_DIM:(g + 1) * HEAD_DIM] for g in range(group)], axis=0)


def _kv_head(ref, n_kv):
    out = ref[:, 0:HEAD_DIM]
    for h in range(1, n_kv):
        out = jnp.where(pl.program_id(0) == h, ref[:, h * HEAD_DIM:(h + 1) * HEAD_DIM], out)
    return out


def _swa_fwd(name, qb, kv, gq, gk, sinks, bias, group):
    t = qb.shape[0]
    kvh = kv.shape[1] // (2 * HEAD_DIM)
    nblk = t // WINDOW
    gw = group * HEAD_DIM
    band = 2 * WINDOW
    cols = group * WINDOW

    def body(q_ref, k_ref, v_ref, gq_ref, gk_ref, sink_ref, bias_ref, o_ref, lse_ref, qs, kpad, vpad):
        for g in range(group):
            qs[:, g * HEAD_DIM:(g + 1) * HEAD_DIM] = _qhead(q_ref[:, g * HEAD_DIM:(g + 1) * HEAD_DIM], gq_ref[...]).astype(BF16)
        kpad[0:WINDOW, :] = jnp.zeros((WINDOW, HEAD_DIM), BF16)
        vpad[0:WINDOW, :] = jnp.zeros((WINDOW, HEAD_DIM), BF16)
        kpad[WINDOW:, :] = _rms(_kv_head(k_ref, kvh), gk_ref[...]).astype(BF16)
        vpad[WINDOW:, :] = _kv_head(v_ref, kvh).astype(BF16)
        sink = sink_ref[0]

        def block(n, carry):
            start = pl.multiple_of(n * WINDOW, WINDOW)
            kb = kpad[pl.ds(start, band), :]
            vb = vpad[pl.ds(start, band), :]
            s = _dot_nt(kb, _swa_stack(qs, start, group)) + bias_ref[0] + _swa_mask(n, group)
            m = jnp.maximum(jnp.max(s, axis=0, keepdims=True), sink)
            e = jnp.exp(s - m)
            l = jnp.sum(e, axis=0, keepdims=True) + jnp.exp(sink - m)
            o_t = _dot_tn(vb, e.astype(BF16)) / l
            for g in range(group):
                o_ref[pl.ds(start, WINDOW), g * HEAD_DIM:(g + 1) * HEAD_DIM] = o_t[:, g * WINDOW:(g + 1) * WINDOW].T.astype(BF16)
            lse_ref[pl.ds(n, 1), :] = m + jnp.log(l)
            return carry

        lax.fori_loop(0, nblk, block, 0)

    fixed = lambda h: (0, 0)
    per = lambda h: (h, 0, 0)
    return pl.pallas_call(
        body, name=name, grid=(kvh,),
        in_specs=[pl.BlockSpec((t, gw), lambda h: (0, h)), pl.BlockSpec((t, kvh * HEAD_DIM), lambda h: (0, 0)),
                  pl.BlockSpec((t, kvh * HEAD_DIM), lambda h: (0, 1)),
                  pl.BlockSpec((1, HEAD_DIM), fixed), pl.BlockSpec((1, HEAD_DIM), fixed),
                  pl.BlockSpec((1, 1, cols), per), pl.BlockSpec((1, band, cols), per)],
        out_specs=[pl.BlockSpec((t, gw), lambda h: (0, h)), pl.BlockSpec((nblk, cols), lambda h: (h, 0))],
        out_shape=[_sds((t, kvh * gw), BF16), _sds((kvh * nblk, cols), F32)],
        scratch_shapes=[pltpu.VMEM((t, gw), BF16), pltpu.VMEM((t + WINDOW, HEAD_DIM), BF16),
                        pltpu.VMEM((t + WINDOW, HEAD_DIM), BF16)],
        compiler_params=_params(1),
    )(qb, kv, kv, gq, gk, sinks, bias)


def _swa_bwd(name, qb, kv, gq, gk, sinks, bias, lse, do, group):
    t = qb.shape[0]
    kvh = kv.shape[1] // (2 * HEAD_DIM)
    kvw = kvh * HEAD_DIM
    nblk = t // WINDOW
    gw = group * HEAD_DIM
    band = 2 * WINDOW
    cols = group * WINDOW

    def body(q_ref, k_ref, v_ref, gq_ref, gk_ref, sink_ref, bias_ref, lse_ref, do_ref,
             dq_ref, dkv_ref, dgq_ref, dgk_ref, dsink_ref, dbias_ref,
             qs, kpad, vpad, dqs, dk_acc, dv_acc, dsink_acc):
        @pl.when(pl.program_id(0) == 0)
        def _():
            dgq_ref[...] = jnp.zeros_like(dgq_ref)
            dgk_ref[...] = jnp.zeros_like(dgk_ref)
            dkv_ref[...] = jnp.zeros_like(dkv_ref)

        for g in range(group):
            qs[:, g * HEAD_DIM:(g + 1) * HEAD_DIM] = _qhead(q_ref[:, g * HEAD_DIM:(g + 1) * HEAD_DIM], gq_ref[...]).astype(BF16)
        kpad[0:WINDOW, :] = jnp.zeros((WINDOW, HEAD_DIM), BF16)
        vpad[0:WINDOW, :] = jnp.zeros((WINDOW, HEAD_DIM), BF16)
        kpad[WINDOW:, :] = _rms(_kv_head(k_ref, kvh), gk_ref[...]).astype(BF16)
        vpad[WINDOW:, :] = _kv_head(v_ref, kvh).astype(BF16)
        dk_acc[...] = jnp.zeros_like(dk_acc)
        dv_acc[...] = jnp.zeros_like(dv_acc)
        dsink_acc[...] = jnp.zeros_like(dsink_acc)
        dbias_ref[...] = jnp.zeros_like(dbias_ref)
        sink = sink_ref[0]

        def block(n, carry):
            start = pl.multiple_of(n * WINDOW, WINDOW)
            kb = kpad[pl.ds(start, band), :]
            vb = vpad[pl.ds(start, band), :]
            q = _swa_stack(qs, start, group)
            dob = _swa_stack(do_ref, start, group)
            lse_n = lse_ref[pl.ds(n, 1), :]
            s = _dot_nt(kb, q) + bias_ref[0] + _swa_mask(n, group)
            p = jnp.exp(s - lse_n)
            dp = _dot_nt(vb, dob)
            dsum = jnp.sum(p * dp, axis=0, keepdims=True)
            ds = p * (dp - dsum)
            dsb = ds.astype(BF16)
            dsink_acc[...] -= jnp.exp(sink - lse_n) * dsum
            dbias_ref[0] += ds
            dq = _dot_tn(dsb, kb)
            for g in range(group):
                dqs[pl.ds(start, WINDOW), g * HEAD_DIM:(g + 1) * HEAD_DIM] = dq[g * WINDOW:(g + 1) * WINDOW]
            dk_acc[pl.ds(start, band), :] += _dot(dsb, q)
            dv_acc[pl.ds(start, band), :] += _dot(p.astype(BF16), dob)
            return carry

        lax.fori_loop(0, nblk, block, 0)
        for g in range(group):
            _, q_vjp = jax.vjp(_qhead, q_ref[:, g * HEAD_DIM:(g + 1) * HEAD_DIM], gq_ref[...])
            dqp, dgq = q_vjp(dqs[:, g * HEAD_DIM:(g + 1) * HEAD_DIM])
            dq_ref[:, g * HEAD_DIM:(g + 1) * HEAD_DIM] = dqp.astype(BF16)
            dgq_ref[...] += dgq
            dsink_g = jnp.sum(dsink_acc[:, g * WINDOW:(g + 1) * WINDOW], axis=1, keepdims=True)
            dsink_ref[0, g:g + 1, :] = jnp.broadcast_to(dsink_g, (1, LANES))
        _, k_vjp = jax.vjp(_rms, _kv_head(k_ref, kvh), gk_ref[...])
        dkp, dgk = k_vjp(dk_acc[WINDOW:, :])
        dgk_ref[...] += dgk
        mine = lax.broadcasted_iota(jnp.int32, (t, kvw), 1) // HEAD_DIM == pl.program_id(0)
        dkv_ref[:, 0:kvw] = jnp.where(mine, jnp.concatenate([dkp] * kvh, axis=1), dkv_ref[:, 0:kvw])
        dkv_ref[:, kvw:] = jnp.where(mine, jnp.concatenate([dv_acc[WINDOW:, :]] * kvh, axis=1), dkv_ref[:, kvw:])

    fixed = lambda h: (0, 0)
    per = lambda h: (h, 0, 0)
    wide = pl.BlockSpec((t, gw), lambda h: (0, h))
    vec = pl.BlockSpec((1, HEAD_DIM), fixed)
    bias_spec = pl.BlockSpec((1, band, cols), per)
    return pl.pallas_call(
        body, name=name, grid=(kvh,),
        in_specs=[wide, pl.BlockSpec((t, kvw), lambda h: (0, 0)), pl.BlockSpec((t, kvw), lambda h: (0, 1)), vec, vec,
                  pl.BlockSpec((1, 1, cols), per), bias_spec, pl.BlockSpec((nblk, cols), lambda h: (h, 0)), wide],
        out_specs=[wide, pl.BlockSpec((t, 2 * kvw), fixed), vec, vec, pl.BlockSpec((1, group, LANES), per), bias_spec],
        out_shape=[_sds((t, kvh * gw), BF16), _sds((t, 2 * kvw), F32),
                   _sds((1, HEAD_DIM), F32), _sds((1, HEAD_DIM), F32),
                   _sds((kvh, group, LANES), F32), _sds((kvh, band, cols), F32)],
        scratch_shapes=[pltpu.VMEM((t, gw), BF16), pltpu.VMEM((t + WINDOW, HEAD_DIM), BF16),
                        pltpu.VMEM((t + WINDOW, HEAD_DIM), BF16), pltpu.VMEM((t, gw), F32),
                        pltpu.VMEM((t + WINDOW, HEAD_DIM), F32), pltpu.VMEM((t + WINDOW, HEAD_DIM), F32),
                        pltpu.VMEM((1, cols), F32)],
        compiler_params=_params(1),
    )(qb, kv, kv, gq, gk, sinks, bias, lse, do)


def _local_step(x, target, p, comm):
    t, d = x.shape
    n_heads = d // HEAD_DIM
    kv_heads = n_heads // 8
    group = n_heads // kv_heads
    hw = n_heads * HEAD_DIM
    gate_col = 3 * hw // LANES
    kvw = kv_heads * HEAD_DIM
    grads = {}

    def mlp_fwd(tag, h, g, layer, last=False):
        w_up, = comm.weights([f"w_up{layer}"], h)
        a, hn = _norm_matmul(f"{tag}_up", h, g, w_up, relu2=True)
        w_down, = comm.weights([f"w_down{layer}"], a)
        out = _matmul_res_loss(f"{tag}_down", a, w_down, h, target) if last else _matmul_res(f"{tag}_down", a, w_down, h)
        return out, (h, g, hn, a, w_up, w_down)

    def mlp_bwd(tag, saved, layer, dy):
        h, g, hn, a, w_up, w_down = saved
        du = _matmul_nt(f"{tag}_du", dy, w_down, a=a)
        dw_down = _matmul_tn(f"{tag}_dwdown", a, dy)
        dw_up = _matmul_tn(f"{tag}_dwup", hn, du, col_blocks=w_up.shape[0])
        sent = comm.send_grads(tag, {f"w_down{layer}": dw_down, f"w_up{layer}": dw_up})
        return _matmul_nt_rmsbwd(f"{tag}_dh", du, w_up, h, g, dy, after=sent)

    bias = _bias_expand("b_bias", p["rel_bias"].T).reshape(kv_heads, group, 2 * WINDOW, WINDOW)
    bias = bias.transpose(0, 2, 1, 3).reshape(kv_heads, 2 * WINDOW, group * WINDOW)
    comm.prefetch(["w_in_a"], bias)
    w_in, = comm.weights(["w_in_a"], None)
    proj, xn1 = _norm_matmul("a_inproj", x, p["g_attn"][0], w_in, tn=640, w_rows=True)
    ahead = comm.prefetch(["w_out_a"], proj)
    b_pad = jnp.pad(p["b_f"], ((0, 0), (0, LANES - n_heads)))
    c = _gate_fwd("a_gate", proj, b_pad, n_heads, gate_col, after=ahead)
    crow = c[:, :n_heads].T.reshape(n_heads // 2, 2, t)
    o_a, lse_a = _fox_fwd("a_attn", proj, c, crow, p["gq_a"], p["gk_a"], n_heads)
    ahead = comm.prefetch(["w_up0", "w_down0", "w_kv", "w_q_b", "w_out_b"], o_a)
    w_out_a, = comm.weights(["w_out_a"], o_a)
    h1 = _matmul_res("a_outproj", o_a, w_out_a, x, after=ahead)
    h2, mlp0 = mlp_fwd("mlp0", h1, p["g_mlp"][0], 0)

    ahead = comm.prefetch(["w_up1", "w_down1"], h2)
    w_kv, w_q_b = comm.weights(["w_kv", "w_q_b"], h2)
    kv, hn_kv = _norm_matmul("kv_proj", h2, p["g_kv"], w_kv, tn=2 * kvw, after=ahead)
    qb, hn_q = _norm_matmul("b_qproj", h2, p["g_attn"][1], w_q_b, tn=512)
    gqb, gkb = p["gq_b"], p["gk_b"].reshape(1, HEAD_DIM)
    sink_rows = jnp.broadcast_to(p["sinks"].reshape(kv_heads, 1, group, 1), (kv_heads, 1, group, WINDOW)).reshape(kv_heads, 1, group * WINDOW)
    o_b, lse_b = _swa_fwd("b_attn", qb, kv, gqb, gkb, sink_rows, bias, group)
    w_out_b, = comm.weights(["w_out_b"], o_b)
    h3 = _matmul_res("b_outproj", o_b, w_out_b, h2)
    (dy, loss_tile), mlp1 = mlp_fwd("mlp1", h3, p["g_mlp"][1], 1, last=True)

    dh3, dg_mlp1 = mlp_bwd("mlp1", mlp1, 1, dy)
    do_b = _matmul_nt("b_do", dh3, w_out_b)
    dw_out_b = _matmul_tn("b_dwout", o_b, dh3)
    dqb, dkv, grads["gq_b"], dgk_b, dsink, dbias = _swa_bwd(
        "b_attn_bwd", qb, kv, gqb, gkb, sink_rows, bias, lse_b, do_b, group)
    grads["gk_b"] = dgk_b
    grads["sinks"] = dsink[:, :, 0].reshape(1, n_heads)
    dbias = dbias.reshape(kv_heads, 2 * WINDOW, group, WINDOW).transpose(0, 2, 1, 3)
    grads["rel_bias"] = _bias_reduce("b_dbias", dbias.reshape(n_heads, WINDOW * 2 * WINDOW)).T
    dw_q_b = _matmul_tn("b_dwq", hn_q, dqb)
    dh2, dg_attn1 = _matmul_nt_rmsbwd("b_dhq", dqb, w_q_b, h2, p["g_attn"][1], dh3)
    dw_kv = _matmul_tn("kv_dw", hn_kv, dkv)
    sent = comm.send_grads("attn_b", {"w_out_b": dw_out_b, "w_q_b": dw_q_b, "w_kv": dw_kv})
    dh2, dg_kv = _matmul_nt_rmsbwd("kv_dh", dkv, w_kv, h2, p["g_kv"], dh2, after=sent)
    grads["g_kv"] = dg_kv
    dh1, dg_mlp0 = mlp_bwd("mlp0", mlp0, 0, dh2)
    grads["g_mlp"] = (dg_mlp0, dg_mlp1)

    do_a = _matmul_nt("a_do", dh1, w_out_a)
    dw_out_a = _matmul_tn("a_dwout", o_a, dh1)
    sent = comm.send_grads("attn_a_out", {"w_out_a": dw_out_a})
    dq, dk, dv, dc, grads["gq_a"], grads["gk_a"] = _fox_bwd(
        "a_attn_bwd", proj, c, crow, p["gq_a"], p["gk_a"], lse_a, do_a, n_heads, after=sent)
    dfl, db_f = _gate_bwd("a_gate_bwd", proj, b_pad, dc, n_heads, gate_col)
    grads["b_f"] = db_f
    dproj = jnp.concatenate([dq, dk, dv, dfl], axis=1)
    dw_in = _matmul_tn("a_dwin", dproj, xn1, tk=640)
    sent = comm.send_grads("attn_a_in", {"w_in_a": dw_in})
    grad_x, dg_attn0 = _matmul_nt_rmsbwd("a_dx", dproj, w_in, x, p["g_attn"][0], dh1, w_rows=True, after=sent)
    grads["g_attn"] = (dg_attn0, dg_attn1)
    return loss_tile, grad_x, grads


EVERYONE = (1, 2, 3, 4, 5, 6, 7)
SAME_CORE = (1, 2, 4, 6)
OTHER_CHIPS = (2, 4, 6)
RELAY_COLLECTIVE_ID = 0


class _InFlight:
    def __init__(self, scatter, ks, send_sems, recv_sems, srcs, lands, token):
        self.scatter, self.ks, self.send_sems, self.recv_sems = scatter, ks, send_sems, recv_sems
        self.srcs, self.lands, self.token = list(srcs), list(lands), token


def _mesh_peers(ks=EVERYONE):
    x, y, c = lax.axis_index("x"), lax.axis_index("y"), lax.axis_index("c")
    peers = []
    for k in ks:
        px, py, pc = x ^ ((k >> 2) & 1), y ^ ((k >> 1) & 1), c ^ (k & 1)
        peers.append(((px, py, pc), 4 * px + 2 * py + pc))
    return 4 * x + 2 * y + c, peers


_HBM_SPEC = pl.BlockSpec(memory_space=pltpu.HBM)
_SEM_SPEC = pl.BlockSpec(memory_space=pltpu.SEMAPHORE)
_SIDE_EFFECT = pltpu.SideEffectType.DATAFLOW_SIDE_EFFECTING


def _exchange_start(name, arrays, scatter, ks=EVERYONE):
    n = len(arrays)
    me, _ = _mesh_peers()
    lands = []
    for a in arrays:
        own = lax.dynamic_index_in_dim(a, me, 0, keepdims=False) if scatter else a
        shape = a.shape if scatter else (N_DEV,) + a.shape
        lands.append(lax.dynamic_update_index_in_dim(lax.empty(shape, a.dtype), own, me, 0))

    def body(*refs):
        src, land = refs[:n], refs[n:2 * n]
        send_sems, recv_sems, token = refs[2 * n], refs[2 * n + 1], refs[-1]
        pos, peers = _mesh_peers(ks)
        for a in range(n):
            for k, (peer, peer_pos) in enumerate(peers):
                pltpu.make_async_remote_copy(
                    src_ref=src[a].at[peer_pos] if scatter else src[a], dst_ref=land[a].at[pos],
                    send_sem=send_sems.at[a * len(ks) + k], recv_sem=recv_sems.at[a * len(ks) + k],
                    device_id=peer, device_id_type=pl.DeviceIdType.MESH).start()
        token[...] = jnp.zeros_like(token)

    operands = [pltpu.with_memory_space_constraint(a, pltpu.HBM) for a in list(arrays) + lands]
    outs = pl.pallas_call(
        body, name=name,
        out_shape=(pltpu.SemaphoreType.DMA((n * len(ks),)), pltpu.SemaphoreType.DMA((n * len(ks),)),
                   *[pltpu.HBM(a.shape, a.dtype) for a in operands], _sds((8, LANES), F32)),
        in_specs=[_HBM_SPEC] * (2 * n),
        out_specs=(_SEM_SPEC, _SEM_SPEC, *[_HBM_SPEC] * (2 * n), pl.BlockSpec(memory_space=pltpu.VMEM)),
        input_output_aliases={i: 2 + i for i in range(2 * n)},
        compiler_params=pltpu.CompilerParams(has_side_effects=_SIDE_EFFECT),
    )(*operands)
    return _InFlight(scatter, ks, outs[0], outs[1], outs[2:2 + n], outs[2 + n:2 + 2 * n], outs[-1])


def _exchange_wait(name, flight, which, after):
    m = len(which)
    scatter, ks = flight.scatter, flight.ks

    def body(*refs):
        src, land = refs[:m], refs[m:2 * m]
        send_sems, recv_sems = refs[2 * m], refs[2 * m + 1]
        _, peers = _mesh_peers(ks)
        for i, a in enumerate(which):
            for k, (peer, peer_pos) in enumerate(peers):
                cp = pltpu.make_async_remote_copy(
                    src_ref=src[i].at[peer_pos] if scatter else src[i], dst_ref=land[i].at[peer_pos],
                    send_sem=send_sems.at[a * len(ks) + k], recv_sem=recv_sems.at[a * len(ks) + k],
                    device_id=peer, device_id_type=pl.DeviceIdType.MESH)
                cp.wait_send()
                cp.wait_recv()

    operands = [flight.srcs[a] for a in which] + [flight.lands[a] for a in which]
    outs = pl.pallas_call(
        body, name=name, out_shape=tuple(pltpu.HBM(a.shape, a.dtype) for a in operands),
        in_specs=[_HBM_SPEC] * (2 * m) + [_SEM_SPEC, _SEM_SPEC, pl.BlockSpec(memory_space=pl.ANY)],
        out_specs=tuple([_HBM_SPEC] * (2 * m)), input_output_aliases={i: i for i in range(2 * m)},
        compiler_params=pltpu.CompilerParams(has_side_effects=_SIDE_EFFECT),
    )(*operands, flight.send_sems, flight.recv_sems, after)
    return list(outs[m:])


def _relay_start(name, lands):
    n = len(lands)

    def body(*refs):
        land, send_sems, recv_sems, token = refs[:n], refs[n], refs[n + 1], refs[-1]
        _, peers = _mesh_peers(OTHER_CHIPS)
        sibling = (lax.axis_index("x"), lax.axis_index("y"), 1 - lax.axis_index("c"))
        barrier = pltpu.get_barrier_semaphore()
        pl.semaphore_signal(barrier, inc=1, device_id=sibling, device_id_type=pl.DeviceIdType.MESH)
        pl.semaphore_wait(barrier, 1)
        for a in range(n):
            for k, (_, peer_pos) in enumerate(peers):
                pltpu.make_async_remote_copy(
                    src_ref=land[a].at[peer_pos], dst_ref=land[a].at[peer_pos],
                    send_sem=send_sems.at[a * len(peers) + k], recv_sem=recv_sems.at[a * len(peers) + k],
                    device_id=sibling, device_id_type=pl.DeviceIdType.MESH).start()
        token[...] = jnp.zeros_like(token)

    count = n * len(OTHER_CHIPS)
    outs = pl.pallas_call(
        body, name=name,
        out_shape=(pltpu.SemaphoreType.DMA((count,)), pltpu.SemaphoreType.DMA((count,)),
                   *[pltpu.HBM(a.shape, a.dtype) for a in lands], _sds((8, LANES), F32)),
        in_specs=[_HBM_SPEC] * n,
        out_specs=(_SEM_SPEC, _SEM_SPEC, *[_HBM_SPEC] * n, pl.BlockSpec(memory_space=pltpu.VMEM)),
        input_output_aliases={i: 2 + i for i in range(n)},
        compiler_params=pltpu.CompilerParams(has_side_effects=_SIDE_EFFECT, collective_id=RELAY_COLLECTIVE_ID),
    )(*[pltpu.with_memory_space_constraint(a, pltpu.HBM) for a in lands])
    return _InFlight(False, OTHER_CHIPS, outs[0], outs[1], [], outs[2:2 + n], outs[-1])


def _relay_wait(name, flight, which, after):
    m = len(which)

    def body(*refs):
        land, send_sems, recv_sems = refs[:m], refs[m], refs[m + 1]
        _, peers = _mesh_peers(OTHER_CHIPS)
        sibling = (lax.axis_index("x"), lax.axis_index("y"), 1 - lax.axis_index("c"))
        for i, a in enumerate(which):
            for k, (_, peer_pos) in enumerate(peers):
                cp = pltpu.make_async_remote_copy(
                    src_ref=land[i].at[peer_pos], dst_ref=land[i].at[peer_pos ^ 1],
                    send_sem=send_sems.at[a * len(peers) + k], recv_sem=recv_sems.at[a * len(peers) + k],
                    device_id=sibling, device_id_type=pl.DeviceIdType.MESH)
                cp.wait_send()
                cp.wait_recv()

    operands = [flight.lands[a] for a in which]
    outs = pl.pallas_call(
        body, name=name, out_shape=tuple(pltpu.HBM(a.shape, a.dtype) for a in operands),
        in_specs=[_HBM_SPEC] * m + [_SEM_SPEC, _SEM_SPEC, pl.BlockSpec(memory_space=pl.ANY)],
        out_specs=tuple([_HBM_SPEC] * m), input_output_aliases={i: i for i in range(m)},
        compiler_params=pltpu.CompilerParams(has_side_effects=_SIDE_EFFECT),
    )(*operands, flight.send_sems, flight.recv_sems, after)
    return list(outs)


def _sum_parts(p_ref):
    g = p_ref[0].astype(F32)
    for dev in range(1, N_DEV):
        g = g + p_ref[dev].astype(F32)
    return g


def _adam_update(g, w, m, v):
    m_new = ADAM_B1 * m + (1.0 - ADAM_B1) * g
    v_new = ADAM_B2 * v + (1.0 - ADAM_B2) * jnp.square(g)
    m_hat = m_new / (1.0 - ADAM_B1 ** ADAM_STEP)
    v_hat = v_new / (1.0 - ADAM_B2 ** ADAM_STEP)
    return -ADAM_LR * (m_hat / (jnp.sqrt(v_hat) + ADAM_EPS) + ADAM_WD * w), m_new, v_new


def _adamw(name, parts, w, m, v, layer=None, into=None):
    r, c = w.shape[-2:]
    tr = 256 if r % 256 == 0 else r
    n_into = 0 if into is None else len(into)

    def body(p_ref, w_ref, m_ref, v_ref, *refs):
        g_ref, d_ref, mo_ref, vo_ref = refs[n_into:]
        g = _sum_parts(p_ref)
        g_ref[...] = g
        d_ref[...], mo_ref[...], vo_ref[...] = _adam_update(g, w_ref[...], m_ref[...], v_ref[...])

    if layer is None:
        blk = pl.BlockSpec((tr, c), lambda i: (i, 0))
    else:
        blk = pl.BlockSpec((None, tr, c), lambda i: (layer, i, 0))
    return pl.pallas_call(
        body, name=name, grid=(r // tr,),
        in_specs=[pl.BlockSpec((N_DEV, tr, c), lambda i: (0, i, 0)), blk, blk, blk] + [pl.BlockSpec(memory_space=pl.ANY)] * n_into,
        out_specs=[blk] * 4, out_shape=[_sds(w.shape, F32)] * 4,
        input_output_aliases={4 + i: i for i in range(n_into)}, compiler_params=_params(1),
    )(parts, w, m, v, *(into or ()))


SMALL_PACK_ROWS = 16
LOSS_ROW = 11


def _small_rows(grads, loss_tile):
    return [(0, 1, grads["g_attn"][0]), (1, 1, grads["g_attn"][1]), (2, 1, grads["g_mlp"][0]), (3, 1, grads["g_mlp"][1]),
            (4, 1, grads["g_kv"]), (5, 1, grads["b_f"]), (6, 1, grads["gq_a"]), (7, 1, grads["gk_a"]), (8, 1, grads["gk_b"]),
            (9, 1, grads["gq_b"]), (10, 1, grads["sinks"]), (LOSS_ROW, 1, loss_tile)]


SMALL_ROWS = {"g_attn": (0, 2), "g_mlp": (2, 2), "g_kv": (4, 1), "b_f": (5, 1), "gq_a": (6, 1), "gk_a": (7, 1),
              "gk_b": (8, 1), "gq_b": (9, 1), "sinks": (10, 1)}


def _pack_small(name, pieces, d):
    def body(*refs):
        out = refs[-1]
        out[...] = jnp.zeros_like(out)
        for (row, rows, _), ref in zip(pieces, refs[:-1]):
            out[row:row + rows, 0:ref.shape[1]] = ref[0:rows, :]

    return pl.pallas_call(body, name=name, out_shape=_sds((SMALL_PACK_ROWS, d), F32), compiler_params=_params(0))(
        *[piece for _, _, piece in pieces])


def _adamw_small(name, parts, parts_rel_bias, w, m, v):
    def body(*refs):
        ins, outs = refs[2:2 + 3 * len(SMALL)], refs[2 + 3 * len(SMALL):]
        pack, rel = _sum_parts(refs[0]), _sum_parts(refs[1])
        for i, k in enumerate(SMALL):
            w_ref, m_ref, v_ref = ins[3 * i:3 * i + 3]
            if k == "rel_bias":
                g = rel
            else:
                row, rows = SMALL_ROWS[k]
                g = pack[row:row + rows, 0:w_ref.shape[1]]
            outs[4 * i][...] = g
            outs[4 * i + 1][...], outs[4 * i + 2][...], outs[4 * i + 3][...] = _adam_update(g, w_ref[...], m_ref[...], v_ref[...])
        outs[-1][...] = pack[LOSS_ROW:LOSS_ROW + 1, 0:LANES]

    operands = [parts, parts_rel_bias] + [t[k] for k in SMALL for t in (w, m, v)]
    out_shape = [_sds(w[k].shape, F32) for k in SMALL for _ in range(4)] + [_sds((1, LANES), F32)]
    outs = pl.pallas_call(body, name=name, out_shape=out_shape, compiler_params=_params(0))(*operands)
    return {k: outs[4 * i:4 * i + 4] for i, k in enumerate(SMALL)}, outs[-1]


class _Comm:
    ORDER = ("w_in_a", "w_out_a", "w_up0", "w_down0", "w_kv", "w_q_b", "w_out_b", "w_up1", "w_down1")

    def __init__(self, shards, d, n_in):
        self.d, self.n_in = d, n_in
        self.flight = _exchange_start("gather_start", [shards[n].astype(BF16) for n in self.ORDER], scatter=False, ks=SAME_CORE)
        self.relays, self.sent = {}, []

    def prefetch(self, names, after):
        which = [self.ORDER.index(n) for n in names]
        landed = _exchange_wait(f"gather_wait_{names[0]}", self.flight, which, self.flight.token if after is None else after)
        relay = _relay_start(f"gather_relay_{names[0]}", landed)
        for n in names:
            self.relays[n] = (relay, names)
        return relay.token

    def weights(self, names, after):
        relay, group = self.relays[names[0]]
        landed = _relay_wait(f"gather_relay_wait_{names[0]}", relay, [group.index(n) for n in names],
                             relay.token if after is None else after)
        return [self._whole(n, g) for n, g in zip(names, landed)]

    def _whole(self, name, g):
        if name == "w_in_a":
            return _join_row_blocks("w_in_join", g, -(-self.n_in // LANES) * LANES)
        if name.startswith("w_up"):
            return g
        return g.reshape(-1, g.shape[-1])

    def _chunks(self, name, g):
        if name == "w_in_a":
            return _split_row_blocks("dw_in_split", g, N_DEV, self.n_in // N_DEV)
        if name.startswith("w_up"):
            return g
        return g.reshape(N_DEV, g.shape[0] // N_DEV, g.shape[1])

    def send_grads(self, tag, partials):
        names = list(partials)
        flight = _exchange_start(f"scatter_start_{tag}", [self._chunks(n, partials[n]) for n in names], scatter=True)
        self.sent.append((tag, flight, names))
        return flight.token

    def received(self, index, after):
        tag, flight, names = self.sent[index]
        landed = _exchange_wait(f"scatter_wait_{tag}", flight, list(range(len(names))), after)
        return dict(zip(names, landed))


def kernel(x, g_attn, g_mlp, w_in_a, b_f, gq_a, gk_a, w_out_a, g_kv, w_kv, gk_b, w_q_b, gq_b, sinks, rel_bias, w_out_b, w_up, w_down, loss_target, m_g_attn, m_g_mlp, m_w_in_a, m_b_f, m_gq_a, m_gk_a, m_w_out_a, m_g_kv, m_w_kv, m_gk_b, m_w_q_b, m_gq_b, m_sinks, m_rel_bias, m_w_out_b, m_w_up, m_w_down, v_g_attn, v_g_mlp, v_w_in_a, v_b_f, v_gq_a, v_gk_a, v_w_out_a, v_g_kv, v_w_kv, v_gk_b, v_w_q_b, v_gq_b, v_sinks, v_rel_bias, v_w_out_b, v_w_up, v_w_down):
    w = dict(g_attn=g_attn, g_mlp=g_mlp, w_in_a=w_in_a, b_f=b_f, gq_a=gq_a, gk_a=gk_a, w_out_a=w_out_a, g_kv=g_kv,
             w_kv=w_kv, gk_b=gk_b, w_q_b=w_q_b, gq_b=gq_b, sinks=sinks, rel_bias=rel_bias, w_out_b=w_out_b,
             w_up=w_up, w_down=w_down)
    mom = dict(g_attn=m_g_attn, g_mlp=m_g_mlp, w_in_a=m_w_in_a, b_f=m_b_f, gq_a=m_gq_a, gk_a=m_gk_a, w_out_a=m_w_out_a,
               g_kv=m_g_kv, w_kv=m_w_kv, gk_b=m_gk_b, w_q_b=m_w_q_b, gq_b=m_gq_b, sinks=m_sinks, rel_bias=m_rel_bias,
               w_out_b=m_w_out_b, w_up=m_w_up, w_down=m_w_down)
    var = dict(g_attn=v_g_attn, g_mlp=v_g_mlp, w_in_a=v_w_in_a, b_f=v_b_f, gq_a=v_gq_a, gk_a=v_gk_a, w_out_a=v_w_out_a,
               g_kv=v_g_kv, w_kv=v_w_kv, gk_b=v_gk_b, w_q_b=v_w_q_b, gq_b=v_gq_b, sinks=v_sinks, rel_bias=v_rel_bias,
               w_out_b=v_w_out_b, w_up=v_w_up, w_down=v_w_down)
    d = x.shape[2]
    where = {"w_in_a": ("w_in_a", 0), "w_out_a": ("w_out_a", 0), "w_kv": ("w_kv", None), "w_q_b": ("w_q_b", 0),
             "w_out_b": ("w_out_b", 0), "w_up0": ("w_up", 0), "w_up1": ("w_up", 1), "w_down0": ("w_down", 0),
             "w_down1": ("w_down", 1)}
    flip = lambda tree: {**tree, "w_in_a": jnp.swapaxes(tree["w_in_a"], 1, 2)}
    w, mom, var = flip(w), flip(mom), flip(var)
    shards = {n: (w[k] if layer is None else w[k][layer]) for n, (k, layer) in where.items()}
    comm = _Comm(shards, d, w_in_a.shape[2] * N_DEV)
    loss_tile, grad_x, grads = _local_step(x[0], loss_target[0], {k: w[k] for k in SMALL}, comm)

    small_flight = _exchange_start(
        "gather_small_grads", [_pack_small("pack_small", _small_rows(grads, loss_tile), d), grads["rel_bias"]], scatter=False)
    res, after = {}, small_flight.token
    for index in range(len(comm.sent)):
        for n, parts in comm.received(index, after).items():
            k, layer = where[n]
            res[k] = _adamw(f"adam_{n}", parts, w[k], mom[k], var[k], layer, res.get(k))
            after = res[k][0]
    as_rows = lambda tree: {k: tree[k] if tree[k].ndim == 2 else tree[k].reshape(1, -1) for k in SMALL}
    small, loss_row = _adamw_small("adam_small", *_exchange_wait("gather_small_wait", small_flight, [0, 1], after),
                                   as_rows(w), as_rows(mom), as_rows(var))
    loss = loss_row[0, 0]
    for k in SMALL:
        res[k] = [a.reshape(w[k].shape) for a in small[k]]
    res["w_in_a"] = [jnp.swapaxes(a, 1, 2) for a in res["w_in_a"]]

    outs = [loss, grad_x[None]]
    for i in range(4):
        outs.extend(res[k][i] for k in WEIGHTS)
    return tuple(outs)
```

```python
import numpy as np
import jax
import jax.numpy as jnp
from jax import lax
from jax.experimental import pallas as pl
from jax.experimental.pallas import tpu as pltpu

F32 = jnp.float32
BF16 = jnp.bfloat16

N_DEV = 8
HEAD_DIM = 64
WINDOW = 128
N_BUCKETS = 32
REL_MAX_DIST = 128
NORM_EPS = 1e-6
NEG = -1e30
LANES = 128
VMEM_LIMIT = 56 * 1024 * 1024

ADAM_LR = 0.001
ADAM_B1 = 0.9
ADAM_B2 = 0.999
ADAM_EPS = 1e-08
ADAM_WD = 0.01
ADAM_STEP = 10

SMALL = ("g_attn", "g_mlp", "b_f", "gq_a", "gk_a", "g_kv", "gk_b", "gq_b", "sinks", "rel_bias")
WEIGHTS = ("g_attn", "g_mlp", "w_in_a", "b_f", "gq_a", "gk_a", "w_out_a", "g_kv", "w_kv", "gk_b",
           "w_q_b", "gq_b", "sinks", "rel_bias", "w_out_b", "w_up", "w_down")


def _params(n_grid):
    return pltpu.CompilerParams(dimension_semantics=("arbitrary",) * n_grid, vmem_limit_bytes=VMEM_LIMIT)


def _sds(shape, dtype):
    return jax.ShapeDtypeStruct(tuple(shape), dtype)


def _after_operand(after):
    if after is None:
        return [], []
    return [pl.BlockSpec((8, LANES), lambda *_: (0, 0))], [after]


def _rms(x, g):
    return (x * lax.rsqrt(jnp.mean(x * x, axis=-1, keepdims=True) + NORM_EPS)) * g


def _dot_nt(a, b):
    return lax.dot_general(a, b, (((1,), (1,)), ((), ())), preferred_element_type=F32)


def _dot_tn(a, b):
    return lax.dot_general(a, b, (((0,), (0,)), ((), ())), preferred_element_type=F32)


def _dot(a, b):
    return jnp.dot(a, b, preferred_element_type=F32)


def _dot_exact(a, b):
    return jnp.dot(a, b, preferred_element_type=F32, precision=lax.Precision.HIGHEST)


def _norm_matmul(name, x, g, w, *, tn=None, relu2=False, w_rows=False, after=None):
    t, d = x.shape
    blocked = w.ndim == 3
    per_step = 2 if blocked else 1
    if blocked:
        tn = per_step * w.shape[2]
        n = w.shape[0] * w.shape[2]
        w_spec = pl.BlockSpec((per_step, d, w.shape[2]), lambda i, j: (j, 0, 0))
    elif w_rows:
        n = w.shape[0]
        w_spec = pl.BlockSpec((tn, d), lambda i, j: (j, 0))
    else:
        n = w.shape[1]
        w_spec = pl.BlockSpec((d, tn), lambda i, j: (0, j))
    tm = min(1024, t)

    def body(x_ref, g_ref, w_ref, *rest):
        y_ref, xn_ref = rest[-2:]

        @pl.when(pl.program_id(1) == 0)
        def _():
            xn_ref[...] = _rms(x_ref[...], g_ref[...]).astype(BF16)

        for b in range(per_step):
            cols = slice(b * (tn // per_step), (b + 1) * (tn // per_step)) if blocked else slice(None)
            wb = w_ref[b] if blocked else w_ref[...]
            y = _dot_nt(xn_ref[...], wb) if w_rows else _dot(xn_ref[...], wb)
            y_ref[:, cols] = jnp.square(jnp.maximum(y, 0.0)).astype(BF16) if relu2 else y

    extra_specs, extra = _after_operand(after)
    out_shape = [_sds((t, n), BF16 if relu2 else F32), _sds((t, d), BF16)]
    out_specs = [pl.BlockSpec((tm, tn), lambda i, j: (i, j)), pl.BlockSpec((tm, d), lambda i, j: (i, 0))]
    return pl.pallas_call(
        body, name=name, grid=(t // tm, n // tn),
        in_specs=[pl.BlockSpec((tm, d), lambda i, j: (i, 0)), pl.BlockSpec((1, d), lambda i, j: (0, 0)), w_spec] + extra_specs,
        out_specs=out_specs, out_shape=out_shape, compiler_params=_params(2),
    )(x, g.reshape(1, d), w, *extra)


def _matmul_res(name, a, w, res, *, tn=512, after=None):
    t, k = a.shape
    n = w.shape[1]
    tm = min(1024, t)

    def body(a_ref, w_ref, r_ref, *rest):
        rest[-1][...] = r_ref[...] + _dot(a_ref[...], w_ref[...])

    extra_specs, extra = _after_operand(after)
    return pl.pallas_call(
        body, name=name, grid=(t // tm, n // tn),
        in_specs=[pl.BlockSpec((tm, k), lambda i, j: (i, 0)), pl.BlockSpec((k, tn), lambda i, j: (0, j)),
                  pl.BlockSpec((tm, tn), lambda i, j: (i, j))] + extra_specs,
        out_specs=pl.BlockSpec((tm, tn), lambda i, j: (i, j)), out_shape=_sds((t, n), F32),
        compiler_params=_params(2),
    )(a, w, res, *extra)


def _matmul_nt(name, dy, w, *, a=None, tk=1024):
    t, n = dy.shape
    k = w.shape[0]
    tm = min(1024, t)

    def body(dy_ref, w_ref, *rest):
        o_ref = rest[-1]
        r = _dot_nt(dy_ref[...].astype(BF16), w_ref[...])
        if a is not None:
            r = r * (2.0 * jnp.sqrt(rest[0][...].astype(F32)))
        o_ref[...] = r.astype(BF16)

    in_specs = [pl.BlockSpec((tm, n), lambda i, j: (i, 0)), pl.BlockSpec((tk, n), lambda i, j: (j, 0))]
    args = [dy, w]
    if a is not None:
        in_specs.append(pl.BlockSpec((tm, tk), lambda i, j: (i, j)))
        args.append(a)
    return pl.pallas_call(
        body, name=name, grid=(t // tm, k // tk), in_specs=in_specs,
        out_specs=pl.BlockSpec((tm, tk), lambda i, j: (i, j)), out_shape=_sds((t, k), BF16),
        compiler_params=_params(2),
    )(*args)


def _matmul_nt_rmsbwd(name, dy, w, x, g, dres, *, w_rows=False, after=None):
    t, k = dy.shape
    blocked = w.ndim == 3
    d = w.shape[1] if blocked or w_rows else w.shape[0]
    tm = min(512, t)

    def body(dy_ref, w_ref, x_ref, g_ref, r_ref, *rest):
        dx_ref, dg_ref = rest[-2:]
        if blocked:
            kb = w.shape[2]
            dxn = _dot_nt(dy_ref[:, 0:kb].astype(BF16), w_ref[0])
            for j in range(1, w.shape[0]):
                dxn += _dot_nt(dy_ref[:, j * kb:(j + 1) * kb].astype(BF16), w_ref[j])
        elif w_rows:
            dxn = _dot(dy_ref[...].astype(BF16), w_ref[...])
        else:
            dxn = _dot_nt(dy_ref[...].astype(BF16), w_ref[...])
        _, vjp = jax.vjp(_rms, x_ref[...], g_ref[...])
        dx, dg = vjp(dxn)
        dx_ref[...] = r_ref[...] + dx

        @pl.when(pl.program_id(0) == 0)
        def _():
            dg_ref[...] = jnp.zeros_like(dg_ref)

        dg_ref[...] += dg

    row = lambda i: (i, 0)
    fixed = lambda i: (0, 0)
    extra_specs, extra = _after_operand(after)
    return pl.pallas_call(
        body, name=name, grid=(t // tm,),
        in_specs=[pl.BlockSpec((tm, k), row), pl.BlockSpec(w.shape, (lambda i: (0, 0, 0)) if blocked else fixed),
                  pl.BlockSpec((tm, d), row), pl.BlockSpec((1, d), fixed), pl.BlockSpec((tm, d), row)] + extra_specs,
        out_specs=[pl.BlockSpec((tm, d), row), pl.BlockSpec((1, d), fixed)],
        out_shape=[_sds((t, d), F32), _sds((1, d), F32)], compiler_params=_params(1),
    )(dy, w, x, g.reshape(1, d), dres, *extra)


def _matmul_tn(name, a, b, *, tk=1024, tn=1024, col_blocks=None):
    t, k = a.shape
    n = b.shape[1]
    tk = min(tk, k)
    if col_blocks:
        tn = n // col_blocks
        out_spec, out_shape = pl.BlockSpec((None, tk, tn), lambda i, j: (j, i, 0)), _sds((col_blocks, k, tn), BF16)
    else:
        tn = min(tn, n)
        out_spec, out_shape = pl.BlockSpec((tk, tn), lambda i, j: (i, j)), _sds((k, n), BF16)

    def body(a_ref, b_ref, o_ref):
        o_ref[...] = _dot_tn(a_ref[...].astype(BF16), b_ref[...].astype(BF16)).astype(BF16)

    return pl.pallas_call(
        body, name=name, grid=(k // tk, n // tn),
        in_specs=[pl.BlockSpec((t, tk), lambda i, j: (0, i)), pl.BlockSpec((t, tn), lambda i, j: (0, j))],
        out_specs=out_spec, out_shape=out_shape, compiler_params=_params(2),
    )(a, b)


def _join_row_blocks(name, blocks, rows):
    b, r, c = blocks.shape
    tc = min(256, c)

    def body(g_ref, o_ref):
        o_ref[...] = jnp.zeros_like(o_ref)
        for j in range(b):
            o_ref[r * j:r * (j + 1), :] = g_ref[j]

    return pl.pallas_call(
        body, name=name, grid=(c // tc,), in_specs=[pl.BlockSpec((b, r, tc), lambda i: (0, 0, i))],
        out_specs=pl.BlockSpec((rows, tc), lambda i: (0, i)), out_shape=_sds((rows, c), blocks.dtype),
        compiler_params=_params(1),
    )(blocks)


def _split_row_blocks(name, mat, b, r):
    rows, c = mat.shape
    tc = min(256, c)

    def body(w_ref, o_ref):
        for j in range(b):
            o_ref[j] = w_ref[r * j:r * (j + 1), :]

    return pl.pallas_call(
        body, name=name, grid=(c // tc,), in_specs=[pl.BlockSpec((rows, tc), lambda i: (0, i))],
        out_specs=pl.BlockSpec((b, r, tc), lambda i: (0, 0, i)), out_shape=_sds((b, r, c), mat.dtype),
        compiler_params=_params(1),
    )(mat)


def _matmul_res_loss(name, a, w, res, target, *, tn=512):
    t, k = a.shape
    n = w.shape[1]
    tm = min(1024, t)

    def body(a_ref, w_ref, r_ref, t_ref, dy_ref, l_ref):
        e = r_ref[...] + _dot(a_ref[...], w_ref[...]) - t_ref[...]
        dy_ref[...] = e * (1.0 / n)

        @pl.when((pl.program_id(0) == 0) & (pl.program_id(1) == 0))
        def _():
            l_ref[...] = jnp.zeros_like(l_ref)

        l_ref[...] += (0.5 / n) * jnp.sum(e * e)

    tile = pl.BlockSpec((tm, tn), lambda i, j: (i, j))
    return pl.pallas_call(
        body, name=name, grid=(t // tm, n // tn),
        in_specs=[pl.BlockSpec((tm, k), lambda i, j: (i, 0)), pl.BlockSpec((k, tn), lambda i, j: (0, j)), tile, tile],
        out_specs=[tile, pl.BlockSpec((8, LANES), lambda i, j: (0, 0))],
        out_shape=[_sds((t, n), F32), _sds((8, LANES), F32)], compiler_params=_params(2),
    )(a, w, res, target)


def _gate_fwd(name, proj, b_pad, n_heads, gate_col, after=None):
    t = proj.shape[0]
    tb = min(256, t)
    tri = jnp.asarray(np.tril(np.ones((tb, tb), np.float32)))
    extra_specs, extra = _after_operand(after)

    def body(p_ref, b_ref, tri_ref, *rest):
        c_ref, carry = rest[-2:]

        @pl.when(pl.program_id(0) == 0)
        def _():
            carry[...] = jnp.zeros_like(carry)

        lane = lax.broadcasted_iota(jnp.int32, (tb, LANES), 1)
        lf = jnp.where(lane < n_heads, jax.nn.log_sigmoid(p_ref[...] + b_ref[...]), 0.0)
        c = _dot_exact(tri_ref[...], lf) + carry[0:1, :]
        c_ref[...] = c
        carry[...] = jnp.broadcast_to(c[tb - 1:tb, :], carry.shape)

    return pl.pallas_call(
        body, name=name, grid=(t // tb,),
        in_specs=[pl.BlockSpec((tb, LANES), lambda i: (i, gate_col)), pl.BlockSpec((1, LANES), lambda i: (0, 0)),
                  pl.BlockSpec((tb, tb), lambda i: (0, 0))] + extra_specs,
        out_specs=pl.BlockSpec((tb, LANES), lambda i: (i, 0)), out_shape=_sds((t, LANES), F32),
        scratch_shapes=[pltpu.VMEM((8, LANES), F32)], compiler_params=_params(1),
    )(proj, b_pad, tri, *extra)


def _gate_bwd(name, proj, b_pad, dc, n_heads, gate_col):
    t = proj.shape[0]
    tb = min(256, t)
    nb = t // tb
    triu = jnp.asarray(np.triu(np.ones((tb, tb), np.float32)))

    def body(p_ref, b_ref, dc_ref, tri_ref, df_ref, db_ref, carry):
        @pl.when(pl.program_id(0) == 0)
        def _():
            carry[...] = jnp.zeros_like(carry)
            db_ref[...] = jnp.zeros_like(db_ref)

        dcv = dc_ref[...]
        dlf = _dot_exact(tri_ref[...], dcv) + carry[0:1, :]
        carry[...] = jnp.broadcast_to(dlf[0:1, :], carry.shape)
        lane = lax.broadcasted_iota(jnp.int32, (tb, LANES), 1)
        z = p_ref[...] + b_ref[...]
        df = jnp.where(lane < n_heads, dlf / (1.0 + jnp.exp(z)), 0.0)
        df_ref[...] = df.astype(BF16)
        db_ref[...] += jnp.sum(df, axis=0, keepdims=True)

    return pl.pallas_call(
        body, name=name, grid=(nb,),
        in_specs=[pl.BlockSpec((tb, LANES), lambda i: (nb - 1 - i, gate_col)), pl.BlockSpec((1, LANES), lambda i: (0, 0)),
                  pl.BlockSpec((tb, LANES), lambda i: (nb - 1 - i, 0)), pl.BlockSpec((tb, tb), lambda i: (0, 0))],
        out_specs=[pl.BlockSpec((tb, LANES), lambda i: (nb - 1 - i, 0)), pl.BlockSpec((1, LANES), lambda i: (0, 0))],
        out_shape=[_sds((t, LANES), BF16), _sds((1, LANES), F32)],
        scratch_shapes=[pltpu.VMEM((8, LANES), F32)], compiler_params=_params(1),
    )(proj, b_pad, dc, triu)


def _qhead(qp, g):
    return _rms(qp, g) * (HEAD_DIM ** -0.5)


def _column(mat, idx):
    lane = lax.broadcasted_iota(jnp.int32, mat.shape, 1)
    return jnp.sum(jnp.where(lane == idx, mat, 0.0), axis=1, keepdims=True)


def _fox_scores(kk, qi, ckey, cq_i, i, bq):
    length = kk.shape[0]
    s = _dot_nt(kk, qi) + cq_i - ckey[:length]
    key = lax.broadcasted_iota(jnp.int32, (length, bq), 0)
    qry = lax.broadcasted_iota(jnp.int32, (length, bq), 1) + i * bq
    return jnp.where(key <= qry, s, NEG)


def _fox_fwd(name, proj, c, crow, gq, gk, n_heads):
    t = proj.shape[0]
    hw = n_heads * HEAD_DIM
    npair = n_heads // 2
    bq = min(512, t)
    nq = t // bq

    def body(q_ref, k_ref, v_ref, c_ref, crow_ref, gq_ref, gk_ref, o_ref, lse_ref):
        hp = pl.program_id(0)
        lse_ref[...] = jnp.zeros_like(lse_ref)
        outs = []
        for hh in range(2):
            sl = slice(hh * HEAD_DIM, (hh + 1) * HEAD_DIM)
            qn = _qhead(q_ref[:, sl], gq_ref[...]).astype(BF16)
            kn = _rms(k_ref[:, sl], gk_ref[...]).astype(BF16)
            v_t = v_ref[:, sl].T.astype(BF16)
            ckey = _column(c_ref[...], 2 * hp + hh)
            cq = crow_ref[0, hh:hh + 1, :]
            o_blocks = []
            for i in range(nq):
                cols = slice(i * bq, (i + 1) * bq)
                length = (i + 1) * bq
                s = _fox_scores(kn[:length], qn[cols], ckey, cq[:, cols], i, bq)
                m = jnp.max(s, axis=0, keepdims=True)
                p = jnp.exp(s - m)
                l = jnp.sum(p, axis=0, keepdims=True)
                o_blocks.append((_dot(v_t[:, :length], p.astype(BF16)) / l).T)
                lse_ref[0, hh:hh + 1, cols] = m + jnp.log(l)
            outs.append(jnp.concatenate(o_blocks, axis=0))
        o_ref[...] = jnp.concatenate(outs, axis=1).astype(BF16)

    col = lambda off: (lambda h: (0, off + h))
    fixed = lambda h: (0, 0)
    return pl.pallas_call(
        body, name=name, grid=(npair,),
        in_specs=[pl.BlockSpec((t, LANES), col(0)), pl.BlockSpec((t, LANES), col(npair)), pl.BlockSpec((t, LANES), col(2 * npair)),
                  pl.BlockSpec((t, LANES), fixed), pl.BlockSpec((1, 2, t), lambda h: (h, 0, 0)),
                  pl.BlockSpec((1, HEAD_DIM), fixed), pl.BlockSpec((1, HEAD_DIM), fixed)],
        out_specs=[pl.BlockSpec((t, LANES), col(0)), pl.BlockSpec((1, 8, t), lambda h: (h, 0, 0))],
        out_shape=[_sds((t, hw), BF16), _sds((npair, 8, t), F32)], compiler_params=_params(1),
    )(proj, proj, proj, c, crow, gq, gk)


def _fox_bwd(name, proj, c, crow, gq, gk, lse, do, n_heads, after=None):
    t = proj.shape[0]
    hw = n_heads * HEAD_DIM
    npair = n_heads // 2
    bq = min(256, t)
    nq = t // bq

    def body(q_ref, k_ref, v_ref, c_ref, crow_ref, gq_ref, gk_ref, lse_ref, do_ref, *rest):
        dq_ref, dk_ref, dv_ref, dc_ref, dgq_ref, dgk_ref, dk_acc, dv_acc, dc_acc = rest[-9:]
        hp = pl.program_id(0)

        @pl.when(hp == 0)
        def _():
            dgq_ref[...] = jnp.zeros_like(dgq_ref)
            dgk_ref[...] = jnp.zeros_like(dgk_ref)
            dc_ref[...] = jnp.zeros_like(dc_ref)

        lane = lax.broadcasted_iota(jnp.int32, (t, LANES), 1)
        dqs, dks, dvs = [], [], []
        for hh in range(2):
            sl = slice(hh * HEAD_DIM, (hh + 1) * HEAD_DIM)
            qf, q_vjp = jax.vjp(_qhead, q_ref[:, sl], gq_ref[...])
            kf, k_vjp = jax.vjp(_rms, k_ref[:, sl], gk_ref[...])
            qn, kn, kn_t = qf.astype(BF16), kf.astype(BF16), kf.T.astype(BF16)
            vb = v_ref[:, sl].astype(BF16)
            dob = do_ref[:, sl]
            ckey = _column(c_ref[...], 2 * hp + hh)
            cq = crow_ref[0, hh:hh + 1, :]
            lse_h = lse_ref[0, hh:hh + 1, :]
            dk_acc[...] = jnp.zeros_like(dk_acc)
            dv_acc[...] = jnp.zeros_like(dv_acc)
            dc_acc[...] = jnp.zeros_like(dc_acc)
            dq_blocks = []
            for i in range(nq):
                cols = slice(i * bq, (i + 1) * bq)
                length = (i + 1) * bq
                qi, doi = qn[cols], dob[cols]
                s = _fox_scores(kn[:length], qi, ckey, cq[:, cols], i, bq)
                p = jnp.exp(s - lse_h[:, cols])
                dp = _dot_nt(vb[:length], doi)
                ds = p * (dp - jnp.sum(p * dp, axis=0, keepdims=True))
                dsb = ds.astype(BF16)
                dq_blocks.append(_dot(kn_t[:, :length], dsb).T)
                dk_acc[0:length, :] += _dot(dsb, qi)
                dv_acc[0:length, :] += _dot(p.astype(BF16), doi)
                part = ds[:, 0:LANES]
                for j in range(1, bq // LANES):
                    part = part + ds[:, j * LANES:(j + 1) * LANES]
                dc_acc[0:length, :] += part
            dqp, dgq = q_vjp(jnp.concatenate(dq_blocks, axis=0))
            dkp, dgk = k_vjp(dk_acc[...])
            dgq_ref[...] += dgq
            dgk_ref[...] += dgk
            dqs.append(dqp)
            dks.append(dkp)
            dvs.append(dv_acc[...])
            dc_ref[...] = jnp.where(lane == 2 * hp + hh, -jnp.sum(dc_acc[...], axis=1, keepdims=True), dc_ref[...])
        dq_ref[...] = jnp.concatenate(dqs, axis=1).astype(BF16)
        dk_ref[...] = jnp.concatenate(dks, axis=1).astype(BF16)
        dv_ref[...] = jnp.concatenate(dvs, axis=1).astype(BF16)

    col = lambda off: (lambda h: (0, off + h))
    fixed = lambda h: (0, 0)
    pair_blk = pl.BlockSpec((t, LANES), col(0))
    extra_specs, extra = _after_operand(after)
    return pl.pallas_call(
        body, name=name, grid=(npair,),
        in_specs=[pl.BlockSpec((t, LANES), col(0)), pl.BlockSpec((t, LANES), col(npair)), pl.BlockSpec((t, LANES), col(2 * npair)),
                  pl.BlockSpec((t, LANES), fixed), pl.BlockSpec((1, 2, t), lambda h: (h, 0, 0)),
                  pl.BlockSpec((1, HEAD_DIM), fixed), pl.BlockSpec((1, HEAD_DIM), fixed),
                  pl.BlockSpec((1, 8, t), lambda h: (h, 0, 0)), pair_blk] + extra_specs,
        out_specs=[pair_blk, pair_blk, pair_blk, pl.BlockSpec((t, LANES), fixed),
                   pl.BlockSpec((1, HEAD_DIM), fixed), pl.BlockSpec((1, HEAD_DIM), fixed)],
        out_shape=[_sds((t, hw), BF16), _sds((t, hw), BF16), _sds((t, hw), BF16), _sds((t, LANES), F32),
                   _sds((1, HEAD_DIM), F32), _sds((1, HEAD_DIM), F32)],
        scratch_shapes=[pltpu.VMEM((t, HEAD_DIM), F32), pltpu.VMEM((t, HEAD_DIM), F32), pltpu.VMEM((t, LANES), F32)],
        compiler_params=_params(1),
    )(proj, proj, proj, c, crow, gq, gk, lse, do, *extra)


def _t5_bucket_table():
    dist = np.arange(WINDOW)[None, :] + WINDOW - np.arange(2 * WINDOW)[:, None]
    n = np.maximum(dist, 0)
    max_exact = N_BUCKETS // 2
    large = max_exact + (np.log(np.maximum(n, 1) / max_exact) / np.log(REL_MAX_DIST / max_exact)
                         * (N_BUCKETS - max_exact)).astype(np.int32)
    large = np.minimum(large, N_BUCKETS - 1)
    return np.where(n < max_exact, n, large).astype(np.int32).reshape(1, -1)


def _bias_expand(name, rel_bias_t):
    n_heads = rel_bias_t.shape[0]
    tbl = jnp.asarray(_t5_bucket_table())
    width = tbl.shape[1]

    def body(rb_ref, tbl_ref, o_ref):
        onehot = (lax.broadcasted_iota(jnp.int32, (N_BUCKETS, width), 0) == tbl_ref[...]).astype(F32)
        o_ref[...] = _dot_exact(rb_ref[...], onehot)

    return pl.pallas_call(body, name=name, out_shape=_sds((n_heads, width), F32), compiler_params=_params(0))(rel_bias_t, tbl)


def _bias_reduce(name, dbias):
    n_heads, width = dbias.shape
    tbl = jnp.asarray(_t5_bucket_table())

    def body(db_ref, tbl_ref, o_ref):
        onehot = (lax.broadcasted_iota(jnp.int32, (N_BUCKETS, width), 0) == tbl_ref[...]).astype(F32)
        o_ref[...] = lax.dot_general(db_ref[...], onehot, (((1,), (1,)), ((), ())), preferred_element_type=F32,
                                     precision=lax.Precision.HIGHEST)

    return pl.pallas_call(body, name=name, out_shape=_sds((n_heads, N_BUCKETS), F32), compiler_params=_params(0))(dbias, tbl)


def _swa_mask(n, group):
    j = lax.broadcasted_iota(jnp.int32, (2 * WINDOW, group * WINDOW), 0)
    i = lax.broadcasted_iota(jnp.int32, (2 * WINDOW, group * WINDOW), 1) & (WINDOW - 1)
    ok = (j > i) & (j <= i + WINDOW) & ((n > 0) | (j >= WINDOW))
    return jnp.where(ok, 0.0, NEG)


def _swa_stack(ref, start, group):
    return jnp.concatenate([ref[pl.ds(start, WINDOW), g * HEAD_DIM:(g + 1) * HEAD_DIM] for g in range(group)], axis=0)


def _kv_head(ref, n_kv):
    out = ref[:, 0:HEAD_DIM]
    for h in range(1, n_kv):
        out = jnp.where(pl.program_id(0) == h, ref[:, h * HEAD_DIM:(h + 1) * HEAD_DIM], out)
    return out


def _swa_fwd(name, qb, kv, gq, gk, sinks, bias, group):
    t = qb.shape[0]
    kvh = kv.shape[1] // (2 * HEAD_DIM)
    nblk = t // WINDOW
    gw = group * HEAD_DIM
    band = 2 * WINDOW
    cols = group * WINDOW

    def body(q_ref, k_ref, v_ref, gq_ref, gk_ref, sink_ref, bias_ref, o_ref, lse_ref, qs, kpad, vpad):
        for g in range(group):
            qs[:, g * HEAD_DIM:(g + 1) * HEAD_DIM] = _qhead(q_ref[:, g * HEAD_DIM:(g + 1) * HEAD_DIM], gq_ref[...]).astype(BF16)
        kpad[0:WINDOW, :] = jnp.zeros((WINDOW, HEAD_DIM), BF16)
        vpad[0:WINDOW, :] = jnp.zeros((WINDOW, HEAD_DIM), BF16)
        kpad[WINDOW:, :] = _rms(_kv_head(k_ref, kvh), gk_ref[...]).astype(BF16)
        vpad[WINDOW:, :] = _kv_head(v_ref, kvh).astype(BF16)
        sink = sink_ref[0]

        def block(n, carry):
            start = pl.multiple_of(n * WINDOW, WINDOW)
            kb = kpad[pl.ds(start, band), :]
            vb = vpad[pl.ds(start, band), :]
            s = _dot_nt(kb, _swa_stack(qs, start, group)) + bias_ref[0] + _swa_mask(n, group)
            m = jnp.maximum(jnp.max(s, axis=0, keepdims=True), sink)
            e = jnp.exp(s - m)
            l = jnp.sum(e, axis=0, keepdims=True) + jnp.exp(sink - m)
            o_t = _dot_tn(vb, e.astype(BF16)) / l
            for g in range(group):
                o_ref[pl.ds(start, WINDOW), g * HEAD_DIM:(g + 1) * HEAD_DIM] = o_t[:, g * WINDOW:(g + 1) * WINDOW].T.astype(BF16)
            lse_ref[pl.ds(n, 1), :] = m + jnp.log(l)
            return carry

        lax.fori_loop(0, nblk, block, 0)

    fixed = lambda h: (0, 0)
    per = lambda h: (h, 0, 0)
    return pl.pallas_call(
        body, name=name, grid=(kvh,),
        in_specs=[pl.BlockSpec((t, gw), lambda h: (0, h)), pl.BlockSpec((t, kvh * HEAD_DIM), lambda h: (0, 0)),
                  pl.BlockSpec((t, kvh * HEAD_DIM), lambda h: (0, 1)),
                  pl.BlockSpec((1, HEAD_DIM), fixed), pl.BlockSpec((1, HEAD_DIM), fixed),
                  pl.BlockSpec((1, 1, cols), per), pl.BlockSpec((1, band, cols), per)],
        out_specs=[pl.BlockSpec((t, gw), lambda h: (0, h)), pl.BlockSpec((nblk, cols), lambda h: (h, 0))],
        out_shape=[_sds((t, kvh * gw), BF16), _sds((kvh * nblk, cols), F32)],
        scratch_shapes=[pltpu.VMEM((t, gw), BF16), pltpu.VMEM((t + WINDOW, HEAD_DIM), BF16),
                        pltpu.VMEM((t + WINDOW, HEAD_DIM), BF16)],
        compiler_params=_params(1),
    )(qb, kv, kv, gq, gk, sinks, bias)


def _swa_bwd(name, qb, kv, gq, gk, sinks, bias, lse, do, group):
    t = qb.shape[0]
    kvh = kv.shape[1] // (2 * HEAD_DIM)
    kvw = kvh * HEAD_DIM
    nblk = t // WINDOW
    gw = group * HEAD_DIM
    band = 2 * WINDOW
    cols = group * WINDOW

    def body(q_ref, k_ref, v_ref, gq_ref, gk_ref, sink_ref, bias_ref, lse_ref, do_ref,
             dq_ref, dkv_ref, dgq_ref, dgk_ref, dsink_ref, dbias_ref,
             qs, kpad, vpad, dqs, dk_acc, dv_acc, dsink_acc):
        @pl.when(pl.program_id(0) == 0)
        def _():
            dgq_ref[...] = jnp.zeros_like(dgq_ref)
            dgk_ref[...] = jnp.zeros_like(dgk_ref)
            dkv_ref[...] = jnp.zeros_like(dkv_ref)

        for g in range(group):
            qs[:, g * HEAD_DIM:(g + 1) * HEAD_DIM] = _qhead(q_ref[:, g * HEAD_DIM:(g + 1) * HEAD_DIM], gq_ref[...]).astype(BF16)
        kpad[0:WINDOW, :] = jnp.zeros((WINDOW, HEAD_DIM), BF16)
        vpad[0:WINDOW, :] = jnp.zeros((WINDOW, HEAD_DIM), BF16)
        kpad[WINDOW:, :] = _rms(_kv_head(k_ref, kvh), gk_ref[...]).astype(BF16)
        vpad[WINDOW:, :] = _kv_head(v_ref, kvh).astype(BF16)
        dk_acc[...] = jnp.zeros_like(dk_acc)
        dv_acc[...] = jnp.zeros_like(dv_acc)
        dsink_acc[...] = jnp.zeros_like(dsink_acc)
        dbias_ref[...] = jnp.zeros_like(dbias_ref)
        sink = sink_ref[0]

        def block(n, carry):
            start = pl.multiple_of(n * WINDOW, WINDOW)
            kb = kpad[pl.ds(start, band), :]
            vb = vpad[pl.ds(start, band), :]
            q = _swa_stack(qs, start, group)
            dob = _swa_stack(do_ref, start, group)
            lse_n = lse_ref[pl.ds(n, 1), :]
            s = _dot_nt(kb, q) + bias_ref[0] + _swa_mask(n, group)
            p = jnp.exp(s - lse_n)
            dp = _dot_nt(vb, dob)
            dsum = jnp.sum(p * dp, axis=0, keepdims=True)
            ds = p * (dp - dsum)
            dsb = ds.astype(BF16)
            dsink_acc[...] -= jnp.exp(sink - lse_n) * dsum
            dbias_ref[0] += ds
            dq = _dot_tn(dsb, kb)
            for g in range(group):
                dqs[pl.ds(start, WINDOW), g * HEAD_DIM:(g + 1) * HEAD_DIM] = dq[g * WINDOW:(g + 1) * WINDOW]
            dk_acc[pl.ds(start, band), :] += _dot(dsb, q)
            dv_acc[pl.ds(start, band), :] += _dot(p.astype(BF16), dob)
            return carry

        lax.fori_loop(0, nblk, block, 0)
        for g in range(group):
            _, q_vjp = jax.vjp(_qhead, q_ref[:, g * HEAD_DIM:(g + 1) * HEAD_DIM], gq_ref[...])
            dqp, dgq = q_vjp(dqs[:, g * HEAD_DIM:(g + 1) * HEAD_DIM])
            dq_ref[:, g * HEAD_DIM:(g + 1) * HEAD_DIM] = dqp.astype(BF16)
            dgq_ref[...] += dgq
            dsink_g = jnp.sum(dsink_acc[:, g * WINDOW:(g + 1) * WINDOW], axis=1, keepdims=True)
            dsink_ref[0, g:g + 1, :] = jnp.broadcast_to(dsink_g, (1, LANES))
        _, k_vjp = jax.vjp(_rms, _kv_head(k_ref, kvh), gk_ref[...])
        dkp, dgk = k_vjp(dk_acc[WINDOW:, :])
        dgk_ref[...] += dgk
        mine = lax.broadcasted_iota(jnp.int32, (t, kvw), 1) // HEAD_DIM == pl.program_id(0)
        dkv_ref[:, 0:kvw] = jnp.where(mine, jnp.concatenate([dkp] * kvh, axis=1), dkv_ref[:, 0:kvw])
        dkv_ref[:, kvw:] = jnp.where(mine, jnp.concatenate([dv_acc[WINDOW:, :]] * kvh, axis=1), dkv_ref[:, kvw:])

    fixed = lambda h: (0, 0)
    per = lambda h: (h, 0, 0)
    wide = pl.BlockSpec((t, gw), lambda h: (0, h))
    vec = pl.BlockSpec((1, HEAD_DIM), fixed)
    bias_spec = pl.BlockSpec((1, band, cols), per)
    return pl.pallas_call(
        body, name=name, grid=(kvh,),
        in_specs=[wide, pl.BlockSpec((t, kvw), lambda h: (0, 0)), pl.BlockSpec((t, kvw), lambda h: (0, 1)), vec, vec,
                  pl.BlockSpec((1, 1, cols), per), bias_spec, pl.BlockSpec((nblk, cols), lambda h: (h, 0)), wide],
        out_specs=[wide, pl.BlockSpec((t, 2 * kvw), fixed), vec, vec, pl.BlockSpec((1, group, LANES), per), bias_spec],
        out_shape=[_sds((t, kvh * gw), BF16), _sds((t, 2 * kvw), F32),
                   _sds((1, HEAD_DIM), F32), _sds((1, HEAD_DIM), F32),
                   _sds((kvh, group, LANES), F32), _sds((kvh, band, cols), F32)],
        scratch_shapes=[pltpu.VMEM((t, gw), BF16), pltpu.VMEM((t + WINDOW, HEAD_DIM), BF16),
                        pltpu.VMEM((t + WINDOW, HEAD_DIM), BF16), pltpu.VMEM((t, gw), F32),
                        pltpu.VMEM((t + WINDOW, HEAD_DIM), F32), pltpu.VMEM((t + WINDOW, HEAD_DIM), F32),
                        pltpu.VMEM((1, cols), F32)],
        compiler_params=_params(1),
    )(qb, kv, kv, gq, gk, sinks, bias, lse, do)


def _local_step(x, target, p, comm):
    t, d = x.shape
    n_heads = d // HEAD_DIM
    kv_heads = n_heads // 8
    group = n_heads // kv_heads
    hw = n_heads * HEAD_DIM
    gate_col = 3 * hw // LANES
    kvw = kv_heads * HEAD_DIM
    grads = {}

    def mlp_fwd(tag, h, g, layer, last=False):
        w_up, = comm.weights([f"w_up{layer}"], h)
        a, hn = _norm_matmul(f"{tag}_up", h, g, w_up, relu2=True)
        w_down, = comm.weights([f"w_down{layer}"], a)
        out = _matmul_res_loss(f"{tag}_down", a, w_down, h, target) if last else _matmul_res(f"{tag}_down", a, w_down, h)
        return out, (h, g, hn, a, w_up, w_down)

    def mlp_bwd(tag, saved, layer, dy):
        h, g, hn, a, w_up, w_down = saved
        du = _matmul_nt(f"{tag}_du", dy, w_down, a=a)
        dw_down = _matmul_tn(f"{tag}_dwdown", a, dy)
        dw_up = _matmul_tn(f"{tag}_dwup", hn, du, col_blocks=w_up.shape[0])
        sent = comm.send_grads(tag, {f"w_down{layer}": dw_down, f"w_up{layer}": dw_up})
        return _matmul_nt_rmsbwd(f"{tag}_dh", du, w_up, h, g, dy, after=sent)

    bias = _bias_expand("b_bias", p["rel_bias"].T).reshape(kv_heads, group, 2 * WINDOW, WINDOW)
    bias = bias.transpose(0, 2, 1, 3).reshape(kv_heads, 2 * WINDOW, group * WINDOW)
    comm.prefetch(["w_in_a"], bias)
    w_in, = comm.weights(["w_in_a"], None)
    proj, xn1 = _norm_matmul("a_inproj", x, p["g_attn"][0], w_in, tn=640, w_rows=True)
    ahead = comm.prefetch(["w_out_a"], proj)
    b_pad = jnp.pad(p["b_f"], ((0, 0), (0, LANES - n_heads)))
    c = _gate_fwd("a_gate", proj, b_pad, n_heads, gate_col, after=ahead)
    crow = c[:, :n_heads].T.reshape(n_heads // 2, 2, t)
    o_a, lse_a = _fox_fwd("a_attn", proj, c, crow, p["gq_a"], p["gk_a"], n_heads)
    ahead = comm.prefetch(["w_up0", "w_down0", "w_kv", "w_q_b", "w_out_b"], o_a)
    w_out_a, = comm.weights(["w_out_a"], o_a)
    h1 = _matmul_res("a_outproj", o_a, w_out_a, x, after=ahead)
    h2, mlp0 = mlp_fwd("mlp0", h1, p["g_mlp"][0], 0)

    ahead = comm.prefetch(["w_up1", "w_down1"], h2)
    w_kv, w_q_b = comm.weights(["w_kv", "w_q_b"], h2)
    kv, hn_kv = _norm_matmul("kv_proj", h2, p["g_kv"], w_kv, tn=2 * kvw, after=ahead)
    qb, hn_q = _norm_matmul("b_qproj", h2, p["g_attn"][1], w_q_b, tn=512)
    gqb, gkb = p["gq_b"], p["gk_b"].reshape(1, HEAD_DIM)
    sink_rows = jnp.broadcast_to(p["sinks"].reshape(kv_heads, 1, group, 1), (kv_heads, 1, group, WINDOW)).reshape(kv_heads, 1, group * WINDOW)
    o_b, lse_b = _swa_fwd("b_attn", qb, kv, gqb, gkb, sink_rows, bias, group)
    w_out_b, = comm.weights(["w_out_b"], o_b)
    h3 = _matmul_res("b_outproj", o_b, w_out_b, h2)
    (dy, loss_tile), mlp1 = mlp_fwd("mlp1", h3, p["g_mlp"][1], 1, last=True)

    dh3, dg_mlp1 = mlp_bwd("mlp1", mlp1, 1, dy)
    do_b = _matmul_nt("b_do", dh3, w_out_b)
    dw_out_b = _matmul_tn("b_dwout", o_b, dh3)
    dqb, dkv, grads["gq_b"], dgk_b, dsink, dbias = _swa_bwd(
        "b_attn_bwd", qb, kv, gqb, gkb, sink_rows, bias, lse_b, do_b, group)
    grads["gk_b"] = dgk_b
    grads["sinks"] = dsink[:, :, 0].reshape(1, n_heads)
    dbias = dbias.reshape(kv_heads, 2 * WINDOW, group, WINDOW).transpose(0, 2, 1, 3)
    grads["rel_bias"] = _bias_reduce("b_dbias", dbias.reshape(n_heads, WINDOW * 2 * WINDOW)).T
    dw_q_b = _matmul_tn("b_dwq", hn_q, dqb)
    dh2, dg_attn1 = _matmul_nt_rmsbwd("b_dhq", dqb, w_q_b, h2, p["g_attn"][1], dh3)
    dw_kv = _matmul_tn("kv_dw", hn_kv, dkv)
    sent = comm.send_grads("attn_b", {"w_out_b": dw_out_b, "w_q_b": dw_q_b, "w_kv": dw_kv})
    dh2, dg_kv = _matmul_nt_rmsbwd("kv_dh", dkv, w_kv, h2, p["g_kv"], dh2, after=sent)
    grads["g_kv"] = dg_kv
    dh1, dg_mlp0 = mlp_bwd("mlp0", mlp0, 0, dh2)
    grads["g_mlp"] = (dg_mlp0, dg_mlp1)

    do_a = _matmul_nt("a_do", dh1, w_out_a)
    dw_out_a = _matmul_tn("a_dwout", o_a, dh1)
    sent = comm.send_grads("attn_a_out", {"w_out_a": dw_out_a})
    dq, dk, dv, dc, grads["gq_a"], grads["gk_a"] = _fox_bwd(
        "a_attn_bwd", proj, c, crow, p["gq_a"], p["gk_a"], lse_a, do_a, n_heads, after=sent)
    dfl, db_f = _gate_bwd("a_gate_bwd", proj, b_pad, dc, n_heads, gate_col)
    grads["b_f"] = db_f
    dproj = jnp.concatenate([dq, dk, dv, dfl], axis=1)
    dw_in = _matmul_tn("a_dwin", dproj, xn1, tk=640)
    sent = comm.send_grads("attn_a_in", {"w_in_a": dw_in})
    grad_x, dg_attn0 = _matmul_nt_rmsbwd("a_dx", dproj, w_in, x, p["g_attn"][0], dh1, w_rows=True, after=sent)
    grads["g_attn"] = (dg_attn0, dg_attn1)
    return loss_tile, grad_x, grads


EVERYONE = (1, 2, 3, 4, 5, 6, 7)
SAME_CORE = (1, 2, 4, 6)
OTHER_CHIPS = (2, 4, 6)
RELAY_COLLECTIVE_ID = 0


class _InFlight:
    def __init__(self, scatter, ks, send_sems, recv_sems, srcs, lands, token):
        self.scatter, self.ks, self.send_sems, self.recv_sems = scatter, ks, send_sems, recv_sems
        self.srcs, self.lands, self.token = list(srcs), list(lands), token


def _mesh_peers(ks=EVERYONE):
    x, y, c = lax.axis_index("x"), lax.axis_index("y"), lax.axis_index("c")
    peers = []
    for k in ks:
        px, py, pc = x ^ ((k >> 2) & 1), y ^ ((k >> 1) & 1), c ^ (k & 1)
        peers.append(((px, py, pc), 4 * px + 2 * py + pc))
    return 4 * x + 2 * y + c, peers


_HBM_SPEC = pl.BlockSpec(memory_space=pltpu.HBM)
_SEM_SPEC = pl.BlockSpec(memory_space=pltpu.SEMAPHORE)
_SIDE_EFFECT = pltpu.SideEffectType.DATAFLOW_SIDE_EFFECTING


def _exchange_start(name, arrays, scatter, collective_id, ks=EVERYONE):
    n = len(arrays)
    me, _ = _mesh_peers()
    lands = []
    for a in arrays:
        own = lax.dynamic_index_in_dim(a, me, 0, keepdims=False) if scatter else a
        shape = a.shape if scatter else (N_DEV,) + a.shape
        lands.append(lax.dynamic_update_index_in_dim(lax.empty(shape, a.dtype), own, me, 0))

    def body(*refs):
        src, land = refs[:n], refs[n:2 * n]
        send_sems, recv_sems, token = refs[2 * n], refs[2 * n + 1], refs[-1]
        pos, peers = _mesh_peers(ks)
        barrier = pltpu.get_barrier_semaphore()
        for peer, _ in peers:
            pl.semaphore_signal(barrier, inc=1, device_id=peer, device_id_type=pl.DeviceIdType.MESH)
        pl.semaphore_wait(barrier, len(peers))
        for a in range(n):
            for k, (peer, peer_pos) in enumerate(peers):
                pltpu.make_async_remote_copy(
                    src_ref=src[a].at[peer_pos] if scatter else src[a], dst_ref=land[a].at[pos],
                    send_sem=send_sems.at[a * len(ks) + k], recv_sem=recv_sems.at[a * len(ks) + k],
                    device_id=peer, device_id_type=pl.DeviceIdType.MESH).start()
        token[...] = jnp.zeros_like(token)

    operands = [pltpu.with_memory_space_constraint(a, pltpu.HBM) for a in list(arrays) + lands]
    outs = pl.pallas_call(
        body, name=name,
        out_shape=(pltpu.SemaphoreType.DMA((n * len(ks),)), pltpu.SemaphoreType.DMA((n * len(ks),)),
                   *[pltpu.HBM(a.shape, a.dtype) for a in operands], _sds((8, LANES), F32)),
        in_specs=[_HBM_SPEC] * (2 * n),
        out_specs=(_SEM_SPEC, _SEM_SPEC, *[_HBM_SPEC] * (2 * n), pl.BlockSpec(memory_space=pltpu.VMEM)),
        input_output_aliases={i: 2 + i for i in range(2 * n)},
        compiler_params=pltpu.CompilerParams(has_side_effects=_SIDE_EFFECT, collective_id=collective_id),
    )(*operands)
    return _InFlight(scatter, ks, outs[0], outs[1], outs[2:2 + n], outs[2 + n:2 + 2 * n], outs[-1])


def _exchange_wait(name, flight, which, after):
    m = len(which)
    scatter, ks = flight.scatter, flight.ks

    def body(*refs):
        src, land = refs[:m], refs[m:2 * m]
        send_sems, recv_sems = refs[2 * m], refs[2 * m + 1]
        _, peers = _mesh_peers(ks)
        for i, a in enumerate(which):
            for k, (peer, peer_pos) in enumerate(peers):
                cp = pltpu.make_async_remote_copy(
                    src_ref=src[i].at[peer_pos] if scatter else src[i], dst_ref=land[i].at[peer_pos],
                    send_sem=send_sems.at[a * len(ks) + k], recv_sem=recv_sems.at[a * len(ks) + k],
                    device_id=peer, device_id_type=pl.DeviceIdType.MESH)
                cp.wait_send()
                cp.wait_recv()

    operands = [flight.srcs[a] for a in which] + [flight.lands[a] for a in which]
    outs = pl.pallas_call(
        body, name=name, out_shape=tuple(pltpu.HBM(a.shape, a.dtype) for a in operands),
        in_specs=[_HBM_SPEC] * (2 * m) + [_SEM_SPEC, _SEM_SPEC, pl.BlockSpec(memory_space=pl.ANY)],
        out_specs=tuple([_HBM_SPEC] * (2 * m)), input_output_aliases={i: i for i in range(2 * m)},
        compiler_params=pltpu.CompilerParams(has_side_effects=_SIDE_EFFECT),
    )(*operands, flight.send_sems, flight.recv_sems, after)
    return list(outs[m:])


def _relay_start(name, lands):
    n = len(lands)

    def body(*refs):
        land, send_sems, recv_sems, token = refs[:n], refs[n], refs[n + 1], refs[-1]
        _, peers = _mesh_peers(OTHER_CHIPS)
        sibling = (lax.axis_index("x"), lax.axis_index("y"), 1 - lax.axis_index("c"))
        barrier = pltpu.get_barrier_semaphore()
        pl.semaphore_signal(barrier, inc=1, device_id=sibling, device_id_type=pl.DeviceIdType.MESH)
        pl.semaphore_wait(barrier, 1)
        for a in range(n):
            for k, (_, peer_pos) in enumerate(peers):
                pltpu.make_async_remote_copy(
                    src_ref=land[a].at[peer_pos], dst_ref=land[a].at[peer_pos],
                    send_sem=send_sems.at[a * len(peers) + k], recv_sem=recv_sems.at[a * len(peers) + k],
                    device_id=sibling, device_id_type=pl.DeviceIdType.MESH).start()
        token[...] = jnp.zeros_like(token)

    count = n * len(OTHER_CHIPS)
    outs = pl.pallas_call(
        body, name=name,
        out_shape=(pltpu.SemaphoreType.DMA((count,)), pltpu.SemaphoreType.DMA((count,)),
                   *[pltpu.HBM(a.shape, a.dtype) for a in lands], _sds((8, LANES), F32)),
        in_specs=[_HBM_SPEC] * n,
        out_specs=(_SEM_SPEC, _SEM_SPEC, *[_HBM_SPEC] * n, pl.BlockSpec(memory_space=pltpu.VMEM)),
        input_output_aliases={i: 2 + i for i in range(n)},
        compiler_params=pltpu.CompilerParams(has_side_effects=_SIDE_EFFECT, collective_id=RELAY_COLLECTIVE_ID),
    )(*[pltpu.with_memory_space_constraint(a, pltpu.HBM) for a in lands])
    return _InFlight(False, OTHER_CHIPS, outs[0], outs[1], [], outs[2:2 + n], outs[-1])


def _relay_wait(name, flight, which, after):
    m = len(which)

    def body(*refs):
        land, send_sems, recv_sems = refs[:m], refs[m], refs[m + 1]
        _, peers = _mesh_peers(OTHER_CHIPS)
        sibling = (lax.axis_index("x"), lax.axis_index("y"), 1 - lax.axis_index("c"))
        for i, a in enumerate(which):
            for k, (_, peer_pos) in enumerate(peers):
                cp = pltpu.make_async_remote_copy(
                    src_ref=land[i].at[peer_pos], dst_ref=land[i].at[peer_pos ^ 1],
                    send_sem=send_sems.at[a * len(peers) + k], recv_sem=recv_sems.at[a * len(peers) + k],
                    device_id=sibling, device_id_type=pl.DeviceIdType.MESH)
                cp.wait_send()
                cp.wait_recv()

    operands = [flight.lands[a] for a in which]
    outs = pl.pallas_call(
        body, name=name, out_shape=tuple(pltpu.HBM(a.shape, a.dtype) for a in operands),
        in_specs=[_HBM_SPEC] * m + [_SEM_SPEC, _SEM_SPEC, pl.BlockSpec(memory_space=pl.ANY)],
        out_specs=tuple([_HBM_SPEC] * m), input_output_aliases={i: i for i in range(m)},
        compiler_params=pltpu.CompilerParams(has_side_effects=_SIDE_EFFECT),
    )(*operands, flight.send_sems, flight.recv_sems, after)
    return list(outs)


def _sum_parts(p_ref):
    g = p_ref[0].astype(F32)
    for dev in range(1, N_DEV):
        g = g + p_ref[dev].astype(F32)
    return g


def _adam_update(g, w, m, v):
    m_new = ADAM_B1 * m + (1.0 - ADAM_B1) * g
    v_new = ADAM_B2 * v + (1.0 - ADAM_B2) * jnp.square(g)
    m_hat = m_new / (1.0 - ADAM_B1 ** ADAM_STEP)
    v_hat = v_new / (1.0 - ADAM_B2 ** ADAM_STEP)
    return -ADAM_LR * (m_hat / (jnp.sqrt(v_hat) + ADAM_EPS) + ADAM_WD * w), m_new, v_new


def _adamw(name, parts, w, m, v, layer=None, into=None):
    r, c = w.shape[-2:]
    tr = 256 if r % 256 == 0 else r
    n_into = 0 if into is None else len(into)

    def body(p_ref, w_ref, m_ref, v_ref, *refs):
        g_ref, d_ref, mo_ref, vo_ref = refs[n_into:]
        g = _sum_parts(p_ref)
        g_ref[...] = g
        d_ref[...], mo_ref[...], vo_ref[...] = _adam_update(g, w_ref[...], m_ref[...], v_ref[...])

    if layer is None:
        blk = pl.BlockSpec((tr, c), lambda i: (i, 0))
    else:
        blk = pl.BlockSpec((None, tr, c), lambda i: (layer, i, 0))
    return pl.pallas_call(
        body, name=name, grid=(r // tr,),
        in_specs=[pl.BlockSpec((N_DEV, tr, c), lambda i: (0, i, 0)), blk, blk, blk] + [pl.BlockSpec(memory_space=pl.ANY)] * n_into,
        out_specs=[blk] * 4, out_shape=[_sds(w.shape, F32)] * 4,
        input_output_aliases={4 + i: i for i in range(n_into)}, compiler_params=_params(1),
    )(parts, w, m, v, *(into or ()))


SMALL_PACK_ROWS = 16
LOSS_ROW = 11


def _small_rows(grads, loss_tile):
    return [(0, 1, grads["g_attn"][0]), (1, 1, grads["g_attn"][1]), (2, 1, grads["g_mlp"][0]), (3, 1, grads["g_mlp"][1]),
            (4, 1, grads["g_kv"]), (5, 1, grads["b_f"]), (6, 1, grads["gq_a"]), (7, 1, grads["gk_a"]), (8, 1, grads["gk_b"]),
            (9, 1, grads["gq_b"]), (10, 1, grads["sinks"]), (LOSS_ROW, 1, loss_tile)]


SMALL_ROWS = {"g_attn": (0, 2), "g_mlp": (2, 2), "g_kv": (4, 1), "b_f": (5, 1), "gq_a": (6, 1), "gk_a": (7, 1),
              "gk_b": (8, 1), "gq_b": (9, 1), "sinks": (10, 1)}


def _pack_small(name, pieces, d):
    def body(*refs):
        out = refs[-1]
        out[...] = jnp.zeros_like(out)
        for (row, rows, _), ref in zip(pieces, refs[:-1]):
            out[row:row + rows, 0:ref.shape[1]] = ref[0:rows, :]

    return pl.pallas_call(body, name=name, out_shape=_sds((SMALL_PACK_ROWS, d), F32), compiler_params=_params(0))(
        *[piece for _, _, piece in pieces])


def _adamw_small(name, parts, parts_rel_bias, w, m, v):
    def body(*refs):
        ins, outs = refs[2:2 + 3 * len(SMALL)], refs[2 + 3 * len(SMALL):]
        pack, rel = _sum_parts(refs[0]), _sum_parts(refs[1])
        for i, k in enumerate(SMALL):
            w_ref, m_ref, v_ref = ins[3 * i:3 * i + 3]
            if k == "rel_bias":
                g = rel
            else:
                row, rows = SMALL_ROWS[k]
                g = pack[row:row + rows, 0:w_ref.shape[1]]
            outs[4 * i][...] = g
            outs[4 * i + 1][...], outs[4 * i + 2][...], outs[4 * i + 3][...] = _adam_update(g, w_ref[...], m_ref[...], v_ref[...])
        outs[-1][...] = pack[LOSS_ROW:LOSS_ROW + 1, 0:LANES]

    operands = [parts, parts_rel_bias] + [t[k] for k in SMALL for t in (w, m, v)]
    out_shape = [_sds(w[k].shape, F32) for k in SMALL for _ in range(4)] + [_sds((1, LANES), F32)]
    outs = pl.pallas_call(body, name=name, out_shape=out_shape, compiler_params=_params(0))(*operands)
    return {k: outs[4 * i:4 * i + 4] for i, k in enumerate(SMALL)}, outs[-1]


class _Comm:
    ORDER = ("w_in_a", "w_out_a", "w_up0", "w_down0", "w_kv", "w_q_b", "w_out_b", "w_up1", "w_down1")

    def __init__(self, shards, d, n_in):
        self.d, self.n_in = d, n_in
        self.ids = iter(range(RELAY_COLLECTIVE_ID + 1, RELAY_COLLECTIVE_ID + 16))
        self.flight = _exchange_start("gather_start", [shards[n].astype(BF16) for n in self.ORDER], False, next(self.ids), SAME_CORE)
        self.relays, self.sent = {}, []

    def prefetch(self, names, after):
        which = [self.ORDER.index(n) for n in names]
        landed = _exchange_wait(f"gather_wait_{names[0]}", self.flight, which, self.flight.token if after is None else after)
        relay = _relay_start(f"gather_relay_{names[0]}", landed)
        for n in names:
            self.relays[n] = (relay, names)
        return relay.token

    def weights(self, names, after):
        relay, group = self.relays[names[0]]
        landed = _relay_wait(f"gather_relay_wait_{names[0]}", relay, [group.index(n) for n in names],
                             relay.token if after is None else after)
        return [self._whole(n, g) for n, g in zip(names, landed)]

    def _whole(self, name, g):
        if name == "w_in_a":
            return _join_row_blocks("w_in_join", g, -(-self.n_in // LANES) * LANES)
        if name.startswith("w_up"):
            return g
        return g.reshape(-1, g.shape[-1])

    def _chunks(self, name, g):
        if name == "w_in_a":
            return _split_row_blocks("dw_in_split", g, N_DEV, self.n_in // N_DEV)
        if name.startswith("w_up"):
            return g
        return g.reshape(N_DEV, g.shape[0] // N_DEV, g.shape[1])

    def send_grads(self, tag, partials):
        names = list(partials)
        flight = _exchange_start(f"scatter_start_{tag}", [self._chunks(n, partials[n]) for n in names], True, next(self.ids))
        self.sent.append((tag, flight, names))
        return flight.token

    def received(self, index, after):
        tag, flight, names = self.sent[index]
        landed = _exchange_wait(f"scatter_wait_{tag}", flight, list(range(len(names))), after)
        return dict(zip(names, landed))


def kernel(x, g_attn, g_mlp, w_in_a, b_f, gq_a, gk_a, w_out_a, g_kv, w_kv, gk_b, w_q_b, gq_b, sinks, rel_bias, w_out_b, w_up, w_down, loss_target, m_g_attn, m_g_mlp, m_w_in_a, m_b_f, m_gq_a, m_gk_a, m_w_out_a, m_g_kv, m_w_kv, m_gk_b, m_w_q_b, m_gq_b, m_sinks, m_rel_bias, m_w_out_b, m_w_up, m_w_down, v_g_attn, v_g_mlp, v_w_in_a, v_b_f, v_gq_a, v_gk_a, v_w_out_a, v_g_kv, v_w_kv, v_gk_b, v_w_q_b, v_gq_b, v_sinks, v_rel_bias, v_w_out_b, v_w_up, v_w_down):
    w = dict(g_attn=g_attn, g_mlp=g_mlp, w_in_a=w_in_a, b_f=b_f, gq_a=gq_a, gk_a=gk_a, w_out_a=w_out_a, g_kv=g_kv,
             w_kv=w_kv, gk_b=gk_b, w_q_b=w_q_b, gq_b=gq_b, sinks=sinks, rel_bias=rel_bias, w_out_b=w_out_b,
             w_up=w_up, w_down=w_down)
    mom = dict(g_attn=m_g_attn, g_mlp=m_g_mlp, w_in_a=m_w_in_a, b_f=m_b_f, gq_a=m_gq_a, gk_a=m_gk_a, w_out_a=m_w_out_a,
               g_kv=m_g_kv, w_kv=m_w_kv, gk_b=m_gk_b, w_q_b=m_w_q_b, gq_b=m_gq_b, sinks=m_sinks, rel_bias=m_rel_bias,
               w_out_b=m_w_out_b, w_up=m_w_up, w_down=m_w_down)
    var = dict(g_attn=v_g_attn, g_mlp=v_g_mlp, w_in_a=v_w_in_a, b_f=v_b_f, gq_a=v_gq_a, gk_a=v_gk_a, w_out_a=v_w_out_a,
               g_kv=v_g_kv, w_kv=v_w_kv, gk_b=v_gk_b, w_q_b=v_w_q_b, gq_b=v_gq_b, sinks=v_sinks, rel_bias=v_rel_bias,
               w_out_b=v_w_out_b, w_up=v_w_up, w_down=v_w_down)
    d = x.shape[2]
    where = {"w_in_a": ("w_in_a", 0), "w_out_a": ("w_out_a", 0), "w_kv": ("w_kv", None), "w_q_b": ("w_q_b", 0),
             "w_out_b": ("w_out_b", 0), "w_up0": ("w_up", 0), "w_up1": ("w_up", 1), "w_down0": ("w_down", 0),
             "w_down1": ("w_down", 1)}
    flip = lambda tree: {**tree, "w_in_a": jnp.swapaxes(tree["w_in_a"], 1, 2)}
    w, mom, var = flip(w), flip(mom), flip(var)
    shards = {n: (w[k] if layer is None else w[k][layer]) for n, (k, layer) in where.items()}
    comm = _Comm(shards, d, w_in_a.shape[2] * N_DEV)
    loss_tile, grad_x, grads = _local_step(x[0], loss_target[0], {k: w[k] for k in SMALL}, comm)

    small_flight = _exchange_start(
        "gather_small_grads", [_pack_small("pack_small", _small_rows(grads, loss_tile), d), grads["rel_bias"]], False, next(comm.ids))
    res, after = {}, small_flight.token
    for index in range(len(comm.sent)):
        for n, parts in comm.received(index, after).items():
            k, layer = where[n]
            res[k] = _adamw(f"adam_{n}", parts, w[k], mom[k], var[k], layer, res.get(k))
            after = res[k][0]
    as_rows = lambda tree: {k: tree[k] if tree[k].ndim == 2 else tree[k].reshape(1, -1) for k in SMALL}
    small, loss_row = _adamw_small("adam_small", *_exchange_wait("gather_small_wait", small_flight, [0, 1], after),
                                   as_rows(w), as_rows(mom), as_rows(var))
    loss = loss_row[0, 0]
    for k in SMALL:
        res[k] = [a.reshape(w[k].shape) for a in small[k]]
    res["w_in_a"] = [jnp.swapaxes(a, 1, 2) for a in res["w_in_a"]]

    outs = [loss, grad_x[None]]
    for i in range(4):
        outs.extend(res[k][i] for k in WEIGHTS)
    return tuple(outs)
```

```python
import numpy as np
import jax
import jax.numpy as jnp
from jax import lax
from jax.experimental import pallas as pl
from jax.experimental.pallas import tpu as pltpu

F32 = jnp.float32
BF16 = jnp.bfloat16

N_DEV = 8
HEAD_DIM = 64
WINDOW = 128
N_BUCKETS = 32
REL_MAX_DIST = 128
NORM_EPS = 1e-6
NEG = -1e30
LANES = 128
VMEM_LIMIT = 56 * 1024 * 1024

ADAM_LR = 0.001
ADAM_B1 = 0.9
ADAM_B2 = 0.999
ADAM_EPS = 1e-08
ADAM_WD = 0.01
ADAM_STEP = 10

SMALL = ("g_attn", "g_mlp", "b_f", "gq_a", "gk_a", "g_kv", "gk_b", "gq_b", "sinks", "rel_bias")
WEIGHTS = ("g_attn", "g_mlp", "w_in_a", "b_f", "gq_a", "gk_a", "w_out_a", "g_kv", "w_kv", "gk_b",
           "w_q_b", "gq_b", "sinks", "rel_bias", "w_out_b", "w_up", "w_down")


def _params(n_grid):
    return pltpu.CompilerParams(dimension_semantics=("arbitrary",) * n_grid, vmem_limit_bytes=VMEM_LIMIT)


def _sds(shape, dtype):
    return jax.ShapeDtypeStruct(tuple(shape), dtype)


def _after_operand(after):
    if after is None:
        return [], []
    return [pl.BlockSpec((8, LANES), lambda *_: (0, 0))], [after]


def _rms(x, g):
    return (x * lax.rsqrt(jnp.mean(x * x, axis=-1, keepdims=True) + NORM_EPS)) * g


def _dot_nt(a, b):
    return lax.dot_general(a, b, (((1,), (1,)), ((), ())), preferred_element_type=F32)


def _dot_tn(a, b):
    return lax.dot_general(a, b, (((0,), (0,)), ((), ())), preferred_element_type=F32)


def _dot(a, b):
    return jnp.dot(a, b, preferred_element_type=F32)


def _dot_exact(a, b):
    return jnp.dot(a, b, preferred_element_type=F32, precision=lax.Precision.HIGHEST)


def _norm_matmul(name, x, g, w, *, tn=None, relu2=False, w_rows=False, after=None):
    t, d = x.shape
    blocked = w.ndim == 3
    per_step = 2 if blocked else 1
    if blocked:
        tn = per_step * w.shape[2]
        n = w.shape[0] * w.shape[2]
        w_spec = pl.BlockSpec((per_step, d, w.shape[2]), lambda i, j: (j, 0, 0))
    elif w_rows:
        n = w.shape[0]
        w_spec = pl.BlockSpec((tn, d), lambda i, j: (j, 0))
    else:
        n = w.shape[1]
        w_spec = pl.BlockSpec((d, tn), lambda i, j: (0, j))
    tm = min(1024, t)

    def body(x_ref, g_ref, w_ref, *rest):
        y_ref, xn_ref = rest[-2:]

        @pl.when(pl.program_id(1) == 0)
        def _():
            xn_ref[...] = _rms(x_ref[...], g_ref[...]).astype(BF16)

        for b in range(per_step):
            cols = slice(b * (tn // per_step), (b + 1) * (tn // per_step)) if blocked else slice(None)
            wb = w_ref[b] if blocked else w_ref[...]
            y = _dot_nt(xn_ref[...], wb) if w_rows else _dot(xn_ref[...], wb)
            y_ref[:, cols] = jnp.square(jnp.maximum(y, 0.0)).astype(BF16) if relu2 else y

    extra_specs, extra = _after_operand(after)
    out_shape = [_sds((t, n), BF16 if relu2 else F32), _sds((t, d), BF16)]
    out_specs = [pl.BlockSpec((tm, tn), lambda i, j: (i, j)), pl.BlockSpec((tm, d), lambda i, j: (i, 0))]
    return pl.pallas_call(
        body, name=name, grid=(t // tm, n // tn),
        in_specs=[pl.BlockSpec((tm, d), lambda i, j: (i, 0)), pl.BlockSpec((1, d), lambda i, j: (0, 0)), w_spec] + extra_specs,
        out_specs=out_specs, out_shape=out_shape, compiler_params=_params(2),
    )(x, g.reshape(1, d), w, *extra)


def _matmul_res(name, a, w, res, *, tn=512, after=None):
    t, k = a.shape
    n = w.shape[1]
    tm = min(1024, t)

    def body(a_ref, w_ref, r_ref, *rest):
        rest[-1][...] = r_ref[...] + _dot(a_ref[...], w_ref[...])

    extra_specs, extra = _after_operand(after)
    return pl.pallas_call(
        body, name=name, grid=(t // tm, n // tn),
        in_specs=[pl.BlockSpec((tm, k), lambda i, j: (i, 0)), pl.BlockSpec((k, tn), lambda i, j: (0, j)),
                  pl.BlockSpec((tm, tn), lambda i, j: (i, j))] + extra_specs,
        out_specs=pl.BlockSpec((tm, tn), lambda i, j: (i, j)), out_shape=_sds((t, n), F32),
        compiler_params=_params(2),
    )(a, w, res, *extra)


def _matmul_nt(name, dy, w, *, a=None, tk=1024):
    t, n = dy.shape
    k = w.shape[0]
    tm = min(1024, t)

    def body(dy_ref, w_ref, *rest):
        o_ref = rest[-1]
        r = _dot_nt(dy_ref[...].astype(BF16), w_ref[...])
        if a is not None:
            r = r * (2.0 * jnp.sqrt(rest[0][...].astype(F32)))
        o_ref[...] = r.astype(BF16)

    in_specs = [pl.BlockSpec((tm, n), lambda i, j: (i, 0)), pl.BlockSpec((tk, n), lambda i, j: (j, 0))]
    args = [dy, w]
    if a is not None:
        in_specs.append(pl.BlockSpec((tm, tk), lambda i, j: (i, j)))
        args.append(a)
    return pl.pallas_call(
        body, name=name, grid=(t // tm, k // tk), in_specs=in_specs,
        out_specs=pl.BlockSpec((tm, tk), lambda i, j: (i, j)), out_shape=_sds((t, k), BF16),
        compiler_params=_params(2),
    )(*args)


def _matmul_nt_rmsbwd(name, dy, w, x, g, dres, *, w_rows=False, after=None):
    pieces = isinstance(dy, tuple)
    dys = list(dy) if pieces else [dy]
    t = dres.shape[0]
    blocked = w.ndim == 3
    d = w.shape[1] if blocked or w_rows else w.shape[0]
    tm = min(512, t)

    def body(*refs):
        dy_ref = refs[0]
        w_ref, x_ref, g_ref, r_ref = refs[len(dys):len(dys) + 4]
        dx_ref, dg_ref = refs[-2:]
        if pieces:
            n_stack, _, k1 = dys[0].shape
            dxn = _dot(refs[1][...], w_ref[n_stack * k1:, :])
            for j in range(n_stack):
                dxn += _dot(dy_ref[j], w_ref[j * k1:(j + 1) * k1, :])
        elif blocked:
            kb = w.shape[2]
            dxn = _dot_nt(dy_ref[:, 0:kb].astype(BF16), w_ref[0])
            for j in range(1, w.shape[0]):
                dxn += _dot_nt(dy_ref[:, j * kb:(j + 1) * kb].astype(BF16), w_ref[j])
        elif w_rows:
            dxn = _dot(dy_ref[...].astype(BF16), w_ref[...])
        else:
            dxn = _dot_nt(dy_ref[...].astype(BF16), w_ref[...])
        _, vjp = jax.vjp(_rms, x_ref[...], g_ref[...])
        dx, dg = vjp(dxn)
        dx_ref[...] = r_ref[...] + dx

        @pl.when(pl.program_id(0) == 0)
        def _():
            dg_ref[...] = jnp.zeros_like(dg_ref)

        dg_ref[...] += dg

    row = lambda i: (i, 0)
    fixed = lambda i: (0, 0)
    extra_specs, extra = _after_operand(after)
    if pieces:
        dy_specs = [pl.BlockSpec((dys[0].shape[0], tm, dys[0].shape[2]), lambda i: (0, i, 0)),
                    pl.BlockSpec((tm, dys[1].shape[1]), row)]
    else:
        dy_specs = [pl.BlockSpec((tm, dy.shape[1]), row)]
    return pl.pallas_call(
        body, name=name, grid=(t // tm,),
        in_specs=dy_specs + [pl.BlockSpec(w.shape, (lambda i: (0, 0, 0)) if blocked else fixed),
                             pl.BlockSpec((tm, d), row), pl.BlockSpec((1, d), fixed), pl.BlockSpec((tm, d), row)] + extra_specs,
        out_specs=[pl.BlockSpec((tm, d), row), pl.BlockSpec((1, d), fixed)],
        out_shape=[_sds((t, d), F32), _sds((1, d), F32)], compiler_params=_params(1),
    )(*dys, w, x, g.reshape(1, d), dres, *extra)


def _matmul_tn(name, a, b, *, tk=1024, tn=1024, col_blocks=None):
    stacked = a.ndim == 3
    t, k1 = a.shape[-2:]
    k = a.shape[0] * k1 if stacked else k1
    n = b.shape[1]
    tk = min(tk, k1)
    per = k1 // tk
    a_spec = (pl.BlockSpec((None, t, tk), lambda i, j: (i // per, 0, i % per)) if stacked
              else pl.BlockSpec((t, tk), lambda i, j: (0, i)))
    if col_blocks:
        tn = n // col_blocks
        out_spec, out_shape = pl.BlockSpec((None, tk, tn), lambda i, j: (j, i, 0)), _sds((col_blocks, k, tn), BF16)
    else:
        tn = min(tn, n)
        out_spec, out_shape = pl.BlockSpec((tk, tn), lambda i, j: (i, j)), _sds((k, n), BF16)

    def body(a_ref, b_ref, o_ref):
        o_ref[...] = _dot_tn(a_ref[...].astype(BF16), b_ref[...].astype(BF16)).astype(BF16)

    return pl.pallas_call(
        body, name=name, grid=(k // tk, n // tn),
        in_specs=[a_spec, pl.BlockSpec((t, tn), lambda i, j: (0, j))],
        out_specs=out_spec, out_shape=out_shape, compiler_params=_params(2),
    )(a, b)


def _join_row_blocks(name, blocks, rows):
    b, r, c = blocks.shape
    tc = min(256, c)

    def body(g_ref, o_ref):
        o_ref[...] = jnp.zeros_like(o_ref)
        for j in range(b):
            o_ref[r * j:r * (j + 1), :] = g_ref[j]

    return pl.pallas_call(
        body, name=name, grid=(c // tc,), in_specs=[pl.BlockSpec((b, r, tc), lambda i: (0, 0, i))],
        out_specs=pl.BlockSpec((rows, tc), lambda i: (0, i)), out_shape=_sds((rows, c), blocks.dtype),
        compiler_params=_params(1),
    )(blocks)


def _split_row_blocks(name, mats, b, r):
    c = mats[0].shape[1]
    tc = min(256, c)

    def body(*refs):
        o_ref = refs[-1]
        for j in range(b):
            first = 0
            for m_ref in refs[:-1]:
                lo, hi = max(r * j, first), min(r * (j + 1), first + m_ref.shape[0])
                if lo < hi:
                    o_ref[j, lo - r * j:hi - r * j, :] = m_ref[lo - first:hi - first, :]
                first += m_ref.shape[0]

    return pl.pallas_call(
        body, name=name, grid=(c // tc,), in_specs=[pl.BlockSpec((m.shape[0], tc), lambda i: (0, i)) for m in mats],
        out_specs=pl.BlockSpec((b, r, tc), lambda i: (0, 0, i)), out_shape=_sds((b, r, c), mats[0].dtype),
        compiler_params=_params(1),
    )(*mats)


def _matmul_res_loss(name, a, w, res, target, *, tn=512):
    t, k = a.shape
    n = w.shape[1]
    tm = min(1024, t)

    def body(a_ref, w_ref, r_ref, t_ref, dy_ref, l_ref):
        e = r_ref[...] + _dot(a_ref[...], w_ref[...]) - t_ref[...]
        dy_ref[...] = e * (1.0 / n)

        @pl.when((pl.program_id(0) == 0) & (pl.program_id(1) == 0))
        def _():
            l_ref[...] = jnp.zeros_like(l_ref)

        l_ref[...] += (0.5 / n) * jnp.sum(e * e)

    tile = pl.BlockSpec((tm, tn), lambda i, j: (i, j))
    return pl.pallas_call(
        body, name=name, grid=(t // tm, n // tn),
        in_specs=[pl.BlockSpec((tm, k), lambda i, j: (i, 0)), pl.BlockSpec((k, tn), lambda i, j: (0, j)), tile, tile],
        out_specs=[tile, pl.BlockSpec((8, LANES), lambda i, j: (0, 0))],
        out_shape=[_sds((t, n), F32), _sds((8, LANES), F32)], compiler_params=_params(2),
    )(a, w, res, target)


def _gate_fwd(name, proj, b_pad, n_heads, gate_col, after=None):
    t = proj.shape[0]
    tb = min(256, t)
    tri = jnp.asarray(np.tril(np.ones((tb, tb), np.float32)))
    extra_specs, extra = _after_operand(after)

    def body(p_ref, b_ref, tri_ref, *rest):
        c_ref, carry = rest[-2:]

        @pl.when(pl.program_id(0) == 0)
        def _():
            carry[...] = jnp.zeros_like(carry)

        lane = lax.broadcasted_iota(jnp.int32, (tb, LANES), 1)
        lf = jnp.where(lane < n_heads, jax.nn.log_sigmoid(p_ref[...] + b_ref[...]), 0.0)
        c = _dot_exact(tri_ref[...], lf) + carry[0:1, :]
        c_ref[...] = c
        carry[...] = jnp.broadcast_to(c[tb - 1:tb, :], carry.shape)

    return pl.pallas_call(
        body, name=name, grid=(t // tb,),
        in_specs=[pl.BlockSpec((tb, LANES), lambda i: (i, gate_col)), pl.BlockSpec((1, LANES), lambda i: (0, 0)),
                  pl.BlockSpec((tb, tb), lambda i: (0, 0))] + extra_specs,
        out_specs=pl.BlockSpec((tb, LANES), lambda i: (i, 0)), out_shape=_sds((t, LANES), F32),
        scratch_shapes=[pltpu.VMEM((8, LANES), F32)], compiler_params=_params(1),
    )(proj, b_pad, tri, *extra)


def _gate_bwd(name, proj, b_pad, dc, n_heads, gate_col):
    t = proj.shape[0]
    tb = min(256, t)
    nb = t // tb
    triu = jnp.asarray(np.triu(np.ones((tb, tb), np.float32)))

    def body(p_ref, b_ref, dc_ref, tri_ref, df_ref, db_ref, carry):
        @pl.when(pl.program_id(0) == 0)
        def _():
            carry[...] = jnp.zeros_like(carry)
            db_ref[...] = jnp.zeros_like(db_ref)

        dcv = dc_ref[...]
        dlf = _dot_exact(tri_ref[...], dcv) + carry[0:1, :]
        carry[...] = jnp.broadcast_to(dlf[0:1, :], carry.shape)
        lane = lax.broadcasted_iota(jnp.int32, (tb, LANES), 1)
        z = p_ref[...] + b_ref[...]
        df = jnp.where(lane < n_heads, dlf / (1.0 + jnp.exp(z)), 0.0)
        df_ref[...] = df.astype(BF16)
        db_ref[...] += jnp.sum(df, axis=0, keepdims=True)

    return pl.pallas_call(
        body, name=name, grid=(nb,),
        in_specs=[pl.BlockSpec((tb, LANES), lambda i: (nb - 1 - i, gate_col)), pl.BlockSpec((1, LANES), lambda i: (0, 0)),
                  pl.BlockSpec((tb, LANES), lambda i: (nb - 1 - i, 0)), pl.BlockSpec((tb, tb), lambda i: (0, 0))],
        out_specs=[pl.BlockSpec((tb, LANES), lambda i: (nb - 1 - i, 0)), pl.BlockSpec((1, LANES), lambda i: (0, 0))],
        out_shape=[_sds((t, LANES), BF16), _sds((1, LANES), F32)],
        scratch_shapes=[pltpu.VMEM((8, LANES), F32)], compiler_params=_params(1),
    )(proj, b_pad, dc, triu)


def _qhead(qp, g):
    return _rms(qp, g) * (HEAD_DIM ** -0.5)


def _column(mat, idx):
    lane = lax.broadcasted_iota(jnp.int32, mat.shape, 1)
    return jnp.sum(jnp.where(lane == idx, mat, 0.0), axis=1, keepdims=True)


def _fox_scores(kk, qi, ckey, cq_i, i, bq):
    length = kk.shape[0]
    s = _dot_nt(kk, qi) + cq_i - ckey[:length]
    key = lax.broadcasted_iota(jnp.int32, (length, bq), 0)
    qry = lax.broadcasted_iota(jnp.int32, (length, bq), 1) + i * bq
    return jnp.where(key <= qry, s, NEG)


def _fox_fwd(name, proj, c, crow, gq, gk, n_heads):
    t = proj.shape[0]
    hw = n_heads * HEAD_DIM
    npair = n_heads // 2
    bq = min(512, t)
    nq = t // bq

    def body(q_ref, k_ref, v_ref, c_ref, crow_ref, gq_ref, gk_ref, o_ref, lse_ref):
        hp = pl.program_id(0)
        lse_ref[...] = jnp.zeros_like(lse_ref)
        outs = []
        for hh in range(2):
            sl = slice(hh * HEAD_DIM, (hh + 1) * HEAD_DIM)
            qn = _qhead(q_ref[:, sl], gq_ref[...]).astype(BF16)
            kn = _rms(k_ref[:, sl], gk_ref[...]).astype(BF16)
            v_t = v_ref[:, sl].T.astype(BF16)
            ckey = _column(c_ref[...], 2 * hp + hh)
            cq = crow_ref[0, hh:hh + 1, :]
            o_blocks = []
            for i in range(nq):
                cols = slice(i * bq, (i + 1) * bq)
                length = (i + 1) * bq
                s = _fox_scores(kn[:length], qn[cols], ckey, cq[:, cols], i, bq)
                m = jnp.max(s, axis=0, keepdims=True)
                p = jnp.exp(s - m)
                l = jnp.sum(p, axis=0, keepdims=True)
                o_blocks.append((_dot(v_t[:, :length], p.astype(BF16)) / l).T)
                lse_ref[0, hh:hh + 1, cols] = m + jnp.log(l)
            outs.append(jnp.concatenate(o_blocks, axis=0))
        o_ref[...] = jnp.concatenate(outs, axis=1).astype(BF16)

    col = lambda off: (lambda h: (0, off + h))
    fixed = lambda h: (0, 0)
    return pl.pallas_call(
        body, name=name, grid=(npair,),
        in_specs=[pl.BlockSpec((t, LANES), col(0)), pl.BlockSpec((t, LANES), col(npair)), pl.BlockSpec((t, LANES), col(2 * npair)),
                  pl.BlockSpec((t, LANES), fixed), pl.BlockSpec((1, 2, t), lambda h: (h, 0, 0)),
                  pl.BlockSpec((1, HEAD_DIM), fixed), pl.BlockSpec((1, HEAD_DIM), fixed)],
        out_specs=[pl.BlockSpec((t, LANES), col(0)), pl.BlockSpec((1, 8, t), lambda h: (h, 0, 0))],
        out_shape=[_sds((t, hw), BF16), _sds((npair, 8, t), F32)], compiler_params=_params(1),
    )(proj, proj, proj, c, crow, gq, gk)


def _fox_bwd(name, proj, c, crow, gq, gk, lse, do, n_heads, after=None):
    t = proj.shape[0]
    hw = n_heads * HEAD_DIM
    npair = n_heads // 2
    bq = min(256, t)
    nq = t // bq

    def body(q_ref, k_ref, v_ref, c_ref, crow_ref, gq_ref, gk_ref, lse_ref, do_ref, *rest):
        dqkv_ref, dc_ref, dgq_ref, dgk_ref, dk_acc, dv_acc, dc_acc = rest[-7:]
        hp = pl.program_id(0)

        @pl.when(hp == 0)
        def _():
            dgq_ref[...] = jnp.zeros_like(dgq_ref)
            dgk_ref[...] = jnp.zeros_like(dgk_ref)
            dc_ref[...] = jnp.zeros_like(dc_ref)

        lane = lax.broadcasted_iota(jnp.int32, (t, LANES), 1)
        dqs, dks, dvs = [], [], []
        for hh in range(2):
            sl = slice(hh * HEAD_DIM, (hh + 1) * HEAD_DIM)
            qf, q_vjp = jax.vjp(_qhead, q_ref[:, sl], gq_ref[...])
            kf, k_vjp = jax.vjp(_rms, k_ref[:, sl], gk_ref[...])
            qn, kn, kn_t = qf.astype(BF16), kf.astype(BF16), kf.T.astype(BF16)
            vb = v_ref[:, sl].astype(BF16)
            dob = do_ref[:, sl]
            ckey = _column(c_ref[...], 2 * hp + hh)
            cq = crow_ref[0, hh:hh + 1, :]
            lse_h = lse_ref[0, hh:hh + 1, :]
            dk_acc[...] = jnp.zeros_like(dk_acc)
            dv_acc[...] = jnp.zeros_like(dv_acc)
            dc_acc[...] = jnp.zeros_like(dc_acc)
            dq_blocks = []
            for i in range(nq):
                cols = slice(i * bq, (i + 1) * bq)
                length = (i + 1) * bq
                qi, doi = qn[cols], dob[cols]
                s = _fox_scores(kn[:length], qi, ckey, cq[:, cols], i, bq)
                p = jnp.exp(s - lse_h[:, cols])
                dp = _dot_nt(vb[:length], doi)
                ds = p * (dp - jnp.sum(p * dp, axis=0, keepdims=True))
                dsb = ds.astype(BF16)
                dq_blocks.append(_dot(kn_t[:, :length], dsb).T)
                dk_acc[0:length, :] += _dot(dsb, qi)
                dv_acc[0:length, :] += _dot(p.astype(BF16), doi)
                part = ds[:, 0:LANES]
                for j in range(1, bq // LANES):
                    part = part + ds[:, j * LANES:(j + 1) * LANES]
                dc_acc[0:length, :] += part
            dqp, dgq = q_vjp(jnp.concatenate(dq_blocks, axis=0))
            dkp, dgk = k_vjp(dk_acc[...])
            dgq_ref[...] += dgq
            dgk_ref[...] += dgk
            dqs.append(dqp)
            dks.append(dkp)
            dvs.append(dv_acc[...])
            dc_ref[...] = jnp.where(lane == 2 * hp + hh, -jnp.sum(dc_acc[...], axis=1, keepdims=True), dc_ref[...])
        for part, halves in enumerate((dqs, dks, dvs)):
            dqkv_ref[part] = jnp.concatenate(halves, axis=1).astype(BF16)

    col = lambda off: (lambda h: (0, off + h))
    fixed = lambda h: (0, 0)
    pair_blk = pl.BlockSpec((t, LANES), col(0))
    extra_specs, extra = _after_operand(after)
    return pl.pallas_call(
        body, name=name, grid=(npair,),
        in_specs=[pl.BlockSpec((t, LANES), col(0)), pl.BlockSpec((t, LANES), col(npair)), pl.BlockSpec((t, LANES), col(2 * npair)),
                  pl.BlockSpec((t, LANES), fixed), pl.BlockSpec((1, 2, t), lambda h: (h, 0, 0)),
                  pl.BlockSpec((1, HEAD_DIM), fixed), pl.BlockSpec((1, HEAD_DIM), fixed),
                  pl.BlockSpec((1, 8, t), lambda h: (h, 0, 0)), pair_blk] + extra_specs,
        out_specs=[pl.BlockSpec((3, t, LANES), lambda h: (0, 0, h)), pl.BlockSpec((t, LANES), fixed),
                   pl.BlockSpec((1, HEAD_DIM), fixed), pl.BlockSpec((1, HEAD_DIM), fixed)],
        out_shape=[_sds((3, t, hw), BF16), _sds((t, LANES), F32),
                   _sds((1, HEAD_DIM), F32), _sds((1, HEAD_DIM), F32)],
        scratch_shapes=[pltpu.VMEM((t, HEAD_DIM), F32), pltpu.VMEM((t, HEAD_DIM), F32), pltpu.VMEM((t, LANES), F32)],
        compiler_params=_params(1),
    )(proj, proj, proj, c, crow, gq, gk, lse, do, *extra)


def _t5_bucket_table():
    dist = np.arange(WINDOW)[None, :] + WINDOW - np.arange(2 * WINDOW)[:, None]
    n = np.maximum(dist, 0)
    max_exact = N_BUCKETS // 2
    large = max_exact + (np.log(np.maximum(n, 1) / max_exact) / np.log(REL_MAX_DIST / max_exact)
                         * (N_BUCKETS - max_exact)).astype(np.int32)
    large = np.minimum(large, N_BUCKETS - 1)
    return np.where(n < max_exact, n, large).astype(np.int32).reshape(1, -1)


def _bias_expand(name, rel_bias_t):
    n_heads = rel_bias_t.shape[0]
    tbl = jnp.asarray(_t5_bucket_table())
    width = tbl.shape[1]

    def body(rb_ref, tbl_ref, o_ref):
        onehot = (lax.broadcasted_iota(jnp.int32, (N_BUCKETS, width), 0) == tbl_ref[...]).astype(F32)
        o_ref[...] = _dot_exact(rb_ref[...], onehot)

    return pl.pallas_call(body, name=name, out_shape=_sds((n_heads, width), F32), compiler_params=_params(0))(rel_bias_t, tbl)


def _bias_reduce(name, dbias):
    n_heads, width = dbias.shape
    tbl = jnp.asarray(_t5_bucket_table())

    def body(db_ref, tbl_ref, o_ref):
        onehot = (lax.broadcasted_iota(jnp.int32, (N_BUCKETS, width), 0) == tbl_ref[...]).astype(F32)
        o_ref[...] = lax.dot_general(db_ref[...], onehot, (((1,), (1,)), ((), ())), preferred_element_type=F32,
                                     precision=lax.Precision.HIGHEST)

    return pl.pallas_call(body, name=name, out_shape=_sds((n_heads, N_BUCKETS), F32), compiler_params=_params(0))(dbias, tbl)


def _swa_mask(n, group):
    j = lax.broadcasted_iota(jnp.int32, (2 * WINDOW, group * WINDOW), 0)
    i = lax.broadcasted_iota(jnp.int32, (2 * WINDOW, group * WINDOW), 1) & (WINDOW - 1)
    ok = (j > i) & (j <= i + WINDOW) & ((n > 0) | (j >= WINDOW))
    return jnp.where(ok, 0.0, NEG)


def _swa_stack(ref, start, group):
    return jnp.concatenate([ref[pl.ds(start, WINDOW), g * HEAD_DIM:(g + 1) * HEAD_DIM] for g in range(group)], axis=0)


def _kv_head(ref, n_kv):
    out = ref[:, 0:HEAD_DIM]
    for h in range(1, n_kv):
        out = jnp.where(pl.program_id(0) == h, ref[:, h * HEAD_DIM:(h + 1) * HEAD_DIM], out)
    return out


def _swa_fwd(name, qb, kv, gq, gk, sinks, bias, group):
    t = qb.shape[0]
    kvh = kv.shape[1] // (2 * HEAD_DIM)
    nblk = t // WINDOW
    gw = group * HEAD_DIM
    band = 2 * WINDOW
    cols = group * WINDOW

    def body(q_ref, k_ref, v_ref, gq_ref, gk_ref, sink_ref, bias_ref, o_ref, lse_ref, qs, kpad, vpad):
        for g in range(group):
            qs[:, g * HEAD_DIM:(g + 1) * HEAD_DIM] = _qhead(q_ref[:, g * HEAD_DIM:(g + 1) * HEAD_DIM], gq_ref[...]).astype(BF16)
        kpad[0:WINDOW, :] = jnp.zeros((WINDOW, HEAD_DIM), BF16)
        vpad[0:WINDOW, :] = jnp.zeros((WINDOW, HEAD_DIM), BF16)
        kpad[WINDOW:, :] = _rms(_kv_head(k_ref, kvh), gk_ref[...]).astype(BF16)
        vpad[WINDOW:, :] = _kv_head(v_ref, kvh).astype(BF16)
        sink = sink_ref[0]

        def block(n, carry):
            start = pl.multiple_of(n * WINDOW, WINDOW)
            kb = kpad[pl.ds(start, band), :]
            vb = vpad[pl.ds(start, band), :]
            s = _dot_nt(kb, _swa_stack(qs, start, group)) + bias_ref[0] + _swa_mask(n, group)
            m = jnp.maximum(jnp.max(s, axis=0, keepdims=True), sink)
            e = jnp.exp(s - m)
            l = jnp.sum(e, axis=0, keepdims=True) + jnp.exp(sink - m)
            o_t = _dot_tn(vb, e.astype(BF16)) / l
            for g in range(group):
                o_ref[pl.ds(start, WINDOW), g * HEAD_DIM:(g + 1) * HEAD_DIM] = o_t[:, g * WINDOW:(g + 1) * WINDOW].T.astype(BF16)
            lse_ref[pl.ds(n, 1), :] = m + jnp.log(l)
            return carry

        lax.fori_loop(0, nblk, block, 0)

    fixed = lambda h: (0, 0)
    per = lambda h: (h, 0, 0)
    return pl.pallas_call(
        body, name=name, grid=(kvh,),
        in_specs=[pl.BlockSpec((t, gw), lambda h: (0, h)), pl.BlockSpec((t, kvh * HEAD_DIM), lambda h: (0, 0)),
                  pl.BlockSpec((t, kvh * HEAD_DIM), lambda h: (0, 1)),
                  pl.BlockSpec((1, HEAD_DIM), fixed), pl.BlockSpec((1, HEAD_DIM), fixed),
                  pl.BlockSpec((1, 1, cols), per), pl.BlockSpec((1, band, cols), per)],
        out_specs=[pl.BlockSpec((t, gw), lambda h: (0, h)), pl.BlockSpec((nblk, cols), lambda h: (h, 0))],
        out_shape=[_sds((t, kvh * gw), BF16), _sds((kvh * nblk, cols), F32)],
        scratch_shapes=[pltpu.VMEM((t, gw), BF16), pltpu.VMEM((t + WINDOW, HEAD_DIM), BF16),
                        pltpu.VMEM((t + WINDOW, HEAD_DIM), BF16)],
        compiler_params=_params(1),
    )(qb, kv, kv, gq, gk, sinks, bias)


def _swa_bwd(name, qb, kv, gq, gk, sinks, bias, lse, do, group):
    t = qb.shape[0]
    kvh = kv.shape[1] // (2 * HEAD_DIM)
    kvw = kvh * HEAD_DIM
    nblk = t // WINDOW
    gw = group * HEAD_DIM
    band = 2 * WINDOW
    cols = group * WINDOW

    def body(q_ref, k_ref, v_ref, gq_ref, gk_ref, sink_ref, bias_ref, lse_ref, do_ref,
             dq_ref, dkv_ref, dgq_ref, dgk_ref, dsink_ref, dbias_ref,
             qs, kpad, vpad, dqs, dk_acc, dv_acc, dsink_acc):
        @pl.when(pl.program_id(0) == 0)
        def _():
            dgq_ref[...] = jnp.zeros_like(dgq_ref)
            dgk_ref[...] = jnp.zeros_like(dgk_ref)
            dkv_ref[...] = jnp.zeros_like(dkv_ref)

        for g in range(group):
            qs[:, g * HEAD_DIM:(g + 1) * HEAD_DIM] = _qhead(q_ref[:, g * HEAD_DIM:(g + 1) * HEAD_DIM], gq_ref[...]).astype(BF16)
        kpad[0:WINDOW, :] = jnp.zeros((WINDOW, HEAD_DIM), BF16)
        vpad[0:WINDOW, :] = jnp.zeros((WINDOW, HEAD_DIM), BF16)
        kpad[WINDOW:, :] = _rms(_kv_head(k_ref, kvh), gk_ref[...]).astype(BF16)
        vpad[WINDOW:, :] = _kv_head(v_ref, kvh).astype(BF16)
        dk_acc[...] = jnp.zeros_like(dk_acc)
        dv_acc[...] = jnp.zeros_like(dv_acc)
        dsink_acc[...] = jnp.zeros_like(dsink_acc)
        dbias_ref[...] = jnp.zeros_like(dbias_ref)
        sink = sink_ref[0]

        def block(n, carry):
            start = pl.multiple_of(n * WINDOW, WINDOW)
            kb = kpad[pl.ds(start, band), :]
            vb = vpad[pl.ds(start, band), :]
            q = _swa_stack(qs, start, group)
            dob = _swa_stack(do_ref, start, group)
            lse_n = lse_ref[pl.ds(n, 1), :]
            s = _dot_nt(kb, q) + bias_ref[0] + _swa_mask(n, group)
            p = jnp.exp(s - lse_n)
            dp = _dot_nt(vb, dob)
            dsum = jnp.sum(p * dp, axis=0, keepdims=True)
            ds = p * (dp - dsum)
            dsb = ds.astype(BF16)
            dsink_acc[...] -= jnp.exp(sink - lse_n) * dsum
            dbias_ref[0] += ds
            dq = _dot_tn(dsb, kb)
            for g in range(group):
                dqs[pl.ds(start, WINDOW), g * HEAD_DIM:(g + 1) * HEAD_DIM] = dq[g * WINDOW:(g + 1) * WINDOW]
            dk_acc[pl.ds(start, band), :] += _dot(dsb, q)
            dv_acc[pl.ds(start, band), :] += _dot(p.astype(BF16), dob)
            return carry

        lax.fori_loop(0, nblk, block, 0)
        for g in range(group):
            _, q_vjp = jax.vjp(_qhead, q_ref[:, g * HEAD_DIM:(g + 1) * HEAD_DIM], gq_ref[...])
            dqp, dgq = q_vjp(dqs[:, g * HEAD_DIM:(g + 1) * HEAD_DIM])
            dq_ref[:, g * HEAD_DIM:(g + 1) * HEAD_DIM] = dqp.astype(BF16)
            dgq_ref[...] += dgq
            dsink_g = jnp.sum(dsink_acc[:, g * WINDOW:(g + 1) * WINDOW], axis=1, keepdims=True)
            dsink_ref[0, g:g + 1, :] = jnp.broadcast_to(dsink_g, (1, LANES))
        _, k_vjp = jax.vjp(_rms, _kv_head(k_ref, kvh), gk_ref[...])
        dkp, dgk = k_vjp(dk_acc[WINDOW:, :])
        dgk_ref[...] += dgk
        mine = lax.broadcasted_iota(jnp.int32, (t, kvw), 1) // HEAD_DIM == pl.program_id(0)
        dkv_ref[:, 0:kvw] = jnp.where(mine, jnp.concatenate([dkp] * kvh, axis=1), dkv_ref[:, 0:kvw])
        dkv_ref[:, kvw:] = jnp.where(mine, jnp.concatenate([dv_acc[WINDOW:, :]] * kvh, axis=1), dkv_ref[:, kvw:])

    fixed = lambda h: (0, 0)
    per = lambda h: (h, 0, 0)
    wide = pl.BlockSpec((t, gw), lambda h: (0, h))
    vec = pl.BlockSpec((1, HEAD_DIM), fixed)
    bias_spec = pl.BlockSpec((1, band, cols), per)
    return pl.pallas_call(
        body, name=name, grid=(kvh,),
        in_specs=[wide, pl.BlockSpec((t, kvw), lambda h: (0, 0)), pl.BlockSpec((t, kvw), lambda h: (0, 1)), vec, vec,
                  pl.BlockSpec((1, 1, cols), per), bias_spec, pl.BlockSpec((nblk, cols), lambda h: (h, 0)), wide],
        out_specs=[wide, pl.BlockSpec((t, 2 * kvw), fixed), vec, vec, pl.BlockSpec((1, group, LANES), per), bias_spec],
        out_shape=[_sds((t, kvh * gw), BF16), _sds((t, 2 * kvw), F32),
                   _sds((1, HEAD_DIM), F32), _sds((1, HEAD_DIM), F32),
                   _sds((kvh, group, LANES), F32), _sds((kvh, band, cols), F32)],
        scratch_shapes=[pltpu.VMEM((t, gw), BF16), pltpu.VMEM((t + WINDOW, HEAD_DIM), BF16),
                        pltpu.VMEM((t + WINDOW, HEAD_DIM), BF16), pltpu.VMEM((t, gw), F32),
                        pltpu.VMEM((t + WINDOW, HEAD_DIM), F32), pltpu.VMEM((t + WINDOW, HEAD_DIM), F32),
                        pltpu.VMEM((1, cols), F32)],
        compiler_params=_params(1),
    )(qb, kv, kv, gq, gk, sinks, bias, lse, do)


def _local_step(x, target, p, comm):
    t, d = x.shape
    n_heads = d // HEAD_DIM
    kv_heads = n_heads // 8
    group = n_heads // kv_heads
    hw = n_heads * HEAD_DIM
    gate_col = 3 * hw // LANES
    kvw = kv_heads * HEAD_DIM
    grads = {}

    def mlp_fwd(tag, h, g, layer, last=False):
        w_up, = comm.weights([f"w_up{layer}"], h)
        a, hn = _norm_matmul(f"{tag}_up", h, g, w_up, relu2=True)
        w_down, = comm.weights([f"w_down{layer}"], a)
        out = _matmul_res_loss(f"{tag}_down", a, w_down, h, target) if last else _matmul_res(f"{tag}_down", a, w_down, h)
        return out, (h, g, hn, a, w_up, w_down)

    def mlp_bwd(tag, saved, layer, dy):
        h, g, hn, a, w_up, w_down = saved
        du = _matmul_nt(f"{tag}_du", dy, w_down, a=a)
        dw_down = _matmul_tn(f"{tag}_dwdown", a, dy)
        dw_up = _matmul_tn(f"{tag}_dwup", hn, du, col_blocks=w_up.shape[0])
        sent = comm.send_grads(tag, {f"w_down{layer}": dw_down, f"w_up{layer}": dw_up})
        return _matmul_nt_rmsbwd(f"{tag}_dh", du, w_up, h, g, dy, after=sent)

    bias = _bias_expand("b_bias", p["rel_bias"].T).reshape(kv_heads, group, 2 * WINDOW, WINDOW)
    bias = bias.transpose(0, 2, 1, 3).reshape(kv_heads, 2 * WINDOW, group * WINDOW)
    comm.prefetch(["w_in_a"], bias)
    w_in, = comm.weights(["w_in_a"], None)
    proj, xn1 = _norm_matmul("a_inproj", x, p["g_attn"][0], w_in, tn=640, w_rows=True)
    ahead = comm.prefetch(["w_out_a"], proj)
    b_pad = jnp.pad(p["b_f"], ((0, 0), (0, LANES - n_heads)))
    c = _gate_fwd("a_gate", proj, b_pad, n_heads, gate_col, after=ahead)
    crow = c[:, :n_heads].T.reshape(n_heads // 2, 2, t)
    o_a, lse_a = _fox_fwd("a_attn", proj, c, crow, p["gq_a"], p["gk_a"], n_heads)
    ahead = comm.prefetch(["w_up0", "w_down0", "w_kv", "w_q_b", "w_out_b"], o_a)
    w_out_a, = comm.weights(["w_out_a"], o_a)
    h1 = _matmul_res("a_outproj", o_a, w_out_a, x, after=ahead)
    h2, mlp0 = mlp_fwd("mlp0", h1, p["g_mlp"][0], 0)

    ahead = comm.prefetch(["w_up1", "w_down1"], h2)
    w_kv, w_q_b = comm.weights(["w_kv", "w_q_b"], h2)
    kv, hn_kv = _norm_matmul("kv_proj", h2, p["g_kv"], w_kv, tn=2 * kvw, after=ahead)
    qb, hn_q = _norm_matmul("b_qproj", h2, p["g_attn"][1], w_q_b, tn=512)
    gqb, gkb = p["gq_b"], p["gk_b"].reshape(1, HEAD_DIM)
    sink_rows = jnp.broadcast_to(p["sinks"].reshape(kv_heads, 1, group, 1), (kv_heads, 1, group, WINDOW)).reshape(kv_heads, 1, group * WINDOW)
    o_b, lse_b = _swa_fwd("b_attn", qb, kv, gqb, gkb, sink_rows, bias, group)
    w_out_b, = comm.weights(["w_out_b"], o_b)
    h3 = _matmul_res("b_outproj", o_b, w_out_b, h2)
    (dy, loss_tile), mlp1 = mlp_fwd("mlp1", h3, p["g_mlp"][1], 1, last=True)

    dh3, dg_mlp1 = mlp_bwd("mlp1", mlp1, 1, dy)
    do_b = _matmul_nt("b_do", dh3, w_out_b)
    dw_out_b = _matmul_tn("b_dwout", o_b, dh3)
    dqb, dkv, grads["gq_b"], dgk_b, dsink, dbias = _swa_bwd(
        "b_attn_bwd", qb, kv, gqb, gkb, sink_rows, bias, lse_b, do_b, group)
    grads["gk_b"] = dgk_b
    grads["sinks"] = dsink[:, :, 0].reshape(1, n_heads)
    dbias = dbias.reshape(kv_heads, 2 * WINDOW, group, WINDOW).transpose(0, 2, 1, 3)
    grads["rel_bias"] = _bias_reduce("b_dbias", dbias.reshape(n_heads, WINDOW * 2 * WINDOW)).T
    dw_q_b = _matmul_tn("b_dwq", hn_q, dqb)
    dh2, dg_attn1 = _matmul_nt_rmsbwd("b_dhq", dqb, w_q_b, h2, p["g_attn"][1], dh3)
    dw_kv = _matmul_tn("kv_dw", hn_kv, dkv)
    sent = comm.send_grads("attn_b", {"w_out_b": dw_out_b, "w_q_b": dw_q_b, "w_kv": dw_kv})
    dh2, dg_kv = _matmul_nt_rmsbwd("kv_dh", dkv, w_kv, h2, p["g_kv"], dh2, after=sent)
    grads["g_kv"] = dg_kv
    dh1, dg_mlp0 = mlp_bwd("mlp0", mlp0, 0, dh2)
    grads["g_mlp"] = (dg_mlp0, dg_mlp1)

    do_a = _matmul_nt("a_do", dh1, w_out_a)
    dw_out_a = _matmul_tn("a_dwout", o_a, dh1)
    sent = comm.send_grads("attn_a_out", {"w_out_a": dw_out_a})
    dqkv, dc, grads["gq_a"], grads["gk_a"] = _fox_bwd(
        "a_attn_bwd", proj, c, crow, p["gq_a"], p["gk_a"], lse_a, do_a, n_heads, after=sent)
    dfl, db_f = _gate_bwd("a_gate_bwd", proj, b_pad, dc, n_heads, gate_col)
    grads["b_f"] = db_f
    dw_in = (_matmul_tn("a_dwin", dqkv, xn1, tk=512), _matmul_tn("a_dwin_gate", dfl, xn1))
    sent = comm.send_grads("attn_a_in", {"w_in_a": dw_in})
    grad_x, dg_attn0 = _matmul_nt_rmsbwd("a_dx", (dqkv, dfl), w_in, x, p["g_attn"][0], dh1, w_rows=True, after=sent)
    grads["g_attn"] = (dg_attn0, dg_attn1)
    return loss_tile, grad_x, grads


EVERYONE = (1, 2, 3, 4, 5, 6, 7)
SAME_CORE = (1, 2, 4, 6)
OTHER_CHIPS = (2, 4, 6)
RELAY_COLLECTIVE_ID = 0


class _InFlight:
    def __init__(self, scatter, ks, send_sems, recv_sems, srcs, lands, token):
        self.scatter, self.ks, self.send_sems, self.recv_sems = scatter, ks, send_sems, recv_sems
        self.srcs, self.lands, self.token = list(srcs), list(lands), token


def _mesh_peers(ks=EVERYONE):
    x, y, c = lax.axis_index("x"), lax.axis_index("y"), lax.axis_index("c")
    peers = []
    for k in ks:
        px, py, pc = x ^ ((k >> 2) & 1), y ^ ((k >> 1) & 1), c ^ (k & 1)
        peers.append(((px, py, pc), 4 * px + 2 * py + pc))
    return 4 * x + 2 * y + c, peers


_HBM_SPEC = pl.BlockSpec(memory_space=pltpu.HBM)
_SEM_SPEC = pl.BlockSpec(memory_space=pltpu.SEMAPHORE)
_SIDE_EFFECT = pltpu.SideEffectType.DATAFLOW_SIDE_EFFECTING


def _exchange_start(name, arrays, scatter, collective_id, ks=EVERYONE):
    n = len(arrays)
    me, _ = _mesh_peers()
    lands = []
    for a in arrays:
        own = lax.dynamic_index_in_dim(a, me, 0, keepdims=False) if scatter else a
        shape = a.shape if scatter else (N_DEV,) + a.shape
        lands.append(lax.dynamic_update_index_in_dim(lax.empty(shape, a.dtype), own, me, 0))

    def body(*refs):
        src, land = refs[:n], refs[n:2 * n]
        send_sems, recv_sems, token = refs[2 * n], refs[2 * n + 1], refs[-1]
        pos, peers = _mesh_peers(ks)
        barrier = pltpu.get_barrier_semaphore()
        for peer, _ in peers:
            pl.semaphore_signal(barrier, inc=1, device_id=peer, device_id_type=pl.DeviceIdType.MESH)
        pl.semaphore_wait(barrier, len(peers))
        for a in range(n):
            for k, (peer, peer_pos) in enumerate(peers):
                pltpu.make_async_remote_copy(
                    src_ref=src[a].at[peer_pos] if scatter else src[a], dst_ref=land[a].at[pos],
                    send_sem=send_sems.at[a * len(ks) + k], recv_sem=recv_sems.at[a * len(ks) + k],
                    device_id=peer, device_id_type=pl.DeviceIdType.MESH).start()
        token[...] = jnp.zeros_like(token)

    operands = [pltpu.with_memory_space_constraint(a, pltpu.HBM) for a in list(arrays) + lands]
    outs = pl.pallas_call(
        body, name=name,
        out_shape=(pltpu.SemaphoreType.DMA((n * len(ks),)), pltpu.SemaphoreType.DMA((n * len(ks),)),
                   *[pltpu.HBM(a.shape, a.dtype) for a in operands], _sds((8, LANES), F32)),
        in_specs=[_HBM_SPEC] * (2 * n),
        out_specs=(_SEM_SPEC, _SEM_SPEC, *[_HBM_SPEC] * (2 * n), pl.BlockSpec(memory_space=pltpu.VMEM)),
        input_output_aliases={i: 2 + i for i in range(2 * n)},
        compiler_params=pltpu.CompilerParams(has_side_effects=_SIDE_EFFECT, collective_id=collective_id),
    )(*operands)
    return _InFlight(scatter, ks, outs[0], outs[1], outs[2:2 + n], outs[2 + n:2 + 2 * n], outs[-1])


def _exchange_wait(name, flight, which, after):
    m = len(which)
    scatter, ks = flight.scatter, flight.ks

    def body(*refs):
        src, land = refs[:m], refs[m:2 * m]
        send_sems, recv_sems = refs[2 * m], refs[2 * m + 1]
        _, peers = _mesh_peers(ks)
        for i, a in enumerate(which):
            for k, (peer, peer_pos) in enumerate(peers):
                cp = pltpu.make_async_remote_copy(
                    src_ref=src[i].at[peer_pos] if scatter else src[i], dst_ref=land[i].at[peer_pos],
                    send_sem=send_sems.at[a * len(ks) + k], recv_sem=recv_sems.at[a * len(ks) + k],
                    device_id=peer, device_id_type=pl.DeviceIdType.MESH)
                cp.wait_send()
                cp.wait_recv()

    operands = [flight.srcs[a] for a in which] + [flight.lands[a] for a in which]
    outs = pl.pallas_call(
        body, name=name, out_shape=tuple(pltpu.HBM(a.shape, a.dtype) for a in operands),
        in_specs=[_HBM_SPEC] * (2 * m) + [_SEM_SPEC, _SEM_SPEC, pl.BlockSpec(memory_space=pl.ANY)],
        out_specs=tuple([_HBM_SPEC] * (2 * m)), input_output_aliases={i: i for i in range(2 * m)},
        compiler_params=pltpu.CompilerParams(has_side_effects=_SIDE_EFFECT),
    )(*operands, flight.send_sems, flight.recv_sems, after)
    return list(outs[m:])


def _relay_start(name, lands):
    n = len(lands)

    def body(*refs):
        land, send_sems, recv_sems, token = refs[:n], refs[n], refs[n + 1], refs[-1]
        _, peers = _mesh_peers(OTHER_CHIPS)
        sibling = (lax.axis_index("x"), lax.axis_index("y"), 1 - lax.axis_index("c"))
        barrier = pltpu.get_barrier_semaphore()
        pl.semaphore_signal(barrier, inc=1, device_id=sibling, device_id_type=pl.DeviceIdType.MESH)
        pl.semaphore_wait(barrier, 1)
        for a in range(n):
            for k, (_, peer_pos) in enumerate(peers):
                pltpu.make_async_remote_copy(
                    src_ref=land[a].at[peer_pos], dst_ref=land[a].at[peer_pos],
                    send_sem=send_sems.at[a * len(peers) + k], recv_sem=recv_sems.at[a * len(peers) + k],
                    device_id=sibling, device_id_type=pl.DeviceIdType.MESH).start()
        token[...] = jnp.zeros_like(token)

    count = n * len(OTHER_CHIPS)
    outs = pl.pallas_call(
        body, name=name,
        out_shape=(pltpu.SemaphoreType.DMA((count,)), pltpu.SemaphoreType.DMA((count,)),
                   *[pltpu.HBM(a.shape, a.dtype) for a in lands], _sds((8, LANES), F32)),
        in_specs=[_HBM_SPEC] * n,
        out_specs=(_SEM_SPEC, _SEM_SPEC, *[_HBM_SPEC] * n, pl.BlockSpec(memory_space=pltpu.VMEM)),
        input_output_aliases={i: 2 + i for i in range(n)},
        compiler_params=pltpu.CompilerParams(has_side_effects=_SIDE_EFFECT, collective_id=RELAY_COLLECTIVE_ID),
    )(*[pltpu.with_memory_space_constraint(a, pltpu.HBM) for a in lands])
    return _InFlight(False, OTHER_CHIPS, outs[0], outs[1], [], outs[2:2 + n], outs[-1])


def _relay_wait(name, flight, which, after):
    m = len(which)

    def body(*refs):
        land, send_sems, recv_sems = refs[:m], refs[m], refs[m + 1]
        _, peers = _mesh_peers(OTHER_CHIPS)
        sibling = (lax.axis_index("x"), lax.axis_index("y"), 1 - lax.axis_index("c"))
        for i, a in enumerate(which):
            for k, (_, peer_pos) in enumerate(peers):
                cp = pltpu.make_async_remote_copy(
                    src_ref=land[i].at[peer_pos], dst_ref=land[i].at[peer_pos ^ 1],
                    send_sem=send_sems.at[a * len(peers) + k], recv_sem=recv_sems.at[a * len(peers) + k],
                    device_id=sibling, device_id_type=pl.DeviceIdType.MESH)
                cp.wait_send()
                cp.wait_recv()

    operands = [flight.lands[a] for a in which]
    outs = pl.pallas_call(
        body, name=name, out_shape=tuple(pltpu.HBM(a.shape, a.dtype) for a in operands),
        in_specs=[_HBM_SPEC] * m + [_SEM_SPEC, _SEM_SPEC, pl.BlockSpec(memory_space=pl.ANY)],
        out_specs=tuple([_HBM_SPEC] * m), input_output_aliases={i: i for i in range(m)},
        compiler_params=pltpu.CompilerParams(has_side_effects=_SIDE_EFFECT),
    )(*operands, flight.send_sems, flight.recv_sems, after)
    return list(outs)


def _sum_parts(p_ref):
    g = p_ref[0].astype(F32)
    for dev in range(1, N_DEV):
        g = g + p_ref[dev].astype(F32)
    return g


def _adam_update(g, w, m, v):
    m_new = ADAM_B1 * m + (1.0 - ADAM_B1) * g
    v_new = ADAM_B2 * v + (1.0 - ADAM_B2) * jnp.square(g)
    m_hat = m_new / (1.0 - ADAM_B1 ** ADAM_STEP)
    v_hat = v_new / (1.0 - ADAM_B2 ** ADAM_STEP)
    return -ADAM_LR * (m_hat / (jnp.sqrt(v_hat) + ADAM_EPS) + ADAM_WD * w), m_new, v_new


def _adamw(name, parts, w, m, v, layer=None, into=None):
    r, c = w.shape[-2:]
    tr = 256 if r % 256 == 0 else r
    n_into = 0 if into is None else len(into)

    def body(p_ref, w_ref, m_ref, v_ref, *refs):
        g_ref, d_ref, mo_ref, vo_ref = refs[n_into:]
        g = _sum_parts(p_ref)
        g_ref[...] = g
        d_ref[...], mo_ref[...], vo_ref[...] = _adam_update(g, w_ref[...], m_ref[...], v_ref[...])

    if layer is None:
        blk = pl.BlockSpec((tr, c), lambda i: (i, 0))
    else:
        blk = pl.BlockSpec((None, tr, c), lambda i: (layer, i, 0))
    return pl.pallas_call(
        body, name=name, grid=(r // tr,),
        in_specs=[pl.BlockSpec((N_DEV, tr, c), lambda i: (0, i, 0)), blk, blk, blk] + [pl.BlockSpec(memory_space=pl.ANY)] * n_into,
        out_specs=[blk] * 4, out_shape=[_sds(w.shape, F32)] * 4,
        input_output_aliases={4 + i: i for i in range(n_into)}, compiler_params=_params(1),
    )(parts, w, m, v, *(into or ()))


SMALL_PACK_ROWS = 16
LOSS_ROW = 11


def _small_rows(grads, loss_tile):
    return [(0, 1, grads["g_attn"][0]), (1, 1, grads["g_attn"][1]), (2, 1, grads["g_mlp"][0]), (3, 1, grads["g_mlp"][1]),
            (4, 1, grads["g_kv"]), (5, 1, grads["b_f"]), (6, 1, grads["gq_a"]), (7, 1, grads["gk_a"]), (8, 1, grads["gk_b"]),
            (9, 1, grads["gq_b"]), (10, 1, grads["sinks"]), (LOSS_ROW, 1, loss_tile)]


SMALL_ROWS = {"g_attn": (0, 2), "g_mlp": (2, 2), "g_kv": (4, 1), "b_f": (5, 1), "gq_a": (6, 1), "gk_a": (7, 1),
              "gk_b": (8, 1), "gq_b": (9, 1), "sinks": (10, 1)}


def _pack_small(name, pieces, d):
    def body(*refs):
        out = refs[-1]
        out[...] = jnp.zeros_like(out)
        for (row, rows, _), ref in zip(pieces, refs[:-1]):
            out[row:row + rows, 0:ref.shape[1]] = ref[0:rows, :]

    return pl.pallas_call(body, name=name, out_shape=_sds((SMALL_PACK_ROWS, d), F32), compiler_params=_params(0))(
        *[piece for _, _, piece in pieces])


def _adamw_small(name, parts, parts_rel_bias, w, m, v):
    def body(*refs):
        ins, outs = refs[2:2 + 3 * len(SMALL)], refs[2 + 3 * len(SMALL):]
        pack, rel = _sum_parts(refs[0]), _sum_parts(refs[1])
        for i, k in enumerate(SMALL):
            w_ref, m_ref, v_ref = ins[3 * i:3 * i + 3]
            if k == "rel_bias":
                g = rel
            else:
                row, rows = SMALL_ROWS[k]
                g = pack[row:row + rows, 0:w_ref.shape[1]]
            outs[4 * i][...] = g
            outs[4 * i + 1][...], outs[4 * i + 2][...], outs[4 * i + 3][...] = _adam_update(g, w_ref[...], m_ref[...], v_ref[...])
        outs[-1][...] = pack[LOSS_ROW:LOSS_ROW + 1, 0:LANES]

    operands = [parts, parts_rel_bias] + [t[k] for k in SMALL for t in (w, m, v)]
    out_shape = [_sds(w[k].shape, F32) for k in SMALL for _ in range(4)] + [_sds((1, LANES), F32)]
    outs = pl.pallas_call(body, name=name, out_shape=out_shape, compiler_params=_params(0))(*operands)
    return {k: outs[4 * i:4 * i + 4] for i, k in enumerate(SMALL)}, outs[-1]


class _Comm:
    ORDER = ("w_in_a", "w_out_a", "w_up0", "w_down0", "w_kv", "w_q_b", "w_out_b", "w_up1", "w_down1")

    def __init__(self, shards, d, n_in):
        self.d, self.n_in = d, n_in
        self.ids = iter(range(RELAY_COLLECTIVE_ID + 1, RELAY_COLLECTIVE_ID + 16))
        self.flight = _exchange_start("gather_start", [shards[n].astype(BF16) for n in self.ORDER], False, next(self.ids), SAME_CORE)
        self.relays, self.sent = {}, []

    def prefetch(self, names, after):
        which = [self.ORDER.index(n) for n in names]
        landed = _exchange_wait(f"gather_wait_{names[0]}", self.flight, which, self.flight.token if after is None else after)
        relay = _relay_start(f"gather_relay_{names[0]}", landed)
        for n in names:
            self.relays[n] = (relay, names)
        return relay.token

    def weights(self, names, after):
        relay, group = self.relays[names[0]]
        landed = _relay_wait(f"gather_relay_wait_{names[0]}", relay, [group.index(n) for n in names],
                             relay.token if after is None else after)
        return [self._whole(n, g) for n, g in zip(names, landed)]

    def _whole(self, name, g):
        if name == "w_in_a":
            return _join_row_blocks("w_in_join", g, -(-self.n_in // LANES) * LANES)
        if name.startswith("w_up"):
            return g
        return g.reshape(-1, g.shape[-1])

    def _chunks(self, name, g):
        if name == "w_in_a":
            return _split_row_blocks("dw_in_split", list(g), N_DEV, self.n_in // N_DEV)
        if name.startswith("w_up"):
            return g
        return g.reshape(N_DEV, g.shape[0] // N_DEV, g.shape[1])

    def send_grads(self, tag, partials):
        names = list(partials)
        flight = _exchange_start(f"scatter_start_{tag}", [self._chunks(n, partials[n]) for n in names], True, next(self.ids))
        self.sent.append((tag, flight, names))
        return flight.token

    def received(self, index, after):
        tag, flight, names = self.sent[index]
        landed = _exchange_wait(f"scatter_wait_{tag}", flight, list(range(len(names))), after)
        return dict(zip(names, landed))


def kernel(x, g_attn, g_mlp, w_in_a, b_f, gq_a, gk_a, w_out_a, g_kv, w_kv, gk_b, w_q_b, gq_b, sinks, rel_bias, w_out_b, w_up, w_down, loss_target, m_g_attn, m_g_mlp, m_w_in_a, m_b_f, m_gq_a, m_gk_a, m_w_out_a, m_g_kv, m_w_kv, m_gk_b, m_w_q_b, m_gq_b, m_sinks, m_rel_bias, m_w_out_b, m_w_up, m_w_down, v_g_attn, v_g_mlp, v_w_in_a, v_b_f, v_gq_a, v_gk_a, v_w_out_a, v_g_kv, v_w_kv, v_gk_b, v_w_q_b, v_gq_b, v_sinks, v_rel_bias, v_w_out_b, v_w_up, v_w_down):
    w = dict(g_attn=g_attn, g_mlp=g_mlp, w_in_a=w_in_a, b_f=b_f, gq_a=gq_a, gk_a=gk_a, w_out_a=w_out_a, g_kv=g_kv,
             w_kv=w_kv, gk_b=gk_b, w_q_b=w_q_b, gq_b=gq_b, sinks=sinks, rel_bias=rel_bias, w_out_b=w_out_b,
             w_up=w_up, w_down=w_down)
    mom = dict(g_attn=m_g_attn, g_mlp=m_g_mlp, w_in_a=m_w_in_a, b_f=m_b_f, gq_a=m_gq_a, gk_a=m_gk_a, w_out_a=m_w_out_a,
               g_kv=m_g_kv, w_kv=m_w_kv, gk_b=m_gk_b, w_q_b=m_w_q_b, gq_b=m_gq_b, sinks=m_sinks, rel_bias=m_rel_bias,
               w_out_b=m_w_out_b, w_up=m_w_up, w_down=m_w_down)
    var = dict(g_attn=v_g_attn, g_mlp=v_g_mlp, w_in_a=v_w_in_a, b_f=v_b_f, gq_a=v_gq_a, gk_a=v_gk_a, w_out_a=v_w_out_a,
               g_kv=v_g_kv, w_kv=v_w_kv, gk_b=v_gk_b, w_q_b=v_w_q_b, gq_b=v_gq_b, sinks=v_sinks, rel_bias=v_rel_bias,
               w_out_b=v_w_out_b, w_up=v_w_up, w_down=v_w_down)
    d = x.shape[2]
    where = {"w_in_a": ("w_in_a", 0), "w_out_a": ("w_out_a", 0), "w_kv": ("w_kv", None), "w_q_b": ("w_q_b", 0),
             "w_out_b": ("w_out_b", 0), "w_up0": ("w_up", 0), "w_up1": ("w_up", 1), "w_down0": ("w_down", 0),
             "w_down1": ("w_down", 1)}
    flip = lambda tree: {**tree, "w_in_a": jnp.swapaxes(tree["w_in_a"], 1, 2)}
    w, mom, var = flip(w), flip(mom), flip(var)
    shards = {n: (w[k] if layer is None else w[k][layer]) for n, (k, layer) in where.items()}
    comm = _Comm(shards, d, w_in_a.shape[2] * N_DEV)
    loss_tile, grad_x, grads = _local_step(x[0], loss_target[0], {k: w[k] for k in SMALL}, comm)

    small_flight = _exchange_start(
        "gather_small_grads", [_pack_small("pack_small", _small_rows(grads, loss_tile), d), grads["rel_bias"]], False, next(comm.ids))
    res, after = {}, small_flight.token
    for index in range(len(comm.sent)):
        for n, parts in comm.received(index, after).items():
            k, layer = where[n]
            res[k] = _adamw(f"adam_{n}", parts, w[k], mom[k], var[k], layer, res.get(k))
            after = res[k][0]
    as_rows = lambda tree: {k: tree[k] if tree[k].ndim == 2 else tree[k].reshape(1, -1) for k in SMALL}
    small, loss_row = _adamw_small("adam_small", *_exchange_wait("gather_small_wait", small_flight, [0, 1], after),
                                   as_rows(w), as_rows(mom), as_rows(var))
    loss = loss_row[0, 0]
    for k in SMALL:
        res[k] = [a.reshape(w[k].shape) for a in small[k]]
    res["w_in_a"] = [jnp.swapaxes(a, 1, 2) for a in res["w_in_a"]]

    outs = [loss, grad_x[None]]
    for i in range(4):
        outs.extend(res[k][i] for k in WEIGHTS)
    return tuple(outs)
```

```python
import numpy as np
import jax
import jax.numpy as jnp
from jax import lax
from jax.experimental import pallas as pl
from jax.experimental.pallas import tpu as pltpu

F32 = jnp.float32
BF16 = jnp.bfloat16

N_DEV = 8
HEAD_DIM = 64
WINDOW = 128
N_BUCKETS = 32
REL_MAX_DIST = 128
NORM_EPS = 1e-6
NEG = -1e30
LANES = 128
VMEM_LIMIT = 56 * 1024 * 1024

ADAM_LR = 0.001
ADAM_B1 = 0.9
ADAM_B2 = 0.999
ADAM_EPS = 1e-08
ADAM_WD = 0.01
ADAM_STEP = 10
ADAM_TILE_ELEMENTS = 64 * 1024

SMALL = ("g_attn", "g_mlp", "b_f", "gq_a", "gk_a", "g_kv", "gk_b", "gq_b", "sinks", "rel_bias")
WEIGHTS = ("g_attn", "g_mlp", "w_in_a", "b_f", "gq_a", "gk_a", "w_out_a", "g_kv", "w_kv", "gk_b",
           "w_q_b", "gq_b", "sinks", "rel_bias", "w_out_b", "w_up", "w_down")


def _params(n_grid):
    return pltpu.CompilerParams(dimension_semantics=("arbitrary",) * n_grid, vmem_limit_bytes=VMEM_LIMIT)


def _sds(shape, dtype):
    return jax.ShapeDtypeStruct(tuple(shape), dtype)


def _after_operand(after):
    if after is None:
        return [], []
    return [pl.BlockSpec((8, LANES), lambda *_: (0, 0))], [after]


def _rms(x, g):
    return (x * lax.rsqrt(jnp.mean(x * x, axis=-1, keepdims=True) + NORM_EPS)) * g


def _dot_nt(a, b):
    return lax.dot_general(a, b, (((1,), (1,)), ((), ())), preferred_element_type=F32)


def _dot_tn(a, b):
    return lax.dot_general(a, b, (((0,), (0,)), ((), ())), preferred_element_type=F32)


def _dot(a, b):
    return jnp.dot(a, b, preferred_element_type=F32)


def _dot_exact(a, b):
    return jnp.dot(a, b, preferred_element_type=F32, precision=lax.Precision.HIGHEST)


def _norm_matmul(name, x, g, w, *, tn=None, relu2=False, w_rows=False, after=None):
    t, d = x.shape
    blocked = w.ndim == 3
    per_step = 2 if blocked else 1
    if blocked:
        tn = per_step * w.shape[2]
        n = w.shape[0] * w.shape[2]
        w_spec = pl.BlockSpec((per_step, d, w.shape[2]), lambda i, j: (j, 0, 0))
    elif w_rows:
        n = w.shape[0]
        w_spec = pl.BlockSpec((tn, d), lambda i, j: (j, 0))
    else:
        n = w.shape[1]
        w_spec = pl.BlockSpec((d, tn), lambda i, j: (0, j))
    tm = min(1024, t)

    def body(x_ref, g_ref, w_ref, *rest):
        y_ref, xn_ref = rest[-2:]

        @pl.when(pl.program_id(1) == 0)
        def _():
            xn_ref[...] = _rms(x_ref[...], g_ref[...]).astype(BF16)

        for b in range(per_step):
            cols = slice(b * (tn // per_step), (b + 1) * (tn // per_step)) if blocked else slice(None)
            wb = w_ref[b] if blocked else w_ref[...]
            y = _dot_nt(xn_ref[...], wb) if w_rows else _dot(xn_ref[...], wb)
            y_ref[:, cols] = jnp.square(jnp.maximum(y, 0.0)).astype(BF16) if relu2 else y

    extra_specs, extra = _after_operand(after)
    out_shape = [_sds((t, n), BF16 if relu2 else F32), _sds((t, d), BF16)]
    out_specs = [pl.BlockSpec((tm, tn), lambda i, j: (i, j)), pl.BlockSpec((tm, d), lambda i, j: (i, 0))]
    return pl.pallas_call(
        body, name=name, grid=(t // tm, n // tn),
        in_specs=[pl.BlockSpec((tm, d), lambda i, j: (i, 0)), pl.BlockSpec((1, d), lambda i, j: (0, 0)), w_spec] + extra_specs,
        out_specs=out_specs, out_shape=out_shape, compiler_params=_params(2),
    )(x, g.reshape(1, d), w, *extra)


def _matmul_res(name, a, w, res, *, tn=512, after=None):
    t, k = a.shape
    n = w.shape[1]
    tm = min(1024, t)

    def body(a_ref, w_ref, r_ref, *rest):
        rest[-1][...] = r_ref[...] + _dot(a_ref[...], w_ref[...])

    extra_specs, extra = _after_operand(after)
    return pl.pallas_call(
        body, name=name, grid=(t // tm, n // tn),
        in_specs=[pl.BlockSpec((tm, k), lambda i, j: (i, 0)), pl.BlockSpec((k, tn), lambda i, j: (0, j)),
                  pl.BlockSpec((tm, tn), lambda i, j: (i, j))] + extra_specs,
        out_specs=pl.BlockSpec((tm, tn), lambda i, j: (i, j)), out_shape=_sds((t, n), F32),
        compiler_params=_params(2),
    )(a, w, res, *extra)


def _matmul_nt(name, dy, w, *, a=None, tk=1024):
    t, n = dy.shape
    k = w.shape[0]
    tm = min(1024, t)

    def body(dy_ref, w_ref, *rest):
        o_ref = rest[-1]
        r = _dot_nt(dy_ref[...].astype(BF16), w_ref[...])
        if a is not None:
            r = r * (2.0 * jnp.sqrt(rest[0][...].astype(F32)))
        o_ref[...] = r.astype(BF16)

    in_specs = [pl.BlockSpec((tm, n), lambda i, j: (i, 0)), pl.BlockSpec((tk, n), lambda i, j: (j, 0))]
    args = [dy, w]
    if a is not None:
        in_specs.append(pl.BlockSpec((tm, tk), lambda i, j: (i, j)))
        args.append(a)
    return pl.pallas_call(
        body, name=name, grid=(t // tm, k // tk), in_specs=in_specs,
        out_specs=pl.BlockSpec((tm, tk), lambda i, j: (i, j)), out_shape=_sds((t, k), BF16),
        compiler_params=_params(2),
    )(*args)


def _matmul_nt_rmsbwd(name, dy, w, x, g, dres, *, w_rows=False, after=None):
    pieces = isinstance(dy, tuple)
    dys = list(dy) if pieces else [dy]
    t = dres.shape[0]
    blocked = w.ndim == 3
    d = w.shape[1] if blocked or w_rows else w.shape[0]
    tm = min(512, t)

    def body(*refs):
        dy_ref = refs[0]
        w_ref, x_ref, g_ref, r_ref = refs[len(dys):len(dys) + 4]
        dx_ref, dg_ref = refs[-2:]
        if pieces:
            n_stack, _, k1 = dys[0].shape
            dxn = _dot(refs[1][...], w_ref[n_stack * k1:, :])
            for j in range(n_stack):
                dxn += _dot(dy_ref[j], w_ref[j * k1:(j + 1) * k1, :])
        elif blocked:
            kb = w.shape[2]
            dxn = _dot_nt(dy_ref[:, 0:kb].astype(BF16), w_ref[0])
            for j in range(1, w.shape[0]):
                dxn += _dot_nt(dy_ref[:, j * kb:(j + 1) * kb].astype(BF16), w_ref[j])
        elif w_rows:
            dxn = _dot(dy_ref[...].astype(BF16), w_ref[...])
        else:
            dxn = _dot_nt(dy_ref[...].astype(BF16), w_ref[...])
        _, vjp = jax.vjp(_rms, x_ref[...], g_ref[...])
        dx, dg = vjp(dxn)
        dx_ref[...] = r_ref[...] + dx

        @pl.when(pl.program_id(0) == 0)
        def _():
            dg_ref[...] = jnp.zeros_like(dg_ref)

        dg_ref[...] += dg

    row = lambda i: (i, 0)
    fixed = lambda i: (0, 0)
    extra_specs, extra = _after_operand(after)
    if pieces:
        dy_specs = [pl.BlockSpec((dys[0].shape[0], tm, dys[0].shape[2]), lambda i: (0, i, 0)),
                    pl.BlockSpec((tm, dys[1].shape[1]), row)]
    else:
        dy_specs = [pl.BlockSpec((tm, dy.shape[1]), row)]
    return pl.pallas_call(
        body, name=name, grid=(t // tm,),
        in_specs=dy_specs + [pl.BlockSpec(w.shape, (lambda i: (0, 0, 0)) if blocked else fixed),
                             pl.BlockSpec((tm, d), row), pl.BlockSpec((1, d), fixed), pl.BlockSpec((tm, d), row)] + extra_specs,
        out_specs=[pl.BlockSpec((tm, d), row), pl.BlockSpec((1, d), fixed)],
        out_shape=[_sds((t, d), F32), _sds((1, d), F32)], compiler_params=_params(1),
    )(*dys, w, x, g.reshape(1, d), dres, *extra)


def _matmul_tn(name, a, b, *, tk=1024, tn=1024, col_blocks=None):
    stacked = a.ndim == 3
    t, k1 = a.shape[-2:]
    k = a.shape[0] * k1 if stacked else k1
    n = b.shape[1]
    tk = min(tk, k1)
    per = k1 // tk
    a_spec = (pl.BlockSpec((None, t, tk), lambda i, j: (i // per, 0, i % per)) if stacked
              else pl.BlockSpec((t, tk), lambda i, j: (0, i)))
    if col_blocks:
        tn = n // col_blocks
        out_spec, out_shape = pl.BlockSpec((None, tk, tn), lambda i, j: (j, i, 0)), _sds((col_blocks, k, tn), BF16)
    else:
        tn = min(tn, n)
        out_spec, out_shape = pl.BlockSpec((tk, tn), lambda i, j: (i, j)), _sds((k, n), BF16)

    def body(a_ref, b_ref, o_ref):
        o_ref[...] = _dot_tn(a_ref[...].astype(BF16), b_ref[...].astype(BF16)).astype(BF16)

    return pl.pallas_call(
        body, name=name, grid=(k // tk, n // tn),
        in_specs=[a_spec, pl.BlockSpec((t, tn), lambda i, j: (0, j))],
        out_specs=out_spec, out_shape=out_shape, compiler_params=_params(2),
    )(a, b)


def _join_row_blocks(name, blocks, rows):
    b, r, c = blocks.shape
    tc = min(256, c)

    def body(g_ref, o_ref):
        o_ref[...] = jnp.zeros_like(o_ref)
        for j in range(b):
            o_ref[r * j:r * (j + 1), :] = g_ref[j]

    return pl.pallas_call(
        body, name=name, grid=(c // tc,), in_specs=[pl.BlockSpec((b, r, tc), lambda i: (0, 0, i))],
        out_specs=pl.BlockSpec((rows, tc), lambda i: (0, i)), out_shape=_sds((rows, c), blocks.dtype),
        compiler_params=_params(1),
    )(blocks)


def _split_row_blocks(name, mats, b, r):
    c = mats[0].shape[1]
    tc = min(256, c)

    def body(*refs):
        o_ref = refs[-1]
        for j in range(b):
            first = 0
            for m_ref in refs[:-1]:
                lo, hi = max(r * j, first), min(r * (j + 1), first + m_ref.shape[0])
                if lo < hi:
                    o_ref[j, lo - r * j:hi - r * j, :] = m_ref[lo - first:hi - first, :]
                first += m_ref.shape[0]

    return pl.pallas_call(
        body, name=name, grid=(c // tc,), in_specs=[pl.BlockSpec((m.shape[0], tc), lambda i: (0, i)) for m in mats],
        out_specs=pl.BlockSpec((b, r, tc), lambda i: (0, 0, i)), out_shape=_sds((b, r, c), mats[0].dtype),
        compiler_params=_params(1),
    )(*mats)


def _matmul_res_loss(name, a, w, res, target, *, tn=512):
    t, k = a.shape
    n = w.shape[1]
    tm = min(1024, t)

    def body(a_ref, w_ref, r_ref, t_ref, dy_ref, l_ref):
        e = r_ref[...] + _dot(a_ref[...], w_ref[...]) - t_ref[...]
        dy_ref[...] = e * (1.0 / n)

        @pl.when((pl.program_id(0) == 0) & (pl.program_id(1) == 0))
        def _():
            l_ref[...] = jnp.zeros_like(l_ref)

        l_ref[...] += (0.5 / n) * jnp.sum(e * e)

    tile = pl.BlockSpec((tm, tn), lambda i, j: (i, j))
    return pl.pallas_call(
        body, name=name, grid=(t // tm, n // tn),
        in_specs=[pl.BlockSpec((tm, k), lambda i, j: (i, 0)), pl.BlockSpec((k, tn), lambda i, j: (0, j)), tile, tile],
        out_specs=[tile, pl.BlockSpec((8, LANES), lambda i, j: (0, 0))],
        out_shape=[_sds((t, n), F32), _sds((8, LANES), F32)], compiler_params=_params(2),
    )(a, w, res, target)


def _gate_fwd(name, proj, b_pad, n_heads, gate_col, after=None):
    t = proj.shape[0]
    tb = min(256, t)
    tri = jnp.asarray(np.tril(np.ones((tb, tb), np.float32)))
    extra_specs, extra = _after_operand(after)

    def body(p_ref, b_ref, tri_ref, *rest):
        c_ref, carry = rest[-2:]

        @pl.when(pl.program_id(0) == 0)
        def _():
            carry[...] = jnp.zeros_like(carry)

        lane = lax.broadcasted_iota(jnp.int32, (tb, LANES), 1)
        lf = jnp.where(lane < n_heads, jax.nn.log_sigmoid(p_ref[...] + b_ref[...]), 0.0)
        c = _dot_exact(tri_ref[...], lf) + carry[0:1, :]
        c_ref[...] = c
        carry[...] = jnp.broadcast_to(c[tb - 1:tb, :], carry.shape)

    return pl.pallas_call(
        body, name=name, grid=(t // tb,),
        in_specs=[pl.BlockSpec((tb, LANES), lambda i: (i, gate_col)), pl.BlockSpec((1, LANES), lambda i: (0, 0)),
                  pl.BlockSpec((tb, tb), lambda i: (0, 0))] + extra_specs,
        out_specs=pl.BlockSpec((tb, LANES), lambda i: (i, 0)), out_shape=_sds((t, LANES), F32),
        scratch_shapes=[pltpu.VMEM((8, LANES), F32)], compiler_params=_params(1),
    )(proj, b_pad, tri, *extra)


def _gate_bwd(name, proj, b_pad, dc, n_heads, gate_col):
    t = proj.shape[0]
    tb = min(256, t)
    nb = t // tb
    triu = jnp.asarray(np.triu(np.ones((tb, tb), np.float32)))

    def body(p_ref, b_ref, dc_ref, tri_ref, df_ref, db_ref, carry):
        @pl.when(pl.program_id(0) == 0)
        def _():
            carry[...] = jnp.zeros_like(carry)
            db_ref[...] = jnp.zeros_like(db_ref)

        dcv = dc_ref[...]
        dlf = _dot_exact(tri_ref[...], dcv) + carry[0:1, :]
        carry[...] = jnp.broadcast_to(dlf[0:1, :], carry.shape)
        lane = lax.broadcasted_iota(jnp.int32, (tb, LANES), 1)
        z = p_ref[...] + b_ref[...]
        df = jnp.where(lane < n_heads, dlf / (1.0 + jnp.exp(z)), 0.0)
        df_ref[...] = df.astype(BF16)
        db_ref[...] += jnp.sum(df, axis=0, keepdims=True)

    return pl.pallas_call(
        body, name=name, grid=(nb,),
        in_specs=[pl.BlockSpec((tb, LANES), lambda i: (nb - 1 - i, gate_col)), pl.BlockSpec((1, LANES), lambda i: (0, 0)),
                  pl.BlockSpec((tb, LANES), lambda i: (nb - 1 - i, 0)), pl.BlockSpec((tb, tb), lambda i: (0, 0))],
        out_specs=[pl.BlockSpec((tb, LANES), lambda i: (nb - 1 - i, 0)), pl.BlockSpec((1, LANES), lambda i: (0, 0))],
        out_shape=[_sds((t, LANES), BF16), _sds((1, LANES), F32)],
        scratch_shapes=[pltpu.VMEM((8, LANES), F32)], compiler_params=_params(1),
    )(proj, b_pad, dc, triu)


def _qhead(qp, g):
    return _rms(qp, g) * (HEAD_DIM ** -0.5)


def _column(mat, idx):
    lane = lax.broadcasted_iota(jnp.int32, mat.shape, 1)
    return jnp.sum(jnp.where(lane == idx, mat, 0.0), axis=1, keepdims=True)


def _fox_scores(kk, qi, ckey, cq_i, i, bq):
    length = kk.shape[0]
    s = _dot_nt(kk, qi) + cq_i - ckey[:length]
    key = lax.broadcasted_iota(jnp.int32, (length, bq), 0)
    qry = lax.broadcasted_iota(jnp.int32, (length, bq), 1) + i * bq
    return jnp.where(key <= qry, s, NEG)


def _fox_fwd(name, proj, c, crow, gq, gk, n_heads):
    t = proj.shape[0]
    hw = n_heads * HEAD_DIM
    npair = n_heads // 2
    bq = min(512, t)
    nq = t // bq

    def body(q_ref, k_ref, v_ref, c_ref, crow_ref, gq_ref, gk_ref, o_ref, lse_ref):
        hp = pl.program_id(0)
        lse_ref[...] = jnp.zeros_like(lse_ref)
        outs = []
        for hh in range(2):
            sl = slice(hh * HEAD_DIM, (hh + 1) * HEAD_DIM)
            qn = _qhead(q_ref[:, sl], gq_ref[...]).astype(BF16)
            kn = _rms(k_ref[:, sl], gk_ref[...]).astype(BF16)
            v_t = v_ref[:, sl].T.astype(BF16)
            ckey = _column(c_ref[...], 2 * hp + hh)
            cq = crow_ref[0, hh:hh + 1, :]
            o_blocks = []
            for i in range(nq):
                cols = slice(i * bq, (i + 1) * bq)
                length = (i + 1) * bq
                s = _fox_scores(kn[:length], qn[cols], ckey, cq[:, cols], i, bq)
                m = jnp.max(s, axis=0, keepdims=True)
                p = jnp.exp(s - m)
                l = jnp.sum(p, axis=0, keepdims=True)
                o_blocks.append((_dot(v_t[:, :length], p.astype(BF16)) / l).T)
                lse_ref[0, hh:hh + 1, cols] = m + jnp.log(l)
            outs.append(jnp.concatenate(o_blocks, axis=0))
        o_ref[...] = jnp.concatenate(outs, axis=1).astype(BF16)

    col = lambda off: (lambda h: (0, off + h))
    fixed = lambda h: (0, 0)
    return pl.pallas_call(
        body, name=name, grid=(npair,),
        in_specs=[pl.BlockSpec((t, LANES), col(0)), pl.BlockSpec((t, LANES), col(npair)), pl.BlockSpec((t, LANES), col(2 * npair)),
                  pl.BlockSpec((t, LANES), fixed), pl.BlockSpec((1, 2, t), lambda h: (h, 0, 0)),
                  pl.BlockSpec((1, HEAD_DIM), fixed), pl.BlockSpec((1, HEAD_DIM), fixed)],
        out_specs=[pl.BlockSpec((t, LANES), col(0)), pl.BlockSpec((1, 8, t), lambda h: (h, 0, 0))],
        out_shape=[_sds((t, hw), BF16), _sds((npair, 8, t), F32)], compiler_params=_params(1),
    )(proj, proj, proj, c, crow, gq, gk)


def _fox_bwd(name, proj, c, crow, gq, gk, lse, do, n_heads, after=None):
    t = proj.shape[0]
    hw = n_heads * HEAD_DIM
    npair = n_heads // 2
    bq = min(256, t)
    nq = t // bq

    def body(q_ref, k_ref, v_ref, c_ref, crow_ref, gq_ref, gk_ref, lse_ref, do_ref, *rest):
        dqkv_ref, dc_ref, dgq_ref, dgk_ref, dk_acc, dv_acc, dc_acc = rest[-7:]
        hp = pl.program_id(0)

        @pl.when(hp == 0)
        def _():
            dgq_ref[...] = jnp.zeros_like(dgq_ref)
            dgk_ref[...] = jnp.zeros_like(dgk_ref)
            dc_ref[...] = jnp.zeros_like(dc_ref)

        lane = lax.broadcasted_iota(jnp.int32, (t, LANES), 1)
        dqs, dks, dvs = [], [], []
        for hh in range(2):
            sl = slice(hh * HEAD_DIM, (hh + 1) * HEAD_DIM)
            qf, q_vjp = jax.vjp(_qhead, q_ref[:, sl], gq_ref[...])
            kf, k_vjp = jax.vjp(_rms, k_ref[:, sl], gk_ref[...])
            qn, kn, kn_t = qf.astype(BF16), kf.astype(BF16), kf.T.astype(BF16)
            vb = v_ref[:, sl].astype(BF16)
            dob = do_ref[:, sl]
            ckey = _column(c_ref[...], 2 * hp + hh)
            cq = crow_ref[0, hh:hh + 1, :]
            lse_h = lse_ref[0, hh:hh + 1, :]
            dk_acc[...] = jnp.zeros_like(dk_acc)
            dv_acc[...] = jnp.zeros_like(dv_acc)
            dc_acc[...] = jnp.zeros_like(dc_acc)
            dq_blocks = []
            for i in range(nq):
                cols = slice(i * bq, (i + 1) * bq)
                length = (i + 1) * bq
                qi, doi = qn[cols], dob[cols]
                s = _fox_scores(kn[:length], qi, ckey, cq[:, cols], i, bq)
                p = jnp.exp(s - lse_h[:, cols])
                dp = _dot_nt(vb[:length], doi)
                ds = p * (dp - jnp.sum(p * dp, axis=0, keepdims=True))
                dsb = ds.astype(BF16)
                dq_blocks.append(_dot(kn_t[:, :length], dsb).T)
                dk_acc[0:length, :] += _dot(dsb, qi)
                dv_acc[0:length, :] += _dot(p.astype(BF16), doi)
                part = ds[:, 0:LANES]
                for j in range(1, bq // LANES):
                    part = part + ds[:, j * LANES:(j + 1) * LANES]
                dc_acc[0:length, :] += part
            dqp, dgq = q_vjp(jnp.concatenate(dq_blocks, axis=0))
            dkp, dgk = k_vjp(dk_acc[...])
            dgq_ref[...] += dgq
            dgk_ref[...] += dgk
            dqs.append(dqp)
            dks.append(dkp)
            dvs.append(dv_acc[...])
            dc_ref[...] = jnp.where(lane == 2 * hp + hh, -jnp.sum(dc_acc[...], axis=1, keepdims=True), dc_ref[...])
        for part, halves in enumerate((dqs, dks, dvs)):
            dqkv_ref[part] = jnp.concatenate(halves, axis=1).astype(BF16)

    col = lambda off: (lambda h: (0, off + h))
    fixed = lambda h: (0, 0)
    pair_blk = pl.BlockSpec((t, LANES), col(0))
    extra_specs, extra = _after_operand(after)
    return pl.pallas_call(
        body, name=name, grid=(npair,),
        in_specs=[pl.BlockSpec((t, LANES), col(0)), pl.BlockSpec((t, LANES), col(npair)), pl.BlockSpec((t, LANES), col(2 * npair)),
                  pl.BlockSpec((t, LANES), fixed), pl.BlockSpec((1, 2, t), lambda h: (h, 0, 0)),
                  pl.BlockSpec((1, HEAD_DIM), fixed), pl.BlockSpec((1, HEAD_DIM), fixed),
                  pl.BlockSpec((1, 8, t), lambda h: (h, 0, 0)), pair_blk] + extra_specs,
        out_specs=[pl.BlockSpec((3, t, LANES), lambda h: (0, 0, h)), pl.BlockSpec((t, LANES), fixed),
                   pl.BlockSpec((1, HEAD_DIM), fixed), pl.BlockSpec((1, HEAD_DIM), fixed)],
        out_shape=[_sds((3, t, hw), BF16), _sds((t, LANES), F32),
                   _sds((1, HEAD_DIM), F32), _sds((1, HEAD_DIM), F32)],
        scratch_shapes=[pltpu.VMEM((t, HEAD_DIM), F32), pltpu.VMEM((t, HEAD_DIM), F32), pltpu.VMEM((t, LANES), F32)],
        compiler_params=_params(1),
    )(proj, proj, proj, c, crow, gq, gk, lse, do, *extra)


def _t5_bucket_table():
    dist = np.arange(WINDOW)[None, :] + WINDOW - np.arange(2 * WINDOW)[:, None]
    n = np.maximum(dist, 0)
    max_exact = N_BUCKETS // 2
    large = max_exact + (np.log(np.maximum(n, 1) / max_exact) / np.log(REL_MAX_DIST / max_exact)
                         * (N_BUCKETS - max_exact)).astype(np.int32)
    large = np.minimum(large, N_BUCKETS - 1)
    return np.where(n < max_exact, n, large).astype(np.int32).reshape(1, -1)


def _bias_expand(name, rel_bias_t):
    n_heads = rel_bias_t.shape[0]
    tbl = jnp.asarray(_t5_bucket_table())
    width = tbl.shape[1]

    def body(rb_ref, tbl_ref, o_ref):
        onehot = (lax.broadcasted_iota(jnp.int32, (N_BUCKETS, width), 0) == tbl_ref[...]).astype(F32)
        o_ref[...] = _dot_exact(rb_ref[...], onehot)

    return pl.pallas_call(body, name=name, out_shape=_sds((n_heads, width), F32), compiler_params=_params(0))(rel_bias_t, tbl)


def _bias_reduce(name, dbias):
    n_heads, width = dbias.shape
    tbl = jnp.asarray(_t5_bucket_table())

    def body(db_ref, tbl_ref, o_ref):
        onehot = (lax.broadcasted_iota(jnp.int32, (N_BUCKETS, width), 0) == tbl_ref[...]).astype(F32)
        o_ref[...] = lax.dot_general(db_ref[...], onehot, (((1,), (1,)), ((), ())), preferred_element_type=F32,
                                     precision=lax.Precision.HIGHEST)

    return pl.pallas_call(body, name=name, out_shape=_sds((n_heads, N_BUCKETS), F32), compiler_params=_params(0))(dbias, tbl)


def _swa_mask(n, group):
    j = lax.broadcasted_iota(jnp.int32, (2 * WINDOW, group * WINDOW), 0)
    i = lax.broadcasted_iota(jnp.int32, (2 * WINDOW, group * WINDOW), 1) & (WINDOW - 1)
    ok = (j > i) & (j <= i + WINDOW) & ((n > 0) | (j >= WINDOW))
    return jnp.where(ok, 0.0, NEG)


def _swa_stack(ref, start, group):
    return jnp.concatenate([ref[pl.ds(start, WINDOW), g * HEAD_DIM:(g + 1) * HEAD_DIM] for g in range(group)], axis=0)


def _kv_head(ref, n_kv):
    out = ref[:, 0:HEAD_DIM]
    for h in range(1, n_kv):
        out = jnp.where(pl.program_id(0) == h, ref[:, h * HEAD_DIM:(h + 1) * HEAD_DIM], out)
    return out


def _swa_fwd(name, qb, kv, gq, gk, sinks, bias, group):
    t = qb.shape[0]
    kvh = kv.shape[1] // (2 * HEAD_DIM)
    nblk = t // WINDOW
    gw = group * HEAD_DIM
    band = 2 * WINDOW
    cols = group * WINDOW

    def body(q_ref, k_ref, v_ref, gq_ref, gk_ref, sink_ref, bias_ref, o_ref, lse_ref, qs, kpad, vpad):
        for g in range(group):
            qs[:, g * HEAD_DIM:(g + 1) * HEAD_DIM] = _qhead(q_ref[:, g * HEAD_DIM:(g + 1) * HEAD_DIM], gq_ref[...]).astype(BF16)
        kpad[0:WINDOW, :] = jnp.zeros((WINDOW, HEAD_DIM), BF16)
        vpad[0:WINDOW, :] = jnp.zeros((WINDOW, HEAD_DIM), BF16)
        kpad[WINDOW:, :] = _rms(_kv_head(k_ref, kvh), gk_ref[...]).astype(BF16)
        vpad[WINDOW:, :] = _kv_head(v_ref, kvh).astype(BF16)
        sink = sink_ref[0]

        def block(n, carry):
            start = pl.multiple_of(n * WINDOW, WINDOW)
            kb = kpad[pl.ds(start, band), :]
            vb = vpad[pl.ds(start, band), :]
            s = _dot_nt(kb, _swa_stack(qs, start, group)) + bias_ref[0] + _swa_mask(n, group)
            m = jnp.maximum(jnp.max(s, axis=0, keepdims=True), sink)
            e = jnp.exp(s - m)
            l = jnp.sum(e, axis=0, keepdims=True) + jnp.exp(sink - m)
            o_t = _dot_tn(vb, e.astype(BF16)) / l
            for g in range(group):
                o_ref[pl.ds(start, WINDOW), g * HEAD_DIM:(g + 1) * HEAD_DIM] = o_t[:, g * WINDOW:(g + 1) * WINDOW].T.astype(BF16)
            lse_ref[pl.ds(n, 1), :] = m + jnp.log(l)
            return carry

        lax.fori_loop(0, nblk, block, 0)

    fixed = lambda h: (0, 0)
    per = lambda h: (h, 0, 0)
    return pl.pallas_call(
        body, name=name, grid=(kvh,),
        in_specs=[pl.BlockSpec((t, gw), lambda h: (0, h)), pl.BlockSpec((t, kvh * HEAD_DIM), lambda h: (0, 0)),
                  pl.BlockSpec((t, kvh * HEAD_DIM), lambda h: (0, 1)),
                  pl.BlockSpec((1, HEAD_DIM), fixed), pl.BlockSpec((1, HEAD_DIM), fixed),
                  pl.BlockSpec((1, 1, cols), per), pl.BlockSpec((1, band, cols), per)],
        out_specs=[pl.BlockSpec((t, gw), lambda h: (0, h)), pl.BlockSpec((nblk, cols), lambda h: (h, 0))],
        out_shape=[_sds((t, kvh * gw), BF16), _sds((kvh * nblk, cols), F32)],
        scratch_shapes=[pltpu.VMEM((t, gw), BF16), pltpu.VMEM((t + WINDOW, HEAD_DIM), BF16),
                        pltpu.VMEM((t + WINDOW, HEAD_DIM), BF16)],
        compiler_params=_params(1),
    )(qb, kv, kv, gq, gk, sinks, bias)


def _swa_bwd(name, qb, kv, gq, gk, sinks, bias, lse, do, group):
    t = qb.shape[0]
    kvh = kv.shape[1] // (2 * HEAD_DIM)
    kvw = kvh * HEAD_DIM
    nblk = t // WINDOW
    gw = group * HEAD_DIM
    band = 2 * WINDOW
    cols = group * WINDOW

    def body(q_ref, k_ref, v_ref, gq_ref, gk_ref, sink_ref, bias_ref, lse_ref, do_ref,
             dq_ref, dkv_ref, dgq_ref, dgk_ref, dsink_ref, dbias_ref,
             qs, kpad, vpad, dqs, dk_acc, dv_acc, dsink_acc):
        @pl.when(pl.program_id(0) == 0)
        def _():
            dgq_ref[...] = jnp.zeros_like(dgq_ref)
            dgk_ref[...] = jnp.zeros_like(dgk_ref)
            dkv_ref[...] = jnp.zeros_like(dkv_ref)

        for g in range(group):
            qs[:, g * HEAD_DIM:(g + 1) * HEAD_DIM] = _qhead(q_ref[:, g * HEAD_DIM:(g + 1) * HEAD_DIM], gq_ref[...]).astype(BF16)
        kpad[0:WINDOW, :] = jnp.zeros((WINDOW, HEAD_DIM), BF16)
        vpad[0:WINDOW, :] = jnp.zeros((WINDOW, HEAD_DIM), BF16)
        kpad[WINDOW:, :] = _rms(_kv_head(k_ref, kvh), gk_ref[...]).astype(BF16)
        vpad[WINDOW:, :] = _kv_head(v_ref, kvh).astype(BF16)
        dk_acc[...] = jnp.zeros_like(dk_acc)
        dv_acc[...] = jnp.zeros_like(dv_acc)
        dsink_acc[...] = jnp.zeros_like(dsink_acc)
        dbias_ref[...] = jnp.zeros_like(dbias_ref)
        sink = sink_ref[0]

        def block(n, carry):
            start = pl.multiple_of(n * WINDOW, WINDOW)
            kb = kpad[pl.ds(start, band), :]
            vb = vpad[pl.ds(start, band), :]
            q = _swa_stack(qs, start, group)
            dob = _swa_stack(do_ref, start, group)
            lse_n = lse_ref[pl.ds(n, 1), :]
            s = _dot_nt(kb, q) + bias_ref[0] + _swa_mask(n, group)
            p = jnp.exp(s - lse_n)
            dp = _dot_nt(vb, dob)
            dsum = jnp.sum(p * dp, axis=0, keepdims=True)
            ds = p * (dp - dsum)
            dsb = ds.astype(BF16)
            dsink_acc[...] -= jnp.exp(sink - lse_n) * dsum
            dbias_ref[0] += ds
            dq = _dot_tn(dsb, kb)
            for g in range(group):
                dqs[pl.ds(start, WINDOW), g * HEAD_DIM:(g + 1) * HEAD_DIM] = dq[g * WINDOW:(g + 1) * WINDOW]
            dk_acc[pl.ds(start, band), :] += _dot(dsb, q)
            dv_acc[pl.ds(start, band), :] += _dot(p.astype(BF16), dob)
            return carry

        lax.fori_loop(0, nblk, block, 0)
        for g in range(group):
            _, q_vjp = jax.vjp(_qhead, q_ref[:, g * HEAD_DIM:(g + 1) * HEAD_DIM], gq_ref[...])
            dqp, dgq = q_vjp(dqs[:, g * HEAD_DIM:(g + 1) * HEAD_DIM])
            dq_ref[:, g * HEAD_DIM:(g + 1) * HEAD_DIM] = dqp.astype(BF16)
            dgq_ref[...] += dgq
            dsink_g = jnp.sum(dsink_acc[:, g * WINDOW:(g + 1) * WINDOW], axis=1, keepdims=True)
            dsink_ref[0, g:g + 1, :] = jnp.broadcast_to(dsink_g, (1, LANES))
        _, k_vjp = jax.vjp(_rms, _kv_head(k_ref, kvh), gk_ref[...])
        dkp, dgk = k_vjp(dk_acc[WINDOW:, :])
        dgk_ref[...] += dgk
        mine = lax.broadcasted_iota(jnp.int32, (t, kvw), 1) // HEAD_DIM == pl.program_id(0)
        dkv_ref[:, 0:kvw] = jnp.where(mine, jnp.concatenate([dkp] * kvh, axis=1), dkv_ref[:, 0:kvw])
        dkv_ref[:, kvw:] = jnp.where(mine, jnp.concatenate([dv_acc[WINDOW:, :]] * kvh, axis=1), dkv_ref[:, kvw:])

    fixed = lambda h: (0, 0)
    per = lambda h: (h, 0, 0)
    wide = pl.BlockSpec((t, gw), lambda h: (0, h))
    vec = pl.BlockSpec((1, HEAD_DIM), fixed)
    bias_spec = pl.BlockSpec((1, band, cols), per)
    return pl.pallas_call(
        body, name=name, grid=(kvh,),
        in_specs=[wide, pl.BlockSpec((t, kvw), lambda h: (0, 0)), pl.BlockSpec((t, kvw), lambda h: (0, 1)), vec, vec,
                  pl.BlockSpec((1, 1, cols), per), bias_spec, pl.BlockSpec((nblk, cols), lambda h: (h, 0)), wide],
        out_specs=[wide, pl.BlockSpec((t, 2 * kvw), fixed), vec, vec, pl.BlockSpec((1, group, LANES), per), bias_spec],
        out_shape=[_sds((t, kvh * gw), BF16), _sds((t, 2 * kvw), F32),
                   _sds((1, HEAD_DIM), F32), _sds((1, HEAD_DIM), F32),
                   _sds((kvh, group, LANES), F32), _sds((kvh, band, cols), F32)],
        scratch_shapes=[pltpu.VMEM((t, gw), BF16), pltpu.VMEM((t + WINDOW, HEAD_DIM), BF16),
                        pltpu.VMEM((t + WINDOW, HEAD_DIM), BF16), pltpu.VMEM((t, gw), F32),
                        pltpu.VMEM((t + WINDOW, HEAD_DIM), F32), pltpu.VMEM((t + WINDOW, HEAD_DIM), F32),
                        pltpu.VMEM((1, cols), F32)],
        compiler_params=_params(1),
    )(qb, kv, kv, gq, gk, sinks, bias, lse, do)


def _local_step(x, target, p, comm):
    t, d = x.shape
    n_heads = d // HEAD_DIM
    kv_heads = n_heads // 8
    group = n_heads // kv_heads
    hw = n_heads * HEAD_DIM
    gate_col = 3 * hw // LANES
    kvw = kv_heads * HEAD_DIM
    grads = {}

    def mlp_fwd(tag, h, g, layer, last=False):
        w_up, = comm.weights([f"w_up{layer}"], h)
        a, hn = _norm_matmul(f"{tag}_up", h, g, w_up, relu2=True)
        w_down, = comm.weights([f"w_down{layer}"], a)
        out = _matmul_res_loss(f"{tag}_down", a, w_down, h, target) if last else _matmul_res(f"{tag}_down", a, w_down, h)
        return out, (h, g, hn, a, w_up, w_down)

    def mlp_bwd(tag, saved, layer, dy):
        h, g, hn, a, w_up, w_down = saved
        du = _matmul_nt(f"{tag}_du", dy, w_down, a=a)
        dw_down = _matmul_tn(f"{tag}_dwdown", a, dy)
        dw_up = _matmul_tn(f"{tag}_dwup", hn, du, col_blocks=w_up.shape[0])
        sent = comm.send_grads(tag, {f"w_down{layer}": dw_down, f"w_up{layer}": dw_up})
        return _matmul_nt_rmsbwd(f"{tag}_dh", du, w_up, h, g, dy, after=sent)

    bias = _bias_expand("b_bias", p["rel_bias"].T).reshape(kv_heads, group, 2 * WINDOW, WINDOW)
    bias = bias.transpose(0, 2, 1, 3).reshape(kv_heads, 2 * WINDOW, group * WINDOW)
    comm.prefetch(["w_in_a"], bias)
    w_in, = comm.weights(["w_in_a"], None)
    proj, xn1 = _norm_matmul("a_inproj", x, p["g_attn"][0], w_in, tn=640, w_rows=True)
    ahead = comm.prefetch(["w_out_a"], proj)
    b_pad = jnp.pad(p["b_f"], ((0, 0), (0, LANES - n_heads)))
    c = _gate_fwd("a_gate", proj, b_pad, n_heads, gate_col, after=ahead)
    crow = c[:, :n_heads].T.reshape(n_heads // 2, 2, t)
    o_a, lse_a = _fox_fwd("a_attn", proj, c, crow, p["gq_a"], p["gk_a"], n_heads)
    ahead = comm.prefetch(["w_up0", "w_down0", "w_kv", "w_q_b", "w_out_b"], o_a)
    w_out_a, = comm.weights(["w_out_a"], o_a)
    h1 = _matmul_res("a_outproj", o_a, w_out_a, x, after=ahead)
    h2, mlp0 = mlp_fwd("mlp0", h1, p["g_mlp"][0], 0)

    ahead = comm.prefetch(["w_up1", "w_down1"], h2)
    w_kv, w_q_b = comm.weights(["w_kv", "w_q_b"], h2)
    kv, hn_kv = _norm_matmul("kv_proj", h2, p["g_kv"], w_kv, tn=2 * kvw, after=ahead)
    qb, hn_q = _norm_matmul("b_qproj", h2, p["g_attn"][1], w_q_b, tn=512)
    gqb, gkb = p["gq_b"], p["gk_b"].reshape(1, HEAD_DIM)
    sink_rows = jnp.broadcast_to(p["sinks"].reshape(kv_heads, 1, group, 1), (kv_heads, 1, group, WINDOW)).reshape(kv_heads, 1, group * WINDOW)
    o_b, lse_b = _swa_fwd("b_attn", qb, kv, gqb, gkb, sink_rows, bias, group)
    w_out_b, = comm.weights(["w_out_b"], o_b)
    h3 = _matmul_res("b_outproj", o_b, w_out_b, h2)
    (dy, loss_tile), mlp1 = mlp_fwd("mlp1", h3, p["g_mlp"][1], 1, last=True)

    dh3, dg_mlp1 = mlp_bwd("mlp1", mlp1, 1, dy)
    do_b = _matmul_nt("b_do", dh3, w_out_b)
    dw_out_b = _matmul_tn("b_dwout", o_b, dh3)
    dqb, dkv, grads["gq_b"], dgk_b, dsink, dbias = _swa_bwd(
        "b_attn_bwd", qb, kv, gqb, gkb, sink_rows, bias, lse_b, do_b, group)
    grads["gk_b"] = dgk_b
    grads["sinks"] = dsink[:, :, 0].reshape(1, n_heads)
    dbias = dbias.reshape(kv_heads, 2 * WINDOW, group, WINDOW).transpose(0, 2, 1, 3)
    grads["rel_bias"] = _bias_reduce("b_dbias", dbias.reshape(n_heads, WINDOW * 2 * WINDOW)).T
    dw_q_b = _matmul_tn("b_dwq", hn_q, dqb)
    dh2, dg_attn1 = _matmul_nt_rmsbwd("b_dhq", dqb, w_q_b, h2, p["g_attn"][1], dh3)
    dw_kv = _matmul_tn("kv_dw", hn_kv, dkv)
    sent = comm.send_grads("attn_b", {"w_out_b": dw_out_b, "w_q_b": dw_q_b, "w_kv": dw_kv})
    dh2, dg_kv = _matmul_nt_rmsbwd("kv_dh", dkv, w_kv, h2, p["g_kv"], dh2, after=sent)
    grads["g_kv"] = dg_kv
    dh1, dg_mlp0 = mlp_bwd("mlp0", mlp0, 0, dh2)
    grads["g_mlp"] = (dg_mlp0, dg_mlp1)

    do_a = _matmul_nt("a_do", dh1, w_out_a)
    dw_out_a = _matmul_tn("a_dwout", o_a, dh1)
    sent = comm.send_grads("attn_a_out", {"w_out_a": dw_out_a})
    dqkv, dc, grads["gq_a"], grads["gk_a"] = _fox_bwd(
        "a_attn_bwd", proj, c, crow, p["gq_a"], p["gk_a"], lse_a, do_a, n_heads, after=sent)
    dfl, db_f = _gate_bwd("a_gate_bwd", proj, b_pad, dc, n_heads, gate_col)
    grads["b_f"] = db_f
    dw_in = (_matmul_tn("a_dwin", dqkv, xn1, tk=512), _matmul_tn("a_dwin_gate", dfl, xn1))
    sent = comm.send_grads("attn_a_in", {"w_in_a": dw_in})
    grad_x, dg_attn0 = _matmul_nt_rmsbwd("a_dx", (dqkv, dfl), w_in, x, p["g_attn"][0], dh1, w_rows=True, after=sent)
    grads["g_attn"] = (dg_attn0, dg_attn1)
    return loss_tile, grad_x, grads


EVERYONE = (1, 2, 3, 4, 5, 6, 7)
SAME_CORE = (1, 2, 4, 6)
OTHER_CHIPS = (2, 4, 6)
RELAY_COLLECTIVE_ID = 0


class _InFlight:
    def __init__(self, scatter, ks, send_sems, recv_sems, srcs, lands, token):
        self.scatter, self.ks, self.send_sems, self.recv_sems = scatter, ks, send_sems, recv_sems
        self.srcs, self.lands, self.token = list(srcs), list(lands), token


def _mesh_peers(ks=EVERYONE):
    x, y, c = lax.axis_index("x"), lax.axis_index("y"), lax.axis_index("c")
    peers = []
    for k in ks:
        px, py, pc = x ^ ((k >> 2) & 1), y ^ ((k >> 1) & 1), c ^ (k & 1)
        peers.append(((px, py, pc), 4 * px + 2 * py + pc))
    return 4 * x + 2 * y + c, peers


_HBM_SPEC = pl.BlockSpec(memory_space=pltpu.HBM)
_SEM_SPEC = pl.BlockSpec(memory_space=pltpu.SEMAPHORE)
_SIDE_EFFECT = pltpu.SideEffectType.DATAFLOW_SIDE_EFFECTING


def _exchange_start(name, arrays, scatter, collective_id, ks=EVERYONE):
    n = len(arrays)
    me, _ = _mesh_peers()
    lands = []
    for a in arrays:
        own = lax.dynamic_index_in_dim(a, me, 0, keepdims=False) if scatter else a
        shape = a.shape if scatter else (N_DEV,) + a.shape
        lands.append(lax.dynamic_update_index_in_dim(lax.empty(shape, a.dtype), own, me, 0))

    def body(*refs):
        src, land = refs[:n], refs[n:2 * n]
        send_sems, recv_sems, token = refs[2 * n], refs[2 * n + 1], refs[-1]
        pos, peers = _mesh_peers(ks)
        barrier = pltpu.get_barrier_semaphore()
        for peer, _ in peers:
            pl.semaphore_signal(barrier, inc=1, device_id=peer, device_id_type=pl.DeviceIdType.MESH)
        pl.semaphore_wait(barrier, len(peers))
        for a in range(n):
            for k, (peer, peer_pos) in enumerate(peers):
                pltpu.make_async_remote_copy(
                    src_ref=src[a].at[peer_pos] if scatter else src[a], dst_ref=land[a].at[pos],
                    send_sem=send_sems.at[a * len(ks) + k], recv_sem=recv_sems.at[a * len(ks) + k],
                    device_id=peer, device_id_type=pl.DeviceIdType.MESH).start()
        token[...] = jnp.zeros_like(token)

    operands = [pltpu.with_memory_space_constraint(a, pltpu.HBM) for a in list(arrays) + lands]
    outs = pl.pallas_call(
        body, name=name,
        out_shape=(pltpu.SemaphoreType.DMA((n * len(ks),)), pltpu.SemaphoreType.DMA((n * len(ks),)),
                   *[pltpu.HBM(a.shape, a.dtype) for a in operands], _sds((8, LANES), F32)),
        in_specs=[_HBM_SPEC] * (2 * n),
        out_specs=(_SEM_SPEC, _SEM_SPEC, *[_HBM_SPEC] * (2 * n), pl.BlockSpec(memory_space=pltpu.VMEM)),
        input_output_aliases={i: 2 + i for i in range(2 * n)},
        compiler_params=pltpu.CompilerParams(has_side_effects=_SIDE_EFFECT, collective_id=collective_id),
    )(*operands)
    return _InFlight(scatter, ks, outs[0], outs[1], outs[2:2 + n], outs[2 + n:2 + 2 * n], outs[-1])


def _exchange_wait(name, flight, which, after):
    m = len(which)
    scatter, ks = flight.scatter, flight.ks

    def body(*refs):
        src, land = refs[:m], refs[m:2 * m]
        send_sems, recv_sems = refs[2 * m], refs[2 * m + 1]
        _, peers = _mesh_peers(ks)
        for i, a in enumerate(which):
            for k, (peer, peer_pos) in enumerate(peers):
                cp = pltpu.make_async_remote_copy(
                    src_ref=src[i].at[peer_pos] if scatter else src[i], dst_ref=land[i].at[peer_pos],
                    send_sem=send_sems.at[a * len(ks) + k], recv_sem=recv_sems.at[a * len(ks) + k],
                    device_id=peer, device_id_type=pl.DeviceIdType.MESH)
                cp.wait_send()
                cp.wait_recv()

    operands = [flight.srcs[a] for a in which] + [flight.lands[a] for a in which]
    outs = pl.pallas_call(
        body, name=name, out_shape=tuple(pltpu.HBM(a.shape, a.dtype) for a in operands),
        in_specs=[_HBM_SPEC] * (2 * m) + [_SEM_SPEC, _SEM_SPEC, pl.BlockSpec(memory_space=pl.ANY)],
        out_specs=tuple([_HBM_SPEC] * (2 * m)), input_output_aliases={i: i for i in range(2 * m)},
        compiler_params=pltpu.CompilerParams(has_side_effects=_SIDE_EFFECT),
    )(*operands, flight.send_sems, flight.recv_sems, after)
    return list(outs[m:])


def _relay_start(name, lands):
    n = len(lands)

    def body(*refs):
        land, send_sems, recv_sems, token = refs[:n], refs[n], refs[n + 1], refs[-1]
        _, peers = _mesh_peers(OTHER_CHIPS)
        sibling = (lax.axis_index("x"), lax.axis_index("y"), 1 - lax.axis_index("c"))
        barrier = pltpu.get_barrier_semaphore()
        pl.semaphore_signal(barrier, inc=1, device_id=sibling, device_id_type=pl.DeviceIdType.MESH)
        pl.semaphore_wait(barrier, 1)
        for a in range(n):
            for k, (_, peer_pos) in enumerate(peers):
                pltpu.make_async_remote_copy(
                    src_ref=land[a].at[peer_pos], dst_ref=land[a].at[peer_pos],
                    send_sem=send_sems.at[a * len(peers) + k], recv_sem=recv_sems.at[a * len(peers) + k],
                    device_id=sibling, device_id_type=pl.DeviceIdType.MESH).start()
        token[...] = jnp.zeros_like(token)

    count = n * len(OTHER_CHIPS)
    outs = pl.pallas_call(
        body, name=name,
        out_shape=(pltpu.SemaphoreType.DMA((count,)), pltpu.SemaphoreType.DMA((count,)),
                   *[pltpu.HBM(a.shape, a.dtype) for a in lands], _sds((8, LANES), F32)),
        in_specs=[_HBM_SPEC] * n,
        out_specs=(_SEM_SPEC, _SEM_SPEC, *[_HBM_SPEC] * n, pl.BlockSpec(memory_space=pltpu.VMEM)),
        input_output_aliases={i: 2 + i for i in range(n)},
        compiler_params=pltpu.CompilerParams(has_side_effects=_SIDE_EFFECT, collective_id=RELAY_COLLECTIVE_ID),
    )(*[pltpu.with_memory_space_constraint(a, pltpu.HBM) for a in lands])
    return _InFlight(False, OTHER_CHIPS, outs[0], outs[1], [], outs[2:2 + n], outs[-1])


def _relay_wait(name, flight, which, after):
    m = len(which)

    def body(*refs):
        land, send_sems, recv_sems = refs[:m], refs[m], refs[m + 1]
        _, peers = _mesh_peers(OTHER_CHIPS)
        sibling = (lax.axis_index("x"), lax.axis_index("y"), 1 - lax.axis_index("c"))
        for i, a in enumerate(which):
            for k, (_, peer_pos) in enumerate(peers):
                cp = pltpu.make_async_remote_copy(
                    src_ref=land[i].at[peer_pos], dst_ref=land[i].at[peer_pos ^ 1],
                    send_sem=send_sems.at[a * len(peers) + k], recv_sem=recv_sems.at[a * len(peers) + k],
                    device_id=sibling, device_id_type=pl.DeviceIdType.MESH)
                cp.wait_send()
                cp.wait_recv()

    operands = [flight.lands[a] for a in which]
    outs = pl.pallas_call(
        body, name=name, out_shape=tuple(pltpu.HBM(a.shape, a.dtype) for a in operands),
        in_specs=[_HBM_SPEC] * m + [_SEM_SPEC, _SEM_SPEC, pl.BlockSpec(memory_space=pl.ANY)],
        out_specs=tuple([_HBM_SPEC] * m), input_output_aliases={i: i for i in range(m)},
        compiler_params=pltpu.CompilerParams(has_side_effects=_SIDE_EFFECT),
    )(*operands, flight.send_sems, flight.recv_sems, after)
    return list(outs)


def _sum_parts(p_ref):
    g = p_ref[0].astype(F32)
    for dev in range(1, N_DEV):
        g = g + p_ref[dev].astype(F32)
    return g


def _adam_update(g, w, m, v):
    m_new = ADAM_B1 * m + (1.0 - ADAM_B1) * g
    v_new = ADAM_B2 * v + (1.0 - ADAM_B2) * jnp.square(g)
    m_hat = m_new / (1.0 - ADAM_B1 ** ADAM_STEP)
    v_hat = v_new / (1.0 - ADAM_B2 ** ADAM_STEP)
    return -ADAM_LR * (m_hat / (jnp.sqrt(v_hat) + ADAM_EPS) + ADAM_WD * w), m_new, v_new


def _adamw(name, parts, w, m, v, layer=None, into=None):
    r, c = w.shape[-2:]
    tr = ADAM_TILE_ELEMENTS // c
    tr = tr if r % tr == 0 else r
    n_into = 0 if into is None else len(into)

    def body(p_ref, w_ref, m_ref, v_ref, *refs):
        g_ref, d_ref, mo_ref, vo_ref = refs[n_into:]
        g = _sum_parts(p_ref)
        g_ref[...] = g
        d_ref[...], mo_ref[...], vo_ref[...] = _adam_update(g, w_ref[...], m_ref[...], v_ref[...])

    if layer is None:
        blk = pl.BlockSpec((tr, c), lambda i: (i, 0))
    else:
        blk = pl.BlockSpec((None, tr, c), lambda i: (layer, i, 0))
    return pl.pallas_call(
        body, name=name, grid=(r // tr,),
        in_specs=[pl.BlockSpec((N_DEV, tr, c), lambda i: (0, i, 0)), blk, blk, blk] + [pl.BlockSpec(memory_space=pl.ANY)] * n_into,
        out_specs=[blk] * 4, out_shape=[_sds(w.shape, F32)] * 4,
        input_output_aliases={4 + i: i for i in range(n_into)}, compiler_params=_params(1),
    )(parts, w, m, v, *(into or ()))


SMALL_PACK_ROWS = 16
LOSS_ROW = 11


def _small_rows(grads, loss_tile):
    return [(0, 1, grads["g_attn"][0]), (1, 1, grads["g_attn"][1]), (2, 1, grads["g_mlp"][0]), (3, 1, grads["g_mlp"][1]),
            (4, 1, grads["g_kv"]), (5, 1, grads["b_f"]), (6, 1, grads["gq_a"]), (7, 1, grads["gk_a"]), (8, 1, grads["gk_b"]),
            (9, 1, grads["gq_b"]), (10, 1, grads["sinks"]), (LOSS_ROW, 1, loss_tile)]


SMALL_ROWS = {"g_attn": (0, 2), "g_mlp": (2, 2), "g_kv": (4, 1), "b_f": (5, 1), "gq_a": (6, 1), "gk_a": (7, 1),
              "gk_b": (8, 1), "gq_b": (9, 1), "sinks": (10, 1)}


def _pack_small(name, pieces, d):
    def body(*refs):
        out = refs[-1]
        out[...] = jnp.zeros_like(out)
        for (row, rows, _), ref in zip(pieces, refs[:-1]):
            out[row:row + rows, 0:ref.shape[1]] = ref[0:rows, :]

    return pl.pallas_call(body, name=name, out_shape=_sds((SMALL_PACK_ROWS, d), F32), compiler_params=_params(0))(
        *[piece for _, _, piece in pieces])


def _adamw_small(name, parts, parts_rel_bias, w, m, v):
    def body(*refs):
        ins, outs = refs[2:2 + 3 * len(SMALL)], refs[2 + 3 * len(SMALL):]
        pack, rel = _sum_parts(refs[0]), _sum_parts(refs[1])
        for i, k in enumerate(SMALL):
            w_ref, m_ref, v_ref = ins[3 * i:3 * i + 3]
            if k == "rel_bias":
                g = rel
            else:
                row, rows = SMALL_ROWS[k]
                g = pack[row:row + rows, 0:w_ref.shape[1]]
            outs[4 * i][...] = g
            outs[4 * i + 1][...], outs[4 * i + 2][...], outs[4 * i + 3][...] = _adam_update(g, w_ref[...], m_ref[...], v_ref[...])
        outs[-1][...] = pack[LOSS_ROW:LOSS_ROW + 1, 0:LANES]

    operands = [parts, parts_rel_bias] + [t[k] for k in SMALL for t in (w, m, v)]
    out_shape = [_sds(w[k].shape, F32) for k in SMALL for _ in range(4)] + [_sds((1, LANES), F32)]
    outs = pl.pallas_call(body, name=name, out_shape=out_shape, compiler_params=_params(0))(*operands)
    return {k: outs[4 * i:4 * i + 4] for i, k in enumerate(SMALL)}, outs[-1]


class _Comm:
    ORDER = ("w_in_a", "w_out_a", "w_up0", "w_down0", "w_kv", "w_q_b", "w_out_b", "w_up1", "w_down1")

    def __init__(self, shards, d, n_in):
        self.d, self.n_in = d, n_in
        self.ids = iter(range(RELAY_COLLECTIVE_ID + 1, RELAY_COLLECTIVE_ID + 16))
        self.flight = _exchange_start("gather_start", [shards[n].astype(BF16) for n in self.ORDER], False, next(self.ids), SAME_CORE)
        self.relays, self.sent = {}, []

    def prefetch(self, names, after):
        which = [self.ORDER.index(n) for n in names]
        landed = _exchange_wait(f"gather_wait_{names[0]}", self.flight, which, self.flight.token if after is None else after)
        relay = _relay_start(f"gather_relay_{names[0]}", landed)
        for n in names:
            self.relays[n] = (relay, names)
        return relay.token

    def weights(self, names, after):
        relay, group = self.relays[names[0]]
        landed = _relay_wait(f"gather_relay_wait_{names[0]}", relay, [group.index(n) for n in names],
                             relay.token if after is None else after)
        return [self._whole(n, g) for n, g in zip(names, landed)]

    def _whole(self, name, g):
        if name == "w_in_a":
            return _join_row_blocks("w_in_join", g, -(-self.n_in // LANES) * LANES)
        if name.startswith("w_up"):
            return g
        return g.reshape(-1, g.shape[-1])

    def _chunks(self, name, g):
        if name == "w_in_a":
            return _split_row_blocks("dw_in_split", list(g), N_DEV, self.n_in // N_DEV)
        if name.startswith("w_up"):
            return g
        return g.reshape(N_DEV, g.shape[0] // N_DEV, g.shape[1])

    def send_grads(self, tag, partials):
        names = list(partials)
        flight = _exchange_start(f"scatter_start_{tag}", [self._chunks(n, partials[n]) for n in names], True, next(self.ids))
        self.sent.append((tag, flight, names))
        return flight.token

    def received(self, index, after):
        tag, flight, names = self.sent[index]
        landed = _exchange_wait(f"scatter_wait_{tag}", flight, list(range(len(names))), after)
        return dict(zip(names, landed))


def kernel(x, g_attn, g_mlp, w_in_a, b_f, gq_a, gk_a, w_out_a, g_kv, w_kv, gk_b, w_q_b, gq_b, sinks, rel_bias, w_out_b, w_up, w_down, loss_target, m_g_attn, m_g_mlp, m_w_in_a, m_b_f, m_gq_a, m_gk_a, m_w_out_a, m_g_kv, m_w_kv, m_gk_b, m_w_q_b, m_gq_b, m_sinks, m_rel_bias, m_w_out_b, m_w_up, m_w_down, v_g_attn, v_g_mlp, v_w_in_a, v_b_f, v_gq_a, v_gk_a, v_w_out_a, v_g_kv, v_w_kv, v_gk_b, v_w_q_b, v_gq_b, v_sinks, v_rel_bias, v_w_out_b, v_w_up, v_w_down):
    w = dict(g_attn=g_attn, g_mlp=g_mlp, w_in_a=w_in_a, b_f=b_f, gq_a=gq_a, gk_a=gk_a, w_out_a=w_out_a, g_kv=g_kv,
             w_kv=w_kv, gk_b=gk_b, w_q_b=w_q_b, gq_b=gq_b, sinks=sinks, rel_bias=rel_bias, w_out_b=w_out_b,
             w_up=w_up, w_down=w_down)
    mom = dict(g_attn=m_g_attn, g_mlp=m_g_mlp, w_in_a=m_w_in_a, b_f=m_b_f, gq_a=m_gq_a, gk_a=m_gk_a, w_out_a=m_w_out_a,
               g_kv=m_g_kv, w_kv=m_w_kv, gk_b=m_gk_b, w_q_b=m_w_q_b, gq_b=m_gq_b, sinks=m_sinks, rel_bias=m_rel_bias,
               w_out_b=m_w_out_b, w_up=m_w_up, w_down=m_w_down)
    var = dict(g_attn=v_g_attn, g_mlp=v_g_mlp, w_in_a=v_w_in_a, b_f=v_b_f, gq_a=v_gq_a, gk_a=v_gk_a, w_out_a=v_w_out_a,
               g_kv=v_g_kv, w_kv=v_w_kv, gk_b=v_gk_b, w_q_b=v_w_q_b, gq_b=v_gq_b, sinks=v_sinks, rel_bias=v_rel_bias,
               w_out_b=v_w_out_b, w_up=v_w_up, w_down=v_w_down)
    d = x.shape[2]
    where = {"w_in_a": ("w_in_a", 0), "w_out_a": ("w_out_a", 0), "w_kv": ("w_kv", None), "w_q_b": ("w_q_b", 0),
             "w_out_b": ("w_out_b", 0), "w_up0": ("w_up", 0), "w_up1": ("w_up", 1), "w_down0": ("w_down", 0),
             "w_down1": ("w_down", 1)}
    flip = lambda tree: {**tree, "w_in_a": jnp.swapaxes(tree["w_in_a"], 1, 2)}
    w, mom, var = flip(w), flip(mom), flip(var)
    shards = {n: (w[k] if layer is None else w[k][layer]) for n, (k, layer) in where.items()}
    comm = _Comm(shards, d, w_in_a.shape[2] * N_DEV)
    loss_tile, grad_x, grads = _local_step(x[0], loss_target[0], {k: w[k] for k in SMALL}, comm)

    small_flight = _exchange_start(
        "gather_small_grads", [_pack_small("pack_small", _small_rows(grads, loss_tile), d), grads["rel_bias"]], False, next(comm.ids))
    res, after = {}, small_flight.token
    for index in range(len(comm.sent)):
        for n, parts in comm.received(index, after).items():
            k, layer = where[n]
            res[k] = _adamw(f"adam_{n}", parts, w[k], mom[k], var[k], layer, res.get(k))
            after = res[k][0]
    as_rows = lambda tree: {k: tree[k] if tree[k].ndim == 2 else tree[k].reshape(1, -1) for k in SMALL}
    small, loss_row = _adamw_small("adam_small", *_exchange_wait("gather_small_wait", small_flight, [0, 1], after),
                                   as_rows(w), as_rows(mom), as_rows(var))
    loss = loss_row[0, 0]
    for k in SMALL:
        res[k] = [a.reshape(w[k].shape) for a in small[k]]
    res["w_in_a"] = [jnp.swapaxes(a, 1, 2) for a in res["w_in_a"]]

    outs = [loss, grad_x[None]]
    for i in range(4):
        outs.extend(res[k][i] for k in WEIGHTS)
    return tuple(outs)
```

```python
import numpy as np
import jax
import jax.numpy as jnp
from jax import lax
from jax.experimental import pallas as pl
from jax.experimental.pallas import tpu as pltpu

F32 = jnp.float32
BF16 = jnp.bfloat16

N_DEV = 8
HEAD_DIM = 64
WINDOW = 128
N_BUCKETS = 32
REL_MAX_DIST = 128
NORM_EPS = 1e-6
NEG = -1e30
LANES = 128
VMEM_LIMIT = 56 * 1024 * 1024

ADAM_LR = 0.001
ADAM_B1 = 0.9
ADAM_B2 = 0.999
ADAM_EPS = 1e-08
ADAM_WD = 0.01
ADAM_STEP = 10

SMALL = ("g_attn", "g_mlp", "b_f", "gq_a", "gk_a", "g_kv", "gk_b", "gq_b", "sinks", "rel_bias")
WEIGHTS = ("g_attn", "g_mlp", "w_in_a", "b_f", "gq_a", "gk_a", "w_out_a", "g_kv", "w_kv", "gk_b",
           "w_q_b", "gq_b", "sinks", "rel_bias", "w_out_b", "w_up", "w_down")


def _params(n_grid):
    return pltpu.CompilerParams(dimension_semantics=("arbitrary",) * n_grid, vmem_limit_bytes=VMEM_LIMIT)


def _sds(shape, dtype):
    return jax.ShapeDtypeStruct(tuple(shape), dtype)


def _after_operand(after):
    if after is None:
        return [], []
    return [pl.BlockSpec((8, LANES), lambda *_: (0, 0))], [after]


def _rms(x, g):
    return (x * lax.rsqrt(jnp.mean(x * x, axis=-1, keepdims=True) + NORM_EPS)) * g


def _dot_nt(a, b):
    return lax.dot_general(a, b, (((1,), (1,)), ((), ())), preferred_element_type=F32)


def _dot_tn(a, b):
    return lax.dot_general(a, b, (((0,), (0,)), ((), ())), preferred_element_type=F32)


def _dot(a, b):
    return jnp.dot(a, b, preferred_element_type=F32)


def _dot_exact(a, b):
    return jnp.dot(a, b, preferred_element_type=F32, precision=lax.Precision.HIGHEST)


def _norm_matmul(name, x, g, w, *, tn=None, relu2=False, w_rows=False, after=None):
    t, d = x.shape
    blocked = w.ndim == 3
    per_step = 2 if blocked else 1
    if blocked:
        tn = per_step * w.shape[2]
        n = w.shape[0] * w.shape[2]
        w_spec = pl.BlockSpec((per_step, d, w.shape[2]), lambda i, j: (j, 0, 0))
    elif w_rows:
        n = w.shape[0]
        w_spec = pl.BlockSpec((tn, d), lambda i, j: (j, 0))
    else:
        n = w.shape[1]
        w_spec = pl.BlockSpec((d, tn), lambda i, j: (0, j))
    tm = min(1024, t)

    def body(x_ref, g_ref, w_ref, *rest):
        y_ref, xn_ref = rest[-2:]

        @pl.when(pl.program_id(1) == 0)
        def _():
            xn_ref[...] = _rms(x_ref[...], g_ref[...]).astype(BF16)

        for b in range(per_step):
            cols = slice(b * (tn // per_step), (b + 1) * (tn // per_step)) if blocked else slice(None)
            wb = w_ref[b] if blocked else w_ref[...]
            y = _dot_nt(xn_ref[...], wb) if w_rows else _dot(xn_ref[...], wb)
            y_ref[:, cols] = jnp.square(jnp.maximum(y, 0.0)).astype(BF16) if relu2 else y

    extra_specs, extra = _after_operand(after)
    out_shape = [_sds((t, n), BF16 if relu2 else F32), _sds((t, d), BF16)]
    out_specs = [pl.BlockSpec((tm, tn), lambda i, j: (i, j)), pl.BlockSpec((tm, d), lambda i, j: (i, 0))]
    return pl.pallas_call(
        body, name=name, grid=(t // tm, n // tn),
        in_specs=[pl.BlockSpec((tm, d), lambda i, j: (i, 0)), pl.BlockSpec((1, d), lambda i, j: (0, 0)), w_spec] + extra_specs,
        out_specs=out_specs, out_shape=out_shape, compiler_params=_params(2),
    )(x, g.reshape(1, d), w, *extra)


def _matmul_res(name, a, w, res, *, tn=512, after=None):
    t, k = a.shape
    n = w.shape[1]
    tm = min(1024, t)

    def body(a_ref, w_ref, r_ref, *rest):
        rest[-1][...] = r_ref[...] + _dot(a_ref[...], w_ref[...])

    extra_specs, extra = _after_operand(after)
    return pl.pallas_call(
        body, name=name, grid=(t // tm, n // tn),
        in_specs=[pl.BlockSpec((tm, k), lambda i, j: (i, 0)), pl.BlockSpec((k, tn), lambda i, j: (0, j)),
                  pl.BlockSpec((tm, tn), lambda i, j: (i, j))] + extra_specs,
        out_specs=pl.BlockSpec((tm, tn), lambda i, j: (i, j)), out_shape=_sds((t, n), F32),
        compiler_params=_params(2),
    )(a, w, res, *extra)


def _matmul_nt(name, dy, w, *, a=None, tk=1024):
    t, n = dy.shape
    k = w.shape[0]
    tm = min(1024, t)

    def body(dy_ref, w_ref, *rest):
        o_ref = rest[-1]
        r = _dot_nt(dy_ref[...].astype(BF16), w_ref[...])
        if a is not None:
            r = r * (2.0 * jnp.sqrt(rest[0][...].astype(F32)))
        o_ref[...] = r.astype(BF16)

    in_specs = [pl.BlockSpec((tm, n), lambda i, j: (i, 0)), pl.BlockSpec((tk, n), lambda i, j: (j, 0))]
    args = [dy, w]
    if a is not None:
        in_specs.append(pl.BlockSpec((tm, tk), lambda i, j: (i, j)))
        args.append(a)
    return pl.pallas_call(
        body, name=name, grid=(t // tm, k // tk), in_specs=in_specs,
        out_specs=pl.BlockSpec((tm, tk), lambda i, j: (i, j)), out_shape=_sds((t, k), BF16),
        compiler_params=_params(2),
    )(*args)


def _matmul_nt_rmsbwd(name, dy, w, x, g, dres, *, w_rows=False, bf16_too=False, after=None):
    pieces = isinstance(dy, tuple)
    dys = list(dy) if pieces else [dy]
    t = dres.shape[0]
    blocked = w.ndim == 3
    d = w.shape[1] if blocked or w_rows else w.shape[0]
    tm = min(512, t)
    extra_specs, extra = _after_operand(after)
    n_in = len(dys) + 4 + len(extra)

    def body(*refs):
        dy_ref = refs[0]
        w_ref, x_ref, g_ref, r_ref = refs[len(dys):len(dys) + 4]
        dx_ref, dg_ref = refs[n_in:n_in + 2]
        if pieces:
            n_stack, _, k1 = dys[0].shape
            dxn = _dot(refs[1][...], w_ref[n_stack * k1:, :])
            for j in range(n_stack):
                dxn += _dot(dy_ref[j], w_ref[j * k1:(j + 1) * k1, :])
        elif blocked:
            kb = w.shape[2]
            dxn = _dot_nt(dy_ref[:, 0:kb].astype(BF16), w_ref[0])
            for j in range(1, w.shape[0]):
                dxn += _dot_nt(dy_ref[:, j * kb:(j + 1) * kb].astype(BF16), w_ref[j])
        elif w_rows:
            dxn = _dot(dy_ref[...].astype(BF16), w_ref[...])
        else:
            dxn = _dot_nt(dy_ref[...].astype(BF16), w_ref[...])
        _, vjp = jax.vjp(_rms, x_ref[...], g_ref[...])
        dx, dg = vjp(dxn)
        dx_ref[...] = r_ref[...] + dx
        if bf16_too:
            refs[n_in + 2][...] = (r_ref[...] + dx).astype(BF16)

        @pl.when(pl.program_id(0) == 0)
        def _():
            dg_ref[...] = jnp.zeros_like(dg_ref)

        dg_ref[...] += dg

    row = lambda i: (i, 0)
    fixed = lambda i: (0, 0)
    if pieces:
        dy_specs = [pl.BlockSpec((dys[0].shape[0], tm, dys[0].shape[2]), lambda i: (0, i, 0)),
                    pl.BlockSpec((tm, dys[1].shape[1]), row)]
    else:
        dy_specs = [pl.BlockSpec((tm, dy.shape[1]), row)]
    outs = pl.pallas_call(
        body, name=name, grid=(t // tm,),
        in_specs=dy_specs + [pl.BlockSpec(w.shape, (lambda i: (0, 0, 0)) if blocked else fixed),
                             pl.BlockSpec((tm, d), row), pl.BlockSpec((1, d), fixed), pl.BlockSpec((tm, d), row)] + extra_specs,
        out_specs=[pl.BlockSpec((tm, d), row), pl.BlockSpec((1, d), fixed)] + [pl.BlockSpec((tm, d), row)] * bf16_too,
        out_shape=[_sds((t, d), F32), _sds((1, d), F32)] + [_sds((t, d), BF16)] * bf16_too, compiler_params=_params(1),
    )(*dys, w, x, g.reshape(1, d), dres, *extra)
    return (outs[0], outs[2], outs[1]) if bf16_too else outs


def _matmul_tn(name, a, b, *, tk=1024, tn=1024, col_blocks=None):
    stacked = a.ndim == 3
    t, k1 = a.shape[-2:]
    k = a.shape[0] * k1 if stacked else k1
    n = b.shape[1]
    tk = min(tk, k1)
    per = k1 // tk
    a_spec = (pl.BlockSpec((None, t, tk), lambda i, j: (i // per, 0, i % per)) if stacked
              else pl.BlockSpec((t, tk), lambda i, j: (0, i)))
    if col_blocks:
        tn = n // col_blocks
        out_spec, out_shape = pl.BlockSpec((None, tk, tn), lambda i, j: (j, i, 0)), _sds((col_blocks, k, tn), BF16)
    else:
        tn = min(tn, n)
        out_spec, out_shape = pl.BlockSpec((tk, tn), lambda i, j: (i, j)), _sds((k, n), BF16)

    def body(a_ref, b_ref, o_ref):
        o_ref[...] = _dot_tn(a_ref[...].astype(BF16), b_ref[...].astype(BF16)).astype(BF16)

    return pl.pallas_call(
        body, name=name, grid=(k // tk, n // tn),
        in_specs=[a_spec, pl.BlockSpec((t, tn), lambda i, j: (0, j))],
        out_specs=out_spec, out_shape=out_shape, compiler_params=_params(2),
    )(a, b)


def _join_row_blocks(name, blocks, rows):
    b, r, c = blocks.shape
    tc = min(256, c)

    def body(g_ref, o_ref):
        o_ref[...] = jnp.zeros_like(o_ref)
        for j in range(b):
            o_ref[r * j:r * (j + 1), :] = g_ref[j]

    return pl.pallas_call(
        body, name=name, grid=(c // tc,), in_specs=[pl.BlockSpec((b, r, tc), lambda i: (0, 0, i))],
        out_specs=pl.BlockSpec((rows, tc), lambda i: (0, i)), out_shape=_sds((rows, c), blocks.dtype),
        compiler_params=_params(1),
    )(blocks)


def _split_row_blocks(name, mats, b, r):
    c = mats[0].shape[1]
    tc = min(256, c)

    def body(*refs):
        o_ref = refs[-1]
        for j in range(b):
            first = 0
            for m_ref in refs[:-1]:
                lo, hi = max(r * j, first), min(r * (j + 1), first + m_ref.shape[0])
                if lo < hi:
                    o_ref[j, lo - r * j:hi - r * j, :] = m_ref[lo - first:hi - first, :]
                first += m_ref.shape[0]

    return pl.pallas_call(
        body, name=name, grid=(c // tc,), in_specs=[pl.BlockSpec((m.shape[0], tc), lambda i: (0, i)) for m in mats],
        out_specs=pl.BlockSpec((b, r, tc), lambda i: (0, 0, i)), out_shape=_sds((b, r, c), mats[0].dtype),
        compiler_params=_params(1),
    )(*mats)


def _matmul_res_loss(name, a, w, res, target, *, tn=512):
    t, k = a.shape
    n = w.shape[1]
    tm = min(1024, t)

    def body(a_ref, w_ref, r_ref, t_ref, dy_ref, dyb_ref, l_ref):
        e = r_ref[...] + _dot(a_ref[...], w_ref[...]) - t_ref[...]
        dy_ref[...] = e * (1.0 / n)
        dyb_ref[...] = (e * (1.0 / n)).astype(BF16)

        @pl.when((pl.program_id(0) == 0) & (pl.program_id(1) == 0))
        def _():
            l_ref[...] = jnp.zeros_like(l_ref)

        l_ref[...] += (0.5 / n) * jnp.sum(e * e)

    tile = pl.BlockSpec((tm, tn), lambda i, j: (i, j))
    return pl.pallas_call(
        body, name=name, grid=(t // tm, n // tn),
        in_specs=[pl.BlockSpec((tm, k), lambda i, j: (i, 0)), pl.BlockSpec((k, tn), lambda i, j: (0, j)), tile, tile],
        out_specs=[tile, tile, pl.BlockSpec((8, LANES), lambda i, j: (0, 0))],
        out_shape=[_sds((t, n), F32), _sds((t, n), BF16), _sds((8, LANES), F32)], compiler_params=_params(2),
    )(a, w, res, target)


def _gate_fwd(name, proj, b_pad, n_heads, gate_col, after=None):
    t = proj.shape[0]
    tb = min(256, t)
    tri = jnp.asarray(np.tril(np.ones((tb, tb), np.float32)))
    extra_specs, extra = _after_operand(after)

    def body(p_ref, b_ref, tri_ref, *rest):
        c_ref, carry = rest[-2:]

        @pl.when(pl.program_id(0) == 0)
        def _():
            carry[...] = jnp.zeros_like(carry)

        lane = lax.broadcasted_iota(jnp.int32, (tb, LANES), 1)
        lf = jnp.where(lane < n_heads, jax.nn.log_sigmoid(p_ref[...] + b_ref[...]), 0.0)
        c = _dot_exact(tri_ref[...], lf) + carry[0:1, :]
        c_ref[...] = c
        carry[...] = jnp.broadcast_to(c[tb - 1:tb, :], carry.shape)

    return pl.pallas_call(
        body, name=name, grid=(t // tb,),
        in_specs=[pl.BlockSpec((tb, LANES), lambda i: (i, gate_col)), pl.BlockSpec((1, LANES), lambda i: (0, 0)),
                  pl.BlockSpec((tb, tb), lambda i: (0, 0))] + extra_specs,
        out_specs=pl.BlockSpec((tb, LANES), lambda i: (i, 0)), out_shape=_sds((t, LANES), F32),
        scratch_shapes=[pltpu.VMEM((8, LANES), F32)], compiler_params=_params(1),
    )(proj, b_pad, tri, *extra)


def _gate_bwd(name, proj, b_pad, dc, n_heads, gate_col):
    t = proj.shape[0]
    tb = min(256, t)
    nb = t // tb
    triu = jnp.asarray(np.triu(np.ones((tb, tb), np.float32)))

    def body(p_ref, b_ref, dc_ref, tri_ref, df_ref, db_ref, carry):
        @pl.when(pl.program_id(0) == 0)
        def _():
            carry[...] = jnp.zeros_like(carry)
            db_ref[...] = jnp.zeros_like(db_ref)

        dcv = dc_ref[...]
        dlf = _dot_exact(tri_ref[...], dcv) + carry[0:1, :]
        carry[...] = jnp.broadcast_to(dlf[0:1, :], carry.shape)
        lane = lax.broadcasted_iota(jnp.int32, (tb, LANES), 1)
        z = p_ref[...] + b_ref[...]
        df = jnp.where(lane < n_heads, dlf / (1.0 + jnp.exp(z)), 0.0)
        df_ref[...] = df.astype(BF16)
        db_ref[...] += jnp.sum(df, axis=0, keepdims=True)

    return pl.pallas_call(
        body, name=name, grid=(nb,),
        in_specs=[pl.BlockSpec((tb, LANES), lambda i: (nb - 1 - i, gate_col)), pl.BlockSpec((1, LANES), lambda i: (0, 0)),
                  pl.BlockSpec((tb, LANES), lambda i: (nb - 1 - i, 0)), pl.BlockSpec((tb, tb), lambda i: (0, 0))],
        out_specs=[pl.BlockSpec((tb, LANES), lambda i: (nb - 1 - i, 0)), pl.BlockSpec((1, LANES), lambda i: (0, 0))],
        out_shape=[_sds((t, LANES), BF16), _sds((1, LANES), F32)],
        scratch_shapes=[pltpu.VMEM((8, LANES), F32)], compiler_params=_params(1),
    )(proj, b_pad, dc, triu)


def _qhead(qp, g):
    return _rms(qp, g) * (HEAD_DIM ** -0.5)


def _column(mat, idx):
    lane = lax.broadcasted_iota(jnp.int32, mat.shape, 1)
    return jnp.sum(jnp.where(lane == idx, mat, 0.0), axis=1, keepdims=True)


def _fox_scores(kk, qi, ckey, cq_i, i, bq):
    length = kk.shape[0]
    s = _dot_nt(kk, qi) + cq_i - ckey[:length]
    key = lax.broadcasted_iota(jnp.int32, (length, bq), 0)
    qry = lax.broadcasted_iota(jnp.int32, (length, bq), 1) + i * bq
    return jnp.where(key <= qry, s, NEG)


def _fox_fwd(name, proj, c, crow, gq, gk, n_heads):
    t = proj.shape[0]
    hw = n_heads * HEAD_DIM
    npair = n_heads // 2
    bq = min(512, t)
    nq = t // bq

    def body(q_ref, k_ref, v_ref, c_ref, crow_ref, gq_ref, gk_ref, o_ref, lse_ref):
        hp = pl.program_id(0)
        lse_ref[...] = jnp.zeros_like(lse_ref)
        outs = []
        for hh in range(2):
            sl = slice(hh * HEAD_DIM, (hh + 1) * HEAD_DIM)
            qn = _qhead(q_ref[:, sl], gq_ref[...]).astype(BF16)
            kn = _rms(k_ref[:, sl], gk_ref[...]).astype(BF16)
            v_t = v_ref[:, sl].T.astype(BF16)
            ckey = _column(c_ref[...], 2 * hp + hh)
            cq = crow_ref[0, hh:hh + 1, :]
            o_blocks = []
            for i in range(nq):
                cols = slice(i * bq, (i + 1) * bq)
                length = (i + 1) * bq
                s = _fox_scores(kn[:length], qn[cols], ckey, cq[:, cols], i, bq)
                m = jnp.max(s, axis=0, keepdims=True)
                p = jnp.exp(s - m)
                l = jnp.sum(p, axis=0, keepdims=True)
                o_blocks.append((_dot(v_t[:, :length], p.astype(BF16)) / l).T)
                lse_ref[0, hh:hh + 1, cols] = m + jnp.log(l)
            outs.append(jnp.concatenate(o_blocks, axis=0))
        o_ref[...] = jnp.concatenate(outs, axis=1).astype(BF16)

    col = lambda off: (lambda h: (0, off + h))
    fixed = lambda h: (0, 0)
    return pl.pallas_call(
        body, name=name, grid=(npair,),
        in_specs=[pl.BlockSpec((t, LANES), col(0)), pl.BlockSpec((t, LANES), col(npair)), pl.BlockSpec((t, LANES), col(2 * npair)),
                  pl.BlockSpec((t, LANES), fixed), pl.BlockSpec((1, 2, t), lambda h: (h, 0, 0)),
                  pl.BlockSpec((1, HEAD_DIM), fixed), pl.BlockSpec((1, HEAD_DIM), fixed)],
        out_specs=[pl.BlockSpec((t, LANES), col(0)), pl.BlockSpec((1, 8, t), lambda h: (h, 0, 0))],
        out_shape=[_sds((t, hw), BF16), _sds((npair, 8, t), F32)], compiler_params=_params(1),
    )(proj, proj, proj, c, crow, gq, gk)


def _fox_bwd(name, proj, c, crow, gq, gk, lse, do, n_heads, after=None):
    t = proj.shape[0]
    hw = n_heads * HEAD_DIM
    npair = n_heads // 2
    bq = min(256, t)
    nq = t // bq

    def body(q_ref, k_ref, v_ref, c_ref, crow_ref, gq_ref, gk_ref, lse_ref, do_ref, *rest):
        dqkv_ref, dc_ref, dgq_ref, dgk_ref, dk_acc, dv_acc, dc_acc = rest[-7:]
        hp = pl.program_id(0)

        @pl.when(hp == 0)
        def _():
            dgq_ref[...] = jnp.zeros_like(dgq_ref)
            dgk_ref[...] = jnp.zeros_like(dgk_ref)
            dc_ref[...] = jnp.zeros_like(dc_ref)

        lane = lax.broadcasted_iota(jnp.int32, (t, LANES), 1)
        dqs, dks, dvs = [], [], []
        for hh in range(2):
            sl = slice(hh * HEAD_DIM, (hh + 1) * HEAD_DIM)
            qf, q_vjp = jax.vjp(_qhead, q_ref[:, sl], gq_ref[...])
            kf, k_vjp = jax.vjp(_rms, k_ref[:, sl], gk_ref[...])
            qn, kn, kn_t = qf.astype(BF16), kf.astype(BF16), kf.T.astype(BF16)
            vb = v_ref[:, sl].astype(BF16)
            dob = do_ref[:, sl]
            ckey = _column(c_ref[...], 2 * hp + hh)
            cq = crow_ref[0, hh:hh + 1, :]
            lse_h = lse_ref[0, hh:hh + 1, :]
            dk_acc[...] = jnp.zeros_like(dk_acc)
            dv_acc[...] = jnp.zeros_like(dv_acc)
            dc_acc[...] = jnp.zeros_like(dc_acc)
            dq_blocks = []
            for i in range(nq):
                cols = slice(i * bq, (i + 1) * bq)
                length = (i + 1) * bq
                qi, doi = qn[cols], dob[cols]
                s = _fox_scores(kn[:length], qi, ckey, cq[:, cols], i, bq)
                p = jnp.exp(s - lse_h[:, cols])
                dp = _dot_nt(vb[:length], doi)
                ds = p * (dp - jnp.sum(p * dp, axis=0, keepdims=True))
                dsb = ds.astype(BF16)
                dq_blocks.append(_dot(kn_t[:, :length], dsb).T)
                dk_acc[0:length, :] += _dot(dsb, qi)
                dv_acc[0:length, :] += _dot(p.astype(BF16), doi)
                part = ds[:, 0:LANES]
                for j in range(1, bq // LANES):
                    part = part + ds[:, j * LANES:(j + 1) * LANES]
                dc_acc[0:length, :] += part
            dqp, dgq = q_vjp(jnp.concatenate(dq_blocks, axis=0))
            dkp, dgk = k_vjp(dk_acc[...])
            dgq_ref[...] += dgq
            dgk_ref[...] += dgk
            dqs.append(dqp)
            dks.append(dkp)
            dvs.append(dv_acc[...])
            dc_ref[...] = jnp.where(lane == 2 * hp + hh, -jnp.sum(dc_acc[...], axis=1, keepdims=True), dc_ref[...])
        for part, halves in enumerate((dqs, dks, dvs)):
            dqkv_ref[part] = jnp.concatenate(halves, axis=1).astype(BF16)

    col = lambda off: (lambda h: (0, off + h))
    fixed = lambda h: (0, 0)
    pair_blk = pl.BlockSpec((t, LANES), col(0))
    extra_specs, extra = _after_operand(after)
    return pl.pallas_call(
        body, name=name, grid=(npair,),
        in_specs=[pl.BlockSpec((t, LANES), col(0)), pl.BlockSpec((t, LANES), col(npair)), pl.BlockSpec((t, LANES), col(2 * npair)),
                  pl.BlockSpec((t, LANES), fixed), pl.BlockSpec((1, 2, t), lambda h: (h, 0, 0)),
                  pl.BlockSpec((1, HEAD_DIM), fixed), pl.BlockSpec((1, HEAD_DIM), fixed),
                  pl.BlockSpec((1, 8, t), lambda h: (h, 0, 0)), pair_blk] + extra_specs,
        out_specs=[pl.BlockSpec((3, t, LANES), lambda h: (0, 0, h)), pl.BlockSpec((t, LANES), fixed),
                   pl.BlockSpec((1, HEAD_DIM), fixed), pl.BlockSpec((1, HEAD_DIM), fixed)],
        out_shape=[_sds((3, t, hw), BF16), _sds((t, LANES), F32),
                   _sds((1, HEAD_DIM), F32), _sds((1, HEAD_DIM), F32)],
        scratch_shapes=[pltpu.VMEM((t, HEAD_DIM), F32), pltpu.VMEM((t, HEAD_DIM), F32), pltpu.VMEM((t, LANES), F32)],
        compiler_params=_params(1),
    )(proj, proj, proj, c, crow, gq, gk, lse, do, *extra)


def _t5_bucket_table():
    dist = np.arange(WINDOW)[None, :] + WINDOW - np.arange(2 * WINDOW)[:, None]
    n = np.maximum(dist, 0)
    max_exact = N_BUCKETS // 2
    large = max_exact + (np.log(np.maximum(n, 1) / max_exact) / np.log(REL_MAX_DIST / max_exact)
                         * (N_BUCKETS - max_exact)).astype(np.int32)
    large = np.minimum(large, N_BUCKETS - 1)
    return np.where(n < max_exact, n, large).astype(np.int32).reshape(1, -1)


def _bias_expand(name, rel_bias_t):
    n_heads = rel_bias_t.shape[0]
    tbl = jnp.asarray(_t5_bucket_table())
    width = tbl.shape[1]

    def body(rb_ref, tbl_ref, o_ref):
        onehot = (lax.broadcasted_iota(jnp.int32, (N_BUCKETS, width), 0) == tbl_ref[...]).astype(F32)
        o_ref[...] = _dot_exact(rb_ref[...], onehot)

    return pl.pallas_call(body, name=name, out_shape=_sds((n_heads, width), F32), compiler_params=_params(0))(rel_bias_t, tbl)


def _bias_reduce(name, dbias):
    n_heads, width = dbias.shape
    tbl = jnp.asarray(_t5_bucket_table())

    def body(db_ref, tbl_ref, o_ref):
        onehot = (lax.broadcasted_iota(jnp.int32, (N_BUCKETS, width), 0) == tbl_ref[...]).astype(F32)
        o_ref[...] = lax.dot_general(db_ref[...], onehot, (((1,), (1,)), ((), ())), preferred_element_type=F32,
                                     precision=lax.Precision.HIGHEST)

    return pl.pallas_call(body, name=name, out_shape=_sds((n_heads, N_BUCKETS), F32), compiler_params=_params(0))(dbias, tbl)


def _swa_mask(n, group):
    j = lax.broadcasted_iota(jnp.int32, (2 * WINDOW, group * WINDOW), 0)
    i = lax.broadcasted_iota(jnp.int32, (2 * WINDOW, group * WINDOW), 1) & (WINDOW - 1)
    ok = (j > i) & (j <= i + WINDOW) & ((n > 0) | (j >= WINDOW))
    return jnp.where(ok, 0.0, NEG)


def _swa_stack(ref, start, group):
    return jnp.concatenate([ref[pl.ds(start, WINDOW), g * HEAD_DIM:(g + 1) * HEAD_DIM] for g in range(group)], axis=0)


def _kv_head(ref, n_kv):
    out = ref[:, 0:HEAD_DIM]
    for h in range(1, n_kv):
        out = jnp.where(pl.program_id(0) == h, ref[:, h * HEAD_DIM:(h + 1) * HEAD_DIM], out)
    return out


def _swa_fwd(name, qb, kv, gq, gk, sinks, bias, group):
    t = qb.shape[0]
    kvh = kv.shape[1] // (2 * HEAD_DIM)
    nblk = t // WINDOW
    gw = group * HEAD_DIM
    band = 2 * WINDOW
    cols = group * WINDOW

    def body(q_ref, k_ref, v_ref, gq_ref, gk_ref, sink_ref, bias_ref, o_ref, lse_ref, qs, kpad, vpad):
        for g in range(group):
            qs[:, g * HEAD_DIM:(g + 1) * HEAD_DIM] = _qhead(q_ref[:, g * HEAD_DIM:(g + 1) * HEAD_DIM], gq_ref[...]).astype(BF16)
        kpad[0:WINDOW, :] = jnp.zeros((WINDOW, HEAD_DIM), BF16)
        vpad[0:WINDOW, :] = jnp.zeros((WINDOW, HEAD_DIM), BF16)
        kpad[WINDOW:, :] = _rms(_kv_head(k_ref, kvh), gk_ref[...]).astype(BF16)
        vpad[WINDOW:, :] = _kv_head(v_ref, kvh).astype(BF16)
        sink = sink_ref[0]

        def block(n, carry):
            start = pl.multiple_of(n * WINDOW, WINDOW)
            kb = kpad[pl.ds(start, band), :]
            vb = vpad[pl.ds(start, band), :]
            s = _dot_nt(kb, _swa_stack(qs, start, group)) + bias_ref[0] + _swa_mask(n, group)
            m = jnp.maximum(jnp.max(s, axis=0, keepdims=True), sink)
            e = jnp.exp(s - m)
            l = jnp.sum(e, axis=0, keepdims=True) + jnp.exp(sink - m)
            o_t = _dot_tn(vb, e.astype(BF16)) / l
            for g in range(group):
                o_ref[pl.ds(start, WINDOW), g * HEAD_DIM:(g + 1) * HEAD_DIM] = o_t[:, g * WINDOW:(g + 1) * WINDOW].T.astype(BF16)
            lse_ref[pl.ds(n, 1), :] = m + jnp.log(l)
            return carry

        lax.fori_loop(0, nblk, block, 0)

    fixed = lambda h: (0, 0)
    per = lambda h: (h, 0, 0)
    return pl.pallas_call(
        body, name=name, grid=(kvh,),
        in_specs=[pl.BlockSpec((t, gw), lambda h: (0, h)), pl.BlockSpec((t, kvh * HEAD_DIM), lambda h: (0, 0)),
                  pl.BlockSpec((t, kvh * HEAD_DIM), lambda h: (0, 1)),
                  pl.BlockSpec((1, HEAD_DIM), fixed), pl.BlockSpec((1, HEAD_DIM), fixed),
                  pl.BlockSpec((1, 1, cols), per), pl.BlockSpec((1, band, cols), per)],
        out_specs=[pl.BlockSpec((t, gw), lambda h: (0, h)), pl.BlockSpec((nblk, cols), lambda h: (h, 0))],
        out_shape=[_sds((t, kvh * gw), BF16), _sds((kvh * nblk, cols), F32)],
        scratch_shapes=[pltpu.VMEM((t, gw), BF16), pltpu.VMEM((t + WINDOW, HEAD_DIM), BF16),
                        pltpu.VMEM((t + WINDOW, HEAD_DIM), BF16)],
        compiler_params=_params(1),
    )(qb, kv, kv, gq, gk, sinks, bias)


def _swa_bwd(name, qb, kv, gq, gk, sinks, bias, lse, do, group):
    t = qb.shape[0]
    kvh = kv.shape[1] // (2 * HEAD_DIM)
    kvw = kvh * HEAD_DIM
    nblk = t // WINDOW
    gw = group * HEAD_DIM
    band = 2 * WINDOW
    cols = group * WINDOW

    def body(q_ref, k_ref, v_ref, gq_ref, gk_ref, sink_ref, bias_ref, lse_ref, do_ref,
             dq_ref, dkv_ref, dgq_ref, dgk_ref, dsink_ref, dbias_ref,
             qs, kpad, vpad, dqs, dk_acc, dv_acc, dsink_acc):
        @pl.when(pl.program_id(0) == 0)
        def _():
            dgq_ref[...] = jnp.zeros_like(dgq_ref)
            dgk_ref[...] = jnp.zeros_like(dgk_ref)
            dkv_ref[...] = jnp.zeros_like(dkv_ref)

        for g in range(group):
            qs[:, g * HEAD_DIM:(g + 1) * HEAD_DIM] = _qhead(q_ref[:, g * HEAD_DIM:(g + 1) * HEAD_DIM], gq_ref[...]).astype(BF16)
        kpad[0:WINDOW, :] = jnp.zeros((WINDOW, HEAD_DIM), BF16)
        vpad[0:WINDOW, :] = jnp.zeros((WINDOW, HEAD_DIM), BF16)
        kpad[WINDOW:, :] = _rms(_kv_head(k_ref, kvh), gk_ref[...]).astype(BF16)
        vpad[WINDOW:, :] = _kv_head(v_ref, kvh).astype(BF16)
        dk_acc[...] = jnp.zeros_like(dk_acc)
        dv_acc[...] = jnp.zeros_like(dv_acc)
        dsink_acc[...] = jnp.zeros_like(dsink_acc)
        dbias_ref[...] = jnp.zeros_like(dbias_ref)
        sink = sink_ref[0]

        def block(n, carry):
            start = pl.multiple_of(n * WINDOW, WINDOW)
            kb = kpad[pl.ds(start, band), :]
            vb = vpad[pl.ds(start, band), :]
            q = _swa_stack(qs, start, group)
            dob = _swa_stack(do_ref, start, group)
            lse_n = lse_ref[pl.ds(n, 1), :]
            s = _dot_nt(kb, q) + bias_ref[0] + _swa_mask(n, group)
            p = jnp.exp(s - lse_n)
            dp = _dot_nt(vb, dob)
            dsum = jnp.sum(p * dp, axis=0, keepdims=True)
            ds = p * (dp - dsum)
            dsb = ds.astype(BF16)
            dsink_acc[...] -= jnp.exp(sink - lse_n) * dsum
            dbias_ref[0] += ds
            dq = _dot_tn(dsb, kb)
            for g in range(group):
                dqs[pl.ds(start, WINDOW), g * HEAD_DIM:(g + 1) * HEAD_DIM] = dq[g * WINDOW:(g + 1) * WINDOW]
            dk_acc[pl.ds(start, band), :] += _dot(dsb, q)
            dv_acc[pl.ds(start, band), :] += _dot(p.astype(BF16), dob)
            return carry

        lax.fori_loop(0, nblk, block, 0)
        for g in range(group):
            _, q_vjp = jax.vjp(_qhead, q_ref[:, g * HEAD_DIM:(g + 1) * HEAD_DIM], gq_ref[...])
            dqp, dgq = q_vjp(dqs[:, g * HEAD_DIM:(g + 1) * HEAD_DIM])
            dq_ref[:, g * HEAD_DIM:(g + 1) * HEAD_DIM] = dqp.astype(BF16)
            dgq_ref[...] += dgq
            dsink_g = jnp.sum(dsink_acc[:, g * WINDOW:(g + 1) * WINDOW], axis=1, keepdims=True)
            dsink_ref[0, g:g + 1, :] = jnp.broadcast_to(dsink_g, (1, LANES))
        _, k_vjp = jax.vjp(_rms, _kv_head(k_ref, kvh), gk_ref[...])
        dkp, dgk = k_vjp(dk_acc[WINDOW:, :])
        dgk_ref[...] += dgk
        mine = lax.broadcasted_iota(jnp.int32, (t, kvw), 1) // HEAD_DIM == pl.program_id(0)
        dkv_ref[:, 0:kvw] = jnp.where(mine, jnp.concatenate([dkp] * kvh, axis=1), dkv_ref[:, 0:kvw])
        dkv_ref[:, kvw:] = jnp.where(mine, jnp.concatenate([dv_acc[WINDOW:, :]] * kvh, axis=1), dkv_ref[:, kvw:])

    fixed = lambda h: (0, 0)
    per = lambda h: (h, 0, 0)
    wide = pl.BlockSpec((t, gw), lambda h: (0, h))
    vec = pl.BlockSpec((1, HEAD_DIM), fixed)
    bias_spec = pl.BlockSpec((1, band, cols), per)
    return pl.pallas_call(
        body, name=name, grid=(kvh,),
        in_specs=[wide, pl.BlockSpec((t, kvw), lambda h: (0, 0)), pl.BlockSpec((t, kvw), lambda h: (0, 1)), vec, vec,
                  pl.BlockSpec((1, 1, cols), per), bias_spec, pl.BlockSpec((nblk, cols), lambda h: (h, 0)), wide],
        out_specs=[wide, pl.BlockSpec((t, 2 * kvw), fixed), vec, vec, pl.BlockSpec((1, group, LANES), per), bias_spec],
        out_shape=[_sds((t, kvh * gw), BF16), _sds((t, 2 * kvw), F32),
                   _sds((1, HEAD_DIM), F32), _sds((1, HEAD_DIM), F32),
                   _sds((kvh, group, LANES), F32), _sds((kvh, band, cols), F32)],
        scratch_shapes=[pltpu.VMEM((t, gw), BF16), pltpu.VMEM((t + WINDOW, HEAD_DIM), BF16),
                        pltpu.VMEM((t + WINDOW, HEAD_DIM), BF16), pltpu.VMEM((t, gw), F32),
                        pltpu.VMEM((t + WINDOW, HEAD_DIM), F32), pltpu.VMEM((t + WINDOW, HEAD_DIM), F32),
                        pltpu.VMEM((1, cols), F32)],
        compiler_params=_params(1),
    )(qb, kv, kv, gq, gk, sinks, bias, lse, do)


def _local_step(x, target, p, comm):
    t, d = x.shape
    n_heads = d // HEAD_DIM
    kv_heads = n_heads // 8
    group = n_heads // kv_heads
    hw = n_heads * HEAD_DIM
    gate_col = 3 * hw // LANES
    kvw = kv_heads * HEAD_DIM
    grads = {}

    def mlp_fwd(tag, h, g, layer, last=False):
        w_up, = comm.weights([f"w_up{layer}"], h)
        a, hn = _norm_matmul(f"{tag}_up", h, g, w_up, relu2=True)
        w_down, = comm.weights([f"w_down{layer}"], a)
        out = _matmul_res_loss(f"{tag}_down", a, w_down, h, target) if last else _matmul_res(f"{tag}_down", a, w_down, h)
        return out, (h, g, hn, a, w_up, w_down)

    def mlp_bwd(tag, saved, layer, dy, dy16):
        h, g, hn, a, w_up, w_down = saved
        du = _matmul_nt(f"{tag}_du", dy16, w_down, a=a)
        dw_down = _matmul_tn(f"{tag}_dwdown", a, dy16)
        dw_up = _matmul_tn(f"{tag}_dwup", hn, du, col_blocks=w_up.shape[0])
        sent = comm.send_grads(tag, {f"w_down{layer}": dw_down, f"w_up{layer}": dw_up})
        return _matmul_nt_rmsbwd(f"{tag}_dh", du, w_up, h, g, dy, bf16_too=True, after=sent)

    bias = _bias_expand("b_bias", p["rel_bias"].T).reshape(kv_heads, group, 2 * WINDOW, WINDOW)
    bias = bias.transpose(0, 2, 1, 3).reshape(kv_heads, 2 * WINDOW, group * WINDOW)
    comm.prefetch(["w_in_a"], bias)
    w_in, = comm.weights(["w_in_a"], None)
    proj, xn1 = _norm_matmul("a_inproj", x, p["g_attn"][0], w_in, tn=640, w_rows=True)
    ahead = comm.prefetch(["w_out_a"], proj)
    b_pad = jnp.pad(p["b_f"], ((0, 0), (0, LANES - n_heads)))
    c = _gate_fwd("a_gate", proj, b_pad, n_heads, gate_col, after=ahead)
    crow = c[:, :n_heads].T.reshape(n_heads // 2, 2, t)
    o_a, lse_a = _fox_fwd("a_attn", proj, c, crow, p["gq_a"], p["gk_a"], n_heads)
    ahead = comm.prefetch(["w_up0", "w_down0", "w_kv", "w_q_b", "w_out_b"], o_a)
    w_out_a, = comm.weights(["w_out_a"], o_a)
    h1 = _matmul_res("a_outproj", o_a, w_out_a, x, after=ahead)
    h2, mlp0 = mlp_fwd("mlp0", h1, p["g_mlp"][0], 0)

    ahead = comm.prefetch(["w_up1", "w_down1"], h2)
    w_kv, w_q_b = comm.weights(["w_kv", "w_q_b"], h2)
    kv, hn_kv = _norm_matmul("kv_proj", h2, p["g_kv"], w_kv, tn=2 * kvw, after=ahead)
    qb, hn_q = _norm_matmul("b_qproj", h2, p["g_attn"][1], w_q_b, tn=512)
    gqb, gkb = p["gq_b"], p["gk_b"].reshape(1, HEAD_DIM)
    sink_rows = jnp.broadcast_to(p["sinks"].reshape(kv_heads, 1, group, 1), (kv_heads, 1, group, WINDOW)).reshape(kv_heads, 1, group * WINDOW)
    o_b, lse_b = _swa_fwd("b_attn", qb, kv, gqb, gkb, sink_rows, bias, group)
    w_out_b, = comm.weights(["w_out_b"], o_b)
    h3 = _matmul_res("b_outproj", o_b, w_out_b, h2)
    (dy, dy16, loss_tile), mlp1 = mlp_fwd("mlp1", h3, p["g_mlp"][1], 1, last=True)

    dh3, dh3_16, dg_mlp1 = mlp_bwd("mlp1", mlp1, 1, dy, dy16)
    do_b = _matmul_nt("b_do", dh3_16, w_out_b)
    dw_out_b = _matmul_tn("b_dwout", o_b, dh3_16)
    dqb, dkv, grads["gq_b"], dgk_b, dsink, dbias = _swa_bwd(
        "b_attn_bwd", qb, kv, gqb, gkb, sink_rows, bias, lse_b, do_b, group)
    grads["gk_b"] = dgk_b
    grads["sinks"] = dsink[:, :, 0].reshape(1, n_heads)
    dbias = dbias.reshape(kv_heads, 2 * WINDOW, group, WINDOW).transpose(0, 2, 1, 3)
    grads["rel_bias"] = _bias_reduce("b_dbias", dbias.reshape(n_heads, WINDOW * 2 * WINDOW)).T
    dw_q_b = _matmul_tn("b_dwq", hn_q, dqb)
    dh2, dg_attn1 = _matmul_nt_rmsbwd("b_dhq", dqb, w_q_b, h2, p["g_attn"][1], dh3)
    dw_kv = _matmul_tn("kv_dw", hn_kv, dkv)
    sent = comm.send_grads("attn_b", {"w_out_b": dw_out_b, "w_q_b": dw_q_b, "w_kv": dw_kv})
    dh2, dh2_16, dg_kv = _matmul_nt_rmsbwd("kv_dh", dkv, w_kv, h2, p["g_kv"], dh2, bf16_too=True, after=sent)
    grads["g_kv"] = dg_kv
    dh1, dh1_16, dg_mlp0 = mlp_bwd("mlp0", mlp0, 0, dh2, dh2_16)
    grads["g_mlp"] = (dg_mlp0, dg_mlp1)

    do_a = _matmul_nt("a_do", dh1_16, w_out_a)
    dw_out_a = _matmul_tn("a_dwout", o_a, dh1_16)
    sent = comm.send_grads("attn_a_out", {"w_out_a": dw_out_a})
    dqkv, dc, grads["gq_a"], grads["gk_a"] = _fox_bwd(
        "a_attn_bwd", proj, c, crow, p["gq_a"], p["gk_a"], lse_a, do_a, n_heads, after=sent)
    dfl, db_f = _gate_bwd("a_gate_bwd", proj, b_pad, dc, n_heads, gate_col)
    grads["b_f"] = db_f
    dw_in = (_matmul_tn("a_dwin", dqkv, xn1, tk=512), _matmul_tn("a_dwin_gate", dfl, xn1))
    sent = comm.send_grads("attn_a_in", {"w_in_a": dw_in})
    grad_x, dg_attn0 = _matmul_nt_rmsbwd("a_dx", (dqkv, dfl), w_in, x, p["g_attn"][0], dh1, w_rows=True, after=sent)
    grads["g_attn"] = (dg_attn0, dg_attn1)
    return loss_tile, grad_x, grads


EVERYONE = (1, 2, 3, 4, 5, 6, 7)
SAME_CORE = (1, 2, 4, 6)
OTHER_CHIPS = (2, 4, 6)
RELAY_COLLECTIVE_ID = 0


class _InFlight:
    def __init__(self, scatter, ks, send_sems, recv_sems, srcs, lands, token):
        self.scatter, self.ks, self.send_sems, self.recv_sems = scatter, ks, send_sems, recv_sems
        self.srcs, self.lands, self.token = list(srcs), list(lands), token


def _mesh_peers(ks=EVERYONE):
    x, y, c = lax.axis_index("x"), lax.axis_index("y"), lax.axis_index("c")
    peers = []
    for k in ks:
        px, py, pc = x ^ ((k >> 2) & 1), y ^ ((k >> 1) & 1), c ^ (k & 1)
        peers.append(((px, py, pc), 4 * px + 2 * py + pc))
    return 4 * x + 2 * y + c, peers


_HBM_SPEC = pl.BlockSpec(memory_space=pltpu.HBM)
_SEM_SPEC = pl.BlockSpec(memory_space=pltpu.SEMAPHORE)
_SIDE_EFFECT = pltpu.SideEffectType.DATAFLOW_SIDE_EFFECTING


def _exchange_start(name, arrays, scatter, collective_id, ks=EVERYONE):
    n = len(arrays)
    me, _ = _mesh_peers()
    lands = []
    for a in arrays:
        own = lax.dynamic_index_in_dim(a, me, 0, keepdims=False) if scatter else a
        shape = a.shape if scatter else (N_DEV,) + a.shape
        lands.append(lax.dynamic_update_index_in_dim(lax.empty(shape, a.dtype), own, me, 0))

    def body(*refs):
        src, land = refs[:n], refs[n:2 * n]
        send_sems, recv_sems, token = refs[2 * n], refs[2 * n + 1], refs[-1]
        pos, peers = _mesh_peers(ks)
        barrier = pltpu.get_barrier_semaphore()
        for peer, _ in peers:
            pl.semaphore_signal(barrier, inc=1, device_id=peer, device_id_type=pl.DeviceIdType.MESH)
        pl.semaphore_wait(barrier, len(peers))
        for a in range(n):
            for k, (peer, peer_pos) in enumerate(peers):
                pltpu.make_async_remote_copy(
                    src_ref=src[a].at[peer_pos] if scatter else src[a], dst_ref=land[a].at[pos],
                    send_sem=send_sems.at[a * len(ks) + k], recv_sem=recv_sems.at[a * len(ks) + k],
                    device_id=peer, device_id_type=pl.DeviceIdType.MESH).start()
        token[...] = jnp.zeros_like(token)

    operands = [pltpu.with_memory_space_constraint(a, pltpu.HBM) for a in list(arrays) + lands]
    outs = pl.pallas_call(
        body, name=name,
        out_shape=(pltpu.SemaphoreType.DMA((n * len(ks),)), pltpu.SemaphoreType.DMA((n * len(ks),)),
                   *[pltpu.HBM(a.shape, a.dtype) for a in operands], _sds((8, LANES), F32)),
        in_specs=[_HBM_SPEC] * (2 * n),
        out_specs=(_SEM_SPEC, _SEM_SPEC, *[_HBM_SPEC] * (2 * n), pl.BlockSpec(memory_space=pltpu.VMEM)),
        input_output_aliases={i: 2 + i for i in range(2 * n)},
        compiler_params=pltpu.CompilerParams(has_side_effects=_SIDE_EFFECT, collective_id=collective_id),
    )(*operands)
    return _InFlight(scatter, ks, outs[0], outs[1], outs[2:2 + n], outs[2 + n:2 + 2 * n], outs[-1])


def _exchange_wait(name, flight, which, after):
    m = len(which)
    scatter, ks = flight.scatter, flight.ks

    def body(*refs):
        src, land = refs[:m], refs[m:2 * m]
        send_sems, recv_sems = refs[2 * m], refs[2 * m + 1]
        _, peers = _mesh_peers(ks)
        for i, a in enumerate(which):
            for k, (peer, peer_pos) in enumerate(peers):
                cp = pltpu.make_async_remote_copy(
                    src_ref=src[i].at[peer_pos] if scatter else src[i], dst_ref=land[i].at[peer_pos],
                    send_sem=send_sems.at[a * len(ks) + k], recv_sem=recv_sems.at[a * len(ks) + k],
                    device_id=peer, device_id_type=pl.DeviceIdType.MESH)
                cp.wait_send()
                cp.wait_recv()

    operands = [flight.srcs[a] for a in which] + [flight.lands[a] for a in which]
    outs = pl.pallas_call(
        body, name=name, out_shape=tuple(pltpu.HBM(a.shape, a.dtype) for a in operands),
        in_specs=[_HBM_SPEC] * (2 * m) + [_SEM_SPEC, _SEM_SPEC, pl.BlockSpec(memory_space=pl.ANY)],
        out_specs=tuple([_HBM_SPEC] * (2 * m)), input_output_aliases={i: i for i in range(2 * m)},
        compiler_params=pltpu.CompilerParams(has_side_effects=_SIDE_EFFECT),
    )(*operands, flight.send_sems, flight.recv_sems, after)
    return list(outs[m:])


def _relay_start(name, lands):
    n = len(lands)

    def body(*refs):
        land, send_sems, recv_sems, token = refs[:n], refs[n], refs[n + 1], refs[-1]
        _, peers = _mesh_peers(OTHER_CHIPS)
        sibling = (lax.axis_index("x"), lax.axis_index("y"), 1 - lax.axis_index("c"))
        barrier = pltpu.get_barrier_semaphore()
        pl.semaphore_signal(barrier, inc=1, device_id=sibling, device_id_type=pl.DeviceIdType.MESH)
        pl.semaphore_wait(barrier, 1)
        for a in range(n):
            for k, (_, peer_pos) in enumerate(peers):
                pltpu.make_async_remote_copy(
                    src_ref=land[a].at[peer_pos], dst_ref=land[a].at[peer_pos],
                    send_sem=send_sems.at[a * len(peers) + k], recv_sem=recv_sems.at[a * len(peers) + k],
                    device_id=sibling, device_id_type=pl.DeviceIdType.MESH).start()
        token[...] = jnp.zeros_like(token)

    count = n * len(OTHER_CHIPS)
    outs = pl.pallas_call(
        body, name=name,
        out_shape=(pltpu.SemaphoreType.DMA((count,)), pltpu.SemaphoreType.DMA((count,)),
                   *[pltpu.HBM(a.shape, a.dtype) for a in lands], _sds((8, LANES), F32)),
        in_specs=[_HBM_SPEC] * n,
        out_specs=(_SEM_SPEC, _SEM_SPEC, *[_HBM_SPEC] * n, pl.BlockSpec(memory_space=pltpu.VMEM)),
        input_output_aliases={i: 2 + i for i in range(n)},
        compiler_params=pltpu.CompilerParams(has_side_effects=_SIDE_EFFECT, collective_id=RELAY_COLLECTIVE_ID),
    )(*[pltpu.with_memory_space_constraint(a, pltpu.HBM) for a in lands])
    return _InFlight(False, OTHER_CHIPS, outs[0], outs[1], [], outs[2:2 + n], outs[-1])


def _relay_wait(name, flight, which, after):
    m = len(which)

    def body(*refs):
        land, send_sems, recv_sems = refs[:m], refs[m], refs[m + 1]
        _, peers = _mesh_peers(OTHER_CHIPS)
        sibling = (lax.axis_index("x"), lax.axis_index("y"), 1 - lax.axis_index("c"))
        for i, a in enumerate(which):
            for k, (_, peer_pos) in enumerate(peers):
                cp = pltpu.make_async_remote_copy(
                    src_ref=land[i].at[peer_pos], dst_ref=land[i].at[peer_pos ^ 1],
                    send_sem=send_sems.at[a * len(peers) + k], recv_sem=recv_sems.at[a * len(peers) + k],
                    device_id=sibling, device_id_type=pl.DeviceIdType.MESH)
                cp.wait_send()
                cp.wait_recv()

    operands = [flight.lands[a] for a in which]
    outs = pl.pallas_call(
        body, name=name, out_shape=tuple(pltpu.HBM(a.shape, a.dtype) for a in operands),
        in_specs=[_HBM_SPEC] * m + [_SEM_SPEC, _SEM_SPEC, pl.BlockSpec(memory_space=pl.ANY)],
        out_specs=tuple([_HBM_SPEC] * m), input_output_aliases={i: i for i in range(m)},
        compiler_params=pltpu.CompilerParams(has_side_effects=_SIDE_EFFECT),
    )(*operands, flight.send_sems, flight.recv_sems, after)
    return list(outs)


def _sum_parts(p_ref):
    g = p_ref[0].astype(F32)
    for dev in range(1, N_DEV):
        g = g + p_ref[dev].astype(F32)
    return g


def _adam_update(g, w, m, v):
    m_new = ADAM_B1 * m + (1.0 - ADAM_B1) * g
    v_new = ADAM_B2 * v + (1.0 - ADAM_B2) * jnp.square(g)
    m_hat = m_new / (1.0 - ADAM_B1 ** ADAM_STEP)
    v_hat = v_new / (1.0 - ADAM_B2 ** ADAM_STEP)
    return -ADAM_LR * (m_hat / (jnp.sqrt(v_hat) + ADAM_EPS) + ADAM_WD * w), m_new, v_new


def _adamw(name, parts, w, m, v, layer=None, into=None):
    r, c = w.shape[-2:]
    tr = 256 if r % 256 == 0 else r
    n_into = 0 if into is None else len(into)

    def body(p_ref, w_ref, m_ref, v_ref, *refs):
        g_ref, d_ref, mo_ref, vo_ref = refs[n_into:]
        g = _sum_parts(p_ref)
        g_ref[...] = g
        d_ref[...], mo_ref[...], vo_ref[...] = _adam_update(g, w_ref[...], m_ref[...], v_ref[...])

    if layer is None:
        blk = pl.BlockSpec((tr, c), lambda i: (i, 0))
    else:
        blk = pl.BlockSpec((None, tr, c), lambda i: (layer, i, 0))
    return pl.pallas_call(
        body, name=name, grid=(r // tr,),
        in_specs=[pl.BlockSpec((N_DEV, tr, c), lambda i: (0, i, 0)), blk, blk, blk] + [pl.BlockSpec(memory_space=pl.ANY)] * n_into,
        out_specs=[blk] * 4, out_shape=[_sds(w.shape, F32)] * 4,
        input_output_aliases={4 + i: i for i in range(n_into)}, compiler_params=_params(1),
    )(parts, w, m, v, *(into or ()))


SMALL_PACK_ROWS = 16
LOSS_ROW = 11


def _small_rows(grads, loss_tile):
    return [(0, 1, grads["g_attn"][0]), (1, 1, grads["g_attn"][1]), (2, 1, grads["g_mlp"][0]), (3, 1, grads["g_mlp"][1]),
            (4, 1, grads["g_kv"]), (5, 1, grads["b_f"]), (6, 1, grads["gq_a"]), (7, 1, grads["gk_a"]), (8, 1, grads["gk_b"]),
            (9, 1, grads["gq_b"]), (10, 1, grads["sinks"]), (LOSS_ROW, 1, loss_tile)]


SMALL_ROWS = {"g_attn": (0, 2), "g_mlp": (2, 2), "g_kv": (4, 1), "b_f": (5, 1), "gq_a": (6, 1), "gk_a": (7, 1),
              "gk_b": (8, 1), "gq_b": (9, 1), "sinks": (10, 1)}


def _pack_small(name, pieces, d):
    def body(*refs):
        out = refs[-1]
        out[...] = jnp.zeros_like(out)
        for (row, rows, _), ref in zip(pieces, refs[:-1]):
            out[row:row + rows, 0:ref.shape[1]] = ref[0:rows, :]

    return pl.pallas_call(body, name=name, out_shape=_sds((SMALL_PACK_ROWS, d), F32), compiler_params=_params(0))(
        *[piece for _, _, piece in pieces])


def _adamw_small(name, parts, parts_rel_bias, w, m, v):
    def body(*refs):
        ins, outs = refs[2:2 + 3 * len(SMALL)], refs[2 + 3 * len(SMALL):]
        pack, rel = _sum_parts(refs[0]), _sum_parts(refs[1])
        for i, k in enumerate(SMALL):
            w_ref, m_ref, v_ref = ins[3 * i:3 * i + 3]
            if k == "rel_bias":
                g = rel
            else:
                row, rows = SMALL_ROWS[k]
                g = pack[row:row + rows, 0:w_ref.shape[1]]
            outs[4 * i][...] = g
            outs[4 * i + 1][...], outs[4 * i + 2][...], outs[4 * i + 3][...] = _adam_update(g, w_ref[...], m_ref[...], v_ref[...])
        outs[-1][...] = pack[LOSS_ROW:LOSS_ROW + 1, 0:LANES]

    operands = [parts, parts_rel_bias] + [t[k] for k in SMALL for t in (w, m, v)]
    out_shape = [_sds(w[k].shape, F32) for k in SMALL for _ in range(4)] + [_sds((1, LANES), F32)]
    outs = pl.pallas_call(body, name=name, out_shape=out_shape, compiler_params=_params(0))(*operands)
    return {k: outs[4 * i:4 * i + 4] for i, k in enumerate(SMALL)}, outs[-1]


class _Comm:
    ORDER = ("w_in_a", "w_out_a", "w_up0", "w_down0", "w_kv", "w_q_b", "w_out_b", "w_up1", "w_down1")

    def __init__(self, shards, d, n_in):
        self.d, self.n_in = d, n_in
        self.ids = iter(range(RELAY_COLLECTIVE_ID + 1, RELAY_COLLECTIVE_ID + 16))
        self.flight = _exchange_start("gather_start", [shards[n].astype(BF16) for n in self.ORDER], False, next(self.ids), SAME_CORE)
        self.relays, self.sent = {}, []

    def prefetch(self, names, after):
        which = [self.ORDER.index(n) for n in names]
        landed = _exchange_wait(f"gather_wait_{names[0]}", self.flight, which, self.flight.token if after is None else after)
        relay = _relay_start(f"gather_relay_{names[0]}", landed)
        for n in names:
            self.relays[n] = (relay, names)
        return relay.token

    def weights(self, names, after):
        relay, group = self.relays[names[0]]
        landed = _relay_wait(f"gather_relay_wait_{names[0]}", relay, [group.index(n) for n in names],
                             relay.token if after is None else after)
        return [self._whole(n, g) for n, g in zip(names, landed)]

    def _whole(self, name, g):
        if name == "w_in_a":
            return _join_row_blocks("w_in_join", g, -(-self.n_in // LANES) * LANES)
        if name.startswith("w_up"):
            return g
        return g.reshape(-1, g.shape[-1])

    def _chunks(self, name, g):
        if name == "w_in_a":
            return _split_row_blocks("dw_in_split", list(g), N_DEV, self.n_in // N_DEV)
        if name.startswith("w_up"):
            return g
        return g.reshape(N_DEV, g.shape[0] // N_DEV, g.shape[1])

    def send_grads(self, tag, partials):
        names = list(partials)
        flight = _exchange_start(f"scatter_start_{tag}", [self._chunks(n, partials[n]) for n in names], True, next(self.ids))
        self.sent.append((tag, flight, names))
        return flight.token

    def received(self, index, after):
        tag, flight, names = self.sent[index]
        landed = _exchange_wait(f"scatter_wait_{tag}", flight, list(range(len(names))), after)
        return dict(zip(names, landed))


def kernel(x, g_attn, g_mlp, w_in_a, b_f, gq_a, gk_a, w_out_a, g_kv, w_kv, gk_b, w_q_b, gq_b, sinks, rel_bias, w_out_b, w_up, w_down, loss_target, m_g_attn, m_g_mlp, m_w_in_a, m_b_f, m_gq_a, m_gk_a, m_w_out_a, m_g_kv, m_w_kv, m_gk_b, m_w_q_b, m_gq_b, m_sinks, m_rel_bias, m_w_out_b, m_w_up, m_w_down, v_g_attn, v_g_mlp, v_w_in_a, v_b_f, v_gq_a, v_gk_a, v_w_out_a, v_g_kv, v_w_kv, v_gk_b, v_w_q_b, v_gq_b, v_sinks, v_rel_bias, v_w_out_b, v_w_up, v_w_down):
    w = dict(g_attn=g_attn, g_mlp=g_mlp, w_in_a=w_in_a, b_f=b_f, gq_a=gq_a, gk_a=gk_a, w_out_a=w_out_a, g_kv=g_kv,
             w_kv=w_kv, gk_b=gk_b, w_q_b=w_q_b, gq_b=gq_b, sinks=sinks, rel_bias=rel_bias, w_out_b=w_out_b,
             w_up=w_up, w_down=w_down)
    mom = dict(g_attn=m_g_attn, g_mlp=m_g_mlp, w_in_a=m_w_in_a, b_f=m_b_f, gq_a=m_gq_a, gk_a=m_gk_a, w_out_a=m_w_out_a,
               g_kv=m_g_kv, w_kv=m_w_kv, gk_b=m_gk_b, w_q_b=m_w_q_b, gq_b=m_gq_b, sinks=m_sinks, rel_bias=m_rel_bias,
               w_out_b=m_w_out_b, w_up=m_w_up, w_down=m_w_down)
    var = dict(g_attn=v_g_attn, g_mlp=v_g_mlp, w_in_a=v_w_in_a, b_f=v_b_f, gq_a=v_gq_a, gk_a=v_gk_a, w_out_a=v_w_out_a,
               g_kv=v_g_kv, w_kv=v_w_kv, gk_b=v_gk_b, w_q_b=v_w_q_b, gq_b=v_gq_b, sinks=v_sinks, rel_bias=v_rel_bias,
               w_out_b=v_w_out_b, w_up=v_w_up, w_down=v_w_down)
    d = x.shape[2]
    where = {"w_in_a": ("w_in_a", 0), "w_out_a": ("w_out_a", 0), "w_kv": ("w_kv", None), "w_q_b": ("w_q_b", 0),
             "w_out_b": ("w_out_b", 0), "w_up0": ("w_up", 0), "w_up1": ("w_up", 1), "w_down0": ("w_down", 0),
             "w_down1": ("w_down", 1)}
    flip = lambda tree: {**tree, "w_in_a": jnp.swapaxes(tree["w_in_a"], 1, 2)}
    w, mom, var = flip(w), flip(mom), flip(var)
    shards = {n: (w[k] if layer is None else w[k][layer]) for n, (k, layer) in where.items()}
    comm = _Comm(shards, d, w_in_a.shape[2] * N_DEV)
    loss_tile, grad_x, grads = _local_step(x[0], loss_target[0], {k: w[k] for k in SMALL}, comm)

    small_flight = _exchange_start(
        "gather_small_grads", [_pack_small("pack_small", _small_rows(grads, loss_tile), d), grads["rel_bias"]], False, next(comm.ids))
    res, after = {}, small_flight.token
    for index in range(len(comm.sent)):
        for n, parts in comm.received(index, after).items():
            k, layer = where[n]
            res[k] = _adamw(f"adam_{n}", parts, w[k], mom[k], var[k], layer, res.get(k))
            after = res[k][0]
    as_rows = lambda tree: {k: tree[k] if tree[k].ndim == 2 else tree[k].reshape(1, -1) for k in SMALL}
    small, loss_row = _adamw_small("adam_small", *_exchange_wait("gather_small_wait", small_flight, [0, 1], after),
                                   as_rows(w), as_rows(mom), as_rows(var))
    loss = loss_row[0, 0]
    for k in SMALL:
        res[k] = [a.reshape(w[k].shape) for a in small[k]]
    res["w_in_a"] = [jnp.swapaxes(a, 1, 2) for a in res["w_in_a"]]

    outs = [loss, grad_x[None]]
    for i in range(4):
        outs.extend(res[k][i] for k in WEIGHTS)
    return tuple(outs)
```

```python
import numpy as np
import jax
import jax.numpy as jnp
from jax import lax
from jax.experimental import pallas as pl
from jax.experimental.pallas import tpu as pltpu

F32 = jnp.float32
BF16 = jnp.bfloat16

N_DEV = 8
HEAD_DIM = 64
WINDOW = 128
N_BUCKETS = 32
REL_MAX_DIST = 128
NORM_EPS = 1e-6
NEG = -1e30
LANES = 128
VMEM_LIMIT = 56 * 1024 * 1024

ADAM_LR = 0.001
ADAM_B1 = 0.9
ADAM_B2 = 0.999
ADAM_EPS = 1e-08
ADAM_WD = 0.01
ADAM_STEP = 10

SMALL = ("g_attn", "g_mlp", "b_f", "gq_a", "gk_a", "g_kv", "gk_b", "gq_b", "sinks", "rel_bias")
WEIGHTS = ("g_attn", "g_mlp", "w_in_a", "b_f", "gq_a", "gk_a", "w_out_a", "g_kv", "w_kv", "gk_b",
           "w_q_b", "gq_b", "sinks", "rel_bias", "w_out_b", "w_up", "w_down")


def _params(n_grid):
    return pltpu.CompilerParams(dimension_semantics=("arbitrary",) * n_grid, vmem_limit_bytes=VMEM_LIMIT)


def _sds(shape, dtype):
    return jax.ShapeDtypeStruct(tuple(shape), dtype)


def _after_operand(after):
    if after is None:
        return [], []
    return [pl.BlockSpec((8, LANES), lambda *_: (0, 0))], [after]


def _rms(x, g):
    return (x * lax.rsqrt(jnp.mean(x * x, axis=-1, keepdims=True) + NORM_EPS)) * g


def _dot_nt(a, b):
    return lax.dot_general(a, b, (((1,), (1,)), ((), ())), preferred_element_type=F32)


def _dot_tn(a, b):
    return lax.dot_general(a, b, (((0,), (0,)), ((), ())), preferred_element_type=F32)


def _dot(a, b):
    return jnp.dot(a, b, preferred_element_type=F32)


def _dot_exact(a, b):
    return jnp.dot(a, b, preferred_element_type=F32, precision=lax.Precision.HIGHEST)


def _norm_matmul(name, x, g, w, *, tn=None, relu2=False, w_rows=False, after=None):
    t, d = x.shape
    blocked = w.ndim == 3
    per_step = 2 if blocked else 1
    if blocked:
        tn = per_step * w.shape[2]
        n = w.shape[0] * w.shape[2]
        w_spec = pl.BlockSpec((per_step, d, w.shape[2]), lambda i, j: (j, 0, 0))
    elif w_rows:
        n = w.shape[0]
        w_spec = pl.BlockSpec((tn, d), lambda i, j: (j, 0))
    else:
        n = w.shape[1]
        w_spec = pl.BlockSpec((d, tn), lambda i, j: (0, j))
    tm = min(1024, t)

    def body(x_ref, g_ref, w_ref, *rest):
        y_ref, xn_ref = rest[-2:]

        @pl.when(pl.program_id(1) == 0)
        def _():
            xn_ref[...] = _rms(x_ref[...], g_ref[...]).astype(BF16)

        for b in range(per_step):
            cols = slice(b * (tn // per_step), (b + 1) * (tn // per_step)) if blocked else slice(None)
            wb = w_ref[b] if blocked else w_ref[...]
            y = _dot_nt(xn_ref[...], wb) if w_rows else _dot(xn_ref[...], wb)
            y_ref[:, cols] = jnp.square(jnp.maximum(y, 0.0)).astype(BF16) if relu2 else y

    extra_specs, extra = _after_operand(after)
    out_shape = [_sds((t, n), BF16 if relu2 else F32), _sds((t, d), BF16)]
    out_specs = [pl.BlockSpec((tm, tn), lambda i, j: (i, j)), pl.BlockSpec((tm, d), lambda i, j: (i, 0))]
    return pl.pallas_call(
        body, name=name, grid=(t // tm, n // tn),
        in_specs=[pl.BlockSpec((tm, d), lambda i, j: (i, 0)), pl.BlockSpec((1, d), lambda i, j: (0, 0)), w_spec] + extra_specs,
        out_specs=out_specs, out_shape=out_shape, compiler_params=_params(2),
    )(x, g.reshape(1, d), w, *extra)


def _matmul_res(name, a, w, res, *, tn=512, after=None):
    t, k = a.shape
    n = w.shape[1]
    tm = min(1024, t)

    def body(a_ref, w_ref, r_ref, *rest):
        rest[-1][...] = r_ref[...] + _dot(a_ref[...], w_ref[...])

    extra_specs, extra = _after_operand(after)
    return pl.pallas_call(
        body, name=name, grid=(t // tm, n // tn),
        in_specs=[pl.BlockSpec((tm, k), lambda i, j: (i, 0)), pl.BlockSpec((k, tn), lambda i, j: (0, j)),
                  pl.BlockSpec((tm, tn), lambda i, j: (i, j))] + extra_specs,
        out_specs=pl.BlockSpec((tm, tn), lambda i, j: (i, j)), out_shape=_sds((t, n), F32),
        compiler_params=_params(2),
    )(a, w, res, *extra)


def _matmul_nt(name, dy, w, *, a=None, tk=1024):
    t, n = dy.shape
    k = w.shape[0]
    tm = min(1024, t)

    def body(dy_ref, w_ref, *rest):
        o_ref = rest[-1]
        r = _dot_nt(dy_ref[...].astype(BF16), w_ref[...])
        if a is not None:
            r = r * (2.0 * jnp.sqrt(rest[0][...].astype(F32)))
        o_ref[...] = r.astype(BF16)

    in_specs = [pl.BlockSpec((tm, n), lambda i, j: (i, 0)), pl.BlockSpec((tk, n), lambda i, j: (j, 0))]
    args = [dy, w]
    if a is not None:
        in_specs.append(pl.BlockSpec((tm, tk), lambda i, j: (i, j)))
        args.append(a)
    return pl.pallas_call(
        body, name=name, grid=(t // tm, k // tk), in_specs=in_specs,
        out_specs=pl.BlockSpec((tm, tk), lambda i, j: (i, j)), out_shape=_sds((t, k), BF16),
        compiler_params=_params(2),
    )(*args)


def _matmul_nt_rmsbwd(name, dy, w, x, g, dres, *, w_rows=False, bf16_too=False, after=None):
    pieces = isinstance(dy, tuple)
    dys = list(dy) if pieces else [dy]
    t = dres.shape[0]
    blocked = w.ndim == 3
    d = w.shape[1] if blocked or w_rows else w.shape[0]
    tm = min(512, t)
    extra_specs, extra = _after_operand(after)
    n_in = len(dys) + 4 + len(extra)

    def body(*refs):
        dy_ref = refs[0]
        w_ref, x_ref, g_ref, r_ref = refs[len(dys):len(dys) + 4]
        dx_ref, dg_ref = refs[n_in:n_in + 2]
        if pieces:
            n_stack, _, k1 = dys[0].shape
            dxn = _dot(refs[1][...], w_ref[n_stack * k1:, :])
            for j in range(n_stack):
                dxn += _dot(dy_ref[j], w_ref[j * k1:(j + 1) * k1, :])
        elif blocked:
            kb = w.shape[2]
            dxn = _dot_nt(dy_ref[:, 0:kb].astype(BF16), w_ref[0])
            for j in range(1, w.shape[0]):
                dxn += _dot_nt(dy_ref[:, j * kb:(j + 1) * kb].astype(BF16), w_ref[j])
        elif w_rows:
            dxn = _dot(dy_ref[...].astype(BF16), w_ref[...])
        else:
            dxn = _dot_nt(dy_ref[...].astype(BF16), w_ref[...])
        _, vjp = jax.vjp(_rms, x_ref[...], g_ref[...])
        dx, dg = vjp(dxn)
        dx_ref[...] = r_ref[...] + dx
        if bf16_too:
            refs[n_in + 2][...] = (r_ref[...] + dx).astype(BF16)

        @pl.when(pl.program_id(0) == 0)
        def _():
            dg_ref[...] = jnp.zeros_like(dg_ref)

        dg_ref[...] += dg

    row = lambda i: (i, 0)
    fixed = lambda i: (0, 0)
    if pieces:
        dy_specs = [pl.BlockSpec((dys[0].shape[0], tm, dys[0].shape[2]), lambda i: (0, i, 0)),
                    pl.BlockSpec((tm, dys[1].shape[1]), row)]
    else:
        dy_specs = [pl.BlockSpec((tm, dy.shape[1]), row)]
    outs = pl.pallas_call(
        body, name=name, grid=(t // tm,),
        in_specs=dy_specs + [pl.BlockSpec(w.shape, (lambda i: (0, 0, 0)) if blocked else fixed),
                             pl.BlockSpec((tm, d), row), pl.BlockSpec((1, d), fixed), pl.BlockSpec((tm, d), row)] + extra_specs,
        out_specs=[pl.BlockSpec((tm, d), row), pl.BlockSpec((1, d), fixed)] + [pl.BlockSpec((tm, d), row)] * bf16_too,
        out_shape=[_sds((t, d), F32), _sds((1, d), F32)] + [_sds((t, d), BF16)] * bf16_too, compiler_params=_params(1),
    )(*dys, w, x, g.reshape(1, d), dres, *extra)
    return (outs[0], outs[2], outs[1]) if bf16_too else outs


def _matmul_tn(name, a, b, *, tk=1024, tn=1024, col_blocks=None):
    stacked = a.ndim == 3
    t, k1 = a.shape[-2:]
    k = a.shape[0] * k1 if stacked else k1
    n = b.shape[1]
    tk = min(tk, k1)
    per = k1 // tk
    a_spec = (pl.BlockSpec((None, t, tk), lambda i, j: (i // per, 0, i % per)) if stacked
              else pl.BlockSpec((t, tk), lambda i, j: (0, i)))
    if col_blocks:
        tn = n // col_blocks
        out_spec, out_shape = pl.BlockSpec((None, tk, tn), lambda i, j: (j, i, 0)), _sds((col_blocks, k, tn), BF16)
    else:
        tn = min(tn, n)
        out_spec, out_shape = pl.BlockSpec((tk, tn), lambda i, j: (i, j)), _sds((k, n), BF16)

    def body(a_ref, b_ref, o_ref):
        o_ref[...] = _dot_tn(a_ref[...].astype(BF16), b_ref[...].astype(BF16)).astype(BF16)

    return pl.pallas_call(
        body, name=name, grid=(k // tk, n // tn),
        in_specs=[a_spec, pl.BlockSpec((t, tn), lambda i, j: (0, j))],
        out_specs=out_spec, out_shape=out_shape, compiler_params=_params(2),
    )(a, b)


def _join_row_blocks(name, blocks, rows):
    b, r, c = blocks.shape
    tc = min(256, c)

    def body(g_ref, o_ref):
        o_ref[...] = jnp.zeros_like(o_ref)
        for j in range(b):
            o_ref[r * j:r * (j + 1), :] = g_ref[j]

    return pl.pallas_call(
        body, name=name, grid=(c // tc,), in_specs=[pl.BlockSpec((b, r, tc), lambda i: (0, 0, i))],
        out_specs=pl.BlockSpec((rows, tc), lambda i: (0, i)), out_shape=_sds((rows, c), blocks.dtype),
        compiler_params=_params(1),
    )(blocks)


def _split_row_blocks(name, mats, b, r):
    c = mats[0].shape[1]
    tc = min(256, c)

    def body(*refs):
        o_ref = refs[-1]
        for j in range(b):
            first = 0
            for m_ref in refs[:-1]:
                lo, hi = max(r * j, first), min(r * (j + 1), first + m_ref.shape[0])
                if lo < hi:
                    o_ref[j, lo - r * j:hi - r * j, :] = m_ref[lo - first:hi - first, :]
                first += m_ref.shape[0]

    return pl.pallas_call(
        body, name=name, grid=(c // tc,), in_specs=[pl.BlockSpec((m.shape[0], tc), lambda i: (0, i)) for m in mats],
        out_specs=pl.BlockSpec((b, r, tc), lambda i: (0, 0, i)), out_shape=_sds((b, r, c), mats[0].dtype),
        compiler_params=_params(1),
    )(*mats)


def _matmul_res_loss(name, a, w, res, target, *, tn=512):
    t, k = a.shape
    n = w.shape[1]
    tm = min(1024, t)

    def body(a_ref, w_ref, r_ref, t_ref, dy_ref, dyb_ref, l_ref):
        e = r_ref[...] + _dot(a_ref[...], w_ref[...]) - t_ref[...]
        dy_ref[...] = e * (1.0 / n)
        dyb_ref[...] = (e * (1.0 / n)).astype(BF16)

        @pl.when((pl.program_id(0) == 0) & (pl.program_id(1) == 0))
        def _():
            l_ref[...] = jnp.zeros_like(l_ref)

        l_ref[...] += (0.5 / n) * jnp.sum(e * e)

    tile = pl.BlockSpec((tm, tn), lambda i, j: (i, j))
    return pl.pallas_call(
        body, name=name, grid=(t // tm, n // tn),
        in_specs=[pl.BlockSpec((tm, k), lambda i, j: (i, 0)), pl.BlockSpec((k, tn), lambda i, j: (0, j)), tile, tile],
        out_specs=[tile, tile, pl.BlockSpec((8, LANES), lambda i, j: (0, 0))],
        out_shape=[_sds((t, n), F32), _sds((t, n), BF16), _sds((8, LANES), F32)], compiler_params=_params(2),
    )(a, w, res, target)


def _gate_fwd(name, proj, b_pad, n_heads, gate_col, after=None):
    t = proj.shape[0]
    tb = min(256, t)
    tri = jnp.asarray(np.tril(np.ones((tb, tb), np.float32)))
    extra_specs, extra = _after_operand(after)

    def body(p_ref, b_ref, tri_ref, *rest):
        c_ref, carry = rest[-2:]

        @pl.when(pl.program_id(0) == 0)
        def _():
            carry[...] = jnp.zeros_like(carry)

        lane = lax.broadcasted_iota(jnp.int32, (tb, LANES), 1)
        lf = jnp.where(lane < n_heads, jax.nn.log_sigmoid(p_ref[...] + b_ref[...]), 0.0)
        c = _dot_exact(tri_ref[...], lf) + carry[0:1, :]
        c_ref[...] = c
        carry[...] = jnp.broadcast_to(c[tb - 1:tb, :], carry.shape)

    return pl.pallas_call(
        body, name=name, grid=(t // tb,),
        in_specs=[pl.BlockSpec((tb, LANES), lambda i: (i, gate_col)), pl.BlockSpec((1, LANES), lambda i: (0, 0)),
                  pl.BlockSpec((tb, tb), lambda i: (0, 0))] + extra_specs,
        out_specs=pl.BlockSpec((tb, LANES), lambda i: (i, 0)), out_shape=_sds((t, LANES), F32),
        scratch_shapes=[pltpu.VMEM((8, LANES), F32)], compiler_params=_params(1),
    )(proj, b_pad, tri, *extra)


def _gate_bwd(name, proj, b_pad, dc, n_heads, gate_col):
    t = proj.shape[0]
    tb = min(256, t)
    nb = t // tb
    triu = jnp.asarray(np.triu(np.ones((tb, tb), np.float32)))

    def body(p_ref, b_ref, dc_ref, tri_ref, df_ref, db_ref, carry):
        @pl.when(pl.program_id(0) == 0)
        def _():
            carry[...] = jnp.zeros_like(carry)
            db_ref[...] = jnp.zeros_like(db_ref)

        dcv = dc_ref[...]
        dlf = _dot_exact(tri_ref[...], dcv) + carry[0:1, :]
        carry[...] = jnp.broadcast_to(dlf[0:1, :], carry.shape)
        lane = lax.broadcasted_iota(jnp.int32, (tb, LANES), 1)
        z = p_ref[...] + b_ref[...]
        df = jnp.where(lane < n_heads, dlf / (1.0 + jnp.exp(z)), 0.0)
        df_ref[...] = df.astype(BF16)
        db_ref[...] += jnp.sum(df, axis=0, keepdims=True)

    return pl.pallas_call(
        body, name=name, grid=(nb,),
        in_specs=[pl.BlockSpec((tb, LANES), lambda i: (nb - 1 - i, gate_col)), pl.BlockSpec((1, LANES), lambda i: (0, 0)),
                  pl.BlockSpec((tb, LANES), lambda i: (nb - 1 - i, 0)), pl.BlockSpec((tb, tb), lambda i: (0, 0))],
        out_specs=[pl.BlockSpec((tb, LANES), lambda i: (nb - 1 - i, 0)), pl.BlockSpec((1, LANES), lambda i: (0, 0))],
        out_shape=[_sds((t, LANES), BF16), _sds((1, LANES), F32)],
        scratch_shapes=[pltpu.VMEM((8, LANES), F32)], compiler_params=_params(1),
    )(proj, b_pad, dc, triu)


def _qhead(qp, g):
    return _rms(qp, g) * (HEAD_DIM ** -0.5)


def _column(mat, idx):
    lane = lax.broadcasted_iota(jnp.int32, mat.shape, 1)
    return jnp.sum(jnp.where(lane == idx, mat, 0.0), axis=1, keepdims=True)


def _fox_scores(kk, qi, ckey, cq_i, i, bq):
    length = kk.shape[0]
    s = _dot_nt(kk, qi) + cq_i - ckey[:length]
    key = lax.broadcasted_iota(jnp.int32, (length, bq), 0)
    qry = lax.broadcasted_iota(jnp.int32, (length, bq), 1) + i * bq
    return jnp.where(key <= qry, s, NEG)


def _fox_fwd(name, proj, c, crow, gq, gk, n_heads):
    t = proj.shape[0]
    hw = n_heads * HEAD_DIM
    npair = n_heads // 2
    bq = min(512, t)
    nq = t // bq

    def body(q_ref, k_ref, v_ref, c_ref, crow_ref, gq_ref, gk_ref, o_ref, lse_ref):
        hp = pl.program_id(0)
        lse_ref[...] = jnp.zeros_like(lse_ref)
        outs = []
        for hh in range(2):
            sl = slice(hh * HEAD_DIM, (hh + 1) * HEAD_DIM)
            qn = _qhead(q_ref[:, sl], gq_ref[...]).astype(BF16)
            kn = _rms(k_ref[:, sl], gk_ref[...]).astype(BF16)
            v_t = v_ref[:, sl].T.astype(BF16)
            ckey = _column(c_ref[...], 2 * hp + hh)
            cq = crow_ref[0, hh:hh + 1, :]
            o_blocks = []
            for i in range(nq):
                cols = slice(i * bq, (i + 1) * bq)
                length = (i + 1) * bq
                s = _fox_scores(kn[:length], qn[cols], ckey, cq[:, cols], i, bq)
                m = jnp.max(s, axis=0, keepdims=True)
                p = jnp.exp(s - m)
                l = jnp.sum(p, axis=0, keepdims=True)
                o_blocks.append((_dot(v_t[:, :length], p.astype(BF16)) / l).T)
                lse_ref[0, hh:hh + 1, cols] = m + jnp.log(l)
            outs.append(jnp.concatenate(o_blocks, axis=0))
        o_ref[...] = jnp.concatenate(outs, axis=1).astype(BF16)

    col = lambda off: (lambda h: (0, off + h))
    fixed = lambda h: (0, 0)
    return pl.pallas_call(
        body, name=name, grid=(npair,),
        in_specs=[pl.BlockSpec((t, LANES), col(0)), pl.BlockSpec((t, LANES), col(npair)), pl.BlockSpec((t, LANES), col(2 * npair)),
                  pl.BlockSpec((t, LANES), fixed), pl.BlockSpec((1, 2, t), lambda h: (h, 0, 0)),
                  pl.BlockSpec((1, HEAD_DIM), fixed), pl.BlockSpec((1, HEAD_DIM), fixed)],
        out_specs=[pl.BlockSpec((t, LANES), col(0)), pl.BlockSpec((1, 8, t), lambda h: (h, 0, 0))],
        out_shape=[_sds((t, hw), BF16), _sds((npair, 8, t), F32)], compiler_params=_params(1),
    )(proj, proj, proj, c, crow, gq, gk)


def _fox_bwd(name, proj, c, crow, gq, gk, lse, do, n_heads, after=None):
    t = proj.shape[0]
    hw = n_heads * HEAD_DIM
    npair = n_heads // 2
    bq = min(256, t)
    nq = t // bq

    def body(q_ref, k_ref, v_ref, c_ref, crow_ref, gq_ref, gk_ref, lse_ref, do_ref, *rest):
        dqkv_ref, dc_ref, dgq_ref, dgk_ref, dk_acc, dv_acc, dc_acc = rest[-7:]
        hp = pl.program_id(0)

        @pl.when(hp == 0)
        def _():
            dgq_ref[...] = jnp.zeros_like(dgq_ref)
            dgk_ref[...] = jnp.zeros_like(dgk_ref)
            dc_ref[...] = jnp.zeros_like(dc_ref)

        lane = lax.broadcasted_iota(jnp.int32, (t, LANES), 1)
        dqs, dks, dvs = [], [], []
        for hh in range(2):
            sl = slice(hh * HEAD_DIM, (hh + 1) * HEAD_DIM)
            qf, q_vjp = jax.vjp(_qhead, q_ref[:, sl], gq_ref[...])
            kf, k_vjp = jax.vjp(_rms, k_ref[:, sl], gk_ref[...])
            qn, kn, kn_t = qf.astype(BF16), kf.astype(BF16), kf.T.astype(BF16)
            vb = v_ref[:, sl].astype(BF16)
            dob = do_ref[:, sl]
            ckey = _column(c_ref[...], 2 * hp + hh)
            cq = crow_ref[0, hh:hh + 1, :]
            lse_h = lse_ref[0, hh:hh + 1, :]
            dk_acc[...] = jnp.zeros_like(dk_acc)
            dv_acc[...] = jnp.zeros_like(dv_acc)
            dc_acc[...] = jnp.zeros_like(dc_acc)
            dq_blocks = []
            for i in range(nq):
                cols = slice(i * bq, (i + 1) * bq)
                length = (i + 1) * bq
                qi, doi = qn[cols], dob[cols]
                s = _fox_scores(kn[:length], qi, ckey, cq[:, cols], i, bq)
                p = jnp.exp(s - lse_h[:, cols])
                dp = _dot_nt(vb[:length], doi)
                ds = p * (dp - jnp.sum(p * dp, axis=0, keepdims=True))
                dsb = ds.astype(BF16)
                dq_blocks.append(_dot(kn_t[:, :length], dsb).T)
                dk_acc[0:length, :] += _dot(dsb, qi)
                dv_acc[0:length, :] += _dot(p.astype(BF16), doi)
                part = ds[:, 0:LANES]
                for j in range(1, bq // LANES):
                    part = part + ds[:, j * LANES:(j + 1) * LANES]
                dc_acc[0:length, :] += part
            dqp, dgq = q_vjp(jnp.concatenate(dq_blocks, axis=0))
            dkp, dgk = k_vjp(dk_acc[...])
            dgq_ref[...] += dgq
            dgk_ref[...] += dgk
            dqs.append(dqp)
            dks.append(dkp)
            dvs.append(dv_acc[...])
            dc_ref[...] = jnp.where(lane == 2 * hp + hh, -jnp.sum(dc_acc[...], axis=1, keepdims=True), dc_ref[...])
        for part, halves in enumerate((dqs, dks, dvs)):
            dqkv_ref[part] = jnp.concatenate(halves, axis=1).astype(BF16)

    col = lambda off: (lambda h: (0, off + h))
    fixed = lambda h: (0, 0)
    pair_blk = pl.BlockSpec((t, LANES), col(0))
    extra_specs, extra = _after_operand(after)
    return pl.pallas_call(
        body, name=name, grid=(npair,),
        in_specs=[pl.BlockSpec((t, LANES), col(0)), pl.BlockSpec((t, LANES), col(npair)), pl.BlockSpec((t, LANES), col(2 * npair)),
                  pl.BlockSpec((t, LANES), fixed), pl.BlockSpec((1, 2, t), lambda h: (h, 0, 0)),
                  pl.BlockSpec((1, HEAD_DIM), fixed), pl.BlockSpec((1, HEAD_DIM), fixed),
                  pl.BlockSpec((1, 8, t), lambda h: (h, 0, 0)), pair_blk] + extra_specs,
        out_specs=[pl.BlockSpec((3, t, LANES), lambda h: (0, 0, h)), pl.BlockSpec((t, LANES), fixed),
                   pl.BlockSpec((1, HEAD_DIM), fixed), pl.BlockSpec((1, HEAD_DIM), fixed)],
        out_shape=[_sds((3, t, hw), BF16), _sds((t, LANES), F32),
                   _sds((1, HEAD_DIM), F32), _sds((1, HEAD_DIM), F32)],
        scratch_shapes=[pltpu.VMEM((t, HEAD_DIM), F32), pltpu.VMEM((t, HEAD_DIM), F32), pltpu.VMEM((t, LANES), F32)],
        compiler_params=_params(1),
    )(proj, proj, proj, c, crow, gq, gk, lse, do, *extra)


def _t5_bucket_table():
    dist = np.arange(WINDOW)[None, :] + WINDOW - np.arange(2 * WINDOW)[:, None]
    n = np.maximum(dist, 0)
    max_exact = N_BUCKETS // 2
    large = max_exact + (np.log(np.maximum(n, 1) / max_exact) / np.log(REL_MAX_DIST / max_exact)
                         * (N_BUCKETS - max_exact)).astype(np.int32)
    large = np.minimum(large, N_BUCKETS - 1)
    return np.where(n < max_exact, n, large).astype(np.int32).reshape(1, -1)


def _bias_expand(name, rel_bias_t):
    n_heads = rel_bias_t.shape[0]
    tbl = jnp.asarray(_t5_bucket_table())
    width = tbl.shape[1]

    def body(rb_ref, tbl_ref, o_ref):
        onehot = (lax.broadcasted_iota(jnp.int32, (N_BUCKETS, width), 0) == tbl_ref[...]).astype(F32)
        o_ref[...] = _dot_exact(rb_ref[...], onehot)

    return pl.pallas_call(body, name=name, out_shape=_sds((n_heads, width), F32), compiler_params=_params(0))(rel_bias_t, tbl)


def _bias_reduce(name, dbias):
    n_heads, width = dbias.shape
    tbl = jnp.asarray(_t5_bucket_table())

    def body(db_ref, tbl_ref, o_ref):
        onehot = (lax.broadcasted_iota(jnp.int32, (N_BUCKETS, width), 0) == tbl_ref[...]).astype(F32)
        o_ref[...] = lax.dot_general(db_ref[...], onehot, (((1,), (1,)), ((), ())), preferred_element_type=F32,
                                     precision=lax.Precision.HIGHEST)

    return pl.pallas_call(body, name=name, out_shape=_sds((n_heads, N_BUCKETS), F32), compiler_params=_params(0))(dbias, tbl)


def _swa_mask(n, group):
    j = lax.broadcasted_iota(jnp.int32, (2 * WINDOW, group * WINDOW), 0)
    i = lax.broadcasted_iota(jnp.int32, (2 * WINDOW, group * WINDOW), 1) & (WINDOW - 1)
    ok = (j > i) & (j <= i + WINDOW) & ((n > 0) | (j >= WINDOW))
    return jnp.where(ok, 0.0, NEG)


def _swa_stack(ref, start, group):
    return jnp.concatenate([ref[pl.ds(start, WINDOW), g * HEAD_DIM:(g + 1) * HEAD_DIM] for g in range(group)], axis=0)


def _kv_head(ref, n_kv):
    out = ref[:, 0:HEAD_DIM]
    for h in range(1, n_kv):
        out = jnp.where(pl.program_id(0) == h, ref[:, h * HEAD_DIM:(h + 1) * HEAD_DIM], out)
    return out


def _swa_fwd(name, qb, kv, gq, gk, sinks, bias, group):
    t = qb.shape[0]
    kvh = kv.shape[1] // (2 * HEAD_DIM)
    nblk = t // WINDOW
    gw = group * HEAD_DIM
    band = 2 * WINDOW
    cols = group * WINDOW

    def body(q_ref, k_ref, v_ref, gq_ref, gk_ref, sink_ref, bias_ref, o_ref, lse_ref, qs, kpad, vpad):
        for g in range(group):
            qs[:, g * HEAD_DIM:(g + 1) * HEAD_DIM] = _qhead(q_ref[:, g * HEAD_DIM:(g + 1) * HEAD_DIM], gq_ref[...]).astype(BF16)
        kpad[0:WINDOW, :] = jnp.zeros((WINDOW, HEAD_DIM), BF16)
        vpad[0:WINDOW, :] = jnp.zeros((WINDOW, HEAD_DIM), BF16)
        kpad[WINDOW:, :] = _rms(_kv_head(k_ref, kvh), gk_ref[...]).astype(BF16)
        vpad[WINDOW:, :] = _kv_head(v_ref, kvh).astype(BF16)
        sink = sink_ref[0]

        def block(n, carry):
            start = pl.multiple_of(n * WINDOW, WINDOW)
            kb = kpad[pl.ds(start, band), :]
            vb = vpad[pl.ds(start, band), :]
            s = _dot_nt(kb, _swa_stack(qs, start, group)) + bias_ref[0] + _swa_mask(n, group)
            m = jnp.maximum(jnp.max(s, axis=0, keepdims=True), sink)
            e = jnp.exp(s - m)
            l = jnp.sum(e, axis=0, keepdims=True) + jnp.exp(sink - m)
            o_t = _dot_tn(vb, e.astype(BF16)) / l
            for g in range(group):
                o_ref[pl.ds(start, WINDOW), g * HEAD_DIM:(g + 1) * HEAD_DIM] = o_t[:, g * WINDOW:(g + 1) * WINDOW].T.astype(BF16)
            lse_ref[pl.ds(n, 1), :] = m + jnp.log(l)
            return carry

        lax.fori_loop(0, nblk, block, 0)

    fixed = lambda h: (0, 0)
    per = lambda h: (h, 0, 0)
    return pl.pallas_call(
        body, name=name, grid=(kvh,),
        in_specs=[pl.BlockSpec((t, gw), lambda h: (0, h)), pl.BlockSpec((t, kvh * HEAD_DIM), lambda h: (0, 0)),
                  pl.BlockSpec((t, kvh * HEAD_DIM), lambda h: (0, 1)),
                  pl.BlockSpec((1, HEAD_DIM), fixed), pl.BlockSpec((1, HEAD_DIM), fixed),
                  pl.BlockSpec((1, 1, cols), per), pl.BlockSpec((1, band, cols), per)],
        out_specs=[pl.BlockSpec((t, gw), lambda h: (0, h)), pl.BlockSpec((nblk, cols), lambda h: (h, 0))],
        out_shape=[_sds((t, kvh * gw), BF16), _sds((kvh * nblk, cols), F32)],
        scratch_shapes=[pltpu.VMEM((t, gw), BF16), pltpu.VMEM((t + WINDOW, HEAD_DIM), BF16),
                        pltpu.VMEM((t + WINDOW, HEAD_DIM), BF16)],
        compiler_params=_params(1),
    )(qb, kv, kv, gq, gk, sinks, bias)


def _swa_bwd(name, qb, kv, gq, gk, sinks, bias, lse, do, group):
    t = qb.shape[0]
    kvh = kv.shape[1] // (2 * HEAD_DIM)
    kvw = kvh * HEAD_DIM
    nblk = t // WINDOW
    gw = group * HEAD_DIM
    band = 2 * WINDOW
    cols = group * WINDOW

    def body(q_ref, k_ref, v_ref, gq_ref, gk_ref, sink_ref, bias_ref, lse_ref, do_ref,
             dq_ref, dkv_ref, dgq_ref, dgk_ref, dsink_ref, dbias_ref,
             qs, kpad, vpad, dqs, dk_acc, dv_acc, dsink_acc):
        @pl.when(pl.program_id(0) == 0)
        def _():
            dgq_ref[...] = jnp.zeros_like(dgq_ref)
            dgk_ref[...] = jnp.zeros_like(dgk_ref)
            dkv_ref[...] = jnp.zeros_like(dkv_ref)

        for g in range(group):
            qs[:, g * HEAD_DIM:(g + 1) * HEAD_DIM] = _qhead(q_ref[:, g * HEAD_DIM:(g + 1) * HEAD_DIM], gq_ref[...]).astype(BF16)
        kpad[0:WINDOW, :] = jnp.zeros((WINDOW, HEAD_DIM), BF16)
        vpad[0:WINDOW, :] = jnp.zeros((WINDOW, HEAD_DIM), BF16)
        kpad[WINDOW:, :] = _rms(_kv_head(k_ref, kvh), gk_ref[...]).astype(BF16)
        vpad[WINDOW:, :] = _kv_head(v_ref, kvh).astype(BF16)
        dk_acc[...] = jnp.zeros_like(dk_acc)
        dv_acc[...] = jnp.zeros_like(dv_acc)
        dsink_acc[...] = jnp.zeros_like(dsink_acc)
        dbias_ref[...] = jnp.zeros_like(dbias_ref)
        sink = sink_ref[0]

        def block(n, carry):
            start = pl.multiple_of(n * WINDOW, WINDOW)
            kb = kpad[pl.ds(start, band), :]
            vb = vpad[pl.ds(start, band), :]
            q = _swa_stack(qs, start, group)
            dob = _swa_stack(do_ref, start, group)
            lse_n = lse_ref[pl.ds(n, 1), :]
            s = _dot_nt(kb, q) + bias_ref[0] + _swa_mask(n, group)
            p = jnp.exp(s - lse_n)
            dp = _dot_nt(vb, dob)
            dsum = jnp.sum(p * dp, axis=0, keepdims=True)
            ds = p * (dp - dsum)
            dsb = ds.astype(BF16)
            dsink_acc[...] -= jnp.exp(sink - lse_n) * dsum
            dbias_ref[0] += ds
            dq = _dot_tn(dsb, kb)
            for g in range(group):
                dqs[pl.ds(start, WINDOW), g * HEAD_DIM:(g + 1) * HEAD_DIM] = dq[g * WINDOW:(g + 1) * WINDOW]
            dk_acc[pl.ds(start, band), :] += _dot(dsb, q)
            dv_acc[pl.ds(start, band), :] += _dot(p.astype(BF16), dob)
            return carry

        lax.fori_loop(0, nblk, block, 0)
        for g in range(group):
            _, q_vjp = jax.vjp(_qhead, q_ref[:, g * HEAD_DIM:(g + 1) * HEAD_DIM], gq_ref[...])
            dqp, dgq = q_vjp(dqs[:, g * HEAD_DIM:(g + 1) * HEAD_DIM])
            dq_ref[:, g * HEAD_DIM:(g + 1) * HEAD_DIM] = dqp.astype(BF16)
            dgq_ref[...] += dgq
            dsink_g = jnp.sum(dsink_acc[:, g * WINDOW:(g + 1) * WINDOW], axis=1, keepdims=True)
            dsink_ref[0, g:g + 1, :] = jnp.broadcast_to(dsink_g, (1, LANES))
        _, k_vjp = jax.vjp(_rms, _kv_head(k_ref, kvh), gk_ref[...])
        dkp, dgk = k_vjp(dk_acc[WINDOW:, :])
        dgk_ref[...] += dgk
        mine = lax.broadcasted_iota(jnp.int32, (t, kvw), 1) // HEAD_DIM == pl.program_id(0)
        dkv_ref[:, 0:kvw] = jnp.where(mine, jnp.concatenate([dkp] * kvh, axis=1), dkv_ref[:, 0:kvw])
        dkv_ref[:, kvw:] = jnp.where(mine, jnp.concatenate([dv_acc[WINDOW:, :]] * kvh, axis=1), dkv_ref[:, kvw:])

    fixed = lambda h: (0, 0)
    per = lambda h: (h, 0, 0)
    wide = pl.BlockSpec((t, gw), lambda h: (0, h))
    vec = pl.BlockSpec((1, HEAD_DIM), fixed)
    bias_spec = pl.BlockSpec((1, band, cols), per)
    return pl.pallas_call(
        body, name=name, grid=(kvh,),
        in_specs=[wide, pl.BlockSpec((t, kvw), lambda h: (0, 0)), pl.BlockSpec((t, kvw), lambda h: (0, 1)), vec, vec,
                  pl.BlockSpec((1, 1, cols), per), bias_spec, pl.BlockSpec((nblk, cols), lambda h: (h, 0)), wide],
        out_specs=[wide, pl.BlockSpec((t, 2 * kvw), fixed), vec, vec, pl.BlockSpec((1, group, LANES), per), bias_spec],
        out_shape=[_sds((t, kvh * gw), BF16), _sds((t, 2 * kvw), F32),
                   _sds((1, HEAD_DIM), F32), _sds((1, HEAD_DIM), F32),
                   _sds((kvh, group, LANES), F32), _sds((kvh, band, cols), F32)],
        scratch_shapes=[pltpu.VMEM((t, gw), BF16), pltpu.VMEM((t + WINDOW, HEAD_DIM), BF16),
                        pltpu.VMEM((t + WINDOW, HEAD_DIM), BF16), pltpu.VMEM((t, gw), F32),
                        pltpu.VMEM((t + WINDOW, HEAD_DIM), F32), pltpu.VMEM((t + WINDOW, HEAD_DIM), F32),
                        pltpu.VMEM((1, cols), F32)],
        compiler_params=_params(1),
    )(qb, kv, kv, gq, gk, sinks, bias, lse, do)


def _local_step(x, target, p, comm):
    t, d = x.shape
    n_heads = d // HEAD_DIM
    kv_heads = n_heads // 8
    group = n_heads // kv_heads
    hw = n_heads * HEAD_DIM
    gate_col = 3 * hw // LANES
    kvw = kv_heads * HEAD_DIM
    grads = {}

    def mlp_fwd(tag, h, g, layer, last=False):
        w_up, = comm.weights([f"w_up{layer}"], h)
        a, hn = _norm_matmul(f"{tag}_up", h, g, w_up, relu2=True)
        w_down, = comm.weights([f"w_down{layer}"], a)
        out = _matmul_res_loss(f"{tag}_down", a, w_down, h, target) if last else _matmul_res(f"{tag}_down", a, w_down, h)
        return out, (h, g, hn, a, w_up, w_down)

    def mlp_bwd(tag, saved, layer, dy, dy16):
        h, g, hn, a, w_up, w_down = saved
        du = _matmul_nt(f"{tag}_du", dy16, w_down, a=a)
        dw_down = _matmul_tn(f"{tag}_dwdown", a, dy16)
        dw_up = _matmul_tn(f"{tag}_dwup", hn, du, col_blocks=w_up.shape[0])
        sent = comm.send_grads(tag, {f"w_down{layer}": dw_down, f"w_up{layer}": dw_up})
        return _matmul_nt_rmsbwd(f"{tag}_dh", du, w_up, h, g, dy, bf16_too=True, after=sent)

    bias = _bias_expand("b_bias", p["rel_bias"].T).reshape(kv_heads, group, 2 * WINDOW, WINDOW)
    bias = bias.transpose(0, 2, 1, 3).reshape(kv_heads, 2 * WINDOW, group * WINDOW)
    comm.prefetch(["w_in_a"], bias)
    w_in, = comm.weights(["w_in_a"], None)
    proj, xn1 = _norm_matmul("a_inproj", x, p["g_attn"][0], w_in, tn=640, w_rows=True)
    ahead = comm.prefetch(["w_out_a"], proj)
    b_pad = jnp.pad(p["b_f"], ((0, 0), (0, LANES - n_heads)))
    c = _gate_fwd("a_gate", proj, b_pad, n_heads, gate_col, after=ahead)
    crow = c[:, :n_heads].T.reshape(n_heads // 2, 2, t)
    o_a, lse_a = _fox_fwd("a_attn", proj, c, crow, p["gq_a"], p["gk_a"], n_heads)
    ahead = comm.prefetch(["w_up0", "w_down0", "w_kv", "w_q_b", "w_out_b"], o_a)
    w_out_a, = comm.weights(["w_out_a"], o_a)
    h1 = _matmul_res("a_outproj", o_a, w_out_a, x, after=ahead)
    h2, mlp0 = mlp_fwd("mlp0", h1, p["g_mlp"][0], 0)

    ahead = comm.prefetch(["w_up1", "w_down1"], h2)
    w_kv, w_q_b = comm.weights(["w_kv", "w_q_b"], h2)
    kv, hn_kv = _norm_matmul("kv_proj", h2, p["g_kv"], w_kv, tn=2 * kvw, after=ahead)
    qb, hn_q = _norm_matmul("b_qproj", h2, p["g_attn"][1], w_q_b, tn=512)
    gqb, gkb = p["gq_b"], p["gk_b"].reshape(1, HEAD_DIM)
    sink_rows = jnp.broadcast_to(p["sinks"].reshape(kv_heads, 1, group, 1), (kv_heads, 1, group, WINDOW)).reshape(kv_heads, 1, group * WINDOW)
    o_b, lse_b = _swa_fwd("b_attn", qb, kv, gqb, gkb, sink_rows, bias, group)
    w_out_b, = comm.weights(["w_out_b"], o_b)
    h3 = _matmul_res("b_outproj", o_b, w_out_b, h2)
    (dy, dy16, loss_tile), mlp1 = mlp_fwd("mlp1", h3, p["g_mlp"][1], 1, last=True)

    dh3, dh3_16, dg_mlp1 = mlp_bwd("mlp1", mlp1, 1, dy, dy16)
    do_b = _matmul_nt("b_do", dh3_16, w_out_b)
    dw_out_b = _matmul_tn("b_dwout", o_b, dh3_16)
    dqb, dkv, grads["gq_b"], dgk_b, dsink, dbias = _swa_bwd(
        "b_attn_bwd", qb, kv, gqb, gkb, sink_rows, bias, lse_b, do_b, group)
    grads["gk_b"] = dgk_b
    grads["sinks"] = dsink[:, :, 0].reshape(1, n_heads)
    dbias = dbias.reshape(kv_heads, 2 * WINDOW, group, WINDOW).transpose(0, 2, 1, 3)
    grads["rel_bias"] = _bias_reduce("b_dbias", dbias.reshape(n_heads, WINDOW * 2 * WINDOW)).T
    dw_q_b = _matmul_tn("b_dwq", hn_q, dqb)
    dh2, dg_attn1 = _matmul_nt_rmsbwd("b_dhq", dqb, w_q_b, h2, p["g_attn"][1], dh3)
    dw_kv = _matmul_tn("kv_dw", hn_kv, dkv)
    sent = comm.send_grads("attn_b", {"w_out_b": dw_out_b, "w_q_b": dw_q_b, "w_kv": dw_kv})
    dh2, dh2_16, dg_kv = _matmul_nt_rmsbwd("kv_dh", dkv, w_kv, h2, p["g_kv"], dh2, bf16_too=True, after=sent)
    grads["g_kv"] = dg_kv
    dh1, dh1_16, dg_mlp0 = mlp_bwd("mlp0", mlp0, 0, dh2, dh2_16)
    grads["g_mlp"] = (dg_mlp0, dg_mlp1)

    do_a = _matmul_nt("a_do", dh1_16, w_out_a)
    dw_out_a = _matmul_tn("a_dwout", o_a, dh1_16)
    sent = comm.send_grads("attn_a_out", {"w_out_a": dw_out_a})
    dqkv, dc, grads["gq_a"], grads["gk_a"] = _fox_bwd(
        "a_attn_bwd", proj, c, crow, p["gq_a"], p["gk_a"], lse_a, do_a, n_heads, after=sent)
    dfl, db_f = _gate_bwd("a_gate_bwd", proj, b_pad, dc, n_heads, gate_col)
    grads["b_f"] = db_f
    dw_in = (_matmul_tn("a_dwin", dqkv, xn1, tk=512), _matmul_tn("a_dwin_gate", dfl, xn1))
    sent = comm.send_grads("attn_a_in", {"w_in_a": dw_in})
    grad_x, dg_attn0 = _matmul_nt_rmsbwd("a_dx", (dqkv, dfl), w_in, x, p["g_attn"][0], dh1, w_rows=True, after=sent)
    grads["g_attn"] = (dg_attn0, dg_attn1)
    return loss_tile, grad_x, grads


EVERYONE = (1, 2, 3, 4, 5, 6, 7)
SAME_CORE = (1, 2, 4, 6)
OTHER_CHIPS = (2, 4, 6)
RELAY_COLLECTIVE_ID = 0


class _InFlight:
    def __init__(self, scatter, ks, send_sems, recv_sems, srcs, lands, token):
        self.scatter, self.ks, self.send_sems, self.recv_sems = scatter, ks, send_sems, recv_sems
        self.srcs, self.lands, self.token = list(srcs), list(lands), token


def _mesh_peers(ks=EVERYONE):
    x, y, c = lax.axis_index("x"), lax.axis_index("y"), lax.axis_index("c")
    peers = []
    for k in ks:
        px, py, pc = x ^ ((k >> 2) & 1), y ^ ((k >> 1) & 1), c ^ (k & 1)
        peers.append(((px, py, pc), 4 * px + 2 * py + pc))
    return 4 * x + 2 * y + c, peers


_HBM_SPEC = pl.BlockSpec(memory_space=pltpu.HBM)
_SEM_SPEC = pl.BlockSpec(memory_space=pltpu.SEMAPHORE)
_SIDE_EFFECT = pltpu.SideEffectType.DATAFLOW_SIDE_EFFECTING


def _exchange_start(name, arrays, scatter, collective_id, ks=EVERYONE):
    n = len(arrays)
    me, _ = _mesh_peers()
    lands = []
    for a in arrays:
        own = lax.dynamic_index_in_dim(a, me, 0, keepdims=False) if scatter else a
        shape = a.shape if scatter else (N_DEV,) + a.shape
        lands.append(lax.dynamic_update_index_in_dim(lax.empty(shape, a.dtype), own, me, 0))

    def body(*refs):
        src, land = refs[:n], refs[n:2 * n]
        send_sems, recv_sems, token = refs[2 * n], refs[2 * n + 1], refs[-1]
        pos, peers = _mesh_peers(ks)
        barrier = pltpu.get_barrier_semaphore()
        for peer, _ in peers:
            pl.semaphore_signal(barrier, inc=1, device_id=peer, device_id_type=pl.DeviceIdType.MESH)
        pl.semaphore_wait(barrier, len(peers))
        for a in range(n):
            for k, (peer, peer_pos) in enumerate(peers):
                pltpu.make_async_remote_copy(
                    src_ref=src[a].at[peer_pos] if scatter else src[a], dst_ref=land[a].at[pos],
                    send_sem=send_sems.at[a * len(ks) + k], recv_sem=recv_sems.at[a * len(ks) + k],
                    device_id=peer, device_id_type=pl.DeviceIdType.MESH).start()
        token[...] = jnp.zeros_like(token)

    operands = [pltpu.with_memory_space_constraint(a, pltpu.HBM) for a in list(arrays) + lands]
    outs = pl.pallas_call(
        body, name=name,
        out_shape=(pltpu.SemaphoreType.DMA((n * len(ks),)), pltpu.SemaphoreType.DMA((n * len(ks),)),
                   *[pltpu.HBM(a.shape, a.dtype) for a in operands], _sds((8, LANES), F32)),
        in_specs=[_HBM_SPEC] * (2 * n),
        out_specs=(_SEM_SPEC, _SEM_SPEC, *[_HBM_SPEC] * (2 * n), pl.BlockSpec(memory_space=pltpu.VMEM)),
        input_output_aliases={i: 2 + i for i in range(2 * n)},
        compiler_params=pltpu.CompilerParams(has_side_effects=_SIDE_EFFECT, collective_id=collective_id),
    )(*operands)
    return _InFlight(scatter, ks, outs[0], outs[1], outs[2:2 + n], outs[2 + n:2 + 2 * n], outs[-1])


def _exchange_wait(name, flight, which, after):
    m = len(which)
    scatter, ks = flight.scatter, flight.ks

    def body(*refs):
        src, land = refs[:m], refs[m:2 * m]
        send_sems, recv_sems = refs[2 * m], refs[2 * m + 1]
        _, peers = _mesh_peers(ks)
        for i, a in enumerate(which):
            for k, (peer, peer_pos) in enumerate(peers):
                cp = pltpu.make_async_remote_copy(
                    src_ref=src[i].at[peer_pos] if scatter else src[i], dst_ref=land[i].at[peer_pos],
                    send_sem=send_sems.at[a * len(ks) + k], recv_sem=recv_sems.at[a * len(ks) + k],
                    device_id=peer, device_id_type=pl.DeviceIdType.MESH)
                cp.wait_send()
                cp.wait_recv()

    operands = [flight.srcs[a] for a in which] + [flight.lands[a] for a in which]
    outs = pl.pallas_call(
        body, name=name, out_shape=tuple(pltpu.HBM(a.shape, a.dtype) for a in operands),
        in_specs=[_HBM_SPEC] * (2 * m) + [_SEM_SPEC, _SEM_SPEC, pl.BlockSpec(memory_space=pl.ANY)],
        out_specs=tuple([_HBM_SPEC] * (2 * m)), input_output_aliases={i: i for i in range(2 * m)},
        compiler_params=pltpu.CompilerParams(has_side_effects=_SIDE_EFFECT),
    )(*operands, flight.send_sems, flight.recv_sems, after)
    return list(outs[m:])


def _relay_start(name, lands):
    n = len(lands)

    def body(*refs):
        land, send_sems, recv_sems, token = refs[:n], refs[n], refs[n + 1], refs[-1]
        _, peers = _mesh_peers(OTHER_CHIPS)
        sibling = (lax.axis_index("x"), lax.axis_index("y"), 1 - lax.axis_index("c"))
        barrier = pltpu.get_barrier_semaphore()
        pl.semaphore_signal(barrier, inc=1, device_id=sibling, device_id_type=pl.DeviceIdType.MESH)
        pl.semaphore_wait(barrier, 1)
        for a in range(n):
            for k, (_, peer_pos) in enumerate(peers):
                pltpu.make_async_remote_copy(
                    src_ref=land[a].at[peer_pos], dst_ref=land[a].at[peer_pos],
                    send_sem=send_sems.at[a * len(peers) + k], recv_sem=recv_sems.at[a * len(peers) + k],
                    device_id=sibling, device_id_type=pl.DeviceIdType.MESH).start()
        token[...] = jnp.zeros_like(token)

    count = n * len(OTHER_CHIPS)
    outs = pl.pallas_call(
        body, name=name,
        out_shape=(pltpu.SemaphoreType.DMA((count,)), pltpu.SemaphoreType.DMA((count,)),
                   *[pltpu.HBM(a.shape, a.dtype) for a in lands], _sds((8, LANES), F32)),
        in_specs=[_HBM_SPEC] * n,
        out_specs=(_SEM_SPEC, _SEM_SPEC, *[_HBM_SPEC] * n, pl.BlockSpec(memory_space=pltpu.VMEM)),
        input_output_aliases={i: 2 + i for i in range(n)},
        compiler_params=pltpu.CompilerParams(has_side_effects=_SIDE_EFFECT, collective_id=RELAY_COLLECTIVE_ID),
    )(*[pltpu.with_memory_space_constraint(a, pltpu.HBM) for a in lands])
    return _InFlight(False, OTHER_CHIPS, outs[0], outs[1], [], outs[2:2 + n], outs[-1])


def _relay_wait(name, flight, which, after):
    m = len(which)

    def body(*refs):
        land, send_sems, recv_sems = refs[:m], refs[m], refs[m + 1]
        _, peers = _mesh_peers(OTHER_CHIPS)
        sibling = (lax.axis_index("x"), lax.axis_index("y"), 1 - lax.axis_index("c"))
        for i, a in enumerate(which):
            for k, (_, peer_pos) in enumerate(peers):
                cp = pltpu.make_async_remote_copy(
                    src_ref=land[i].at[peer_pos], dst_ref=land[i].at[peer_pos ^ 1],
                    send_sem=send_sems.at[a * len(peers) + k], recv_sem=recv_sems.at[a * len(peers) + k],
                    device_id=sibling, device_id_type=pl.DeviceIdType.MESH)
                cp.wait_send()
                cp.wait_recv()

    operands = [flight.lands[a] for a in which]
    outs = pl.pallas_call(
        body, name=name, out_shape=tuple(pltpu.HBM(a.shape, a.dtype) for a in operands),
        in_specs=[_HBM_SPEC] * m + [_SEM_SPEC, _SEM_SPEC, pl.BlockSpec(memory_space=pl.ANY)],
        out_specs=tuple([_HBM_SPEC] * m), input_output_aliases={i: i for i in range(m)},
        compiler_params=pltpu.CompilerParams(has_side_effects=_SIDE_EFFECT),
    )(*operands, flight.send_sems, flight.recv_sems, after)
    return list(outs)


def _sum_parts(p_ref):
    g = p_ref[0].astype(F32)
    for dev in range(1, N_DEV):
        g = g + p_ref[dev].astype(F32)
    return g


def _adam_update(g, w, m, v):
    m_new = ADAM_B1 * m + (1.0 - ADAM_B1) * g
    v_new = ADAM_B2 * v + (1.0 - ADAM_B2) * jnp.square(g)
    m_hat = m_new / (1.0 - ADAM_B1 ** ADAM_STEP)
    v_hat = v_new / (1.0 - ADAM_B2 ** ADAM_STEP)
    return -ADAM_LR * (m_hat / (jnp.sqrt(v_hat) + ADAM_EPS) + ADAM_WD * w), m_new, v_new


def _adamw(name, parts, w, m, v, layer=None, into=None):
    r, c = w.shape[-2:]
    row_tiles = layer is None and w.ndim == 3
    if row_tiles:
        r = w.shape[0]
    tr = 256 if r % 256 == 0 and not row_tiles else r
    n_into = 0 if into is None else len(into)
    at = (slice(None), 0, slice(None)) if row_tiles else Ellipsis

    def body(p_ref, w_ref, m_ref, v_ref, *refs):
        g_ref, d_ref, mo_ref, vo_ref = refs[n_into:]
        g = _sum_parts(p_ref)
        g_ref[at] = g
        d_ref[at], mo_ref[at], vo_ref[at] = _adam_update(g, w_ref[at], m_ref[at], v_ref[at])

    if row_tiles:
        blk = pl.BlockSpec((r, 1, c), lambda i: (0, 0, 0))
    elif layer is None:
        blk = pl.BlockSpec((tr, c), lambda i: (i, 0))
    else:
        blk = pl.BlockSpec((None, tr, c), lambda i: (layer, i, 0))
    return pl.pallas_call(
        body, name=name, grid=(r // tr,),
        in_specs=[pl.BlockSpec((N_DEV, tr, c), lambda i: (0, i, 0)), blk, blk, blk] + [pl.BlockSpec(memory_space=pl.ANY)] * n_into,
        out_specs=[blk] * 4, out_shape=[_sds(w.shape, F32)] * 4,
        input_output_aliases={4 + i: i for i in range(n_into)}, compiler_params=_params(1),
    )(parts, w, m, v, *(into or ()))


SMALL_PACK_ROWS = 16
LOSS_ROW = 11


def _small_rows(grads, loss_tile):
    return [(0, 1, grads["g_attn"][0]), (1, 1, grads["g_attn"][1]), (2, 1, grads["g_mlp"][0]), (3, 1, grads["g_mlp"][1]),
            (4, 1, grads["g_kv"]), (5, 1, grads["b_f"]), (6, 1, grads["gq_a"]), (7, 1, grads["gk_a"]), (8, 1, grads["gk_b"]),
            (9, 1, grads["gq_b"]), (10, 1, grads["sinks"]), (LOSS_ROW, 1, loss_tile)]


SMALL_ROWS = {"g_attn": (0, 2), "g_mlp": (2, 2), "g_kv": (4, 1), "b_f": (5, 1), "gq_a": (6, 1), "gk_a": (7, 1),
              "gk_b": (8, 1), "gq_b": (9, 1), "sinks": (10, 1)}


def _pack_small(name, pieces, d):
    def body(*refs):
        out = refs[-1]
        out[...] = jnp.zeros_like(out)
        for (row, rows, _), ref in zip(pieces, refs[:-1]):
            out[row:row + rows, 0:ref.shape[1]] = ref[0:rows, :]

    return pl.pallas_call(body, name=name, out_shape=_sds((SMALL_PACK_ROWS, d), F32), compiler_params=_params(0))(
        *[piece for _, _, piece in pieces])


def _adamw_small(name, parts, parts_rel_bias, w, m, v):
    def body(*refs):
        ins, outs = refs[2:2 + 3 * len(SMALL)], refs[2 + 3 * len(SMALL):]
        pack, rel = _sum_parts(refs[0]), _sum_parts(refs[1])
        for i, k in enumerate(SMALL):
            w_ref, m_ref, v_ref = ins[3 * i:3 * i + 3]
            if k == "rel_bias":
                g = rel
            else:
                row, rows = SMALL_ROWS[k]
                g = pack[row:row + rows, 0:w_ref.shape[1]]
            outs[4 * i][...] = g
            outs[4 * i + 1][...], outs[4 * i + 2][...], outs[4 * i + 3][...] = _adam_update(g, w_ref[...], m_ref[...], v_ref[...])
        outs[-1][...] = pack[LOSS_ROW:LOSS_ROW + 1, 0:LANES]

    operands = [parts, parts_rel_bias] + [t[k] for k in SMALL for t in (w, m, v)]
    out_shape = [_sds(w[k].shape, F32) for k in SMALL for _ in range(4)] + [_sds((1, LANES), F32)]
    outs = pl.pallas_call(body, name=name, out_shape=out_shape, compiler_params=_params(0))(*operands)
    return {k: outs[4 * i:4 * i + 4] for i, k in enumerate(SMALL)}, outs[-1]


class _Comm:
    ORDER = ("w_in_a", "w_out_a", "w_up0", "w_down0", "w_kv", "w_q_b", "w_out_b", "w_up1", "w_down1")

    def __init__(self, shards, d, n_in):
        self.d, self.n_in = d, n_in
        self.ids = iter(range(RELAY_COLLECTIVE_ID + 1, RELAY_COLLECTIVE_ID + 16))
        self.flight = _exchange_start("gather_start", [shards[n].astype(BF16) for n in self.ORDER], False, next(self.ids), SAME_CORE)
        self.relays, self.sent = {}, []

    def prefetch(self, names, after):
        which = [self.ORDER.index(n) for n in names]
        landed = _exchange_wait(f"gather_wait_{names[0]}", self.flight, which, self.flight.token if after is None else after)
        relay = _relay_start(f"gather_relay_{names[0]}", landed)
        for n in names:
            self.relays[n] = (relay, names)
        return relay.token

    def weights(self, names, after):
        relay, group = self.relays[names[0]]
        landed = _relay_wait(f"gather_relay_wait_{names[0]}", relay, [group.index(n) for n in names],
                             relay.token if after is None else after)
        return [self._whole(n, g) for n, g in zip(names, landed)]

    def _whole(self, name, g):
        if name == "w_in_a":
            return _join_row_blocks("w_in_join", g, -(-self.n_in // LANES) * LANES)
        if name.startswith("w_up"):
            return g
        return g.reshape(-1, g.shape[-1])

    def _chunks(self, name, g):
        if name == "w_in_a":
            return _split_row_blocks("dw_in_split", list(g), N_DEV, self.n_in // N_DEV)
        if name.startswith("w_up"):
            return g
        return g.reshape(N_DEV, g.shape[0] // N_DEV, g.shape[1])

    def send_grads(self, tag, partials):
        names = list(partials)
        flight = _exchange_start(f"scatter_start_{tag}", [self._chunks(n, partials[n]) for n in names], True, next(self.ids))
        self.sent.append((tag, flight, names))
        return flight.token

    def received(self, index, after):
        tag, flight, names = self.sent[index]
        landed = _exchange_wait(f"scatter_wait_{tag}", flight, list(range(len(names))), after)
        return dict(zip(names, landed))


def kernel(x, g_attn, g_mlp, w_in_a, b_f, gq_a, gk_a, w_out_a, g_kv, w_kv, gk_b, w_q_b, gq_b, sinks, rel_bias, w_out_b, w_up, w_down, loss_target, m_g_attn, m_g_mlp, m_w_in_a, m_b_f, m_gq_a, m_gk_a, m_w_out_a, m_g_kv, m_w_kv, m_gk_b, m_w_q_b, m_gq_b, m_sinks, m_rel_bias, m_w_out_b, m_w_up, m_w_down, v_g_attn, v_g_mlp, v_w_in_a, v_b_f, v_gq_a, v_gk_a, v_w_out_a, v_g_kv, v_w_kv, v_gk_b, v_w_q_b, v_gq_b, v_sinks, v_rel_bias, v_w_out_b, v_w_up, v_w_down):
    w = dict(g_attn=g_attn, g_mlp=g_mlp, w_in_a=w_in_a, b_f=b_f, gq_a=gq_a, gk_a=gk_a, w_out_a=w_out_a, g_kv=g_kv,
             w_kv=w_kv, gk_b=gk_b, w_q_b=w_q_b, gq_b=gq_b, sinks=sinks, rel_bias=rel_bias, w_out_b=w_out_b,
             w_up=w_up, w_down=w_down)
    mom = dict(g_attn=m_g_attn, g_mlp=m_g_mlp, w_in_a=m_w_in_a, b_f=m_b_f, gq_a=m_gq_a, gk_a=m_gk_a, w_out_a=m_w_out_a,
               g_kv=m_g_kv, w_kv=m_w_kv, gk_b=m_gk_b, w_q_b=m_w_q_b, gq_b=m_gq_b, sinks=m_sinks, rel_bias=m_rel_bias,
               w_out_b=m_w_out_b, w_up=m_w_up, w_down=m_w_down)
    var = dict(g_attn=v_g_attn, g_mlp=v_g_mlp, w_in_a=v_w_in_a, b_f=v_b_f, gq_a=v_gq_a, gk_a=v_gk_a, w_out_a=v_w_out_a,
               g_kv=v_g_kv, w_kv=v_w_kv, gk_b=v_gk_b, w_q_b=v_w_q_b, gq_b=v_gq_b, sinks=v_sinks, rel_bias=v_rel_bias,
               w_out_b=v_w_out_b, w_up=v_w_up, w_down=v_w_down)
    d = x.shape[2]
    where = {"w_in_a": ("w_in_a", 0), "w_out_a": ("w_out_a", 0), "w_kv": ("w_kv", None), "w_q_b": ("w_q_b", 0),
             "w_out_b": ("w_out_b", 0), "w_up0": ("w_up", 0), "w_up1": ("w_up", 1), "w_down0": ("w_down", 0),
             "w_down1": ("w_down", 1)}
    flip = lambda tree: {**tree, "w_in_a": jnp.swapaxes(tree["w_in_a"], 1, 2)}
    w, mom, var = flip(w), flip(mom), flip(var)
    shards = {n: (w[k] if layer is None else w[k][layer]) for n, (k, layer) in where.items()}
    comm = _Comm(shards, d, w_in_a.shape[2] * N_DEV)
    loss_tile, grad_x, grads = _local_step(x[0], loss_target[0], {k: w[k] for k in SMALL}, comm)

    small_flight = _exchange_start(
        "gather_small_grads", [_pack_small("pack_small", _small_rows(grads, loss_tile), d), grads["rel_bias"]], False, next(comm.ids))
    res, after = {}, small_flight.token
    for index in range(len(comm.sent)):
        for n, parts in comm.received(index, after).items():
            k, layer = where[n]
            if n == "w_in_a":
                rows_first = lambda a: jnp.swapaxes(a, 0, 1)
                res[k] = [rows_first(a) for a in _adamw(f"adam_{n}", parts, rows_first(w[k]), rows_first(mom[k]), rows_first(var[k]))]
            else:
                res[k] = _adamw(f"adam_{n}", parts, w[k], mom[k], var[k], layer, res.get(k))
            after = res[k][0]
    as_rows = lambda tree: {k: tree[k] if tree[k].ndim == 2 else tree[k].reshape(1, -1) for k in SMALL}
    small, loss_row = _adamw_small("adam_small", *_exchange_wait("gather_small_wait", small_flight, [0, 1], after),
                                   as_rows(w), as_rows(mom), as_rows(var))
    loss = loss_row[0, 0]
    for k in SMALL:
        res[k] = [a.reshape(w[k].shape) for a in small[k]]
    res["w_in_a"] = [jnp.swapaxes(a, 1, 2) for a in res["w_in_a"]]

    outs = [loss, grad_x[None]]
    for i in range(4):
        outs.extend(res[k][i] for k in WEIGHTS)
    return tuple(outs)
```

```python
import numpy as np
import jax
import jax.numpy as jnp
from jax import lax
from jax.experimental import pallas as pl
from jax.experimental.pallas import tpu as pltpu

F32 = jnp.float32
BF16 = jnp.bfloat16

N_DEV = 8
HEAD_DIM = 64
WINDOW = 128
N_BUCKETS = 32
REL_MAX_DIST = 128
NORM_EPS = 1e-6
NEG = -1e30
LANES = 128
VMEM_LIMIT = 56 * 1024 * 1024

ADAM_LR = 0.001
ADAM_B1 = 0.9
ADAM_B2 = 0.999
ADAM_EPS = 1e-08
ADAM_WD = 0.01
ADAM_STEP = 10

SMALL = ("g_attn", "g_mlp", "b_f", "gq_a", "gk_a", "g_kv", "gk_b", "gq_b", "sinks", "rel_bias")
WEIGHTS = ("g_attn", "g_mlp", "w_in_a", "b_f", "gq_a", "gk_a", "w_out_a", "g_kv", "w_kv", "gk_b",
           "w_q_b", "gq_b", "sinks", "rel_bias", "w_out_b", "w_up", "w_down")


def _params(n_grid):
    return pltpu.CompilerParams(dimension_semantics=("arbitrary",) * n_grid, vmem_limit_bytes=VMEM_LIMIT)


def _sds(shape, dtype):
    return jax.ShapeDtypeStruct(tuple(shape), dtype)


def _after_operand(after):
    if after is None:
        return [], []
    return [pl.BlockSpec((8, LANES), lambda *_: (0, 0))], [after]


def _rms(x, g):
    return (x * lax.rsqrt(jnp.mean(x * x, axis=-1, keepdims=True) + NORM_EPS)) * g


def _dot_nt(a, b):
    return lax.dot_general(a, b, (((1,), (1,)), ((), ())), preferred_element_type=F32)


def _dot_tn(a, b):
    return lax.dot_general(a, b, (((0,), (0,)), ((), ())), preferred_element_type=F32)


def _dot(a, b):
    return jnp.dot(a, b, preferred_element_type=F32)


def _dot_exact(a, b):
    return jnp.dot(a, b, preferred_element_type=F32, precision=lax.Precision.HIGHEST)


def _norm_matmul(name, x, g, w, *, tn=None, relu2=False, w_rows=False, after=None):
    t, d = x.shape
    blocked = w.ndim == 3
    per_step = 2 if blocked else 1
    if blocked:
        tn = per_step * w.shape[2]
        n = w.shape[0] * w.shape[2]
        w_spec = pl.BlockSpec((per_step, d, w.shape[2]), lambda i, j: (j, 0, 0))
    elif w_rows:
        n = w.shape[0]
        w_spec = pl.BlockSpec((tn, d), lambda i, j: (j, 0))
    else:
        n = w.shape[1]
        w_spec = pl.BlockSpec((d, tn), lambda i, j: (0, j))
    tm = min(1024, t)

    def body(x_ref, g_ref, w_ref, *rest):
        y_ref, xn_ref = rest[-2:]

        @pl.when(pl.program_id(1) == 0)
        def _():
            xn_ref[...] = _rms(x_ref[...], g_ref[...]).astype(BF16)

        for b in range(per_step):
            cols = slice(b * (tn // per_step), (b + 1) * (tn // per_step)) if blocked else slice(None)
            wb = w_ref[b] if blocked else w_ref[...]
            y = _dot_nt(xn_ref[...], wb) if w_rows else _dot(xn_ref[...], wb)
            y_ref[:, cols] = jnp.square(jnp.maximum(y, 0.0)).astype(BF16) if relu2 else y

    extra_specs, extra = _after_operand(after)
    out_shape = [_sds((t, n), BF16 if relu2 else F32), _sds((t, d), BF16)]
    out_specs = [pl.BlockSpec((tm, tn), lambda i, j: (i, j)), pl.BlockSpec((tm, d), lambda i, j: (i, 0))]
    return pl.pallas_call(
        body, name=name, grid=(t // tm, n // tn),
        in_specs=[pl.BlockSpec((tm, d), lambda i, j: (i, 0)), pl.BlockSpec((1, d), lambda i, j: (0, 0)), w_spec] + extra_specs,
        out_specs=out_specs, out_shape=out_shape, compiler_params=_params(2),
    )(x, g.reshape(1, d), w, *extra)


def _matmul_res(name, a, w, res, *, tn=512, after=None):
    t, k = a.shape
    n = w.shape[1]
    tm = min(1024, t)

    def body(a_ref, w_ref, r_ref, *rest):
        rest[-1][...] = r_ref[...] + _dot(a_ref[...], w_ref[...])

    extra_specs, extra = _after_operand(after)
    return pl.pallas_call(
        body, name=name, grid=(t // tm, n // tn),
        in_specs=[pl.BlockSpec((tm, k), lambda i, j: (i, 0)), pl.BlockSpec((k, tn), lambda i, j: (0, j)),
                  pl.BlockSpec((tm, tn), lambda i, j: (i, j))] + extra_specs,
        out_specs=pl.BlockSpec((tm, tn), lambda i, j: (i, j)), out_shape=_sds((t, n), F32),
        compiler_params=_params(2),
    )(a, w, res, *extra)


def _matmul_nt(name, dy, w, *, a=None, tk=1024):
    t, n = dy.shape
    k = w.shape[0]
    tm = min(1024, t)

    def body(dy_ref, w_ref, *rest):
        o_ref = rest[-1]
        r = _dot_nt(dy_ref[...].astype(BF16), w_ref[...])
        if a is not None:
            r = r * (2.0 * jnp.sqrt(rest[0][...].astype(F32)))
        o_ref[...] = r.astype(BF16)

    in_specs = [pl.BlockSpec((tm, n), lambda i, j: (i, 0)), pl.BlockSpec((tk, n), lambda i, j: (j, 0))]
    args = [dy, w]
    if a is not None:
        in_specs.append(pl.BlockSpec((tm, tk), lambda i, j: (i, j)))
        args.append(a)
    return pl.pallas_call(
        body, name=name, grid=(t // tm, k // tk), in_specs=in_specs,
        out_specs=pl.BlockSpec((tm, tk), lambda i, j: (i, j)), out_shape=_sds((t, k), BF16),
        compiler_params=_params(2),
    )(*args)


def _matmul_nt_rmsbwd(name, dy, w, x, g, dres, *, w_rows=False, bf16_too=False, after=None):
    pieces = isinstance(dy, tuple)
    dys = list(dy) if pieces else [dy]
    t = dres.shape[0]
    blocked = w.ndim == 3
    d = w.shape[1] if blocked or w_rows else w.shape[0]
    tm = min(512, t)
    extra_specs, extra = _after_operand(after)
    n_in = len(dys) + 4 + len(extra)

    def body(*refs):
        dy_ref = refs[0]
        w_ref, x_ref, g_ref, r_ref = refs[len(dys):len(dys) + 4]
        dx_ref, dg_ref = refs[n_in:n_in + 2]
        if pieces:
            n_stack, _, k1 = dys[0].shape
            dxn = _dot(refs[1][...], w_ref[n_stack * k1:, :])
            for j in range(n_stack):
                dxn += _dot(dy_ref[j], w_ref[j * k1:(j + 1) * k1, :])
        elif blocked:
            kb = w.shape[2]
            dxn = _dot_nt(dy_ref[:, 0:kb].astype(BF16), w_ref[0])
            for j in range(1, w.shape[0]):
                dxn += _dot_nt(dy_ref[:, j * kb:(j + 1) * kb].astype(BF16), w_ref[j])
        elif w_rows:
            dxn = _dot(dy_ref[...].astype(BF16), w_ref[...])
        else:
            dxn = _dot_nt(dy_ref[...].astype(BF16), w_ref[...])
        _, vjp = jax.vjp(_rms, x_ref[...], g_ref[...])
        dx, dg = vjp(dxn)
        dx_ref[...] = r_ref[...] + dx
        if bf16_too:
            refs[n_in + 2][...] = (r_ref[...] + dx).astype(BF16)

        @pl.when(pl.program_id(0) == 0)
        def _():
            dg_ref[...] = jnp.zeros_like(dg_ref)

        dg_ref[...] += dg

    row = lambda i: (i, 0)
    fixed = lambda i: (0, 0)
    if pieces:
        dy_specs = [pl.BlockSpec((dys[0].shape[0], tm, dys[0].shape[2]), lambda i: (0, i, 0)),
                    pl.BlockSpec((tm, dys[1].shape[1]), row)]
    else:
        dy_specs = [pl.BlockSpec((tm, dy.shape[1]), row)]
    outs = pl.pallas_call(
        body, name=name, grid=(t // tm,),
        in_specs=dy_specs + [pl.BlockSpec(w.shape, (lambda i: (0, 0, 0)) if blocked else fixed),
                             pl.BlockSpec((tm, d), row), pl.BlockSpec((1, d), fixed), pl.BlockSpec((tm, d), row)] + extra_specs,
        out_specs=[pl.BlockSpec((tm, d), row), pl.BlockSpec((1, d), fixed)] + [pl.BlockSpec((tm, d), row)] * bf16_too,
        out_shape=[_sds((t, d), F32), _sds((1, d), F32)] + [_sds((t, d), BF16)] * bf16_too, compiler_params=_params(1),
    )(*dys, w, x, g.reshape(1, d), dres, *extra)
    return (outs[0], outs[2], outs[1]) if bf16_too else outs


def _matmul_tn(name, a, b, *, tk=1024, tn=1024, col_blocks=None):
    stacked = a.ndim == 3
    t, k1 = a.shape[-2:]
    k = a.shape[0] * k1 if stacked else k1
    n = b.shape[1]
    tk = min(tk, k1)
    per = k1 // tk
    a_spec = (pl.BlockSpec((None, t, tk), lambda i, j: (i // per, 0, i % per)) if stacked
              else pl.BlockSpec((t, tk), lambda i, j: (0, i)))
    if col_blocks:
        tn = n // col_blocks
        out_spec, out_shape = pl.BlockSpec((None, tk, tn), lambda i, j: (j, i, 0)), _sds((col_blocks, k, tn), BF16)
    else:
        tn = min(tn, n)
        out_spec, out_shape = pl.BlockSpec((tk, tn), lambda i, j: (i, j)), _sds((k, n), BF16)

    def body(a_ref, b_ref, o_ref):
        o_ref[...] = _dot_tn(a_ref[...].astype(BF16), b_ref[...].astype(BF16)).astype(BF16)

    return pl.pallas_call(
        body, name=name, grid=(k // tk, n // tn),
        in_specs=[a_spec, pl.BlockSpec((t, tn), lambda i, j: (0, j))],
        out_specs=out_spec, out_shape=out_shape, compiler_params=_params(2),
    )(a, b)


def _join_row_blocks(name, blocks, rows):
    b, r, c = blocks.shape
    tc = min(256, c)

    def body(g_ref, o_ref):
        o_ref[...] = jnp.zeros_like(o_ref)
        for j in range(b):
            o_ref[r * j:r * (j + 1), :] = g_ref[j]

    return pl.pallas_call(
        body, name=name, grid=(c // tc,), in_specs=[pl.BlockSpec((b, r, tc), lambda i: (0, 0, i))],
        out_specs=pl.BlockSpec((rows, tc), lambda i: (0, i)), out_shape=_sds((rows, c), blocks.dtype),
        compiler_params=_params(1),
    )(blocks)


def _split_row_blocks(name, mats, b, r):
    c = mats[0].shape[1]
    tc = min(256, c)

    def body(*refs):
        o_ref = refs[-1]
        for j in range(b):
            first = 0
            for m_ref in refs[:-1]:
                lo, hi = max(r * j, first), min(r * (j + 1), first + m_ref.shape[0])
                if lo < hi:
                    o_ref[j, lo - r * j:hi - r * j, :] = m_ref[lo - first:hi - first, :]
                first += m_ref.shape[0]

    return pl.pallas_call(
        body, name=name, grid=(c // tc,), in_specs=[pl.BlockSpec((m.shape[0], tc), lambda i: (0, i)) for m in mats],
        out_specs=pl.BlockSpec((b, r, tc), lambda i: (0, 0, i)), out_shape=_sds((b, r, c), mats[0].dtype),
        compiler_params=_params(1),
    )(*mats)


def _matmul_res_loss(name, a, w, res, target, *, tn=512):
    t, k = a.shape
    n = w.shape[1]
    tm = min(1024, t)

    def body(a_ref, w_ref, r_ref, t_ref, dy_ref, dyb_ref, l_ref):
        e = r_ref[...] + _dot(a_ref[...], w_ref[...]) - t_ref[...]
        dy_ref[...] = e * (1.0 / n)
        dyb_ref[...] = (e * (1.0 / n)).astype(BF16)

        @pl.when((pl.program_id(0) == 0) & (pl.program_id(1) == 0))
        def _():
            l_ref[...] = jnp.zeros_like(l_ref)

        l_ref[...] += (0.5 / n) * jnp.sum(e * e)

    tile = pl.BlockSpec((tm, tn), lambda i, j: (i, j))
    return pl.pallas_call(
        body, name=name, grid=(t // tm, n // tn),
        in_specs=[pl.BlockSpec((tm, k), lambda i, j: (i, 0)), pl.BlockSpec((k, tn), lambda i, j: (0, j)), tile, tile],
        out_specs=[tile, tile, pl.BlockSpec((8, LANES), lambda i, j: (0, 0))],
        out_shape=[_sds((t, n), F32), _sds((t, n), BF16), _sds((8, LANES), F32)], compiler_params=_params(2),
    )(a, w, res, target)


def _gate_fwd(name, proj, b_pad, n_heads, gate_col, after=None):
    t = proj.shape[0]
    tb = min(256, t)
    tri = jnp.asarray(np.tril(np.ones((tb, tb), np.float32)))
    extra_specs, extra = _after_operand(after)

    def body(p_ref, b_ref, tri_ref, *rest):
        c_ref, carry = rest[-2:]

        @pl.when(pl.program_id(0) == 0)
        def _():
            carry[...] = jnp.zeros_like(carry)

        lane = lax.broadcasted_iota(jnp.int32, (tb, LANES), 1)
        lf = jnp.where(lane < n_heads, jax.nn.log_sigmoid(p_ref[...] + b_ref[...]), 0.0)
        c = _dot_exact(tri_ref[...], lf) + carry[0:1, :]
        c_ref[...] = c
        carry[...] = jnp.broadcast_to(c[tb - 1:tb, :], carry.shape)

    return pl.pallas_call(
        body, name=name, grid=(t // tb,),
        in_specs=[pl.BlockSpec((tb, LANES), lambda i: (i, gate_col)), pl.BlockSpec((1, LANES), lambda i: (0, 0)),
                  pl.BlockSpec((tb, tb), lambda i: (0, 0))] + extra_specs,
        out_specs=pl.BlockSpec((tb, LANES), lambda i: (i, 0)), out_shape=_sds((t, LANES), F32),
        scratch_shapes=[pltpu.VMEM((8, LANES), F32)], compiler_params=_params(1),
    )(proj, b_pad, tri, *extra)


def _gate_bwd(name, proj, b_pad, dc, n_heads, gate_col):
    t = proj.shape[0]
    tb = min(256, t)
    nb = t // tb
    triu = jnp.asarray(np.triu(np.ones((tb, tb), np.float32)))

    def body(p_ref, b_ref, dc_ref, tri_ref, df_ref, db_ref, carry):
        @pl.when(pl.program_id(0) == 0)
        def _():
            carry[...] = jnp.zeros_like(carry)
            db_ref[...] = jnp.zeros_like(db_ref)

        dcv = dc_ref[...]
        dlf = _dot_exact(tri_ref[...], dcv) + carry[0:1, :]
        carry[...] = jnp.broadcast_to(dlf[0:1, :], carry.shape)
        lane = lax.broadcasted_iota(jnp.int32, (tb, LANES), 1)
        z = p_ref[...] + b_ref[...]
        df = jnp.where(lane < n_heads, dlf / (1.0 + jnp.exp(z)), 0.0)
        df_ref[...] = df.astype(BF16)
        db_ref[...] += jnp.sum(df, axis=0, keepdims=True)

    return pl.pallas_call(
        body, name=name, grid=(nb,),
        in_specs=[pl.BlockSpec((tb, LANES), lambda i: (nb - 1 - i, gate_col)), pl.BlockSpec((1, LANES), lambda i: (0, 0)),
                  pl.BlockSpec((tb, LANES), lambda i: (nb - 1 - i, 0)), pl.BlockSpec((tb, tb), lambda i: (0, 0))],
        out_specs=[pl.BlockSpec((tb, LANES), lambda i: (nb - 1 - i, 0)), pl.BlockSpec((1, LANES), lambda i: (0, 0))],
        out_shape=[_sds((t, LANES), BF16), _sds((1, LANES), F32)],
        scratch_shapes=[pltpu.VMEM((8, LANES), F32)], compiler_params=_params(1),
    )(proj, b_pad, dc, triu)


def _qhead(qp, g):
    return _rms(qp, g) * (HEAD_DIM ** -0.5)


def _column(mat, idx):
    lane = lax.broadcasted_iota(jnp.int32, mat.shape, 1)
    return jnp.sum(jnp.where(lane == idx, mat, 0.0), axis=1, keepdims=True)


def _fox_scores(kk, qi, ckey, cq_i, i, bq):
    length = kk.shape[0]
    s = _dot_nt(kk, qi) + cq_i - ckey[:length]
    key = lax.broadcasted_iota(jnp.int32, (length, bq), 0)
    qry = lax.broadcasted_iota(jnp.int32, (length, bq), 1) + i * bq
    return jnp.where(key <= qry, s, NEG)


def _fox_fwd(name, proj, c, crow, gq, gk, n_heads):
    t = proj.shape[0]
    hw = n_heads * HEAD_DIM
    npair = n_heads // 2
    bq = min(512, t)
    nq = t // bq

    def body(q_ref, k_ref, v_ref, c_ref, crow_ref, gq_ref, gk_ref, o_ref, lse_ref):
        hp = pl.program_id(0)
        lse_ref[...] = jnp.zeros_like(lse_ref)
        outs = []
        for hh in range(2):
            sl = slice(hh * HEAD_DIM, (hh + 1) * HEAD_DIM)
            qn = _qhead(q_ref[:, sl], gq_ref[...]).astype(BF16)
            kn = _rms(k_ref[:, sl], gk_ref[...]).astype(BF16)
            v_t = v_ref[:, sl].T.astype(BF16)
            ckey = _column(c_ref[...], 2 * hp + hh)
            cq = crow_ref[0, hh:hh + 1, :]
            o_blocks = []
            for i in range(nq):
                cols = slice(i * bq, (i + 1) * bq)
                length = (i + 1) * bq
                s = _fox_scores(kn[:length], qn[cols], ckey, cq[:, cols], i, bq)
                m = jnp.max(s, axis=0, keepdims=True)
                p = jnp.exp(s - m)
                l = jnp.sum(p, axis=0, keepdims=True)
                o_blocks.append((_dot(v_t[:, :length], p.astype(BF16)) / l).T)
                lse_ref[0, hh:hh + 1, cols] = m + jnp.log(l)
            outs.append(jnp.concatenate(o_blocks, axis=0))
        o_ref[...] = jnp.concatenate(outs, axis=1).astype(BF16)

    col = lambda off: (lambda h: (0, off + h))
    fixed = lambda h: (0, 0)
    return pl.pallas_call(
        body, name=name, grid=(npair,),
        in_specs=[pl.BlockSpec((t, LANES), col(0)), pl.BlockSpec((t, LANES), col(npair)), pl.BlockSpec((t, LANES), col(2 * npair)),
                  pl.BlockSpec((t, LANES), fixed), pl.BlockSpec((1, 2, t), lambda h: (h, 0, 0)),
                  pl.BlockSpec((1, HEAD_DIM), fixed), pl.BlockSpec((1, HEAD_DIM), fixed)],
        out_specs=[pl.BlockSpec((t, LANES), col(0)), pl.BlockSpec((1, 8, t), lambda h: (h, 0, 0))],
        out_shape=[_sds((t, hw), BF16), _sds((npair, 8, t), F32)], compiler_params=_params(1),
    )(proj, proj, proj, c, crow, gq, gk)


def _fox_bwd(name, proj, c, crow, gq, gk, lse, do, n_heads, after=None):
    t = proj.shape[0]
    hw = n_heads * HEAD_DIM
    npair = n_heads // 2
    bq = min(256, t)
    nq = t // bq

    def body(q_ref, k_ref, v_ref, c_ref, crow_ref, gq_ref, gk_ref, lse_ref, do_ref, *rest):
        dqkv_ref, dc_ref, dgq_ref, dgk_ref, dk_acc, dv_acc, dc_acc = rest[-7:]
        hp = pl.program_id(0)

        @pl.when(hp == 0)
        def _():
            dgq_ref[...] = jnp.zeros_like(dgq_ref)
            dgk_ref[...] = jnp.zeros_like(dgk_ref)
            dc_ref[...] = jnp.zeros_like(dc_ref)

        lane = lax.broadcasted_iota(jnp.int32, (t, LANES), 1)
        dqs, dks, dvs = [], [], []
        for hh in range(2):
            sl = slice(hh * HEAD_DIM, (hh + 1) * HEAD_DIM)
            qf, q_vjp = jax.vjp(_qhead, q_ref[:, sl], gq_ref[...])
            kf, k_vjp = jax.vjp(_rms, k_ref[:, sl], gk_ref[...])
            qn, kn, kn_t = qf.astype(BF16), kf.astype(BF16), kf.T.astype(BF16)
            vb = v_ref[:, sl].astype(BF16)
            dob = do_ref[:, sl]
            ckey = _column(c_ref[...], 2 * hp + hh)
            cq = crow_ref[0, hh:hh + 1, :]
            lse_h = lse_ref[0, hh:hh + 1, :]
            dk_acc[...] = jnp.zeros_like(dk_acc)
            dv_acc[...] = jnp.zeros_like(dv_acc)
            dc_acc[...] = jnp.zeros_like(dc_acc)
            dq_blocks = []
            for i in range(nq):
                cols = slice(i * bq, (i + 1) * bq)
                length = (i + 1) * bq
                qi, doi = qn[cols], dob[cols]
                s = _fox_scores(kn[:length], qi, ckey, cq[:, cols], i, bq)
                p = jnp.exp(s - lse_h[:, cols])
                dp = _dot_nt(vb[:length], doi)
                ds = p * (dp - jnp.sum(p * dp, axis=0, keepdims=True))
                dsb = ds.astype(BF16)
                dq_blocks.append(_dot(kn_t[:, :length], dsb).T)
                dk_acc[0:length, :] += _dot(dsb, qi)
                dv_acc[0:length, :] += _dot(p.astype(BF16), doi)
                part = ds[:, 0:LANES]
                for j in range(1, bq // LANES):
                    part = part + ds[:, j * LANES:(j + 1) * LANES]
                dc_acc[0:length, :] += part
            dqp, dgq = q_vjp(jnp.concatenate(dq_blocks, axis=0))
            dkp, dgk = k_vjp(dk_acc[...])
            dgq_ref[...] += dgq
            dgk_ref[...] += dgk
            dqs.append(dqp)
            dks.append(dkp)
            dvs.append(dv_acc[...])
            dc_ref[...] = jnp.where(lane == 2 * hp + hh, -jnp.sum(dc_acc[...], axis=1, keepdims=True), dc_ref[...])
        for part, halves in enumerate((dqs, dks, dvs)):
            dqkv_ref[part] = jnp.concatenate(halves, axis=1).astype(BF16)

    col = lambda off: (lambda h: (0, off + h))
    fixed = lambda h: (0, 0)
    pair_blk = pl.BlockSpec((t, LANES), col(0))
    extra_specs, extra = _after_operand(after)
    return pl.pallas_call(
        body, name=name, grid=(npair,),
        in_specs=[pl.BlockSpec((t, LANES), col(0)), pl.BlockSpec((t, LANES), col(npair)), pl.BlockSpec((t, LANES), col(2 * npair)),
                  pl.BlockSpec((t, LANES), fixed), pl.BlockSpec((1, 2, t), lambda h: (h, 0, 0)),
                  pl.BlockSpec((1, HEAD_DIM), fixed), pl.BlockSpec((1, HEAD_DIM), fixed),
                  pl.BlockSpec((1, 8, t), lambda h: (h, 0, 0)), pair_blk] + extra_specs,
        out_specs=[pl.BlockSpec((3, t, LANES), lambda h: (0, 0, h)), pl.BlockSpec((t, LANES), fixed),
                   pl.BlockSpec((1, HEAD_DIM), fixed), pl.BlockSpec((1, HEAD_DIM), fixed)],
        out_shape=[_sds((3, t, hw), BF16), _sds((t, LANES), F32),
                   _sds((1, HEAD_DIM), F32), _sds((1, HEAD_DIM), F32)],
        scratch_shapes=[pltpu.VMEM((t, HEAD_DIM), F32), pltpu.VMEM((t, HEAD_DIM), F32), pltpu.VMEM((t, LANES), F32)],
        compiler_params=_params(1),
    )(proj, proj, proj, c, crow, gq, gk, lse, do, *extra)


def _t5_bucket_table():
    dist = np.arange(WINDOW)[None, :] + WINDOW - np.arange(2 * WINDOW)[:, None]
    n = np.maximum(dist, 0)
    max_exact = N_BUCKETS // 2
    large = max_exact + (np.log(np.maximum(n, 1) / max_exact) / np.log(REL_MAX_DIST / max_exact)
                         * (N_BUCKETS - max_exact)).astype(np.int32)
    large = np.minimum(large, N_BUCKETS - 1)
    return np.where(n < max_exact, n, large).astype(np.int32).reshape(1, -1)


def _bias_expand(name, rel_bias_t):
    n_heads = rel_bias_t.shape[0]
    tbl = jnp.asarray(_t5_bucket_table())
    width = tbl.shape[1]

    def body(rb_ref, tbl_ref, o_ref):
        onehot = (lax.broadcasted_iota(jnp.int32, (N_BUCKETS, width), 0) == tbl_ref[...]).astype(F32)
        o_ref[...] = _dot_exact(rb_ref[...], onehot)

    return pl.pallas_call(body, name=name, out_shape=_sds((n_heads, width), F32), compiler_params=_params(0))(rel_bias_t, tbl)


def _bias_reduce(name, dbias):
    n_heads, width = dbias.shape
    tbl = jnp.asarray(_t5_bucket_table())

    def body(db_ref, tbl_ref, o_ref):
        onehot = (lax.broadcasted_iota(jnp.int32, (N_BUCKETS, width), 0) == tbl_ref[...]).astype(F32)
        o_ref[...] = lax.dot_general(db_ref[...], onehot, (((1,), (1,)), ((), ())), preferred_element_type=F32,
                                     precision=lax.Precision.HIGHEST)

    return pl.pallas_call(body, name=name, out_shape=_sds((n_heads, N_BUCKETS), F32), compiler_params=_params(0))(dbias, tbl)


def _swa_mask(n, group):
    j = lax.broadcasted_iota(jnp.int32, (2 * WINDOW, group * WINDOW), 0)
    i = lax.broadcasted_iota(jnp.int32, (2 * WINDOW, group * WINDOW), 1) & (WINDOW - 1)
    ok = (j > i) & (j <= i + WINDOW) & ((n > 0) | (j >= WINDOW))
    return jnp.where(ok, 0.0, NEG)


def _swa_stack(ref, start, group):
    return jnp.concatenate([ref[pl.ds(start, WINDOW), g * HEAD_DIM:(g + 1) * HEAD_DIM] for g in range(group)], axis=0)


def _kv_head(ref, n_kv):
    out = ref[:, 0:HEAD_DIM]
    for h in range(1, n_kv):
        out = jnp.where(pl.program_id(0) == h, ref[:, h * HEAD_DIM:(h + 1) * HEAD_DIM], out)
    return out


def _swa_fwd(name, qb, kv, gq, gk, sinks, bias, group):
    t = qb.shape[0]
    kvh = kv.shape[1] // (2 * HEAD_DIM)
    nblk = t // WINDOW
    gw = group * HEAD_DIM
    band = 2 * WINDOW
    cols = group * WINDOW

    def body(q_ref, k_ref, v_ref, gq_ref, gk_ref, sink_ref, bias_ref, o_ref, lse_ref, qs, kpad, vpad):
        for g in range(group):
            qs[:, g * HEAD_DIM:(g + 1) * HEAD_DIM] = _qhead(q_ref[:, g * HEAD_DIM:(g + 1) * HEAD_DIM], gq_ref[...]).astype(BF16)
        kpad[0:WINDOW, :] = jnp.zeros((WINDOW, HEAD_DIM), BF16)
        vpad[0:WINDOW, :] = jnp.zeros((WINDOW, HEAD_DIM), BF16)
        kpad[WINDOW:, :] = _rms(_kv_head(k_ref, kvh), gk_ref[...]).astype(BF16)
        vpad[WINDOW:, :] = _kv_head(v_ref, kvh).astype(BF16)
        sink = sink_ref[0]

        def block(n, carry):
            start = pl.multiple_of(n * WINDOW, WINDOW)
            kb = kpad[pl.ds(start, band), :]
            vb = vpad[pl.ds(start, band), :]
            s = _dot_nt(kb, _swa_stack(qs, start, group)) + bias_ref[0] + _swa_mask(n, group)
            m = jnp.maximum(jnp.max(s, axis=0, keepdims=True), sink)
            e = jnp.exp(s - m)
            l = jnp.sum(e, axis=0, keepdims=True) + jnp.exp(sink - m)
            o_t = _dot_tn(vb, e.astype(BF16)) / l
            for g in range(group):
                o_ref[pl.ds(start, WINDOW), g * HEAD_DIM:(g + 1) * HEAD_DIM] = o_t[:, g * WINDOW:(g + 1) * WINDOW].T.astype(BF16)
            lse_ref[pl.ds(n, 1), :] = m + jnp.log(l)
            return carry

        lax.fori_loop(0, nblk, block, 0)

    fixed = lambda h: (0, 0)
    per = lambda h: (h, 0, 0)
    return pl.pallas_call(
        body, name=name, grid=(kvh,),
        in_specs=[pl.BlockSpec((t, gw), lambda h: (0, h)), pl.BlockSpec((t, kvh * HEAD_DIM), lambda h: (0, 0)),
                  pl.BlockSpec((t, kvh * HEAD_DIM), lambda h: (0, 1)),
                  pl.BlockSpec((1, HEAD_DIM), fixed), pl.BlockSpec((1, HEAD_DIM), fixed),
                  pl.BlockSpec((1, 1, cols), per), pl.BlockSpec((1, band, cols), per)],
        out_specs=[pl.BlockSpec((t, gw), lambda h: (0, h)), pl.BlockSpec((nblk, cols), lambda h: (h, 0))],
        out_shape=[_sds((t, kvh * gw), BF16), _sds((kvh * nblk, cols), F32)],
        scratch_shapes=[pltpu.VMEM((t, gw), BF16), pltpu.VMEM((t + WINDOW, HEAD_DIM), BF16),
                        pltpu.VMEM((t + WINDOW, HEAD_DIM), BF16)],
        compiler_params=_params(1),
    )(qb, kv, kv, gq, gk, sinks, bias)


def _swa_bwd(name, qb, kv, gq, gk, sinks, bias, lse, do, group):
    t = qb.shape[0]
    kvh = kv.shape[1] // (2 * HEAD_DIM)
    kvw = kvh * HEAD_DIM
    nblk = t // WINDOW
    gw = group * HEAD_DIM
    band = 2 * WINDOW
    cols = group * WINDOW

    def body(q_ref, k_ref, v_ref, gq_ref, gk_ref, sink_ref, bias_ref, lse_ref, do_ref,
             dq_ref, dkv_ref, dgq_ref, dgk_ref, dsink_ref, dbias_ref,
             qs, kpad, vpad, dqs, dk_acc, dv_acc, dsink_acc):
        @pl.when(pl.program_id(0) == 0)
        def _():
            dgq_ref[...] = jnp.zeros_like(dgq_ref)
            dgk_ref[...] = jnp.zeros_like(dgk_ref)
            dkv_ref[...] = jnp.zeros_like(dkv_ref)

        for g in range(group):
            qs[:, g * HEAD_DIM:(g + 1) * HEAD_DIM] = _qhead(q_ref[:, g * HEAD_DIM:(g + 1) * HEAD_DIM], gq_ref[...]).astype(BF16)
        kpad[0:WINDOW, :] = jnp.zeros((WINDOW, HEAD_DIM), BF16)
        vpad[0:WINDOW, :] = jnp.zeros((WINDOW, HEAD_DIM), BF16)
        kpad[WINDOW:, :] = _rms(_kv_head(k_ref, kvh), gk_ref[...]).astype(BF16)
        vpad[WINDOW:, :] = _kv_head(v_ref, kvh).astype(BF16)
        dk_acc[...] = jnp.zeros_like(dk_acc)
        dv_acc[...] = jnp.zeros_like(dv_acc)
        dsink_acc[...] = jnp.zeros_like(dsink_acc)
        dbias_ref[...] = jnp.zeros_like(dbias_ref)
        sink = sink_ref[0]

        def block(n, carry):
            start = pl.multiple_of(n * WINDOW, WINDOW)
            kb = kpad[pl.ds(start, band), :]
            vb = vpad[pl.ds(start, band), :]
            q = _swa_stack(qs, start, group)
            dob = _swa_stack(do_ref, start, group)
            lse_n = lse_ref[pl.ds(n, 1), :]
            s = _dot_nt(kb, q) + bias_ref[0] + _swa_mask(n, group)
            p = jnp.exp(s - lse_n)
            dp = _dot_nt(vb, dob)
            dsum = jnp.sum(p * dp, axis=0, keepdims=True)
            ds = p * (dp - dsum)
            dsb = ds.astype(BF16)
            dsink_acc[...] -= jnp.exp(sink - lse_n) * dsum
            dbias_ref[0] += ds
            dq = _dot_tn(dsb, kb)
            for g in range(group):
                dqs[pl.ds(start, WINDOW), g * HEAD_DIM:(g + 1) * HEAD_DIM] = dq[g * WINDOW:(g + 1) * WINDOW]
            dk_acc[pl.ds(start, band), :] += _dot(dsb, q)
            dv_acc[pl.ds(start, band), :] += _dot(p.astype(BF16), dob)
            return carry

        lax.fori_loop(0, nblk, block, 0)
        for g in range(group):
            _, q_vjp = jax.vjp(_qhead, q_ref[:, g * HEAD_DIM:(g + 1) * HEAD_DIM], gq_ref[...])
            dqp, dgq = q_vjp(dqs[:, g * HEAD_DIM:(g + 1) * HEAD_DIM])
            dq_ref[:, g * HEAD_DIM:(g + 1) * HEAD_DIM] = dqp.astype(BF16)
            dgq_ref[...] += dgq
            dsink_g = jnp.sum(dsink_acc[:, g * WINDOW:(g + 1) * WINDOW], axis=1, keepdims=True)
            dsink_ref[0, g:g + 1, :] = jnp.broadcast_to(dsink_g, (1, LANES))
        _, k_vjp = jax.vjp(_rms, _kv_head(k_ref, kvh), gk_ref[...])
        dkp, dgk = k_vjp(dk_acc[WINDOW:, :])
        dgk_ref[...] += dgk
        mine = lax.broadcasted_iota(jnp.int32, (t, kvw), 1) // HEAD_DIM == pl.program_id(0)
        dkv_ref[:, 0:kvw] = jnp.where(mine, jnp.concatenate([dkp] * kvh, axis=1), dkv_ref[:, 0:kvw])
        dkv_ref[:, kvw:] = jnp.where(mine, jnp.concatenate([dv_acc[WINDOW:, :]] * kvh, axis=1), dkv_ref[:, kvw:])

    fixed = lambda h: (0, 0)
    per = lambda h: (h, 0, 0)
    wide = pl.BlockSpec((t, gw), lambda h: (0, h))
    vec = pl.BlockSpec((1, HEAD_DIM), fixed)
    bias_spec = pl.BlockSpec((1, band, cols), per)
    return pl.pallas_call(
        body, name=name, grid=(kvh,),
        in_specs=[wide, pl.BlockSpec((t, kvw), lambda h: (0, 0)), pl.BlockSpec((t, kvw), lambda h: (0, 1)), vec, vec,
                  pl.BlockSpec((1, 1, cols), per), bias_spec, pl.BlockSpec((nblk, cols), lambda h: (h, 0)), wide],
        out_specs=[wide, pl.BlockSpec((t, 2 * kvw), fixed), vec, vec, pl.BlockSpec((1, group, LANES), per), bias_spec],
        out_shape=[_sds((t, kvh * gw), BF16), _sds((t, 2 * kvw), F32),
                   _sds((1, HEAD_DIM), F32), _sds((1, HEAD_DIM), F32),
                   _sds((kvh, group, LANES), F32), _sds((kvh, band, cols), F32)],
        scratch_shapes=[pltpu.VMEM((t, gw), BF16), pltpu.VMEM((t + WINDOW, HEAD_DIM), BF16),
                        pltpu.VMEM((t + WINDOW, HEAD_DIM), BF16), pltpu.VMEM((t, gw), F32),
                        pltpu.VMEM((t + WINDOW, HEAD_DIM), F32), pltpu.VMEM((t + WINDOW, HEAD_DIM), F32),
                        pltpu.VMEM((1, cols), F32)],
        compiler_params=_params(1),
    )(qb, kv, kv, gq, gk, sinks, bias, lse, do)


def _local_step(x, target, p, comm):
    t, d = x.shape
    n_heads = d // HEAD_DIM
    kv_heads = n_heads // 8
    group = n_heads // kv_heads
    hw = n_heads * HEAD_DIM
    gate_col = 3 * hw // LANES
    kvw = kv_heads * HEAD_DIM
    grads = {}

    def mlp_fwd(tag, h, g, layer, last=False):
        w_up, = comm.weights([f"w_up{layer}"], h)
        a, hn = _norm_matmul(f"{tag}_up", h, g, w_up, relu2=True)
        w_down, = comm.weights([f"w_down{layer}"], a)
        out = _matmul_res_loss(f"{tag}_down", a, w_down, h, target) if last else _matmul_res(f"{tag}_down", a, w_down, h)
        return out, (h, g, hn, a, w_up, w_down)

    def mlp_bwd(tag, saved, layer, dy, dy16):
        h, g, hn, a, w_up, w_down = saved
        du = _matmul_nt(f"{tag}_du", dy16, w_down, a=a)
        dw_down = _matmul_tn(f"{tag}_dwdown", a, dy16)
        dw_up = _matmul_tn(f"{tag}_dwup", hn, du, col_blocks=w_up.shape[0])
        sent = comm.send_grads(tag, {f"w_down{layer}": dw_down, f"w_up{layer}": dw_up})
        return _matmul_nt_rmsbwd(f"{tag}_dh", du, w_up, h, g, dy, bf16_too=True, after=sent)

    bias = _bias_expand("b_bias", p["rel_bias"].T).reshape(kv_heads, group, 2 * WINDOW, WINDOW)
    bias = bias.transpose(0, 2, 1, 3).reshape(kv_heads, 2 * WINDOW, group * WINDOW)
    comm.prefetch(["w_in_a"], bias)
    w_in, = comm.weights(["w_in_a"], None)
    proj, xn1 = _norm_matmul("a_inproj", x, p["g_attn"][0], w_in, tn=640, w_rows=True)
    ahead = comm.prefetch(["w_out_a"], proj)
    b_pad = jnp.pad(p["b_f"], ((0, 0), (0, LANES - n_heads)))
    c = _gate_fwd("a_gate", proj, b_pad, n_heads, gate_col, after=ahead)
    crow = c[:, :n_heads].T.reshape(n_heads // 2, 2, t)
    o_a, lse_a = _fox_fwd("a_attn", proj, c, crow, p["gq_a"], p["gk_a"], n_heads)
    ahead = comm.prefetch(["w_up0", "w_down0", "w_kv", "w_q_b", "w_out_b"], o_a)
    w_out_a, = comm.weights(["w_out_a"], o_a)
    h1 = _matmul_res("a_outproj", o_a, w_out_a, x, after=ahead)
    h2, mlp0 = mlp_fwd("mlp0", h1, p["g_mlp"][0], 0)

    ahead = comm.prefetch(["w_up1", "w_down1"], h2)
    w_kv, w_q_b = comm.weights(["w_kv", "w_q_b"], h2)
    kv, hn_kv = _norm_matmul("kv_proj", h2, p["g_kv"], w_kv, tn=2 * kvw, after=ahead)
    qb, hn_q = _norm_matmul("b_qproj", h2, p["g_attn"][1], w_q_b, tn=512)
    gqb, gkb = p["gq_b"], p["gk_b"].reshape(1, HEAD_DIM)
    sink_rows = jnp.broadcast_to(p["sinks"].reshape(kv_heads, 1, group, 1), (kv_heads, 1, group, WINDOW)).reshape(kv_heads, 1, group * WINDOW)
    o_b, lse_b = _swa_fwd("b_attn", qb, kv, gqb, gkb, sink_rows, bias, group)
    w_out_b, = comm.weights(["w_out_b"], o_b)
    h3 = _matmul_res("b_outproj", o_b, w_out_b, h2)
    (dy, dy16, loss_tile), mlp1 = mlp_fwd("mlp1", h3, p["g_mlp"][1], 1, last=True)

    dh3, dh3_16, dg_mlp1 = mlp_bwd("mlp1", mlp1, 1, dy, dy16)
    do_b = _matmul_nt("b_do", dh3_16, w_out_b)
    dw_out_b = _matmul_tn("b_dwout", o_b, dh3_16)
    dqb, dkv, grads["gq_b"], dgk_b, dsink, dbias = _swa_bwd(
        "b_attn_bwd", qb, kv, gqb, gkb, sink_rows, bias, lse_b, do_b, group)
    grads["gk_b"] = dgk_b
    grads["sinks"] = dsink[:, :, 0].reshape(1, n_heads)
    dbias = dbias.reshape(kv_heads, 2 * WINDOW, group, WINDOW).transpose(0, 2, 1, 3)
    grads["rel_bias"] = _bias_reduce("b_dbias", dbias.reshape(n_heads, WINDOW * 2 * WINDOW)).T
    dw_q_b = _matmul_tn("b_dwq", hn_q, dqb)
    dh2, dg_attn1 = _matmul_nt_rmsbwd("b_dhq", dqb, w_q_b, h2, p["g_attn"][1], dh3)
    dw_kv = _matmul_tn("kv_dw", hn_kv, dkv)
    sent = comm.send_grads("attn_b", {"w_out_b": dw_out_b, "w_q_b": dw_q_b, "w_kv": dw_kv})
    dh2, dh2_16, dg_kv = _matmul_nt_rmsbwd("kv_dh", dkv, w_kv, h2, p["g_kv"], dh2, bf16_too=True, after=sent)
    grads["g_kv"] = dg_kv
    dh1, dh1_16, dg_mlp0 = mlp_bwd("mlp0", mlp0, 0, dh2, dh2_16)
    grads["g_mlp"] = (dg_mlp0, dg_mlp1)

    do_a = _matmul_nt("a_do", dh1_16, w_out_a)
    dw_out_a = _matmul_tn("a_dwout", o_a, dh1_16)
    sent = comm.send_grads("attn_a_out", {"w_out_a": dw_out_a})
    dqkv, dc, grads["gq_a"], grads["gk_a"] = _fox_bwd(
        "a_attn_bwd", proj, c, crow, p["gq_a"], p["gk_a"], lse_a, do_a, n_heads, after=sent)
    dfl, db_f = _gate_bwd("a_gate_bwd", proj, b_pad, dc, n_heads, gate_col)
    grads["b_f"] = db_f
    dw_in = (_matmul_tn("a_dwin", dqkv, xn1, tk=512), _matmul_tn("a_dwin_gate", dfl, xn1))
    sent = comm.send_grads("attn_a_in", {"w_in_a": dw_in})
    grad_x, dg_attn0 = _matmul_nt_rmsbwd("a_dx", (dqkv, dfl), w_in, x, p["g_attn"][0], dh1, w_rows=True, after=sent)
    grads["g_attn"] = (dg_attn0, dg_attn1)
    return loss_tile, grad_x, grads


EVERYONE = (1, 2, 3, 4, 5, 6, 7)
SAME_CORE = (1, 2, 4, 6)
OTHER_CHIPS = (2, 4, 6)
RELAY_COLLECTIVE_ID = 0


class _InFlight:
    def __init__(self, scatter, ks, send_sems, recv_sems, srcs, lands, token):
        self.scatter, self.ks, self.send_sems, self.recv_sems = scatter, ks, send_sems, recv_sems
        self.srcs, self.lands, self.token = list(srcs), list(lands), token


def _mesh_peers(ks=EVERYONE):
    x, y, c = lax.axis_index("x"), lax.axis_index("y"), lax.axis_index("c")
    peers = []
    for k in ks:
        px, py, pc = x ^ ((k >> 2) & 1), y ^ ((k >> 1) & 1), c ^ (k & 1)
        peers.append(((px, py, pc), 4 * px + 2 * py + pc))
    return 4 * x + 2 * y + c, peers


_HBM_SPEC = pl.BlockSpec(memory_space=pltpu.HBM)
_SEM_SPEC = pl.BlockSpec(memory_space=pltpu.SEMAPHORE)
_SIDE_EFFECT = pltpu.SideEffectType.DATAFLOW_SIDE_EFFECTING


def _exchange_start(name, arrays, scatter, collective_id, ks=EVERYONE):
    n = len(arrays)
    me, _ = _mesh_peers()
    lands = []
    for a in arrays:
        own = lax.dynamic_index_in_dim(a, me, 0, keepdims=False) if scatter else a
        shape = a.shape if scatter else (N_DEV,) + a.shape
        lands.append(lax.dynamic_update_index_in_dim(lax.empty(shape, a.dtype), own, me, 0))

    def body(*refs):
        src, land = refs[:n], refs[n:2 * n]
        send_sems, recv_sems, token = refs[2 * n], refs[2 * n + 1], refs[-1]
        pos, peers = _mesh_peers(ks)
        barrier = pltpu.get_barrier_semaphore()
        for peer, _ in peers:
            pl.semaphore_signal(barrier, inc=1, device_id=peer, device_id_type=pl.DeviceIdType.MESH)
        pl.semaphore_wait(barrier, len(peers))
        for a in range(n):
            for k, (peer, peer_pos) in enumerate(peers):
                pltpu.make_async_remote_copy(
                    src_ref=src[a].at[peer_pos] if scatter else src[a], dst_ref=land[a].at[pos],
                    send_sem=send_sems.at[a * len(ks) + k], recv_sem=recv_sems.at[a * len(ks) + k],
                    device_id=peer, device_id_type=pl.DeviceIdType.MESH).start()
        token[...] = jnp.zeros_like(token)

    operands = [pltpu.with_memory_space_constraint(a, pltpu.HBM) for a in list(arrays) + lands]
    outs = pl.pallas_call(
        body, name=name,
        out_shape=(pltpu.SemaphoreType.DMA((n * len(ks),)), pltpu.SemaphoreType.DMA((n * len(ks),)),
                   *[pltpu.HBM(a.shape, a.dtype) for a in operands], _sds((8, LANES), F32)),
        in_specs=[_HBM_SPEC] * (2 * n),
        out_specs=(_SEM_SPEC, _SEM_SPEC, *[_HBM_SPEC] * (2 * n), pl.BlockSpec(memory_space=pltpu.VMEM)),
        input_output_aliases={i: 2 + i for i in range(2 * n)},
        compiler_params=pltpu.CompilerParams(has_side_effects=_SIDE_EFFECT, collective_id=collective_id),
    )(*operands)
    return _InFlight(scatter, ks, outs[0], outs[1], outs[2:2 + n], outs[2 + n:2 + 2 * n], outs[-1])


def _exchange_wait(name, flight, which, after):
    m = len(which)
    scatter, ks = flight.scatter, flight.ks

    def body(*refs):
        src, land = refs[:m], refs[m:2 * m]
        send_sems, recv_sems = refs[2 * m], refs[2 * m + 1]
        _, peers = _mesh_peers(ks)
        for i, a in enumerate(which):
            for k, (peer, peer_pos) in enumerate(peers):
                cp = pltpu.make_async_remote_copy(
                    src_ref=src[i].at[peer_pos] if scatter else src[i], dst_ref=land[i].at[peer_pos],
                    send_sem=send_sems.at[a * len(ks) + k], recv_sem=recv_sems.at[a * len(ks) + k],
                    device_id=peer, device_id_type=pl.DeviceIdType.MESH)
                cp.wait_send()
                cp.wait_recv()

    operands = [flight.srcs[a] for a in which] + [flight.lands[a] for a in which]
    outs = pl.pallas_call(
        body, name=name, out_shape=tuple(pltpu.HBM(a.shape, a.dtype) for a in operands),
        in_specs=[_HBM_SPEC] * (2 * m) + [_SEM_SPEC, _SEM_SPEC, pl.BlockSpec(memory_space=pl.ANY)],
        out_specs=tuple([_HBM_SPEC] * (2 * m)), input_output_aliases={i: i for i in range(2 * m)},
        compiler_params=pltpu.CompilerParams(has_side_effects=_SIDE_EFFECT),
    )(*operands, flight.send_sems, flight.recv_sems, after)
    return list(outs[m:])


def _relay_start(name, lands):
    n = len(lands)

    def body(*refs):
        land, send_sems, recv_sems, token = refs[:n], refs[n], refs[n + 1], refs[-1]
        _, peers = _mesh_peers(OTHER_CHIPS)
        sibling = (lax.axis_index("x"), lax.axis_index("y"), 1 - lax.axis_index("c"))
        barrier = pltpu.get_barrier_semaphore()
        pl.semaphore_signal(barrier, inc=1, device_id=sibling, device_id_type=pl.DeviceIdType.MESH)
        pl.semaphore_wait(barrier, 1)
        for a in range(n):
            for k, (_, peer_pos) in enumerate(peers):
                pltpu.make_async_remote_copy(
                    src_ref=land[a].at[peer_pos], dst_ref=land[a].at[peer_pos],
                    send_sem=send_sems.at[a * len(peers) + k], recv_sem=recv_sems.at[a * len(peers) + k],
                    device_id=sibling, device_id_type=pl.DeviceIdType.MESH).start()
        token[...] = jnp.zeros_like(token)

    count = n * len(OTHER_CHIPS)
    outs = pl.pallas_call(
        body, name=name,
        out_shape=(pltpu.SemaphoreType.DMA((count,)), pltpu.SemaphoreType.DMA((count,)),
                   *[pltpu.HBM(a.shape, a.dtype) for a in lands], _sds((8, LANES), F32)),
        in_specs=[_HBM_SPEC] * n,
        out_specs=(_SEM_SPEC, _SEM_SPEC, *[_HBM_SPEC] * n, pl.BlockSpec(memory_space=pltpu.VMEM)),
        input_output_aliases={i: 2 + i for i in range(n)},
        compiler_params=pltpu.CompilerParams(has_side_effects=_SIDE_EFFECT, collective_id=RELAY_COLLECTIVE_ID),
    )(*[pltpu.with_memory_space_constraint(a, pltpu.HBM) for a in lands])
    return _InFlight(False, OTHER_CHIPS, outs[0], outs[1], [], outs[2:2 + n], outs[-1])


def _relay_wait(name, flight, which, after):
    m = len(which)

    def body(*refs):
        land, send_sems, recv_sems = refs[:m], refs[m], refs[m + 1]
        _, peers = _mesh_peers(OTHER_CHIPS)
        sibling = (lax.axis_index("x"), lax.axis_index("y"), 1 - lax.axis_index("c"))
        for i, a in enumerate(which):
            for k, (_, peer_pos) in enumerate(peers):
                cp = pltpu.make_async_remote_copy(
                    src_ref=land[i].at[peer_pos], dst_ref=land[i].at[peer_pos ^ 1],
                    send_sem=send_sems.at[a * len(peers) + k], recv_sem=recv_sems.at[a * len(peers) + k],
                    device_id=sibling, device_id_type=pl.DeviceIdType.MESH)
                cp.wait_send()
                cp.wait_recv()

    operands = [flight.lands[a] for a in which]
    outs = pl.pallas_call(
        body, name=name, out_shape=tuple(pltpu.HBM(a.shape, a.dtype) for a in operands),
        in_specs=[_HBM_SPEC] * m + [_SEM_SPEC, _SEM_SPEC, pl.BlockSpec(memory_space=pl.ANY)],
        out_specs=tuple([_HBM_SPEC] * m), input_output_aliases={i: i for i in range(m)},
        compiler_params=pltpu.CompilerParams(has_side_effects=_SIDE_EFFECT),
    )(*operands, flight.send_sems, flight.recv_sems, after)
    return list(outs)


def _sum_parts(p_ref):
    g = p_ref[0].astype(F32)
    for dev in range(1, N_DEV):
        g = g + p_ref[dev].astype(F32)
    return g


def _adam_update(g, w, m, v):
    m_new = ADAM_B1 * m + (1.0 - ADAM_B1) * g
    v_new = ADAM_B2 * v + (1.0 - ADAM_B2) * jnp.square(g)
    m_hat = m_new / (1.0 - ADAM_B1 ** ADAM_STEP)
    v_hat = v_new / (1.0 - ADAM_B2 ** ADAM_STEP)
    return -ADAM_LR * (m_hat / (jnp.sqrt(v_hat) + ADAM_EPS) + ADAM_WD * w), m_new, v_new


def _adamw(name, parts, w, m, v, layer=None, into=None):
    r, c = w.shape[-2:]
    row_tiles = layer is None and w.ndim == 3
    if row_tiles:
        r = w.shape[0]
    tr = 256 if r % 256 == 0 and not row_tiles else r
    n_into = 0 if into is None else len(into)
    at = (slice(None), 0, slice(None)) if row_tiles else Ellipsis

    def body(p_ref, w_ref, m_ref, v_ref, *refs):
        g_ref, d_ref, mo_ref, vo_ref = refs[n_into:]
        g = _sum_parts(p_ref)
        g_ref[at] = g
        d_ref[at], mo_ref[at], vo_ref[at] = _adam_update(g, w_ref[at], m_ref[at], v_ref[at])

    if row_tiles:
        blk = pl.BlockSpec((r, 1, c), lambda i: (0, 0, 0))
    elif layer is None:
        blk = pl.BlockSpec((tr, c), lambda i: (i, 0))
    else:
        blk = pl.BlockSpec((None, tr, c), lambda i: (layer, i, 0))
    return pl.pallas_call(
        body, name=name, grid=(r // tr,),
        in_specs=[pl.BlockSpec((N_DEV, tr, c), lambda i: (0, i, 0)), blk, blk, blk] + [pl.BlockSpec(memory_space=pl.ANY)] * n_into,
        out_specs=[blk] * 4, out_shape=[_sds(w.shape, F32)] * 4,
        input_output_aliases={4 + i: i for i in range(n_into)}, compiler_params=_params(1),
    )(parts, w, m, v, *(into or ()))


SMALL_PACK_ROWS = 16
LOSS_ROW = 11


def _small_rows(grads, loss_tile):
    return [(0, 1, grads["g_attn"][0]), (1, 1, grads["g_attn"][1]), (2, 1, grads["g_mlp"][0]), (3, 1, grads["g_mlp"][1]),
            (4, 1, grads["g_kv"]), (5, 1, grads["b_f"]), (6, 1, grads["gq_a"]), (7, 1, grads["gk_a"]), (8, 1, grads["gk_b"]),
            (9, 1, grads["gq_b"]), (10, 1, grads["sinks"]), (LOSS_ROW, 1, loss_tile)]


SMALL_ROWS = {"g_attn": (0, 2), "g_mlp": (2, 2), "g_kv": (4, 1), "b_f": (5, 1), "gq_a": (6, 1), "gk_a": (7, 1),
              "gk_b": (8, 1), "gq_b": (9, 1), "sinks": (10, 1)}


def _pack_small(name, pieces, d):
    def body(*refs):
        out = refs[-1]
        out[...] = jnp.zeros_like(out)
        for (row, rows, _), ref in zip(pieces, refs[:-1]):
            out[row:row + rows, 0:ref.shape[1]] = ref[0:rows, :]

    return pl.pallas_call(body, name=name, out_shape=_sds((SMALL_PACK_ROWS, d), F32), compiler_params=_params(0))(
        *[piece for _, _, piece in pieces])


def _adamw_small(name, parts, parts_rel_bias, w, m, v):
    def body(*refs):
        ins, outs = refs[2:2 + 3 * len(SMALL)], refs[2 + 3 * len(SMALL):]
        pack, rel = _sum_parts(refs[0]), _sum_parts(refs[1])
        for i, k in enumerate(SMALL):
            w_ref, m_ref, v_ref = ins[3 * i:3 * i + 3]
            if k == "rel_bias":
                g = rel
            else:
                row, rows = SMALL_ROWS[k]
                g = pack[row:row + rows, 0:w_ref.shape[1]]
            outs[4 * i][...] = g
            outs[4 * i + 1][...], outs[4 * i + 2][...], outs[4 * i + 3][...] = _adam_update(g, w_ref[...], m_ref[...], v_ref[...])
        outs[-1][...] = pack[LOSS_ROW:LOSS_ROW + 1, 0:LANES]

    operands = [parts, parts_rel_bias] + [t[k] for k in SMALL for t in (w, m, v)]
    out_shape = [_sds(w[k].shape, F32) for k in SMALL for _ in range(4)] + [_sds((1, LANES), F32)]
    outs = pl.pallas_call(body, name=name, out_shape=out_shape, compiler_params=_params(0))(*operands)
    return {k: outs[4 * i:4 * i + 4] for i, k in enumerate(SMALL)}, outs[-1]


class _Comm:
    ORDER = ("w_in_a", "w_out_a", "w_up0", "w_down0", "w_kv", "w_q_b", "w_out_b", "w_up1", "w_down1")

    def __init__(self, shards, d, n_in):
        self.d, self.n_in = d, n_in
        self.ids = iter(range(RELAY_COLLECTIVE_ID + 1, RELAY_COLLECTIVE_ID + 16))
        self.flight = _exchange_start("gather_start", [shards[n].astype(BF16) for n in self.ORDER], False, next(self.ids), SAME_CORE)
        self.relays, self.sent = {}, []

    def prefetch(self, names, after):
        which = [self.ORDER.index(n) for n in names]
        landed = _exchange_wait(f"gather_wait_{names[0]}", self.flight, which, self.flight.token if after is None else after)
        relay = _relay_start(f"gather_relay_{names[0]}", landed)
        for n in names:
            self.relays[n] = (relay, names)
        return relay.token

    def weights(self, names, after):
        relay, group = self.relays[names[0]]
        landed = _relay_wait(f"gather_relay_wait_{names[0]}", relay, [group.index(n) for n in names],
                             relay.token if after is None else after)
        return [self._whole(n, g) for n, g in zip(names, landed)]

    def _whole(self, name, g):
        if name == "w_in_a":
            return _join_row_blocks("w_in_join", g, -(-self.n_in // LANES) * LANES)
        if name.startswith("w_up"):
            return g
        return g.reshape(-1, g.shape[-1])

    def _chunks(self, name, g):
        if name == "w_in_a":
            return _split_row_blocks("dw_in_split", list(g), N_DEV, self.n_in // N_DEV)
        if name.startswith("w_up"):
            return g
        return g.reshape(N_DEV, g.shape[0] // N_DEV, g.shape[1])

    def send_grads(self, tag, partials):
        names = list(partials)
        flight = _exchange_start(f"scatter_start_{tag}", [self._chunks(n, partials[n]) for n in names], True, next(self.ids))
        self.sent.append((tag, flight, names))
        return flight.token

    def received(self, index, after):
        tag, flight, names = self.sent[index]
        landed = _exchange_wait(f"scatter_wait_{tag}", flight, list(range(len(names))), after)
        return dict(zip(names, landed))


def kernel(x, g_attn, g_mlp, w_in_a, b_f, gq_a, gk_a, w_out_a, g_kv, w_kv, gk_b, w_q_b, gq_b, sinks, rel_bias, w_out_b, w_up, w_down, loss_target, m_g_attn, m_g_mlp, m_w_in_a, m_b_f, m_gq_a, m_gk_a, m_w_out_a, m_g_kv, m_w_kv, m_gk_b, m_w_q_b, m_gq_b, m_sinks, m_rel_bias, m_w_out_b, m_w_up, m_w_down, v_g_attn, v_g_mlp, v_w_in_a, v_b_f, v_gq_a, v_gk_a, v_w_out_a, v_g_kv, v_w_kv, v_gk_b, v_w_q_b, v_gq_b, v_sinks, v_rel_bias, v_w_out_b, v_w_up, v_w_down):
    w = dict(g_attn=g_attn, g_mlp=g_mlp, w_in_a=w_in_a, b_f=b_f, gq_a=gq_a, gk_a=gk_a, w_out_a=w_out_a, g_kv=g_kv,
             w_kv=w_kv, gk_b=gk_b, w_q_b=w_q_b, gq_b=gq_b, sinks=sinks, rel_bias=rel_bias, w_out_b=w_out_b,
             w_up=w_up, w_down=w_down)
    mom = dict(g_attn=m_g_attn, g_mlp=m_g_mlp, w_in_a=m_w_in_a, b_f=m_b_f, gq_a=m_gq_a, gk_a=m_gk_a, w_out_a=m_w_out_a,
               g_kv=m_g_kv, w_kv=m_w_kv, gk_b=m_gk_b, w_q_b=m_w_q_b, gq_b=m_gq_b, sinks=m_sinks, rel_bias=m_rel_bias,
               w_out_b=m_w_out_b, w_up=m_w_up, w_down=m_w_down)
    var = dict(g_attn=v_g_attn, g_mlp=v_g_mlp, w_in_a=v_w_in_a, b_f=v_b_f, gq_a=v_gq_a, gk_a=v_gk_a, w_out_a=v_w_out_a,
               g_kv=v_g_kv, w_kv=v_w_kv, gk_b=v_gk_b, w_q_b=v_w_q_b, gq_b=v_gq_b, sinks=v_sinks, rel_bias=v_rel_bias,
               w_out_b=v_w_out_b, w_up=v_w_up, w_down=v_w_down)
    d = x.shape[2]
    where = {"w_in_a": ("w_in_a", 0), "w_out_a": ("w_out_a", 0), "w_kv": ("w_kv", None), "w_q_b": ("w_q_b", 0),
             "w_out_b": ("w_out_b", 0), "w_up0": ("w_up", 0), "w_up1": ("w_up", 1), "w_down0": ("w_down", 0),
             "w_down1": ("w_down", 1)}
    flip = lambda tree: {**tree, "w_in_a": jnp.swapaxes(tree["w_in_a"], 1, 2)}
    w, mom, var = flip(w), flip(mom), flip(var)
    shards = {n: (w[k] if layer is None else w[k][layer]) for n, (k, layer) in where.items()}
    comm = _Comm(shards, d, w_in_a.shape[2] * N_DEV)
    loss_tile, grad_x, grads = _local_step(x[0], loss_target[0], {k: w[k] for k in SMALL}, comm)

    small_flight = _exchange_start(
        "gather_small_grads", [_pack_small("pack_small", _small_rows(grads, loss_tile), d), grads["rel_bias"]], False, next(comm.ids))
    res, after = {}, small_flight.token
    for index in range(len(comm.sent)):
        for n, parts in comm.received(index, after).items():
            k, layer = where[n]
            if n == "w_in_a":
                rows_first = lambda a: jnp.swapaxes(a, 0, 1)
                res[k] = [rows_first(a) for a in _adamw(f"adam_{n}", parts, rows_first(w[k]), rows_first(mom[k]), rows_first(var[k]))]
                continue
            res[k] = _adamw(f"adam_{n}", parts, w[k], mom[k], var[k], layer, res.get(k))
            after = res[k][0]
    as_rows = lambda tree: {k: tree[k] if tree[k].ndim == 2 else tree[k].reshape(1, -1) for k in SMALL}
    small, loss_row = _adamw_small("adam_small", *_exchange_wait("gather_small_wait", small_flight, [0, 1], after),
                                   as_rows(w), as_rows(mom), as_rows(var))
    loss = loss_row[0, 0]
    for k in SMALL:
        res[k] = [a.reshape(w[k].shape) for a in small[k]]
    res["w_in_a"] = [jnp.swapaxes(a, 1, 2) for a in res["w_in_a"]]

    outs = [loss, grad_x[None]]
    for i in range(4):
        outs.extend(res[k][i] for k in WEIGHTS)
    return tuple(outs)
```

```python
import numpy as np
import jax
import jax.numpy as jnp
from jax import lax
from jax.experimental import pallas as pl
from jax.experimental.pallas import tpu as pltpu

F32 = jnp.float32
BF16 = jnp.bfloat16

N_DEV = 8
HEAD_DIM = 64
WINDOW = 128
N_BUCKETS = 32
REL_MAX_DIST = 128
NORM_EPS = 1e-6
NEG = -1e30
LANES = 128
VMEM_LIMIT = 56 * 1024 * 1024

ADAM_LR = 0.001
ADAM_B1 = 0.9
ADAM_B2 = 0.999
ADAM_EPS = 1e-08
ADAM_WD = 0.01
ADAM_STEP = 10

SMALL = ("g_attn", "g_mlp", "b_f", "gq_a", "gk_a", "g_kv", "gk_b", "gq_b", "sinks", "rel_bias")
WEIGHTS = ("g_attn", "g_mlp", "w_in_a", "b_f", "gq_a", "gk_a", "w_out_a", "g_kv", "w_kv", "gk_b",
           "w_q_b", "gq_b", "sinks", "rel_bias", "w_out_b", "w_up", "w_down")


def _params(n_grid):
    return pltpu.CompilerParams(dimension_semantics=("arbitrary",) * n_grid, vmem_limit_bytes=VMEM_LIMIT)


def _sds(shape, dtype):
    return jax.ShapeDtypeStruct(tuple(shape), dtype)


def _after_operand(after):
    if after is None:
        return [], []
    return [pl.BlockSpec((8, LANES), lambda *_: (0, 0))], [after]


def _rms(x, g):
    return (x * lax.rsqrt(jnp.mean(x * x, axis=-1, keepdims=True) + NORM_EPS)) * g


def _dot_nt(a, b):
    return lax.dot_general(a, b, (((1,), (1,)), ((), ())), preferred_element_type=F32)


def _dot_tn(a, b):
    return lax.dot_general(a, b, (((0,), (0,)), ((), ())), preferred_element_type=F32)


def _dot(a, b):
    return jnp.dot(a, b, preferred_element_type=F32)


def _dot_exact(a, b):
    return jnp.dot(a, b, preferred_element_type=F32, precision=lax.Precision.HIGHEST)


def _norm_matmul(name, x, g, w, *, tn=None, relu2=False, w_rows=False, after=None):
    t, d = x.shape
    blocked = w.ndim == 3
    per_step = 2 if blocked else 1
    if blocked:
        tn = per_step * w.shape[2]
        n = w.shape[0] * w.shape[2]
        w_spec = pl.BlockSpec((per_step, d, w.shape[2]), lambda i, j: (j, 0, 0))
    elif w_rows:
        n = w.shape[0]
        w_spec = pl.BlockSpec((tn, d), lambda i, j: (j, 0))
    else:
        n = w.shape[1]
        w_spec = pl.BlockSpec((d, tn), lambda i, j: (0, j))
    tm = min(1024, t)

    def body(x_ref, g_ref, w_ref, *rest):
        y_ref, xn_ref = rest[-2:]

        @pl.when(pl.program_id(1) == 0)
        def _():
            xn_ref[...] = _rms(x_ref[...], g_ref[...]).astype(BF16)

        for b in range(per_step):
            cols = slice(b * (tn // per_step), (b + 1) * (tn // per_step)) if blocked else slice(None)
            wb = w_ref[b] if blocked else w_ref[...]
            y = _dot_nt(xn_ref[...], wb) if w_rows else _dot(xn_ref[...], wb)
            y_ref[:, cols] = jnp.square(jnp.maximum(y, 0.0)).astype(BF16) if relu2 else y

    extra_specs, extra = _after_operand(after)
    out_shape = [_sds((t, n), BF16 if relu2 else F32), _sds((t, d), BF16)]
    out_specs = [pl.BlockSpec((tm, tn), lambda i, j: (i, j)), pl.BlockSpec((tm, d), lambda i, j: (i, 0))]
    return pl.pallas_call(
        body, name=name, grid=(t // tm, n // tn),
        in_specs=[pl.BlockSpec((tm, d), lambda i, j: (i, 0)), pl.BlockSpec((1, d), lambda i, j: (0, 0)), w_spec] + extra_specs,
        out_specs=out_specs, out_shape=out_shape, compiler_params=_params(2),
    )(x, g.reshape(1, d), w, *extra)


def _matmul_res(name, a, w, res, *, tn=512, after=None):
    t, k = a.shape
    n = w.shape[1]
    tm = min(1024, t)

    def body(a_ref, w_ref, r_ref, *rest):
        rest[-1][...] = r_ref[...] + _dot(a_ref[...], w_ref[...])

    extra_specs, extra = _after_operand(after)
    return pl.pallas_call(
        body, name=name, grid=(t // tm, n // tn),
        in_specs=[pl.BlockSpec((tm, k), lambda i, j: (i, 0)), pl.BlockSpec((k, tn), lambda i, j: (0, j)),
                  pl.BlockSpec((tm, tn), lambda i, j: (i, j))] + extra_specs,
        out_specs=pl.BlockSpec((tm, tn), lambda i, j: (i, j)), out_shape=_sds((t, n), F32),
        compiler_params=_params(2),
    )(a, w, res, *extra)


def _matmul_nt(name, dy, w, *, tk=1024):
    t, n = dy.shape
    k = w.shape[0]
    tm = min(1024, t)

    def body(dy_ref, w_ref, o_ref):
        o_ref[...] = _dot_nt(dy_ref[...].astype(BF16), w_ref[...]).astype(BF16)

    return pl.pallas_call(
        body, name=name, grid=(t // tm, k // tk),
        in_specs=[pl.BlockSpec((tm, n), lambda i, j: (i, 0)), pl.BlockSpec((tk, n), lambda i, j: (j, 0))],
        out_specs=pl.BlockSpec((tm, tk), lambda i, j: (i, j)), out_shape=_sds((t, k), BF16),
        compiler_params=_params(2),
    )(dy, w)


RING = 3


def _relu2_bwd_matmul(name, dy, w, a, *, tk=1024):
    t, n = dy.shape
    k = w.shape[0]
    tm = min(1024, t)
    nj = k // tk
    steps = (t // tm) * nj

    def body(dy_ref, w_hbm, a_hbm, o_ref, w_buf, a_buf, sems):
        s = pl.program_id(0) * nj + pl.program_id(1)

        def copies(step):
            i, j, slot = step // nj, step % nj, step % RING
            return (pltpu.make_async_copy(w_hbm.at[pl.ds(j * tk, tk), :], w_buf.at[slot], sems.at[2 * slot]),
                    pltpu.make_async_copy(a_hbm.at[pl.ds(i * tm, tm), pl.ds(j * tk, tk)], a_buf.at[slot], sems.at[2 * slot + 1]))

        def fetch(step):
            for cp in copies(step):
                cp.start()

        @pl.when(s == 0)
        def _():
            for step in range(min(RING - 1, steps)):
                fetch(step)

        @pl.when(s + RING - 1 < steps)
        def _():
            fetch(s + RING - 1)

        for cp in copies(s):
            cp.wait()
        slot = s % RING
        r = _dot_nt(dy_ref[...].astype(BF16), w_buf[slot])
        o_ref[...] = (r * (2.0 * jnp.sqrt(a_buf[slot].astype(F32)))).astype(BF16)

    any_spec = pl.BlockSpec(memory_space=pl.ANY)
    return pl.pallas_call(
        body, name=name, grid=(t // tm, nj),
        in_specs=[pl.BlockSpec((tm, n), lambda i, j: (i, 0)), any_spec, any_spec],
        out_specs=pl.BlockSpec((tm, tk), lambda i, j: (i, j)), out_shape=_sds((t, k), BF16),
        scratch_shapes=[pltpu.VMEM((RING, tk, n), BF16), pltpu.VMEM((RING, tm, tk), BF16),
                        pltpu.SemaphoreType.DMA((2 * RING,))],
        compiler_params=_params(2),
    )(dy, w, a)


def _matmul_nt_rmsbwd(name, dy, w, x, g, dres, *, w_rows=False, bf16_too=False, after=None):
    pieces = isinstance(dy, tuple)
    dys = list(dy) if pieces else [dy]
    t = dres.shape[0]
    blocked = w.ndim == 3
    d = w.shape[1] if blocked or w_rows else w.shape[0]
    tm = min(512, t)
    extra_specs, extra = _after_operand(after)
    n_in = len(dys) + 4 + len(extra)

    def body(*refs):
        dy_ref = refs[0]
        w_ref, x_ref, g_ref, r_ref = refs[len(dys):len(dys) + 4]
        dx_ref, dg_ref = refs[n_in:n_in + 2]
        if pieces:
            n_stack, _, k1 = dys[0].shape
            dxn = _dot(refs[1][...], w_ref[n_stack * k1:, :])
            for j in range(n_stack):
                dxn += _dot(dy_ref[j], w_ref[j * k1:(j + 1) * k1, :])
        elif blocked:
            kb = w.shape[2]
            dxn = _dot_nt(dy_ref[:, 0:kb].astype(BF16), w_ref[0])
            for j in range(1, w.shape[0]):
                dxn += _dot_nt(dy_ref[:, j * kb:(j + 1) * kb].astype(BF16), w_ref[j])
        elif w_rows:
            dxn = _dot(dy_ref[...].astype(BF16), w_ref[...])
        else:
            dxn = _dot_nt(dy_ref[...].astype(BF16), w_ref[...])
        _, vjp = jax.vjp(_rms, x_ref[...], g_ref[...])
        dx, dg = vjp(dxn)
        dx_ref[...] = r_ref[...] + dx
        if bf16_too:
            refs[n_in + 2][...] = (r_ref[...] + dx).astype(BF16)

        @pl.when(pl.program_id(0) == 0)
        def _():
            dg_ref[...] = jnp.zeros_like(dg_ref)

        dg_ref[...] += dg

    row = lambda i: (i, 0)
    fixed = lambda i: (0, 0)
    if pieces:
        dy_specs = [pl.BlockSpec((dys[0].shape[0], tm, dys[0].shape[2]), lambda i: (0, i, 0)),
                    pl.BlockSpec((tm, dys[1].shape[1]), row)]
    else:
        dy_specs = [pl.BlockSpec((tm, dy.shape[1]), row)]
    outs = pl.pallas_call(
        body, name=name, grid=(t // tm,),
        in_specs=dy_specs + [pl.BlockSpec(w.shape, (lambda i: (0, 0, 0)) if blocked else fixed),
                             pl.BlockSpec((tm, d), row), pl.BlockSpec((1, d), fixed), pl.BlockSpec((tm, d), row)] + extra_specs,
        out_specs=[pl.BlockSpec((tm, d), row), pl.BlockSpec((1, d), fixed)] + [pl.BlockSpec((tm, d), row)] * bf16_too,
        out_shape=[_sds((t, d), F32), _sds((1, d), F32)] + [_sds((t, d), BF16)] * bf16_too, compiler_params=_params(1),
    )(*dys, w, x, g.reshape(1, d), dres, *extra)
    return (outs[0], outs[2], outs[1]) if bf16_too else outs


def _matmul_tn(name, a, b, *, tk=1024, tn=1024, col_blocks=None):
    stacked = a.ndim == 3
    t, k1 = a.shape[-2:]
    k = a.shape[0] * k1 if stacked else k1
    n = b.shape[1]
    tk = min(tk, k1)
    per = k1 // tk
    a_spec = (pl.BlockSpec((None, t, tk), lambda i, j: (i // per, 0, i % per)) if stacked
              else pl.BlockSpec((t, tk), lambda i, j: (0, i)))
    if col_blocks:
        tn = n // col_blocks
        out_spec, out_shape = pl.BlockSpec((None, tk, tn), lambda i, j: (j, i, 0)), _sds((col_blocks, k, tn), BF16)
    else:
        tn = min(tn, n)
        out_spec, out_shape = pl.BlockSpec((tk, tn), lambda i, j: (i, j)), _sds((k, n), BF16)

    def body(a_ref, b_ref, o_ref):
        o_ref[...] = _dot_tn(a_ref[...].astype(BF16), b_ref[...].astype(BF16)).astype(BF16)

    return pl.pallas_call(
        body, name=name, grid=(k // tk, n // tn),
        in_specs=[a_spec, pl.BlockSpec((t, tn), lambda i, j: (0, j))],
        out_specs=out_spec, out_shape=out_shape, compiler_params=_params(2),
    )(a, b)


def _join_row_blocks(name, blocks, rows):
    b, r, c = blocks.shape
    tc = min(256, c)

    def body(g_ref, o_ref):
        o_ref[...] = jnp.zeros_like(o_ref)
        for j in range(b):
            o_ref[r * j:r * (j + 1), :] = g_ref[j]

    return pl.pallas_call(
        body, name=name, grid=(c // tc,), in_specs=[pl.BlockSpec((b, r, tc), lambda i: (0, 0, i))],
        out_specs=pl.BlockSpec((rows, tc), lambda i: (0, i)), out_shape=_sds((rows, c), blocks.dtype),
        compiler_params=_params(1),
    )(blocks)


def _split_row_blocks(name, mats, b, r):
    c = mats[0].shape[1]
    tc = min(256, c)

    def body(*refs):
        o_ref = refs[-1]
        for j in range(b):
            first = 0
            for m_ref in refs[:-1]:
                lo, hi = max(r * j, first), min(r * (j + 1), first + m_ref.shape[0])
                if lo < hi:
                    o_ref[j, lo - r * j:hi - r * j, :] = m_ref[lo - first:hi - first, :]
                first += m_ref.shape[0]

    return pl.pallas_call(
        body, name=name, grid=(c // tc,), in_specs=[pl.BlockSpec((m.shape[0], tc), lambda i: (0, i)) for m in mats],
        out_specs=pl.BlockSpec((b, r, tc), lambda i: (0, 0, i)), out_shape=_sds((b, r, c), mats[0].dtype),
        compiler_params=_params(1),
    )(*mats)


def _matmul_res_loss(name, a, w, res, target, *, tn=512):
    t, k = a.shape
    n = w.shape[1]
    tm = min(1024, t)

    def body(a_ref, w_ref, r_ref, t_ref, dy_ref, dyb_ref, l_ref):
        e = r_ref[...] + _dot(a_ref[...], w_ref[...]) - t_ref[...]
        dy_ref[...] = e * (1.0 / n)
        dyb_ref[...] = (e * (1.0 / n)).astype(BF16)

        @pl.when((pl.program_id(0) == 0) & (pl.program_id(1) == 0))
        def _():
            l_ref[...] = jnp.zeros_like(l_ref)

        l_ref[...] += (0.5 / n) * jnp.sum(e * e)

    tile = pl.BlockSpec((tm, tn), lambda i, j: (i, j))
    return pl.pallas_call(
        body, name=name, grid=(t // tm, n // tn),
        in_specs=[pl.BlockSpec((tm, k), lambda i, j: (i, 0)), pl.BlockSpec((k, tn), lambda i, j: (0, j)), tile, tile],
        out_specs=[tile, tile, pl.BlockSpec((8, LANES), lambda i, j: (0, 0))],
        out_shape=[_sds((t, n), F32), _sds((t, n), BF16), _sds((8, LANES), F32)], compiler_params=_params(2),
    )(a, w, res, target)


def _gate_fwd(name, proj, b_pad, n_heads, gate_col, after=None):
    t = proj.shape[0]
    tb = min(256, t)
    tri = jnp.asarray(np.tril(np.ones((tb, tb), np.float32)))
    extra_specs, extra = _after_operand(after)

    def body(p_ref, b_ref, tri_ref, *rest):
        c_ref, carry = rest[-2:]

        @pl.when(pl.program_id(0) == 0)
        def _():
            carry[...] = jnp.zeros_like(carry)

        lane = lax.broadcasted_iota(jnp.int32, (tb, LANES), 1)
        lf = jnp.where(lane < n_heads, jax.nn.log_sigmoid(p_ref[...] + b_ref[...]), 0.0)
        c = _dot_exact(tri_ref[...], lf) + carry[0:1, :]
        c_ref[...] = c
        carry[...] = jnp.broadcast_to(c[tb - 1:tb, :], carry.shape)

    return pl.pallas_call(
        body, name=name, grid=(t // tb,),
        in_specs=[pl.BlockSpec((tb, LANES), lambda i: (i, gate_col)), pl.BlockSpec((1, LANES), lambda i: (0, 0)),
                  pl.BlockSpec((tb, tb), lambda i: (0, 0))] + extra_specs,
        out_specs=pl.BlockSpec((tb, LANES), lambda i: (i, 0)), out_shape=_sds((t, LANES), F32),
        scratch_shapes=[pltpu.VMEM((8, LANES), F32)], compiler_params=_params(1),
    )(proj, b_pad, tri, *extra)


def _gate_bwd(name, proj, b_pad, dc, n_heads, gate_col):
    t = proj.shape[0]
    tb = min(256, t)
    nb = t // tb
    triu = jnp.asarray(np.triu(np.ones((tb, tb), np.float32)))

    def body(p_ref, b_ref, dc_ref, tri_ref, df_ref, db_ref, carry):
        @pl.when(pl.program_id(0) == 0)
        def _():
            carry[...] = jnp.zeros_like(carry)
            db_ref[...] = jnp.zeros_like(db_ref)

        dcv = dc_ref[...]
        dlf = _dot_exact(tri_ref[...], dcv) + carry[0:1, :]
        carry[...] = jnp.broadcast_to(dlf[0:1, :], carry.shape)
        lane = lax.broadcasted_iota(jnp.int32, (tb, LANES), 1)
        z = p_ref[...] + b_ref[...]
        df = jnp.where(lane < n_heads, dlf / (1.0 + jnp.exp(z)), 0.0)
        df_ref[...] = df.astype(BF16)
        db_ref[...] += jnp.sum(df, axis=0, keepdims=True)

    return pl.pallas_call(
        body, name=name, grid=(nb,),
        in_specs=[pl.BlockSpec((tb, LANES), lambda i: (nb - 1 - i, gate_col)), pl.BlockSpec((1, LANES), lambda i: (0, 0)),
                  pl.BlockSpec((tb, LANES), lambda i: (nb - 1 - i, 0)), pl.BlockSpec((tb, tb), lambda i: (0, 0))],
        out_specs=[pl.BlockSpec((tb, LANES), lambda i: (nb - 1 - i, 0)), pl.BlockSpec((1, LANES), lambda i: (0, 0))],
        out_shape=[_sds((t, LANES), BF16), _sds((1, LANES), F32)],
        scratch_shapes=[pltpu.VMEM((8, LANES), F32)], compiler_params=_params(1),
    )(proj, b_pad, dc, triu)


def _qhead(qp, g):
    return _rms(qp, g) * (HEAD_DIM ** -0.5)


def _column(mat, idx):
    lane = lax.broadcasted_iota(jnp.int32, mat.shape, 1)
    return jnp.sum(jnp.where(lane == idx, mat, 0.0), axis=1, keepdims=True)


def _fox_scores(kk, qi, ckey, cq_i, i, bq):
    length = kk.shape[0]
    s = _dot_nt(kk, qi) + cq_i - ckey[:length]
    key = lax.broadcasted_iota(jnp.int32, (length, bq), 0)
    qry = lax.broadcasted_iota(jnp.int32, (length, bq), 1) + i * bq
    return jnp.where(key <= qry, s, NEG)


def _fox_fwd(name, proj, c, crow, gq, gk, n_heads):
    t = proj.shape[0]
    hw = n_heads * HEAD_DIM
    npair = n_heads // 2
    bq = min(512, t)
    nq = t // bq

    def body(q_ref, k_ref, v_ref, c_ref, crow_ref, gq_ref, gk_ref, o_ref, lse_ref):
        hp = pl.program_id(0)
        lse_ref[...] = jnp.zeros_like(lse_ref)
        outs = []
        for hh in range(2):
            sl = slice(hh * HEAD_DIM, (hh + 1) * HEAD_DIM)
            qn = _qhead(q_ref[:, sl], gq_ref[...]).astype(BF16)
            kn = _rms(k_ref[:, sl], gk_ref[...]).astype(BF16)
            v_t = v_ref[:, sl].T.astype(BF16)
            ckey = _column(c_ref[...], 2 * hp + hh)
            cq = crow_ref[0, hh:hh + 1, :]
            o_blocks = []
            for i in range(nq):
                cols = slice(i * bq, (i + 1) * bq)
                length = (i + 1) * bq
                s = _fox_scores(kn[:length], qn[cols], ckey, cq[:, cols], i, bq)
                m = jnp.max(s, axis=0, keepdims=True)
                p = jnp.exp(s - m)
                l = jnp.sum(p, axis=0, keepdims=True)
                o_blocks.append((_dot(v_t[:, :length], p.astype(BF16)) / l).T)
                lse_ref[0, hh:hh + 1, cols] = m + jnp.log(l)
            outs.append(jnp.concatenate(o_blocks, axis=0))
        o_ref[...] = jnp.concatenate(outs, axis=1).astype(BF16)

    col = lambda off: (lambda h: (0, off + h))
    fixed = lambda h: (0, 0)
    return pl.pallas_call(
        body, name=name, grid=(npair,),
        in_specs=[pl.BlockSpec((t, LANES), col(0)), pl.BlockSpec((t, LANES), col(npair)), pl.BlockSpec((t, LANES), col(2 * npair)),
                  pl.BlockSpec((t, LANES), fixed), pl.BlockSpec((1, 2, t), lambda h: (h, 0, 0)),
                  pl.BlockSpec((1, HEAD_DIM), fixed), pl.BlockSpec((1, HEAD_DIM), fixed)],
        out_specs=[pl.BlockSpec((t, LANES), col(0)), pl.BlockSpec((1, 8, t), lambda h: (h, 0, 0))],
        out_shape=[_sds((t, hw), BF16), _sds((npair, 8, t), F32)], compiler_params=_params(1),
    )(proj, proj, proj, c, crow, gq, gk)


def _fox_bwd(name, proj, c, crow, gq, gk, lse, do, n_heads, after=None):
    t = proj.shape[0]
    hw = n_heads * HEAD_DIM
    npair = n_heads // 2
    bq = min(256, t)
    nq = t // bq

    def body(q_ref, k_ref, v_ref, c_ref, crow_ref, gq_ref, gk_ref, lse_ref, do_ref, *rest):
        dqkv_ref, dc_ref, dgq_ref, dgk_ref, dk_acc, dv_acc, dc_acc = rest[-7:]
        hp = pl.program_id(0)

        @pl.when(hp == 0)
        def _():
            dgq_ref[...] = jnp.zeros_like(dgq_ref)
            dgk_ref[...] = jnp.zeros_like(dgk_ref)
            dc_ref[...] = jnp.zeros_like(dc_ref)

        lane = lax.broadcasted_iota(jnp.int32, (t, LANES), 1)
        dqs, dks, dvs = [], [], []
        for hh in range(2):
            sl = slice(hh * HEAD_DIM, (hh + 1) * HEAD_DIM)
            qf, q_vjp = jax.vjp(_qhead, q_ref[:, sl], gq_ref[...])
            kf, k_vjp = jax.vjp(_rms, k_ref[:, sl], gk_ref[...])
            qn, kn, kn_t = qf.astype(BF16), kf.astype(BF16), kf.T.astype(BF16)
            vb = v_ref[:, sl].astype(BF16)
            dob = do_ref[:, sl]
            ckey = _column(c_ref[...], 2 * hp + hh)
            cq = crow_ref[0, hh:hh + 1, :]
            lse_h = lse_ref[0, hh:hh + 1, :]
            dk_acc[...] = jnp.zeros_like(dk_acc)
            dv_acc[...] = jnp.zeros_like(dv_acc)
            dc_acc[...] = jnp.zeros_like(dc_acc)
            dq_blocks = []
            for i in range(nq):
                cols = slice(i * bq, (i + 1) * bq)
                length = (i + 1) * bq
                qi, doi = qn[cols], dob[cols]
                s = _fox_scores(kn[:length], qi, ckey, cq[:, cols], i, bq)
                p = jnp.exp(s - lse_h[:, cols])
                dp = _dot_nt(vb[:length], doi)
                ds = p * (dp - jnp.sum(p * dp, axis=0, keepdims=True))
                dsb = ds.astype(BF16)
                dq_blocks.append(_dot(kn_t[:, :length], dsb).T)
                dk_acc[0:length, :] += _dot(dsb, qi)
                dv_acc[0:length, :] += _dot(p.astype(BF16), doi)
                part = ds[:, 0:LANES]
                for j in range(1, bq // LANES):
                    part = part + ds[:, j * LANES:(j + 1) * LANES]
                dc_acc[0:length, :] += part
            dqp, dgq = q_vjp(jnp.concatenate(dq_blocks, axis=0))
            dkp, dgk = k_vjp(dk_acc[...])
            dgq_ref[...] += dgq
            dgk_ref[...] += dgk
            dqs.append(dqp)
            dks.append(dkp)
            dvs.append(dv_acc[...])
            dc_ref[...] = jnp.where(lane == 2 * hp + hh, -jnp.sum(dc_acc[...], axis=1, keepdims=True), dc_ref[...])
        for part, halves in enumerate((dqs, dks, dvs)):
            dqkv_ref[part] = jnp.concatenate(halves, axis=1).astype(BF16)

    col = lambda off: (lambda h: (0, off + h))
    fixed = lambda h: (0, 0)
    pair_blk = pl.BlockSpec((t, LANES), col(0))
    extra_specs, extra = _after_operand(after)
    return pl.pallas_call(
        body, name=name, grid=(npair,),
        in_specs=[pl.BlockSpec((t, LANES), col(0)), pl.BlockSpec((t, LANES), col(npair)), pl.BlockSpec((t, LANES), col(2 * npair)),
                  pl.BlockSpec((t, LANES), fixed), pl.BlockSpec((1, 2, t), lambda h: (h, 0, 0)),
                  pl.BlockSpec((1, HEAD_DIM), fixed), pl.BlockSpec((1, HEAD_DIM), fixed),
                  pl.BlockSpec((1, 8, t), lambda h: (h, 0, 0)), pair_blk] + extra_specs,
        out_specs=[pl.BlockSpec((3, t, LANES), lambda h: (0, 0, h)), pl.BlockSpec((t, LANES), fixed),
                   pl.BlockSpec((1, HEAD_DIM), fixed), pl.BlockSpec((1, HEAD_DIM), fixed)],
        out_shape=[_sds((3, t, hw), BF16), _sds((t, LANES), F32),
                   _sds((1, HEAD_DIM), F32), _sds((1, HEAD_DIM), F32)],
        scratch_shapes=[pltpu.VMEM((t, HEAD_DIM), F32), pltpu.VMEM((t, HEAD_DIM), F32), pltpu.VMEM((t, LANES), F32)],
        compiler_params=_params(1),
    )(proj, proj, proj, c, crow, gq, gk, lse, do, *extra)


def _t5_bucket_table():
    dist = np.arange(WINDOW)[None, :] + WINDOW - np.arange(2 * WINDOW)[:, None]
    n = np.maximum(dist, 0)
    max_exact = N_BUCKETS // 2
    large = max_exact + (np.log(np.maximum(n, 1) / max_exact) / np.log(REL_MAX_DIST / max_exact)
                         * (N_BUCKETS - max_exact)).astype(np.int32)
    large = np.minimum(large, N_BUCKETS - 1)
    return np.where(n < max_exact, n, large).astype(np.int32).reshape(1, -1)


def _bias_expand(name, rel_bias_t):
    n_heads = rel_bias_t.shape[0]
    tbl = jnp.asarray(_t5_bucket_table())
    width = tbl.shape[1]

    def body(rb_ref, tbl_ref, o_ref):
        onehot = (lax.broadcasted_iota(jnp.int32, (N_BUCKETS, width), 0) == tbl_ref[...]).astype(F32)
        o_ref[...] = _dot_exact(rb_ref[...], onehot)

    return pl.pallas_call(body, name=name, out_shape=_sds((n_heads, width), F32), compiler_params=_params(0))(rel_bias_t, tbl)


def _bias_reduce(name, dbias):
    n_heads, width = dbias.shape
    tbl = jnp.asarray(_t5_bucket_table())

    def body(db_ref, tbl_ref, o_ref):
        onehot = (lax.broadcasted_iota(jnp.int32, (N_BUCKETS, width), 0) == tbl_ref[...]).astype(F32)
        o_ref[...] = lax.dot_general(db_ref[...], onehot, (((1,), (1,)), ((), ())), preferred_element_type=F32,
                                     precision=lax.Precision.HIGHEST)

    return pl.pallas_call(body, name=name, out_shape=_sds((n_heads, N_BUCKETS), F32), compiler_params=_params(0))(dbias, tbl)


def _swa_mask(n, group):
    j = lax.broadcasted_iota(jnp.int32, (2 * WINDOW, group * WINDOW), 0)
    i = lax.broadcasted_iota(jnp.int32, (2 * WINDOW, group * WINDOW), 1) & (WINDOW - 1)
    ok = (j > i) & (j <= i + WINDOW) & ((n > 0) | (j >= WINDOW))
    return jnp.where(ok, 0.0, NEG)


def _swa_stack(ref, start, group):
    return jnp.concatenate([ref[pl.ds(start, WINDOW), g * HEAD_DIM:(g + 1) * HEAD_DIM] for g in range(group)], axis=0)


def _kv_head(ref, n_kv):
    out = ref[:, 0:HEAD_DIM]
    for h in range(1, n_kv):
        out = jnp.where(pl.program_id(0) == h, ref[:, h * HEAD_DIM:(h + 1) * HEAD_DIM], out)
    return out


def _swa_fwd(name, qb, kv, gq, gk, sinks, bias, group):
    t = qb.shape[0]
    kvh = kv.shape[1] // (2 * HEAD_DIM)
    nblk = t // WINDOW
    gw = group * HEAD_DIM
    band = 2 * WINDOW
    cols = group * WINDOW

    def body(q_ref, k_ref, v_ref, gq_ref, gk_ref, sink_ref, bias_ref, o_ref, lse_ref, qs, kpad, vpad):
        for g in range(group):
            qs[:, g * HEAD_DIM:(g + 1) * HEAD_DIM] = _qhead(q_ref[:, g * HEAD_DIM:(g + 1) * HEAD_DIM], gq_ref[...]).astype(BF16)
        kpad[0:WINDOW, :] = jnp.zeros((WINDOW, HEAD_DIM), BF16)
        vpad[0:WINDOW, :] = jnp.zeros((WINDOW, HEAD_DIM), BF16)
        kpad[WINDOW:, :] = _rms(_kv_head(k_ref, kvh), gk_ref[...]).astype(BF16)
        vpad[WINDOW:, :] = _kv_head(v_ref, kvh).astype(BF16)
        sink = sink_ref[0]

        def block(n, carry):
            start = pl.multiple_of(n * WINDOW, WINDOW)
            kb = kpad[pl.ds(start, band), :]
            vb = vpad[pl.ds(start, band), :]
            s = _dot_nt(kb, _swa_stack(qs, start, group)) + bias_ref[0] + _swa_mask(n, group)
            m = jnp.maximum(jnp.max(s, axis=0, keepdims=True), sink)
            e = jnp.exp(s - m)
            l = jnp.sum(e, axis=0, keepdims=True) + jnp.exp(sink - m)
            o_t = _dot_tn(vb, e.astype(BF16)) / l
            for g in range(group):
                o_ref[pl.ds(start, WINDOW), g * HEAD_DIM:(g + 1) * HEAD_DIM] = o_t[:, g * WINDOW:(g + 1) * WINDOW].T.astype(BF16)
            lse_ref[pl.ds(n, 1), :] = m + jnp.log(l)
            return carry

        lax.fori_loop(0, nblk, block, 0)

    fixed = lambda h: (0, 0)
    per = lambda h: (h, 0, 0)
    return pl.pallas_call(
        body, name=name, grid=(kvh,),
        in_specs=[pl.BlockSpec((t, gw), lambda h: (0, h)), pl.BlockSpec((t, kvh * HEAD_DIM), lambda h: (0, 0)),
                  pl.BlockSpec((t, kvh * HEAD_DIM), lambda h: (0, 1)),
                  pl.BlockSpec((1, HEAD_DIM), fixed), pl.BlockSpec((1, HEAD_DIM), fixed),
                  pl.BlockSpec((1, 1, cols), per), pl.BlockSpec((1, band, cols), per)],
        out_specs=[pl.BlockSpec((t, gw), lambda h: (0, h)), pl.BlockSpec((nblk, cols), lambda h: (h, 0))],
        out_shape=[_sds((t, kvh * gw), BF16), _sds((kvh * nblk, cols), F32)],
        scratch_shapes=[pltpu.VMEM((t, gw), BF16), pltpu.VMEM((t + WINDOW, HEAD_DIM), BF16),
                        pltpu.VMEM((t + WINDOW, HEAD_DIM), BF16)],
        compiler_params=_params(1),
    )(qb, kv, kv, gq, gk, sinks, bias)


def _swa_bwd(name, qb, kv, gq, gk, sinks, bias, lse, do, group):
    t = qb.shape[0]
    kvh = kv.shape[1] // (2 * HEAD_DIM)
    kvw = kvh * HEAD_DIM
    nblk = t // WINDOW
    gw = group * HEAD_DIM
    band = 2 * WINDOW
    cols = group * WINDOW

    def body(q_ref, k_ref, v_ref, gq_ref, gk_ref, sink_ref, bias_ref, lse_ref, do_ref,
             dq_ref, dkv_ref, dgq_ref, dgk_ref, dsink_ref, dbias_ref,
             qs, kpad, vpad, dqs, dk_acc, dv_acc, dsink_acc):
        @pl.when(pl.program_id(0) == 0)
        def _():
            dgq_ref[...] = jnp.zeros_like(dgq_ref)
            dgk_ref[...] = jnp.zeros_like(dgk_ref)
            dkv_ref[...] = jnp.zeros_like(dkv_ref)

        for g in range(group):
            qs[:, g * HEAD_DIM:(g + 1) * HEAD_DIM] = _qhead(q_ref[:, g * HEAD_DIM:(g + 1) * HEAD_DIM], gq_ref[...]).astype(BF16)
        kpad[0:WINDOW, :] = jnp.zeros((WINDOW, HEAD_DIM), BF16)
        vpad[0:WINDOW, :] = jnp.zeros((WINDOW, HEAD_DIM), BF16)
        kpad[WINDOW:, :] = _rms(_kv_head(k_ref, kvh), gk_ref[...]).astype(BF16)
        vpad[WINDOW:, :] = _kv_head(v_ref, kvh).astype(BF16)
        dk_acc[...] = jnp.zeros_like(dk_acc)
        dv_acc[...] = jnp.zeros_like(dv_acc)
        dsink_acc[...] = jnp.zeros_like(dsink_acc)
        dbias_ref[...] = jnp.zeros_like(dbias_ref)
        sink = sink_ref[0]

        def block(n, carry):
            start = pl.multiple_of(n * WINDOW, WINDOW)
            kb = kpad[pl.ds(start, band), :]
            vb = vpad[pl.ds(start, band), :]
            q = _swa_stack(qs, start, group)
            dob = _swa_stack(do_ref, start, group)
            lse_n = lse_ref[pl.ds(n, 1), :]
            s = _dot_nt(kb, q) + bias_ref[0] + _swa_mask(n, group)
            p = jnp.exp(s - lse_n)
            dp = _dot_nt(vb, dob)
            dsum = jnp.sum(p * dp, axis=0, keepdims=True)
            ds = p * (dp - dsum)
            dsb = ds.astype(BF16)
            dsink_acc[...] -= jnp.exp(sink - lse_n) * dsum
            dbias_ref[0] += ds
            dq = _dot_tn(dsb, kb)
            for g in range(group):
                dqs[pl.ds(start, WINDOW), g * HEAD_DIM:(g + 1) * HEAD_DIM] = dq[g * WINDOW:(g + 1) * WINDOW]
            dk_acc[pl.ds(start, band), :] += _dot(dsb, q)
            dv_acc[pl.ds(start, band), :] += _dot(p.astype(BF16), dob)
            return carry

        lax.fori_loop(0, nblk, block, 0)
        for g in range(group):
            _, q_vjp = jax.vjp(_qhead, q_ref[:, g * HEAD_DIM:(g + 1) * HEAD_DIM], gq_ref[...])
            dqp, dgq = q_vjp(dqs[:, g * HEAD_DIM:(g + 1) * HEAD_DIM])
            dq_ref[:, g * HEAD_DIM:(g + 1) * HEAD_DIM] = dqp.astype(BF16)
            dgq_ref[...] += dgq
            dsink_g = jnp.sum(dsink_acc[:, g * WINDOW:(g + 1) * WINDOW], axis=1, keepdims=True)
            dsink_ref[0, g:g + 1, :] = jnp.broadcast_to(dsink_g, (1, LANES))
        _, k_vjp = jax.vjp(_rms, _kv_head(k_ref, kvh), gk_ref[...])
        dkp, dgk = k_vjp(dk_acc[WINDOW:, :])
        dgk_ref[...] += dgk
        mine = lax.broadcasted_iota(jnp.int32, (t, kvw), 1) // HEAD_DIM == pl.program_id(0)
        dkv_ref[:, 0:kvw] = jnp.where(mine, jnp.concatenate([dkp] * kvh, axis=1), dkv_ref[:, 0:kvw])
        dkv_ref[:, kvw:] = jnp.where(mine, jnp.concatenate([dv_acc[WINDOW:, :]] * kvh, axis=1), dkv_ref[:, kvw:])

    fixed = lambda h: (0, 0)
    per = lambda h: (h, 0, 0)
    wide = pl.BlockSpec((t, gw), lambda h: (0, h))
    vec = pl.BlockSpec((1, HEAD_DIM), fixed)
    bias_spec = pl.BlockSpec((1, band, cols), per)
    return pl.pallas_call(
        body, name=name, grid=(kvh,),
        in_specs=[wide, pl.BlockSpec((t, kvw), lambda h: (0, 0)), pl.BlockSpec((t, kvw), lambda h: (0, 1)), vec, vec,
                  pl.BlockSpec((1, 1, cols), per), bias_spec, pl.BlockSpec((nblk, cols), lambda h: (h, 0)), wide],
        out_specs=[wide, pl.BlockSpec((t, 2 * kvw), fixed), vec, vec, pl.BlockSpec((1, group, LANES), per), bias_spec],
        out_shape=[_sds((t, kvh * gw), BF16), _sds((t, 2 * kvw), F32),
                   _sds((1, HEAD_DIM), F32), _sds((1, HEAD_DIM), F32),
                   _sds((kvh, group, LANES), F32), _sds((kvh, band, cols), F32)],
        scratch_shapes=[pltpu.VMEM((t, gw), BF16), pltpu.VMEM((t + WINDOW, HEAD_DIM), BF16),
                        pltpu.VMEM((t + WINDOW, HEAD_DIM), BF16), pltpu.VMEM((t, gw), F32),
                        pltpu.VMEM((t + WINDOW, HEAD_DIM), F32), pltpu.VMEM((t + WINDOW, HEAD_DIM), F32),
                        pltpu.VMEM((1, cols), F32)],
        compiler_params=_params(1),
    )(qb, kv, kv, gq, gk, sinks, bias, lse, do)


def _local_step(x, target, p, comm):
    t, d = x.shape
    n_heads = d // HEAD_DIM
    kv_heads = n_heads // 8
    group = n_heads // kv_heads
    hw = n_heads * HEAD_DIM
    gate_col = 3 * hw // LANES
    kvw = kv_heads * HEAD_DIM
    grads = {}

    def mlp_fwd(tag, h, g, layer, last=False):
        w_up, = comm.weights([f"w_up{layer}"], h)
        a, hn = _norm_matmul(f"{tag}_up", h, g, w_up, relu2=True)
        w_down, = comm.weights([f"w_down{layer}"], a)
        out = _matmul_res_loss(f"{tag}_down", a, w_down, h, target) if last else _matmul_res(f"{tag}_down", a, w_down, h)
        return out, (h, g, hn, a, w_up, w_down)

    def mlp_bwd(tag, saved, layer, dy, dy16):
        h, g, hn, a, w_up, w_down = saved
        du = _relu2_bwd_matmul(f"{tag}_du", dy16, w_down, a)
        dw_down = _matmul_tn(f"{tag}_dwdown", a, dy16)
        dw_up = _matmul_tn(f"{tag}_dwup", hn, du, col_blocks=w_up.shape[0])
        sent = comm.send_grads(tag, {f"w_down{layer}": dw_down, f"w_up{layer}": dw_up})
        return _matmul_nt_rmsbwd(f"{tag}_dh", du, w_up, h, g, dy, bf16_too=True, after=sent)

    bias = _bias_expand("b_bias", p["rel_bias"].T).reshape(kv_heads, group, 2 * WINDOW, WINDOW)
    bias = bias.transpose(0, 2, 1, 3).reshape(kv_heads, 2 * WINDOW, group * WINDOW)
    comm.prefetch(["w_in_a"], bias)
    w_in, = comm.weights(["w_in_a"], None)
    proj, xn1 = _norm_matmul("a_inproj", x, p["g_attn"][0], w_in, tn=640, w_rows=True)
    ahead = comm.prefetch(["w_out_a"], proj)
    b_pad = jnp.pad(p["b_f"], ((0, 0), (0, LANES - n_heads)))
    c = _gate_fwd("a_gate", proj, b_pad, n_heads, gate_col, after=ahead)
    crow = c[:, :n_heads].T.reshape(n_heads // 2, 2, t)
    o_a, lse_a = _fox_fwd("a_attn", proj, c, crow, p["gq_a"], p["gk_a"], n_heads)
    ahead = comm.prefetch(["w_up0", "w_down0", "w_kv", "w_q_b", "w_out_b"], o_a)
    w_out_a, = comm.weights(["w_out_a"], o_a)
    h1 = _matmul_res("a_outproj", o_a, w_out_a, x, after=ahead)
    h2, mlp0 = mlp_fwd("mlp0", h1, p["g_mlp"][0], 0)

    ahead = comm.prefetch(["w_up1", "w_down1"], h2)
    w_kv, w_q_b = comm.weights(["w_kv", "w_q_b"], h2)
    kv, hn_kv = _norm_matmul("kv_proj", h2, p["g_kv"], w_kv, tn=2 * kvw, after=ahead)
    qb, hn_q = _norm_matmul("b_qproj", h2, p["g_attn"][1], w_q_b, tn=512)
    gqb, gkb = p["gq_b"], p["gk_b"].reshape(1, HEAD_DIM)
    sink_rows = jnp.broadcast_to(p["sinks"].reshape(kv_heads, 1, group, 1), (kv_heads, 1, group, WINDOW)).reshape(kv_heads, 1, group * WINDOW)
    o_b, lse_b = _swa_fwd("b_attn", qb, kv, gqb, gkb, sink_rows, bias, group)
    w_out_b, = comm.weights(["w_out_b"], o_b)
    h3 = _matmul_res("b_outproj", o_b, w_out_b, h2)
    (dy, dy16, loss_tile), mlp1 = mlp_fwd("mlp1", h3, p["g_mlp"][1], 1, last=True)

    dh3, dh3_16, dg_mlp1 = mlp_bwd("mlp1", mlp1, 1, dy, dy16)
    do_b = _matmul_nt("b_do", dh3_16, w_out_b)
    dw_out_b = _matmul_tn("b_dwout", o_b, dh3_16)
    dqb, dkv, grads["gq_b"], dgk_b, dsink, dbias = _swa_bwd(
        "b_attn_bwd", qb, kv, gqb, gkb, sink_rows, bias, lse_b, do_b, group)
    grads["gk_b"] = dgk_b
    grads["sinks"] = dsink[:, :, 0].reshape(1, n_heads)
    dbias = dbias.reshape(kv_heads, 2 * WINDOW, group, WINDOW).transpose(0, 2, 1, 3)
    grads["rel_bias"] = _bias_reduce("b_dbias", dbias.reshape(n_heads, WINDOW * 2 * WINDOW)).T
    dw_q_b = _matmul_tn("b_dwq", hn_q, dqb)
    dh2, dg_attn1 = _matmul_nt_rmsbwd("b_dhq", dqb, w_q_b, h2, p["g_attn"][1], dh3)
    dw_kv = _matmul_tn("kv_dw", hn_kv, dkv)
    sent = comm.send_grads("attn_b", {"w_out_b": dw_out_b, "w_q_b": dw_q_b, "w_kv": dw_kv})
    dh2, dh2_16, dg_kv = _matmul_nt_rmsbwd("kv_dh", dkv, w_kv, h2, p["g_kv"], dh2, bf16_too=True, after=sent)
    grads["g_kv"] = dg_kv
    dh1, dh1_16, dg_mlp0 = mlp_bwd("mlp0", mlp0, 0, dh2, dh2_16)
    grads["g_mlp"] = (dg_mlp0, dg_mlp1)

    do_a = _matmul_nt("a_do", dh1_16, w_out_a)
    dw_out_a = _matmul_tn("a_dwout", o_a, dh1_16)
    sent = comm.send_grads("attn_a_out", {"w_out_a": dw_out_a})
    dqkv, dc, grads["gq_a"], grads["gk_a"] = _fox_bwd(
        "a_attn_bwd", proj, c, crow, p["gq_a"], p["gk_a"], lse_a, do_a, n_heads, after=sent)
    dfl, db_f = _gate_bwd("a_gate_bwd", proj, b_pad, dc, n_heads, gate_col)
    grads["b_f"] = db_f
    dw_in = (_matmul_tn("a_dwin", dqkv, xn1, tk=512), _matmul_tn("a_dwin_gate", dfl, xn1))
    sent = comm.send_grads("attn_a_in", {"w_in_a": dw_in})
    grad_x, dg_attn0 = _matmul_nt_rmsbwd("a_dx", (dqkv, dfl), w_in, x, p["g_attn"][0], dh1, w_rows=True, after=sent)
    grads["g_attn"] = (dg_attn0, dg_attn1)
    return loss_tile, grad_x, grads


EVERYONE = (1, 2, 3, 4, 5, 6, 7)
SAME_CORE = (1, 2, 4, 6)
OTHER_CHIPS = (2, 4, 6)
RELAY_COLLECTIVE_ID = 0


class _InFlight:
    def __init__(self, scatter, ks, send_sems, recv_sems, srcs, lands, token):
        self.scatter, self.ks, self.send_sems, self.recv_sems = scatter, ks, send_sems, recv_sems
        self.srcs, self.lands, self.token = list(srcs), list(lands), token


def _mesh_peers(ks=EVERYONE):
    x, y, c = lax.axis_index("x"), lax.axis_index("y"), lax.axis_index("c")
    peers = []
    for k in ks:
        px, py, pc = x ^ ((k >> 2) & 1), y ^ ((k >> 1) & 1), c ^ (k & 1)
        peers.append(((px, py, pc), 4 * px + 2 * py + pc))
    return 4 * x + 2 * y + c, peers


_HBM_SPEC = pl.BlockSpec(memory_space=pltpu.HBM)
_SEM_SPEC = pl.BlockSpec(memory_space=pltpu.SEMAPHORE)
_SIDE_EFFECT = pltpu.SideEffectType.DATAFLOW_SIDE_EFFECTING


def _exchange_start(name, arrays, scatter, collective_id, ks=EVERYONE):
    n = len(arrays)
    me, _ = _mesh_peers()
    lands = []
    for a in arrays:
        own = lax.dynamic_index_in_dim(a, me, 0, keepdims=False) if scatter else a
        shape = a.shape if scatter else (N_DEV,) + a.shape
        lands.append(lax.dynamic_update_index_in_dim(lax.empty(shape, a.dtype), own, me, 0))

    def body(*refs):
        src, land = refs[:n], refs[n:2 * n]
        send_sems, recv_sems, token = refs[2 * n], refs[2 * n + 1], refs[-1]
        pos, peers = _mesh_peers(ks)
        barrier = pltpu.get_barrier_semaphore()
        for peer, _ in peers:
            pl.semaphore_signal(barrier, inc=1, device_id=peer, device_id_type=pl.DeviceIdType.MESH)
        pl.semaphore_wait(barrier, len(peers))
        for a in range(n):
            for k, (peer, peer_pos) in enumerate(peers):
                pltpu.make_async_remote_copy(
                    src_ref=src[a].at[peer_pos] if scatter else src[a], dst_ref=land[a].at[pos],
                    send_sem=send_sems.at[a * len(ks) + k], recv_sem=recv_sems.at[a * len(ks) + k],
                    device_id=peer, device_id_type=pl.DeviceIdType.MESH).start()
        token[...] = jnp.zeros_like(token)

    operands = [pltpu.with_memory_space_constraint(a, pltpu.HBM) for a in list(arrays) + lands]
    outs = pl.pallas_call(
        body, name=name,
        out_shape=(pltpu.SemaphoreType.DMA((n * len(ks),)), pltpu.SemaphoreType.DMA((n * len(ks),)),
                   *[pltpu.HBM(a.shape, a.dtype) for a in operands], _sds((8, LANES), F32)),
        in_specs=[_HBM_SPEC] * (2 * n),
        out_specs=(_SEM_SPEC, _SEM_SPEC, *[_HBM_SPEC] * (2 * n), pl.BlockSpec(memory_space=pltpu.VMEM)),
        input_output_aliases={i: 2 + i for i in range(2 * n)},
        compiler_params=pltpu.CompilerParams(has_side_effects=_SIDE_EFFECT, collective_id=collective_id),
    )(*operands)
    return _InFlight(scatter, ks, outs[0], outs[1], outs[2:2 + n], outs[2 + n:2 + 2 * n], outs[-1])


def _exchange_wait(name, flight, which, after):
    m = len(which)
    scatter, ks = flight.scatter, flight.ks

    def body(*refs):
        src, land = refs[:m], refs[m:2 * m]
        send_sems, recv_sems = refs[2 * m], refs[2 * m + 1]
        _, peers = _mesh_peers(ks)
        for i, a in enumerate(which):
            for k, (peer, peer_pos) in enumerate(peers):
                cp = pltpu.make_async_remote_copy(
                    src_ref=src[i].at[peer_pos] if scatter else src[i], dst_ref=land[i].at[peer_pos],
                    send_sem=send_sems.at[a * len(ks) + k], recv_sem=recv_sems.at[a * len(ks) + k],
                    device_id=peer, device_id_type=pl.DeviceIdType.MESH)
                cp.wait_send()
                cp.wait_recv()

    operands = [flight.srcs[a] for a in which] + [flight.lands[a] for a in which]
    outs = pl.pallas_call(
        body, name=name, out_shape=tuple(pltpu.HBM(a.shape, a.dtype) for a in operands),
        in_specs=[_HBM_SPEC] * (2 * m) + [_SEM_SPEC, _SEM_SPEC, pl.BlockSpec(memory_space=pl.ANY)],
        out_specs=tuple([_HBM_SPEC] * (2 * m)), input_output_aliases={i: i for i in range(2 * m)},
        compiler_params=pltpu.CompilerParams(has_side_effects=_SIDE_EFFECT),
    )(*operands, flight.send_sems, flight.recv_sems, after)
    return list(outs[m:])


def _relay_start(name, lands):
    n = len(lands)

    def body(*refs):
        land, send_sems, recv_sems, token = refs[:n], refs[n], refs[n + 1], refs[-1]
        _, peers = _mesh_peers(OTHER_CHIPS)
        sibling = (lax.axis_index("x"), lax.axis_index("y"), 1 - lax.axis_index("c"))
        barrier = pltpu.get_barrier_semaphore()
        pl.semaphore_signal(barrier, inc=1, device_id=sibling, device_id_type=pl.DeviceIdType.MESH)
        pl.semaphore_wait(barrier, 1)
        for a in range(n):
            for k, (_, peer_pos) in enumerate(peers):
                pltpu.make_async_remote_copy(
                    src_ref=land[a].at[peer_pos], dst_ref=land[a].at[peer_pos],
                    send_sem=send_sems.at[a * len(peers) + k], recv_sem=recv_sems.at[a * len(peers) + k],
                    device_id=sibling, device_id_type=pl.DeviceIdType.MESH).start()
        token[...] = jnp.zeros_like(token)

    count = n * len(OTHER_CHIPS)
    outs = pl.pallas_call(
        body, name=name,
        out_shape=(pltpu.SemaphoreType.DMA((count,)), pltpu.SemaphoreType.DMA((count,)),
                   *[pltpu.HBM(a.shape, a.dtype) for a in lands], _sds((8, LANES), F32)),
        in_specs=[_HBM_SPEC] * n,
        out_specs=(_SEM_SPEC, _SEM_SPEC, *[_HBM_SPEC] * n, pl.BlockSpec(memory_space=pltpu.VMEM)),
        input_output_aliases={i: 2 + i for i in range(n)},
        compiler_params=pltpu.CompilerParams(has_side_effects=_SIDE_EFFECT, collective_id=RELAY_COLLECTIVE_ID),
    )(*[pltpu.with_memory_space_constraint(a, pltpu.HBM) for a in lands])
    return _InFlight(False, OTHER_CHIPS, outs[0], outs[1], [], outs[2:2 + n], outs[-1])


def _relay_wait(name, flight, which, after):
    m = len(which)

    def body(*refs):
        land, send_sems, recv_sems = refs[:m], refs[m], refs[m + 1]
        _, peers = _mesh_peers(OTHER_CHIPS)
        sibling = (lax.axis_index("x"), lax.axis_index("y"), 1 - lax.axis_index("c"))
        for i, a in enumerate(which):
            for k, (_, peer_pos) in enumerate(peers):
                cp = pltpu.make_async_remote_copy(
                    src_ref=land[i].at[peer_pos], dst_ref=land[i].at[peer_pos ^ 1],
                    send_sem=send_sems.at[a * len(peers) + k], recv_sem=recv_sems.at[a * len(peers) + k],
                    device_id=sibling, device_id_type=pl.DeviceIdType.MESH)
                cp.wait_send()
                cp.wait_recv()

    operands = [flight.lands[a] for a in which]
    outs = pl.pallas_call(
        body, name=name, out_shape=tuple(pltpu.HBM(a.shape, a.dtype) for a in operands),
        in_specs=[_HBM_SPEC] * m + [_SEM_SPEC, _SEM_SPEC, pl.BlockSpec(memory_space=pl.ANY)],
        out_specs=tuple([_HBM_SPEC] * m), input_output_aliases={i: i for i in range(m)},
        compiler_params=pltpu.CompilerParams(has_side_effects=_SIDE_EFFECT),
    )(*operands, flight.send_sems, flight.recv_sems, after)
    return list(outs)


def _sum_parts(p_ref):
    g = p_ref[0].astype(F32)
    for dev in range(1, N_DEV):
        g = g + p_ref[dev].astype(F32)
    return g


def _adam_update(g, w, m, v):
    m_new = ADAM_B1 * m + (1.0 - ADAM_B1) * g
    v_new = ADAM_B2 * v + (1.0 - ADAM_B2) * jnp.square(g)
    m_hat = m_new / (1.0 - ADAM_B1 ** ADAM_STEP)
    v_hat = v_new / (1.0 - ADAM_B2 ** ADAM_STEP)
    return -ADAM_LR * (m_hat / (jnp.sqrt(v_hat) + ADAM_EPS) + ADAM_WD * w), m_new, v_new


def _adamw(name, parts, w, m, v, layer=None, into=None):
    r, c = w.shape[-2:]
    row_tiles = layer is None and w.ndim == 3
    if row_tiles:
        r = w.shape[0]
    tr = 256 if r % 256 == 0 and not row_tiles else r
    n_into = 0 if into is None else len(into)
    at = (slice(None), 0, slice(None)) if row_tiles else Ellipsis

    def body(p_ref, w_ref, m_ref, v_ref, *refs):
        g_ref, d_ref, mo_ref, vo_ref = refs[n_into:]
        g = _sum_parts(p_ref)
        g_ref[at] = g
        d_ref[at], mo_ref[at], vo_ref[at] = _adam_update(g, w_ref[at], m_ref[at], v_ref[at])

    if row_tiles:
        blk = pl.BlockSpec((r, 1, c), lambda i: (0, 0, 0))
    elif layer is None:
        blk = pl.BlockSpec((tr, c), lambda i: (i, 0))
    else:
        blk = pl.BlockSpec((None, tr, c), lambda i: (layer, i, 0))
    return pl.pallas_call(
        body, name=name, grid=(r // tr,),
        in_specs=[pl.BlockSpec((N_DEV, tr, c), lambda i: (0, i, 0)), blk, blk, blk] + [pl.BlockSpec(memory_space=pl.ANY)] * n_into,
        out_specs=[blk] * 4, out_shape=[_sds(w.shape, F32)] * 4,
        input_output_aliases={4 + i: i for i in range(n_into)}, compiler_params=_params(1),
    )(parts, w, m, v, *(into or ()))


SMALL_PACK_ROWS = 16
LOSS_ROW = 11


def _small_rows(grads, loss_tile):
    return [(0, 1, grads["g_attn"][0]), (1, 1, grads["g_attn"][1]), (2, 1, grads["g_mlp"][0]), (3, 1, grads["g_mlp"][1]),
            (4, 1, grads["g_kv"]), (5, 1, grads["b_f"]), (6, 1, grads["gq_a"]), (7, 1, grads["gk_a"]), (8, 1, grads["gk_b"]),
            (9, 1, grads["gq_b"]), (10, 1, grads["sinks"]), (LOSS_ROW, 1, loss_tile)]


SMALL_ROWS = {"g_attn": (0, 2), "g_mlp": (2, 2), "g_kv": (4, 1), "b_f": (5, 1), "gq_a": (6, 1), "gk_a": (7, 1),
              "gk_b": (8, 1), "gq_b": (9, 1), "sinks": (10, 1)}


def _pack_small(name, pieces, d):
    def body(*refs):
        out = refs[-1]
        out[...] = jnp.zeros_like(out)
        for (row, rows, _), ref in zip(pieces, refs[:-1]):
            out[row:row + rows, 0:ref.shape[1]] = ref[0:rows, :]

    return pl.pallas_call(body, name=name, out_shape=_sds((SMALL_PACK_ROWS, d), F32), compiler_params=_params(0))(
        *[piece for _, _, piece in pieces])


def _adamw_small(name, parts, parts_rel_bias, w, m, v):
    def body(*refs):
        ins, outs = refs[2:2 + 3 * len(SMALL)], refs[2 + 3 * len(SMALL):]
        pack, rel = _sum_parts(refs[0]), _sum_parts(refs[1])
        for i, k in enumerate(SMALL):
            w_ref, m_ref, v_ref = ins[3 * i:3 * i + 3]
            if k == "rel_bias":
                g = rel
            else:
                row, rows = SMALL_ROWS[k]
                g = pack[row:row + rows, 0:w_ref.shape[1]]
            outs[4 * i][...] = g
            outs[4 * i + 1][...], outs[4 * i + 2][...], outs[4 * i + 3][...] = _adam_update(g, w_ref[...], m_ref[...], v_ref[...])
        outs[-1][...] = pack[LOSS_ROW:LOSS_ROW + 1, 0:LANES]

    operands = [parts, parts_rel_bias] + [t[k] for k in SMALL for t in (w, m, v)]
    out_shape = [_sds(w[k].shape, F32) for k in SMALL for _ in range(4)] + [_sds((1, LANES), F32)]
    outs = pl.pallas_call(body, name=name, out_shape=out_shape, compiler_params=_params(0))(*operands)
    return {k: outs[4 * i:4 * i + 4] for i, k in enumerate(SMALL)}, outs[-1]


class _Comm:
    ORDER = ("w_in_a", "w_out_a", "w_up0", "w_down0", "w_kv", "w_q_b", "w_out_b", "w_up1", "w_down1")

    def __init__(self, shards, d, n_in):
        self.d, self.n_in = d, n_in
        self.ids = iter(range(RELAY_COLLECTIVE_ID + 1, RELAY_COLLECTIVE_ID + 16))
        self.flight = _exchange_start("gather_start", [shards[n].astype(BF16) for n in self.ORDER], False, next(self.ids), SAME_CORE)
        self.relays, self.sent = {}, []

    def prefetch(self, names, after):
        which = [self.ORDER.index(n) for n in names]
        landed = _exchange_wait(f"gather_wait_{names[0]}", self.flight, which, self.flight.token if after is None else after)
        relay = _relay_start(f"gather_relay_{names[0]}", landed)
        for n in names:
            self.relays[n] = (relay, names)
        return relay.token

    def weights(self, names, after):
        relay, group = self.relays[names[0]]
        landed = _relay_wait(f"gather_relay_wait_{names[0]}", relay, [group.index(n) for n in names],
                             relay.token if after is None else after)
        return [self._whole(n, g) for n, g in zip(names, landed)]

    def _whole(self, name, g):
        if name == "w_in_a":
            return _join_row_blocks("w_in_join", g, -(-self.n_in // LANES) * LANES)
        if name.startswith("w_up"):
            return g
        return g.reshape(-1, g.shape[-1])

    def _chunks(self, name, g):
        if name == "w_in_a":
            return _split_row_blocks("dw_in_split", list(g), N_DEV, self.n_in // N_DEV)
        if name.startswith("w_up"):
            return g
        return g.reshape(N_DEV, g.shape[0] // N_DEV, g.shape[1])

    def send_grads(self, tag, partials):
        names = list(partials)
        flight = _exchange_start(f"scatter_start_{tag}", [self._chunks(n, partials[n]) for n in names], True, next(self.ids))
        self.sent.append((tag, flight, names))
        return flight.token

    def received(self, index, after):
        tag, flight, names = self.sent[index]
        landed = _exchange_wait(f"scatter_wait_{tag}", flight, list(range(len(names))), after)
        return dict(zip(names, landed))


def kernel(x, g_attn, g_mlp, w_in_a, b_f, gq_a, gk_a, w_out_a, g_kv, w_kv, gk_b, w_q_b, gq_b, sinks, rel_bias, w_out_b, w_up, w_down, loss_target, m_g_attn, m_g_mlp, m_w_in_a, m_b_f, m_gq_a, m_gk_a, m_w_out_a, m_g_kv, m_w_kv, m_gk_b, m_w_q_b, m_gq_b, m_sinks, m_rel_bias, m_w_out_b, m_w_up, m_w_down, v_g_attn, v_g_mlp, v_w_in_a, v_b_f, v_gq_a, v_gk_a, v_w_out_a, v_g_kv, v_w_kv, v_gk_b, v_w_q_b, v_gq_b, v_sinks, v_rel_bias, v_w_out_b, v_w_up, v_w_down):
    w = dict(g_attn=g_attn, g_mlp=g_mlp, w_in_a=w_in_a, b_f=b_f, gq_a=gq_a, gk_a=gk_a, w_out_a=w_out_a, g_kv=g_kv,
             w_kv=w_kv, gk_b=gk_b, w_q_b=w_q_b, gq_b=gq_b, sinks=sinks, rel_bias=rel_bias, w_out_b=w_out_b,
             w_up=w_up, w_down=w_down)
    mom = dict(g_attn=m_g_attn, g_mlp=m_g_mlp, w_in_a=m_w_in_a, b_f=m_b_f, gq_a=m_gq_a, gk_a=m_gk_a, w_out_a=m_w_out_a,
               g_kv=m_g_kv, w_kv=m_w_kv, gk_b=m_gk_b, w_q_b=m_w_q_b, gq_b=m_gq_b, sinks=m_sinks, rel_bias=m_rel_bias,
               w_out_b=m_w_out_b, w_up=m_w_up, w_down=m_w_down)
    var = dict(g_attn=v_g_attn, g_mlp=v_g_mlp, w_in_a=v_w_in_a, b_f=v_b_f, gq_a=v_gq_a, gk_a=v_gk_a, w_out_a=v_w_out_a,
               g_kv=v_g_kv, w_kv=v_w_kv, gk_b=v_gk_b, w_q_b=v_w_q_b, gq_b=v_gq_b, sinks=v_sinks, rel_bias=v_rel_bias,
               w_out_b=v_w_out_b, w_up=v_w_up, w_down=v_w_down)
    d = x.shape[2]
    where = {"w_in_a": ("w_in_a", 0), "w_out_a": ("w_out_a", 0), "w_kv": ("w_kv", None), "w_q_b": ("w_q_b", 0),
             "w_out_b": ("w_out_b", 0), "w_up0": ("w_up", 0), "w_up1": ("w_up", 1), "w_down0": ("w_down", 0),
             "w_down1": ("w_down", 1)}
    flip = lambda tree: {**tree, "w_in_a": jnp.swapaxes(tree["w_in_a"], 1, 2)}
    w, mom, var = flip(w), flip(mom), flip(var)
    shards = {n: (w[k] if layer is None else w[k][layer]) for n, (k, layer) in where.items()}
    comm = _Comm(shards, d, w_in_a.shape[2] * N_DEV)
    loss_tile, grad_x, grads = _local_step(x[0], loss_target[0], {k: w[k] for k in SMALL}, comm)

    small_flight = _exchange_start(
        "gather_small_grads", [_pack_small("pack_small", _small_rows(grads, loss_tile), d), grads["rel_bias"]], False, next(comm.ids))
    res, after = {}, small_flight.token
    for index in range(len(comm.sent)):
        for n, parts in comm.received(index, after).items():
            k, layer = where[n]
            if n == "w_in_a":
                rows_first = lambda a: jnp.swapaxes(a, 0, 1)
                res[k] = [rows_first(a) for a in _adamw(f"adam_{n}", parts, rows_first(w[k]), rows_first(mom[k]), rows_first(var[k]))]
                continue
            res[k] = _adamw(f"adam_{n}", parts, w[k], mom[k], var[k], layer, res.get(k))
            after = res[k][0]
    as_rows = lambda tree: {k: tree[k] if tree[k].ndim == 2 else tree[k].reshape(1, -1) for k in SMALL}
    small, loss_row = _adamw_small("adam_small", *_exchange_wait("gather_small_wait", small_flight, [0, 1], after),
                                   as_rows(w), as_rows(mom), as_rows(var))
    loss = loss_row[0, 0]
    for k in SMALL:
        res[k] = [a.reshape(w[k].shape) for a in small[k]]
    res["w_in_a"] = [jnp.swapaxes(a, 1, 2) for a in res["w_in_a"]]

    outs = [loss, grad_x[None]]
    for i in range(4):
        outs.extend(res[k][i] for k in WEIGHTS)
    return tuple(outs)
```
